```python
import math
import jax, jax.numpy as jnp
from jax import lax
import numpy as np

D_MODEL = 1024
BATCH = 8
SEQ = 8192
DEPTH = 2

RMS_EPS = 1e-6
N_BRANCH = 3

S5_WIDTH = D_MODEL // 2
S5_GROUP = 16
S5_GROUPS = S5_WIDTH // S5_GROUP
S5_STATE = 64
S5_STEP_MIN = 1e-3
S5_STEP_MAX = 1e-1

ATT_HEAD_DIM = 64
ATT_PAIRS = ((128, 1), (512, 4), (2048, 16))
ATT_HEADS_PER_GROUP = 4
ATT_HEADS = len(ATT_PAIRS) * ATT_HEADS_PER_GROUP
ATT_WIDTH = ATT_HEADS * ATT_HEAD_DIM
ATT_OUT_WIDTH = ATT_HEADS_PER_GROUP * ATT_HEAD_DIM
ATT_BLOCK = 128

SSD_HEAD_DIM = 64
SSD_WIDTH = 3 * D_MODEL // 4
SSD_HEADS = SSD_WIDTH // SSD_HEAD_DIM
SSD_GROUPS = 2
SSD_STATE = 128
SSD_CONV = 4
SSD_CHUNK = 128
SSD_CONV_DIM = SSD_WIDTH + 2 * SSD_GROUPS * SSD_STATE
SSD_DT_MIN = 1e-3
SSD_DT_MAX = 1e-1

IN_SPLITS = (S5_WIDTH, S5_WIDTH,
             ATT_WIDTH, ATT_WIDTH, ATT_WIDTH, ATT_OUT_WIDTH,
             SSD_CONV_DIM, SSD_HEADS, SSD_WIDTH,
             N_BRANCH * D_MODEL)
IN_WIDTH = sum(IN_SPLITS)

kernel_name = "hybrid_s5_dilated_attn_ssd_gated_merge"


def rms_norm(x, w):
    xf = x.astype(jnp.float32)
    y = xf * lax.rsqrt(jnp.mean(xf * xf, axis=-1, keepdims=True) + RMS_EPS)
    return y * w.astype(jnp.float32)


def s5_mixer(u, a_re, a_im, log_step, b_re, b_im, c_re, c_im, d, glu_w, glu_b):
    f32 = jnp.float32
    a_re, a_im = a_re.astype(f32), a_im.astype(f32)
    b_re, b_im = b_re.astype(f32), b_im.astype(f32)
    c_re, c_im = c_re.astype(f32), c_im.astype(f32)
    Bt, S, _ = u.shape
    ug = u.reshape(Bt, S, S5_GROUPS, S5_GROUP)
    step = jnp.exp(log_step.astype(f32))[:, None]
    mag = jnp.exp(a_re * step)
    ang = a_im * step
    lam_re, lam_im = mag * jnp.cos(ang), mag * jnp.sin(ang)
    num_re, num_im = lam_re - 1.0, lam_im
    den = a_re * a_re + a_im * a_im
    f_re = (num_re * a_re + num_im * a_im) / den
    f_im = (num_im * a_re - num_re * a_im) / den
    bb_re = f_re[..., None] * b_re - f_im[..., None] * b_im
    bb_im = f_re[..., None] * b_im + f_im[..., None] * b_re
    bu_re = jnp.einsum('gpi,bsgi->bsgp', bb_re, ug)
    bu_im = jnp.einsum('gpi,bsgi->bsgp', bb_im, ug)
    lam_re_t = jnp.broadcast_to(lam_re, (1, S) + lam_re.shape)
    lam_im_t = jnp.broadcast_to(lam_im, (1, S) + lam_im.shape)

    def combine(left, right):
        ar_l, ai_l, br_l, bi_l = left
        ar_r, ai_r, br_r, bi_r = right
        return (ar_r * ar_l - ai_r * ai_l,
                ar_r * ai_l + ai_r * ar_l,
                ar_r * br_l - ai_r * bi_l + br_r,
                ar_r * bi_l + ai_r * br_l + bi_r)

    _, _, h_re, h_im = lax.associative_scan(combine, (lam_re_t, lam_im_t, bu_re, bu_im), axis=1)
    y = jnp.einsum('gip,bsgp->bsgi', c_re, h_re) - jnp.einsum('gip,bsgp->bsgi', c_im, h_im)
    y = y.reshape(Bt, S, S5_WIDTH) + d.astype(f32) * u
    g = jax.nn.gelu(y)
    return g * jax.nn.sigmoid(g @ glu_w + glu_b)


def dilated_window_attention(q, k, v, window, dilation):
    Bt, S, H, Dh = q.shape
    span = window // dilation
    seg = dilation * ATT_BLOCK
    S_pad = -(-S // seg) * seg
    L = S_pad // dilation
    nb = L // ATT_BLOCK

    def to_strided(t):
        t = jnp.pad(t, ((0, 0), (0, S_pad - S), (0, 0), (0, 0)))
        t = t.reshape(Bt, L, dilation, H, Dh).transpose(0, 2, 1, 3, 4)
        return t.reshape(Bt, dilation, nb, ATT_BLOCK, H, Dh)

    def with_prev(t):
        prev = jnp.pad(t[:, :, :-1], ((0, 0), (0, 0), (1, 0), (0, 0), (0, 0), (0, 0)))
        return jnp.concatenate([prev, t], axis=3)

    qb = to_strided(q)
    kb = with_prev(to_strided(k))
    vb = with_prev(to_strided(v))
    s = jnp.einsum('brnqhd,brnkhd->brnhqk', qb, kb) * (Dh ** -0.5)
    qi = jnp.arange(ATT_BLOCK)[:, None] + ATT_BLOCK
    kj = jnp.arange(2 * ATT_BLOCK)[None, :]
    band = (qi - kj >= 0) & (qi - kj <= span)
    has_prev = (jnp.arange(nb) > 0)[:, None, None] | (kj >= ATT_BLOCK)[None]
    mask = band[None] & has_prev
    s = jnp.where(mask[None, None, :, None], s, -jnp.inf)
    m = jnp.max(s, axis=-1, keepdims=True)
    p = jnp.exp(s - m)
    l = jnp.sum(p, axis=-1, keepdims=True)
    o = jnp.einsum('brnhqk,brnkhd->brnqhd', p / l, vb)
    lse = (m + jnp.log(l))[..., 0]
    o = o.reshape(Bt, dilation, L, H, Dh).transpose(0, 2, 1, 3, 4).reshape(Bt, S_pad, H, Dh)[:, :S]
    lse = lse.transpose(0, 1, 2, 4, 3).reshape(Bt, dilation, L, H)
    lse = lse.transpose(0, 2, 1, 3).reshape(Bt, S_pad, H)[:, :S]
    return o, lse


def attention_mixer(q, k, v, q_norm_w, k_norm_w):
    Bt, S = q.shape[:2]
    q = rms_norm(q, q_norm_w)
    k = rms_norm(k, k_norm_w)
    v = v.astype(jnp.float32)
    outs, lses = [], []
    for g, (window, dilation) in enumerate(ATT_PAIRS):
        sl = slice(g * ATT_HEADS_PER_GROUP, (g + 1) * ATT_HEADS_PER_GROUP)
        o, l = dilated_window_attention(q[:, :, sl], k[:, :, sl], v[:, :, sl], window, dilation)
        outs.append(o)
        lses.append(l)
    o = jnp.stack(outs, axis=0)
    alpha = jax.nn.softmax(jnp.stack(lses, axis=0), axis=0)
    y = jnp.sum(alpha[..., None] * o, axis=0)
    return y.reshape(Bt, S, ATT_OUT_WIDTH)


def segsum(a):
    T = a.shape[-1]
    cs = jnp.cumsum(a, axis=-1)
    diff = cs[..., :, None] - cs[..., None, :]
    return jnp.where(jnp.tril(jnp.ones((T, T), dtype=bool)), diff, -jnp.inf)


def causal_depthwise_conv(x, w, b):
    y = lax.conv_general_dilated(x, w.astype(x.dtype)[:, None, :], window_strides=(1,),
                                 padding=((SSD_CONV - 1, 0),),
                                 dimension_numbers=('NWC', 'WIO', 'NWC'),
                                 feature_group_count=x.shape[-1])
    return y + b


def ssd_mixer(xbc, dt, z, conv_w, conv_b, dt_bias, a_log, d, norm_w):
    f32 = jnp.float32
    Bt, S, _ = xbc.shape
    E = SSD_HEADS // SSD_GROUPS
    nc = S // SSD_CHUNK
    xbc = jax.nn.silu(causal_depthwise_conv(xbc, conv_w, conv_b))
    xs, bm, cm = jnp.split(xbc, [SSD_WIDTH, SSD_WIDTH + SSD_GROUPS * SSD_STATE], axis=-1)
    xs = xs.reshape(Bt, nc, SSD_CHUNK, SSD_GROUPS, E, SSD_HEAD_DIM)
    bm = bm.reshape(Bt, nc, SSD_CHUNK, SSD_GROUPS, SSD_STATE)
    cm = cm.reshape(Bt, nc, SSD_CHUNK, SSD_GROUPS, SSD_STATE)
    dt = jax.nn.softplus(dt + dt_bias.astype(f32))
    a = -jnp.exp(a_log.astype(f32))
    dt_c = dt.reshape(Bt, nc, SSD_CHUNK, SSD_GROUPS, E)
    a_dt = (dt_c * a.reshape(SSD_GROUPS, E)).transpose(0, 3, 4, 1, 2)
    xdt = xs * dt_c[..., None]
    a_cs = jnp.cumsum(a_dt, axis=-1)
    decay_in = jnp.exp(segsum(a_dt))
    cb = jnp.einsum('bclgn,bcsgn->bgcls', cm, bm)
    y_diag = jnp.einsum('bgcls,bgecls,bcsgep->bclgep', cb, decay_in, xdt)
    decay_st = jnp.exp(a_cs[..., -1:] - a_cs)
    states = jnp.einsum('bclgn,bgecl,bclgep->bcgepn', bm, decay_st, xdt)
    states = jnp.concatenate([jnp.zeros_like(states[:, :1]), states], axis=1)
    chunk_a = jnp.pad(a_cs[..., -1], ((0, 0), (0, 0), (0, 0), (1, 0)))
    decay_chunk = jnp.exp(segsum(chunk_a))
    states = jnp.einsum('bgezc,bcgepn->bzgepn', decay_chunk, states)[:, :-1]
    y_off = jnp.einsum('bclgn,bcgepn,bgecl->bclgep', cm, states, jnp.exp(a_cs))
    y = y_diag + y_off + xs * d.astype(f32).reshape(SSD_GROUPS, E)[:, :, None]
    y = y.reshape(Bt, S, SSD_WIDTH)
    return rms_norm(y * jax.nn.silu(z), norm_w)


def hybrid_layer(x, norm_w, w_in, s5_a_re, s5_a_im, s5_log_step, s5_b_re, s5_b_im, s5_c_re,
                 s5_c_im, s5_d, s5_glu_w, s5_glu_b, q_norm_w, k_norm_w, conv_w, conv_b,
                 dt_bias, ssd_a_log, ssd_d, ssd_norm_w, proj_a, proj_b, proj_c, w_out):
    Bt, S, _ = x.shape
    h = rms_norm(x, norm_w)
    proj = h @ w_in
    (u_a, z_a, q, k, v, z_b, xbc, dt, z_c, gate_logits) = jnp.split(
        proj, np.cumsum(IN_SPLITS)[:-1].tolist(), axis=-1)
    y_a = s5_mixer(u_a, s5_a_re, s5_a_im, s5_log_step, s5_b_re, s5_b_im, s5_c_re, s5_c_im,
                   s5_d, s5_glu_w, s5_glu_b) * jax.nn.silu(z_a)
    hd = (Bt, S, ATT_HEADS, ATT_HEAD_DIM)
    y_b = attention_mixer(q.reshape(hd), k.reshape(hd), v.reshape(hd),
                          q_norm_w, k_norm_w) * jax.nn.silu(z_b)
    y_c = ssd_mixer(xbc, dt, z_c, conv_w, conv_b, dt_bias, ssd_a_log, ssd_d, ssd_norm_w)
    gates = jax.nn.sigmoid(gate_logits).reshape(Bt, S, N_BRANCH, D_MODEL)
    merged = (gates[:, :, 0] * (y_a @ proj_a)
              + gates[:, :, 1] * (y_b @ proj_b)
              + gates[:, :, 2] * (y_c @ proj_c))
    return x + (merged @ w_out).astype(x.dtype)


def _fwd_setup_inputs(seed: int = 0) -> dict:
    key = jax.random.key(seed)
    ks = jax.random.split(key, 32)
    L = DEPTH
    nrm = jax.random.normal
    P, I, G = S5_STATE, S5_GROUP, S5_GROUPS
    x = nrm(ks[0], (BATCH, SEQ, D_MODEL), jnp.float32)
    norm_w = 1.0 + 0.02 * nrm(ks[1], (L, D_MODEL))
    w_in = nrm(ks[2], (L, D_MODEL, IN_WIDTH)) * D_MODEL ** -0.5
    s5_a_re = -0.5 + 0.01 * nrm(ks[3], (L, G, P))
    s5_a_im = math.pi * jnp.arange(P, dtype=jnp.float32) + 0.01 * nrm(ks[4], (L, G, P))
    s5_log_step = jax.random.uniform(ks[5], (L, G), minval=math.log(S5_STEP_MIN),
                                     maxval=math.log(S5_STEP_MAX))
    s5_b_re = nrm(ks[6], (L, G, P, I)) * (2 * I) ** -0.5
    s5_b_im = nrm(ks[7], (L, G, P, I)) * (2 * I) ** -0.5
    s5_c_re = nrm(ks[8], (L, G, I, P)) * (2 * P) ** -0.5
    s5_c_im = nrm(ks[9], (L, G, I, P)) * (2 * P) ** -0.5
    s5_d = nrm(ks[10], (L, S5_WIDTH))
    s5_glu_w = nrm(ks[11], (L, S5_WIDTH, S5_WIDTH)) * S5_WIDTH ** -0.5
    s5_glu_b = 0.01 * nrm(ks[12], (L, S5_WIDTH))
    q_norm_w = 1.0 + 0.02 * nrm(ks[13], (L, ATT_HEAD_DIM))
    k_norm_w = 1.0 + 0.02 * nrm(ks[14], (L, ATT_HEAD_DIM))
    conv_w = nrm(ks[15], (L, SSD_CONV, SSD_CONV_DIM)) * SSD_CONV ** -0.5
    conv_b = 0.01 * nrm(ks[16], (L, SSD_CONV_DIM))
    dt0 = jnp.exp(jax.random.uniform(ks[17], (L, SSD_HEADS), minval=math.log(SSD_DT_MIN),
                                     maxval=math.log(SSD_DT_MAX)))
    dt_bias = dt0 + jnp.log(-jnp.expm1(-dt0))
    ssd_a_log = jnp.log(jax.random.uniform(ks[18], (L, SSD_HEADS), minval=1.0, maxval=16.0))
    ssd_d = 1.0 + 0.1 * nrm(ks[19], (L, SSD_HEADS))
    ssd_norm_w = 1.0 + 0.02 * nrm(ks[20], (L, SSD_WIDTH))
    proj_a = nrm(ks[21], (L, S5_WIDTH, D_MODEL)) * S5_WIDTH ** -0.5
    proj_b = nrm(ks[22], (L, ATT_OUT_WIDTH, D_MODEL)) * ATT_OUT_WIDTH ** -0.5
    proj_c = nrm(ks[23], (L, SSD_WIDTH, D_MODEL)) * SSD_WIDTH ** -0.5
    w_out = nrm(ks[24], (L, D_MODEL, D_MODEL)) * (0.5 * D_MODEL ** -0.5)
    return {"x": x, "norm_w": norm_w, "w_in": w_in,
            "s5_a_re": s5_a_re, "s5_a_im": s5_a_im, "s5_log_step": s5_log_step,
            "s5_b_re": s5_b_re, "s5_b_im": s5_b_im, "s5_c_re": s5_c_re, "s5_c_im": s5_c_im,
            "s5_d": s5_d, "s5_glu_w": s5_glu_w, "s5_glu_b": s5_glu_b,
            "q_norm_w": q_norm_w, "k_norm_w": k_norm_w,
            "conv_w": conv_w, "conv_b": conv_b, "dt_bias": dt_bias,
            "ssd_a_log": ssd_a_log, "ssd_d": ssd_d, "ssd_norm_w": ssd_norm_w,
            "proj_a": proj_a, "proj_b": proj_b, "proj_c": proj_c, "w_out": w_out}


def _fwd_reference(x, norm_w, w_in, s5_a_re, s5_a_im, s5_log_step, s5_b_re, s5_b_im, s5_c_re,
              s5_c_im, s5_d, s5_glu_w, s5_glu_b, q_norm_w, k_norm_w, conv_w, conv_b,
              dt_bias, ssd_a_log, ssd_d, ssd_norm_w, proj_a, proj_b, proj_c, w_out):
    for i in range(DEPTH):
        x = hybrid_layer(x, norm_w[i], w_in[i], s5_a_re[i], s5_a_im[i], s5_log_step[i],
                         s5_b_re[i], s5_b_im[i], s5_c_re[i], s5_c_im[i], s5_d[i],
                         s5_glu_w[i], s5_glu_b[i], q_norm_w[i], k_norm_w[i], conv_w[i],
                         conv_b[i], dt_bias[i], ssd_a_log[i], ssd_d[i], ssd_norm_w[i],
                         proj_a[i], proj_b[i], proj_c[i], w_out[i])
    return x


import jax as _jax
import jax.numpy as _jnp

TWIN_FORMAT = 'train_step'
FWD_PARAMS = ['x', 'norm_w', 'w_in', 's5_a_re', 's5_a_im', 's5_log_step', 's5_b_re', 's5_b_im', 's5_c_re', 's5_c_im', 's5_d', 's5_glu_w', 's5_glu_b', 'q_norm_w', 'k_norm_w', 'conv_w', 'conv_b', 'dt_bias', 'ssd_a_log', 'ssd_d', 'ssd_norm_w', 'proj_a', 'proj_b', 'proj_c', 'w_out']
TWIN_WEIGHTS = ['norm_w', 'w_in', 's5_a_re', 's5_a_im', 's5_log_step', 's5_b_re', 's5_b_im', 's5_c_re', 's5_c_im', 's5_d', 's5_glu_w', 's5_glu_b', 'q_norm_w', 'k_norm_w', 'conv_w', 'conv_b', 'dt_bias', 'ssd_a_log', 'ssd_d', 'ssd_norm_w', 'proj_a', 'proj_b', 'proj_c', 'w_out']
TWIN_DIFF_INPUT = 'x'
TWIN_INPUTS = ['x', 'norm_w', 'w_in', 's5_a_re', 's5_a_im', 's5_log_step', 's5_b_re', 's5_b_im', 's5_c_re', 's5_c_im', 's5_d', 's5_glu_w', 's5_glu_b', 'q_norm_w', 'k_norm_w', 'conv_w', 'conv_b', 'dt_bias', 'ssd_a_log', 'ssd_d', 'ssd_norm_w', 'proj_a', 'proj_b', 'proj_c', 'w_out', 'loss_target', 'm_norm_w', 'm_w_in', 'm_s5_a_re', 'm_s5_a_im', 'm_s5_log_step', 'm_s5_b_re', 'm_s5_b_im', 'm_s5_c_re', 'm_s5_c_im', 'm_s5_d', 'm_s5_glu_w', 'm_s5_glu_b', 'm_q_norm_w', 'm_k_norm_w', 'm_conv_w', 'm_conv_b', 'm_dt_bias', 'm_ssd_a_log', 'm_ssd_d', 'm_ssd_norm_w', 'm_proj_a', 'm_proj_b', 'm_proj_c', 'm_w_out', 'v_norm_w', 'v_w_in', 'v_s5_a_re', 'v_s5_a_im', 'v_s5_log_step', 'v_s5_b_re', 'v_s5_b_im', 'v_s5_c_re', 'v_s5_c_im', 'v_s5_d', 'v_s5_glu_w', 'v_s5_glu_b', 'v_q_norm_w', 'v_k_norm_w', 'v_conv_w', 'v_conv_b', 'v_dt_bias', 'v_ssd_a_log', 'v_ssd_d', 'v_ssd_norm_w', 'v_proj_a', 'v_proj_b', 'v_proj_c', 'v_w_out']
TWIN_OUTPUTS = ['loss', 'grad_x', 'grad_norm_w', 'grad_w_in', 'grad_s5_a_re', 'grad_s5_a_im', 'grad_s5_log_step', 'grad_s5_b_re', 'grad_s5_b_im', 'grad_s5_c_re', 'grad_s5_c_im', 'grad_s5_d', 'grad_s5_glu_w', 'grad_s5_glu_b', 'grad_q_norm_w', 'grad_k_norm_w', 'grad_conv_w', 'grad_conv_b', 'grad_dt_bias', 'grad_ssd_a_log', 'grad_ssd_d', 'grad_ssd_norm_w', 'grad_proj_a', 'grad_proj_b', 'grad_proj_c', 'grad_w_out', 'delta_norm_w', 'delta_w_in', 'delta_s5_a_re', 'delta_s5_a_im', 'delta_s5_log_step', 'delta_s5_b_re', 'delta_s5_b_im', 'delta_s5_c_re', 'delta_s5_c_im', 'delta_s5_d', 'delta_s5_glu_w', 'delta_s5_glu_b', 'delta_q_norm_w', 'delta_k_norm_w', 'delta_conv_w', 'delta_conv_b', 'delta_dt_bias', 'delta_ssd_a_log', 'delta_ssd_d', 'delta_ssd_norm_w', 'delta_proj_a', 'delta_proj_b', 'delta_proj_c', 'delta_w_out', 'new_m_norm_w', 'new_m_w_in', 'new_m_s5_a_re', 'new_m_s5_a_im', 'new_m_s5_log_step', 'new_m_s5_b_re', 'new_m_s5_b_im', 'new_m_s5_c_re', 'new_m_s5_c_im', 'new_m_s5_d', 'new_m_s5_glu_w', 'new_m_s5_glu_b', 'new_m_q_norm_w', 'new_m_k_norm_w', 'new_m_conv_w', 'new_m_conv_b', 'new_m_dt_bias', 'new_m_ssd_a_log', 'new_m_ssd_d', 'new_m_ssd_norm_w', 'new_m_proj_a', 'new_m_proj_b', 'new_m_proj_c', 'new_m_w_out', 'new_v_norm_w', 'new_v_w_in', 'new_v_s5_a_re', 'new_v_s5_a_im', 'new_v_s5_log_step', 'new_v_s5_b_re', 'new_v_s5_b_im', 'new_v_s5_c_re', 'new_v_s5_c_im', 'new_v_s5_d', 'new_v_s5_glu_w', 'new_v_s5_glu_b', 'new_v_q_norm_w', 'new_v_k_norm_w', 'new_v_conv_w', 'new_v_conv_b', 'new_v_dt_bias', 'new_v_ssd_a_log', 'new_v_ssd_d', 'new_v_ssd_norm_w', 'new_v_proj_a', 'new_v_proj_b', 'new_v_proj_c', 'new_v_w_out']
TWIN_LEAF_KINDS = {'loss': 'loss', 'grad_x': 'grad_x', 'grad_norm_w': 'grad_w', 'grad_w_in': 'grad_w', 'grad_s5_a_re': 'grad_w', 'grad_s5_a_im': 'grad_w', 'grad_s5_log_step': 'grad_w', 'grad_s5_b_re': 'grad_w', 'grad_s5_b_im': 'grad_w', 'grad_s5_c_re': 'grad_w', 'grad_s5_c_im': 'grad_w', 'grad_s5_d': 'grad_w', 'grad_s5_glu_w': 'grad_w', 'grad_s5_glu_b': 'grad_w', 'grad_q_norm_w': 'grad_w', 'grad_k_norm_w': 'grad_w', 'grad_conv_w': 'grad_w', 'grad_conv_b': 'grad_w', 'grad_dt_bias': 'grad_w', 'grad_ssd_a_log': 'grad_w', 'grad_ssd_d': 'grad_w', 'grad_ssd_norm_w': 'grad_w', 'grad_proj_a': 'grad_w', 'grad_proj_b': 'grad_w', 'grad_proj_c': 'grad_w', 'grad_w_out': 'grad_w', 'delta_norm_w': 'delta_w', 'delta_w_in': 'delta_w', 'delta_s5_a_re': 'delta_w', 'delta_s5_a_im': 'delta_w', 'delta_s5_log_step': 'delta_w', 'delta_s5_b_re': 'delta_w', 'delta_s5_b_im': 'delta_w', 'delta_s5_c_re': 'delta_w', 'delta_s5_c_im': 'delta_w', 'delta_s5_d': 'delta_w', 'delta_s5_glu_w': 'delta_w', 'delta_s5_glu_b': 'delta_w', 'delta_q_norm_w': 'delta_w', 'delta_k_norm_w': 'delta_w', 'delta_conv_w': 'delta_w', 'delta_conv_b': 'delta_w', 'delta_dt_bias': 'delta_w', 'delta_ssd_a_log': 'delta_w', 'delta_ssd_d': 'delta_w', 'delta_ssd_norm_w': 'delta_w', 'delta_proj_a': 'delta_w', 'delta_proj_b': 'delta_w', 'delta_proj_c': 'delta_w', 'delta_w_out': 'delta_w', 'new_m_norm_w': 'new_m', 'new_m_w_in': 'new_m', 'new_m_s5_a_re': 'new_m', 'new_m_s5_a_im': 'new_m', 'new_m_s5_log_step': 'new_m', 'new_m_s5_b_re': 'new_m', 'new_m_s5_b_im': 'new_m', 'new_m_s5_c_re': 'new_m', 'new_m_s5_c_im': 'new_m', 'new_m_s5_d': 'new_m', 'new_m_s5_glu_w': 'new_m', 'new_m_s5_glu_b': 'new_m', 'new_m_q_norm_w': 'new_m', 'new_m_k_norm_w': 'new_m', 'new_m_conv_w': 'new_m', 'new_m_conv_b': 'new_m', 'new_m_dt_bias': 'new_m', 'new_m_ssd_a_log': 'new_m', 'new_m_ssd_d': 'new_m', 'new_m_ssd_norm_w': 'new_m', 'new_m_proj_a': 'new_m', 'new_m_proj_b': 'new_m', 'new_m_proj_c': 'new_m', 'new_m_w_out': 'new_m', 'new_v_norm_w': 'new_v', 'new_v_w_in': 'new_v', 'new_v_s5_a_re': 'new_v', 'new_v_s5_a_im': 'new_v', 'new_v_s5_log_step': 'new_v', 'new_v_s5_b_re': 'new_v', 'new_v_s5_b_im': 'new_v', 'new_v_s5_c_re': 'new_v', 'new_v_s5_c_im': 'new_v', 'new_v_s5_d': 'new_v', 'new_v_s5_glu_w': 'new_v', 'new_v_s5_glu_b': 'new_v', 'new_v_q_norm_w': 'new_v', 'new_v_k_norm_w': 'new_v', 'new_v_conv_w': 'new_v', 'new_v_conv_b': 'new_v', 'new_v_dt_bias': 'new_v', 'new_v_ssd_a_log': 'new_v', 'new_v_ssd_d': 'new_v', 'new_v_ssd_norm_w': 'new_v', 'new_v_proj_a': 'new_v', 'new_v_proj_b': 'new_v', 'new_v_proj_c': 'new_v', 'new_v_w_out': 'new_v'}


def _forward(args):
    return _fwd_reference(*[args[k] for k in FWD_PARAMS])


def _output_shape():
    out = _jax.eval_shape(lambda: _forward(_fwd_setup_inputs(0)))
    return out.shape, out.dtype

N_MICROBATCH = 1
ADAM_LR = 0.001
ADAM_B1 = 0.9
ADAM_B2 = 0.999
ADAM_EPS = 1e-08
ADAM_WD = 0.01
ADAM_STEP = 10
PER_EXAMPLE_BATCH_AXIS = {'x': 0, 'loss_target': 0}
SHARED_INPUTS = []
_WEIGHT_DTYPES = {'norm_w': _jnp.float32, 'w_in': _jnp.float32, 's5_a_re': _jnp.float32, 's5_a_im': _jnp.float32, 's5_log_step': _jnp.float32, 's5_b_re': _jnp.float32, 's5_b_im': _jnp.float32, 's5_c_re': _jnp.float32, 's5_c_im': _jnp.float32, 's5_d': _jnp.float32, 's5_glu_w': _jnp.float32, 's5_glu_b': _jnp.float32, 'q_norm_w': _jnp.float32, 'k_norm_w': _jnp.float32, 'conv_w': _jnp.float32, 'conv_b': _jnp.float32, 'dt_bias': _jnp.float32, 'ssd_a_log': _jnp.float32, 'ssd_d': _jnp.float32, 'ssd_norm_w': _jnp.float32, 'proj_a': _jnp.float32, 'proj_b': _jnp.float32, 'proj_c': _jnp.float32, 'w_out': _jnp.float32}
MOMENT_SCALE = {'norm_w': 6.832285e-01, 'w_in': 6.965248e-02, 's5_a_re': 3.117346e-03, 's5_a_im': 2.450148e-03, 's5_log_step': 1.888136e+00, 's5_b_re': 1.442626e-03, 's5_b_im': 1.424299e-03, 's5_c_re': 3.120262e-03, 's5_c_im': 3.057585e-03, 's5_d': 4.677041e-01, 's5_glu_w': 8.246668e-02, 's5_glu_b': 2.804421e-01, 'q_norm_w': 8.607943e-02, 'k_norm_w': 8.607659e-02, 'conv_w': 1.800862e-01, 'conv_b': 4.908787e-01, 'dt_bias': 2.518351e-01, 'ssd_a_log': 3.167207e+00, 'ssd_d': 1.817538e+00, 'ssd_norm_w': 9.202413e+00, 'proj_a': 4.176934e-02, 'proj_b': 1.200052e-02, 'proj_c': 2.846495e-01, 'w_out': 4.757419e-01}


def _to_microbatches(a, axis):
    t = _jnp.moveaxis(a, axis, 0)
    t = t.reshape((N_MICROBATCH, t.shape[0] // N_MICROBATCH) + t.shape[1:])
    return _jnp.moveaxis(t, 1, axis + 1)


def setup_inputs(seed: int = 0) -> dict:
    inp = _fwd_setup_inputs(seed)
    key = _jax.random.fold_in(_jax.random.key(seed), 7919)
    shape, _ = _output_shape()
    out = dict(inp)
    out["loss_target"] = _jax.random.normal(_jax.random.fold_in(key, 0), shape, _jnp.float32)
    for i, name in enumerate(TWIN_WEIGHTS):
        w = inp[name].astype(_jnp.float32)
        if MOMENT_SCALE is None:
            s = _jnp.sqrt(_jnp.mean(_jnp.square(w)) + 1e-30)
        else:
            s = MOMENT_SCALE[name]
        km, kv = _jax.random.split(_jax.random.fold_in(key, i + 1))
        out[name] = w
        out["m_" + name] = s * _jax.random.normal(km, w.shape, _jnp.float32)
        out["v_" + name] = (s * s) * _jax.random.uniform(kv, w.shape, _jnp.float32, 0.5, 1.5)
    if N_MICROBATCH > 1:
        for name, axis in PER_EXAMPLE_BATCH_AXIS.items():
            out[name] = _to_microbatches(out[name], axis)
    return {'x': out['x'], 'norm_w': out['norm_w'], 'w_in': out['w_in'], 's5_a_re': out['s5_a_re'], 's5_a_im': out['s5_a_im'], 's5_log_step': out['s5_log_step'], 's5_b_re': out['s5_b_re'], 's5_b_im': out['s5_b_im'], 's5_c_re': out['s5_c_re'], 's5_c_im': out['s5_c_im'], 's5_d': out['s5_d'], 's5_glu_w': out['s5_glu_w'], 's5_glu_b': out['s5_glu_b'], 'q_norm_w': out['q_norm_w'], 'k_norm_w': out['k_norm_w'], 'conv_w': out['conv_w'], 'conv_b': out['conv_b'], 'dt_bias': out['dt_bias'], 'ssd_a_log': out['ssd_a_log'], 'ssd_d': out['ssd_d'], 'ssd_norm_w': out['ssd_norm_w'], 'proj_a': out['proj_a'], 'proj_b': out['proj_b'], 'proj_c': out['proj_c'], 'w_out': out['w_out'], 'loss_target': out['loss_target'], 'm_norm_w': out['m_norm_w'], 'm_w_in': out['m_w_in'], 'm_s5_a_re': out['m_s5_a_re'], 'm_s5_a_im': out['m_s5_a_im'], 'm_s5_log_step': out['m_s5_log_step'], 'm_s5_b_re': out['m_s5_b_re'], 'm_s5_b_im': out['m_s5_b_im'], 'm_s5_c_re': out['m_s5_c_re'], 'm_s5_c_im': out['m_s5_c_im'], 'm_s5_d': out['m_s5_d'], 'm_s5_glu_w': out['m_s5_glu_w'], 'm_s5_glu_b': out['m_s5_glu_b'], 'm_q_norm_w': out['m_q_norm_w'], 'm_k_norm_w': out['m_k_norm_w'], 'm_conv_w': out['m_conv_w'], 'm_conv_b': out['m_conv_b'], 'm_dt_bias': out['m_dt_bias'], 'm_ssd_a_log': out['m_ssd_a_log'], 'm_ssd_d': out['m_ssd_d'], 'm_ssd_norm_w': out['m_ssd_norm_w'], 'm_proj_a': out['m_proj_a'], 'm_proj_b': out['m_proj_b'], 'm_proj_c': out['m_proj_c'], 'm_w_out': out['m_w_out'], 'v_norm_w': out['v_norm_w'], 'v_w_in': out['v_w_in'], 'v_s5_a_re': out['v_s5_a_re'], 'v_s5_a_im': out['v_s5_a_im'], 'v_s5_log_step': out['v_s5_log_step'], 'v_s5_b_re': out['v_s5_b_re'], 'v_s5_b_im': out['v_s5_b_im'], 'v_s5_c_re': out['v_s5_c_re'], 'v_s5_c_im': out['v_s5_c_im'], 'v_s5_d': out['v_s5_d'], 'v_s5_glu_w': out['v_s5_glu_w'], 'v_s5_glu_b': out['v_s5_glu_b'], 'v_q_norm_w': out['v_q_norm_w'], 'v_k_norm_w': out['v_k_norm_w'], 'v_conv_w': out['v_conv_w'], 'v_conv_b': out['v_conv_b'], 'v_dt_bias': out['v_dt_bias'], 'v_ssd_a_log': out['v_ssd_a_log'], 'v_ssd_d': out['v_ssd_d'], 'v_ssd_norm_w': out['v_ssd_norm_w'], 'v_proj_a': out['v_proj_a'], 'v_proj_b': out['v_proj_b'], 'v_proj_c': out['v_proj_c'], 'v_w_out': out['v_w_out']}


def _loss(weights, diff, rest, loss_target):
    with _jax.named_scope("forward"):
        args = {**rest, TWIN_DIFF_INPUT: diff, **{k: w.astype(_WEIGHT_DTYPES[k]) for k, w in weights.items()}}
        y = _forward(args)
    with _jax.named_scope("loss_head"):
        err = _jnp.square(y.astype(_jnp.float32) - loss_target)
        return 0.5 * _jnp.sum(_jnp.mean(err, axis=-1)) if err.ndim else 0.5 * err


def _adamw(w, g, m, v):
    m = ADAM_B1 * m + (1.0 - ADAM_B1) * g
    v = ADAM_B2 * v + (1.0 - ADAM_B2) * _jnp.square(g)
    m_hat = m / (1.0 - ADAM_B1 ** ADAM_STEP)
    v_hat = v / (1.0 - ADAM_B2 ** ADAM_STEP)
    delta = -ADAM_LR * (m_hat / (_jnp.sqrt(v_hat) + ADAM_EPS) + ADAM_WD * w)
    return delta, m, v


def reference(x, norm_w, w_in, s5_a_re, s5_a_im, s5_log_step, s5_b_re, s5_b_im, s5_c_re, s5_c_im, s5_d, s5_glu_w, s5_glu_b, q_norm_w, k_norm_w, conv_w, conv_b, dt_bias, ssd_a_log, ssd_d, ssd_norm_w, proj_a, proj_b, proj_c, w_out, loss_target, m_norm_w, m_w_in, m_s5_a_re, m_s5_a_im, m_s5_log_step, m_s5_b_re, m_s5_b_im, m_s5_c_re, m_s5_c_im, m_s5_d, m_s5_glu_w, m_s5_glu_b, m_q_norm_w, m_k_norm_w, m_conv_w, m_conv_b, m_dt_bias, m_ssd_a_log, m_ssd_d, m_ssd_norm_w, m_proj_a, m_proj_b, m_proj_c, m_w_out, v_norm_w, v_w_in, v_s5_a_re, v_s5_a_im, v_s5_log_step, v_s5_b_re, v_s5_b_im, v_s5_c_re, v_s5_c_im, v_s5_d, v_s5_glu_w, v_s5_glu_b, v_q_norm_w, v_k_norm_w, v_conv_w, v_conv_b, v_dt_bias, v_ssd_a_log, v_ssd_d, v_ssd_norm_w, v_proj_a, v_proj_b, v_proj_c, v_w_out):
    given = dict(x=x, norm_w=norm_w, w_in=w_in, s5_a_re=s5_a_re, s5_a_im=s5_a_im, s5_log_step=s5_log_step, s5_b_re=s5_b_re, s5_b_im=s5_b_im, s5_c_re=s5_c_re, s5_c_im=s5_c_im, s5_d=s5_d, s5_glu_w=s5_glu_w, s5_glu_b=s5_glu_b, q_norm_w=q_norm_w, k_norm_w=k_norm_w, conv_w=conv_w, conv_b=conv_b, dt_bias=dt_bias, ssd_a_log=ssd_a_log, ssd_d=ssd_d, ssd_norm_w=ssd_norm_w, proj_a=proj_a, proj_b=proj_b, proj_c=proj_c, w_out=w_out, loss_target=loss_target, m_norm_w=m_norm_w, m_w_in=m_w_in, m_s5_a_re=m_s5_a_re, m_s5_a_im=m_s5_a_im, m_s5_log_step=m_s5_log_step, m_s5_b_re=m_s5_b_re, m_s5_b_im=m_s5_b_im, m_s5_c_re=m_s5_c_re, m_s5_c_im=m_s5_c_im, m_s5_d=m_s5_d, m_s5_glu_w=m_s5_glu_w, m_s5_glu_b=m_s5_glu_b, m_q_norm_w=m_q_norm_w, m_k_norm_w=m_k_norm_w, m_conv_w=m_conv_w, m_conv_b=m_conv_b, m_dt_bias=m_dt_bias, m_ssd_a_log=m_ssd_a_log, m_ssd_d=m_ssd_d, m_ssd_norm_w=m_ssd_norm_w, m_proj_a=m_proj_a, m_proj_b=m_proj_b, m_proj_c=m_proj_c, m_w_out=m_w_out, v_norm_w=v_norm_w, v_w_in=v_w_in, v_s5_a_re=v_s5_a_re, v_s5_a_im=v_s5_a_im, v_s5_log_step=v_s5_log_step, v_s5_b_re=v_s5_b_re, v_s5_b_im=v_s5_b_im, v_s5_c_re=v_s5_c_re, v_s5_c_im=v_s5_c_im, v_s5_d=v_s5_d, v_s5_glu_w=v_s5_glu_w, v_s5_glu_b=v_s5_glu_b, v_q_norm_w=v_q_norm_w, v_k_norm_w=v_k_norm_w, v_conv_w=v_conv_w, v_conv_b=v_conv_b, v_dt_bias=v_dt_bias, v_ssd_a_log=v_ssd_a_log, v_ssd_d=v_ssd_d, v_ssd_norm_w=v_ssd_norm_w, v_proj_a=v_proj_a, v_proj_b=v_proj_b, v_proj_c=v_proj_c, v_w_out=v_w_out)
    weights = {n: given[n] for n in TWIN_WEIGHTS}
    shared = {n: given[n] for n in SHARED_INPUTS}
    per_example = {n: given[n] for n in ['x']}
    grad_fn = _jax.value_and_grad(_loss, argnums=(0, 1))

    def one_microbatch(ex, loss_target):
        ex = dict(ex)
        diff = ex.pop(TWIN_DIFF_INPUT)
        return grad_fn(weights, diff, {**shared, **ex}, loss_target)

    if N_MICROBATCH == 1:
        loss, (grad_w, grad_x) = one_microbatch(per_example, given["loss_target"])
    else:
        def body(carry, xs):
            loss_sum, grad_sum = carry
            l_k, (gw_k, gx_k) = one_microbatch(xs[0], xs[1])
            with _jax.named_scope("update"):
                return (loss_sum + l_k, _jax.tree.map(_jnp.add, grad_sum, gw_k)), gx_k

        init = (_jnp.zeros((), _jnp.float32), _jax.tree.map(_jnp.zeros_like, weights))
        (loss, grad_w), grad_x = _jax.lax.scan(body, init, (per_example, given["loss_target"]))
    with _jax.named_scope("update"):
        delta_w, new_m, new_v = {}, {}, {}
        for n in TWIN_WEIGHTS:
            delta_w[n], new_m[n], new_v[n] = _adamw(weights[n], grad_w[n], given["m_" + n], given["v_" + n])
    return (loss, grad_x, *[grad_w[n] for n in TWIN_WEIGHTS], *[delta_w[n] for n in TWIN_WEIGHTS],
            *[new_m[n] for n in TWIN_WEIGHTS], *[new_v[n] for n in TWIN_WEIGHTS])
```

```python
import functools
import math
from typing import Any, NamedTuple

import jax
import jax.numpy as jnp
from jax import lax
from jax.experimental import pallas as pl
from jax.experimental.pallas import tpu as pltpu

f32 = jnp.float32
bf16 = jnp.bfloat16

N_DEV = 8
D_MODEL = 1024
RMS_EPS = 1e-6
S5_WIDTH = 512
S5_GROUPS = 32
S5_GROUP = 16
S5_STATE = 64
S5_TILE = 256
S5_ND = 8
S5_CHUNKS = 4
ATT_HEAD_DIM = 64
ATT_PAIRS = ((128, 1), (512, 4), (2048, 16))
ATT_HPG = 4
ATT_BLOCK = 128
ATT_GW = ATT_HPG * ATT_HEAD_DIM
ATT_WIDTH = 768
SSD_HEADS = 12
SSD_HEAD_DIM = 64
SSD_WIDTH = 768
SSD_STATE = 128
SSD_GROUPS = 2
SSD_CHUNK = 128
SSD_CONV = 4
SSD_CONV_DIM = 1280
HPAD = 128
IN_SPLITS = (512, 512, 768, 768, 768, 256, 1280, 12, 768, 3072)
ADAM_LR, ADAM_B1, ADAM_B2, ADAM_EPS, ADAM_WD, ADAM_STEP = 0.001, 0.9, 0.999, 1e-08, 0.01, 10
VMEM_LIMIT = 56 * 1024 * 1024

WEIGHTS = ['norm_w', 'w_in', 's5_a_re', 's5_a_im', 's5_log_step', 's5_b_re', 's5_b_im', 's5_c_re',
           's5_c_im', 's5_d', 's5_glu_w', 's5_glu_b', 'q_norm_w', 'k_norm_w', 'conv_w', 'conv_b',
           'dt_bias', 'ssd_a_log', 'ssd_d', 'ssd_norm_w', 'proj_a', 'proj_b', 'proj_c', 'w_out']
SHARDED = {'w_in': 1, 's5_glu_w': 1, 'conv_w': 2, 'proj_a': 2, 'proj_b': 2, 'proj_c': 2, 'w_out': 1}


class Arg(NamedTuple):
    arr: Any
    block: tuple
    imap: Any
    kind: str = 'const'
    gshape: Any = None
    gimap: Any = None


class Out(NamedTuple):
    shape: tuple
    dtype: Any
    block: tuple
    imap: Any


def _cparams(n):
    return pltpu.CompilerParams(dimension_semantics=("arbitrary",) * n, vmem_limit_bytes=VMEM_LIMIT)


def _rows(a, tm, kind='tile', col=0, width=None, gshape=None, gcol=None):
    width = a.shape[1] if width is None else width
    g = None if gshape is None else (lambda i, gc=(0 if gcol is None else gcol): (i, gc))
    return Arg(a, (tm, width), lambda i, c=col: (i, c), kind, gshape, g)


def _whole(a, kind='const'):
    nd = a.ndim
    return Arg(a, a.shape, lambda *i, nd=nd: (0,) * nd, kind)


def map_fwd(name, fn, grid, args, outs):
    n_in = len(args)

    def body(*refs):
        pid = tuple(pl.program_id(a) for a in range(len(grid)))
        res = fn(pid, *[r[...] for r in refs[:n_in]])
        for o, r in zip(refs[n_in:], res):
            o[...] = r.astype(o.dtype)

    res = pl.pallas_call(
        body, name=name, grid=grid,
        in_specs=[pl.BlockSpec(a.block, a.imap) for a in args],
        out_specs=[pl.BlockSpec(o.block, o.imap) for o in outs],
        out_shape=[jax.ShapeDtypeStruct(o.shape, o.dtype) for o in outs],
        compiler_params=_cparams(len(grid)))(*[a.arr for a in args])
    return tuple(res)


def _grad_outs(args, wrt):
    outs = []
    for i in wrt:
        a = args[i]
        shape = a.arr.shape if a.gshape is None else a.gshape
        imap = a.imap if a.gimap is None else a.gimap
        outs.append(Out(shape, f32, a.block, imap))
    return outs


def _store_grads(pid, args, wrt, grads, grefs, adds):
    first_all = functools.reduce(jnp.logical_and, [p == 0 for p in pid])
    first_in = functools.reduce(jnp.logical_and, [p == 0 for p in pid[1:]]) if len(pid) > 1 else first_all
    for j, i in enumerate(wrt):
        g = grads[j].astype(f32)
        ref = grefs[j]
        kind = args[i].kind
        if kind == 'tile':
            if j in adds:
                g = g + adds[j]
            ref[...] = g
        else:
            first = first_all if kind == 'acc' else first_in

            @pl.when(first)
            def _(ref=ref):
                ref[...] = jnp.zeros_like(ref)

            ref[...] += g


def map_bwd(name, fn, grid, args, douts, wrt, add=None):
    add = add or {}
    n_in, n_d, n_add = len(args), len(douts), len(add)
    add_keys = sorted(add)
    gouts = _grad_outs(args, wrt)

    def body(*refs):
        pid = tuple(pl.program_id(a) for a in range(len(grid)))
        vals = [r[...] for r in refs[:n_in]]
        dvals = [r[...].astype(f32) for r in refs[n_in:n_in + n_d]]
        avals = {k: refs[n_in + n_d + j][...].astype(f32) for j, k in enumerate(add_keys)}
        grefs = refs[n_in + n_d + n_add:]

        def f(*w):
            full = list(vals)
            for i, x in zip(wrt, w):
                full[i] = x
            return tuple(fn(pid, *full))

        _, vjp = jax.vjp(f, *[vals[i] for i in wrt])
        grads = vjp(tuple(dvals))
        _store_grads(pid, args, wrt, grads, grefs, avals)

    ins = list(args) + list(douts) + [add[k] for k in add_keys]
    res = pl.pallas_call(
        body, name=name, grid=grid,
        in_specs=[pl.BlockSpec(a.block, a.imap) for a in ins],
        out_specs=[pl.BlockSpec(o.block, o.imap) for o in gouts],
        out_shape=[jax.ShapeDtypeStruct(o.shape, o.dtype) for o in gouts],
        compiler_params=_cparams(len(grid)))(*[a.arr for a in ins])
    return tuple(res)


def scan_fwd(name, fn, grid, carry_shapes, args, outs):
    no, nt = grid
    n_in, n_out, n_c = len(args), len(outs), len(carry_shapes)
    cks = [Out((no, nt) + cs, f32, (None, None) + cs, lambda o, t, n=len(cs): (o, t) + (0,) * n) for cs in carry_shapes]

    def body(*refs):
        pid = (pl.program_id(0), pl.program_id(1))
        ins = refs[:n_in]
        orefs = refs[n_in:n_in + n_out]
        ckrefs = refs[n_in + n_out:n_in + n_out + n_c]
        crefs = refs[n_in + n_out + n_c:]

        @pl.when(pid[1] == 0)
        def _():
            for c in crefs:
                c[...] = jnp.zeros_like(c)

        carry = tuple(c[...] for c in crefs)
        for ck, c in zip(ckrefs, carry):
            ck[...] = c
        res, newc = fn(pid, carry, *[r[...] for r in ins])
        for o, r in zip(orefs, res):
            o[...] = r.astype(o.dtype)
        for c, v in zip(crefs, newc):
            c[...] = v

    allouts = list(outs) + cks
    res = pl.pallas_call(
        body, name=name, grid=grid,
        in_specs=[pl.BlockSpec(a.block, a.imap) for a in args],
        out_specs=[pl.BlockSpec(o.block, o.imap) for o in allouts],
        out_shape=[jax.ShapeDtypeStruct(o.shape, o.dtype) for o in allouts],
        scratch_shapes=[pltpu.VMEM(cs, f32) for cs in carry_shapes],
        compiler_params=_cparams(2))(*[a.arr for a in args])
    return tuple(res[:n_out]), tuple(res[n_out:])


def scan_bwd(name, fn, grid, carry_shapes, args, ckpts, douts, wrt):
    no, nt = grid
    n_in, n_d, n_c = len(args), len(douts), len(carry_shapes)

    def rev(imap):
        return lambda o, t: imap(o, nt - 1 - t)

    rargs = [a._replace(imap=rev(a.imap), gimap=None if a.gimap is None else rev(a.gimap)) for a in args]
    rdouts = [a._replace(imap=rev(a.imap)) for a in douts]
    ckargs = [Arg(ck, (None, None) + cs, rev(lambda o, t, n=len(cs): (o, t) + (0,) * n)) for ck, cs in zip(ckpts, carry_shapes)]
    gouts = _grad_outs(rargs, wrt)

    def body(*refs):
        o, t = pl.program_id(0), pl.program_id(1)
        tt = nt - 1 - t
        vals = [r[...] for r in refs[:n_in]]
        dvals = [r[...].astype(f32) for r in refs[n_in:n_in + n_d]]
        carry = tuple(r[...] for r in refs[n_in + n_d:n_in + n_d + n_c])
        grefs = refs[n_in + n_d + n_c:n_in + n_d + n_c + len(wrt)]
        dcrefs = refs[n_in + n_d + n_c + len(wrt):]

        @pl.when(t == 0)
        def _():
            for c in dcrefs:
                c[...] = jnp.zeros_like(c)

        def f(carry, *w):
            full = list(vals)
            for i, x in zip(wrt, w):
                full[i] = x
            res, newc = fn((o, tt), carry, *full)
            return tuple(res), tuple(newc)

        _, vjp = jax.vjp(f, carry, *[vals[i] for i in wrt])
        grads = vjp((tuple(dvals), tuple(c[...] for c in dcrefs)))
        for c, g in zip(dcrefs, grads[0]):
            c[...] = g
        _store_grads((o, t), rargs, wrt, grads[1:], grefs, {})

    ins = rargs + rdouts + ckargs
    res = pl.pallas_call(
        body, name=name, grid=grid,
        in_specs=[pl.BlockSpec(a.block, a.imap) for a in ins],
        out_specs=[pl.BlockSpec(g.block, g.imap) for g in gouts],
        out_shape=[jax.ShapeDtypeStruct(g.shape, g.dtype) for g in gouts],
        scratch_shapes=[pltpu.VMEM(cs, f32) for cs in carry_shapes],
        compiler_params=_cparams(2))(*[a.arr for a in ins])
    return tuple(res)


def _pick(dim, target):
    if dim <= target:
        return dim
    for t in range(target, 127, -128):
        if dim % t == 0:
            return t
    return dim


def matmul(name, a, b, mode='nn', add=None, out_dtype=f32, tm=512, tn=512, tk=512):
    if mode == 'tn':
        K, M = a.shape
    else:
        M, K = a.shape
    N = b.shape[0] if mode == 'nt' else b.shape[1]
    assert (b.shape[1] if mode == 'nt' else b.shape[0]) == K
    tm, tn, tk = _pick(M, tm), _pick(N, tn), _pick(K, tk)
    nk = K // tk
    a_spec = pl.BlockSpec((tk, tm), lambda i, j, k: (k, i)) if mode == 'tn' else pl.BlockSpec((tm, tk), lambda i, j, k: (i, k))
    b_spec = pl.BlockSpec((tn, tk), lambda i, j, k: (j, k)) if mode == 'nt' else pl.BlockSpec((tk, tn), lambda i, j, k: (k, j))
    dims = {'nn': (((1,), (0,)), ((), ())), 'nt': (((1,), (1,)), ((), ())), 'tn': (((0,), (0,)), ((), ()))}[mode]
    has_add = add is not None

    def body(*refs):
        if has_add:
            a_ref, b_ref, add_ref, o_ref, acc = refs
        else:
            a_ref, b_ref, o_ref, acc = refs
        k = pl.program_id(2)

        @pl.when(k == 0)
        def _():
            acc[...] = add_ref[...].astype(f32) if has_add else jnp.zeros_like(acc)

        acc[...] += lax.dot_general(a_ref[...].astype(bf16), b_ref[...].astype(bf16), dims, preferred_element_type=f32)

        @pl.when(k == nk - 1)
        def _():
            o_ref[...] = acc[...].astype(o_ref.dtype)

    in_specs = [a_spec, b_spec] + ([pl.BlockSpec((tm, tn), lambda i, j, k: (i, j))] if has_add else [])
    ops = [a, b] + ([add] if has_add else [])
    return pl.pallas_call(
        body, name=name, grid=(M // tm, N // tn, nk), in_specs=in_specs,
        out_specs=pl.BlockSpec((tm, tn), lambda i, j, k: (i, j)),
        out_shape=jax.ShapeDtypeStruct((M, N), out_dtype),
        scratch_shapes=[pltpu.VMEM((tm, tn), f32)],
        compiler_params=pltpu.CompilerParams(dimension_semantics=("parallel", "parallel", "arbitrary"), vmem_limit_bytes=VMEM_LIMIT))(*ops)


def _dot(a, b, dims=(((1,), (0,)), ((), ()))):
    return lax.dot_general(a.astype(bf16), b.astype(bf16), dims, preferred_element_type=f32)


_NT = (((1,), (1,)), ((), ()))
_TN = (((0,), (0,)), ((), ()))


def _rmsnorm_tile(pid, x, w):
    return (x * lax.rsqrt(jnp.mean(x * x, axis=-1, keepdims=True) + RMS_EPS) * w,)


def _shift_rows(h, d, fill):
    pad = jnp.full((d, h.shape[1]), fill, f32)
    return jnp.concatenate([pad, h[:-d]], axis=0)


def _s5_prep_tile(pid, a_re, a_im, ls, btr, bti, ctr, cti):
    o = pid[0]
    w = a_re.shape[1]
    r = lax.broadcasted_iota(jnp.int32, (S5_GROUPS, w), 0)
    c = lax.broadcasted_iota(jnp.int32, (S5_GROUPS, w), 1)
    sel = (r == o * (w // S5_STATE) + c // S5_STATE).astype(f32)
    step = jnp.dot(jnp.exp(ls), sel, precision=lax.Precision.HIGHEST, preferred_element_type=f32)
    mag = jnp.exp(a_re * step)
    ang = a_im * step
    lr, li = mag * jnp.cos(ang), mag * jnp.sin(ang)
    nr, ni = lr - 1.0, li
    den = a_re * a_re + a_im * a_im
    fr = (nr * a_re + ni * a_im) / den
    fi = (ni * a_re - nr * a_im) / den
    bbr = fr * btr - fi * bti
    bbi = fr * bti + fi * btr
    reps = w // S5_STATE
    rr = lax.broadcasted_iota(jnp.int32, (reps * S5_GROUP, w), 0)
    cc = lax.broadcasted_iota(jnp.int32, (reps * S5_GROUP, w), 1)
    diag = (rr // S5_GROUP) == (cc // S5_STATE)

    def expand(m):
        return jnp.where(diag, jnp.concatenate([m] * reps, axis=0), 0.0)

    pr, pi = lr, li
    rows_r, rows_i = [pr], [pi]
    for _ in range(S5_ND - 1):
        pr, pi = pr * pr - pi * pi, 2.0 * pr * pi
        rows_r.append(pr)
        rows_i.append(pi)
    lamd_r, lamd_i = jnp.concatenate(rows_r, axis=0), jnp.concatenate(rows_i, axis=0)
    tr = jnp.broadcast_to(lr, (S5_TILE, w))
    ti = jnp.broadcast_to(li, (S5_TILE, w))
    for j in range(S5_ND):
        sr, si = _shift_rows(tr, 1 << j, 1.0), _shift_rows(ti, 1 << j, 0.0)
        tr, ti = tr * sr - ti * si, tr * si + ti * sr
    return lamd_r, lamd_i, tr, ti, expand(bbr), expand(bbi), expand(ctr), expand(cti)


def _s5_tile(pid, carry, u, lamd_r, lamd_i, lamt_r, lamt_i, bbr, bbi, ccr, cci, dvec):
    cr, ci = carry
    hr = _dot(u, bbr)
    hi = _dot(u, bbi)
    for j in range(S5_ND):
        sr, si = _shift_rows(hr, 1 << j, 0.0), _shift_rows(hi, 1 << j, 0.0)
        ar, ai = lamd_r[j:j + 1], lamd_i[j:j + 1]
        hr, hi = hr + ar * sr - ai * si, hi + ar * si + ai * sr
    hr, hi = hr + lamt_r * cr - lamt_i * ci, hi + lamt_r * ci + lamt_i * cr
    y = _dot(hr, ccr, _NT) - _dot(hi, cci, _NT) + dvec * u
    return (jax.nn.gelu(y),), (hr[-1:], hi[-1:])


def _glu_tile(pid, g, glu, za, b):
    return (g * jax.nn.sigmoid(glu + b) * jax.nn.silu(za),)


def _attn_tile(pid, carry, q, k, v, qw, kw):
    n = pid[1]
    kp, vp = carry
    D = ATT_HEAD_DIM

    def heads(x):
        return [x[:, h * D:(h + 1) * D] for h in range(ATT_HPG)]

    def hnorm(x, w):
        return [xh * lax.rsqrt(jnp.mean(xh * xh, axis=-1, keepdims=True) + RMS_EPS) * w for xh in heads(x)]

    qh, kh = hnorm(q, qw), hnorm(k, kw)
    kph, vph, vh = heads(kp), heads(vp), heads(v)
    r = lax.broadcasted_iota(jnp.int32, (ATT_BLOCK, 2 * ATT_BLOCK), 0)
    c = lax.broadcasted_iota(jnp.int32, (ATT_BLOCK, 2 * ATT_BLOCK), 1)
    diff = r + ATT_BLOCK - c
    mask = (diff >= 0) & (diff <= ATT_BLOCK) & ((c >= ATT_BLOCK) | (n > 0))
    outs, lses = [], []
    for h in range(ATT_HPG):
        kk = jnp.concatenate([kph[h], kh[h]], axis=0)
        vv = jnp.concatenate([vph[h], vh[h]], axis=0)
        s = _dot(qh[h], kk, _NT) * (D ** -0.5)
        s = jnp.where(mask, s, -1e30)
        m = jnp.max(s, axis=-1, keepdims=True)
        p = jnp.exp(s - m)
        l = jnp.sum(p, axis=-1, keepdims=True)
        outs.append(_dot(p / l, vv))
        lses.append(jnp.broadcast_to(m + jnp.log(l), (ATT_BLOCK, D)))
    return (jnp.concatenate(outs, axis=1), jnp.concatenate(lses, axis=1)), (jnp.concatenate(kh, axis=1), v)


def _combine_tile(pid, o1, l1, o2, l2, o3, l3, zb):
    m = jnp.maximum(jnp.maximum(l1, l2), l3)
    e1, e2, e3 = jnp.exp(l1 - m), jnp.exp(l2 - m), jnp.exp(l3 - m)
    y = (e1 * o1 + e2 * o2 + e3 * o3) / (e1 + e2 + e3)
    return (y * jax.nn.silu(zb),)


def _softplus(x):
    return jnp.maximum(x, 0.0) + jnp.log(1.0 + jnp.exp(-jnp.abs(x)))


def _ssd_tile(pid, carry, xbc, dt, z, conv_w, conv_b, dt_bias, a_log, dvec, norm_w):
    xprev, state = carry
    T, P, N = SSD_CHUNK, SSD_HEAD_DIM, SSD_STATE
    xx = jnp.concatenate([xprev, xbc], axis=0)
    conv = conv_b
    for k in range(SSD_CONV):
        off = 8 - (SSD_CONV - 1) + k
        conv = conv + conv_w[k:k + 1] * xx[off:off + T]
    xc = jax.nn.silu(conv)
    dtp = _softplus(dt + dt_bias)
    a_dt = dtp * (-jnp.exp(a_log))
    r = lax.broadcasted_iota(jnp.int32, (T, T), 0)
    c = lax.broadcasted_iota(jnp.int32, (T, T), 1)
    tri = r >= c
    trif = tri.astype(f32)
    hi = lax.Precision.HIGHEST
    a_cs = jnp.dot(trif, a_dt, precision=hi, preferred_element_type=f32)
    a_cs_t = lax.dot_general(a_dt, trif, (((0,), (1,)), ((), ())), precision=hi, preferred_element_type=f32)
    ys, states = [], []
    for g in range(SSD_GROUPS):
        bg = xc[:, SSD_WIDTH + g * N:SSD_WIDTH + (g + 1) * N]
        cg = xc[:, SSD_WIDTH + SSD_GROUPS * N + g * N:SSD_WIDTH + SSD_GROUPS * N + (g + 1) * N]
        cb = _dot(cg, bg, _NT)
        for e in range(SSD_HEADS // SSD_GROUPS):
            h = g * (SSD_HEADS // SSD_GROUPS) + e
            col, row = a_cs[:, h:h + 1], a_cs_t[h:h + 1, :]
            decay = jnp.exp(jnp.where(tri, col - row, -1e30))
            xh = xc[:, h * P:(h + 1) * P]
            xdt = xh * dtp[:, h:h + 1]
            st = state[h * P:(h + 1) * P, :]
            last = a_cs[T - 1:T, h:h + 1]
            y = _dot(cb * decay, xdt) + _dot(cg, st, _NT) * jnp.exp(col) + xh * dvec[:, h:h + 1]
            ys.append(y)
            states.append(jnp.exp(last) * st + _dot(xdt * jnp.exp(last - col), bg, _TN))
    y = jnp.concatenate(ys, axis=1) * jax.nn.silu(z)
    out = y * lax.rsqrt(jnp.mean(y * y, axis=-1, keepdims=True) + RMS_EPS) * norm_w
    return (out,), (xbc[T - 8:], jnp.concatenate(states, axis=0))


def _merge_tile(pid, pa, pb, pc, g0, g1, g2):
    return (jax.nn.sigmoid(g0) * pa + jax.nn.sigmoid(g1) * pb + jax.nn.sigmoid(g2) * pc,)


def loss_and_grad(y, target, tm=512):
    S, D = y.shape
    nt = S // tm

    def body(y_ref, t_ref, dy_ref, l_ref, acc):
        i = pl.program_id(0)

        @pl.when(i == 0)
        def _():
            acc[...] = jnp.zeros_like(acc)

        diff = y_ref[...] - t_ref[...]
        dy_ref[...] = diff * (1.0 / D)
        acc[...] += jnp.sum((diff * diff).reshape(tm // 8, 8, D), axis=0)

        @pl.when(i == nt - 1)
        def _():
            l_ref[...] = jnp.broadcast_to(0.5 / D * jnp.sum(acc[...]), l_ref.shape)

    dy, l = pl.pallas_call(
        body, name="loss_head", grid=(nt,),
        in_specs=[pl.BlockSpec((tm, D), lambda i: (i, 0))] * 2,
        out_specs=[pl.BlockSpec((tm, D), lambda i: (i, 0)), pl.BlockSpec((8, 128), lambda i: (0, 0))],
        out_shape=[jax.ShapeDtypeStruct((S, D), f32), jax.ShapeDtypeStruct((8, 128), f32)],
        scratch_shapes=[pltpu.VMEM((8, D), f32)],
        compiler_params=_cparams(1))(y, target)
    return dy, l[0, 0]


def _row_tile(R, C, budget=1 << 20):
    best = R
    for t in range(8, R, 8):
        if R % t == 0 and t * C * 4 <= budget:
            best = t
    if best == R and R * C * 4 > budget:
        for t in range(8, R, 8):
            if R % t == 0:
                return t
    return best


def adamw(name, w, g, m, v):
    shape = w.shape
    size = math.prod(shape)
    C = 128 if size % 128 == 0 else shape[-1]
    R = size // C
    tr = _row_tile(R, C)

    def body(w_ref, g_ref, m_ref, v_ref, d_ref, nm_ref, nv_ref):
        gg = g_ref[...]
        nm = ADAM_B1 * m_ref[...] + (1.0 - ADAM_B1) * gg
        nv = ADAM_B2 * v_ref[...] + (1.0 - ADAM_B2) * jnp.square(gg)
        m_hat = nm / (1.0 - ADAM_B1 ** ADAM_STEP)
        v_hat = nv / (1.0 - ADAM_B2 ** ADAM_STEP)
        d_ref[...] = -ADAM_LR * (m_hat / (jnp.sqrt(v_hat) + ADAM_EPS) + ADAM_WD * w_ref[...])
        nm_ref[...] = nm
        nv_ref[...] = nv

    spec = pl.BlockSpec((tr, C), lambda i: (i, 0))
    res = pl.pallas_call(
        body, name=name, grid=(R // tr,), in_specs=[spec] * 4, out_specs=[spec] * 3,
        out_shape=[jax.ShapeDtypeStruct((R, C), f32)] * 3,
        compiler_params=_cparams(1))(*[t.reshape(R, C) for t in (w, g, m, v)])
    return tuple(t.reshape(shape) for t in res)


def sum_slots(name, x, tr=256):
    n, R, C = x.shape
    tr = _pick_rows(R, tr)

    def body(x_ref, o_ref):
        acc = x_ref[0]
        for s in range(1, n):
            acc = acc + x_ref[s]
        o_ref[...] = acc

    return pl.pallas_call(
        body, name=name, grid=(R // tr,),
        in_specs=[pl.BlockSpec((n, tr, C), lambda i: (0, i, 0))],
        out_specs=pl.BlockSpec((tr, C), lambda i: (i, 0)),
        out_shape=jax.ShapeDtypeStruct((R, C), f32),
        compiler_params=_cparams(1))(x)


def _pick_rows(R, target):
    if R <= target:
        return R
    for t in range(target, 7, -8):
        if R % t == 0:
            return t
    return R


def exchange(name, src, all_to_all):
    slab = src.shape[1:] if all_to_all else src.shape

    def body(src_ref, out_ref, send_sems, recv_sems, local_sem):
        x, y, c = lax.axis_index("x"), lax.axis_index("y"), lax.axis_index("c")
        me = 4 * x + 2 * y + c
        copies = []
        for k in (1, 2, 4, 3, 5, 6, 7):
            px = 1 - x if k & 4 else x
            py = 1 - y if k & 2 else y
            pc = 1 - c if k & 1 else c
            s = src_ref.at[4 * px + 2 * py + pc] if all_to_all else src_ref
            cp = pltpu.make_async_remote_copy(
                src_ref=s, dst_ref=out_ref.at[me], send_sem=send_sems.at[k - 1], recv_sem=recv_sems.at[k - 1],
                device_id=(px, py, pc), device_id_type=pl.DeviceIdType.MESH)
            cp.start()
            copies.append(cp)
        own = pltpu.make_async_copy(src_ref.at[me] if all_to_all else src_ref, out_ref.at[me], local_sem)
        own.start()
        for cp in copies:
            cp.wait()
        own.wait()

    return pl.pallas_call(
        body, name=name,
        in_specs=[pl.BlockSpec(memory_space=pl.ANY)],
        out_specs=pl.BlockSpec(memory_space=pl.ANY),
        out_shape=jax.ShapeDtypeStruct((N_DEV,) + tuple(slab), src.dtype),
        scratch_shapes=[pltpu.SemaphoreType.DMA((N_DEV - 1,)), pltpu.SemaphoreType.DMA((N_DEV - 1,)), pltpu.SemaphoreType.DMA],
    )(src)


def _pack(parts, dtype, lane=128, mult=8):
    flat = jnp.concatenate([p.reshape(-1).astype(dtype) for p in parts])
    n = flat.shape[0]
    tot = -(-n // (lane * mult)) * lane * mult
    return jnp.pad(flat, (0, tot - n)).reshape(tot // lane, lane)


def _unpack(flat, shapes):
    flat = flat.reshape(-1)
    out, off = [], 0
    for s in shapes:
        n = math.prod(s)
        out.append(flat[off:off + n].reshape(s))
        off += n
    return out


def _relayout_w_in(w):
    offs = [0]
    for s in IN_SPLITS:
        offs.append(offs[-1] + s)
    p = [w[:, offs[i]:offs[i + 1]] for i in range(len(IN_SPLITS))]
    ua, za, q, k, v, zb, xbc, dt, zc, gates = p
    dtp = jnp.pad(dt, ((0, 0), (0, HPAD - dt.shape[1])))
    return (jnp.concatenate([ua, za], 1), jnp.concatenate([q, k, v], 1),
            jnp.concatenate([xbc, zb, zc], 1), jnp.concatenate([gates, dtp], 1))


def _pad_lanes(v, n=HPAD):
    return jnp.pad(v.reshape(1, -1), ((0, 0), (0, n - v.shape[-1])))


def _s5_prep_args(W):
    g2 = S5_GROUPS * S5_STATE
    w = g2 // S5_CHUNKS
    a_re, a_im = W['s5_a_re'].reshape(1, g2), W['s5_a_im'].reshape(1, g2)
    ls = W['s5_log_step'].reshape(1, S5_GROUPS)
    btr, bti = W['s5_b_re'].reshape(g2, S5_GROUP).T, W['s5_b_im'].reshape(g2, S5_GROUP).T
    ctr = W['s5_c_re'].transpose(1, 0, 2).reshape(S5_GROUP, g2)
    cti = W['s5_c_im'].transpose(1, 0, 2).reshape(S5_GROUP, g2)
    col = lambda a, rows: Arg(a, (rows, w), lambda o: (0, o), 'tile')
    return [col(a_re, 1), col(a_im, 1), _whole(ls, 'acc'), col(btr, S5_GROUP), col(bti, S5_GROUP), col(ctr, S5_GROUP), col(cti, S5_GROUP)]


def _s5_prep_outs():
    g2 = S5_GROUPS * S5_STATE
    w = g2 // S5_CHUNKS
    rows = (S5_ND, S5_ND, S5_TILE, S5_TILE, 128, 128, 128, 128)
    return [Out((r, g2), f32, (r, w), lambda o: (0, o)) for r in rows]


def _s5_args(A, prep, dvec, S):
    w = S5_GROUPS * S5_STATE // S5_CHUNKS
    args = [Arg(A, (S5_TILE, 128), lambda o, t: (t, o), 'tile', (S, S5_WIDTH), None)]
    for p in prep:
        args.append(Arg(p, (p.shape[0], w), lambda o, t: (0, o), 'acc0'))
    args.append(Arg(dvec, (1, 128), lambda o, t: (0, o), 'acc0'))
    return args


def _attn_args(QKV, g, r, qw, kw, S):
    L = S // r
    view = QKV.reshape(L, r * 3 * ATT_WIDTH)
    nblk = 3 * ATT_WIDTH // ATT_GW
    gshape = (L, r * ATT_GW)
    gimap = lambda rho, n: (n, rho)
    mk = lambda j: Arg(view, (ATT_BLOCK, ATT_GW), lambda rho, n, j=j: (n, rho * nblk + j), 'tile', gshape, gimap)
    return [mk(g), mk(3 + g), mk(6 + g), _whole(qw, 'acc'), _whole(kw, 'acc')]


def _ssd_args(C, G, W, S):
    T = SSD_CHUNK
    return [Arg(C, (T, SSD_CONV_DIM), lambda o, t: (t, 0), 'tile', (S, SSD_CONV_DIM), None),
            Arg(G, (T, HPAD), lambda o, t: (t, 3 * D_MODEL // HPAD), 'tile', (S, HPAD), lambda o, t: (t, 0)),
            Arg(C, (T, SSD_WIDTH), lambda o, t: (t, 2), 'tile', (S, SSD_WIDTH), lambda o, t: (t, 0)),
            _whole(W['conv_w'], 'acc'), _whole(W['conv_b'].reshape(1, -1), 'acc'),
            _whole(_pad_lanes(W['dt_bias']), 'acc'), _whole(_pad_lanes(W['ssd_a_log']), 'acc'),
            _whole(_pad_lanes(W['ssd_d']), 'acc'), _whole(W['ssd_norm_w'].reshape(1, -1), 'acc')]


_SSD_CARRY = ((8, SSD_CONV_DIM), (SSD_WIDTH, SSD_STATE))
_ATT_CARRY = ((ATT_BLOCK, ATT_GW), (ATT_BLOCK, ATT_GW))
_S5_CARRY = ((1, 512), (1, 512))


def layer_fwd(li, x, W):
    S = x.shape[0]
    n = lambda s: f"l{li}_{s}"
    sv = {'x': x}
    (h,) = map_fwd(n("norm"), _rmsnorm_tile, (S // 512,), [_rows(x, 512), _whole(W['norm_w'].reshape(1, -1))],
                   [Out((S, D_MODEL), bf16, (512, D_MODEL), lambda i: (i, 0))])
    wA, wQ, wC, wG = W['w_in_pieces']
    A = matmul(n("in_a"), h, wA)
    QKV = matmul(n("in_qkv"), h, wQ)
    C = matmul(n("in_c"), h, wC)
    G = matmul(n("in_g"), h, wG)
    sv.update(h=h, A=A, QKV=QKV, C=C, G=G)

    prep = map_fwd(n("s5_prep"), _s5_prep_tile, (S5_CHUNKS,), _s5_prep_args(W), _s5_prep_outs())
    dvec = W['s5_d'].reshape(1, -1)
    (g,), s5_ck = scan_fwd(n("s5_scan"), _s5_tile, (S5_CHUNKS, S // S5_TILE), _S5_CARRY, _s5_args(A, prep, dvec, S),
                           [Out((S, S5_WIDTH), f32, (S5_TILE, 128), lambda o, t: (t, o))])
    glu = matmul(n("glu"), g, W['s5_glu_w'])
    glu_b = W['s5_glu_b'].reshape(1, -1)
    (ya,) = map_fwd(n("glu_gate"), _glu_tile, (S // 512,),
                    [_rows(g, 512), _rows(glu, 512), _rows(A, 512, col=1, width=S5_WIDTH), _whole(glu_b)],
                    [Out((S, S5_WIDTH), f32, (512, S5_WIDTH), lambda i: (i, 0))])
    sv.update(prep=prep, g=g, glu=glu, ya=ya, s5_ck=s5_ck)

    qw, kw = W['q_norm_w'].reshape(1, -1), W['k_norm_w'].reshape(1, -1)
    att, att_ck = [], []
    for gi, (window, r) in enumerate(ATT_PAIRS):
        assert window // r == ATT_BLOCK and S % (r * ATT_BLOCK) == 0
        L = S // r
        spec = Out((L, r * ATT_GW), f32, (ATT_BLOCK, ATT_GW), lambda rho, nb: (nb, rho))
        (o, lse), ck = scan_fwd(n(f"attn{gi}"), _attn_tile, (r, L // ATT_BLOCK), _ATT_CARRY, _attn_args(QKV, gi, r, qw, kw, S), [spec, spec])
        att += [o.reshape(S, ATT_GW), lse.reshape(S, ATT_GW)]
        att_ck.append(ck)
    (yb,) = map_fwd(n("combine"), _combine_tile, (S // 512,),
                    [_rows(t, 512) for t in att] + [_rows(C, 512, col=SSD_CONV_DIM // ATT_GW, width=ATT_GW)],
                    [Out((S, ATT_GW), f32, (512, ATT_GW), lambda i: (i, 0))])
    sv.update(att=att, att_ck=att_ck, yb=yb)

    (yc,), ssd_ck = scan_fwd(n("ssd"), _ssd_tile, (1, S // SSD_CHUNK), _SSD_CARRY, _ssd_args(C, G, W, S),
                             [Out((S, SSD_WIDTH), f32, (SSD_CHUNK, SSD_WIDTH), lambda o, t: (t, 0))])
    sv.update(yc=yc, ssd_ck=ssd_ck)

    pa = matmul(n("proj_a"), ya, W['proj_a'])
    pb = matmul(n("proj_b"), yb, W['proj_b'])
    pc = matmul(n("proj_c"), yc, W['proj_c'])
    (merged,) = map_fwd(n("merge"), _merge_tile, (S // 256,),
                        [_rows(pa, 256), _rows(pb, 256), _rows(pc, 256)] + [_rows(G, 256, col=j, width=D_MODEL) for j in range(3)],
                        [Out((S, D_MODEL), f32, (256, D_MODEL), lambda i: (i, 0))])
    out = matmul(n("w_out"), merged, W['w_out'], add=x)
    sv.update(pa=pa, pb=pb, pc=pc, merged=merged)
    return out, sv


def layer_bwd(li, dout, sv, W):
    S = dout.shape[0]
    n = lambda s: f"l{li}_{s}"
    gr = {}
    x, A, QKV, C, G = sv['x'], sv['A'], sv['QKV'], sv['C'], sv['G']

    dmerged = matmul(n("d_merged"), dout, W['w_out'], 'nt')
    gr['w_out'] = matmul(n("g_w_out"), sv['merged'], dout, 'tn')
    margs = [_rows(sv['pa'], 256), _rows(sv['pb'], 256), _rows(sv['pc'], 256)] + \
            [_rows(G, 256, col=j, width=D_MODEL, gshape=(S, D_MODEL)) for j in range(3)]
    dpa, dpb, dpc, dg0, dg1, dg2 = map_bwd(n("merge_bwd"), _merge_tile, (S // 256,), margs, [_rows(dmerged, 256)], list(range(6)))
    dya = matmul(n("d_ya"), dpa, W['proj_a'], 'nt')
    dyb = matmul(n("d_yb"), dpb, W['proj_b'], 'nt')
    dyc = matmul(n("d_yc"), dpc, W['proj_c'], 'nt')
    gr['proj_a'] = matmul(n("g_proj_a"), sv['ya'], dpa, 'tn')
    gr['proj_b'] = matmul(n("g_proj_b"), sv['yb'], dpb, 'tn')
    gr['proj_c'] = matmul(n("g_proj_c"), sv['yc'], dpc, 'tn')

    glu_b = W['s5_glu_b'].reshape(1, -1)
    gargs = [_rows(sv['g'], 512), _rows(sv['glu'], 512), _rows(A, 512, col=1, width=S5_WIDTH, gshape=(S, S5_WIDTH)), _whole(glu_b, 'acc')]
    dg_a, dglu, dza, dglu_b = map_bwd(n("glu_gate_bwd"), _glu_tile, (S // 512,), gargs, [_rows(dya, 512)], [0, 1, 2, 3])
    gr['s5_glu_b'] = dglu_b.reshape(-1)
    dg = matmul(n("d_g"), dglu, W['s5_glu_w'], 'nt', add=dg_a)
    gr['s5_glu_w'] = matmul(n("g_glu_w"), sv['g'], dglu, 'tn')
    dvec = W['s5_d'].reshape(1, -1)
    sargs = _s5_args(A, sv['prep'], dvec, S)
    res = scan_bwd(n("s5_scan_bwd"), _s5_tile, (S5_CHUNKS, S // S5_TILE), _S5_CARRY, sargs, sv['s5_ck'],
                   [Arg(dg, (S5_TILE, 128), lambda o, t: (t, o))], list(range(len(sargs))))
    dua, dprep, dd = res[0], res[1:9], res[9]
    gr['s5_d'] = dd.reshape(-1)
    pargs = _s5_prep_args(W)
    pouts = _s5_prep_outs()
    da_re, da_im, dls, dbtr, dbti, dctr, dcti = map_bwd(
        n("s5_prep_bwd"), _s5_prep_tile, (S5_CHUNKS,), pargs,
        [Arg(d, o.block, o.imap) for d, o in zip(dprep, pouts)], list(range(7)))
    gshape = (S5_GROUPS, S5_STATE)
    gr['s5_a_re'], gr['s5_a_im'] = da_re.reshape(gshape), da_im.reshape(gshape)
    gr['s5_log_step'] = dls.reshape(-1)
    gr['s5_b_re'] = dbtr.T.reshape(S5_GROUPS, S5_STATE, S5_GROUP)
    gr['s5_b_im'] = dbti.T.reshape(S5_GROUPS, S5_STATE, S5_GROUP)
    gr['s5_c_re'] = dctr.reshape(S5_GROUP, S5_GROUPS, S5_STATE).transpose(1, 0, 2)
    gr['s5_c_im'] = dcti.reshape(S5_GROUP, S5_GROUPS, S5_STATE).transpose(1, 0, 2)

    cargs = [_rows(t, 512) for t in sv['att']] + [_rows(C, 512, col=SSD_CONV_DIM // ATT_GW, width=ATT_GW, gshape=(S, ATT_GW))]
    cres = map_bwd(n("combine_bwd"), _combine_tile, (S // 512,), cargs, [_rows(dyb, 512)], list(range(7)))
    dzb = cres[6]
    qw, kw = W['q_norm_w'].reshape(1, -1), W['k_norm_w'].reshape(1, -1)
    dqs, dks, dvs = [], [], []
    dqw = dkw = None
    for gi, (window, r) in enumerate(ATT_PAIRS):
        L = S // r
        dspec = lambda t: Arg(t.reshape(L, r * ATT_GW), (ATT_BLOCK, ATT_GW), lambda rho, nb: (nb, rho))
        dq, dk, dv, dqw_g, dkw_g = scan_bwd(n(f"attn{gi}_bwd"), _attn_tile, (r, L // ATT_BLOCK), _ATT_CARRY,
                                            _attn_args(QKV, gi, r, qw, kw, S), sv['att_ck'][gi],
                                            [dspec(cres[2 * gi]), dspec(cres[2 * gi + 1])], [0, 1, 2, 3, 4])
        dqs.append(dq.reshape(S, ATT_GW))
        dks.append(dk.reshape(S, ATT_GW))
        dvs.append(dv.reshape(S, ATT_GW))
        dqw = dqw_g if dqw is None else dqw + dqw_g
        dkw = dkw_g if dkw is None else dkw + dkw_g
    gr['q_norm_w'], gr['k_norm_w'] = dqw.reshape(-1), dkw.reshape(-1)

    ssd_args = _ssd_args(C, G, W, S)
    sres = scan_bwd(n("ssd_bwd"), _ssd_tile, (1, S // SSD_CHUNK), _SSD_CARRY, ssd_args, sv['ssd_ck'],
                    [Arg(dyc, (SSD_CHUNK, SSD_WIDTH), lambda o, t: (t, 0))], list(range(9)))
    dxbc, ddt, dzc = sres[0], sres[1], sres[2]
    gr['conv_w'] = sres[3]
    gr['conv_b'] = sres[4].reshape(-1)
    gr['dt_bias'] = sres[5].reshape(-1)[:SSD_HEADS]
    gr['ssd_a_log'] = sres[6].reshape(-1)[:SSD_HEADS]
    gr['ssd_d'] = sres[7].reshape(-1)[:SSD_HEADS]
    gr['ssd_norm_w'] = sres[8].reshape(-1)

    dproj = jnp.concatenate([dua, dza] + dqs + dks + dvs + [dxbc, dzb, dzc, dg0, dg1, dg2, ddt], axis=1)
    w_all = W['w_in_all']
    dh = matmul(n("d_h"), dproj, w_all, 'nt', tk=384)
    gr['w_in_relaid'] = matmul(n("g_w_in"), sv['h'], dproj, 'tn', tn=384)
    nargs = [_rows(x, 512), _whole(W['norm_w'].reshape(1, -1), 'acc')]
    dx, dnw = map_bwd(n("norm_bwd"), _rmsnorm_tile, (S // 512,), nargs, [_rows(dh, 512)], [0, 1], add={0: _rows(dout, 512)})
    gr['norm_w'] = dnw.reshape(-1)
    return dx, gr


def _unrelayout_w_in_grad(g):
    widths = [512, 512, 768, 768, 768, 1280, 256, 768, 3072, HPAD]
    names = ['ua', 'za', 'q', 'k', 'v', 'xbc', 'zb', 'zc', 'gates', 'dt']
    p, off = {}, 0
    for nm, wd in zip(names, widths):
        p[nm] = g[:, off:off + wd]
        off += wd
    return jnp.concatenate([p['ua'], p['za'], p['q'], p['k'], p['v'], p['zb'], p['xbc'], p['dt'][:, :12], p['zc'], p['gates']], axis=1)


def _shard_axis_to_front(full, axis):
    shp = full.shape
    t = full.reshape(shp[:axis] + (N_DEV, shp[axis] // N_DEV) + shp[axis + 1:])
    return jnp.moveaxis(t, axis, 0)


def _front_to_shard_axis(stacked, axis):
    t = jnp.moveaxis(stacked, 0, axis)
    shp = t.shape
    return t.reshape(shp[:axis] + (shp[axis] * shp[axis + 1],) + shp[axis + 2:])


def kernel(x, norm_w, w_in, s5_a_re, s5_a_im, s5_log_step, s5_b_re, s5_b_im, s5_c_re, s5_c_im, s5_d, s5_glu_w, s5_glu_b, q_norm_w, k_norm_w, conv_w, conv_b, dt_bias, ssd_a_log, ssd_d, ssd_norm_w, proj_a, proj_b, proj_c, w_out, loss_target, m_norm_w, m_w_in, m_s5_a_re, m_s5_a_im, m_s5_log_step, m_s5_b_re, m_s5_b_im, m_s5_c_re, m_s5_c_im, m_s5_d, m_s5_glu_w, m_s5_glu_b, m_q_norm_w, m_k_norm_w, m_conv_w, m_conv_b, m_dt_bias, m_ssd_a_log, m_ssd_d, m_ssd_norm_w, m_proj_a, m_proj_b, m_proj_c, m_w_out, v_norm_w, v_w_in, v_s5_a_re, v_s5_a_im, v_s5_log_step, v_s5_b_re, v_s5_b_im, v_s5_c_re, v_s5_c_im, v_s5_d, v_s5_glu_w, v_s5_glu_b, v_q_norm_w, v_k_norm_w, v_conv_w, v_conv_b, v_dt_bias, v_ssd_a_log, v_ssd_d, v_ssd_norm_w, v_proj_a, v_proj_b, v_proj_c, v_w_out):
    args = dict(locals())
    w = {k: args[k] for k in WEIGHTS}
    m = {k: args['m_' + k] for k in WEIGHTS}
    v = {k: args['v_' + k] for k in WEIGHTS}
    depth = norm_w.shape[0]
    S = x.shape[1]
    xs = x.reshape(S, D_MODEL)
    tgt = loss_target.reshape(S, D_MODEL)

    sh_names = list(SHARDED)
    shard_shapes = [w[k].shape for k in sh_names]
    packed = _pack([w[k] for k in sh_names], bf16, mult=16)
    gathered = exchange("gather_weights", packed, all_to_all=False)
    per_dev = [_unpack(gathered[d], shard_shapes) for d in range(N_DEV)]
    full = {}
    for j, k in enumerate(sh_names):
        stacked = jnp.stack([per_dev[d][j] for d in range(N_DEV)], axis=0)
        full[k] = _front_to_shard_axis(stacked, SHARDED[k])

    layers = []
    for li in range(depth):
        W = {k: (full[k][li] if k in SHARDED else w[k][li]) for k in WEIGHTS}
        W['w_in_pieces'] = _relayout_w_in(W['w_in'])
        W['w_in_all'] = jnp.concatenate(W['w_in_pieces'], axis=1)
        layers.append(W)

    act, saved = xs, []
    for li in range(depth):
        act, sv = layer_fwd(li, act, layers[li])
        saved.append(sv)
    dy, loss_local = loss_and_grad(act, tgt)
    loss = lax.psum(loss_local, ("x", "y", "c"))

    grads = [None] * depth
    for li in reversed(range(depth)):
        dy, gr = layer_bwd(li, dy, saved[li], layers[li])
        gr['w_in'] = _unrelayout_w_in_grad(gr.pop('w_in_relaid'))
        grads[li] = gr
    grad_x = dy.reshape(x.shape)
    gfull = {k: jnp.stack([grads[li][k] for li in range(depth)], axis=0) for k in WEIGHTS}

    big = jnp.stack([_pack([_shard_axis_to_front(gfull[k], SHARDED[k])[d] for k in sh_names], f32, mult=256) for d in range(N_DEV)], axis=0)
    big_sum = sum_slots("sum_sharded", exchange("scatter_grads", big, all_to_all=True))
    rep_names = [k for k in WEIGHTS if k not in SHARDED]
    small = _pack([gfull[k] for k in rep_names], f32)
    small_sum = sum_slots("sum_replicated", exchange("gather_small_grads", small, all_to_all=False))
    g_final = dict(zip(sh_names, _unpack(big_sum, shard_shapes)))
    g_final.update(zip(rep_names, _unpack(small_sum, [w[k].shape for k in rep_names])))

    deltas, new_m, new_v = {}, {}, {}
    for k in WEIGHTS:
        deltas[k], new_m[k], new_v[k] = adamw("adamw_" + k, w[k], g_final[k], m[k], v[k])
    return (loss, grad_x, *[g_final[k] for k in WEIGHTS], *[deltas[k] for k in WEIGHTS],
            *[new_m[k] for k in WEIGHTS], *[new_v[k] for k in WEIGHTS])
```

```python
import functools
import math
from typing import Any, NamedTuple

import jax
import jax.numpy as jnp
from jax import lax
from jax.experimental import pallas as pl
from jax.experimental.pallas import tpu as pltpu

f32 = jnp.float32
bf16 = jnp.bfloat16

N_DEV = 8
D_MODEL = 1024
RMS_EPS = 1e-6
S5_WIDTH = 512
S5_GROUPS = 32
S5_GROUP = 16
S5_STATE = 64
S5_TILE = 256
S5_ND = 8
S5_CHUNKS = 4
ATT_HEAD_DIM = 64
ATT_PAIRS = ((128, 1), (512, 4), (2048, 16))
ATT_HPG = 4
ATT_BLOCK = 128
ATT_GW = ATT_HPG * ATT_HEAD_DIM
ATT_WIDTH = 768
SSD_HEADS = 12
SSD_HEAD_DIM = 64
SSD_WIDTH = 768
SSD_STATE = 128
SSD_GROUPS = 2
SSD_CHUNK = 128
SSD_CONV = 4
SSD_CONV_DIM = 1280
HPAD = 128
IN_SPLITS = (512, 512, 768, 768, 768, 256, 1280, 12, 768, 3072)
ADAM_LR, ADAM_B1, ADAM_B2, ADAM_EPS, ADAM_WD, ADAM_STEP = 0.001, 0.9, 0.999, 1e-08, 0.01, 10
VMEM_LIMIT = 56 * 1024 * 1024

WEIGHTS = ['norm_w', 'w_in', 's5_a_re', 's5_a_im', 's5_log_step', 's5_b_re', 's5_b_im', 's5_c_re',
           's5_c_im', 's5_d', 's5_glu_w', 's5_glu_b', 'q_norm_w', 'k_norm_w', 'conv_w', 'conv_b',
           'dt_bias', 'ssd_a_log', 'ssd_d', 'ssd_norm_w', 'proj_a', 'proj_b', 'proj_c', 'w_out']
ROW_SHARDED = ('w_in', 's5_glu_w', 'w_out')
SHARDED = ROW_SHARDED + ('conv_w', 'proj_a', 'proj_b', 'proj_c')


class Arg(NamedTuple):
    arr: Any
    block: tuple
    imap: Any
    kind: str = 'const'
    gshape: Any = None
    gimap: Any = None


class Out(NamedTuple):
    shape: tuple
    dtype: Any
    block: tuple
    imap: Any


def _cparams(n):
    return pltpu.CompilerParams(dimension_semantics=("arbitrary",) * n, vmem_limit_bytes=VMEM_LIMIT)


def _rows(a, tm, kind='tile', col=0, width=None, gshape=None, gcol=None):
    width = a.shape[1] if width is None else width
    g = None if gshape is None else (lambda i, gc=(0 if gcol is None else gcol): (i, gc))
    return Arg(a, (tm, width), lambda i, c=col: (i, c), kind, gshape, g)


def _whole(a, kind='const'):
    nd = a.ndim
    return Arg(a, a.shape, lambda *i, nd=nd: (0,) * nd, kind)


def map_fwd(name, fn, grid, args, outs):
    n_in = len(args)

    def body(*refs):
        pid = tuple(pl.program_id(a) for a in range(len(grid)))
        res = fn(pid, *[r[...] for r in refs[:n_in]])
        for o, r in zip(refs[n_in:], res):
            o[...] = r.astype(o.dtype)

    res = pl.pallas_call(
        body, name=name, grid=grid,
        in_specs=[pl.BlockSpec(a.block, a.imap) for a in args],
        out_specs=[pl.BlockSpec(o.block, o.imap) for o in outs],
        out_shape=[jax.ShapeDtypeStruct(o.shape, o.dtype) for o in outs],
        compiler_params=_cparams(len(grid)))(*[a.arr for a in args])
    return tuple(res)


def _grad_outs(args, wrt):
    outs = []
    for i in wrt:
        a = args[i]
        shape = a.arr.shape if a.gshape is None else a.gshape
        imap = a.imap if a.gimap is None else a.gimap
        outs.append(Out(shape, f32, a.block, imap))
    return outs


def _store_grads(pid, args, wrt, grads, grefs, adds):
    first_all = functools.reduce(jnp.logical_and, [p == 0 for p in pid])
    first_in = functools.reduce(jnp.logical_and, [p == 0 for p in pid[1:]]) if len(pid) > 1 else first_all
    for j, i in enumerate(wrt):
        g = grads[j].astype(f32)
        ref = grefs[j]
        kind = args[i].kind
        if kind == 'tile':
            if j in adds:
                g = g + adds[j]
            ref[...] = g
        else:
            first = first_all if kind == 'acc' else first_in

            @pl.when(first)
            def _(ref=ref):
                ref[...] = jnp.zeros_like(ref)

            ref[...] += g


def map_bwd(name, fn, grid, args, douts, wrt, add=None):
    add = add or {}
    n_in, n_d, n_add = len(args), len(douts), len(add)
    add_keys = sorted(add)
    gouts = _grad_outs(args, wrt)

    def body(*refs):
        pid = tuple(pl.program_id(a) for a in range(len(grid)))
        vals = [r[...] for r in refs[:n_in]]
        dvals = [r[...].astype(f32) for r in refs[n_in:n_in + n_d]]
        avals = {k: refs[n_in + n_d + j][...].astype(f32) for j, k in enumerate(add_keys)}
        grefs = refs[n_in + n_d + n_add:]

        def f(*w):
            full = list(vals)
            for i, x in zip(wrt, w):
                full[i] = x
            return tuple(fn(pid, *full))

        _, vjp = jax.vjp(f, *[vals[i] for i in wrt])
        grads = vjp(tuple(dvals))
        _store_grads(pid, args, wrt, grads, grefs, avals)

    ins = list(args) + list(douts) + [add[k] for k in add_keys]
    res = pl.pallas_call(
        body, name=name, grid=grid,
        in_specs=[pl.BlockSpec(a.block, a.imap) for a in ins],
        out_specs=[pl.BlockSpec(o.block, o.imap) for o in gouts],
        out_shape=[jax.ShapeDtypeStruct(o.shape, o.dtype) for o in gouts],
        compiler_params=_cparams(len(grid)))(*[a.arr for a in ins])
    return tuple(res)


def scan_fwd(name, fn, grid, carry_shapes, args, outs):
    no, nt = grid
    n_in, n_out, n_c = len(args), len(outs), len(carry_shapes)
    cks = [Out((no, nt) + cs, f32, (None, None) + cs, lambda o, t, n=len(cs): (o, t) + (0,) * n) for cs in carry_shapes]

    def body(*refs):
        pid = (pl.program_id(0), pl.program_id(1))
        ins = refs[:n_in]
        orefs = refs[n_in:n_in + n_out]
        ckrefs = refs[n_in + n_out:n_in + n_out + n_c]
        crefs = refs[n_in + n_out + n_c:]

        @pl.when(pid[1] == 0)
        def _():
            for c in crefs:
                c[...] = jnp.zeros_like(c)

        carry = tuple(c[...] for c in crefs)
        for ck, c in zip(ckrefs, carry):
            ck[...] = c
        res, newc = fn(pid, carry, *[r[...] for r in ins])
        for o, r in zip(orefs, res):
            o[...] = r.astype(o.dtype)
        for c, v in zip(crefs, newc):
            c[...] = v

    allouts = list(outs) + cks
    res = pl.pallas_call(
        body, name=name, grid=grid,
        in_specs=[pl.BlockSpec(a.block, a.imap) for a in args],
        out_specs=[pl.BlockSpec(o.block, o.imap) for o in allouts],
        out_shape=[jax.ShapeDtypeStruct(o.shape, o.dtype) for o in allouts],
        scratch_shapes=[pltpu.VMEM(cs, f32) for cs in carry_shapes],
        compiler_params=_cparams(2))(*[a.arr for a in args])
    return tuple(res[:n_out]), tuple(res[n_out:])


def scan_bwd(name, fn, grid, carry_shapes, args, ckpts, douts, wrt):
    no, nt = grid
    n_in, n_d, n_c = len(args), len(douts), len(carry_shapes)

    def rev(imap):
        return lambda o, t: imap(o, nt - 1 - t)

    rargs = [a._replace(imap=rev(a.imap), gimap=None if a.gimap is None else rev(a.gimap)) for a in args]
    rdouts = [a._replace(imap=rev(a.imap)) for a in douts]
    ckargs = [Arg(ck, (None, None) + cs, rev(lambda o, t, n=len(cs): (o, t) + (0,) * n)) for ck, cs in zip(ckpts, carry_shapes)]
    gouts = _grad_outs(rargs, wrt)

    def body(*refs):
        o, t = pl.program_id(0), pl.program_id(1)
        tt = nt - 1 - t
        vals = [r[...] for r in refs[:n_in]]
        dvals = [r[...].astype(f32) for r in refs[n_in:n_in + n_d]]
        carry = tuple(r[...] for r in refs[n_in + n_d:n_in + n_d + n_c])
        grefs = refs[n_in + n_d + n_c:n_in + n_d + n_c + len(wrt)]
        dcrefs = refs[n_in + n_d + n_c + len(wrt):]

        @pl.when(t == 0)
        def _():
            for c in dcrefs:
                c[...] = jnp.zeros_like(c)

        def f(carry, *w):
            full = list(vals)
            for i, x in zip(wrt, w):
                full[i] = x
            res, newc = fn((o, tt), carry, *full)
            return tuple(res), tuple(newc)

        _, vjp = jax.vjp(f, carry, *[vals[i] for i in wrt])
        grads = vjp((tuple(dvals), tuple(c[...] for c in dcrefs)))
        for c, g in zip(dcrefs, grads[0]):
            c[...] = g
        _store_grads((o, t), rargs, wrt, grads[1:], grefs, {})

    ins = rargs + rdouts + ckargs
    res = pl.pallas_call(
        body, name=name, grid=grid,
        in_specs=[pl.BlockSpec(a.block, a.imap) for a in ins],
        out_specs=[pl.BlockSpec(g.block, g.imap) for g in gouts],
        out_shape=[jax.ShapeDtypeStruct(g.shape, g.dtype) for g in gouts],
        scratch_shapes=[pltpu.VMEM(cs, f32) for cs in carry_shapes],
        compiler_params=_cparams(2))(*[a.arr for a in ins])
    return tuple(res)


def _pick(dim, target):
    if dim <= target:
        return dim
    for t in range(target, 127, -128):
        if dim % t == 0:
            return t
    return dim


def matmul(name, a, b, mode='nn', add=None, out_dtype=f32, tm=None, tn=1152, tk=None):
    if mode == 'tn':
        K, M = a.shape
    else:
        M, K = a.shape
    N = b.shape[0] if mode == 'nt' else b.shape[1]
    assert (b.shape[1] if mode == 'nt' else b.shape[0]) == K
    tm = (1024 if mode == 'tn' else 512) if tm is None else tm
    tk = (512 if mode == 'tn' else 1152) if tk is None else tk
    tm, tn, tk = _pick(M, tm), _pick(N, tn), _pick(K, tk)
    nk = K // tk
    a_spec = pl.BlockSpec((tk, tm), lambda i, j, k: (k, i)) if mode == 'tn' else pl.BlockSpec((tm, tk), lambda i, j, k: (i, k))
    b_spec = pl.BlockSpec((tn, tk), lambda i, j, k: (j, k)) if mode == 'nt' else pl.BlockSpec((tk, tn), lambda i, j, k: (k, j))
    dims = {'nn': (((1,), (0,)), ((), ())), 'nt': (((1,), (1,)), ((), ())), 'tn': (((0,), (0,)), ((), ()))}[mode]
    has_add = add is not None

    def body(*refs):
        if has_add:
            a_ref, b_ref, add_ref, o_ref, acc = refs
        else:
            a_ref, b_ref, o_ref, acc = refs
        k = pl.program_id(2)

        @pl.when(k == 0)
        def _():
            acc[...] = add_ref[...].astype(f32) if has_add else jnp.zeros_like(acc)

        acc[...] += lax.dot_general(a_ref[...].astype(bf16), b_ref[...].astype(bf16), dims, preferred_element_type=f32)

        @pl.when(k == nk - 1)
        def _():
            o_ref[...] = acc[...].astype(o_ref.dtype)

    in_specs = [a_spec, b_spec] + ([pl.BlockSpec((tm, tn), lambda i, j, k: (i, j))] if has_add else [])
    ops = [a, b] + ([add] if has_add else [])
    return pl.pallas_call(
        body, name=name, grid=(M // tm, N // tn, nk), in_specs=in_specs,
        out_specs=pl.BlockSpec((tm, tn), lambda i, j, k: (i, j)),
        out_shape=jax.ShapeDtypeStruct((M, N), out_dtype),
        scratch_shapes=[pltpu.VMEM((tm, tn), f32)],
        compiler_params=pltpu.CompilerParams(dimension_semantics=("parallel", "parallel", "arbitrary"), vmem_limit_bytes=VMEM_LIMIT))(*ops)


def _dot(a, b, dims=(((1,), (0,)), ((), ()))):
    return lax.dot_general(a.astype(bf16), b.astype(bf16), dims, preferred_element_type=f32)


_NT = (((1,), (1,)), ((), ()))
_TN = (((0,), (0,)), ((), ()))


def _rmsnorm_tile(pid, x, w):
    return (x * lax.rsqrt(jnp.mean(x * x, axis=-1, keepdims=True) + RMS_EPS) * w,)


def _shift_rows(h, d, fill):
    pad = jnp.full((d, h.shape[1]), fill, f32)
    return jnp.concatenate([pad, h[:-d]], axis=0)


def _s5_prep_tile(pid, a_re, a_im, ls, btr, bti, ctr, cti):
    o = pid[0]
    w = a_re.shape[1]
    r = lax.broadcasted_iota(jnp.int32, (S5_GROUPS, w), 0)
    c = lax.broadcasted_iota(jnp.int32, (S5_GROUPS, w), 1)
    sel = (r == o * (w // S5_STATE) + c // S5_STATE).astype(f32)
    step = jnp.dot(jnp.exp(ls), sel, precision=lax.Precision.HIGHEST, preferred_element_type=f32)
    mag = jnp.exp(a_re * step)
    ang = a_im * step
    lr, li = mag * jnp.cos(ang), mag * jnp.sin(ang)
    nr, ni = lr - 1.0, li
    den = a_re * a_re + a_im * a_im
    fr = (nr * a_re + ni * a_im) / den
    fi = (ni * a_re - nr * a_im) / den
    bbr = fr * btr - fi * bti
    bbi = fr * bti + fi * btr
    reps = w // S5_STATE
    rr = lax.broadcasted_iota(jnp.int32, (reps * S5_GROUP, w), 0)
    cc = lax.broadcasted_iota(jnp.int32, (reps * S5_GROUP, w), 1)
    diag = (rr // S5_GROUP) == (cc // S5_STATE)

    def expand(m):
        return jnp.where(diag, jnp.concatenate([m] * reps, axis=0), 0.0)

    pr, pi = lr, li
    rows_r, rows_i = [pr], [pi]
    for _ in range(S5_ND - 1):
        pr, pi = pr * pr - pi * pi, 2.0 * pr * pi
        rows_r.append(pr)
        rows_i.append(pi)
    lamd_r, lamd_i = jnp.concatenate(rows_r, axis=0), jnp.concatenate(rows_i, axis=0)
    tr = jnp.broadcast_to(lr, (S5_TILE, w))
    ti = jnp.broadcast_to(li, (S5_TILE, w))
    for j in range(S5_ND):
        sr, si = _shift_rows(tr, 1 << j, 1.0), _shift_rows(ti, 1 << j, 0.0)
        tr, ti = tr * sr - ti * si, tr * si + ti * sr
    return lamd_r, lamd_i, tr, ti, expand(bbr), expand(bbi), expand(ctr), expand(cti)


def _s5_tile(pid, carry, u, lamd_r, lamd_i, lamt_r, lamt_i, bbr, bbi, ccr, cci, dvec):
    cr, ci = carry
    hr = _dot(u, bbr)
    hi = _dot(u, bbi)
    for j in range(S5_ND):
        sr, si = _shift_rows(hr, 1 << j, 0.0), _shift_rows(hi, 1 << j, 0.0)
        ar, ai = lamd_r[j:j + 1], lamd_i[j:j + 1]
        hr, hi = hr + ar * sr - ai * si, hi + ar * si + ai * sr
    hr, hi = hr + lamt_r * cr - lamt_i * ci, hi + lamt_r * ci + lamt_i * cr
    y = _dot(hr, ccr, _NT) - _dot(hi, cci, _NT) + dvec * u
    return (jax.nn.gelu(y),), (hr[-1:], hi[-1:])


def _glu_tile(pid, g, glu, za, b):
    return (g * jax.nn.sigmoid(glu + b) * jax.nn.silu(za),)


def _attn_tile(pid, carry, q, k, v, qw, kw):
    n = pid[1]
    kp, vp = carry
    D = ATT_HEAD_DIM

    def heads(x):
        return [x[:, h * D:(h + 1) * D] for h in range(ATT_HPG)]

    def hnorm(x, w):
        return [xh * lax.rsqrt(jnp.mean(xh * xh, axis=-1, keepdims=True) + RMS_EPS) * w for xh in heads(x)]

    qh, kh = hnorm(q, qw), hnorm(k, kw)
    kph, vph, vh = heads(kp), heads(vp), heads(v)
    r = lax.broadcasted_iota(jnp.int32, (ATT_BLOCK, 2 * ATT_BLOCK), 0)
    c = lax.broadcasted_iota(jnp.int32, (ATT_BLOCK, 2 * ATT_BLOCK), 1)
    diff = r + ATT_BLOCK - c
    mask = (diff >= 0) & (diff <= ATT_BLOCK) & ((c >= ATT_BLOCK) | (n > 0))
    outs, lses = [], []
    for h in range(ATT_HPG):
        kk = jnp.concatenate([kph[h], kh[h]], axis=0)
        vv = jnp.concatenate([vph[h], vh[h]], axis=0)
        s = _dot(qh[h], kk, _NT) * (D ** -0.5)
        s = jnp.where(mask, s, -1e30)
        m = jnp.max(s, axis=-1, keepdims=True)
        p = jnp.exp(s - m)
        l = jnp.sum(p, axis=-1, keepdims=True)
        outs.append(_dot(p / l, vv))
        lses.append(jnp.broadcast_to(m + jnp.log(l), (ATT_BLOCK, D)))
    return (jnp.concatenate(outs, axis=1), jnp.concatenate(lses, axis=1)), (jnp.concatenate(kh, axis=1), v)


def _combine_tile(pid, o1, l1, o2, l2, o3, l3, zb):
    m = jnp.maximum(jnp.maximum(l1, l2), l3)
    e1, e2, e3 = jnp.exp(l1 - m), jnp.exp(l2 - m), jnp.exp(l3 - m)
    y = (e1 * o1 + e2 * o2 + e3 * o3) / (e1 + e2 + e3)
    return (y * jax.nn.silu(zb),)


def _softplus(x):
    return jnp.maximum(x, 0.0) + jnp.log(1.0 + jnp.exp(-jnp.abs(x)))


def _ssd_tile(pid, carry, xbc, dt, z, conv_w, conv_b, dt_bias, a_log, dvec, norm_w):
    xprev, state = carry
    T, P, N = SSD_CHUNK, SSD_HEAD_DIM, SSD_STATE
    xx = jnp.concatenate([xprev, xbc], axis=0)
    conv = conv_b
    for k in range(SSD_CONV):
        off = 8 - (SSD_CONV - 1) + k
        conv = conv + conv_w[k:k + 1] * xx[off:off + T]
    xc = jax.nn.silu(conv)
    dtp = _softplus(dt + dt_bias)
    a_dt = dtp * (-jnp.exp(a_log))
    r = lax.broadcasted_iota(jnp.int32, (T, T), 0)
    c = lax.broadcasted_iota(jnp.int32, (T, T), 1)
    tri = r >= c
    trif = tri.astype(f32)
    hi = lax.Precision.HIGHEST
    a_cs = jnp.dot(trif, a_dt, precision=hi, preferred_element_type=f32)
    a_cs_t = lax.dot_general(a_dt, trif, (((0,), (1,)), ((), ())), precision=hi, preferred_element_type=f32)
    ys, states = [], []
    for g in range(SSD_GROUPS):
        bg = xc[:, SSD_WIDTH + g * N:SSD_WIDTH + (g + 1) * N]
        cg = xc[:, SSD_WIDTH + SSD_GROUPS * N + g * N:SSD_WIDTH + SSD_GROUPS * N + (g + 1) * N]
        cb = _dot(cg, bg, _NT)
        for e in range(SSD_HEADS // SSD_GROUPS):
            h = g * (SSD_HEADS // SSD_GROUPS) + e
            col, row = a_cs[:, h:h + 1], a_cs_t[h:h + 1, :]
            decay = jnp.exp(jnp.where(tri, col - row, -1e30))
            xh = xc[:, h * P:(h + 1) * P]
            xdt = xh * dtp[:, h:h + 1]
            st = state[h * P:(h + 1) * P, :]
            last = a_cs[T - 1:T, h:h + 1]
            y = _dot(cb * decay, xdt) + _dot(cg, st, _NT) * jnp.exp(col) + xh * dvec[:, h:h + 1]
            ys.append(y)
            states.append(jnp.exp(last) * st + _dot(xdt * jnp.exp(last - col), bg, _TN))
    y = jnp.concatenate(ys, axis=1) * jax.nn.silu(z)
    out = y * lax.rsqrt(jnp.mean(y * y, axis=-1, keepdims=True) + RMS_EPS) * norm_w
    return (out,), (xbc[T - 8:], jnp.concatenate(states, axis=0))


def _merge_tile(pid, pa, pb, pc, g0, g1, g2):
    return (jax.nn.sigmoid(g0) * pa + jax.nn.sigmoid(g1) * pb + jax.nn.sigmoid(g2) * pc,)


def loss_and_grad(y, target, tm=512):
    S, D = y.shape
    nt = S // tm

    def body(y_ref, t_ref, dy_ref, l_ref, acc):
        i = pl.program_id(0)

        @pl.when(i == 0)
        def _():
            acc[...] = jnp.zeros_like(acc)

        diff = y_ref[...] - t_ref[...]
        dy_ref[...] = diff * (1.0 / D)
        acc[...] += jnp.sum((diff * diff).reshape(tm // 8, 8, D), axis=0)

        @pl.when(i == nt - 1)
        def _():
            l_ref[...] = jnp.broadcast_to(0.5 / D * jnp.sum(acc[...]), l_ref.shape)

    dy, l = pl.pallas_call(
        body, name="loss_head", grid=(nt,),
        in_specs=[pl.BlockSpec((tm, D), lambda i: (i, 0))] * 2,
        out_specs=[pl.BlockSpec((tm, D), lambda i: (i, 0)), pl.BlockSpec((8, 128), lambda i: (0, 0))],
        out_shape=[jax.ShapeDtypeStruct((S, D), f32), jax.ShapeDtypeStruct((8, 128), f32)],
        scratch_shapes=[pltpu.VMEM((8, D), f32)],
        compiler_params=_cparams(1))(y, target)
    return dy, l[0, 0]


def _row_tile(R, C, budget=1 << 20):
    best = R
    for t in range(8, R, 8):
        if R % t == 0 and t * C * 4 <= budget:
            best = t
    if best == R and R * C * 4 > budget:
        for t in range(8, R, 8):
            if R % t == 0:
                return t
    return best


def _as2d(t, lead=0):
    return t.reshape(t.shape[:lead] + (math.prod(t.shape[lead:-1]), t.shape[-1]))


def adamw(name, w, gslots, m, v):
    shape = w.shape
    n = gslots.shape[0]
    C = shape[-1]
    R = math.prod(shape[:-1])
    lanes = -(-C // 128) * 128
    tr = _row_tile(R, lanes * (n + 7), budget=10 << 20)

    def body(w_ref, g_ref, m_ref, v_ref, go_ref, d_ref, nm_ref, nv_ref):
        gg = g_ref[0]
        for s in range(1, n):
            gg = gg + g_ref[s]
        go_ref[...] = gg
        nm = ADAM_B1 * m_ref[...] + (1.0 - ADAM_B1) * gg
        nv = ADAM_B2 * v_ref[...] + (1.0 - ADAM_B2) * jnp.square(gg)
        m_hat = nm / (1.0 - ADAM_B1 ** ADAM_STEP)
        v_hat = nv / (1.0 - ADAM_B2 ** ADAM_STEP)
        d_ref[...] = -ADAM_LR * (m_hat / (jnp.sqrt(v_hat) + ADAM_EPS) + ADAM_WD * w_ref[...])
        nm_ref[...] = nm
        nv_ref[...] = nv

    spec = pl.BlockSpec((tr, C), lambda i: (i, 0))
    res = pl.pallas_call(
        body, name=name, grid=(R // tr,),
        in_specs=[spec, pl.BlockSpec((n, tr, C), lambda i: (0, i, 0)), spec, spec], out_specs=[spec] * 4,
        out_shape=[jax.ShapeDtypeStruct((R, C), f32)] * 4,
        compiler_params=_cparams(1))(w.reshape(R, C), gslots.reshape(n, R, C), m.reshape(R, C), v.reshape(R, C))
    return tuple(t.reshape(shape) for t in res)


def sum_slots(name, x):
    n, R, C = x.shape

    def body(x_ref, o_ref):
        acc = x_ref[0]
        for s in range(1, n):
            acc = acc + x_ref[s]
        o_ref[...] = acc

    return pl.pallas_call(body, name=name, out_shape=jax.ShapeDtypeStruct((R, C), f32))(x)


def exchange(name, srcs, modes):
    nt = len(srcs)
    slabs = []
    for s, mode in zip(srcs, modes):
        R, C = s.shape
        slabs.append({'all': (R, C), 'rows': (R // N_DEV, C), 'cols': (R, C // N_DEV)}[mode])

    def piece(ref, mode, slab, p):
        if mode == 'all':
            return ref
        if mode == 'rows':
            return ref.at[pl.ds(p * slab[0], slab[0]), :]
        return ref.at[:, pl.ds(p * slab[1], slab[1])]

    def body(*refs):
        src_refs, out_refs = refs[:nt], refs[nt:2 * nt]
        send_sems, recv_sems, local_sems = refs[2 * nt:]
        x, y, c = lax.axis_index("x"), lax.axis_index("y"), lax.axis_index("c")
        me = 4 * x + 2 * y + c
        copies = []
        for k in (1, 2, 4, 3, 5, 6, 7):
            px = 1 - x if k & 4 else x
            py = 1 - y if k & 2 else y
            pc = 1 - c if k & 1 else c
            for t in range(nt):
                cp = pltpu.make_async_remote_copy(
                    src_ref=piece(src_refs[t], modes[t], slabs[t], 4 * px + 2 * py + pc), dst_ref=out_refs[t].at[me],
                    send_sem=send_sems.at[t, k - 1], recv_sem=recv_sems.at[t, k - 1],
                    device_id=(px, py, pc), device_id_type=pl.DeviceIdType.MESH)
                cp.start()
                copies.append(cp)
        for t in range(nt):
            own = pltpu.make_async_copy(piece(src_refs[t], modes[t], slabs[t], me), out_refs[t].at[me], local_sems.at[t])
            own.start()
            copies.append(own)
        for cp in copies:
            cp.wait()

    res = pl.pallas_call(
        body, name=name,
        in_specs=[pl.BlockSpec(memory_space=pl.ANY)] * nt,
        out_specs=[pl.BlockSpec(memory_space=pl.ANY)] * nt,
        out_shape=[jax.ShapeDtypeStruct((N_DEV,) + sl, s.dtype) for s, sl in zip(srcs, slabs)],
        scratch_shapes=[pltpu.SemaphoreType.DMA((nt, N_DEV - 1)), pltpu.SemaphoreType.DMA((nt, N_DEV - 1)), pltpu.SemaphoreType.DMA((nt,))],
    )(*srcs)
    return list(res)


def _relayout_w_in(w):
    offs = [0]
    for s in IN_SPLITS:
        offs.append(offs[-1] + s)
    p = [w[:, offs[i]:offs[i + 1]] for i in range(len(IN_SPLITS))]
    ua, za, q, k, v, zb, xbc, dt, zc, gates = p
    dtp = jnp.pad(dt, ((0, 0), (0, HPAD - dt.shape[1])))
    return (jnp.concatenate([ua, za, dtp], 1), w[:, offs[2]:offs[5]], jnp.concatenate([xbc, zb, zc], 1), gates)


def _pad_lanes(v, n=HPAD):
    return jnp.pad(v.reshape(1, -1), ((0, 0), (0, n - v.shape[-1])))


def _s5_prep_args(W):
    g2 = S5_GROUPS * S5_STATE
    w = g2 // S5_CHUNKS
    a_re, a_im = W['s5_a_re'].reshape(1, g2), W['s5_a_im'].reshape(1, g2)
    ls = W['s5_log_step'].reshape(1, S5_GROUPS)
    btr, bti = W['s5_b_re'].reshape(g2, S5_GROUP).T, W['s5_b_im'].reshape(g2, S5_GROUP).T
    ctr = W['s5_c_re'].transpose(1, 0, 2).reshape(S5_GROUP, g2)
    cti = W['s5_c_im'].transpose(1, 0, 2).reshape(S5_GROUP, g2)
    col = lambda a, rows: Arg(a, (rows, w), lambda o: (0, o), 'tile')
    return [col(a_re, 1), col(a_im, 1), _whole(ls, 'acc'), col(btr, S5_GROUP), col(bti, S5_GROUP), col(ctr, S5_GROUP), col(cti, S5_GROUP)]


def _s5_prep_outs():
    g2 = S5_GROUPS * S5_STATE
    w = g2 // S5_CHUNKS
    rows = (S5_ND, S5_ND, S5_TILE, S5_TILE, 128, 128, 128, 128)
    return [Out((r, g2), f32, (r, w), lambda o: (0, o)) for r in rows]


def _s5_args(A, prep, dvec, S):
    w = S5_GROUPS * S5_STATE // S5_CHUNKS
    args = [Arg(A, (S5_TILE, 128), lambda o, t: (t, o), 'tile', (S, S5_WIDTH), None)]
    for p in prep:
        args.append(Arg(p, (p.shape[0], w), lambda o, t: (0, o), 'acc0'))
    args.append(Arg(dvec, (1, 128), lambda o, t: (0, o), 'acc0'))
    return args


def _attn_args(QKV, g, r, qw, kw, S):
    L = S // r
    nblk = 3 * ATT_WIDTH // ATT_GW
    gshape = (L, r * ATT_GW)
    gimap = lambda rho, n: (n, rho)
    if r == 1:
        mk = lambda j: Arg(QKV, (ATT_BLOCK, ATT_GW), lambda rho, n, j=j: (n, j), 'tile', gshape, gimap)
    else:
        def mk(j):
            view = QKV[:, j * ATT_GW:(j + 1) * ATT_GW].reshape(L, r * ATT_GW)
            return Arg(view, (ATT_BLOCK, ATT_GW), gimap, 'tile')
    return [mk(g), mk(3 + g), mk(6 + g), _whole(qw, 'acc'), _whole(kw, 'acc')]


def _ssd_args(C, A, W, S):
    T = SSD_CHUNK
    return [Arg(C, (T, SSD_CONV_DIM), lambda o, t: (t, 0), 'tile', (S, SSD_CONV_DIM), None),
            Arg(A, (T, HPAD), lambda o, t: (t, 2 * S5_WIDTH // HPAD), 'tile', (S, HPAD), lambda o, t: (t, 0)),
            Arg(C, (T, SSD_WIDTH), lambda o, t: (t, 2), 'tile', (S, SSD_WIDTH), lambda o, t: (t, 0)),
            _whole(W['conv_w'], 'acc'), _whole(W['conv_b'].reshape(1, -1), 'acc'),
            _whole(_pad_lanes(W['dt_bias']), 'acc'), _whole(_pad_lanes(W['ssd_a_log']), 'acc'),
            _whole(_pad_lanes(W['ssd_d']), 'acc'), _whole(W['ssd_norm_w'].reshape(1, -1), 'acc')]


_SSD_CARRY = ((8, SSD_CONV_DIM), (SSD_WIDTH, SSD_STATE))
_ATT_CARRY = ((ATT_BLOCK, ATT_GW), (ATT_BLOCK, ATT_GW))
_S5_CARRY = ((1, 512), (1, 512))


def layer_fwd(li, x, W):
    S = x.shape[0]
    n = lambda s: f"l{li}_{s}"
    sv = {'x': x}
    (h,) = map_fwd(n("norm"), _rmsnorm_tile, (S // 512,), [_rows(x, 512), _whole(W['norm_w'].reshape(1, -1))],
                   [Out((S, D_MODEL), bf16, (512, D_MODEL), lambda i: (i, 0))])
    wA, wQ, wC, wG = W['w_in_pieces']
    A = matmul(n("in_a"), h, wA)
    QKV = matmul(n("in_qkv"), h, wQ)
    C = matmul(n("in_c"), h, wC)
    G = matmul(n("in_g"), h, wG)
    sv.update(h=h, A=A, QKV=QKV, C=C, G=G)

    prep = map_fwd(n("s5_prep"), _s5_prep_tile, (S5_CHUNKS,), _s5_prep_args(W), _s5_prep_outs())
    dvec = W['s5_d'].reshape(1, -1)
    (g,), s5_ck = scan_fwd(n("s5_scan"), _s5_tile, (S5_CHUNKS, S // S5_TILE), _S5_CARRY, _s5_args(A, prep, dvec, S),
                           [Out((S, S5_WIDTH), f32, (S5_TILE, 128), lambda o, t: (t, o))])
    glu = matmul(n("glu"), g, W['s5_glu_w'])
    glu_b = W['s5_glu_b'].reshape(1, -1)
    (ya,) = map_fwd(n("glu_gate"), _glu_tile, (S // 512,),
                    [_rows(g, 512), _rows(glu, 512), _rows(A, 512, col=1, width=S5_WIDTH), _whole(glu_b)],
                    [Out((S, S5_WIDTH), f32, (512, S5_WIDTH), lambda i: (i, 0))])
    sv.update(prep=prep, g=g, glu=glu, ya=ya, s5_ck=s5_ck)

    qw, kw = W['q_norm_w'].reshape(1, -1), W['k_norm_w'].reshape(1, -1)
    att, att_ck = [], []
    for gi, (window, r) in enumerate(ATT_PAIRS):
        assert window // r == ATT_BLOCK and S % (r * ATT_BLOCK) == 0
        L = S // r
        spec = Out((L, r * ATT_GW), f32, (ATT_BLOCK, ATT_GW), lambda rho, nb: (nb, rho))
        (o, lse), ck = scan_fwd(n(f"attn{gi}"), _attn_tile, (r, L // ATT_BLOCK), _ATT_CARRY, _attn_args(QKV, gi, r, qw, kw, S), [spec, spec])
        att += [o.reshape(S, ATT_GW), lse.reshape(S, ATT_GW)]
        att_ck.append(ck)
    (yb,) = map_fwd(n("combine"), _combine_tile, (S // 512,),
                    [_rows(t, 512) for t in att] + [_rows(C, 512, col=SSD_CONV_DIM // ATT_GW, width=ATT_GW)],
                    [Out((S, ATT_GW), f32, (512, ATT_GW), lambda i: (i, 0))])
    sv.update(att=att, att_ck=att_ck, yb=yb)

    (yc,), ssd_ck = scan_fwd(n("ssd"), _ssd_tile, (1, S // SSD_CHUNK), _SSD_CARRY, _ssd_args(C, A, W, S),
                             [Out((S, SSD_WIDTH), f32, (SSD_CHUNK, SSD_WIDTH), lambda o, t: (t, 0))])
    sv.update(yc=yc, ssd_ck=ssd_ck)

    pa = matmul(n("proj_a"), ya, W['proj_a'])
    pb = matmul(n("proj_b"), yb, W['proj_b'])
    pc = matmul(n("proj_c"), yc, W['proj_c'])
    (merged,) = map_fwd(n("merge"), _merge_tile, (S // 256,),
                        [_rows(pa, 256), _rows(pb, 256), _rows(pc, 256)] + [_rows(G, 256, col=j, width=D_MODEL) for j in range(3)],
                        [Out((S, D_MODEL), f32, (256, D_MODEL), lambda i: (i, 0))])
    out = matmul(n("w_out"), merged, W['w_out'], add=x)
    sv.update(pa=pa, pb=pb, pc=pc, merged=merged)
    return out, sv


def layer_bwd(li, dout, sv, W):
    S = dout.shape[0]
    n = lambda s: f"l{li}_{s}"
    gr = {}
    x, A, QKV, C, G = sv['x'], sv['A'], sv['QKV'], sv['C'], sv['G']

    dmerged = matmul(n("d_merged"), dout, W['w_out'], 'nt')
    gr['w_out'] = matmul(n("g_w_out"), sv['merged'], dout, 'tn')
    margs = [_rows(sv['pa'], 256), _rows(sv['pb'], 256), _rows(sv['pc'], 256)] + \
            [_rows(G, 256, col=j, width=D_MODEL, gshape=(S, D_MODEL)) for j in range(3)]
    dpa, dpb, dpc, dg0, dg1, dg2 = map_bwd(n("merge_bwd"), _merge_tile, (S // 256,), margs, [_rows(dmerged, 256)], list(range(6)))
    dya = matmul(n("d_ya"), dpa, W['proj_a'], 'nt')
    dyb = matmul(n("d_yb"), dpb, W['proj_b'], 'nt')
    dyc = matmul(n("d_yc"), dpc, W['proj_c'], 'nt')
    gr['proj_a'] = matmul(n("g_proj_a"), sv['ya'], dpa, 'tn')
    gr['proj_b'] = matmul(n("g_proj_b"), sv['yb'], dpb, 'tn')
    gr['proj_c'] = matmul(n("g_proj_c"), sv['yc'], dpc, 'tn')

    glu_b = W['s5_glu_b'].reshape(1, -1)
    gargs = [_rows(sv['g'], 512), _rows(sv['glu'], 512), _rows(A, 512, col=1, width=S5_WIDTH, gshape=(S, S5_WIDTH)), _whole(glu_b, 'acc')]
    dg_a, dglu, dza, dglu_b = map_bwd(n("glu_gate_bwd"), _glu_tile, (S // 512,), gargs, [_rows(dya, 512)], [0, 1, 2, 3])
    gr['s5_glu_b'] = dglu_b.reshape(-1)
    dg = matmul(n("d_g"), dglu, W['s5_glu_w'], 'nt', add=dg_a)
    gr['s5_glu_w'] = matmul(n("g_glu_w"), sv['g'], dglu, 'tn')
    dvec = W['s5_d'].reshape(1, -1)
    sargs = _s5_args(A, sv['prep'], dvec, S)
    res = scan_bwd(n("s5_scan_bwd"), _s5_tile, (S5_CHUNKS, S // S5_TILE), _S5_CARRY, sargs, sv['s5_ck'],
                   [Arg(dg, (S5_TILE, 128), lambda o, t: (t, o))], list(range(len(sargs))))
    dua, dprep, dd = res[0], res[1:9], res[9]
    gr['s5_d'] = dd.reshape(-1)
    pargs = _s5_prep_args(W)
    pouts = _s5_prep_outs()
    da_re, da_im, dls, dbtr, dbti, dctr, dcti = map_bwd(
        n("s5_prep_bwd"), _s5_prep_tile, (S5_CHUNKS,), pargs,
        [Arg(d, o.block, o.imap) for d, o in zip(dprep, pouts)], list(range(7)))
    gshape = (S5_GROUPS, S5_STATE)
    gr['s5_a_re'], gr['s5_a_im'] = da_re.reshape(gshape), da_im.reshape(gshape)
    gr['s5_log_step'] = dls.reshape(-1)
    gr['s5_b_re'] = dbtr.T.reshape(S5_GROUPS, S5_STATE, S5_GROUP)
    gr['s5_b_im'] = dbti.T.reshape(S5_GROUPS, S5_STATE, S5_GROUP)
    gr['s5_c_re'] = dctr.reshape(S5_GROUP, S5_GROUPS, S5_STATE).transpose(1, 0, 2)
    gr['s5_c_im'] = dcti.reshape(S5_GROUP, S5_GROUPS, S5_STATE).transpose(1, 0, 2)

    cargs = [_rows(t, 512) for t in sv['att']] + [_rows(C, 512, col=SSD_CONV_DIM // ATT_GW, width=ATT_GW, gshape=(S, ATT_GW))]
    cres = map_bwd(n("combine_bwd"), _combine_tile, (S // 512,), cargs, [_rows(dyb, 512)], list(range(7)))
    dzb = cres[6]
    qw, kw = W['q_norm_w'].reshape(1, -1), W['k_norm_w'].reshape(1, -1)
    dqs, dks, dvs = [], [], []
    dqw = dkw = None
    for gi, (window, r) in enumerate(ATT_PAIRS):
        L = S // r
        dspec = lambda t: Arg(t.reshape(L, r * ATT_GW), (ATT_BLOCK, ATT_GW), lambda rho, nb: (nb, rho))
        dq, dk, dv, dqw_g, dkw_g = scan_bwd(n(f"attn{gi}_bwd"), _attn_tile, (r, L // ATT_BLOCK), _ATT_CARRY,
                                            _attn_args(QKV, gi, r, qw, kw, S), sv['att_ck'][gi],
                                            [dspec(cres[2 * gi]), dspec(cres[2 * gi + 1])], [0, 1, 2, 3, 4])
        dqs.append(dq.reshape(S, ATT_GW))
        dks.append(dk.reshape(S, ATT_GW))
        dvs.append(dv.reshape(S, ATT_GW))
        dqw = dqw_g if dqw is None else dqw + dqw_g
        dkw = dkw_g if dkw is None else dkw + dkw_g
    gr['q_norm_w'], gr['k_norm_w'] = dqw.reshape(-1), dkw.reshape(-1)

    ssd_args = _ssd_args(C, A, W, S)
    sres = scan_bwd(n("ssd_bwd"), _ssd_tile, (1, S // SSD_CHUNK), _SSD_CARRY, ssd_args, sv['ssd_ck'],
                    [Arg(dyc, (SSD_CHUNK, SSD_WIDTH), lambda o, t: (t, 0))], list(range(9)))
    dxbc, ddt, dzc = sres[0], sres[1], sres[2]
    gr['conv_w'] = sres[3]
    gr['conv_b'] = sres[4].reshape(-1)
    gr['dt_bias'] = sres[5].reshape(-1)[:SSD_HEADS]
    gr['ssd_a_log'] = sres[6].reshape(-1)[:SSD_HEADS]
    gr['ssd_d'] = sres[7].reshape(-1)[:SSD_HEADS]
    gr['ssd_norm_w'] = sres[8].reshape(-1)

    dpieces = [jnp.concatenate([dua, dza, ddt], axis=1), jnp.concatenate(dqs + dks + dvs, axis=1),
               jnp.concatenate([dxbc, dzb, dzc], axis=1), jnp.concatenate([dg0, dg1, dg2], axis=1)]
    dh, gw = None, []
    for j, (dp, wp) in enumerate(zip(dpieces, W['w_in_pieces'])):
        dh = matmul(n(f"d_h{j}"), dp, wp, 'nt', add=dh)
        gw.append(matmul(n(f"g_w_in{j}"), sv['h'], dp, 'tn'))
    gr['w_in'] = _unrelayout_w_in_grad(gw)
    nargs = [_rows(x, 512), _whole(W['norm_w'].reshape(1, -1), 'acc')]
    dx, dnw = map_bwd(n("norm_bwd"), _rmsnorm_tile, (S // 512,), nargs, [_rows(dh, 512)], [0, 1], add={0: _rows(dout, 512)})
    gr['norm_w'] = dnw.reshape(-1)
    return dx, gr


def _unrelayout_w_in_grad(pieces):
    gA, gQ, gC, gG = pieces
    uaza, dt = gA[:, :2 * S5_WIDTH], gA[:, 2 * S5_WIDTH:2 * S5_WIDTH + SSD_HEADS]
    xbc, zb, zc = gC[:, :SSD_CONV_DIM], gC[:, SSD_CONV_DIM:SSD_CONV_DIM + ATT_GW], gC[:, SSD_CONV_DIM + ATT_GW:]
    return jnp.concatenate([uaza, gQ, zb, xbc, dt, zc, gG], axis=1)


def kernel(x, norm_w, w_in, s5_a_re, s5_a_im, s5_log_step, s5_b_re, s5_b_im, s5_c_re, s5_c_im, s5_d, s5_glu_w, s5_glu_b, q_norm_w, k_norm_w, conv_w, conv_b, dt_bias, ssd_a_log, ssd_d, ssd_norm_w, proj_a, proj_b, proj_c, w_out, loss_target, m_norm_w, m_w_in, m_s5_a_re, m_s5_a_im, m_s5_log_step, m_s5_b_re, m_s5_b_im, m_s5_c_re, m_s5_c_im, m_s5_d, m_s5_glu_w, m_s5_glu_b, m_q_norm_w, m_k_norm_w, m_conv_w, m_conv_b, m_dt_bias, m_ssd_a_log, m_ssd_d, m_ssd_norm_w, m_proj_a, m_proj_b, m_proj_c, m_w_out, v_norm_w, v_w_in, v_s5_a_re, v_s5_a_im, v_s5_log_step, v_s5_b_re, v_s5_b_im, v_s5_c_re, v_s5_c_im, v_s5_d, v_s5_glu_w, v_s5_glu_b, v_q_norm_w, v_k_norm_w, v_conv_w, v_conv_b, v_dt_bias, v_ssd_a_log, v_ssd_d, v_ssd_norm_w, v_proj_a, v_proj_b, v_proj_c, v_w_out):
    args = dict(locals())
    w = {k: args[k] for k in WEIGHTS}
    m = {k: args['m_' + k] for k in WEIGHTS}
    v = {k: args['v_' + k] for k in WEIGHTS}
    depth = norm_w.shape[0]
    S = x.shape[1]
    xs = x.reshape(S, D_MODEL)
    tgt = loss_target.reshape(S, D_MODEL)

    keys = [(li, k) for li in range(depth) for k in SHARDED]
    gathered = exchange("gather_weights", [w[k][li].astype(bf16) for li, k in keys], ['all'] * len(keys))
    layers = [{k: w[k][li] for k in WEIGHTS if k not in SHARDED} for li in range(depth)]
    for (li, k), t in zip(keys, gathered):
        n_dev, R, C = t.shape
        layers[li][k] = t.reshape(n_dev * R, C) if k in ROW_SHARDED else t.transpose(1, 0, 2).reshape(R, n_dev * C)
    for W in layers:
        W['w_in_pieces'] = _relayout_w_in(W['w_in'])

    act, saved = xs, []
    for li in range(depth):
        act, sv = layer_fwd(li, act, layers[li])
        saved.append(sv)
    dy, loss_local = loss_and_grad(act, tgt)
    loss = lax.psum(loss_local, ("x", "y", "c"))

    grads = [None] * depth
    for li in reversed(range(depth)):
        dy, grads[li] = layer_bwd(li, dy, saved[li], layers[li])
    grad_x = dy.reshape(x.shape)

    big_keys = [(li, k) for li in range(depth) for k in SHARDED if k != 'conv_w']
    slots = exchange("scatter_grads", [grads[li][k] for li, k in big_keys],
                     ['rows' if k in ROW_SHARDED else 'cols' for _, k in big_keys])
    per_layer = {}
    for (li, k), s in zip(big_keys, slots):
        per_layer[li, k] = adamw(f"adamw_{k}{li}", w[k][li], s, m[k][li], v[k][li])
    result = {k: tuple(jnp.stack([per_layer[li, k][j] for li in range(depth)], axis=0) for j in range(4))
              for k in SHARDED if k != 'conv_w'}

    small_keys = [k for k in WEIGHTS if k not in SHARDED] + ['conv_w']
    small = exchange("gather_small_grads", [_as2d(jnp.stack([grads[li][k] for li in range(depth)], axis=0)) for k in small_keys],
                     ['all'] * len(small_keys))
    for k, s in zip(small_keys, small):
        if k == 'conv_w':
            total = sum_slots("sum_conv_w", s).reshape((depth,) + grads[0][k].shape)
            width = w[k].shape[-1]
            me = 4 * lax.axis_index("x") + 2 * lax.axis_index("y") + lax.axis_index("c")
            s = lax.dynamic_slice_in_dim(total, me * width, width, axis=2)[None]
        result[k] = adamw("adamw_" + k, w[k], s, m[k], v[k])

    return (loss, grad_x, *[result[k][0] for k in WEIGHTS], *[result[k][1] for k in WEIGHTS],
            *[result[k][2] for k in WEIGHTS], *[result[k][3] for k in WEIGHTS])
```

```python
import functools
import math
from typing import Any, NamedTuple

import jax
import jax.numpy as jnp
from jax import lax
from jax.experimental import pallas as pl
from jax.experimental.pallas import tpu as pltpu

f32 = jnp.float32
bf16 = jnp.bfloat16

N_DEV = 8
D_MODEL = 1024
RMS_EPS = 1e-6
S5_WIDTH = 512
S5_GROUPS = 32
S5_GROUP = 16
S5_STATE = 64
S5_TILE = 256
S5_SUB = 8
S5_ND = 3
S5_CHUNKS = 4
ATT_HEAD_DIM = 64
ATT_PAIRS = ((128, 1), (512, 4), (2048, 16))
ATT_HPG = 4
ATT_BLOCK = 128
ATT_GW = ATT_HPG * ATT_HEAD_DIM
ATT_WIDTH = 768
SSD_HEADS = 12
SSD_HEAD_DIM = 64
SSD_WIDTH = 768
SSD_STATE = 128
SSD_GROUPS = 2
SSD_CHUNK = 128
SSD_CONV = 4
SSD_CONV_DIM = 1280
HPAD = 128
IN_SPLITS = (512, 512, 768, 768, 768, 256, 1280, 12, 768, 3072)
ADAM_LR, ADAM_B1, ADAM_B2, ADAM_EPS, ADAM_WD, ADAM_STEP = 0.001, 0.9, 0.999, 1e-08, 0.01, 10
VMEM_LIMIT = 56 * 1024 * 1024

WEIGHTS = ['norm_w', 'w_in', 's5_a_re', 's5_a_im', 's5_log_step', 's5_b_re', 's5_b_im', 's5_c_re',
           's5_c_im', 's5_d', 's5_glu_w', 's5_glu_b', 'q_norm_w', 'k_norm_w', 'conv_w', 'conv_b',
           'dt_bias', 'ssd_a_log', 'ssd_d', 'ssd_norm_w', 'proj_a', 'proj_b', 'proj_c', 'w_out']
ROW_SHARDED = ('w_in', 's5_glu_w', 'w_out')
SHARDED = ROW_SHARDED + ('conv_w', 'proj_a', 'proj_b', 'proj_c')


class Arg(NamedTuple):
    arr: Any
    block: tuple
    imap: Any
    kind: str = 'const'
    gshape: Any = None
    gimap: Any = None


class Out(NamedTuple):
    shape: tuple
    dtype: Any
    block: tuple
    imap: Any


def _cparams(n):
    return pltpu.CompilerParams(dimension_semantics=("arbitrary",) * n, vmem_limit_bytes=VMEM_LIMIT)


def _rows(a, tm, kind='tile', col=0, width=None, gshape=None, gcol=None):
    width = a.shape[1] if width is None else width
    g = None if gshape is None else (lambda i, gc=(0 if gcol is None else gcol): (i, gc))
    return Arg(a, (tm, width), lambda i, c=col: (i, c), kind, gshape, g)


def _whole(a, kind='const'):
    nd = a.ndim
    return Arg(a, a.shape, lambda *i, nd=nd: (0,) * nd, kind)


def map_fwd(name, fn, grid, args, outs):
    n_in = len(args)

    def body(*refs):
        pid = tuple(pl.program_id(a) for a in range(len(grid)))
        res = fn(pid, *[r[...] for r in refs[:n_in]])
        for o, r in zip(refs[n_in:], res):
            o[...] = r.astype(o.dtype)

    res = pl.pallas_call(
        body, name=name, grid=grid,
        in_specs=[pl.BlockSpec(a.block, a.imap) for a in args],
        out_specs=[pl.BlockSpec(o.block, o.imap) for o in outs],
        out_shape=[jax.ShapeDtypeStruct(o.shape, o.dtype) for o in outs],
        compiler_params=_cparams(len(grid)))(*[a.arr for a in args])
    return tuple(res)


def _grad_outs(args, wrt):
    outs = []
    for i in wrt:
        a = args[i]
        shape = a.arr.shape if a.gshape is None else a.gshape
        imap = a.imap if a.gimap is None else a.gimap
        outs.append(Out(shape, f32, a.block, imap))
    return outs


def _store_grads(pid, args, wrt, grads, grefs, adds):
    first_all = functools.reduce(jnp.logical_and, [p == 0 for p in pid])
    first_in = functools.reduce(jnp.logical_and, [p == 0 for p in pid[1:]]) if len(pid) > 1 else first_all
    for j, i in enumerate(wrt):
        g = grads[j].astype(f32)
        ref = grefs[j]
        kind = args[i].kind
        if kind == 'tile':
            if j in adds:
                g = g + adds[j]
            ref[...] = g
        else:
            first = first_all if kind == 'acc' else first_in

            @pl.when(first)
            def _(ref=ref):
                ref[...] = jnp.zeros_like(ref)

            ref[...] += g


def map_bwd(name, fn, grid, args, douts, wrt, add=None):
    add = add or {}
    n_in, n_d, n_add = len(args), len(douts), len(add)
    add_keys = sorted(add)
    gouts = _grad_outs(args, wrt)

    def body(*refs):
        pid = tuple(pl.program_id(a) for a in range(len(grid)))
        vals = [r[...] for r in refs[:n_in]]
        dvals = [r[...].astype(f32) for r in refs[n_in:n_in + n_d]]
        avals = {k: refs[n_in + n_d + j][...].astype(f32) for j, k in enumerate(add_keys)}
        grefs = refs[n_in + n_d + n_add:]

        def f(*w):
            full = list(vals)
            for i, x in zip(wrt, w):
                full[i] = x
            return tuple(fn(pid, *full))

        _, vjp = jax.vjp(f, *[vals[i] for i in wrt])
        grads = vjp(tuple(dvals))
        _store_grads(pid, args, wrt, grads, grefs, avals)

    ins = list(args) + list(douts) + [add[k] for k in add_keys]
    res = pl.pallas_call(
        body, name=name, grid=grid,
        in_specs=[pl.BlockSpec(a.block, a.imap) for a in ins],
        out_specs=[pl.BlockSpec(o.block, o.imap) for o in gouts],
        out_shape=[jax.ShapeDtypeStruct(o.shape, o.dtype) for o in gouts],
        compiler_params=_cparams(len(grid)))(*[a.arr for a in ins])
    return tuple(res)


def scan_fwd(name, fn, grid, carry_shapes, args, outs):
    no, nt = grid
    n_in, n_out, n_c = len(args), len(outs), len(carry_shapes)
    cks = [Out((no, nt) + cs, f32, (None, None) + cs, lambda o, t, n=len(cs): (o, t) + (0,) * n) for cs in carry_shapes]

    def body(*refs):
        pid = (pl.program_id(0), pl.program_id(1))
        ins = refs[:n_in]
        orefs = refs[n_in:n_in + n_out]
        ckrefs = refs[n_in + n_out:n_in + n_out + n_c]
        crefs = refs[n_in + n_out + n_c:]

        @pl.when(pid[1] == 0)
        def _():
            for c in crefs:
                c[...] = jnp.zeros_like(c)

        carry = tuple(c[...] for c in crefs)
        for ck, c in zip(ckrefs, carry):
            ck[...] = c
        res, newc = fn(pid, carry, *[r[...] for r in ins])
        for o, r in zip(orefs, res):
            o[...] = r.astype(o.dtype)
        for c, v in zip(crefs, newc):
            c[...] = v

    allouts = list(outs) + cks
    res = pl.pallas_call(
        body, name=name, grid=grid,
        in_specs=[pl.BlockSpec(a.block, a.imap) for a in args],
        out_specs=[pl.BlockSpec(o.block, o.imap) for o in allouts],
        out_shape=[jax.ShapeDtypeStruct(o.shape, o.dtype) for o in allouts],
        scratch_shapes=[pltpu.VMEM(cs, f32) for cs in carry_shapes],
        compiler_params=_cparams(2))(*[a.arr for a in args])
    return tuple(res[:n_out]), tuple(res[n_out:])


def scan_bwd(name, fn, grid, carry_shapes, args, ckpts, douts, wrt):
    no, nt = grid
    n_in, n_d, n_c = len(args), len(douts), len(carry_shapes)

    def rev(imap):
        return lambda o, t: imap(o, nt - 1 - t)

    rargs = [a._replace(imap=rev(a.imap), gimap=None if a.gimap is None else rev(a.gimap)) for a in args]
    rdouts = [a._replace(imap=rev(a.imap)) for a in douts]
    ckargs = [Arg(ck, (None, None) + cs, rev(lambda o, t, n=len(cs): (o, t) + (0,) * n)) for ck, cs in zip(ckpts, carry_shapes)]
    gouts = _grad_outs(rargs, wrt)

    def body(*refs):
        o, t = pl.program_id(0), pl.program_id(1)
        tt = nt - 1 - t
        vals = [r[...] for r in refs[:n_in]]
        dvals = [r[...].astype(f32) for r in refs[n_in:n_in + n_d]]
        carry = tuple(r[...] for r in refs[n_in + n_d:n_in + n_d + n_c])
        grefs = refs[n_in + n_d + n_c:n_in + n_d + n_c + len(wrt)]
        dcrefs = refs[n_in + n_d + n_c + len(wrt):]

        @pl.when(t == 0)
        def _():
            for c in dcrefs:
                c[...] = jnp.zeros_like(c)

        def f(carry, *w):
            full = list(vals)
            for i, x in zip(wrt, w):
                full[i] = x
            res, newc = fn((o, tt), carry, *full)
            return tuple(res), tuple(newc)

        _, vjp = jax.vjp(f, carry, *[vals[i] for i in wrt])
        grads = vjp((tuple(dvals), tuple(c[...] for c in dcrefs)))
        for c, g in zip(dcrefs, grads[0]):
            c[...] = g
        _store_grads((o, t), rargs, wrt, grads[1:], grefs, {})

    ins = rargs + rdouts + ckargs
    res = pl.pallas_call(
        body, name=name, grid=grid,
        in_specs=[pl.BlockSpec(a.block, a.imap) for a in ins],
        out_specs=[pl.BlockSpec(g.block, g.imap) for g in gouts],
        out_shape=[jax.ShapeDtypeStruct(g.shape, g.dtype) for g in gouts],
        scratch_shapes=[pltpu.VMEM(cs, f32) for cs in carry_shapes],
        compiler_params=_cparams(2))(*[a.arr for a in ins])
    return tuple(res)


def _pick(dim, target):
    if dim <= target:
        return dim
    for t in range(target, 127, -128):
        if dim % t == 0:
            return t
    return dim


def matmul(name, a, b, mode='nn', add=None, out_dtype=f32, tm=None, tn=1152, tk=None):
    if mode == 'tn':
        K, M = a.shape
    else:
        M, K = a.shape
    N = b.shape[0] if mode == 'nt' else b.shape[1]
    assert (b.shape[1] if mode == 'nt' else b.shape[0]) == K
    tm = (1024 if mode == 'tn' else 512) if tm is None else tm
    tk = (512 if mode == 'tn' else 1152) if tk is None else tk
    tm, tn, tk = _pick(M, tm), _pick(N, tn), _pick(K, tk)
    nk = K // tk
    a_spec = pl.BlockSpec((tk, tm), lambda i, j, k: (k, i)) if mode == 'tn' else pl.BlockSpec((tm, tk), lambda i, j, k: (i, k))
    b_spec = pl.BlockSpec((tn, tk), lambda i, j, k: (j, k)) if mode == 'nt' else pl.BlockSpec((tk, tn), lambda i, j, k: (k, j))
    dims = {'nn': (((1,), (0,)), ((), ())), 'nt': (((1,), (1,)), ((), ())), 'tn': (((0,), (0,)), ((), ()))}[mode]
    has_add = add is not None

    def body(*refs):
        if has_add:
            a_ref, b_ref, add_ref, o_ref, acc = refs
        else:
            a_ref, b_ref, o_ref, acc = refs
        k = pl.program_id(2)

        @pl.when(k == 0)
        def _():
            acc[...] = add_ref[...].astype(f32) if has_add else jnp.zeros_like(acc)

        acc[...] += lax.dot_general(a_ref[...].astype(bf16), b_ref[...].astype(bf16), dims, preferred_element_type=f32)

        @pl.when(k == nk - 1)
        def _():
            o_ref[...] = acc[...].astype(o_ref.dtype)

    in_specs = [a_spec, b_spec] + ([pl.BlockSpec((tm, tn), lambda i, j, k: (i, j))] if has_add else [])
    ops = [a, b] + ([add] if has_add else [])
    return pl.pallas_call(
        body, name=name, grid=(M // tm, N // tn, nk), in_specs=in_specs,
        out_specs=pl.BlockSpec((tm, tn), lambda i, j, k: (i, j)),
        out_shape=jax.ShapeDtypeStruct((M, N), out_dtype),
        scratch_shapes=[pltpu.VMEM((tm, tn), f32)],
        compiler_params=pltpu.CompilerParams(dimension_semantics=("parallel", "parallel", "arbitrary"), vmem_limit_bytes=VMEM_LIMIT))(*ops)


def wgrad(name, act, dout):
    return matmul(name, act, dout, 'tn', out_dtype=bf16)


def _dot(a, b, dims=(((1,), (0,)), ((), ()))):
    return lax.dot_general(a.astype(bf16), b.astype(bf16), dims, preferred_element_type=f32)


_NT = (((1,), (1,)), ((), ()))
_TN = (((0,), (0,)), ((), ()))


def _rmsnorm_tile(pid, x, w):
    return (x * lax.rsqrt(jnp.mean(x * x, axis=-1, keepdims=True) + RMS_EPS) * w,)


def _shift_rows(h, d, fill):
    pad = jnp.full((d, h.shape[1]), fill, f32)
    return jnp.concatenate([pad, h[:-d]], axis=0)


def _s5_prep_tile(pid, a_re, a_im, ls, btr, bti, ctr, cti):
    o = pid[0]
    w = a_re.shape[1]
    r = lax.broadcasted_iota(jnp.int32, (S5_GROUPS, w), 0)
    c = lax.broadcasted_iota(jnp.int32, (S5_GROUPS, w), 1)
    sel = (r == o * (w // S5_STATE) + c // S5_STATE).astype(f32)
    step = jnp.dot(jnp.exp(ls), sel, precision=lax.Precision.HIGHEST, preferred_element_type=f32)
    mag = jnp.exp(a_re * step)
    ang = a_im * step
    lr, li = mag * jnp.cos(ang), mag * jnp.sin(ang)
    nr, ni = lr - 1.0, li
    den = a_re * a_re + a_im * a_im
    fr = (nr * a_re + ni * a_im) / den
    fi = (ni * a_re - nr * a_im) / den
    bbr = fr * btr - fi * bti
    bbi = fr * bti + fi * btr
    reps = w // S5_STATE
    rr = lax.broadcasted_iota(jnp.int32, (reps * S5_GROUP, w), 0)
    cc = lax.broadcasted_iota(jnp.int32, (reps * S5_GROUP, w), 1)
    diag = (rr // S5_GROUP) == (cc // S5_STATE)

    def expand(m):
        return jnp.where(diag, jnp.concatenate([m] * reps, axis=0), 0.0)

    pr, pi = lr, li
    rows_r, rows_i = [pr], [pi]
    for _ in range(S5_ND - 1):
        pr, pi = pr * pr - pi * pi, 2.0 * pr * pi
        rows_r.append(pr)
        rows_i.append(pi)
    lamd_r, lamd_i = jnp.concatenate(rows_r, axis=0), jnp.concatenate(rows_i, axis=0)
    tr = jnp.broadcast_to(lr, (S5_SUB, w))
    ti = jnp.broadcast_to(li, (S5_SUB, w))
    for j in range(S5_ND):
        sr, si = _shift_rows(tr, 1 << j, 1.0), _shift_rows(ti, 1 << j, 0.0)
        tr, ti = tr * sr - ti * si, tr * si + ti * sr
    return lamd_r, lamd_i, tr, ti, expand(bbr), expand(bbi), expand(ctr), expand(cti)


def _s5_tile(pid, carry, u, lamd_r, lamd_i, lam8_r, lam8_i, bbr, bbi, ccr, cci, dvec):
    cr, ci = carry
    hr = _dot(u, bbr)
    hi = _dot(u, bbi)
    T = hr.shape[0]
    sub = lax.broadcasted_iota(jnp.int32, (T, 1), 0) % S5_SUB
    for j in range(S5_ND):
        keep = sub >= (1 << j)
        sr = jnp.where(keep, _shift_rows(hr, 1 << j, 0.0), 0.0)
        si = jnp.where(keep, _shift_rows(hi, 1 << j, 0.0), 0.0)
        ar, ai = lamd_r[j:j + 1], lamd_i[j:j + 1]
        hr, hi = hr + ar * sr - ai * si, hi + ar * si + ai * sr
    rows_r, rows_i = [], []
    for i in range(T // S5_SUB):
        gr_, gi_ = hr[i * S5_SUB:(i + 1) * S5_SUB], hi[i * S5_SUB:(i + 1) * S5_SUB]
        gr_, gi_ = gr_ + lam8_r * cr - lam8_i * ci, gi_ + lam8_r * ci + lam8_i * cr
        cr, ci = gr_[S5_SUB - 1:], gi_[S5_SUB - 1:]
        rows_r.append(gr_)
        rows_i.append(gi_)
    hr, hi = jnp.concatenate(rows_r, axis=0), jnp.concatenate(rows_i, axis=0)
    y = _dot(hr, ccr, _NT) - _dot(hi, cci, _NT) + dvec * u
    return (jax.nn.gelu(y),), (cr, ci)


def _glu_tile(pid, g, glu, za, b):
    return (g * jax.nn.sigmoid(glu + b) * jax.nn.silu(za),)


def _attn_tile(pid, carry, q, k, v, qw, kw):
    n = pid[1]
    kp, vp = carry
    D, B = ATT_HEAD_DIM, ATT_BLOCK
    nq, ncol = q.shape[0] // B, q.shape[1] // D
    r = lax.broadcasted_iota(jnp.int32, (B, 2 * B), 0)
    c = lax.broadcasted_iota(jnp.int32, (B, 2 * B), 1)
    diff = r + B - c
    band = (diff >= 0) & (diff <= B)
    band_first = band & ((c >= B) | (n > 0))

    def hnorm(x, w):
        return x * lax.rsqrt(jnp.mean(x * x, axis=-1, keepdims=True) + RMS_EPS) * w

    out_cols, lse_cols, kn_cols = [], [], []
    for j in range(ncol):
        sl = slice(j * D, (j + 1) * D)
        qn, kn, vj = hnorm(q[:, sl], qw), hnorm(k[:, sl], kw), v[:, sl]
        kn_cols.append(kn[(nq - 1) * B:])
        outs, lses = [], []
        for b in range(nq):
            rows = slice(b * B, (b + 1) * B)
            prev = slice((b - 1) * B, b * B)
            kk = jnp.concatenate([kp[:, sl] if b == 0 else kn[prev], kn[rows]], axis=0)
            vv = jnp.concatenate([vp[:, sl] if b == 0 else vj[prev], vj[rows]], axis=0)
            s = _dot(qn[rows], kk, _NT) * (D ** -0.5)
            s = jnp.where(band_first if b == 0 else band, s, -1e30)
            m = jnp.max(s, axis=-1, keepdims=True)
            p = jnp.exp(s - m)
            l = jnp.sum(p, axis=-1, keepdims=True)
            outs.append(_dot(p / l, vv))
            lses.append(jnp.broadcast_to(m + jnp.log(l), (B, D)))
        out_cols.append(jnp.concatenate(outs, axis=0) if nq > 1 else outs[0])
        lse_cols.append(jnp.concatenate(lses, axis=0) if nq > 1 else lses[0])
    return ((jnp.concatenate(out_cols, axis=1), jnp.concatenate(lse_cols, axis=1)),
            (jnp.concatenate(kn_cols, axis=1), v[(nq - 1) * B:]))


def _combine_tile(pid, o1, l1, o2, l2, o3, l3, zb):
    m = jnp.maximum(jnp.maximum(l1, l2), l3)
    e1, e2, e3 = jnp.exp(l1 - m), jnp.exp(l2 - m), jnp.exp(l3 - m)
    y = (e1 * o1 + e2 * o2 + e3 * o3) / (e1 + e2 + e3)
    return (y * jax.nn.silu(zb),)


def _softplus(x):
    return jnp.maximum(x, 0.0) + jnp.log(1.0 + jnp.exp(-jnp.abs(x)))


def _ssd_tile(pid, carry, xbc, dt, z, conv_w, conv_b, dt_bias, a_log, dvec, norm_w):
    xprev, state = carry
    T, P, N = SSD_CHUNK, SSD_HEAD_DIM, SSD_STATE
    xx = jnp.concatenate([xprev, xbc], axis=0)
    conv = conv_b
    for k in range(SSD_CONV):
        off = 8 - (SSD_CONV - 1) + k
        conv = conv + conv_w[k:k + 1] * xx[off:off + T]
    xc = jax.nn.silu(conv)
    dtp = _softplus(dt + dt_bias)
    a_dt = dtp * (-jnp.exp(a_log))
    r = lax.broadcasted_iota(jnp.int32, (T, T), 0)
    c = lax.broadcasted_iota(jnp.int32, (T, T), 1)
    tri = r >= c
    trif = tri.astype(f32)
    hi = lax.Precision.HIGHEST
    a_cs = jnp.dot(trif, a_dt, precision=hi, preferred_element_type=f32)
    a_cs_t = lax.dot_general(a_dt, trif, (((0,), (1,)), ((), ())), precision=hi, preferred_element_type=f32)
    ys, states = [], []
    for g in range(SSD_GROUPS):
        bg = xc[:, SSD_WIDTH + g * N:SSD_WIDTH + (g + 1) * N]
        cg = xc[:, SSD_WIDTH + SSD_GROUPS * N + g * N:SSD_WIDTH + SSD_GROUPS * N + (g + 1) * N]
        cb = _dot(cg, bg, _NT)
        for e in range(SSD_HEADS // SSD_GROUPS):
            h = g * (SSD_HEADS // SSD_GROUPS) + e
            col, row = a_cs[:, h:h + 1], a_cs_t[h:h + 1, :]
            decay = jnp.exp(jnp.where(tri, col - row, -1e30))
            xh = xc[:, h * P:(h + 1) * P]
            xdt = xh * dtp[:, h:h + 1]
            st = state[h * P:(h + 1) * P, :]
            last = a_cs[T - 1:T, h:h + 1]
            y = _dot(cb * decay, xdt) + _dot(cg, st, _NT) * jnp.exp(col) + xh * dvec[:, h:h + 1]
            ys.append(y)
            states.append(jnp.exp(last) * st + _dot(xdt * jnp.exp(last - col), bg, _TN))
    y = jnp.concatenate(ys, axis=1) * jax.nn.silu(z)
    out = y * lax.rsqrt(jnp.mean(y * y, axis=-1, keepdims=True) + RMS_EPS) * norm_w
    return (out,), (xbc[T - 8:], jnp.concatenate(states, axis=0))


def _merge_tile(pid, pa, pb, pc, g0, g1, g2):
    return (jax.nn.sigmoid(g0) * pa + jax.nn.sigmoid(g1) * pb + jax.nn.sigmoid(g2) * pc,)


def loss_and_grad(y, target, tm=512):
    S, D = y.shape
    nt = S // tm

    def body(y_ref, t_ref, dy_ref, l_ref, acc):
        i = pl.program_id(0)

        @pl.when(i == 0)
        def _():
            acc[...] = jnp.zeros_like(acc)

        diff = y_ref[...] - t_ref[...]
        dy_ref[...] = diff * (1.0 / D)
        acc[...] += jnp.sum((diff * diff).reshape(tm // 8, 8, D), axis=0)

        @pl.when(i == nt - 1)
        def _():
            l_ref[...] = jnp.broadcast_to(0.5 / D * jnp.sum(acc[...]), l_ref.shape)

    dy, l = pl.pallas_call(
        body, name="loss_head", grid=(nt,),
        in_specs=[pl.BlockSpec((tm, D), lambda i: (i, 0))] * 2,
        out_specs=[pl.BlockSpec((tm, D), lambda i: (i, 0)), pl.BlockSpec((8, 128), lambda i: (0, 0))],
        out_shape=[jax.ShapeDtypeStruct((S, D), f32), jax.ShapeDtypeStruct((8, 128), f32)],
        scratch_shapes=[pltpu.VMEM((8, D), f32)],
        compiler_params=_cparams(1))(y, target)
    return dy, l[0, 0]


def _row_tile(R, C, budget=1 << 20):
    best = R
    for t in range(8, R, 8):
        if R % t == 0 and t * C * 4 <= budget:
            best = t
    if best == R and R * C * 4 > budget:
        for t in range(8, R, 8):
            if R % t == 0:
                return t
    return best


def _as2d(t, lead=0):
    return t.reshape(t.shape[:lead] + (math.prod(t.shape[lead:-1]), t.shape[-1]))


def adamw(name, w, gslots, m, v):
    shape = w.shape
    n = gslots.shape[0]
    C = shape[-1]
    R = math.prod(shape[:-1])
    lanes = -(-C // 128) * 128
    tr = _row_tile(R, lanes * (n + 7), budget=10 << 20)

    def body(w_ref, g_ref, m_ref, v_ref, go_ref, d_ref, nm_ref, nv_ref):
        gg = g_ref[0].astype(f32)
        for s in range(1, n):
            gg = gg + g_ref[s].astype(f32)
        go_ref[...] = gg
        nm = ADAM_B1 * m_ref[...] + (1.0 - ADAM_B1) * gg
        nv = ADAM_B2 * v_ref[...] + (1.0 - ADAM_B2) * jnp.square(gg)
        m_hat = nm / (1.0 - ADAM_B1 ** ADAM_STEP)
        v_hat = nv / (1.0 - ADAM_B2 ** ADAM_STEP)
        d_ref[...] = -ADAM_LR * (m_hat / (jnp.sqrt(v_hat) + ADAM_EPS) + ADAM_WD * w_ref[...])
        nm_ref[...] = nm
        nv_ref[...] = nv

    spec = pl.BlockSpec((tr, C), lambda i: (i, 0))
    res = pl.pallas_call(
        body, name=name, grid=(R // tr,),
        in_specs=[spec, pl.BlockSpec((n, tr, C), lambda i: (0, i, 0)), spec, spec], out_specs=[spec] * 4,
        out_shape=[jax.ShapeDtypeStruct((R, C), f32)] * 4,
        compiler_params=_cparams(1))(w.reshape(R, C), gslots.reshape(n, R, C), m.reshape(R, C), v.reshape(R, C))
    return tuple(t.reshape(shape) for t in res)


PACK_ROWS = 256


def sum_slots(name, x):
    n, R, C = x.shape

    def body(x_ref, o_ref):
        acc = x_ref[0]
        for s in range(1, n):
            acc = acc + x_ref[s]
        o_ref[...] = acc

    return pl.pallas_call(
        body, name=name, grid=(R // PACK_ROWS,),
        in_specs=[pl.BlockSpec((n, PACK_ROWS, C), lambda i: (0, i, 0))],
        out_specs=pl.BlockSpec((PACK_ROWS, C), lambda i: (i, 0)),
        out_shape=jax.ShapeDtypeStruct((R, C), f32), compiler_params=_cparams(1))(x)


def _pack(parts):
    flat = jnp.concatenate([p.reshape(-1) for p in parts])
    unit = 128 * PACK_ROWS
    tot = -(-flat.shape[0] // unit) * unit
    return jnp.pad(flat, (0, tot - flat.shape[0])).reshape(tot // 128, 128)


def _unpack(buf, shapes):
    flat = buf.reshape(-1)
    out, off = [], 0
    for s in shapes:
        size = math.prod(s)
        out.append(flat[off:off + size].reshape(s))
        off += size
    return out


def exchange(name, srcs, modes):
    nt = len(srcs)
    slabs = []
    for s, mode in zip(srcs, modes):
        R, C = s.shape
        slabs.append({'all': (R, C), 'rows': (R // N_DEV, C), 'cols': (R, C // N_DEV)}[mode])

    def piece(ref, mode, slab, p):
        if mode == 'all':
            return ref
        if mode == 'rows':
            return ref.at[pl.ds(p * slab[0], slab[0]), :]
        return ref.at[:, pl.ds(p * slab[1], slab[1])]

    def body(*refs):
        src_refs, out_refs = refs[:nt], refs[nt:2 * nt]
        send_sems, recv_sems, local_sems = refs[2 * nt:]
        x, y, c = lax.axis_index("x"), lax.axis_index("y"), lax.axis_index("c")
        me = 4 * x + 2 * y + c
        copies = []
        for k in (1, 2, 4, 3, 5, 6, 7):
            px = 1 - x if k & 4 else x
            py = 1 - y if k & 2 else y
            pc = 1 - c if k & 1 else c
            for t in range(nt):
                cp = pltpu.make_async_remote_copy(
                    src_ref=piece(src_refs[t], modes[t], slabs[t], 4 * px + 2 * py + pc), dst_ref=out_refs[t].at[me],
                    send_sem=send_sems.at[t, k - 1], recv_sem=recv_sems.at[t, k - 1],
                    device_id=(px, py, pc), device_id_type=pl.DeviceIdType.MESH)
                cp.start()
                copies.append(cp)
        for t in range(nt):
            own = pltpu.make_async_copy(piece(src_refs[t], modes[t], slabs[t], me), out_refs[t].at[me], local_sems.at[t])
            own.start()
            copies.append(own)
        for cp in copies:
            cp.wait()

    res = pl.pallas_call(
        body, name=name,
        in_specs=[pl.BlockSpec(memory_space=pl.ANY)] * nt,
        out_specs=[pl.BlockSpec(memory_space=pl.ANY)] * nt,
        out_shape=[jax.ShapeDtypeStruct((N_DEV,) + sl, s.dtype) for s, sl in zip(srcs, slabs)],
        scratch_shapes=[pltpu.SemaphoreType.DMA((nt, N_DEV - 1)), pltpu.SemaphoreType.DMA((nt, N_DEV - 1)), pltpu.SemaphoreType.DMA((nt,))],
    )(*srcs)
    return list(res)


def _relayout_w_in(w):
    offs = [0]
    for s in IN_SPLITS:
        offs.append(offs[-1] + s)
    p = [w[:, offs[i]:offs[i + 1]] for i in range(len(IN_SPLITS))]
    ua, za, q, k, v, zb, xbc, dt, zc, gates = p
    dtp = jnp.pad(dt, ((0, 0), (0, HPAD - dt.shape[1])))
    return (jnp.concatenate([ua, za, dtp], 1), w[:, offs[2]:offs[5]], jnp.concatenate([xbc, zb, zc], 1), gates)


def _pad_lanes(v, n=HPAD):
    return jnp.pad(v.reshape(1, -1), ((0, 0), (0, n - v.shape[-1])))


def _s5_prep_args(W):
    g2 = S5_GROUPS * S5_STATE
    w = g2 // S5_CHUNKS
    a_re, a_im = W['s5_a_re'].reshape(1, g2), W['s5_a_im'].reshape(1, g2)
    ls = W['s5_log_step'].reshape(1, S5_GROUPS)
    btr, bti = W['s5_b_re'].reshape(g2, S5_GROUP).T, W['s5_b_im'].reshape(g2, S5_GROUP).T
    ctr = W['s5_c_re'].transpose(1, 0, 2).reshape(S5_GROUP, g2)
    cti = W['s5_c_im'].transpose(1, 0, 2).reshape(S5_GROUP, g2)
    col = lambda a, rows: Arg(a, (rows, w), lambda o: (0, o), 'tile')
    return [col(a_re, 1), col(a_im, 1), _whole(ls, 'acc'), col(btr, S5_GROUP), col(bti, S5_GROUP), col(ctr, S5_GROUP), col(cti, S5_GROUP)]


def _s5_prep_outs():
    g2 = S5_GROUPS * S5_STATE
    w = g2 // S5_CHUNKS
    rows = (S5_ND, S5_ND, S5_SUB, S5_SUB, 128, 128, 128, 128)
    return [Out((r, g2), f32, (r, w), lambda o: (0, o)) for r in rows]


def _s5_args(A, prep, dvec, S):
    w = S5_GROUPS * S5_STATE // S5_CHUNKS
    args = [Arg(A, (S5_TILE, 128), lambda o, t: (t, o), 'tile', (S, S5_WIDTH), None)]
    for p in prep:
        args.append(Arg(p, (p.shape[0], w), lambda o, t: (0, o), 'acc0'))
    args.append(Arg(dvec, (1, 128), lambda o, t: (0, o), 'acc0'))
    return args


def _attn_args(QKV, g, r, qw, kw, S):
    L = S // r
    nq, rb = _attn_plan(r)
    block = (nq * ATT_BLOCK, rb * ATT_GW)
    gshape = (L, r * ATT_GW)
    gimap = lambda rho, n: (n, rho)
    if r == 1:
        mk = lambda j: Arg(QKV, block, lambda rho, n, j=j: (n, j), 'tile', gshape, gimap)
    else:
        def mk(j):
            view = QKV[:, j * ATT_GW:(j + 1) * ATT_GW].reshape(L, r * ATT_GW)
            return Arg(view, block, gimap, 'tile')
    return [mk(g), mk(3 + g), mk(6 + g), _whole(qw, 'acc'), _whole(kw, 'acc')]


def _attn_plan(r):
    return (4, 1) if r == 1 else (1, min(r, 4))


def _attn_grid(r, S):
    nq, rb = _attn_plan(r)
    return (r // rb, S // r // ATT_BLOCK // nq)


def _attn_carry(r):
    return ((ATT_BLOCK, _attn_plan(r)[1] * ATT_GW),) * 2


def _ssd_args(C, A, W, S):
    T = SSD_CHUNK
    return [Arg(C, (T, SSD_CONV_DIM), lambda o, t: (t, 0), 'tile', (S, SSD_CONV_DIM), None),
            Arg(A, (T, HPAD), lambda o, t: (t, 2 * S5_WIDTH // HPAD), 'tile', (S, HPAD), lambda o, t: (t, 0)),
            Arg(C, (T, SSD_WIDTH), lambda o, t: (t, 2), 'tile', (S, SSD_WIDTH), lambda o, t: (t, 0)),
            _whole(W['conv_w'], 'acc'), _whole(W['conv_b'].reshape(1, -1), 'acc'),
            _whole(_pad_lanes(W['dt_bias']), 'acc'), _whole(_pad_lanes(W['ssd_a_log']), 'acc'),
            _whole(_pad_lanes(W['ssd_d']), 'acc'), _whole(W['ssd_norm_w'].reshape(1, -1), 'acc')]


_SSD_CARRY = ((8, SSD_CONV_DIM), (SSD_WIDTH, SSD_STATE))
_S5_CARRY = ((1, 512), (1, 512))


def layer_fwd(li, x, W):
    S = x.shape[0]
    n = lambda s: f"l{li}_{s}"
    sv = {'x': x}
    (h,) = map_fwd(n("norm"), _rmsnorm_tile, (S // 512,), [_rows(x, 512), _whole(W['norm_w'].reshape(1, -1))],
                   [Out((S, D_MODEL), bf16, (512, D_MODEL), lambda i: (i, 0))])
    wA, wQ, wC, wG = W['w_in_pieces']
    A = matmul(n("in_a"), h, wA)
    QKV = matmul(n("in_qkv"), h, wQ)
    C = matmul(n("in_c"), h, wC)
    G = matmul(n("in_g"), h, wG)
    sv.update(h=h, A=A, QKV=QKV, C=C, G=G)

    prep = map_fwd(n("s5_prep"), _s5_prep_tile, (S5_CHUNKS,), _s5_prep_args(W), _s5_prep_outs())
    dvec = W['s5_d'].reshape(1, -1)
    (g,), s5_ck = scan_fwd(n("s5_scan"), _s5_tile, (S5_CHUNKS, S // S5_TILE), _S5_CARRY, _s5_args(A, prep, dvec, S),
                           [Out((S, S5_WIDTH), f32, (S5_TILE, 128), lambda o, t: (t, o))])
    glu = matmul(n("glu"), g, W['s5_glu_w'])
    glu_b = W['s5_glu_b'].reshape(1, -1)
    (ya,) = map_fwd(n("glu_gate"), _glu_tile, (S // 512,),
                    [_rows(g, 512), _rows(glu, 512), _rows(A, 512, col=1, width=S5_WIDTH), _whole(glu_b)],
                    [Out((S, S5_WIDTH), f32, (512, S5_WIDTH), lambda i: (i, 0))])
    sv.update(prep=prep, g=g, glu=glu, ya=ya, s5_ck=s5_ck)

    qw, kw = W['q_norm_w'].reshape(1, -1), W['k_norm_w'].reshape(1, -1)
    att, att_ck = [], []
    for gi, (window, r) in enumerate(ATT_PAIRS):
        assert window // r == ATT_BLOCK and S % (r * ATT_BLOCK) == 0
        L = S // r
        nq, rb = _attn_plan(r)
        assert S // r // ATT_BLOCK % nq == 0
        spec = Out((L, r * ATT_GW), f32, (nq * ATT_BLOCK, rb * ATT_GW), lambda rho, nb: (nb, rho))
        (o, lse), ck = scan_fwd(n(f"attn{gi}"), _attn_tile, _attn_grid(r, S), _attn_carry(r), _attn_args(QKV, gi, r, qw, kw, S), [spec, spec])
        att += [o.reshape(S, ATT_GW), lse.reshape(S, ATT_GW)]
        att_ck.append(ck)
    (yb,) = map_fwd(n("combine"), _combine_tile, (S // 512,),
                    [_rows(t, 512) for t in att] + [_rows(C, 512, col=SSD_CONV_DIM // ATT_GW, width=ATT_GW)],
                    [Out((S, ATT_GW), f32, (512, ATT_GW), lambda i: (i, 0))])
    sv.update(att=att, att_ck=att_ck, yb=yb)

    (yc,), ssd_ck = scan_fwd(n("ssd"), _ssd_tile, (1, S // SSD_CHUNK), _SSD_CARRY, _ssd_args(C, A, W, S),
                             [Out((S, SSD_WIDTH), f32, (SSD_CHUNK, SSD_WIDTH), lambda o, t: (t, 0))])
    sv.update(yc=yc, ssd_ck=ssd_ck)

    pa = matmul(n("proj_a"), ya, W['proj_a'])
    pb = matmul(n("proj_b"), yb, W['proj_b'])
    pc = matmul(n("proj_c"), yc, W['proj_c'])
    (merged,) = map_fwd(n("merge"), _merge_tile, (S // 256,),
                        [_rows(pa, 256), _rows(pb, 256), _rows(pc, 256)] + [_rows(G, 256, col=j, width=D_MODEL) for j in range(3)],
                        [Out((S, D_MODEL), f32, (256, D_MODEL), lambda i: (i, 0))])
    out = matmul(n("w_out"), merged, W['w_out'], add=x)
    sv.update(pa=pa, pb=pb, pc=pc, merged=merged)
    return out, sv


def layer_bwd(li, dout, sv, W):
    S = dout.shape[0]
    n = lambda s: f"l{li}_{s}"
    gr = {}
    x, A, QKV, C, G = sv['x'], sv['A'], sv['QKV'], sv['C'], sv['G']

    dmerged = matmul(n("d_merged"), dout, W['w_out'], 'nt')
    gr['w_out'] = wgrad(n("g_w_out"), sv['merged'], dout)
    margs = [_rows(sv['pa'], 256), _rows(sv['pb'], 256), _rows(sv['pc'], 256)] + \
            [_rows(G, 256, col=j, width=D_MODEL, gshape=(S, D_MODEL)) for j in range(3)]
    dpa, dpb, dpc, dg0, dg1, dg2 = map_bwd(n("merge_bwd"), _merge_tile, (S // 256,), margs, [_rows(dmerged, 256)], list(range(6)))
    dya = matmul(n("d_ya"), dpa, W['proj_a'], 'nt')
    dyb = matmul(n("d_yb"), dpb, W['proj_b'], 'nt')
    dyc = matmul(n("d_yc"), dpc, W['proj_c'], 'nt')
    gr['proj_a'] = wgrad(n("g_proj_a"), sv['ya'], dpa)
    gr['proj_b'] = wgrad(n("g_proj_b"), sv['yb'], dpb)
    gr['proj_c'] = wgrad(n("g_proj_c"), sv['yc'], dpc)

    glu_b = W['s5_glu_b'].reshape(1, -1)
    gargs = [_rows(sv['g'], 512), _rows(sv['glu'], 512), _rows(A, 512, col=1, width=S5_WIDTH, gshape=(S, S5_WIDTH)), _whole(glu_b, 'acc')]
    dg_a, dglu, dza, dglu_b = map_bwd(n("glu_gate_bwd"), _glu_tile, (S // 512,), gargs, [_rows(dya, 512)], [0, 1, 2, 3])
    gr['s5_glu_b'] = dglu_b.reshape(-1)
    dg = matmul(n("d_g"), dglu, W['s5_glu_w'], 'nt', add=dg_a)
    gr['s5_glu_w'] = wgrad(n("g_glu_w"), sv['g'], dglu)
    dvec = W['s5_d'].reshape(1, -1)
    sargs = _s5_args(A, sv['prep'], dvec, S)
    res = scan_bwd(n("s5_scan_bwd"), _s5_tile, (S5_CHUNKS, S // S5_TILE), _S5_CARRY, sargs, sv['s5_ck'],
                   [Arg(dg, (S5_TILE, 128), lambda o, t: (t, o))], list(range(len(sargs))))
    dua, dprep, dd = res[0], res[1:9], res[9]
    gr['s5_d'] = dd.reshape(-1)
    pargs = _s5_prep_args(W)
    pouts = _s5_prep_outs()
    da_re, da_im, dls, dbtr, dbti, dctr, dcti = map_bwd(
        n("s5_prep_bwd"), _s5_prep_tile, (S5_CHUNKS,), pargs,
        [Arg(d, o.block, o.imap) for d, o in zip(dprep, pouts)], list(range(7)))
    gshape = (S5_GROUPS, S5_STATE)
    gr['s5_a_re'], gr['s5_a_im'] = da_re.reshape(gshape), da_im.reshape(gshape)
    gr['s5_log_step'] = dls.reshape(-1)
    gr['s5_b_re'] = dbtr.T.reshape(S5_GROUPS, S5_STATE, S5_GROUP)
    gr['s5_b_im'] = dbti.T.reshape(S5_GROUPS, S5_STATE, S5_GROUP)
    gr['s5_c_re'] = dctr.reshape(S5_GROUP, S5_GROUPS, S5_STATE).transpose(1, 0, 2)
    gr['s5_c_im'] = dcti.reshape(S5_GROUP, S5_GROUPS, S5_STATE).transpose(1, 0, 2)

    cargs = [_rows(t, 512) for t in sv['att']] + [_rows(C, 512, col=SSD_CONV_DIM // ATT_GW, width=ATT_GW, gshape=(S, ATT_GW))]
    cres = map_bwd(n("combine_bwd"), _combine_tile, (S // 512,), cargs, [_rows(dyb, 512)], list(range(7)))
    dzb = cres[6]
    qw, kw = W['q_norm_w'].reshape(1, -1), W['k_norm_w'].reshape(1, -1)
    dqs, dks, dvs = [], [], []
    dqw = dkw = None
    for gi, (window, r) in enumerate(ATT_PAIRS):
        L = S // r
        nq, rb = _attn_plan(r)
        dspec = lambda t: Arg(t.reshape(L, r * ATT_GW), (nq * ATT_BLOCK, rb * ATT_GW), lambda rho, nb: (nb, rho))
        dq, dk, dv, dqw_g, dkw_g = scan_bwd(n(f"attn{gi}_bwd"), _attn_tile, _attn_grid(r, S), _attn_carry(r),
                                            _attn_args(QKV, gi, r, qw, kw, S), sv['att_ck'][gi],
                                            [dspec(cres[2 * gi]), dspec(cres[2 * gi + 1])], [0, 1, 2, 3, 4])
        dqs.append(dq.reshape(S, ATT_GW))
        dks.append(dk.reshape(S, ATT_GW))
        dvs.append(dv.reshape(S, ATT_GW))
        dqw = dqw_g if dqw is None else dqw + dqw_g
        dkw = dkw_g if dkw is None else dkw + dkw_g
    gr['q_norm_w'], gr['k_norm_w'] = dqw.reshape(-1), dkw.reshape(-1)

    ssd_args = _ssd_args(C, A, W, S)
    sres = scan_bwd(n("ssd_bwd"), _ssd_tile, (1, S // SSD_CHUNK), _SSD_CARRY, ssd_args, sv['ssd_ck'],
                    [Arg(dyc, (SSD_CHUNK, SSD_WIDTH), lambda o, t: (t, 0))], list(range(9)))
    dxbc, ddt, dzc = sres[0], sres[1], sres[2]
    gr['conv_w'] = sres[3]
    gr['conv_b'] = sres[4].reshape(-1)
    gr['dt_bias'] = sres[5].reshape(-1)[:SSD_HEADS]
    gr['ssd_a_log'] = sres[6].reshape(-1)[:SSD_HEADS]
    gr['ssd_d'] = sres[7].reshape(-1)[:SSD_HEADS]
    gr['ssd_norm_w'] = sres[8].reshape(-1)

    dpieces = [jnp.concatenate([dua, dza, ddt], axis=1), jnp.concatenate(dqs + dks + dvs, axis=1),
               jnp.concatenate([dxbc, dzb, dzc], axis=1), jnp.concatenate([dg0, dg1, dg2], axis=1)]
    dh, gw = None, []
    for j, (dp, wp) in enumerate(zip(dpieces, W['w_in_pieces'])):
        dh = matmul(n(f"d_h{j}"), dp, wp, 'nt', add=dh)
        gw.append(wgrad(n(f"g_w_in{j}"), sv['h'], dp))
    gr['w_in'] = _unrelayout_w_in_grad(gw)
    nargs = [_rows(x, 512), _whole(W['norm_w'].reshape(1, -1), 'acc')]
    dx, dnw = map_bwd(n("norm_bwd"), _rmsnorm_tile, (S // 512,), nargs, [_rows(dh, 512)], [0, 1], add={0: _rows(dout, 512)})
    gr['norm_w'] = dnw.reshape(-1)
    return dx, gr


def _unrelayout_w_in_grad(pieces):
    gA, gQ, gC, gG = pieces
    uaza, dt = gA[:, :2 * S5_WIDTH], gA[:, 2 * S5_WIDTH:2 * S5_WIDTH + SSD_HEADS]
    xbc, zb, zc = gC[:, :SSD_CONV_DIM], gC[:, SSD_CONV_DIM:SSD_CONV_DIM + ATT_GW], gC[:, SSD_CONV_DIM + ATT_GW:]
    return jnp.concatenate([uaza, gQ, zb, xbc, dt, zc, gG], axis=1)


def kernel(x, norm_w, w_in, s5_a_re, s5_a_im, s5_log_step, s5_b_re, s5_b_im, s5_c_re, s5_c_im, s5_d, s5_glu_w, s5_glu_b, q_norm_w, k_norm_w, conv_w, conv_b, dt_bias, ssd_a_log, ssd_d, ssd_norm_w, proj_a, proj_b, proj_c, w_out, loss_target, m_norm_w, m_w_in, m_s5_a_re, m_s5_a_im, m_s5_log_step, m_s5_b_re, m_s5_b_im, m_s5_c_re, m_s5_c_im, m_s5_d, m_s5_glu_w, m_s5_glu_b, m_q_norm_w, m_k_norm_w, m_conv_w, m_conv_b, m_dt_bias, m_ssd_a_log, m_ssd_d, m_ssd_norm_w, m_proj_a, m_proj_b, m_proj_c, m_w_out, v_norm_w, v_w_in, v_s5_a_re, v_s5_a_im, v_s5_log_step, v_s5_b_re, v_s5_b_im, v_s5_c_re, v_s5_c_im, v_s5_d, v_s5_glu_w, v_s5_glu_b, v_q_norm_w, v_k_norm_w, v_conv_w, v_conv_b, v_dt_bias, v_ssd_a_log, v_ssd_d, v_ssd_norm_w, v_proj_a, v_proj_b, v_proj_c, v_w_out):
    args = dict(locals())
    w = {k: args[k] for k in WEIGHTS}
    m = {k: args['m_' + k] for k in WEIGHTS}
    v = {k: args['v_' + k] for k in WEIGHTS}
    depth = norm_w.shape[0]
    S = x.shape[1]
    xs = x.reshape(S, D_MODEL)
    tgt = loss_target.reshape(S, D_MODEL)

    keys = [(li, k) for li in range(depth) for k in SHARDED]
    gathered = exchange("gather_weights", [w[k][li].astype(bf16) for li, k in keys], ['all'] * len(keys))
    layers = [{k: w[k][li] for k in WEIGHTS if k not in SHARDED} for li in range(depth)]
    for (li, k), t in zip(keys, gathered):
        n_dev, R, C = t.shape
        layers[li][k] = t.reshape(n_dev * R, C) if k in ROW_SHARDED else t.transpose(1, 0, 2).reshape(R, n_dev * C)
    for W in layers:
        W['w_in_pieces'] = _relayout_w_in(W['w_in'])

    act, saved = xs, []
    for li in range(depth):
        act, sv = layer_fwd(li, act, layers[li])
        saved.append(sv)
    dy, loss_local = loss_and_grad(act, tgt)
    loss = lax.psum(loss_local, ("x", "y", "c"))

    grads = [None] * depth
    for li in reversed(range(depth)):
        dy, grads[li] = layer_bwd(li, dy, saved[li], layers[li])
    grad_x = dy.reshape(x.shape)

    big_keys = [(li, k) for li in range(depth) for k in SHARDED if k != 'conv_w']
    slots = exchange("scatter_grads", [grads[li][k] for li, k in big_keys],
                     ['rows' if k in ROW_SHARDED else 'cols' for _, k in big_keys])
    per_layer = {}
    for (li, k), s in zip(big_keys, slots):
        per_layer[li, k] = adamw(f"adamw_{k}{li}", w[k][li], s, m[k][li], v[k][li])
    result = {k: tuple(jnp.stack([per_layer[li, k][j] for li in range(depth)], axis=0) for j in range(4))
              for k in SHARDED if k != 'conv_w'}

    small_keys = [k for k in WEIGHTS if k not in SHARDED] + ['conv_w']
    stacked = [jnp.stack([grads[li][k] for li in range(depth)], axis=0) for k in small_keys]
    (small_slots,) = exchange("gather_small_grads", [_pack(stacked)], ['all'])
    totals = _unpack(sum_slots("sum_small_grads", small_slots), [t.shape for t in stacked])
    for k, g in zip(small_keys, totals):
        if k == 'conv_w':
            width = w[k].shape[-1]
            me = 4 * lax.axis_index("x") + 2 * lax.axis_index("y") + lax.axis_index("c")
            g = lax.dynamic_slice_in_dim(g, me * width, width, axis=2)
        result[k] = adamw("adamw_" + k, w[k], g[None], m[k], v[k])

    return (loss, grad_x, *[result[k][0] for k in WEIGHTS], *[result[k][1] for k in WEIGHTS],
            *[result[k][2] for k in WEIGHTS], *[result[k][3] for k in WEIGHTS])
```

```python
import functools
import math
from typing import Any, NamedTuple

import jax
import jax.numpy as jnp
from jax import lax
from jax.experimental import pallas as pl
from jax.experimental.pallas import tpu as pltpu

f32 = jnp.float32
bf16 = jnp.bfloat16

N_DEV = 8
D_MODEL = 1024
RMS_EPS = 1e-6
S5_WIDTH = 512
S5_GROUPS = 32
S5_GROUP = 16
S5_STATE = 64
S5_TILE = 256
S5_SUB = 8
S5_ND = 3
S5_CHUNKS = 4
ATT_HEAD_DIM = 64
ATT_PAIRS = ((128, 1), (512, 4), (2048, 16))
ATT_HPG = 4
ATT_BLOCK = 128
ATT_GW = ATT_HPG * ATT_HEAD_DIM
ATT_WIDTH = 768
SSD_HEADS = 12
SSD_HEAD_DIM = 64
SSD_WIDTH = 768
SSD_STATE = 128
SSD_GROUPS = 2
SSD_CHUNK = 128
SSD_CONV = 4
SSD_CONV_DIM = 1280
HPAD = 128
IN_SPLITS = (512, 512, 768, 768, 768, 256, 1280, 12, 768, 3072)
ADAM_LR, ADAM_B1, ADAM_B2, ADAM_EPS, ADAM_WD, ADAM_STEP = 0.001, 0.9, 0.999, 1e-08, 0.01, 10
VMEM_LIMIT = 56 * 1024 * 1024

WEIGHTS = ['norm_w', 'w_in', 's5_a_re', 's5_a_im', 's5_log_step', 's5_b_re', 's5_b_im', 's5_c_re',
           's5_c_im', 's5_d', 's5_glu_w', 's5_glu_b', 'q_norm_w', 'k_norm_w', 'conv_w', 'conv_b',
           'dt_bias', 'ssd_a_log', 'ssd_d', 'ssd_norm_w', 'proj_a', 'proj_b', 'proj_c', 'w_out']
ROW_SHARDED = ('w_in', 's5_glu_w', 'w_out')
SHARDED = ROW_SHARDED + ('conv_w', 'proj_a', 'proj_b', 'proj_c')


class Arg(NamedTuple):
    arr: Any
    block: tuple
    imap: Any
    kind: str = 'const'
    gshape: Any = None
    gimap: Any = None
    gdtype: Any = None


class Out(NamedTuple):
    shape: tuple
    dtype: Any
    block: tuple
    imap: Any


def _cparams(n):
    return pltpu.CompilerParams(dimension_semantics=("arbitrary",) * n, vmem_limit_bytes=VMEM_LIMIT)


def _rows(a, tm, kind='tile', col=0, width=None, gshape=None, gcol=None, gdtype=None):
    width = a.shape[1] if width is None else width
    g = None if gshape is None else (lambda i, gc=(0 if gcol is None else gcol): (i, gc))
    return Arg(a, (tm, width), lambda i, c=col: (i, c), kind, gshape, g, gdtype)


def _whole(a, kind='const'):
    nd = a.ndim
    return Arg(a, a.shape, lambda *i, nd=nd: (0,) * nd, kind)


def map_fwd(name, fn, grid, args, outs):
    n_in = len(args)

    def body(*refs):
        pid = tuple(pl.program_id(a) for a in range(len(grid)))
        res = fn(pid, *[r[...] for r in refs[:n_in]])
        for o, r in zip(refs[n_in:], res):
            o[...] = r.astype(o.dtype)

    res = pl.pallas_call(
        body, name=name, grid=grid,
        in_specs=[pl.BlockSpec(a.block, a.imap) for a in args],
        out_specs=[pl.BlockSpec(o.block, o.imap) for o in outs],
        out_shape=[jax.ShapeDtypeStruct(o.shape, o.dtype) for o in outs],
        compiler_params=_cparams(len(grid)))(*[a.arr for a in args])
    return tuple(res)


def _grad_outs(args, wrt):
    outs = []
    for i in wrt:
        a = args[i]
        shape = a.arr.shape if a.gshape is None else a.gshape
        imap = a.imap if a.gimap is None else a.gimap
        outs.append(Out(shape, f32 if a.gdtype is None else a.gdtype, a.block, imap))
    return outs


def _store_grads(pid, args, wrt, grads, grefs, adds):
    first_all = functools.reduce(jnp.logical_and, [p == 0 for p in pid])
    first_in = functools.reduce(jnp.logical_and, [p == 0 for p in pid[1:]]) if len(pid) > 1 else first_all
    for j, i in enumerate(wrt):
        g = grads[j].astype(f32)
        ref = grefs[j]
        kind = args[i].kind
        if kind == 'tile':
            if j in adds:
                g = g + adds[j]
            ref[...] = g.astype(ref.dtype)
        else:
            first = first_all if kind == 'acc' else first_in

            @pl.when(first)
            def _(ref=ref):
                ref[...] = jnp.zeros_like(ref)

            ref[...] += g


def map_bwd(name, fn, grid, args, douts, wrt, add=None):
    add = add or {}
    n_in, n_d, n_add = len(args), len(douts), len(add)
    add_keys = sorted(add)
    gouts = _grad_outs(args, wrt)

    def body(*refs):
        pid = tuple(pl.program_id(a) for a in range(len(grid)))
        vals = [r[...] for r in refs[:n_in]]
        dvals = [r[...].astype(f32) for r in refs[n_in:n_in + n_d]]
        avals = {k: refs[n_in + n_d + j][...].astype(f32) for j, k in enumerate(add_keys)}
        grefs = refs[n_in + n_d + n_add:]

        def f(*w):
            full = list(vals)
            for i, x in zip(wrt, w):
                full[i] = x
            return tuple(fn(pid, *full))

        _, vjp = jax.vjp(f, *[vals[i] for i in wrt])
        grads = vjp(tuple(dvals))
        _store_grads(pid, args, wrt, grads, grefs, avals)

    ins = list(args) + list(douts) + [add[k] for k in add_keys]
    res = pl.pallas_call(
        body, name=name, grid=grid,
        in_specs=[pl.BlockSpec(a.block, a.imap) for a in ins],
        out_specs=[pl.BlockSpec(o.block, o.imap) for o in gouts],
        out_shape=[jax.ShapeDtypeStruct(o.shape, o.dtype) for o in gouts],
        compiler_params=_cparams(len(grid)))(*[a.arr for a in ins])
    return tuple(res)


def scan_fwd(name, fn, grid, carry_shapes, args, outs):
    no, nt = grid
    n_in, n_out, n_c = len(args), len(outs), len(carry_shapes)
    cks = [Out((no, nt) + cs, f32, (None, None) + cs, lambda o, t, n=len(cs): (o, t) + (0,) * n) for cs in carry_shapes]

    def body(*refs):
        pid = (pl.program_id(0), pl.program_id(1))
        ins = refs[:n_in]
        orefs = refs[n_in:n_in + n_out]
        ckrefs = refs[n_in + n_out:n_in + n_out + n_c]
        crefs = refs[n_in + n_out + n_c:]

        @pl.when(pid[1] == 0)
        def _():
            for c in crefs:
                c[...] = jnp.zeros_like(c)

        carry = tuple(c[...] for c in crefs)
        for ck, c in zip(ckrefs, carry):
            ck[...] = c
        res, newc = fn(pid, carry, *[r[...] for r in ins])
        for o, r in zip(orefs, res):
            o[...] = r.astype(o.dtype)
        for c, v in zip(crefs, newc):
            c[...] = v

    allouts = list(outs) + cks
    res = pl.pallas_call(
        body, name=name, grid=grid,
        in_specs=[pl.BlockSpec(a.block, a.imap) for a in args],
        out_specs=[pl.BlockSpec(o.block, o.imap) for o in allouts],
        out_shape=[jax.ShapeDtypeStruct(o.shape, o.dtype) for o in allouts],
        scratch_shapes=[pltpu.VMEM(cs, f32) for cs in carry_shapes],
        compiler_params=_cparams(2))(*[a.arr for a in args])
    return tuple(res[:n_out]), tuple(res[n_out:])


def scan_bwd(name, fn, grid, carry_shapes, args, ckpts, douts, wrt):
    no, nt = grid
    n_in, n_d, n_c = len(args), len(douts), len(carry_shapes)

    def rev(imap):
        return lambda o, t: imap(o, nt - 1 - t)

    rargs = [a._replace(imap=rev(a.imap), gimap=None if a.gimap is None else rev(a.gimap)) for a in args]
    rdouts = [a._replace(imap=rev(a.imap)) for a in douts]
    ckargs = [Arg(ck, (None, None) + cs, rev(lambda o, t, n=len(cs): (o, t) + (0,) * n)) for ck, cs in zip(ckpts, carry_shapes)]
    gouts = _grad_outs(rargs, wrt)

    def body(*refs):
        o, t = pl.program_id(0), pl.program_id(1)
        tt = nt - 1 - t
        vals = [r[...] for r in refs[:n_in]]
        dvals = [r[...].astype(f32) for r in refs[n_in:n_in + n_d]]
        carry = tuple(r[...] for r in refs[n_in + n_d:n_in + n_d + n_c])
        grefs = refs[n_in + n_d + n_c:n_in + n_d + n_c + len(wrt)]
        dcrefs = refs[n_in + n_d + n_c + len(wrt):]

        @pl.when(t == 0)
        def _():
            for c in dcrefs:
                c[...] = jnp.zeros_like(c)

        def f(carry, *w):
            full = list(vals)
            for i, x in zip(wrt, w):
                full[i] = x
            res, newc = fn((o, tt), carry, *full)
            return tuple(res), tuple(newc)

        _, vjp = jax.vjp(f, carry, *[vals[i] for i in wrt])
        grads = vjp((tuple(dvals), tuple(c[...] for c in dcrefs)))
        for c, g in zip(dcrefs, grads[0]):
            c[...] = g
        _store_grads((o, t), rargs, wrt, grads[1:], grefs, {})

    ins = rargs + rdouts + ckargs
    res = pl.pallas_call(
        body, name=name, grid=grid,
        in_specs=[pl.BlockSpec(a.block, a.imap) for a in ins],
        out_specs=[pl.BlockSpec(g.block, g.imap) for g in gouts],
        out_shape=[jax.ShapeDtypeStruct(g.shape, g.dtype) for g in gouts],
        scratch_shapes=[pltpu.VMEM(cs, f32) for cs in carry_shapes],
        compiler_params=_cparams(2))(*[a.arr for a in ins])
    return tuple(res)


def _pick(dim, target):
    if dim <= target:
        return dim
    for t in range(target, 127, -128):
        if dim % t == 0:
            return t
    return dim


def matmul(name, a, b, mode='nn', add=None, out_dtype=f32, tm=None, tn=1152, tk=None):
    if mode == 'tn':
        K, M = a.shape
    else:
        M, K = a.shape
    N = b.shape[0] if mode == 'nt' else b.shape[1]
    assert (b.shape[1] if mode == 'nt' else b.shape[0]) == K
    tm = (1024 if mode == 'tn' else 512) if tm is None else tm
    tk = (512 if mode == 'tn' else 1152) if tk is None else tk
    tm, tn, tk = _pick(M, tm), _pick(N, tn), _pick(K, tk)
    nk = K // tk
    a_spec = pl.BlockSpec((tk, tm), lambda i, j, k: (k, i)) if mode == 'tn' else pl.BlockSpec((tm, tk), lambda i, j, k: (i, k))
    b_spec = pl.BlockSpec((tn, tk), lambda i, j, k: (j, k)) if mode == 'nt' else pl.BlockSpec((tk, tn), lambda i, j, k: (k, j))
    dims = {'nn': (((1,), (0,)), ((), ())), 'nt': (((1,), (1,)), ((), ())), 'tn': (((0,), (0,)), ((), ()))}[mode]
    has_add = add is not None

    def body(*refs):
        if has_add:
            a_ref, b_ref, add_ref, o_ref, acc = refs
        else:
            a_ref, b_ref, o_ref, acc = refs
        k = pl.program_id(2)

        @pl.when(k == 0)
        def _():
            acc[...] = add_ref[...].astype(f32) if has_add else jnp.zeros_like(acc)

        acc[...] += lax.dot_general(a_ref[...].astype(bf16), b_ref[...].astype(bf16), dims, preferred_element_type=f32)

        @pl.when(k == nk - 1)
        def _():
            o_ref[...] = acc[...].astype(o_ref.dtype)

    in_specs = [a_spec, b_spec] + ([pl.BlockSpec((tm, tn), lambda i, j, k: (i, j))] if has_add else [])
    ops = [a, b] + ([add] if has_add else [])
    return pl.pallas_call(
        body, name=name, grid=(M // tm, N // tn, nk), in_specs=in_specs,
        out_specs=pl.BlockSpec((tm, tn), lambda i, j, k: (i, j)),
        out_shape=jax.ShapeDtypeStruct((M, N), out_dtype),
        scratch_shapes=[pltpu.VMEM((tm, tn), f32)],
        compiler_params=pltpu.CompilerParams(dimension_semantics=("parallel", "parallel", "arbitrary"), vmem_limit_bytes=VMEM_LIMIT))(*ops)


def wgrad(name, act, dout):
    return matmul(name, act, dout, 'tn', out_dtype=bf16)


def _dot(a, b, dims=(((1,), (0,)), ((), ()))):
    return lax.dot_general(a.astype(bf16), b.astype(bf16), dims, preferred_element_type=f32)


_NT = (((1,), (1,)), ((), ()))
_TN = (((0,), (0,)), ((), ()))


def _three_term_dot(v, sel, dims):
    hi = v.astype(bf16)
    rest = v - hi.astype(f32)
    mid = rest.astype(bf16)
    lo = (rest - mid.astype(f32)).astype(bf16)
    dot = lambda t: lax.dot_general(t, sel, dims, preferred_element_type=f32)
    return dot(hi) + dot(mid) + dot(lo)


@jax.custom_vjp
def _dot_exact01(v, sel):
    return _three_term_dot(v, sel, (((1,), (0,)), ((), ())))


def _dot_exact01_fwd(v, sel):
    return _dot_exact01(v, sel), sel


def _dot_exact01_bwd(sel, ct):
    return _three_term_dot(ct, sel, _NT), jnp.zeros_like(sel)


_dot_exact01.defvjp(_dot_exact01_fwd, _dot_exact01_bwd)


def _spread_heads(v, width):
    r = lax.broadcasted_iota(jnp.int32, (HPAD, SSD_HEADS * width), 0)
    c = lax.broadcasted_iota(jnp.int32, (HPAD, SSD_HEADS * width), 1)
    return _dot_exact01(v, (r == c // width).astype(bf16))


def _rmsnorm_tile(pid, x, w):
    return (x * lax.rsqrt(jnp.mean(x * x, axis=-1, keepdims=True) + RMS_EPS) * w,)


def _shift_rows(h, d, fill):
    pad = jnp.full((d, h.shape[1]), fill, f32)
    return jnp.concatenate([pad, h[:-d]], axis=0)


def _s5_prep_tile(pid, a_re, a_im, ls, btr, bti, ctr, cti):
    o = pid[0]
    w = a_re.shape[1]
    r = lax.broadcasted_iota(jnp.int32, (S5_GROUPS, w), 0)
    c = lax.broadcasted_iota(jnp.int32, (S5_GROUPS, w), 1)
    sel = (r == o * (w // S5_STATE) + c // S5_STATE).astype(f32)
    step = jnp.dot(jnp.exp(ls), sel, precision=lax.Precision.HIGHEST, preferred_element_type=f32)
    mag = jnp.exp(a_re * step)
    ang = a_im * step
    lr, li = mag * jnp.cos(ang), mag * jnp.sin(ang)
    nr, ni = lr - 1.0, li
    den = a_re * a_re + a_im * a_im
    fr = (nr * a_re + ni * a_im) / den
    fi = (ni * a_re - nr * a_im) / den
    bbr = fr * btr - fi * bti
    bbi = fr * bti + fi * btr
    reps = w // S5_STATE
    rr = lax.broadcasted_iota(jnp.int32, (reps * S5_GROUP, w), 0)
    cc = lax.broadcasted_iota(jnp.int32, (reps * S5_GROUP, w), 1)
    diag = (rr // S5_GROUP) == (cc // S5_STATE)

    def expand(m):
        return jnp.where(diag, jnp.concatenate([m] * reps, axis=0), 0.0)

    pr, pi = lr, li
    rows_r, rows_i = [pr], [pi]
    for _ in range(S5_ND - 1):
        pr, pi = pr * pr - pi * pi, 2.0 * pr * pi
        rows_r.append(pr)
        rows_i.append(pi)
    lamd_r, lamd_i = jnp.concatenate(rows_r, axis=0), jnp.concatenate(rows_i, axis=0)
    tr = jnp.broadcast_to(lr, (S5_SUB, w))
    ti = jnp.broadcast_to(li, (S5_SUB, w))
    for j in range(S5_ND):
        sr, si = _shift_rows(tr, 1 << j, 1.0), _shift_rows(ti, 1 << j, 0.0)
        tr, ti = tr * sr - ti * si, tr * si + ti * sr
    return lamd_r, lamd_i, tr, ti, expand(bbr), expand(bbi), expand(ctr), expand(cti)


def _s5_tile(pid, carry, u, lamd_r, lamd_i, lam8_r, lam8_i, bbr, bbi, ccr, cci, dvec):
    cr, ci = carry
    hr = _dot(u, bbr)
    hi = _dot(u, bbi)
    T = hr.shape[0]
    sub = lax.broadcasted_iota(jnp.int32, (T, 1), 0) % S5_SUB
    for j in range(S5_ND):
        keep = sub >= (1 << j)
        sr = jnp.where(keep, _shift_rows(hr, 1 << j, 0.0), 0.0)
        si = jnp.where(keep, _shift_rows(hi, 1 << j, 0.0), 0.0)
        ar, ai = lamd_r[j:j + 1], lamd_i[j:j + 1]
        hr, hi = hr + ar * sr - ai * si, hi + ar * si + ai * sr
    rows_r, rows_i = [], []
    for i in range(T // S5_SUB):
        gr_, gi_ = hr[i * S5_SUB:(i + 1) * S5_SUB], hi[i * S5_SUB:(i + 1) * S5_SUB]
        gr_, gi_ = gr_ + lam8_r * cr - lam8_i * ci, gi_ + lam8_r * ci + lam8_i * cr
        cr, ci = gr_[S5_SUB - 1:], gi_[S5_SUB - 1:]
        rows_r.append(gr_)
        rows_i.append(gi_)
    hr, hi = jnp.concatenate(rows_r, axis=0), jnp.concatenate(rows_i, axis=0)
    y = _dot(hr, ccr, _NT) - _dot(hi, cci, _NT) + dvec * u
    return (jax.nn.gelu(y),), (cr, ci)


def _glu_tile(pid, g, glu, za, b):
    return (g * jax.nn.sigmoid(glu + b) * jax.nn.silu(za),)


def _attn_tile(pid, carry, q, k, v, qw, kw):
    n = pid[1]
    kp, vp = carry
    D, B = ATT_HEAD_DIM, ATT_BLOCK
    W = 2 * D
    nq, ncol = q.shape[0] // B, q.shape[1] // W
    r = lax.broadcasted_iota(jnp.int32, (B, 2 * B), 0)
    c = lax.broadcasted_iota(jnp.int32, (B, 2 * B), 1)
    diff = r + B - c
    band = (diff >= 0) & (diff <= B)
    band_first = band & ((c >= B) | (n > 0))
    low = lax.broadcasted_iota(jnp.int32, (1, W), 1) < D
    same_head = (lax.broadcasted_iota(jnp.int32, (W, W), 0) // D == lax.broadcasted_iota(jnp.int32, (W, W), 1) // D).astype(bf16)

    def hnorm(x, w):
        rows = x.shape[0]
        t = jnp.concatenate([x[:, j * W:(j + 1) * W] for j in range(ncol)], axis=0) if ncol > 1 else x
        ms = _dot_exact01(t * t, same_head) * (1.0 / D)
        t = t * lax.rsqrt(ms + RMS_EPS) * jnp.concatenate([w, w], axis=1)
        return [t[j * rows:(j + 1) * rows] for j in range(ncol)]

    qns, kns = hnorm(q, qw), hnorm(k, kw)
    out_cols, lse_cols, kn_cols = [], [], []
    for j in range(ncol):
        sl = slice(j * W, (j + 1) * W)
        qn, kn, vj = qns[j], kns[j], v[:, sl]
        kn_cols.append(kn[(nq - 1) * B:])
        outs, lses = [], []
        for b in range(nq):
            rows = slice(b * B, (b + 1) * B)
            prev = slice((b - 1) * B, b * B)
            kk = jnp.concatenate([kp[:, sl] if b == 0 else kn[prev], kn[rows]], axis=0)
            vv = jnp.concatenate([vp[:, sl] if b == 0 else vj[prev], vj[rows]], axis=0)
            o2, l2 = [], []
            for head_lanes in (low, ~low):
                s = _dot(jnp.where(head_lanes, qn[rows], 0.0), kk, _NT) * (D ** -0.5)
                s = jnp.where(band_first if b == 0 else band, s, -1e30)
                m = jnp.max(s, axis=-1, keepdims=True)
                p = jnp.exp(s - m)
                l = jnp.sum(p, axis=-1, keepdims=True)
                o2.append(_dot(p / l, vv))
                l2.append(m + jnp.log(l))
            outs.append(jnp.where(low, o2[0], o2[1]))
            lses.append(jnp.where(low, l2[0], l2[1]))
        out_cols.append(jnp.concatenate(outs, axis=0) if nq > 1 else outs[0])
        lse_cols.append(jnp.concatenate(lses, axis=0) if nq > 1 else lses[0])
    return ((jnp.concatenate(out_cols, axis=1), jnp.concatenate(lse_cols, axis=1)),
            (jnp.concatenate(kn_cols, axis=1), v[(nq - 1) * B:]))


def _combine_tile(pid, o1, l1, o2, l2, o3, l3, zb):
    m = jnp.maximum(jnp.maximum(l1, l2), l3)
    e1, e2, e3 = jnp.exp(l1 - m), jnp.exp(l2 - m), jnp.exp(l3 - m)
    y = (e1 * o1 + e2 * o2 + e3 * o3) / (e1 + e2 + e3)
    return (y * jax.nn.silu(zb),)


def _softplus(x):
    return jnp.maximum(x, 0.0) + jnp.log(1.0 + jnp.exp(-jnp.abs(x)))


def _ssd_tile(pid, carry, xbc, dt, z, conv_w, conv_b, dt_bias, a_log, dvec, norm_w):
    xprev, state = carry
    T, P, N = SSD_CHUNK, SSD_HEAD_DIM, SSD_STATE
    xx = jnp.concatenate([xprev, xbc], axis=0)
    conv = conv_b
    for k in range(SSD_CONV):
        off = 8 - (SSD_CONV - 1) + k
        conv = conv + conv_w[k:k + 1] * xx[off:off + T]
    xc = jax.nn.silu(conv)
    dtp = _softplus(dt + dt_bias)
    a_dt = dtp * (-jnp.exp(a_log))
    r = lax.broadcasted_iota(jnp.int32, (T, T), 0)
    c = lax.broadcasted_iota(jnp.int32, (T, T), 1)
    tri = r >= c
    trif = tri.astype(f32)
    hi = lax.Precision.HIGHEST
    a_cs = jnp.dot(trif, a_dt, precision=hi, preferred_element_type=f32)
    a_cs_t = lax.dot_general(a_dt, trif, (((0,), (1,)), ((), ())), precision=hi, preferred_element_type=f32)
    xs = xc[:, :SSD_WIDTH]
    acs_p = _spread_heads(a_cs, P)
    acs_t = _spread_heads(a_cs, T)
    xdt = xs * _spread_heads(dtp, P)
    skip = _spread_heads(dvec, P)
    to_end = jnp.exp(acs_p[T - 1:T] - acs_p)
    low = lax.broadcasted_iota(jnp.int32, (1, 2 * P), 1) < P
    low_rows = lax.broadcasted_iota(jnp.int32, (2 * P, 1), 0) < P
    ys, states = [], []
    for j in range(SSD_HEADS // 2):
        g = 2 * j // (SSD_HEADS // SSD_GROUPS)
        if 2 * j % (SSD_HEADS // SSD_GROUPS) == 0:
            bg = xc[:, SSD_WIDTH + g * N:SSD_WIDTH + (g + 1) * N]
            cg = xc[:, SSD_WIDTH + SSD_GROUPS * N + g * N:SSD_WIDTH + SSD_GROUPS * N + (g + 1) * N]
            cb = _dot(cg, bg, _NT)
        lanes = slice(2 * j * P, 2 * (j + 1) * P)
        st = state[lanes, :]
        diag, last = [], []
        for h in (2 * j, 2 * j + 1):
            decay = jnp.exp(jnp.where(tri, acs_t[:, h * T:(h + 1) * T] - a_cs_t[h:h + 1, :], -1e30))
            diag.append(_dot(cb * decay, xdt[:, lanes]))
            last.append(jnp.exp(a_cs_t[h:h + 1, T - 1:T]))
        y = (jnp.where(low, diag[0], diag[1]) + _dot(cg, st, _NT) * jnp.exp(acs_p[:, lanes])
             + xs[:, lanes] * skip[:, lanes])
        ys.append(y)
        states.append(jnp.where(low_rows, last[0], last[1]) * st + _dot(xdt[:, lanes] * to_end[:, lanes], bg, _TN))
    y = jnp.concatenate(ys, axis=1) * jax.nn.silu(z)
    out = y * lax.rsqrt(jnp.mean(y * y, axis=-1, keepdims=True) + RMS_EPS) * norm_w
    return (out,), (xbc[T - 8:], jnp.concatenate(states, axis=0))


def _merge_tile(pid, pa, pb, pc, g0, g1, g2):
    return (jax.nn.sigmoid(g0) * pa + jax.nn.sigmoid(g1) * pb + jax.nn.sigmoid(g2) * pc,)


def loss_and_grad(y, target, tm=512):
    S, D = y.shape
    nt = S // tm

    def body(y_ref, t_ref, dy_ref, l_ref, acc):
        i = pl.program_id(0)

        @pl.when(i == 0)
        def _():
            acc[...] = jnp.zeros_like(acc)

        diff = y_ref[...] - t_ref[...]
        dy_ref[...] = diff * (1.0 / D)
        acc[...] += jnp.sum((diff * diff).reshape(tm // 8, 8, D), axis=0)

        @pl.when(i == nt - 1)
        def _():
            l_ref[...] = jnp.broadcast_to(0.5 / D * jnp.sum(acc[...]), l_ref.shape)

    dy, l = pl.pallas_call(
        body, name="loss_head", grid=(nt,),
        in_specs=[pl.BlockSpec((tm, D), lambda i: (i, 0))] * 2,
        out_specs=[pl.BlockSpec((tm, D), lambda i: (i, 0)), pl.BlockSpec((8, 128), lambda i: (0, 0))],
        out_shape=[jax.ShapeDtypeStruct((S, D), f32), jax.ShapeDtypeStruct((8, 128), f32)],
        scratch_shapes=[pltpu.VMEM((8, D), f32)],
        compiler_params=_cparams(1))(y, target)
    return dy, l[0, 0]


def _row_tile(R, C, budget=1 << 20):
    best = R
    for t in range(8, R, 8):
        if R % t == 0 and t * C * 4 <= budget:
            best = t
    if best == R and R * C * 4 > budget:
        for t in range(8, R, 8):
            if R % t == 0:
                return t
    return best


def _as2d(t, lead=0):
    return t.reshape(t.shape[:lead] + (math.prod(t.shape[lead:-1]), t.shape[-1]))


def adamw(name, w, gslots, m, v):
    shape = w.shape
    n = gslots.shape[0]
    C = shape[-1]
    R = math.prod(shape[:-1])
    lanes = -(-C // 128) * 128
    tr = _row_tile(R, lanes * (n + 7), budget=10 << 20)

    def body(w_ref, g_ref, m_ref, v_ref, go_ref, d_ref, nm_ref, nv_ref):
        gg = g_ref[0].astype(f32)
        for s in range(1, n):
            gg = gg + g_ref[s].astype(f32)
        go_ref[...] = gg
        nm = ADAM_B1 * m_ref[...] + (1.0 - ADAM_B1) * gg
        nv = ADAM_B2 * v_ref[...] + (1.0 - ADAM_B2) * jnp.square(gg)
        m_hat = nm / (1.0 - ADAM_B1 ** ADAM_STEP)
        v_hat = nv / (1.0 - ADAM_B2 ** ADAM_STEP)
        d_ref[...] = -ADAM_LR * (m_hat / (jnp.sqrt(v_hat) + ADAM_EPS) + ADAM_WD * w_ref[...])
        nm_ref[...] = nm
        nv_ref[...] = nv

    spec = pl.BlockSpec((tr, C), lambda i: (i, 0))
    res = pl.pallas_call(
        body, name=name, grid=(R // tr,),
        in_specs=[spec, pl.BlockSpec((n, tr, C), lambda i: (0, i, 0)), spec, spec], out_specs=[spec] * 4,
        out_shape=[jax.ShapeDtypeStruct((R, C), f32)] * 4,
        compiler_params=_cparams(1))(w.reshape(R, C), gslots.reshape(n, R, C), m.reshape(R, C), v.reshape(R, C))
    return tuple(t.reshape(shape) for t in res)


PACK_ROWS = 256


def sum_slots(name, x):
    n, R, C = x.shape

    def body(x_ref, o_ref):
        acc = x_ref[0]
        for s in range(1, n):
            acc = acc + x_ref[s]
        o_ref[...] = acc

    return pl.pallas_call(
        body, name=name, grid=(R // PACK_ROWS,),
        in_specs=[pl.BlockSpec((n, PACK_ROWS, C), lambda i: (0, i, 0))],
        out_specs=pl.BlockSpec((PACK_ROWS, C), lambda i: (i, 0)),
        out_shape=jax.ShapeDtypeStruct((R, C), f32), compiler_params=_cparams(1))(x)


def _pack(parts):
    flat = jnp.concatenate([p.reshape(-1) for p in parts])
    unit = 128 * PACK_ROWS
    tot = -(-flat.shape[0] // unit) * unit
    return jnp.pad(flat, (0, tot - flat.shape[0])).reshape(tot // 128, 128)


def _unpack(buf, shapes):
    flat = buf.reshape(-1)
    out, off = [], 0
    for s in shapes:
        size = math.prod(s)
        out.append(flat[off:off + size].reshape(s))
        off += size
    return out


def exchange(name, srcs, modes):
    nt = len(srcs)
    slabs = []
    for s, mode in zip(srcs, modes):
        R, C = s.shape
        slabs.append({'all': (R, C), 'rows': (R // N_DEV, C), 'cols': (R, C // N_DEV)}[mode])

    def piece(ref, mode, slab, p):
        if mode == 'all':
            return ref
        if mode == 'rows':
            return ref.at[pl.ds(p * slab[0], slab[0]), :]
        return ref.at[:, pl.ds(p * slab[1], slab[1])]

    def body(*refs):
        src_refs, out_refs = refs[:nt], refs[nt:2 * nt]
        send_sems, recv_sems, local_sems = refs[2 * nt:]
        x, y, c = lax.axis_index("x"), lax.axis_index("y"), lax.axis_index("c")
        me = 4 * x + 2 * y + c
        copies = []
        for k in (1, 2, 4, 3, 5, 6, 7):
            px = 1 - x if k & 4 else x
            py = 1 - y if k & 2 else y
            pc = 1 - c if k & 1 else c
            for t in range(nt):
                cp = pltpu.make_async_remote_copy(
                    src_ref=piece(src_refs[t], modes[t], slabs[t], 4 * px + 2 * py + pc), dst_ref=out_refs[t].at[me],
                    send_sem=send_sems.at[t, k - 1], recv_sem=recv_sems.at[t, k - 1],
                    device_id=(px, py, pc), device_id_type=pl.DeviceIdType.MESH)
                cp.start()
                copies.append(cp)
        for t in range(nt):
            own = pltpu.make_async_copy(piece(src_refs[t], modes[t], slabs[t], me), out_refs[t].at[me], local_sems.at[t])
            own.start()
            copies.append(own)
        for cp in copies:
            cp.wait()

    res = pl.pallas_call(
        body, name=name,
        in_specs=[pl.BlockSpec(memory_space=pl.ANY)] * nt,
        out_specs=[pl.BlockSpec(memory_space=pl.ANY)] * nt,
        out_shape=[jax.ShapeDtypeStruct((N_DEV,) + sl, s.dtype) for s, sl in zip(srcs, slabs)],
        scratch_shapes=[pltpu.SemaphoreType.DMA((nt, N_DEV - 1)), pltpu.SemaphoreType.DMA((nt, N_DEV - 1)), pltpu.SemaphoreType.DMA((nt,))],
    )(*srcs)
    return list(res)


def gather_two_level(name, srcs):
    nt = len(srcs)

    def body(*refs):
        src_refs, out_refs = refs[:nt], refs[nt:2 * nt]
        send_sems, recv_sems, local_sems = refs[2 * nt:]
        x, y, c = lax.axis_index("x"), lax.axis_index("y"), lax.axis_index("c")
        me, sibling = (x, y, c), (x, y, 1 - c)
        chips = [(1 - x, y), (x, 1 - y), (1 - x, 1 - y)]

        def slot(t, dev):
            return out_refs[t].at[4 * dev[0] + 2 * dev[1] + dev[2]]

        def copy(t, k, block, to, src=None):
            return pltpu.make_async_remote_copy(
                src_ref=slot(t, block) if src is None else src, dst_ref=slot(t, block),
                send_sem=send_sems.at[t, k], recv_sem=recv_sems.at[t, k], device_id=to, device_id_type=pl.DeviceIdType.MESH)

        mine = [pltpu.make_async_copy(src_refs[t], slot(t, me), local_sems.at[t]) for t in range(nt)]
        for cp in mine:
            cp.start()
        sent = []
        for t in range(nt):
            sent.append(copy(t, 0, me, sibling, src=src_refs[t]))
            sent += [copy(t, 1 + j, me, (*chip, c), src=src_refs[t]) for j, chip in enumerate(chips)]
        for cp in sent:
            cp.start()
        for j, chip in enumerate(chips):
            for t in range(nt):
                copy(t, 1 + j, (*chip, c), me).wait_recv()
                fwd = copy(t, 4 + j, (*chip, c), sibling)
                fwd.start()
                sent.append(fwd)
        for t in range(nt):
            copy(t, 0, sibling, me).wait_recv()
            for j, chip in enumerate(chips):
                copy(t, 4 + j, (*chip, 1 - c), me).wait_recv()
        for cp in sent:
            cp.wait_send()
        for cp in mine:
            cp.wait()

    res = pl.pallas_call(
        body, name=name,
        in_specs=[pl.BlockSpec(memory_space=pl.ANY)] * nt,
        out_specs=[pl.BlockSpec(memory_space=pl.ANY)] * nt,
        out_shape=[jax.ShapeDtypeStruct((N_DEV,) + s.shape, s.dtype) for s in srcs],
        scratch_shapes=[pltpu.SemaphoreType.DMA((nt, N_DEV - 1)), pltpu.SemaphoreType.DMA((nt, N_DEV - 1)), pltpu.SemaphoreType.DMA((nt,))],
    )(*srcs)
    return list(res)


def _relayout_w_in(w):
    offs = [0]
    for s in IN_SPLITS:
        offs.append(offs[-1] + s)
    p = [w[:, offs[i]:offs[i + 1]] for i in range(len(IN_SPLITS))]
    ua, za, q, k, v, zb, xbc, dt, zc, gates = p
    dtp = jnp.pad(dt, ((0, 0), (0, HPAD - dt.shape[1])))
    return (jnp.concatenate([ua, za, dtp], 1), w[:, offs[2]:offs[5]], jnp.concatenate([xbc, zb, zc], 1), gates)


def _pad_lanes(v, n=HPAD):
    return jnp.pad(v.reshape(1, -1), ((0, 0), (0, n - v.shape[-1])))


def _s5_prep_args(W):
    g2 = S5_GROUPS * S5_STATE
    w = g2 // S5_CHUNKS
    a_re, a_im = W['s5_a_re'].reshape(1, g2), W['s5_a_im'].reshape(1, g2)
    ls = W['s5_log_step'].reshape(1, S5_GROUPS)
    btr, bti = W['s5_b_re'].reshape(g2, S5_GROUP).T, W['s5_b_im'].reshape(g2, S5_GROUP).T
    ctr = W['s5_c_re'].transpose(1, 0, 2).reshape(S5_GROUP, g2)
    cti = W['s5_c_im'].transpose(1, 0, 2).reshape(S5_GROUP, g2)
    col = lambda a, rows: Arg(a, (rows, w), lambda o: (0, o), 'tile')
    return [col(a_re, 1), col(a_im, 1), _whole(ls, 'acc'), col(btr, S5_GROUP), col(bti, S5_GROUP), col(ctr, S5_GROUP), col(cti, S5_GROUP)]


def _s5_prep_outs():
    g2 = S5_GROUPS * S5_STATE
    w = g2 // S5_CHUNKS
    rows = (S5_ND, S5_ND, S5_SUB, S5_SUB, 128, 128, 128, 128)
    return [Out((r, g2), f32, (r, w), lambda o: (0, o)) for r in rows]


def _s5_args(A, prep, dvec, S):
    w = S5_GROUPS * S5_STATE // S5_CHUNKS
    args = [Arg(A, (S5_TILE, 128), lambda o, t: (t, o), 'tile', (S, S5_WIDTH), None, bf16)]
    for p in prep:
        args.append(Arg(p, (p.shape[0], w), lambda o, t: (0, o), 'acc0'))
    args.append(Arg(dvec, (1, 128), lambda o, t: (0, o), 'acc0'))
    return args


def _attn_args(QKV, g, r, qw, kw, S):
    L = S // r
    nq, rb = _attn_plan(r)
    block = (nq * ATT_BLOCK, rb * ATT_GW)
    gshape = (L, r * ATT_GW)
    gimap = lambda rho, n: (n, rho)
    if r == 1:
        mk = lambda j: Arg(QKV, block, lambda rho, n, j=j: (n, j), 'tile', gshape, gimap, bf16)
    else:
        def mk(j):
            view = QKV[:, j * ATT_GW:(j + 1) * ATT_GW].reshape(L, r * ATT_GW)
            return Arg(view, block, gimap, 'tile', None, None, bf16)
    return [mk(g), mk(3 + g), mk(6 + g), _whole(qw, 'acc'), _whole(kw, 'acc')]


def _attn_plan(r):
    return (4, 1) if r == 1 else (1, min(r, 4))


def _attn_grid(r, S):
    nq, rb = _attn_plan(r)
    return (r // rb, S // r // ATT_BLOCK // nq)


def _attn_carry(r):
    return ((ATT_BLOCK, _attn_plan(r)[1] * ATT_GW),) * 2


def _ssd_args(C, A, W, S):
    T = SSD_CHUNK
    return [Arg(C, (T, SSD_CONV_DIM), lambda o, t: (t, 0), 'tile', (S, SSD_CONV_DIM), None, bf16),
            Arg(A, (T, HPAD), lambda o, t: (t, 2 * S5_WIDTH // HPAD), 'tile', (S, HPAD), lambda o, t: (t, 0), bf16),
            Arg(C, (T, SSD_WIDTH), lambda o, t: (t, 2), 'tile', (S, SSD_WIDTH), lambda o, t: (t, 0), bf16),
            _whole(W['conv_w'], 'acc'), _whole(W['conv_b'].reshape(1, -1), 'acc'),
            _whole(_pad_lanes(W['dt_bias']), 'acc'), _whole(_pad_lanes(W['ssd_a_log']), 'acc'),
            _whole(_pad_lanes(W['ssd_d']), 'acc'), _whole(W['ssd_norm_w'].reshape(1, -1), 'acc')]


_SSD_CARRY = ((8, SSD_CONV_DIM), (SSD_WIDTH, SSD_STATE))
_S5_CARRY = ((1, 512), (1, 512))


def layer_fwd(li, x, W):
    S = x.shape[0]
    n = lambda s: f"l{li}_{s}"
    sv = {'x': x}
    (h,) = map_fwd(n("norm"), _rmsnorm_tile, (S // 512,), [_rows(x, 512), _whole(W['norm_w'].reshape(1, -1))],
                   [Out((S, D_MODEL), bf16, (512, D_MODEL), lambda i: (i, 0))])
    wA, wQ, wC, wG = W['w_in_pieces']
    A = matmul(n("in_a"), h, wA)
    QKV = matmul(n("in_qkv"), h, wQ)
    C = matmul(n("in_c"), h, wC)
    G = matmul(n("in_g"), h, wG)
    sv.update(h=h, A=A, QKV=QKV, C=C, G=G)

    prep = map_fwd(n("s5_prep"), _s5_prep_tile, (S5_CHUNKS,), _s5_prep_args(W), _s5_prep_outs())
    dvec = W['s5_d'].reshape(1, -1)
    (g,), s5_ck = scan_fwd(n("s5_scan"), _s5_tile, (S5_CHUNKS, S // S5_TILE), _S5_CARRY, _s5_args(A, prep, dvec, S),
                           [Out((S, S5_WIDTH), f32, (S5_TILE, 128), lambda o, t: (t, o))])
    glu = matmul(n("glu"), g, W['s5_glu_w'])
    glu_b = W['s5_glu_b'].reshape(1, -1)
    (ya,) = map_fwd(n("glu_gate"), _glu_tile, (S // 512,),
                    [_rows(g, 512), _rows(glu, 512), _rows(A, 512, col=1, width=S5_WIDTH), _whole(glu_b)],
                    [Out((S, S5_WIDTH), bf16, (512, S5_WIDTH), lambda i: (i, 0))])
    sv.update(prep=prep, g=g, glu=glu, ya=ya, s5_ck=s5_ck)

    qw, kw = W['q_norm_w'].reshape(1, -1), W['k_norm_w'].reshape(1, -1)
    att, att_ck = [], []
    for gi, (window, r) in enumerate(ATT_PAIRS):
        assert window // r == ATT_BLOCK and S % (r * ATT_BLOCK) == 0
        L = S // r
        nq, rb = _attn_plan(r)
        assert S // r // ATT_BLOCK % nq == 0
        spec = Out((L, r * ATT_GW), f32, (nq * ATT_BLOCK, rb * ATT_GW), lambda rho, nb: (nb, rho))
        (o, lse), ck = scan_fwd(n(f"attn{gi}"), _attn_tile, _attn_grid(r, S), _attn_carry(r), _attn_args(QKV, gi, r, qw, kw, S), [spec, spec])
        att += [o.reshape(S, ATT_GW), lse.reshape(S, ATT_GW)]
        att_ck.append(ck)
    (yb,) = map_fwd(n("combine"), _combine_tile, (S // 512,),
                    [_rows(t, 512) for t in att] + [_rows(C, 512, col=SSD_CONV_DIM // ATT_GW, width=ATT_GW)],
                    [Out((S, ATT_GW), bf16, (512, ATT_GW), lambda i: (i, 0))])
    sv.update(att=att, att_ck=att_ck, yb=yb)

    (yc,), ssd_ck = scan_fwd(n("ssd"), _ssd_tile, (1, S // SSD_CHUNK), _SSD_CARRY, _ssd_args(C, A, W, S),
                             [Out((S, SSD_WIDTH), bf16, (SSD_CHUNK, SSD_WIDTH), lambda o, t: (t, 0))])
    sv.update(yc=yc, ssd_ck=ssd_ck)

    pa = matmul(n("proj_a"), ya, W['proj_a'])
    pb = matmul(n("proj_b"), yb, W['proj_b'])
    pc = matmul(n("proj_c"), yc, W['proj_c'])
    (merged,) = map_fwd(n("merge"), _merge_tile, (S // 256,),
                        [_rows(pa, 256), _rows(pb, 256), _rows(pc, 256)] + [_rows(G, 256, col=j, width=D_MODEL) for j in range(3)],
                        [Out((S, D_MODEL), bf16, (256, D_MODEL), lambda i: (i, 0))])
    out = matmul(n("w_out"), merged, W['w_out'], add=x)
    sv.update(pa=pa, pb=pb, pc=pc, merged=merged)
    return out, sv


def layer_bwd(li, dout, sv, W):
    S = dout.shape[0]
    n = lambda s: f"l{li}_{s}"
    gr = {}
    x, A, QKV, C, G = sv['x'], sv['A'], sv['QKV'], sv['C'], sv['G']

    dmerged = matmul(n("d_merged"), dout, W['w_out'], 'nt')
    gr['w_out'] = wgrad(n("g_w_out"), sv['merged'], dout)
    margs = [_rows(sv['pa'], 256, gdtype=bf16), _rows(sv['pb'], 256, gdtype=bf16), _rows(sv['pc'], 256, gdtype=bf16)] + \
            [_rows(G, 256, col=j, width=D_MODEL, gshape=(S, D_MODEL), gdtype=bf16) for j in range(3)]
    dpa, dpb, dpc, dg0, dg1, dg2 = map_bwd(n("merge_bwd"), _merge_tile, (S // 256,), margs, [_rows(dmerged, 256)], list(range(6)))
    dya = matmul(n("d_ya"), dpa, W['proj_a'], 'nt')
    dyb = matmul(n("d_yb"), dpb, W['proj_b'], 'nt')
    dyc = matmul(n("d_yc"), dpc, W['proj_c'], 'nt')
    gr['proj_a'] = wgrad(n("g_proj_a"), sv['ya'], dpa)
    gr['proj_b'] = wgrad(n("g_proj_b"), sv['yb'], dpb)
    gr['proj_c'] = wgrad(n("g_proj_c"), sv['yc'], dpc)

    glu_b = W['s5_glu_b'].reshape(1, -1)
    gargs = [_rows(sv['g'], 512), _rows(sv['glu'], 512, gdtype=bf16),
             _rows(A, 512, col=1, width=S5_WIDTH, gshape=(S, S5_WIDTH), gdtype=bf16), _whole(glu_b, 'acc')]
    dg_a, dglu, dza, dglu_b = map_bwd(n("glu_gate_bwd"), _glu_tile, (S // 512,), gargs, [_rows(dya, 512)], [0, 1, 2, 3])
    gr['s5_glu_b'] = dglu_b.reshape(-1)
    dg = matmul(n("d_g"), dglu, W['s5_glu_w'], 'nt', add=dg_a)
    gr['s5_glu_w'] = wgrad(n("g_glu_w"), sv['g'], dglu)
    dvec = W['s5_d'].reshape(1, -1)
    sargs = _s5_args(A, sv['prep'], dvec, S)
    res = scan_bwd(n("s5_scan_bwd"), _s5_tile, (S5_CHUNKS, S // S5_TILE), _S5_CARRY, sargs, sv['s5_ck'],
                   [Arg(dg, (S5_TILE, 128), lambda o, t: (t, o))], list(range(len(sargs))))
    dua, dprep, dd = res[0], res[1:9], res[9]
    gr['s5_d'] = dd.reshape(-1)
    pargs = _s5_prep_args(W)
    pouts = _s5_prep_outs()
    da_re, da_im, dls, dbtr, dbti, dctr, dcti = map_bwd(
        n("s5_prep_bwd"), _s5_prep_tile, (S5_CHUNKS,), pargs,
        [Arg(d, o.block, o.imap) for d, o in zip(dprep, pouts)], list(range(7)))
    gshape = (S5_GROUPS, S5_STATE)
    gr['s5_a_re'], gr['s5_a_im'] = da_re.reshape(gshape), da_im.reshape(gshape)
    gr['s5_log_step'] = dls.reshape(-1)
    gr['s5_b_re'] = dbtr.T.reshape(S5_GROUPS, S5_STATE, S5_GROUP)
    gr['s5_b_im'] = dbti.T.reshape(S5_GROUPS, S5_STATE, S5_GROUP)
    gr['s5_c_re'] = dctr.reshape(S5_GROUP, S5_GROUPS, S5_STATE).transpose(1, 0, 2)
    gr['s5_c_im'] = dcti.reshape(S5_GROUP, S5_GROUPS, S5_STATE).transpose(1, 0, 2)

    cargs = [_rows(t, 512) for t in sv['att']] + \
            [_rows(C, 512, col=SSD_CONV_DIM // ATT_GW, width=ATT_GW, gshape=(S, ATT_GW), gdtype=bf16)]
    cres = map_bwd(n("combine_bwd"), _combine_tile, (S // 512,), cargs, [_rows(dyb, 512)], list(range(7)))
    dzb = cres[6]
    qw, kw = W['q_norm_w'].reshape(1, -1), W['k_norm_w'].reshape(1, -1)
    dqs, dks, dvs = [], [], []
    dqw = dkw = None
    for gi, (window, r) in enumerate(ATT_PAIRS):
        L = S // r
        nq, rb = _attn_plan(r)
        dspec = lambda t: Arg(t.reshape(L, r * ATT_GW), (nq * ATT_BLOCK, rb * ATT_GW), lambda rho, nb: (nb, rho))
        dq, dk, dv, dqw_g, dkw_g = scan_bwd(n(f"attn{gi}_bwd"), _attn_tile, _attn_grid(r, S), _attn_carry(r),
                                            _attn_args(QKV, gi, r, qw, kw, S), sv['att_ck'][gi],
                                            [dspec(cres[2 * gi]), dspec(cres[2 * gi + 1])], [0, 1, 2, 3, 4])
        dqs.append(dq.reshape(S, ATT_GW))
        dks.append(dk.reshape(S, ATT_GW))
        dvs.append(dv.reshape(S, ATT_GW))
        dqw = dqw_g if dqw is None else dqw + dqw_g
        dkw = dkw_g if dkw is None else dkw + dkw_g
    gr['q_norm_w'], gr['k_norm_w'] = dqw.reshape(-1), dkw.reshape(-1)

    ssd_args = _ssd_args(C, A, W, S)
    sres = scan_bwd(n("ssd_bwd"), _ssd_tile, (1, S // SSD_CHUNK), _SSD_CARRY, ssd_args, sv['ssd_ck'],
                    [Arg(dyc, (SSD_CHUNK, SSD_WIDTH), lambda o, t: (t, 0))], list(range(9)))
    dxbc, ddt, dzc = sres[0], sres[1], sres[2]
    gr['conv_w'] = sres[3]
    gr['conv_b'] = sres[4].reshape(-1)
    gr['dt_bias'] = sres[5].reshape(-1)[:SSD_HEADS]
    gr['ssd_a_log'] = sres[6].reshape(-1)[:SSD_HEADS]
    gr['ssd_d'] = sres[7].reshape(-1)[:SSD_HEADS]
    gr['ssd_norm_w'] = sres[8].reshape(-1)

    dpieces = [jnp.concatenate([dua, dza, ddt], axis=1), jnp.concatenate(dqs + dks + dvs, axis=1),
               jnp.concatenate([dxbc, dzb, dzc], axis=1), jnp.concatenate([dg0, dg1, dg2], axis=1)]
    dh, gw = None, []
    for j, (dp, wp) in enumerate(zip(dpieces, W['w_in_pieces'])):
        dh = matmul(n(f"d_h{j}"), dp, wp, 'nt', add=dh)
        gw.append(wgrad(n(f"g_w_in{j}"), sv['h'], dp))
    gr['w_in'] = _unrelayout_w_in_grad(gw)
    nargs = [_rows(x, 512), _whole(W['norm_w'].reshape(1, -1), 'acc')]
    dx, dnw = map_bwd(n("norm_bwd"), _rmsnorm_tile, (S // 512,), nargs, [_rows(dh, 512)], [0, 1], add={0: _rows(dout, 512)})
    gr['norm_w'] = dnw.reshape(-1)
    return dx, gr


def _unrelayout_w_in_grad(pieces):
    gA, gQ, gC, gG = pieces
    uaza, dt = gA[:, :2 * S5_WIDTH], gA[:, 2 * S5_WIDTH:2 * S5_WIDTH + SSD_HEADS]
    xbc, zb, zc = gC[:, :SSD_CONV_DIM], gC[:, SSD_CONV_DIM:SSD_CONV_DIM + ATT_GW], gC[:, SSD_CONV_DIM + ATT_GW:]
    return jnp.concatenate([uaza, gQ, zb, xbc, dt, zc, gG], axis=1)


def kernel(x, norm_w, w_in, s5_a_re, s5_a_im, s5_log_step, s5_b_re, s5_b_im, s5_c_re, s5_c_im, s5_d, s5_glu_w, s5_glu_b, q_norm_w, k_norm_w, conv_w, conv_b, dt_bias, ssd_a_log, ssd_d, ssd_norm_w, proj_a, proj_b, proj_c, w_out, loss_target, m_norm_w, m_w_in, m_s5_a_re, m_s5_a_im, m_s5_log_step, m_s5_b_re, m_s5_b_im, m_s5_c_re, m_s5_c_im, m_s5_d, m_s5_glu_w, m_s5_glu_b, m_q_norm_w, m_k_norm_w, m_conv_w, m_conv_b, m_dt_bias, m_ssd_a_log, m_ssd_d, m_ssd_norm_w, m_proj_a, m_proj_b, m_proj_c, m_w_out, v_norm_w, v_w_in, v_s5_a_re, v_s5_a_im, v_s5_log_step, v_s5_b_re, v_s5_b_im, v_s5_c_re, v_s5_c_im, v_s5_d, v_s5_glu_w, v_s5_glu_b, v_q_norm_w, v_k_norm_w, v_conv_w, v_conv_b, v_dt_bias, v_ssd_a_log, v_ssd_d, v_ssd_norm_w, v_proj_a, v_proj_b, v_proj_c, v_w_out):
    args = dict(locals())
    w = {k: args[k] for k in WEIGHTS}
    m = {k: args['m_' + k] for k in WEIGHTS}
    v = {k: args['v_' + k] for k in WEIGHTS}
    depth = norm_w.shape[0]
    S = x.shape[1]
    xs = x.reshape(S, D_MODEL)
    tgt = loss_target.reshape(S, D_MODEL)

    keys = [(li, k) for li in range(depth) for k in SHARDED]
    gathered = gather_two_level("gather_weights", [w[k][li].astype(bf16) for li, k in keys])
    layers = [{k: w[k][li] for k in WEIGHTS if k not in SHARDED} for li in range(depth)]
    for (li, k), t in zip(keys, gathered):
        n_dev, R, C = t.shape
        layers[li][k] = t.reshape(n_dev * R, C) if k in ROW_SHARDED else t.transpose(1, 0, 2).reshape(R, n_dev * C)
    for W in layers:
        W['w_in_pieces'] = _relayout_w_in(W['w_in'])

    act, saved = xs, []
    for li in range(depth):
        act, sv = layer_fwd(li, act, layers[li])
        saved.append(sv)
    dy, loss_local = loss_and_grad(act, tgt)
    loss = lax.psum(loss_local, ("x", "y", "c"))

    grads = [None] * depth
    for li in reversed(range(depth)):
        dy, grads[li] = layer_bwd(li, dy, saved[li], layers[li])
    grad_x = dy.reshape(x.shape)

    big_keys = [(li, k) for li in range(depth) for k in SHARDED if k != 'conv_w']
    slots = exchange("scatter_grads", [grads[li][k] for li, k in big_keys],
                     ['rows' if k in ROW_SHARDED else 'cols' for _, k in big_keys])
    per_layer = {}
    for (li, k), s in zip(big_keys, slots):
        per_layer[li, k] = adamw(f"adamw_{k}{li}", w[k][li], s, m[k][li], v[k][li])
    result = {k: tuple(jnp.stack([per_layer[li, k][j] for li in range(depth)], axis=0) for j in range(4))
              for k in SHARDED if k != 'conv_w'}

    small_keys = [k for k in WEIGHTS if k not in SHARDED] + ['conv_w']
    stacked = [jnp.stack([grads[li][k] for li in range(depth)], axis=0) for k in small_keys]
    (small_slots,) = exchange("gather_small_grads", [_pack(stacked)], ['all'])
    totals = _unpack(sum_slots("sum_small_grads", small_slots), [t.shape for t in stacked])
    for k, g in zip(small_keys, totals):
        if k == 'conv_w':
            width = w[k].shape[-1]
            me = 4 * lax.axis_index("x") + 2 * lax.axis_index("y") + lax.axis_index("c")
            g = lax.dynamic_slice_in_dim(g, me * width, width, axis=2)
        result[k] = adamw("adamw_" + k, w[k], g[None], m[k], v[k])

    return (loss, grad_x, *[result[k][0] for k in WEIGHTS], *[result[k][1] for k in WEIGHTS],
            *[result[k][2] for k in WEIGHTS], *[result[k][3] for k in WEIGHTS])
```

```python
import functools
import math
from typing import Any, NamedTuple

import jax
import jax.numpy as jnp
from jax import lax
from jax.experimental import pallas as pl
from jax.experimental.pallas import tpu as pltpu

f32 = jnp.float32
bf16 = jnp.bfloat16

N_DEV = 8
D_MODEL = 1024
RMS_EPS = 1e-6
S5_WIDTH = 512
S5_GROUPS = 32
S5_GROUP = 16
S5_STATE = 64
S5_TILE = 256
S5_SUB = 8
S5_ND = 3
S5_CHUNKS = 4
ATT_HEAD_DIM = 64
ATT_PAIRS = ((128, 1), (512, 4), (2048, 16))
ATT_HPG = 4
ATT_BLOCK = 128
ATT_GW = ATT_HPG * ATT_HEAD_DIM
ATT_WIDTH = 768
SSD_HEADS = 12
SSD_HEAD_DIM = 64
SSD_WIDTH = 768
SSD_STATE = 128
SSD_GROUPS = 2
SSD_CHUNK = 128
SSD_CONV = 4
SSD_CONV_DIM = 1280
HPAD = 128
IN_SPLITS = (512, 512, 768, 768, 768, 256, 1280, 12, 768, 3072)
ADAM_LR, ADAM_B1, ADAM_B2, ADAM_EPS, ADAM_WD, ADAM_STEP = 0.001, 0.9, 0.999, 1e-08, 0.01, 10
VMEM_LIMIT = 56 * 1024 * 1024

WEIGHTS = ['norm_w', 'w_in', 's5_a_re', 's5_a_im', 's5_log_step', 's5_b_re', 's5_b_im', 's5_c_re',
           's5_c_im', 's5_d', 's5_glu_w', 's5_glu_b', 'q_norm_w', 'k_norm_w', 'conv_w', 'conv_b',
           'dt_bias', 'ssd_a_log', 'ssd_d', 'ssd_norm_w', 'proj_a', 'proj_b', 'proj_c', 'w_out']
ROW_SHARDED = ('w_in', 's5_glu_w', 'w_out')
SHARDED = ROW_SHARDED + ('conv_w', 'proj_a', 'proj_b', 'proj_c')


class Arg(NamedTuple):
    arr: Any
    block: tuple
    imap: Any
    kind: str = 'const'
    gshape: Any = None
    gimap: Any = None
    gdtype: Any = None


class Out(NamedTuple):
    shape: tuple
    dtype: Any
    block: tuple
    imap: Any


def _cparams(n):
    return pltpu.CompilerParams(dimension_semantics=("arbitrary",) * n, vmem_limit_bytes=VMEM_LIMIT)


def _rows(a, tm, kind='tile', col=0, width=None, gshape=None, gcol=None, gdtype=None):
    width = a.shape[1] if width is None else width
    g = None if gshape is None else (lambda i, gc=(0 if gcol is None else gcol): (i, gc))
    return Arg(a, (tm, width), lambda i, c=col: (i, c), kind, gshape, g, gdtype)


def _whole(a, kind='const'):
    nd = a.ndim
    return Arg(a, a.shape, lambda *i, nd=nd: (0,) * nd, kind)


def map_fwd(name, fn, grid, args, outs):
    n_in = len(args)

    def body(*refs):
        pid = tuple(pl.program_id(a) for a in range(len(grid)))
        res = fn(pid, *[r[...] for r in refs[:n_in]])
        for o, r in zip(refs[n_in:], res):
            o[...] = r.astype(o.dtype)

    res = pl.pallas_call(
        body, name=name, grid=grid,
        in_specs=[pl.BlockSpec(a.block, a.imap) for a in args],
        out_specs=[pl.BlockSpec(o.block, o.imap) for o in outs],
        out_shape=[jax.ShapeDtypeStruct(o.shape, o.dtype) for o in outs],
        compiler_params=_cparams(len(grid)))(*[a.arr for a in args])
    return tuple(res)


def _grad_outs(args, wrt):
    outs = []
    for i in wrt:
        a = args[i]
        shape = a.arr.shape if a.gshape is None else a.gshape
        imap = a.imap if a.gimap is None else a.gimap
        outs.append(Out(shape, f32 if a.gdtype is None else a.gdtype, a.block, imap))
    return outs


def _store_grads(pid, args, wrt, grads, grefs, adds):
    first_all = functools.reduce(jnp.logical_and, [p == 0 for p in pid])
    first_in = functools.reduce(jnp.logical_and, [p == 0 for p in pid[1:]]) if len(pid) > 1 else first_all
    for j, i in enumerate(wrt):
        g = grads[j].astype(f32)
        ref = grefs[j]
        kind = args[i].kind
        if kind == 'tile':
            if j in adds:
                g = g + adds[j]
            ref[...] = g.astype(ref.dtype)
        else:
            first = first_all if kind == 'acc' else first_in

            @pl.when(first)
            def _(ref=ref):
                ref[...] = jnp.zeros_like(ref)

            ref[...] += g


def map_bwd(name, fn, grid, args, douts, wrt, add=None):
    add = add or {}
    n_in, n_d, n_add = len(args), len(douts), len(add)
    add_keys = sorted(add)
    gouts = _grad_outs(args, wrt)

    def body(*refs):
        pid = tuple(pl.program_id(a) for a in range(len(grid)))
        vals = [r[...] for r in refs[:n_in]]
        dvals = [r[...].astype(f32) for r in refs[n_in:n_in + n_d]]
        avals = {k: refs[n_in + n_d + j][...].astype(f32) for j, k in enumerate(add_keys)}
        grefs = refs[n_in + n_d + n_add:]

        def f(*w):
            full = list(vals)
            for i, x in zip(wrt, w):
                full[i] = x
            return tuple(fn(pid, *full))

        _, vjp = jax.vjp(f, *[vals[i] for i in wrt])
        grads = vjp(tuple(dvals))
        _store_grads(pid, args, wrt, grads, grefs, avals)

    ins = list(args) + list(douts) + [add[k] for k in add_keys]
    res = pl.pallas_call(
        body, name=name, grid=grid,
        in_specs=[pl.BlockSpec(a.block, a.imap) for a in ins],
        out_specs=[pl.BlockSpec(o.block, o.imap) for o in gouts],
        out_shape=[jax.ShapeDtypeStruct(o.shape, o.dtype) for o in gouts],
        compiler_params=_cparams(len(grid)))(*[a.arr for a in ins])
    return tuple(res)


def scan_fwd(name, fn, grid, carry_shapes, args, outs):
    no, nt = grid
    n_in, n_out, n_c = len(args), len(outs), len(carry_shapes)
    cks = [Out((no, nt) + cs, f32, (None, None) + cs, lambda o, t, n=len(cs): (o, t) + (0,) * n) for cs in carry_shapes]

    def body(*refs):
        pid = (pl.program_id(0), pl.program_id(1))
        ins = refs[:n_in]
        orefs = refs[n_in:n_in + n_out]
        ckrefs = refs[n_in + n_out:n_in + n_out + n_c]
        crefs = refs[n_in + n_out + n_c:]

        @pl.when(pid[1] == 0)
        def _():
            for c in crefs:
                c[...] = jnp.zeros_like(c)

        carry = tuple(c[...] for c in crefs)
        for ck, c in zip(ckrefs, carry):
            ck[...] = c
        res, newc = fn(pid, carry, *[r[...] for r in ins])
        for o, r in zip(orefs, res):
            o[...] = r.astype(o.dtype)
        for c, v in zip(crefs, newc):
            c[...] = v

    allouts = list(outs) + cks
    res = pl.pallas_call(
        body, name=name, grid=grid,
        in_specs=[pl.BlockSpec(a.block, a.imap) for a in args],
        out_specs=[pl.BlockSpec(o.block, o.imap) for o in allouts],
        out_shape=[jax.ShapeDtypeStruct(o.shape, o.dtype) for o in allouts],
        scratch_shapes=[pltpu.VMEM(cs, f32) for cs in carry_shapes],
        compiler_params=_cparams(2))(*[a.arr for a in args])
    return tuple(res[:n_out]), tuple(res[n_out:])


def scan_bwd(name, fn, grid, carry_shapes, args, ckpts, douts, wrt, bwd_fn=None):
    no, nt = grid
    n_in, n_d, n_c = len(args), len(douts), len(carry_shapes)

    def rev(imap):
        return lambda o, t: imap(o, nt - 1 - t)

    rargs = [a._replace(imap=rev(a.imap), gimap=None if a.gimap is None else rev(a.gimap)) for a in args]
    rdouts = [a._replace(imap=rev(a.imap)) for a in douts]
    ckargs = [Arg(ck, (None, None) + cs, rev(lambda o, t, n=len(cs): (o, t) + (0,) * n)) for ck, cs in zip(ckpts, carry_shapes)]
    gouts = _grad_outs(rargs, wrt)

    def body(*refs):
        o, t = pl.program_id(0), pl.program_id(1)
        tt = nt - 1 - t
        vals = [r[...] for r in refs[:n_in]]
        dvals = [r[...].astype(f32) for r in refs[n_in:n_in + n_d]]
        carry = tuple(r[...] for r in refs[n_in + n_d:n_in + n_d + n_c])
        grefs = refs[n_in + n_d + n_c:n_in + n_d + n_c + len(wrt)]
        dcrefs = refs[n_in + n_d + n_c + len(wrt):]

        @pl.when(t == 0)
        def _():
            for c in dcrefs:
                c[...] = jnp.zeros_like(c)

        def f(carry, *w):
            full = list(vals)
            for i, x in zip(wrt, w):
                full[i] = x
            res, newc = fn((o, tt), carry, *full)
            return tuple(res), tuple(newc)

        dcarry = tuple(c[...] for c in dcrefs)
        if bwd_fn is None:
            _, vjp = jax.vjp(f, carry, *[vals[i] for i in wrt])
            grads = vjp((tuple(dvals), dcarry))
            dcarry_in, grads = grads[0], grads[1:]
        else:
            dcarry_in, grads = bwd_fn((o, tt), carry, vals, dvals, dcarry)
        for c, g in zip(dcrefs, dcarry_in):
            c[...] = g
        _store_grads((o, t), rargs, wrt, grads, grefs, {})

    ins = rargs + rdouts + ckargs
    res = pl.pallas_call(
        body, name=name, grid=grid,
        in_specs=[pl.BlockSpec(a.block, a.imap) for a in ins],
        out_specs=[pl.BlockSpec(g.block, g.imap) for g in gouts],
        out_shape=[jax.ShapeDtypeStruct(g.shape, g.dtype) for g in gouts],
        scratch_shapes=[pltpu.VMEM(cs, f32) for cs in carry_shapes],
        compiler_params=_cparams(2))(*[a.arr for a in ins])
    return tuple(res)


def _pick(dim, target):
    if dim <= target:
        return dim
    for t in range(target, 127, -128):
        if dim % t == 0:
            return t
    return dim


def matmul(name, a, b, mode='nn', add=None, out_dtype=f32, tm=None, tn=1152, tk=None):
    if mode == 'tn':
        K, M = a.shape
    else:
        M, K = a.shape
    N = b.shape[0] if mode == 'nt' else b.shape[1]
    assert (b.shape[1] if mode == 'nt' else b.shape[0]) == K
    tm = (1024 if mode == 'tn' else 512) if tm is None else tm
    tk = (512 if mode == 'tn' else 1152) if tk is None else tk
    tm, tn, tk = _pick(M, tm), _pick(N, tn), _pick(K, tk)
    nk = K // tk
    a_spec = pl.BlockSpec((tk, tm), lambda i, j, k: (k, i)) if mode == 'tn' else pl.BlockSpec((tm, tk), lambda i, j, k: (i, k))
    b_spec = pl.BlockSpec((tn, tk), lambda i, j, k: (j, k)) if mode == 'nt' else pl.BlockSpec((tk, tn), lambda i, j, k: (k, j))
    dims = {'nn': (((1,), (0,)), ((), ())), 'nt': (((1,), (1,)), ((), ())), 'tn': (((0,), (0,)), ((), ()))}[mode]
    has_add = add is not None

    def body(*refs):
        if has_add:
            a_ref, b_ref, add_ref, o_ref, acc = refs
        else:
            a_ref, b_ref, o_ref, acc = refs
        k = pl.program_id(2)

        @pl.when(k == 0)
        def _():
            acc[...] = add_ref[...].astype(f32) if has_add else jnp.zeros_like(acc)

        acc[...] += lax.dot_general(a_ref[...].astype(bf16), b_ref[...].astype(bf16), dims, preferred_element_type=f32)

        @pl.when(k == nk - 1)
        def _():
            o_ref[...] = acc[...].astype(o_ref.dtype)

    in_specs = [a_spec, b_spec] + ([pl.BlockSpec((tm, tn), lambda i, j, k: (i, j))] if has_add else [])
    ops = [a, b] + ([add] if has_add else [])
    return pl.pallas_call(
        body, name=name, grid=(M // tm, N // tn, nk), in_specs=in_specs,
        out_specs=pl.BlockSpec((tm, tn), lambda i, j, k: (i, j)),
        out_shape=jax.ShapeDtypeStruct((M, N), out_dtype),
        scratch_shapes=[pltpu.VMEM((tm, tn), f32)],
        compiler_params=pltpu.CompilerParams(dimension_semantics=("parallel", "parallel", "arbitrary"), vmem_limit_bytes=VMEM_LIMIT))(*ops)


def matmul_nt_sum(name, lhs, rhs, tm=1024, tk=512):
    M, N = lhs[0].shape[0], rhs[0].shape[0]
    tm = _pick(M, tm)
    tks = [_pick(a.shape[1], tk) for a in lhs]
    starts, total = [], 0
    for a, t in zip(lhs, tks):
        starts.append(total)
        total += a.shape[1] // t
    npc = len(lhs)

    def body(*refs):
        a_refs, b_refs, o_ref, acc = refs[:npc], refs[npc:2 * npc], refs[2 * npc], refs[2 * npc + 1]
        k = pl.program_id(1)

        @pl.when(k == 0)
        def _():
            acc[...] = jnp.zeros_like(acc)

        for p in range(npc):
            @pl.when((k >= starts[p]) & (k < starts[p] + lhs[p].shape[1] // tks[p]))
            def _(p=p):
                acc[...] += lax.dot_general(a_refs[p][...].astype(bf16), b_refs[p][...].astype(bf16), _NT, preferred_element_type=f32)

        @pl.when(k == total - 1)
        def _():
            o_ref[...] = acc[...]

    def kblock(p):
        return lambda k: jnp.clip(k - starts[p], 0, lhs[p].shape[1] // tks[p] - 1)

    in_specs = [pl.BlockSpec((tm, tks[p]), lambda i, k, kb=kblock(p): (i, kb(k))) for p in range(npc)]
    in_specs += [pl.BlockSpec((N, tks[p]), lambda i, k, kb=kblock(p): (0, kb(k))) for p in range(npc)]
    return pl.pallas_call(
        body, name=name, grid=(M // tm, total), in_specs=in_specs,
        out_specs=pl.BlockSpec((tm, N), lambda i, k: (i, 0)),
        out_shape=jax.ShapeDtypeStruct((M, N), f32),
        scratch_shapes=[pltpu.VMEM((tm, N), f32)],
        compiler_params=pltpu.CompilerParams(dimension_semantics=("parallel", "arbitrary"), vmem_limit_bytes=VMEM_LIMIT))(*lhs, *rhs)


def wgrad(name, act, dout):
    return matmul(name, act, dout, 'tn', out_dtype=bf16)


def _dot(a, b, dims=(((1,), (0,)), ((), ()))):
    return lax.dot_general(a.astype(bf16), b.astype(bf16), dims, preferred_element_type=f32)


_NT = (((1,), (1,)), ((), ()))
_TN = (((0,), (0,)), ((), ()))


def _three_term_dot(v, sel, dims):
    hi = v.astype(bf16)
    rest = v - hi.astype(f32)
    mid = rest.astype(bf16)
    lo = (rest - mid.astype(f32)).astype(bf16)
    dot = lambda t: lax.dot_general(t, sel, dims, preferred_element_type=f32)
    return dot(hi) + dot(mid) + dot(lo)


@jax.custom_vjp
def _dot_exact01(v, sel):
    return _three_term_dot(v, sel, (((1,), (0,)), ((), ())))


def _dot_exact01_fwd(v, sel):
    return _dot_exact01(v, sel), sel


def _dot_exact01_bwd(sel, ct):
    return _three_term_dot(ct, sel, _NT), jnp.zeros_like(sel)


_dot_exact01.defvjp(_dot_exact01_fwd, _dot_exact01_bwd)


def _spread_heads(v, width):
    r = lax.broadcasted_iota(jnp.int32, (HPAD, SSD_HEADS * width), 0)
    c = lax.broadcasted_iota(jnp.int32, (HPAD, SSD_HEADS * width), 1)
    return _dot_exact01(v, (r == c // width).astype(bf16))


def _rmsnorm_tile(pid, x, w):
    return (x * lax.rsqrt(jnp.mean(x * x, axis=-1, keepdims=True) + RMS_EPS) * w,)


def _shift_rows(h, d, fill):
    pad = jnp.full((d, h.shape[1]), fill, f32)
    return jnp.concatenate([pad, h[:-d]], axis=0)


def _s5_prep_tile(pid, a_re, a_im, ls, btr, bti, ctr, cti):
    o = pid[0]
    w = a_re.shape[1]
    r = lax.broadcasted_iota(jnp.int32, (S5_GROUPS, w), 0)
    c = lax.broadcasted_iota(jnp.int32, (S5_GROUPS, w), 1)
    sel = (r == o * (w // S5_STATE) + c // S5_STATE).astype(f32)
    step = jnp.dot(jnp.exp(ls), sel, precision=lax.Precision.HIGHEST, preferred_element_type=f32)
    mag = jnp.exp(a_re * step)
    ang = a_im * step
    lr, li = mag * jnp.cos(ang), mag * jnp.sin(ang)
    nr, ni = lr - 1.0, li
    den = a_re * a_re + a_im * a_im
    fr = (nr * a_re + ni * a_im) / den
    fi = (ni * a_re - nr * a_im) / den
    bbr = fr * btr - fi * bti
    bbi = fr * bti + fi * btr
    reps = w // S5_STATE
    rr = lax.broadcasted_iota(jnp.int32, (reps * S5_GROUP, w), 0)
    cc = lax.broadcasted_iota(jnp.int32, (reps * S5_GROUP, w), 1)
    diag = (rr // S5_GROUP) == (cc // S5_STATE)

    def expand(m):
        return jnp.where(diag, jnp.concatenate([m] * reps, axis=0), 0.0)

    pr, pi = lr, li
    rows_r, rows_i = [pr], [pi]
    for _ in range(S5_ND - 1):
        pr, pi = pr * pr - pi * pi, 2.0 * pr * pi
        rows_r.append(pr)
        rows_i.append(pi)
    lamd_r, lamd_i = jnp.concatenate(rows_r, axis=0), jnp.concatenate(rows_i, axis=0)
    tr = jnp.broadcast_to(lr, (S5_SUB, w))
    ti = jnp.broadcast_to(li, (S5_SUB, w))
    for j in range(S5_ND):
        sr, si = _shift_rows(tr, 1 << j, 1.0), _shift_rows(ti, 1 << j, 0.0)
        tr, ti = tr * sr - ti * si, tr * si + ti * sr
    return lamd_r, lamd_i, tr, ti, expand(bbr), expand(bbi), expand(ctr), expand(cti)


def _s5_tile(pid, carry, u, lamd_r, lamd_i, lam8_r, lam8_i, bbr, bbi, ccr, cci, dvec):
    hr, hi = _s5_scan(_dot(u, bbr), _dot(u, bbi), carry, lamd_r, lamd_i, lam8_r, lam8_i, reverse=False)
    return (_s5_readout(hr, hi, u, ccr, cci, dvec),), (hr[-1:], hi[-1:])


def _s5_readout(hr, hi, u, ccr, cci, dvec):
    return jax.nn.gelu(_dot(hr, ccr, _NT) - _dot(hi, cci, _NT) + dvec * u)


def _s5_scan(xr, xi, carry, lamd_r, lamd_i, lam8_r, lam8_i, reverse):
    cr, ci = carry
    T, G = xr.shape[0], S5_SUB
    sign = -1.0 if reverse else 1.0
    sub = lax.broadcasted_iota(jnp.int32, (T, 1), 0) % G
    for j in range(S5_ND):
        d = 1 << j
        if reverse:
            keep = sub < G - d
            sr = jnp.where(keep, jnp.concatenate([xr[d:], jnp.zeros((d, xr.shape[1]), f32)], axis=0), 0.0)
            si = jnp.where(keep, jnp.concatenate([xi[d:], jnp.zeros((d, xi.shape[1]), f32)], axis=0), 0.0)
        else:
            keep = sub >= d
            sr = jnp.where(keep, _shift_rows(xr, d, 0.0), 0.0)
            si = jnp.where(keep, _shift_rows(xi, d, 0.0), 0.0)
        ar, ai = lamd_r[j:j + 1], sign * lamd_i[j:j + 1]
        xr, xi = xr + ar * sr - ai * si, xi + ar * si + ai * sr
    if reverse:
        pr = jnp.concatenate([lam8_r[G - 1 - s:G - s] for s in range(G)], axis=0)
        pi = -jnp.concatenate([lam8_i[G - 1 - s:G - s] for s in range(G)], axis=0)
    else:
        pr, pi = lam8_r, lam8_i
    n = T // G
    rows_r, rows_i = [None] * n, [None] * n
    for i in (reversed(range(n)) if reverse else range(n)):
        gr_, gi_ = xr[i * G:(i + 1) * G], xi[i * G:(i + 1) * G]
        gr_, gi_ = gr_ + pr * cr - pi * ci, gi_ + pr * ci + pi * cr
        cr, ci = (gr_[:1], gi_[:1]) if reverse else (gr_[G - 1:], gi_[G - 1:])
        rows_r[i], rows_i[i] = gr_, gi_
    return jnp.concatenate(rows_r, axis=0), jnp.concatenate(rows_i, axis=0)


def _s5_tile_bwd(pid, carry, vals, douts, dcarry):
    u, lamd_r, lamd_i, lam8_r, lam8_i, bbr, bbi, ccr, cci, dvec = vals
    (dg,) = douts
    hr, hi = _s5_scan(_dot(u, bbr), _dot(u, bbi), carry, lamd_r, lamd_i, lam8_r, lam8_i, reverse=False)
    _, vjp = jax.vjp(_s5_readout, hr, hi, u, ccr, cci, dvec)
    dhr, dhi, du, dccr, dcci, ddvec = vjp(dg)
    Hr, Hi = _s5_scan(dhr, dhi, dcarry, lamd_r, lamd_i, lam8_r, lam8_i, reverse=True)
    _, vjp_in = jax.vjp(lambda u, bbr, bbi: (_dot(u, bbr), _dot(u, bbi)), u, bbr, bbi)
    du2, dbbr, dbbi = vjp_in((Hr, Hi))
    pr = jnp.concatenate([carry[0], hr[:-1]], axis=0)
    pi = jnp.concatenate([carry[1], hi[:-1]], axis=0)
    dlam_r = jnp.sum(Hr * pr + Hi * pi, axis=0, keepdims=True)
    dlam_i = jnp.sum(Hi * pr - Hr * pi, axis=0, keepdims=True)
    zrow = jnp.zeros((S5_ND - 1, dlam_r.shape[1]), f32)
    dlamd_r, dlamd_i = jnp.concatenate([dlam_r, zrow], axis=0), jnp.concatenate([dlam_i, zrow], axis=0)
    grads = (du + du2, dlamd_r, dlamd_i, jnp.zeros_like(lam8_r), jnp.zeros_like(lam8_i), dbbr, dbbi, dccr, dcci, ddvec)
    return (Hr[:1], Hi[:1]), grads


def _glu_tile(pid, g, glu, za, b):
    return (g * jax.nn.sigmoid(glu + b) * jax.nn.silu(za),)


def _attn_tile(pid, carry, q, k, v, qw, kw):
    n = pid[1]
    kp, vp = carry
    D, B = ATT_HEAD_DIM, ATT_BLOCK
    W = 2 * D
    nq, ncol = q.shape[0] // B, q.shape[1] // W
    r = lax.broadcasted_iota(jnp.int32, (B, 2 * B), 0)
    c = lax.broadcasted_iota(jnp.int32, (B, 2 * B), 1)
    diff = r + B - c
    band = (diff >= 0) & (diff <= B)
    band_first = band & ((c >= B) | (n > 0))
    low = lax.broadcasted_iota(jnp.int32, (1, W), 1) < D
    same_head = (lax.broadcasted_iota(jnp.int32, (W, W), 0) // D == lax.broadcasted_iota(jnp.int32, (W, W), 1) // D).astype(bf16)

    def hnorm(x, w):
        rows = x.shape[0]
        t = jnp.concatenate([x[:, j * W:(j + 1) * W] for j in range(ncol)], axis=0) if ncol > 1 else x
        ms = _dot_exact01(t * t, same_head) * (1.0 / D)
        t = t * lax.rsqrt(ms + RMS_EPS) * jnp.concatenate([w, w], axis=1)
        return [t[j * rows:(j + 1) * rows] for j in range(ncol)]

    qns, kns = hnorm(q, qw), hnorm(k, kw)
    out_cols, lse_cols, kn_cols = [], [], []
    for j in range(ncol):
        sl = slice(j * W, (j + 1) * W)
        qn, kn, vj = qns[j], kns[j], v[:, sl]
        kn_cols.append(kn[(nq - 1) * B:])
        outs, lses = [], []
        for b in range(nq):
            rows = slice(b * B, (b + 1) * B)
            prev = slice((b - 1) * B, b * B)
            kk = jnp.concatenate([kp[:, sl] if b == 0 else kn[prev], kn[rows]], axis=0)
            vv = jnp.concatenate([vp[:, sl] if b == 0 else vj[prev], vj[rows]], axis=0)
            o2, l2 = [], []
            for head_lanes in (low, ~low):
                s = _dot(jnp.where(head_lanes, qn[rows], 0.0), kk, _NT) * (D ** -0.5)
                s = jnp.where(band_first if b == 0 else band, s, -1e30)
                m = jnp.max(s, axis=-1, keepdims=True)
                p = jnp.exp(s - m)
                l = jnp.sum(p, axis=-1, keepdims=True)
                o2.append(_dot(p / l, vv))
                l2.append(m + jnp.log(l))
            outs.append(jnp.where(low, o2[0], o2[1]))
            lses.append(jnp.where(low, l2[0], l2[1]))
        out_cols.append(jnp.concatenate(outs, axis=0) if nq > 1 else outs[0])
        lse_cols.append(jnp.concatenate(lses, axis=0) if nq > 1 else lses[0])
    return ((jnp.concatenate(out_cols, axis=1), jnp.concatenate(lse_cols, axis=1)),
            (jnp.concatenate(kn_cols, axis=1), v[(nq - 1) * B:]))


def _combine_tile(pid, o1, l1, o2, l2, o3, l3, zb):
    m = jnp.maximum(jnp.maximum(l1, l2), l3)
    e1, e2, e3 = jnp.exp(l1 - m), jnp.exp(l2 - m), jnp.exp(l3 - m)
    y = (e1 * o1 + e2 * o2 + e3 * o3) / (e1 + e2 + e3)
    return (y * jax.nn.silu(zb),)


def _softplus(x):
    return jnp.maximum(x, 0.0) + jnp.log(1.0 + jnp.exp(-jnp.abs(x)))


def _ssd_tile(pid, carry, xbc, dt, z, conv_w, conv_b, dt_bias, a_log, dvec, norm_w):
    xprev, state = carry
    T, P, N = SSD_CHUNK, SSD_HEAD_DIM, SSD_STATE
    xx = jnp.concatenate([xprev, xbc], axis=0)
    conv = conv_b
    for k in range(SSD_CONV):
        off = 8 - (SSD_CONV - 1) + k
        conv = conv + conv_w[k:k + 1] * xx[off:off + T]
    xc = jax.nn.silu(conv)
    dtp = _softplus(dt + dt_bias)
    a_dt = dtp * (-jnp.exp(a_log))
    r = lax.broadcasted_iota(jnp.int32, (T, T), 0)
    c = lax.broadcasted_iota(jnp.int32, (T, T), 1)
    tri = r >= c
    trif = tri.astype(f32)
    hi = lax.Precision.HIGHEST
    a_cs = jnp.dot(trif, a_dt, precision=hi, preferred_element_type=f32)
    a_cs_t = lax.dot_general(a_dt, trif, (((0,), (1,)), ((), ())), precision=hi, preferred_element_type=f32)
    xs = xc[:, :SSD_WIDTH]
    acs_p = _spread_heads(a_cs, P)
    acs_t = _spread_heads(a_cs, T)
    xdt = xs * _spread_heads(dtp, P)
    skip = _spread_heads(dvec, P)
    to_end = jnp.exp(acs_p[T - 1:T] - acs_p)
    low = lax.broadcasted_iota(jnp.int32, (1, 2 * P), 1) < P
    low_rows = lax.broadcasted_iota(jnp.int32, (2 * P, 1), 0) < P
    ys, states = [], []
    for j in range(SSD_HEADS // 2):
        g = 2 * j // (SSD_HEADS // SSD_GROUPS)
        if 2 * j % (SSD_HEADS // SSD_GROUPS) == 0:
            bg = xc[:, SSD_WIDTH + g * N:SSD_WIDTH + (g + 1) * N]
            cg = xc[:, SSD_WIDTH + SSD_GROUPS * N + g * N:SSD_WIDTH + SSD_GROUPS * N + (g + 1) * N]
            cb = _dot(cg, bg, _NT)
        lanes = slice(2 * j * P, 2 * (j + 1) * P)
        st = state[lanes, :]
        diag, last = [], []
        for h in (2 * j, 2 * j + 1):
            decay = jnp.exp(jnp.where(tri, acs_t[:, h * T:(h + 1) * T] - a_cs_t[h:h + 1, :], -1e30))
            diag.append(_dot(cb * decay, xdt[:, lanes]))
            last.append(jnp.exp(a_cs_t[h:h + 1, T - 1:T]))
        y = (jnp.where(low, diag[0], diag[1]) + _dot(cg, st, _NT) * jnp.exp(acs_p[:, lanes])
             + xs[:, lanes] * skip[:, lanes])
        ys.append(y)
        states.append(jnp.where(low_rows, last[0], last[1]) * st + _dot(xdt[:, lanes] * to_end[:, lanes], bg, _TN))
    y = jnp.concatenate(ys, axis=1) * jax.nn.silu(z)
    out = y * lax.rsqrt(jnp.mean(y * y, axis=-1, keepdims=True) + RMS_EPS) * norm_w
    return (out,), (xbc[T - 8:], jnp.concatenate(states, axis=0))


def _merge_tile(pid, pa, pb, pc, g0, g1, g2):
    return (jax.nn.sigmoid(g0) * pa + jax.nn.sigmoid(g1) * pb + jax.nn.sigmoid(g2) * pc,)


def loss_and_grad(y, target, tm=512):
    S, D = y.shape
    nt = S // tm

    def body(y_ref, t_ref, dy_ref, l_ref, acc):
        i = pl.program_id(0)

        @pl.when(i == 0)
        def _():
            acc[...] = jnp.zeros_like(acc)

        diff = y_ref[...] - t_ref[...]
        dy_ref[...] = diff * (1.0 / D)
        acc[...] += jnp.sum((diff * diff).reshape(tm // 8, 8, D), axis=0)

        @pl.when(i == nt - 1)
        def _():
            l_ref[...] = jnp.broadcast_to(0.5 / D * jnp.sum(acc[...]), l_ref.shape)

    dy, l = pl.pallas_call(
        body, name="loss_head", grid=(nt,),
        in_specs=[pl.BlockSpec((tm, D), lambda i: (i, 0))] * 2,
        out_specs=[pl.BlockSpec((tm, D), lambda i: (i, 0)), pl.BlockSpec((8, 128), lambda i: (0, 0))],
        out_shape=[jax.ShapeDtypeStruct((S, D), f32), jax.ShapeDtypeStruct((8, 128), f32)],
        scratch_shapes=[pltpu.VMEM((8, D), f32)],
        compiler_params=_cparams(1))(y, target)
    return dy, l[0, 0]


def _row_tile(R, C, budget=1 << 20):
    best = R
    for t in range(8, R, 8):
        if R % t == 0 and t * C * 4 <= budget:
            best = t
    if best == R and R * C * 4 > budget:
        for t in range(8, R, 8):
            if R % t == 0:
                return t
    return best


def _as2d(t, lead=0):
    return t.reshape(t.shape[:lead] + (math.prod(t.shape[lead:-1]), t.shape[-1]))


def adamw(name, w, gslots, m, v):
    shape = w.shape
    n = gslots.shape[0]
    C = shape[-1]
    R = math.prod(shape[:-1])
    lanes = -(-C // 128) * 128
    tr = _row_tile(R, lanes * (n + 7), budget=10 << 20)

    def body(w_ref, g_ref, m_ref, v_ref, go_ref, d_ref, nm_ref, nv_ref):
        gg = g_ref[0].astype(f32)
        for s in range(1, n):
            gg = gg + g_ref[s].astype(f32)
        go_ref[...] = gg
        nm = ADAM_B1 * m_ref[...] + (1.0 - ADAM_B1) * gg
        nv = ADAM_B2 * v_ref[...] + (1.0 - ADAM_B2) * jnp.square(gg)
        m_hat = nm / (1.0 - ADAM_B1 ** ADAM_STEP)
        v_hat = nv / (1.0 - ADAM_B2 ** ADAM_STEP)
        d_ref[...] = -ADAM_LR * (m_hat / (jnp.sqrt(v_hat) + ADAM_EPS) + ADAM_WD * w_ref[...])
        nm_ref[...] = nm
        nv_ref[...] = nv

    spec = pl.BlockSpec((tr, C), lambda i: (i, 0))
    res = pl.pallas_call(
        body, name=name, grid=(R // tr,),
        in_specs=[spec, pl.BlockSpec((n, tr, C), lambda i: (0, i, 0)), spec, spec], out_specs=[spec] * 4,
        out_shape=[jax.ShapeDtypeStruct((R, C), f32)] * 4,
        compiler_params=_cparams(1))(w.reshape(R, C), gslots.reshape(n, R, C), m.reshape(R, C), v.reshape(R, C))
    return tuple(t.reshape(shape) for t in res)


PACK_ROWS = 256


def sum_slots(name, x):
    n, R, C = x.shape

    def body(x_ref, o_ref):
        acc = x_ref[0]
        for s in range(1, n):
            acc = acc + x_ref[s]
        o_ref[...] = acc

    return pl.pallas_call(
        body, name=name, grid=(R // PACK_ROWS,),
        in_specs=[pl.BlockSpec((n, PACK_ROWS, C), lambda i: (0, i, 0))],
        out_specs=pl.BlockSpec((PACK_ROWS, C), lambda i: (i, 0)),
        out_shape=jax.ShapeDtypeStruct((R, C), f32), compiler_params=_cparams(1))(x)


def _pack(parts):
    flat = jnp.concatenate([p.reshape(-1) for p in parts])
    unit = 128 * PACK_ROWS
    tot = -(-flat.shape[0] // unit) * unit
    return jnp.pad(flat, (0, tot - flat.shape[0])).reshape(tot // 128, 128)


def _unpack(buf, shapes):
    flat = buf.reshape(-1)
    out, off = [], 0
    for s in shapes:
        size = math.prod(s)
        out.append(flat[off:off + size].reshape(s))
        off += size
    return out


def exchange(name, srcs, modes):
    nt = len(srcs)
    slabs = []
    for s, mode in zip(srcs, modes):
        R, C = s.shape
        slabs.append({'all': (R, C), 'rows': (R // N_DEV, C), 'cols': (R, C // N_DEV)}[mode])

    def piece(ref, mode, slab, p):
        if mode == 'all':
            return ref
        if mode == 'rows':
            return ref.at[pl.ds(p * slab[0], slab[0]), :]
        return ref.at[:, pl.ds(p * slab[1], slab[1])]

    def body(*refs):
        src_refs, out_refs = refs[:nt], refs[nt:2 * nt]
        send_sems, recv_sems, local_sems = refs[2 * nt:]
        x, y, c = lax.axis_index("x"), lax.axis_index("y"), lax.axis_index("c")
        me = 4 * x + 2 * y + c
        copies = []
        for k in (1, 2, 4, 3, 5, 6, 7):
            px = 1 - x if k & 4 else x
            py = 1 - y if k & 2 else y
            pc = 1 - c if k & 1 else c
            for t in range(nt):
                cp = pltpu.make_async_remote_copy(
                    src_ref=piece(src_refs[t], modes[t], slabs[t], 4 * px + 2 * py + pc), dst_ref=out_refs[t].at[me],
                    send_sem=send_sems.at[t, k - 1], recv_sem=recv_sems.at[t, k - 1],
                    device_id=(px, py, pc), device_id_type=pl.DeviceIdType.MESH)
                cp.start()
                copies.append(cp)
        for t in range(nt):
            own = pltpu.make_async_copy(piece(src_refs[t], modes[t], slabs[t], me), out_refs[t].at[me], local_sems.at[t])
            own.start()
            copies.append(own)
        for cp in copies:
            cp.wait()

    res = pl.pallas_call(
        body, name=name,
        in_specs=[pl.BlockSpec(memory_space=pl.ANY)] * nt,
        out_specs=[pl.BlockSpec(memory_space=pl.ANY)] * nt,
        out_shape=[jax.ShapeDtypeStruct((N_DEV,) + sl, s.dtype) for s, sl in zip(srcs, slabs)],
        scratch_shapes=[pltpu.SemaphoreType.DMA((nt, N_DEV - 1)), pltpu.SemaphoreType.DMA((nt, N_DEV - 1)), pltpu.SemaphoreType.DMA((nt,))],
    )(*srcs)
    return list(res)


def gather_two_level(name, srcs):
    nt = len(srcs)

    def body(*refs):
        src_refs, out_refs = refs[:nt], refs[nt:2 * nt]
        send_sems, recv_sems, local_sems = refs[2 * nt:]
        x, y, c = lax.axis_index("x"), lax.axis_index("y"), lax.axis_index("c")
        me, sibling = (x, y, c), (x, y, 1 - c)
        chips = [(1 - x, y), (x, 1 - y), (1 - x, 1 - y)]

        def slot(t, dev):
            return out_refs[t].at[4 * dev[0] + 2 * dev[1] + dev[2]]

        def copy(t, k, block, to, src=None):
            return pltpu.make_async_remote_copy(
                src_ref=slot(t, block) if src is None else src, dst_ref=slot(t, block),
                send_sem=send_sems.at[t, k], recv_sem=recv_sems.at[t, k], device_id=to, device_id_type=pl.DeviceIdType.MESH)

        mine = [pltpu.make_async_copy(src_refs[t], slot(t, me), local_sems.at[t]) for t in range(nt)]
        for cp in mine:
            cp.start()
        sent = []
        for t in range(nt):
            sent.append(copy(t, 0, me, sibling, src=src_refs[t]))
            sent += [copy(t, 1 + j, me, (*chip, c), src=src_refs[t]) for j, chip in enumerate(chips)]
        for cp in sent:
            cp.start()
        for j, chip in enumerate(chips):
            for t in range(nt):
                copy(t, 1 + j, (*chip, c), me).wait_recv()
                fwd = copy(t, 4 + j, (*chip, c), sibling)
                fwd.start()
                sent.append(fwd)
        for t in range(nt):
            copy(t, 0, sibling, me).wait_recv()
            for j, chip in enumerate(chips):
                copy(t, 4 + j, (*chip, 1 - c), me).wait_recv()
        for cp in sent:
            cp.wait_send()
        for cp in mine:
            cp.wait()

    res = pl.pallas_call(
        body, name=name,
        in_specs=[pl.BlockSpec(memory_space=pl.ANY)] * nt,
        out_specs=[pl.BlockSpec(memory_space=pl.ANY)] * nt,
        out_shape=[jax.ShapeDtypeStruct((N_DEV,) + s.shape, s.dtype) for s in srcs],
        scratch_shapes=[pltpu.SemaphoreType.DMA((nt, N_DEV - 1)), pltpu.SemaphoreType.DMA((nt, N_DEV - 1)), pltpu.SemaphoreType.DMA((nt,))],
    )(*srcs)
    return list(res)


def _relayout_w_in(w):
    offs = [0]
    for s in IN_SPLITS:
        offs.append(offs[-1] + s)
    p = [w[:, offs[i]:offs[i + 1]] for i in range(len(IN_SPLITS))]
    ua, za, q, k, v, zb, xbc, dt, zc, gates = p
    dtp = jnp.pad(dt, ((0, 0), (0, HPAD - dt.shape[1])))
    return (jnp.concatenate([ua, za, dtp], 1), w[:, offs[2]:offs[5]], jnp.concatenate([xbc, zb, zc], 1), gates)


def _pad_lanes(v, n=HPAD):
    return jnp.pad(v.reshape(1, -1), ((0, 0), (0, n - v.shape[-1])))


def _s5_prep_args(W):
    g2 = S5_GROUPS * S5_STATE
    w = g2 // S5_CHUNKS
    a_re, a_im = W['s5_a_re'].reshape(1, g2), W['s5_a_im'].reshape(1, g2)
    ls = W['s5_log_step'].reshape(1, S5_GROUPS)
    btr, bti = W['s5_b_re'].reshape(g2, S5_GROUP).T, W['s5_b_im'].reshape(g2, S5_GROUP).T
    ctr = W['s5_c_re'].transpose(1, 0, 2).reshape(S5_GROUP, g2)
    cti = W['s5_c_im'].transpose(1, 0, 2).reshape(S5_GROUP, g2)
    col = lambda a, rows: Arg(a, (rows, w), lambda o: (0, o), 'tile')
    return [col(a_re, 1), col(a_im, 1), _whole(ls, 'acc'), col(btr, S5_GROUP), col(bti, S5_GROUP), col(ctr, S5_GROUP), col(cti, S5_GROUP)]


def _s5_prep_outs():
    g2 = S5_GROUPS * S5_STATE
    w = g2 // S5_CHUNKS
    rows = (S5_ND, S5_ND, S5_SUB, S5_SUB, 128, 128, 128, 128)
    return [Out((r, g2), f32, (r, w), lambda o: (0, o)) for r in rows]


def _s5_args(A, prep, dvec, S):
    w = S5_GROUPS * S5_STATE // S5_CHUNKS
    args = [Arg(A, (S5_TILE, 128), lambda o, t: (t, o), 'tile', (S, S5_WIDTH), None, bf16)]
    for p in prep:
        args.append(Arg(p, (p.shape[0], w), lambda o, t: (0, o), 'acc0'))
    args.append(Arg(dvec, (1, 128), lambda o, t: (0, o), 'acc0'))
    return args


def _attn_args(QKV, g, r, qw, kw, S):
    L = S // r
    nq, rb = _attn_plan(r)
    block = (nq * ATT_BLOCK, rb * ATT_GW)
    gshape = (L, r * ATT_GW)
    gimap = lambda rho, n: (n, rho)
    if r == 1:
        mk = lambda j: Arg(QKV, block, lambda rho, n, j=j: (n, j), 'tile', gshape, gimap, bf16)
    else:
        def mk(j):
            view = QKV[:, j * ATT_GW:(j + 1) * ATT_GW].reshape(L, r * ATT_GW)
            return Arg(view, block, gimap, 'tile', None, None, bf16)
    return [mk(g), mk(3 + g), mk(6 + g), _whole(qw, 'acc'), _whole(kw, 'acc')]


def _attn_plan(r):
    return (4, 1) if r == 1 else (1, min(r, 4))


def _attn_grid(r, S):
    nq, rb = _attn_plan(r)
    return (r // rb, S // r // ATT_BLOCK // nq)


def _attn_carry(r):
    return ((ATT_BLOCK, _attn_plan(r)[1] * ATT_GW),) * 2


def _ssd_args(C, A, W, S):
    T = SSD_CHUNK
    return [Arg(C, (T, SSD_CONV_DIM), lambda o, t: (t, 0), 'tile', (S, SSD_CONV_DIM), None, bf16),
            Arg(A, (T, HPAD), lambda o, t: (t, 2 * S5_WIDTH // HPAD), 'tile', (S, HPAD), lambda o, t: (t, 0), bf16),
            Arg(C, (T, SSD_WIDTH), lambda o, t: (t, 2), 'tile', (S, SSD_WIDTH), lambda o, t: (t, 0), bf16),
            _whole(W['conv_w'], 'acc'), _whole(W['conv_b'].reshape(1, -1), 'acc'),
            _whole(_pad_lanes(W['dt_bias']), 'acc'), _whole(_pad_lanes(W['ssd_a_log']), 'acc'),
            _whole(_pad_lanes(W['ssd_d']), 'acc'), _whole(W['ssd_norm_w'].reshape(1, -1), 'acc')]


_SSD_CARRY = ((8, SSD_CONV_DIM), (SSD_WIDTH, SSD_STATE))
_S5_CARRY = ((1, 512), (1, 512))


def layer_fwd(li, x, W):
    S = x.shape[0]
    n = lambda s: f"l{li}_{s}"
    sv = {'x': x}
    (h,) = map_fwd(n("norm"), _rmsnorm_tile, (S // 512,), [_rows(x, 512), _whole(W['norm_w'].reshape(1, -1))],
                   [Out((S, D_MODEL), bf16, (512, D_MODEL), lambda i: (i, 0))])
    wA, wQ, wC, wG = W['w_in_pieces']
    A = matmul(n("in_a"), h, wA)
    QKV = matmul(n("in_qkv"), h, wQ)
    C = matmul(n("in_c"), h, wC)
    G = matmul(n("in_g"), h, wG)
    sv.update(h=h, A=A, QKV=QKV, C=C, G=G)

    prep = map_fwd(n("s5_prep"), _s5_prep_tile, (S5_CHUNKS,), _s5_prep_args(W), _s5_prep_outs())
    dvec = W['s5_d'].reshape(1, -1)
    (g,), s5_ck = scan_fwd(n("s5_scan"), _s5_tile, (S5_CHUNKS, S // S5_TILE), _S5_CARRY, _s5_args(A, prep, dvec, S),
                           [Out((S, S5_WIDTH), f32, (S5_TILE, 128), lambda o, t: (t, o))])
    glu = matmul(n("glu"), g, W['s5_glu_w'])
    glu_b = W['s5_glu_b'].reshape(1, -1)
    (ya,) = map_fwd(n("glu_gate"), _glu_tile, (S // 512,),
                    [_rows(g, 512), _rows(glu, 512), _rows(A, 512, col=1, width=S5_WIDTH), _whole(glu_b)],
                    [Out((S, S5_WIDTH), bf16, (512, S5_WIDTH), lambda i: (i, 0))])
    sv.update(prep=prep, g=g, glu=glu, ya=ya, s5_ck=s5_ck)

    qw, kw = W['q_norm_w'].reshape(1, -1), W['k_norm_w'].reshape(1, -1)
    att, att_ck = [], []
    for gi, (window, r) in enumerate(ATT_PAIRS):
        assert window // r == ATT_BLOCK and S % (r * ATT_BLOCK) == 0
        L = S // r
        nq, rb = _attn_plan(r)
        assert S // r // ATT_BLOCK % nq == 0
        spec = Out((L, r * ATT_GW), f32, (nq * ATT_BLOCK, rb * ATT_GW), lambda rho, nb: (nb, rho))
        (o, lse), ck = scan_fwd(n(f"attn{gi}"), _attn_tile, _attn_grid(r, S), _attn_carry(r), _attn_args(QKV, gi, r, qw, kw, S), [spec, spec])
        att += [o.reshape(S, ATT_GW), lse.reshape(S, ATT_GW)]
        att_ck.append(ck)
    (yb,) = map_fwd(n("combine"), _combine_tile, (S // 512,),
                    [_rows(t, 512) for t in att] + [_rows(C, 512, col=SSD_CONV_DIM // ATT_GW, width=ATT_GW)],
                    [Out((S, ATT_GW), bf16, (512, ATT_GW), lambda i: (i, 0))])
    sv.update(att=att, att_ck=att_ck, yb=yb)

    (yc,), ssd_ck = scan_fwd(n("ssd"), _ssd_tile, (1, S // SSD_CHUNK), _SSD_CARRY, _ssd_args(C, A, W, S),
                             [Out((S, SSD_WIDTH), bf16, (SSD_CHUNK, SSD_WIDTH), lambda o, t: (t, 0))])
    sv.update(yc=yc, ssd_ck=ssd_ck)

    pa = matmul(n("proj_a"), ya, W['proj_a'])
    pb = matmul(n("proj_b"), yb, W['proj_b'])
    pc = matmul(n("proj_c"), yc, W['proj_c'])
    (merged,) = map_fwd(n("merge"), _merge_tile, (S // 256,),
                        [_rows(pa, 256), _rows(pb, 256), _rows(pc, 256)] + [_rows(G, 256, col=j, width=D_MODEL) for j in range(3)],
                        [Out((S, D_MODEL), bf16, (256, D_MODEL), lambda i: (i, 0))])
    out = matmul(n("w_out"), merged, W['w_out'], add=x)
    sv.update(pa=pa, pb=pb, pc=pc, merged=merged)
    return out, sv


def layer_bwd(li, dout, sv, W):
    S = dout.shape[0]
    n = lambda s: f"l{li}_{s}"
    gr = {}
    x, A, QKV, C, G = sv['x'], sv['A'], sv['QKV'], sv['C'], sv['G']

    dmerged = matmul(n("d_merged"), dout, W['w_out'], 'nt')
    gr['w_out'] = wgrad(n("g_w_out"), sv['merged'], dout)
    margs = [_rows(sv['pa'], 256, gdtype=bf16), _rows(sv['pb'], 256, gdtype=bf16), _rows(sv['pc'], 256, gdtype=bf16)] + \
            [_rows(G, 256, col=j, width=D_MODEL, gshape=(S, D_MODEL), gdtype=bf16) for j in range(3)]
    dpa, dpb, dpc, dg0, dg1, dg2 = map_bwd(n("merge_bwd"), _merge_tile, (S // 256,), margs, [_rows(dmerged, 256)], list(range(6)))
    dya = matmul(n("d_ya"), dpa, W['proj_a'], 'nt')
    dyb = matmul(n("d_yb"), dpb, W['proj_b'], 'nt')
    dyc = matmul(n("d_yc"), dpc, W['proj_c'], 'nt')
    gr['proj_a'] = wgrad(n("g_proj_a"), sv['ya'], dpa)
    gr['proj_b'] = wgrad(n("g_proj_b"), sv['yb'], dpb)
    gr['proj_c'] = wgrad(n("g_proj_c"), sv['yc'], dpc)

    glu_b = W['s5_glu_b'].reshape(1, -1)
    gargs = [_rows(sv['g'], 512), _rows(sv['glu'], 512, gdtype=bf16),
             _rows(A, 512, col=1, width=S5_WIDTH, gshape=(S, S5_WIDTH), gdtype=bf16), _whole(glu_b, 'acc')]
    dg_a, dglu, dza, dglu_b = map_bwd(n("glu_gate_bwd"), _glu_tile, (S // 512,), gargs, [_rows(dya, 512)], [0, 1, 2, 3])
    gr['s5_glu_b'] = dglu_b.reshape(-1)
    dg = matmul(n("d_g"), dglu, W['s5_glu_w'], 'nt', add=dg_a)
    gr['s5_glu_w'] = wgrad(n("g_glu_w"), sv['g'], dglu)
    dvec = W['s5_d'].reshape(1, -1)
    sargs = _s5_args(A, sv['prep'], dvec, S)
    res = scan_bwd(n("s5_scan_bwd"), _s5_tile, (S5_CHUNKS, S // S5_TILE), _S5_CARRY, sargs, sv['s5_ck'],
                   [Arg(dg, (S5_TILE, 128), lambda o, t: (t, o))], list(range(len(sargs))), bwd_fn=_s5_tile_bwd)
    dua, dprep, dd = res[0], res[1:9], res[9]
    gr['s5_d'] = dd.reshape(-1)
    pargs = _s5_prep_args(W)
    pouts = _s5_prep_outs()
    da_re, da_im, dls, dbtr, dbti, dctr, dcti = map_bwd(
        n("s5_prep_bwd"), _s5_prep_tile, (S5_CHUNKS,), pargs,
        [Arg(d, o.block, o.imap) for d, o in zip(dprep, pouts)], list(range(7)))
    gshape = (S5_GROUPS, S5_STATE)
    gr['s5_a_re'], gr['s5_a_im'] = da_re.reshape(gshape), da_im.reshape(gshape)
    gr['s5_log_step'] = dls.reshape(-1)
    gr['s5_b_re'] = dbtr.T.reshape(S5_GROUPS, S5_STATE, S5_GROUP)
    gr['s5_b_im'] = dbti.T.reshape(S5_GROUPS, S5_STATE, S5_GROUP)
    gr['s5_c_re'] = dctr.reshape(S5_GROUP, S5_GROUPS, S5_STATE).transpose(1, 0, 2)
    gr['s5_c_im'] = dcti.reshape(S5_GROUP, S5_GROUPS, S5_STATE).transpose(1, 0, 2)

    cargs = [_rows(t, 512) for t in sv['att']] + \
            [_rows(C, 512, col=SSD_CONV_DIM // ATT_GW, width=ATT_GW, gshape=(S, ATT_GW), gdtype=bf16)]
    cres = map_bwd(n("combine_bwd"), _combine_tile, (S // 512,), cargs, [_rows(dyb, 512)], list(range(7)))
    dzb = cres[6]
    qw, kw = W['q_norm_w'].reshape(1, -1), W['k_norm_w'].reshape(1, -1)
    dqs, dks, dvs = [], [], []
    dqw = dkw = None
    for gi, (window, r) in enumerate(ATT_PAIRS):
        L = S // r
        nq, rb = _attn_plan(r)
        dspec = lambda t: Arg(t.reshape(L, r * ATT_GW), (nq * ATT_BLOCK, rb * ATT_GW), lambda rho, nb: (nb, rho))
        dq, dk, dv, dqw_g, dkw_g = scan_bwd(n(f"attn{gi}_bwd"), _attn_tile, _attn_grid(r, S), _attn_carry(r),
                                            _attn_args(QKV, gi, r, qw, kw, S), sv['att_ck'][gi],
                                            [dspec(cres[2 * gi]), dspec(cres[2 * gi + 1])], [0, 1, 2, 3, 4])
        dqs.append(dq.reshape(S, ATT_GW))
        dks.append(dk.reshape(S, ATT_GW))
        dvs.append(dv.reshape(S, ATT_GW))
        dqw = dqw_g if dqw is None else dqw + dqw_g
        dkw = dkw_g if dkw is None else dkw + dkw_g
    gr['q_norm_w'], gr['k_norm_w'] = dqw.reshape(-1), dkw.reshape(-1)

    ssd_args = _ssd_args(C, A, W, S)
    sres = scan_bwd(n("ssd_bwd"), _ssd_tile, (1, S // SSD_CHUNK), _SSD_CARRY, ssd_args, sv['ssd_ck'],
                    [Arg(dyc, (SSD_CHUNK, SSD_WIDTH), lambda o, t: (t, 0))], list(range(9)))
    dxbc, ddt, dzc = sres[0], sres[1], sres[2]
    gr['conv_w'] = sres[3]
    gr['conv_b'] = sres[4].reshape(-1)
    gr['dt_bias'] = sres[5].reshape(-1)[:SSD_HEADS]
    gr['ssd_a_log'] = sres[6].reshape(-1)[:SSD_HEADS]
    gr['ssd_d'] = sres[7].reshape(-1)[:SSD_HEADS]
    gr['ssd_norm_w'] = sres[8].reshape(-1)

    dpieces = [jnp.concatenate([dua, dza, ddt], axis=1), jnp.concatenate(dqs + dks + dvs, axis=1),
               jnp.concatenate([dxbc, dzb, dzc], axis=1), jnp.concatenate([dg0, dg1, dg2], axis=1)]
    dh = matmul_nt_sum(n("d_h"), dpieces, list(W['w_in_pieces']))
    gr['w_in'] = _unrelayout_w_in_grad([wgrad(n(f"g_w_in{j}"), sv['h'], dp) for j, dp in enumerate(dpieces)])
    nargs = [_rows(x, 512), _whole(W['norm_w'].reshape(1, -1), 'acc')]
    dx, dnw = map_bwd(n("norm_bwd"), _rmsnorm_tile, (S // 512,), nargs, [_rows(dh, 512)], [0, 1], add={0: _rows(dout, 512)})
    gr['norm_w'] = dnw.reshape(-1)
    return dx, gr


def _unrelayout_w_in_grad(pieces):
    gA, gQ, gC, gG = pieces
    uaza, dt = gA[:, :2 * S5_WIDTH], gA[:, 2 * S5_WIDTH:2 * S5_WIDTH + SSD_HEADS]
    xbc, zb, zc = gC[:, :SSD_CONV_DIM], gC[:, SSD_CONV_DIM:SSD_CONV_DIM + ATT_GW], gC[:, SSD_CONV_DIM + ATT_GW:]
    return jnp.concatenate([uaza, gQ, zb, xbc, dt, zc, gG], axis=1)


def kernel(x, norm_w, w_in, s5_a_re, s5_a_im, s5_log_step, s5_b_re, s5_b_im, s5_c_re, s5_c_im, s5_d, s5_glu_w, s5_glu_b, q_norm_w, k_norm_w, conv_w, conv_b, dt_bias, ssd_a_log, ssd_d, ssd_norm_w, proj_a, proj_b, proj_c, w_out, loss_target, m_norm_w, m_w_in, m_s5_a_re, m_s5_a_im, m_s5_log_step, m_s5_b_re, m_s5_b_im, m_s5_c_re, m_s5_c_im, m_s5_d, m_s5_glu_w, m_s5_glu_b, m_q_norm_w, m_k_norm_w, m_conv_w, m_conv_b, m_dt_bias, m_ssd_a_log, m_ssd_d, m_ssd_norm_w, m_proj_a, m_proj_b, m_proj_c, m_w_out, v_norm_w, v_w_in, v_s5_a_re, v_s5_a_im, v_s5_log_step, v_s5_b_re, v_s5_b_im, v_s5_c_re, v_s5_c_im, v_s5_d, v_s5_glu_w, v_s5_glu_b, v_q_norm_w, v_k_norm_w, v_conv_w, v_conv_b, v_dt_bias, v_ssd_a_log, v_ssd_d, v_ssd_norm_w, v_proj_a, v_proj_b, v_proj_c, v_w_out):
    args = dict(locals())
    w = {k: args[k] for k in WEIGHTS}
    m = {k: args['m_' + k] for k in WEIGHTS}
    v = {k: args['v_' + k] for k in WEIGHTS}
    depth = norm_w.shape[0]
    S = x.shape[1]
    xs = x.reshape(S, D_MODEL)
    tgt = loss_target.reshape(S, D_MODEL)

    keys = [(li, k) for li in range(depth) for k in SHARDED]
    gathered = gather_two_level("gather_weights", [w[k][li].astype(bf16) for li, k in keys])
    layers = [{k: w[k][li] for k in WEIGHTS if k not in SHARDED} for li in range(depth)]
    for (li, k), t in zip(keys, gathered):
        n_dev, R, C = t.shape
        layers[li][k] = t.reshape(n_dev * R, C) if k in ROW_SHARDED else t.transpose(1, 0, 2).reshape(R, n_dev * C)
    for W in layers:
        W['w_in_pieces'] = _relayout_w_in(W['w_in'])

    act, saved = xs, []
    for li in range(depth):
        act, sv = layer_fwd(li, act, layers[li])
        saved.append(sv)
    dy, loss_local = loss_and_grad(act, tgt)
    loss = lax.psum(loss_local, ("x", "y", "c"))

    grads = [None] * depth
    for li in reversed(range(depth)):
        dy, grads[li] = layer_bwd(li, dy, saved[li], layers[li])
    grad_x = dy.reshape(x.shape)

    big_keys = [(li, k) for li in range(depth) for k in SHARDED if k != 'conv_w']
    slots = exchange("scatter_grads", [grads[li][k] for li, k in big_keys],
                     ['rows' if k in ROW_SHARDED else 'cols' for _, k in big_keys])
    per_layer = {}
    for (li, k), s in zip(big_keys, slots):
        per_layer[li, k] = adamw(f"adamw_{k}{li}", w[k][li], s, m[k][li], v[k][li])
    result = {k: tuple(jnp.stack([per_layer[li, k][j] for li in range(depth)], axis=0) for j in range(4))
              for k in SHARDED if k != 'conv_w'}

    small_keys = [k for k in WEIGHTS if k not in SHARDED] + ['conv_w']
    stacked = [jnp.stack([grads[li][k] for li in range(depth)], axis=0) for k in small_keys]
    (small_slots,) = exchange("gather_small_grads", [_pack(stacked)], ['all'])
    totals = _unpack(sum_slots("sum_small_grads", small_slots), [t.shape for t in stacked])
    for k, g in zip(small_keys, totals):
        if k == 'conv_w':
            width = w[k].shape[-1]
            me = 4 * lax.axis_index("x") + 2 * lax.axis_index("y") + lax.axis_index("c")
            g = lax.dynamic_slice_in_dim(g, me * width, width, axis=2)
        result[k] = adamw("adamw_" + k, w[k], g[None], m[k], v[k])

    return (loss, grad_x, *[result[k][0] for k in WEIGHTS], *[result[k][1] for k in WEIGHTS],
            *[result[k][2] for k in WEIGHTS], *[result[k][3] for k in WEIGHTS])
```

```python
import functools
import math
from typing import Any, NamedTuple

import jax
import jax.numpy as jnp
from jax import lax
from jax.experimental import pallas as pl
from jax.experimental.pallas import tpu as pltpu

f32 = jnp.float32
bf16 = jnp.bfloat16

N_DEV = 8
D_MODEL = 1024
RMS_EPS = 1e-6
S5_WIDTH = 512
S5_GROUPS = 32
S5_GROUP = 16
S5_STATE = 64
S5_TILE = 256
S5_SUB = 8
S5_ND = 3
S5_CHUNKS = 4
ATT_HEAD_DIM = 64
ATT_PAIRS = ((128, 1), (512, 4), (2048, 16))
ATT_HPG = 4
ATT_BLOCK = 128
ATT_GW = ATT_HPG * ATT_HEAD_DIM
ATT_WIDTH = 768
SSD_HEADS = 12
SSD_HEAD_DIM = 64
SSD_WIDTH = 768
SSD_STATE = 128
SSD_GROUPS = 2
SSD_CHUNK = 128
SSD_CONV = 4
SSD_CONV_DIM = 1280
HPAD = 128
IN_SPLITS = (512, 512, 768, 768, 768, 256, 1280, 12, 768, 3072)
ADAM_LR, ADAM_B1, ADAM_B2, ADAM_EPS, ADAM_WD, ADAM_STEP = 0.001, 0.9, 0.999, 1e-08, 0.01, 10
VMEM_LIMIT = 56 * 1024 * 1024

WEIGHTS = ['norm_w', 'w_in', 's5_a_re', 's5_a_im', 's5_log_step', 's5_b_re', 's5_b_im', 's5_c_re',
           's5_c_im', 's5_d', 's5_glu_w', 's5_glu_b', 'q_norm_w', 'k_norm_w', 'conv_w', 'conv_b',
           'dt_bias', 'ssd_a_log', 'ssd_d', 'ssd_norm_w', 'proj_a', 'proj_b', 'proj_c', 'w_out']
ROW_SHARDED = ('w_in', 's5_glu_w', 'w_out')
SHARDED = ROW_SHARDED + ('conv_w', 'proj_a', 'proj_b', 'proj_c')


class Arg(NamedTuple):
    arr: Any
    block: tuple
    imap: Any
    kind: str = 'const'
    gshape: Any = None
    gimap: Any = None
    gdtype: Any = None


class Out(NamedTuple):
    shape: tuple
    dtype: Any
    block: tuple
    imap: Any


def _cparams(n):
    return pltpu.CompilerParams(dimension_semantics=("arbitrary",) * n, vmem_limit_bytes=VMEM_LIMIT)


def _rows(a, tm, kind='tile', col=0, width=None, gshape=None, gcol=None, gdtype=None):
    width = a.shape[1] if width is None else width
    g = None if gshape is None else (lambda i, gc=(0 if gcol is None else gcol): (i, gc))
    return Arg(a, (tm, width), lambda i, c=col: (i, c), kind, gshape, g, gdtype)


def _whole(a, kind='const'):
    nd = a.ndim
    return Arg(a, a.shape, lambda *i, nd=nd: (0,) * nd, kind)


def map_fwd(name, fn, grid, args, outs):
    n_in = len(args)

    def body(*refs):
        pid = tuple(pl.program_id(a) for a in range(len(grid)))
        res = fn(pid, *[r[...] for r in refs[:n_in]])
        for o, r in zip(refs[n_in:], res):
            o[...] = r.astype(o.dtype)

    res = pl.pallas_call(
        body, name=name, grid=grid,
        in_specs=[pl.BlockSpec(a.block, a.imap) for a in args],
        out_specs=[pl.BlockSpec(o.block, o.imap) for o in outs],
        out_shape=[jax.ShapeDtypeStruct(o.shape, o.dtype) for o in outs],
        compiler_params=_cparams(len(grid)))(*[a.arr for a in args])
    return tuple(res)


def _grad_outs(args, wrt):
    outs = []
    for i in wrt:
        a = args[i]
        shape = a.arr.shape if a.gshape is None else a.gshape
        imap = a.imap if a.gimap is None else a.gimap
        outs.append(Out(shape, f32 if a.gdtype is None else a.gdtype, a.block, imap))
    return outs


def _store_grads(pid, args, wrt, grads, grefs, adds):
    first_all = functools.reduce(jnp.logical_and, [p == 0 for p in pid])
    first_in = functools.reduce(jnp.logical_and, [p == 0 for p in pid[1:]]) if len(pid) > 1 else first_all
    for j, i in enumerate(wrt):
        g = grads[j].astype(f32)
        ref = grefs[j]
        kind = args[i].kind
        if kind == 'tile':
            if j in adds:
                g = g + adds[j]
            ref[...] = g.astype(ref.dtype)
        else:
            first = first_all if kind == 'acc' else first_in

            @pl.when(first)
            def _(ref=ref):
                ref[...] = jnp.zeros_like(ref)

            ref[...] += g


def map_bwd(name, fn, grid, args, douts, wrt, add=None):
    add = add or {}
    n_in, n_d, n_add = len(args), len(douts), len(add)
    add_keys = sorted(add)
    gouts = _grad_outs(args, wrt)

    def body(*refs):
        pid = tuple(pl.program_id(a) for a in range(len(grid)))
        vals = [r[...] for r in refs[:n_in]]
        dvals = [r[...].astype(f32) for r in refs[n_in:n_in + n_d]]
        avals = {k: refs[n_in + n_d + j][...].astype(f32) for j, k in enumerate(add_keys)}
        grefs = refs[n_in + n_d + n_add:]

        def f(*w):
            full = list(vals)
            for i, x in zip(wrt, w):
                full[i] = x
            return tuple(fn(pid, *full))

        _, vjp = jax.vjp(f, *[vals[i] for i in wrt])
        grads = vjp(tuple(dvals))
        _store_grads(pid, args, wrt, grads, grefs, avals)

    ins = list(args) + list(douts) + [add[k] for k in add_keys]
    res = pl.pallas_call(
        body, name=name, grid=grid,
        in_specs=[pl.BlockSpec(a.block, a.imap) for a in ins],
        out_specs=[pl.BlockSpec(o.block, o.imap) for o in gouts],
        out_shape=[jax.ShapeDtypeStruct(o.shape, o.dtype) for o in gouts],
        compiler_params=_cparams(len(grid)))(*[a.arr for a in ins])
    return tuple(res)


class Side(NamedTuple):
    arrs: list
    out_shapes: list
    sem_shapes: list
    phases: list


def _run_side(side, step, total, src_refs, out_refs, sem_refs):
    for frac, phase in side.phases:
        @pl.when(step == int(round(frac * (total - 1))))
        def _(phase=phase):
            phase(src_refs, out_refs, *sem_refs)


_ANY = pl.BlockSpec(memory_space=pl.ANY)


def scan_fwd(name, fn, grid, carry_shapes, args, outs, side=None):
    no, nt = grid
    n_in, n_out, n_c = len(args), len(outs), len(carry_shapes)
    ns_in, ns_out = (len(side.arrs), len(side.out_shapes)) if side else (0, 0)
    cks = [Out((no, nt) + cs, f32, (None, None) + cs, lambda o, t, n=len(cs): (o, t) + (0,) * n) for cs in carry_shapes]

    def body(*refs):
        pid = (pl.program_id(0), pl.program_id(1))
        ins = refs[:n_in]
        sins = refs[n_in:n_in + ns_in]
        refs = refs[n_in + ns_in:]
        orefs = refs[:n_out]
        ckrefs = refs[n_out:n_out + n_c]
        souts = refs[n_out + n_c:n_out + n_c + ns_out]
        crefs = refs[n_out + n_c + ns_out:n_out + n_c + ns_out + n_c]
        if side:
            _run_side(side, pid[0] * nt + pid[1], no * nt, sins, souts, refs[n_out + n_c + ns_out + n_c:])

        @pl.when(pid[1] == 0)
        def _():
            for c in crefs:
                c[...] = jnp.zeros_like(c)

        carry = tuple(c[...] for c in crefs)
        for ck, c in zip(ckrefs, carry):
            ck[...] = c
        res, newc = fn(pid, carry, *[r[...] for r in ins])
        for o, r in zip(orefs, res):
            o[...] = r.astype(o.dtype)
        for c, v in zip(crefs, newc):
            c[...] = v

    allouts = list(outs) + cks
    res = pl.pallas_call(
        body, name=name, grid=grid,
        in_specs=[pl.BlockSpec(a.block, a.imap) for a in args] + [_ANY] * ns_in,
        out_specs=[pl.BlockSpec(o.block, o.imap) for o in allouts] + [_ANY] * ns_out,
        out_shape=[jax.ShapeDtypeStruct(o.shape, o.dtype) for o in allouts] + (list(side.out_shapes) if side else []),
        scratch_shapes=[pltpu.VMEM(cs, f32) for cs in carry_shapes] + (list(side.sem_shapes) if side else []),
        compiler_params=_cparams(2))(*[a.arr for a in args], *(side.arrs if side else []))
    if side:
        return tuple(res[:n_out]), tuple(res[n_out:n_out + n_c]), list(res[n_out + n_c:])
    return tuple(res[:n_out]), tuple(res[n_out:])


def scan_bwd(name, fn, grid, carry_shapes, args, ckpts, douts, wrt, bwd_fn=None, side=None):
    no, nt = grid
    n_in, n_d, n_c = len(args), len(douts), len(carry_shapes)
    ns_in, ns_out = (len(side.arrs), len(side.out_shapes)) if side else (0, 0)

    def rev(imap):
        return lambda o, t: imap(o, nt - 1 - t)

    rargs = [a._replace(imap=rev(a.imap), gimap=None if a.gimap is None else rev(a.gimap)) for a in args]
    rdouts = [a._replace(imap=rev(a.imap)) for a in douts]
    ckargs = [Arg(ck, (None, None) + cs, rev(lambda o, t, n=len(cs): (o, t) + (0,) * n)) for ck, cs in zip(ckpts, carry_shapes)]
    gouts = _grad_outs(rargs, wrt)

    def body(*refs):
        o, t = pl.program_id(0), pl.program_id(1)
        tt = nt - 1 - t
        vals = [r[...] for r in refs[:n_in]]
        dvals = [r[...].astype(f32) for r in refs[n_in:n_in + n_d]]
        carry = tuple(r[...] for r in refs[n_in + n_d:n_in + n_d + n_c])
        sins = refs[n_in + n_d + n_c:n_in + n_d + n_c + ns_in]
        refs = refs[n_in + n_d + n_c + ns_in:]
        grefs = refs[:len(wrt)]
        souts = refs[len(wrt):len(wrt) + ns_out]
        dcrefs = refs[len(wrt) + ns_out:len(wrt) + ns_out + n_c]
        if side:
            _run_side(side, o * nt + t, no * nt, sins, souts, refs[len(wrt) + ns_out + n_c:])

        @pl.when(t == 0)
        def _():
            for c in dcrefs:
                c[...] = jnp.zeros_like(c)

        def f(carry, *w):
            full = list(vals)
            for i, x in zip(wrt, w):
                full[i] = x
            res, newc = fn((o, tt), carry, *full)
            return tuple(res), tuple(newc)

        dcarry = tuple(c[...] for c in dcrefs)
        if bwd_fn is None:
            _, vjp = jax.vjp(f, carry, *[vals[i] for i in wrt])
            grads = vjp((tuple(dvals), dcarry))
            dcarry_in, grads = grads[0], grads[1:]
        else:
            dcarry_in, grads = bwd_fn((o, tt), carry, vals, dvals, dcarry)
        for c, g in zip(dcrefs, dcarry_in):
            c[...] = g
        _store_grads((o, t), rargs, wrt, grads, grefs, {})

    ins = rargs + rdouts + ckargs
    res = pl.pallas_call(
        body, name=name, grid=grid,
        in_specs=[pl.BlockSpec(a.block, a.imap) for a in ins] + [_ANY] * ns_in,
        out_specs=[pl.BlockSpec(g.block, g.imap) for g in gouts] + [_ANY] * ns_out,
        out_shape=[jax.ShapeDtypeStruct(g.shape, g.dtype) for g in gouts] + (list(side.out_shapes) if side else []),
        scratch_shapes=[pltpu.VMEM(cs, f32) for cs in carry_shapes] + (list(side.sem_shapes) if side else []),
        compiler_params=_cparams(2))(*[a.arr for a in ins], *(side.arrs if side else []))
    if side:
        return tuple(res[:len(gouts)]), list(res[len(gouts):])
    return tuple(res)


def _pick(dim, target):
    if dim <= target:
        return dim
    for unit in (256, 128):
        for t in range(target // unit * unit, unit - 1, -unit):
            if dim % t == 0:
                return t
    return dim


def matmul(name, a, b, mode='nn', add=None, out_dtype=f32, tm=None, tn=1152, tk=None):
    if mode == 'tn':
        K, M = a.shape
    else:
        M, K = a.shape
    N = b.shape[0] if mode == 'nt' else b.shape[1]
    assert (b.shape[1] if mode == 'nt' else b.shape[0]) == K
    tm = (1024 if mode == 'tn' else 512) if tm is None else tm
    tk = (512 if mode == 'tn' else 1152) if tk is None else tk
    tm, tn, tk = _pick(M, tm), _pick(N, tn), _pick(K, tk)
    nk = K // tk
    a_spec = pl.BlockSpec((tk, tm), lambda i, j, k: (k, i)) if mode == 'tn' else pl.BlockSpec((tm, tk), lambda i, j, k: (i, k))
    b_spec = pl.BlockSpec((tn, tk), lambda i, j, k: (j, k)) if mode == 'nt' else pl.BlockSpec((tk, tn), lambda i, j, k: (k, j))
    dims = {'nn': (((1,), (0,)), ((), ())), 'nt': (((1,), (1,)), ((), ())), 'tn': (((0,), (0,)), ((), ()))}[mode]
    has_add = add is not None

    def body(*refs):
        if has_add:
            a_ref, b_ref, add_ref, o_ref, acc = refs
        else:
            a_ref, b_ref, o_ref, acc = refs
        k = pl.program_id(2)

        @pl.when(k == 0)
        def _():
            acc[...] = add_ref[...].astype(f32) if has_add else jnp.zeros_like(acc)

        acc[...] += lax.dot_general(a_ref[...].astype(bf16), b_ref[...].astype(bf16), dims, preferred_element_type=f32)

        @pl.when(k == nk - 1)
        def _():
            o_ref[...] = acc[...].astype(o_ref.dtype)

    in_specs = [a_spec, b_spec] + ([pl.BlockSpec((tm, tn), lambda i, j, k: (i, j))] if has_add else [])
    ops = [a, b] + ([add] if has_add else [])
    return pl.pallas_call(
        body, name=name, grid=(M // tm, N // tn, nk), in_specs=in_specs,
        out_specs=pl.BlockSpec((tm, tn), lambda i, j, k: (i, j)),
        out_shape=jax.ShapeDtypeStruct((M, N), out_dtype),
        scratch_shapes=[pltpu.VMEM((tm, tn), f32)],
        compiler_params=pltpu.CompilerParams(dimension_semantics=("parallel", "parallel", "arbitrary"), vmem_limit_bytes=VMEM_LIMIT))(*ops)


def matmul_nt_sum(name, lhs, rhs, tm=512, tk=1152):
    M, N = lhs[0].shape[0], rhs[0].shape[0]
    tm = _pick(M, tm)
    tks = [_pick(a.shape[1], tk) for a in lhs]
    starts, total = [], 0
    for a, t in zip(lhs, tks):
        starts.append(total)
        total += a.shape[1] // t
    npc = len(lhs)

    def body(*refs):
        a_refs, b_refs, o_ref, acc = refs[:npc], refs[npc:2 * npc], refs[2 * npc], refs[2 * npc + 1]
        k = pl.program_id(1)

        @pl.when(k == 0)
        def _():
            acc[...] = jnp.zeros_like(acc)

        for p in range(npc):
            @pl.when((k >= starts[p]) & (k < starts[p] + lhs[p].shape[1] // tks[p]))
            def _(p=p):
                acc[...] += lax.dot_general(a_refs[p][...].astype(bf16), b_refs[p][...].astype(bf16), _NT, preferred_element_type=f32)

        @pl.when(k == total - 1)
        def _():
            o_ref[...] = acc[...]

    def kblock(p):
        return lambda k: jnp.clip(k - starts[p], 0, lhs[p].shape[1] // tks[p] - 1)

    in_specs = [pl.BlockSpec((tm, tks[p]), lambda i, k, kb=kblock(p): (i, kb(k))) for p in range(npc)]
    in_specs += [pl.BlockSpec((N, tks[p]), lambda i, k, kb=kblock(p): (0, kb(k))) for p in range(npc)]
    return pl.pallas_call(
        body, name=name, grid=(M // tm, total), in_specs=in_specs,
        out_specs=pl.BlockSpec((tm, N), lambda i, k: (i, 0)),
        out_shape=jax.ShapeDtypeStruct((M, N), f32),
        scratch_shapes=[pltpu.VMEM((tm, N), f32)],
        compiler_params=pltpu.CompilerParams(dimension_semantics=("parallel", "arbitrary"), vmem_limit_bytes=VMEM_LIMIT))(*lhs, *rhs)


def wgrad(name, act, dout):
    return matmul(name, act, dout, 'tn', out_dtype=bf16)


def _dot(a, b, dims=(((1,), (0,)), ((), ()))):
    return lax.dot_general(a.astype(bf16), b.astype(bf16), dims, preferred_element_type=f32)


_NT = (((1,), (1,)), ((), ()))
_TN = (((0,), (0,)), ((), ()))


def _three_term_dot(v, sel, dims):
    hi = v.astype(bf16)
    rest = v - hi.astype(f32)
    mid = rest.astype(bf16)
    lo = (rest - mid.astype(f32)).astype(bf16)
    dot = lambda t: lax.dot_general(t, sel, dims, preferred_element_type=f32)
    return dot(hi) + dot(mid) + dot(lo)


@jax.custom_vjp
def _dot_exact01(v, sel):
    return _three_term_dot(v, sel, (((1,), (0,)), ((), ())))


def _dot_exact01_fwd(v, sel):
    return _dot_exact01(v, sel), sel


def _dot_exact01_bwd(sel, ct):
    return _three_term_dot(ct, sel, _NT), jnp.zeros_like(sel)


_dot_exact01.defvjp(_dot_exact01_fwd, _dot_exact01_bwd)


def _spread_heads(v, width):
    r = lax.broadcasted_iota(jnp.int32, (HPAD, SSD_HEADS * width), 0)
    c = lax.broadcasted_iota(jnp.int32, (HPAD, SSD_HEADS * width), 1)
    return _dot_exact01(v, (r == c // width).astype(bf16))


def _rmsnorm_tile(pid, x, w):
    return (x * lax.rsqrt(jnp.mean(x * x, axis=-1, keepdims=True) + RMS_EPS) * w,)


def _shift_rows(h, d, fill):
    pad = jnp.full((d, h.shape[1]), fill, f32)
    return jnp.concatenate([pad, h[:-d]], axis=0)


def _s5_prep_tile(pid, a_re, a_im, ls, btr, bti, ctr, cti):
    o = pid[0]
    w = a_re.shape[1]
    r = lax.broadcasted_iota(jnp.int32, (S5_GROUPS, w), 0)
    c = lax.broadcasted_iota(jnp.int32, (S5_GROUPS, w), 1)
    sel = (r == o * (w // S5_STATE) + c // S5_STATE).astype(f32)
    step = jnp.dot(jnp.exp(ls), sel, precision=lax.Precision.HIGHEST, preferred_element_type=f32)
    mag = jnp.exp(a_re * step)
    ang = a_im * step
    lr, li = mag * jnp.cos(ang), mag * jnp.sin(ang)
    nr, ni = lr - 1.0, li
    den = a_re * a_re + a_im * a_im
    fr = (nr * a_re + ni * a_im) / den
    fi = (ni * a_re - nr * a_im) / den
    bbr = fr * btr - fi * bti
    bbi = fr * bti + fi * btr
    reps = w // S5_STATE
    rr = lax.broadcasted_iota(jnp.int32, (reps * S5_GROUP, w), 0)
    cc = lax.broadcasted_iota(jnp.int32, (reps * S5_GROUP, w), 1)
    diag = (rr // S5_GROUP) == (cc // S5_STATE)

    def expand(m):
        return jnp.where(diag, jnp.concatenate([m] * reps, axis=0), 0.0)

    pr, pi = lr, li
    rows_r, rows_i = [pr], [pi]
    for _ in range(S5_ND - 1):
        pr, pi = pr * pr - pi * pi, 2.0 * pr * pi
        rows_r.append(pr)
        rows_i.append(pi)
    lamd_r, lamd_i = jnp.concatenate(rows_r, axis=0), jnp.concatenate(rows_i, axis=0)
    tr = jnp.broadcast_to(lr, (S5_SUB, w))
    ti = jnp.broadcast_to(li, (S5_SUB, w))
    for j in range(S5_ND):
        sr, si = _shift_rows(tr, 1 << j, 1.0), _shift_rows(ti, 1 << j, 0.0)
        tr, ti = tr * sr - ti * si, tr * si + ti * sr
    return lamd_r, lamd_i, tr, ti, expand(bbr), expand(bbi), expand(ctr), expand(cti)


def _s5_tile(pid, carry, u, lamd_r, lamd_i, lam8_r, lam8_i, bbr, bbi, ccr, cci, dvec):
    hr, hi = _s5_scan(_dot(u, bbr), _dot(u, bbi), carry, lamd_r, lamd_i, lam8_r, lam8_i, reverse=False)
    return (_s5_readout(hr, hi, u, ccr, cci, dvec),), (hr[-1:], hi[-1:])


def _s5_readout(hr, hi, u, ccr, cci, dvec):
    return jax.nn.gelu(_dot(hr, ccr, _NT) - _dot(hi, cci, _NT) + dvec * u)


def _s5_scan(xr, xi, carry, lamd_r, lamd_i, lam8_r, lam8_i, reverse):
    cr, ci = carry
    T, G = xr.shape[0], S5_SUB
    sign = -1.0 if reverse else 1.0
    sub = lax.broadcasted_iota(jnp.int32, (T, 1), 0) % G
    for j in range(S5_ND):
        d = 1 << j
        if reverse:
            keep = sub < G - d
            sr = jnp.where(keep, jnp.concatenate([xr[d:], jnp.zeros((d, xr.shape[1]), f32)], axis=0), 0.0)
            si = jnp.where(keep, jnp.concatenate([xi[d:], jnp.zeros((d, xi.shape[1]), f32)], axis=0), 0.0)
        else:
            keep = sub >= d
            sr = jnp.where(keep, _shift_rows(xr, d, 0.0), 0.0)
            si = jnp.where(keep, _shift_rows(xi, d, 0.0), 0.0)
        ar, ai = lamd_r[j:j + 1], sign * lamd_i[j:j + 1]
        xr, xi = xr + ar * sr - ai * si, xi + ar * si + ai * sr
    if reverse:
        pr = jnp.concatenate([lam8_r[G - 1 - s:G - s] for s in range(G)], axis=0)
        pi = -jnp.concatenate([lam8_i[G - 1 - s:G - s] for s in range(G)], axis=0)
    else:
        pr, pi = lam8_r, lam8_i
    n = T // G
    rows_r, rows_i = [None] * n, [None] * n
    for i in (reversed(range(n)) if reverse else range(n)):
        gr_, gi_ = xr[i * G:(i + 1) * G], xi[i * G:(i + 1) * G]
        gr_, gi_ = gr_ + pr * cr - pi * ci, gi_ + pr * ci + pi * cr
        cr, ci = (gr_[:1], gi_[:1]) if reverse else (gr_[G - 1:], gi_[G - 1:])
        rows_r[i], rows_i[i] = gr_, gi_
    return jnp.concatenate(rows_r, axis=0), jnp.concatenate(rows_i, axis=0)


def _s5_tile_bwd(pid, carry, vals, douts, dcarry):
    u, lamd_r, lamd_i, lam8_r, lam8_i, bbr, bbi, ccr, cci, dvec = vals
    (dg,) = douts
    hr, hi = _s5_scan(_dot(u, bbr), _dot(u, bbi), carry, lamd_r, lamd_i, lam8_r, lam8_i, reverse=False)
    _, vjp = jax.vjp(_s5_readout, hr, hi, u, ccr, cci, dvec)
    dhr, dhi, du, dccr, dcci, ddvec = vjp(dg)
    Hr, Hi = _s5_scan(dhr, dhi, dcarry, lamd_r, lamd_i, lam8_r, lam8_i, reverse=True)
    _, vjp_in = jax.vjp(lambda u, bbr, bbi: (_dot(u, bbr), _dot(u, bbi)), u, bbr, bbi)
    du2, dbbr, dbbi = vjp_in((Hr, Hi))
    pr = jnp.concatenate([carry[0], hr[:-1]], axis=0)
    pi = jnp.concatenate([carry[1], hi[:-1]], axis=0)
    dlam_r = jnp.sum(Hr * pr + Hi * pi, axis=0, keepdims=True)
    dlam_i = jnp.sum(Hi * pr - Hr * pi, axis=0, keepdims=True)
    zrow = jnp.zeros((S5_ND - 1, dlam_r.shape[1]), f32)
    dlamd_r, dlamd_i = jnp.concatenate([dlam_r, zrow], axis=0), jnp.concatenate([dlam_i, zrow], axis=0)
    grads = (du + du2, dlamd_r, dlamd_i, jnp.zeros_like(lam8_r), jnp.zeros_like(lam8_i), dbbr, dbbi, dccr, dcci, ddvec)
    return (Hr[:1], Hi[:1]), grads


def _glu_tile(pid, g, glu, za, b):
    return (g * jax.nn.sigmoid(glu + b) * jax.nn.silu(za),)


def _attn_tile(pid, carry, q, k, v, qw, kw):
    n = pid[1]
    kp, vp = carry
    D, B = ATT_HEAD_DIM, ATT_BLOCK
    W = 2 * D
    nq, ncol = q.shape[0] // B, q.shape[1] // W
    r = lax.broadcasted_iota(jnp.int32, (B, 2 * B), 0)
    c = lax.broadcasted_iota(jnp.int32, (B, 2 * B), 1)
    diff = r + B - c
    band = (diff >= 0) & (diff <= B)
    band_first = band & ((c >= B) | (n > 0))
    low = lax.broadcasted_iota(jnp.int32, (1, W), 1) < D
    same_head = (lax.broadcasted_iota(jnp.int32, (W, W), 0) // D == lax.broadcasted_iota(jnp.int32, (W, W), 1) // D).astype(bf16)

    def hnorm(x, w):
        rows = x.shape[0]
        t = jnp.concatenate([x[:, j * W:(j + 1) * W] for j in range(ncol)], axis=0) if ncol > 1 else x
        ms = _dot_exact01(t * t, same_head) * (1.0 / D)
        t = t * lax.rsqrt(ms + RMS_EPS) * jnp.concatenate([w, w], axis=1)
        return [t[j * rows:(j + 1) * rows] for j in range(ncol)]

    qns, kns = hnorm(q, qw), hnorm(k, kw)
    out_cols, lse_cols, kn_cols = [], [], []
    for j in range(ncol):
        sl = slice(j * W, (j + 1) * W)
        qn, kn, vj = qns[j], kns[j], v[:, sl]
        kn_cols.append(kn[(nq - 1) * B:])
        outs, lses = [], []
        for b in range(nq):
            rows = slice(b * B, (b + 1) * B)
            prev = slice((b - 1) * B, b * B)
            kk = jnp.concatenate([kp[:, sl] if b == 0 else kn[prev], kn[rows]], axis=0)
            vv = jnp.concatenate([vp[:, sl] if b == 0 else vj[prev], vj[rows]], axis=0)
            o2, l2 = [], []
            for head_lanes in (low, ~low):
                s = _dot(jnp.where(head_lanes, qn[rows], 0.0), kk, _NT) * (D ** -0.5)
                s = jnp.where(band_first if b == 0 else band, s, -1e30)
                m = jnp.max(s, axis=-1, keepdims=True)
                p = jnp.exp(s - m)
                l = jnp.sum(p, axis=-1, keepdims=True)
                o2.append(_dot(p / l, vv))
                l2.append(m + jnp.log(l))
            outs.append(jnp.where(low, o2[0], o2[1]))
            lses.append(jnp.where(low, l2[0], l2[1]))
        out_cols.append(jnp.concatenate(outs, axis=0) if nq > 1 else outs[0])
        lse_cols.append(jnp.concatenate(lses, axis=0) if nq > 1 else lses[0])
    return ((jnp.concatenate(out_cols, axis=1), jnp.concatenate(lse_cols, axis=1)),
            (jnp.concatenate(kn_cols, axis=1), v[(nq - 1) * B:]))


def _combine_tile(pid, o1, l1, o2, l2, o3, l3, zb):
    m = jnp.maximum(jnp.maximum(l1, l2), l3)
    e1, e2, e3 = jnp.exp(l1 - m), jnp.exp(l2 - m), jnp.exp(l3 - m)
    y = (e1 * o1 + e2 * o2 + e3 * o3) / (e1 + e2 + e3)
    return (y * jax.nn.silu(zb),)


def _softplus(x):
    return jnp.maximum(x, 0.0) + jnp.log(1.0 + jnp.exp(-jnp.abs(x)))


def _ssd_tile(pid, carry, xbc, dt, z, conv_w, conv_b, dt_bias, a_log, dvec, norm_w):
    xprev, state = carry
    T, P, N = SSD_CHUNK, SSD_HEAD_DIM, SSD_STATE
    xx = jnp.concatenate([xprev, xbc], axis=0)
    conv = conv_b
    for k in range(SSD_CONV):
        off = 8 - (SSD_CONV - 1) + k
        conv = conv + conv_w[k:k + 1] * xx[off:off + T]
    xc = jax.nn.silu(conv)
    dtp = _softplus(dt + dt_bias)
    a_dt = dtp * (-jnp.exp(a_log))
    r = lax.broadcasted_iota(jnp.int32, (T, T), 0)
    c = lax.broadcasted_iota(jnp.int32, (T, T), 1)
    tri = r >= c
    trif = tri.astype(f32)
    hi = lax.Precision.HIGHEST
    a_cs = jnp.dot(trif, a_dt, precision=hi, preferred_element_type=f32)
    a_cs_t = lax.dot_general(a_dt, trif, (((0,), (1,)), ((), ())), precision=hi, preferred_element_type=f32)
    xs = xc[:, :SSD_WIDTH]
    acs_p = _spread_heads(a_cs, P)
    acs_t = _spread_heads(a_cs, T)
    xdt = xs * _spread_heads(dtp, P)
    skip = _spread_heads(dvec, P)
    to_end = jnp.exp(acs_p[T - 1:T] - acs_p)
    low = lax.broadcasted_iota(jnp.int32, (1, 2 * P), 1) < P
    low_rows = lax.broadcasted_iota(jnp.int32, (2 * P, 1), 0) < P
    ys, states = [], []
    for j in range(SSD_HEADS // 2):
        g = 2 * j // (SSD_HEADS // SSD_GROUPS)
        if 2 * j % (SSD_HEADS // SSD_GROUPS) == 0:
            bg = xc[:, SSD_WIDTH + g * N:SSD_WIDTH + (g + 1) * N]
            cg = xc[:, SSD_WIDTH + SSD_GROUPS * N + g * N:SSD_WIDTH + SSD_GROUPS * N + (g + 1) * N]
            cb = _dot(cg, bg, _NT)
        lanes = slice(2 * j * P, 2 * (j + 1) * P)
        st = state[lanes, :]
        diag, last = [], []
        for h in (2 * j, 2 * j + 1):
            decay = jnp.exp(jnp.where(tri, acs_t[:, h * T:(h + 1) * T] - a_cs_t[h:h + 1, :], -1e30))
            diag.append(_dot(cb * decay, xdt[:, lanes]))
            last.append(jnp.exp(a_cs_t[h:h + 1, T - 1:T]))
        y = (jnp.where(low, diag[0], diag[1]) + _dot(cg, st, _NT) * jnp.exp(acs_p[:, lanes])
             + xs[:, lanes] * skip[:, lanes])
        ys.append(y)
        states.append(jnp.where(low_rows, last[0], last[1]) * st + _dot(xdt[:, lanes] * to_end[:, lanes], bg, _TN))
    y = jnp.concatenate(ys, axis=1) * jax.nn.silu(z)
    out = y * lax.rsqrt(jnp.mean(y * y, axis=-1, keepdims=True) + RMS_EPS) * norm_w
    return (out,), (xbc[T - 8:], jnp.concatenate(states, axis=0))


def _merge_tile(pid, pa, pb, pc, g0, g1, g2):
    return (jax.nn.sigmoid(g0) * pa + jax.nn.sigmoid(g1) * pb + jax.nn.sigmoid(g2) * pc,)


def loss_and_grad(y, target, tm=512):
    S, D = y.shape
    nt = S // tm

    def body(y_ref, t_ref, dy_ref, l_ref, acc):
        i = pl.program_id(0)

        @pl.when(i == 0)
        def _():
            acc[...] = jnp.zeros_like(acc)

        diff = y_ref[...] - t_ref[...]
        dy_ref[...] = diff * (1.0 / D)
        acc[...] += jnp.sum((diff * diff).reshape(tm // 8, 8, D), axis=0)

        @pl.when(i == nt - 1)
        def _():
            l_ref[...] = jnp.broadcast_to(0.5 / D * jnp.sum(acc[...]), l_ref.shape)

    dy, l = pl.pallas_call(
        body, name="loss_head", grid=(nt,),
        in_specs=[pl.BlockSpec((tm, D), lambda i: (i, 0))] * 2,
        out_specs=[pl.BlockSpec((tm, D), lambda i: (i, 0)), pl.BlockSpec((8, 128), lambda i: (0, 0))],
        out_shape=[jax.ShapeDtypeStruct((S, D), f32), jax.ShapeDtypeStruct((8, 128), f32)],
        scratch_shapes=[pltpu.VMEM((8, D), f32)],
        compiler_params=_cparams(1))(y, target)
    return dy, l[0, 0]


def _row_tile(R, C, budget=1 << 20):
    best = R
    for t in range(8, R, 8):
        if R % t == 0 and t * C * 4 <= budget:
            best = t
    if best == R and R * C * 4 > budget:
        for t in range(8, R, 8):
            if R % t == 0:
                return t
    return best


def _as2d(t, lead=0):
    return t.reshape(t.shape[:lead] + (math.prod(t.shape[lead:-1]), t.shape[-1]))


def adamw(name, w, gslots, m, v):
    shape = w.shape
    n = gslots.shape[0]
    C = shape[-1]
    R = math.prod(shape[:-1])
    lanes = -(-C // 128) * 128
    tr = _row_tile(R, lanes * (n + 7), budget=10 << 20)

    def body(w_ref, g_ref, m_ref, v_ref, go_ref, d_ref, nm_ref, nv_ref):
        gg = g_ref[0].astype(f32)
        for s in range(1, n):
            gg = gg + g_ref[s].astype(f32)
        go_ref[...] = gg
        nm = ADAM_B1 * m_ref[...] + (1.0 - ADAM_B1) * gg
        nv = ADAM_B2 * v_ref[...] + (1.0 - ADAM_B2) * jnp.square(gg)
        m_hat = nm / (1.0 - ADAM_B1 ** ADAM_STEP)
        v_hat = nv / (1.0 - ADAM_B2 ** ADAM_STEP)
        d_ref[...] = -ADAM_LR * (m_hat / (jnp.sqrt(v_hat) + ADAM_EPS) + ADAM_WD * w_ref[...])
        nm_ref[...] = nm
        nv_ref[...] = nv

    spec = pl.BlockSpec((tr, C), lambda i: (i, 0))
    res = pl.pallas_call(
        body, name=name, grid=(R // tr,),
        in_specs=[spec, pl.BlockSpec((n, tr, C), lambda i: (0, i, 0)), spec, spec], out_specs=[spec] * 4,
        out_shape=[jax.ShapeDtypeStruct((R, C), f32)] * 4,
        compiler_params=_cparams(1))(w.reshape(R, C), gslots.reshape(n, R, C), m.reshape(R, C), v.reshape(R, C))
    return tuple(t.reshape(shape) for t in res)


PACK_ROWS = 256


def sum_slots(name, x):
    n, R, C = x.shape

    def body(x_ref, o_ref):
        acc = x_ref[0]
        for s in range(1, n):
            acc = acc + x_ref[s]
        o_ref[...] = acc

    return pl.pallas_call(
        body, name=name, grid=(R // PACK_ROWS,),
        in_specs=[pl.BlockSpec((n, PACK_ROWS, C), lambda i: (0, i, 0))],
        out_specs=pl.BlockSpec((PACK_ROWS, C), lambda i: (i, 0)),
        out_shape=jax.ShapeDtypeStruct((R, C), f32), compiler_params=_cparams(1))(x)


def _pack(parts):
    flat = jnp.concatenate([p.reshape(-1) for p in parts])
    unit = 128 * PACK_ROWS
    tot = -(-flat.shape[0] // unit) * unit
    return jnp.pad(flat, (0, tot - flat.shape[0])).reshape(tot // 128, 128)


def _unpack(buf, shapes):
    flat = buf.reshape(-1)
    out, off = [], 0
    for s in shapes:
        size = math.prod(s)
        out.append(flat[off:off + size].reshape(s))
        off += size
    return out


def _comm_sems(nt):
    return [pltpu.SemaphoreType.DMA((nt, N_DEV - 1)), pltpu.SemaphoreType.DMA((nt, N_DEV - 1)), pltpu.SemaphoreType.DMA((nt,))]


def exchange_side(srcs, modes):
    nt = len(srcs)
    slabs = []
    for s, mode in zip(srcs, modes):
        R, C = s.shape
        slabs.append({'all': (R, C), 'rows': (R // N_DEV, C), 'cols': (R, C // N_DEV)}[mode])

    def piece(ref, mode, slab, p):
        if mode == 'all':
            return ref
        if mode == 'rows':
            return ref.at[pl.ds(p * slab[0], slab[0]), :]
        return ref.at[:, pl.ds(p * slab[1], slab[1])]

    def copies(src_refs, out_refs, send_sems, recv_sems, local_sems):
        x, y, c = lax.axis_index("x"), lax.axis_index("y"), lax.axis_index("c")
        me = 4 * x + 2 * y + c
        out = []
        for k in (1, 2, 4, 3, 5, 6, 7):
            px = 1 - x if k & 4 else x
            py = 1 - y if k & 2 else y
            pc = 1 - c if k & 1 else c
            for t in range(nt):
                out.append(pltpu.make_async_remote_copy(
                    src_ref=piece(src_refs[t], modes[t], slabs[t], 4 * px + 2 * py + pc), dst_ref=out_refs[t].at[me],
                    send_sem=send_sems.at[t, k - 1], recv_sem=recv_sems.at[t, k - 1],
                    device_id=(px, py, pc), device_id_type=pl.DeviceIdType.MESH))
        for t in range(nt):
            out.append(pltpu.make_async_copy(piece(src_refs[t], modes[t], slabs[t], me), out_refs[t].at[me], local_sems.at[t]))
        return out

    def start(*refs):
        for cp in copies(*refs):
            cp.start()

    def finish(*refs):
        for cp in copies(*refs):
            cp.wait()

    out_shapes = [jax.ShapeDtypeStruct((N_DEV,) + sl, s.dtype) for s, sl in zip(srcs, slabs)]
    return Side(list(srcs), out_shapes, _comm_sems(nt), [(0.0, start), (1.0, finish)])


def gather_side(srcs):
    nt = len(srcs)

    def plan(src_refs, out_refs, send_sems, recv_sems, local_sems):
        x, y, c = lax.axis_index("x"), lax.axis_index("y"), lax.axis_index("c")
        me, sibling = (x, y, c), (x, y, 1 - c)
        chips = [(1 - x, y), (x, 1 - y), (1 - x, 1 - y)]

        def slot(t, dev):
            return out_refs[t].at[4 * dev[0] + 2 * dev[1] + dev[2]]

        def copy(t, k, block, to, src=None):
            return pltpu.make_async_remote_copy(
                src_ref=slot(t, block) if src is None else src, dst_ref=slot(t, block),
                send_sem=send_sems.at[t, k], recv_sem=recv_sems.at[t, k], device_id=to, device_id_type=pl.DeviceIdType.MESH)

        mine = [pltpu.make_async_copy(src_refs[t], slot(t, me), local_sems.at[t]) for t in range(nt)]
        first = []
        for t in range(nt):
            first.append(copy(t, 0, me, sibling, src=src_refs[t]))
            first += [copy(t, 1 + j, me, (*chip, c), src=src_refs[t]) for j, chip in enumerate(chips)]
        landed = [copy(t, 1 + j, (*chip, c), me) for j, chip in enumerate(chips) for t in range(nt)]
        passed = [copy(t, 4 + j, (*chip, c), sibling) for j, chip in enumerate(chips) for t in range(nt)]
        from_sibling = [copy(t, 0, sibling, me) for t in range(nt)]
        from_sibling += [copy(t, 4 + j, (*chip, 1 - c), me) for t in range(nt) for j, chip in enumerate(chips)]
        return mine, first, landed, passed, from_sibling

    def start(*refs):
        mine, first, _, _, _ = plan(*refs)
        for cp in mine + first:
            cp.start()

    def forward(*refs):
        _, _, landed, passed, _ = plan(*refs)
        for got, fwd in zip(landed, passed):
            got.wait_recv()
            fwd.start()

    def finish(*refs):
        mine, first, _, passed, from_sibling = plan(*refs)
        for cp in from_sibling:
            cp.wait_recv()
        for cp in first + passed:
            cp.wait_send()
        for cp in mine:
            cp.wait()

    out_shapes = [jax.ShapeDtypeStruct((N_DEV,) + s.shape, s.dtype) for s in srcs]
    return Side(list(srcs), out_shapes, _comm_sems(nt), [(0.0, start), (0.5, forward), (1.0, finish)])


def run_side(name, side):
    ns = len(side.arrs)

    def body(*refs):
        for _, phase in side.phases:
            phase(refs[:ns], refs[ns:ns + len(side.out_shapes)], *refs[ns + len(side.out_shapes):])

    return list(pl.pallas_call(
        body, name=name, in_specs=[_ANY] * ns, out_specs=[_ANY] * len(side.out_shapes),
        out_shape=list(side.out_shapes), scratch_shapes=list(side.sem_shapes))(*side.arrs))


def _relayout_w_in(w):
    offs = [0]
    for s in IN_SPLITS:
        offs.append(offs[-1] + s)
    p = [w[:, offs[i]:offs[i + 1]] for i in range(len(IN_SPLITS))]
    ua, za, q, k, v, zb, xbc, dt, zc, gates = p
    dtp = jnp.pad(dt, ((0, 0), (0, HPAD - dt.shape[1])))
    return (jnp.concatenate([ua, za, dtp], 1), w[:, offs[2]:offs[5]], jnp.concatenate([xbc, zb, zc], 1), gates)


def _pad_lanes(v, n=HPAD):
    return jnp.pad(v.reshape(1, -1), ((0, 0), (0, n - v.shape[-1])))


def _s5_prep_args(W):
    g2 = S5_GROUPS * S5_STATE
    w = g2 // S5_CHUNKS
    a_re, a_im = W['s5_a_re'].reshape(1, g2), W['s5_a_im'].reshape(1, g2)
    ls = W['s5_log_step'].reshape(1, S5_GROUPS)
    btr, bti = W['s5_b_re'].reshape(g2, S5_GROUP).T, W['s5_b_im'].reshape(g2, S5_GROUP).T
    ctr = W['s5_c_re'].transpose(1, 0, 2).reshape(S5_GROUP, g2)
    cti = W['s5_c_im'].transpose(1, 0, 2).reshape(S5_GROUP, g2)
    col = lambda a, rows: Arg(a, (rows, w), lambda o: (0, o), 'tile')
    return [col(a_re, 1), col(a_im, 1), _whole(ls, 'acc'), col(btr, S5_GROUP), col(bti, S5_GROUP), col(ctr, S5_GROUP), col(cti, S5_GROUP)]


def _s5_prep_outs():
    g2 = S5_GROUPS * S5_STATE
    w = g2 // S5_CHUNKS
    rows = (S5_ND, S5_ND, S5_SUB, S5_SUB, 128, 128, 128, 128)
    return [Out((r, g2), f32, (r, w), lambda o: (0, o)) for r in rows]


def _s5_args(A, prep, dvec, S):
    w = S5_GROUPS * S5_STATE // S5_CHUNKS
    args = [Arg(A, (S5_TILE, 128), lambda o, t: (t, o), 'tile', (S, S5_WIDTH), None, bf16)]
    for p in prep:
        args.append(Arg(p, (p.shape[0], w), lambda o, t: (0, o), 'acc0'))
    args.append(Arg(dvec, (1, 128), lambda o, t: (0, o), 'acc0'))
    return args


def _attn_args(QKV, g, r, qw, kw, S):
    L = S // r
    nq, rb = _attn_plan(r)
    block = (nq * ATT_BLOCK, rb * ATT_GW)
    gshape = (L, r * ATT_GW)
    gimap = lambda rho, n: (n, rho)
    if r == 1:
        mk = lambda j: Arg(QKV, block, lambda rho, n, j=j: (n, j), 'tile', gshape, gimap, bf16)
    else:
        def mk(j):
            view = QKV[:, j * ATT_GW:(j + 1) * ATT_GW].reshape(L, r * ATT_GW)
            return Arg(view, block, gimap, 'tile', None, None, bf16)
    return [mk(g), mk(3 + g), mk(6 + g), _whole(qw, 'acc'), _whole(kw, 'acc')]


def _attn_plan(r):
    return (4, 1) if r == 1 else (1, min(r, 4))


def _attn_grid(r, S):
    nq, rb = _attn_plan(r)
    return (r // rb, S // r // ATT_BLOCK // nq)


def _attn_carry(r):
    return ((ATT_BLOCK, _attn_plan(r)[1] * ATT_GW),) * 2


def _ssd_args(C, A, W, S):
    T = SSD_CHUNK
    return [Arg(C, (T, SSD_CONV_DIM), lambda o, t: (t, 0), 'tile', (S, SSD_CONV_DIM), None, bf16),
            Arg(A, (T, HPAD), lambda o, t: (t, 2 * S5_WIDTH // HPAD), 'tile', (S, HPAD), lambda o, t: (t, 0), bf16),
            Arg(C, (T, SSD_WIDTH), lambda o, t: (t, 2), 'tile', (S, SSD_WIDTH), lambda o, t: (t, 0), bf16),
            _whole(W['conv_w'], 'acc'), _whole(W['conv_b'].reshape(1, -1), 'acc'),
            _whole(_pad_lanes(W['dt_bias']), 'acc'), _whole(_pad_lanes(W['ssd_a_log']), 'acc'),
            _whole(_pad_lanes(W['ssd_d']), 'acc'), _whole(W['ssd_norm_w'].reshape(1, -1), 'acc')]


_SSD_CARRY = ((8, SSD_CONV_DIM), (SSD_WIDTH, SSD_STATE))
_S5_CARRY = ((1, 512), (1, 512))


def layer_fwd(li, x, W, side=None):
    S = x.shape[0]
    n = lambda s: f"l{li}_{s}"
    sv = {'x': x}
    (h,) = map_fwd(n("norm"), _rmsnorm_tile, (S // 512,), [_rows(x, 512), _whole(W['norm_w'].reshape(1, -1))],
                   [Out((S, D_MODEL), bf16, (512, D_MODEL), lambda i: (i, 0))])
    wA, wQ, wC, wG = W['w_in_pieces']
    A = matmul(n("in_a"), h, wA)
    QKV = matmul(n("in_qkv"), h, wQ)
    C = matmul(n("in_c"), h, wC)
    G = matmul(n("in_g"), h, wG)
    sv.update(h=h, A=A, QKV=QKV, C=C, G=G)

    prep = map_fwd(n("s5_prep"), _s5_prep_tile, (S5_CHUNKS,), _s5_prep_args(W), _s5_prep_outs())
    dvec = W['s5_d'].reshape(1, -1)
    (g,), s5_ck, *got = scan_fwd(n("s5_scan"), _s5_tile, (S5_CHUNKS, S // S5_TILE), _S5_CARRY, _s5_args(A, prep, dvec, S),
                                 [Out((S, S5_WIDTH), f32, (S5_TILE, 128), lambda o, t: (t, o))], side=side)
    glu = matmul(n("glu"), g, W['s5_glu_w'])
    glu_b = W['s5_glu_b'].reshape(1, -1)
    (ya,) = map_fwd(n("glu_gate"), _glu_tile, (S // 512,),
                    [_rows(g, 512), _rows(glu, 512), _rows(A, 512, col=1, width=S5_WIDTH), _whole(glu_b)],
                    [Out((S, S5_WIDTH), bf16, (512, S5_WIDTH), lambda i: (i, 0))])
    sv.update(prep=prep, g=g, glu=glu, ya=ya, s5_ck=s5_ck)

    qw, kw = W['q_norm_w'].reshape(1, -1), W['k_norm_w'].reshape(1, -1)
    att, att_ck = [], []
    for gi, (window, r) in enumerate(ATT_PAIRS):
        assert window // r == ATT_BLOCK and S % (r * ATT_BLOCK) == 0
        L = S // r
        nq, rb = _attn_plan(r)
        assert S // r // ATT_BLOCK % nq == 0
        spec = Out((L, r * ATT_GW), f32, (nq * ATT_BLOCK, rb * ATT_GW), lambda rho, nb: (nb, rho))
        (o, lse), ck = scan_fwd(n(f"attn{gi}"), _attn_tile, _attn_grid(r, S), _attn_carry(r), _attn_args(QKV, gi, r, qw, kw, S), [spec, spec])
        att += [o.reshape(S, ATT_GW), lse.reshape(S, ATT_GW)]
        att_ck.append(ck)
    (yb,) = map_fwd(n("combine"), _combine_tile, (S // 512,),
                    [_rows(t, 512) for t in att] + [_rows(C, 512, col=SSD_CONV_DIM // ATT_GW, width=ATT_GW)],
                    [Out((S, ATT_GW), bf16, (512, ATT_GW), lambda i: (i, 0))])
    sv.update(att=att, att_ck=att_ck, yb=yb)

    (yc,), ssd_ck = scan_fwd(n("ssd"), _ssd_tile, (1, S // SSD_CHUNK), _SSD_CARRY, _ssd_args(C, A, W, S),
                             [Out((S, SSD_WIDTH), bf16, (SSD_CHUNK, SSD_WIDTH), lambda o, t: (t, 0))])
    sv.update(yc=yc, ssd_ck=ssd_ck)

    pa = matmul(n("proj_a"), ya, W['proj_a'])
    pb = matmul(n("proj_b"), yb, W['proj_b'])
    pc = matmul(n("proj_c"), yc, W['proj_c'])
    (merged,) = map_fwd(n("merge"), _merge_tile, (S // 256,),
                        [_rows(pa, 256), _rows(pb, 256), _rows(pc, 256)] + [_rows(G, 256, col=j, width=D_MODEL) for j in range(3)],
                        [Out((S, D_MODEL), bf16, (256, D_MODEL), lambda i: (i, 0))])
    out = matmul(n("w_out"), merged, W['w_out'], add=x)
    sv.update(pa=pa, pb=pb, pc=pc, merged=merged)
    return out, sv, (got[0] if got else None)


def layer_bwd(li, dout, sv, W, side=None):
    S = dout.shape[0]
    n = lambda s: f"l{li}_{s}"
    gr = {}
    x, A, QKV, C, G = sv['x'], sv['A'], sv['QKV'], sv['C'], sv['G']

    dmerged = matmul(n("d_merged"), dout, W['w_out'], 'nt')
    gr['w_out'] = wgrad(n("g_w_out"), sv['merged'], dout)
    margs = [_rows(sv['pa'], 256, gdtype=bf16), _rows(sv['pb'], 256, gdtype=bf16), _rows(sv['pc'], 256, gdtype=bf16)] + \
            [_rows(G, 256, col=j, width=D_MODEL, gshape=(S, D_MODEL), gdtype=bf16) for j in range(3)]
    dpa, dpb, dpc, dg0, dg1, dg2 = map_bwd(n("merge_bwd"), _merge_tile, (S // 256,), margs, [_rows(dmerged, 256)], list(range(6)))
    dya = matmul(n("d_ya"), dpa, W['proj_a'], 'nt')
    dyb = matmul(n("d_yb"), dpb, W['proj_b'], 'nt')
    dyc = matmul(n("d_yc"), dpc, W['proj_c'], 'nt')
    gr['proj_a'] = wgrad(n("g_proj_a"), sv['ya'], dpa)
    gr['proj_b'] = wgrad(n("g_proj_b"), sv['yb'], dpb)
    gr['proj_c'] = wgrad(n("g_proj_c"), sv['yc'], dpc)

    glu_b = W['s5_glu_b'].reshape(1, -1)
    gargs = [_rows(sv['g'], 512), _rows(sv['glu'], 512, gdtype=bf16),
             _rows(A, 512, col=1, width=S5_WIDTH, gshape=(S, S5_WIDTH), gdtype=bf16), _whole(glu_b, 'acc')]
    dg_a, dglu, dza, dglu_b = map_bwd(n("glu_gate_bwd"), _glu_tile, (S // 512,), gargs, [_rows(dya, 512)], [0, 1, 2, 3])
    gr['s5_glu_b'] = dglu_b.reshape(-1)
    dg = matmul(n("d_g"), dglu, W['s5_glu_w'], 'nt', add=dg_a)
    gr['s5_glu_w'] = wgrad(n("g_glu_w"), sv['g'], dglu)
    dvec = W['s5_d'].reshape(1, -1)
    sargs = _s5_args(A, sv['prep'], dvec, S)
    res = scan_bwd(n("s5_scan_bwd"), _s5_tile, (S5_CHUNKS, S // S5_TILE), _S5_CARRY, sargs, sv['s5_ck'],
                   [Arg(dg, (S5_TILE, 128), lambda o, t: (t, o))], list(range(len(sargs))), bwd_fn=_s5_tile_bwd)
    dua, dprep, dd = res[0], res[1:9], res[9]
    gr['s5_d'] = dd.reshape(-1)
    pargs = _s5_prep_args(W)
    pouts = _s5_prep_outs()
    da_re, da_im, dls, dbtr, dbti, dctr, dcti = map_bwd(
        n("s5_prep_bwd"), _s5_prep_tile, (S5_CHUNKS,), pargs,
        [Arg(d, o.block, o.imap) for d, o in zip(dprep, pouts)], list(range(7)))
    gshape = (S5_GROUPS, S5_STATE)
    gr['s5_a_re'], gr['s5_a_im'] = da_re.reshape(gshape), da_im.reshape(gshape)
    gr['s5_log_step'] = dls.reshape(-1)
    gr['s5_b_re'] = dbtr.T.reshape(S5_GROUPS, S5_STATE, S5_GROUP)
    gr['s5_b_im'] = dbti.T.reshape(S5_GROUPS, S5_STATE, S5_GROUP)
    gr['s5_c_re'] = dctr.reshape(S5_GROUP, S5_GROUPS, S5_STATE).transpose(1, 0, 2)
    gr['s5_c_im'] = dcti.reshape(S5_GROUP, S5_GROUPS, S5_STATE).transpose(1, 0, 2)

    cargs = [_rows(t, 512) for t in sv['att']] + \
            [_rows(C, 512, col=SSD_CONV_DIM // ATT_GW, width=ATT_GW, gshape=(S, ATT_GW), gdtype=bf16)]
    cres = map_bwd(n("combine_bwd"), _combine_tile, (S // 512,), cargs, [_rows(dyb, 512)], list(range(7)))
    dzb = cres[6]
    qw, kw = W['q_norm_w'].reshape(1, -1), W['k_norm_w'].reshape(1, -1)
    dqs, dks, dvs = [], [], []
    dqw = dkw = None
    for gi, (window, r) in enumerate(ATT_PAIRS):
        L = S // r
        nq, rb = _attn_plan(r)
        dspec = lambda t: Arg(t.reshape(L, r * ATT_GW), (nq * ATT_BLOCK, rb * ATT_GW), lambda rho, nb: (nb, rho))
        dq, dk, dv, dqw_g, dkw_g = scan_bwd(n(f"attn{gi}_bwd"), _attn_tile, _attn_grid(r, S), _attn_carry(r),
                                            _attn_args(QKV, gi, r, qw, kw, S), sv['att_ck'][gi],
                                            [dspec(cres[2 * gi]), dspec(cres[2 * gi + 1])], [0, 1, 2, 3, 4])
        dqs.append(dq.reshape(S, ATT_GW))
        dks.append(dk.reshape(S, ATT_GW))
        dvs.append(dv.reshape(S, ATT_GW))
        dqw = dqw_g if dqw is None else dqw + dqw_g
        dkw = dkw_g if dkw is None else dkw + dkw_g
    gr['q_norm_w'], gr['k_norm_w'] = dqw.reshape(-1), dkw.reshape(-1)

    ssd_args = _ssd_args(C, A, W, S)
    sres = scan_bwd(n("ssd_bwd"), _ssd_tile, (1, S // SSD_CHUNK), _SSD_CARRY, ssd_args, sv['ssd_ck'],
                    [Arg(dyc, (SSD_CHUNK, SSD_WIDTH), lambda o, t: (t, 0))], list(range(9)), side=side)
    sres, got = sres if side else (sres, None)
    dxbc, ddt, dzc = sres[0], sres[1], sres[2]
    gr['conv_w'] = sres[3]
    gr['conv_b'] = sres[4].reshape(-1)
    gr['dt_bias'] = sres[5].reshape(-1)[:SSD_HEADS]
    gr['ssd_a_log'] = sres[6].reshape(-1)[:SSD_HEADS]
    gr['ssd_d'] = sres[7].reshape(-1)[:SSD_HEADS]
    gr['ssd_norm_w'] = sres[8].reshape(-1)

    dpieces = [jnp.concatenate([dua, dza, ddt], axis=1), jnp.concatenate(dqs + dks + dvs, axis=1),
               jnp.concatenate([dxbc, dzb, dzc], axis=1), jnp.concatenate([dg0, dg1, dg2], axis=1)]
    dh = matmul_nt_sum(n("d_h"), dpieces, list(W['w_in_pieces']))
    gr['w_in'] = _unrelayout_w_in_grad([wgrad(n(f"g_w_in{j}"), sv['h'], dp) for j, dp in enumerate(dpieces)])
    nargs = [_rows(x, 512), _whole(W['norm_w'].reshape(1, -1), 'acc')]
    dx, dnw = map_bwd(n("norm_bwd"), _rmsnorm_tile, (S // 512,), nargs, [_rows(dh, 512)], [0, 1], add={0: _rows(dout, 512)})
    gr['norm_w'] = dnw.reshape(-1)
    return dx, gr, got


def _unrelayout_w_in_grad(pieces):
    gA, gQ, gC, gG = pieces
    uaza, dt = gA[:, :2 * S5_WIDTH], gA[:, 2 * S5_WIDTH:2 * S5_WIDTH + SSD_HEADS]
    xbc, zb, zc = gC[:, :SSD_CONV_DIM], gC[:, SSD_CONV_DIM:SSD_CONV_DIM + ATT_GW], gC[:, SSD_CONV_DIM + ATT_GW:]
    return jnp.concatenate([uaza, gQ, zb, xbc, dt, zc, gG], axis=1)


def kernel(x, norm_w, w_in, s5_a_re, s5_a_im, s5_log_step, s5_b_re, s5_b_im, s5_c_re, s5_c_im, s5_d, s5_glu_w, s5_glu_b, q_norm_w, k_norm_w, conv_w, conv_b, dt_bias, ssd_a_log, ssd_d, ssd_norm_w, proj_a, proj_b, proj_c, w_out, loss_target, m_norm_w, m_w_in, m_s5_a_re, m_s5_a_im, m_s5_log_step, m_s5_b_re, m_s5_b_im, m_s5_c_re, m_s5_c_im, m_s5_d, m_s5_glu_w, m_s5_glu_b, m_q_norm_w, m_k_norm_w, m_conv_w, m_conv_b, m_dt_bias, m_ssd_a_log, m_ssd_d, m_ssd_norm_w, m_proj_a, m_proj_b, m_proj_c, m_w_out, v_norm_w, v_w_in, v_s5_a_re, v_s5_a_im, v_s5_log_step, v_s5_b_re, v_s5_b_im, v_s5_c_re, v_s5_c_im, v_s5_d, v_s5_glu_w, v_s5_glu_b, v_q_norm_w, v_k_norm_w, v_conv_w, v_conv_b, v_dt_bias, v_ssd_a_log, v_ssd_d, v_ssd_norm_w, v_proj_a, v_proj_b, v_proj_c, v_w_out):
    args = dict(locals())
    w = {k: args[k] for k in WEIGHTS}
    m = {k: args['m_' + k] for k in WEIGHTS}
    v = {k: args['v_' + k] for k in WEIGHTS}
    depth = norm_w.shape[0]
    S = x.shape[1]
    xs = x.reshape(S, D_MODEL)
    tgt = loss_target.reshape(S, D_MODEL)

    def weight_gather(li):
        return gather_side([w[k][li].astype(bf16) for k in SHARDED])

    def assemble(li, gathered):
        W = {k: w[k][li] for k in WEIGHTS if k not in SHARDED}
        for k, t in zip(SHARDED, gathered):
            n_dev, R, C = t.shape
            W[k] = t.reshape(n_dev * R, C) if k in ROW_SHARDED else t.transpose(1, 0, 2).reshape(R, n_dev * C)
        W['w_in_pieces'] = _relayout_w_in(W['w_in'])
        return W

    layers = [assemble(0, run_side("gather_weights0", weight_gather(0)))]
    act, saved = xs, []
    for li in range(depth):
        act, sv, got = layer_fwd(li, act, layers[li], weight_gather(li + 1) if li + 1 < depth else None)
        saved.append(sv)
        if got is not None:
            layers.append(assemble(li + 1, got))
    dy, loss_local = loss_and_grad(act, tgt)
    loss = lax.psum(loss_local, ("x", "y", "c"))

    big = [k for k in SHARDED if k != 'conv_w']

    def grad_scatter(gr):
        return exchange_side([gr[k] for k in big], ['rows' if k in ROW_SHARDED else 'cols' for k in big])

    grads, slots = [None] * depth, [None] * depth
    for li in reversed(range(depth)):
        dy, grads[li], got = layer_bwd(li, dy, saved[li], layers[li], grad_scatter(grads[li + 1]) if li + 1 < depth else None)
        if got is not None:
            slots[li + 1] = got
    slots[0] = run_side("scatter_grads0", grad_scatter(grads[0]))
    grad_x = dy.reshape(x.shape)
    per_layer = {}
    for li in range(depth):
        for k, s in zip(big, slots[li]):
            per_layer[li, k] = adamw(f"adamw_{k}{li}", w[k][li], s, m[k][li], v[k][li])
    result = {k: tuple(jnp.stack([per_layer[li, k][j] for li in range(depth)], axis=0) for j in range(4)) for k in big}

    small_keys = [k for k in WEIGHTS if k not in SHARDED] + ['conv_w']
    stacked = [jnp.stack([grads[li][k] for li in range(depth)], axis=0) for k in small_keys]
    (small_slots,) = run_side("gather_small_grads", exchange_side([_pack(stacked)], ['all']))
    totals = _unpack(sum_slots("sum_small_grads", small_slots), [t.shape for t in stacked])
    for k, g in zip(small_keys, totals):
        if k == 'conv_w':
            width = w[k].shape[-1]
            me = 4 * lax.axis_index("x") + 2 * lax.axis_index("y") + lax.axis_index("c")
            g = lax.dynamic_slice_in_dim(g, me * width, width, axis=2)
        result[k] = adamw("adamw_" + k, w[k], g[None], m[k], v[k])

    return (loss, grad_x, *[result[k][0] for k in WEIGHTS], *[result[k][1] for k in WEIGHTS],
            *[result[k][2] for k in WEIGHTS], *[result[k][3] for k in WEIGHTS])
```

```python
import functools
import math
from typing import Any, NamedTuple

import jax
import jax.numpy as jnp
from jax import lax
from jax.experimental import pallas as pl
from jax.experimental.pallas import tpu as pltpu

f32 = jnp.float32
bf16 = jnp.bfloat16

N_DEV = 8
D_MODEL = 1024
RMS_EPS = 1e-6
S5_WIDTH = 512
S5_GROUPS = 32
S5_GROUP = 16
S5_STATE = 64
S5_TILE = 512
S5_SUB = 8
S5_ND = 3
S5_CHUNKS = 4
ATT_HEAD_DIM = 64
ATT_PAIRS = ((128, 1), (512, 4), (2048, 16))
ATT_HPG = 4
ATT_BLOCK = 128
ATT_GW = ATT_HPG * ATT_HEAD_DIM
ATT_WIDTH = 768
SSD_HEADS = 12
SSD_HEAD_DIM = 64
SSD_WIDTH = 768
SSD_STATE = 128
SSD_GROUPS = 2
SSD_CHUNK = 128
SSD_CONV = 4
SSD_CONV_DIM = 1280
HPAD = 128
IN_SPLITS = (512, 512, 768, 768, 768, 256, 1280, 12, 768, 3072)
ADAM_LR, ADAM_B1, ADAM_B2, ADAM_EPS, ADAM_WD, ADAM_STEP = 0.001, 0.9, 0.999, 1e-08, 0.01, 10
VMEM_LIMIT = 56 * 1024 * 1024

WEIGHTS = ['norm_w', 'w_in', 's5_a_re', 's5_a_im', 's5_log_step', 's5_b_re', 's5_b_im', 's5_c_re',
           's5_c_im', 's5_d', 's5_glu_w', 's5_glu_b', 'q_norm_w', 'k_norm_w', 'conv_w', 'conv_b',
           'dt_bias', 'ssd_a_log', 'ssd_d', 'ssd_norm_w', 'proj_a', 'proj_b', 'proj_c', 'w_out']
ROW_SHARDED = ('w_in', 's5_glu_w', 'w_out')
SHARDED = ROW_SHARDED + ('conv_w', 'proj_a', 'proj_b', 'proj_c')


class Arg(NamedTuple):
    arr: Any
    block: tuple
    imap: Any
    kind: str = 'const'
    gshape: Any = None
    gimap: Any = None
    gdtype: Any = None


class Out(NamedTuple):
    shape: tuple
    dtype: Any
    block: tuple
    imap: Any


def _cparams(n):
    return pltpu.CompilerParams(dimension_semantics=("arbitrary",) * n, vmem_limit_bytes=VMEM_LIMIT)


def _rows(a, tm, kind='tile', col=0, width=None, gshape=None, gcol=None, gdtype=None):
    width = a.shape[1] if width is None else width
    g = None if gshape is None else (lambda i, gc=(0 if gcol is None else gcol): (i, gc))
    return Arg(a, (tm, width), lambda i, c=col: (i, c), kind, gshape, g, gdtype)


def _whole(a, kind='const'):
    nd = a.ndim
    return Arg(a, a.shape, lambda *i, nd=nd: (0,) * nd, kind)


def map_fwd(name, fn, grid, args, outs):
    n_in = len(args)

    def body(*refs):
        pid = tuple(pl.program_id(a) for a in range(len(grid)))
        res = fn(pid, *[r[...] for r in refs[:n_in]])
        for o, r in zip(refs[n_in:], res):
            o[...] = r.astype(o.dtype)

    res = pl.pallas_call(
        body, name=name, grid=grid,
        in_specs=[pl.BlockSpec(a.block, a.imap) for a in args],
        out_specs=[pl.BlockSpec(o.block, o.imap) for o in outs],
        out_shape=[jax.ShapeDtypeStruct(o.shape, o.dtype) for o in outs],
        compiler_params=_cparams(len(grid)))(*[a.arr for a in args])
    return tuple(res)


def _grad_outs(args, wrt):
    outs = []
    for i in wrt:
        a = args[i]
        shape = a.arr.shape if a.gshape is None else a.gshape
        imap = a.imap if a.gimap is None else a.gimap
        outs.append(Out(shape, f32 if a.gdtype is None else a.gdtype, a.block, imap))
    return outs


def _store_grads(pid, args, wrt, grads, grefs, adds):
    first_all = functools.reduce(jnp.logical_and, [p == 0 for p in pid])
    first_in = functools.reduce(jnp.logical_and, [p == 0 for p in pid[1:]]) if len(pid) > 1 else first_all
    for j, i in enumerate(wrt):
        g = grads[j].astype(f32)
        ref = grefs[j]
        kind = args[i].kind
        if kind == 'tile':
            if j in adds:
                g = g + adds[j]
            ref[...] = g.astype(ref.dtype)
        else:
            first = first_all if kind == 'acc' else first_in

            @pl.when(first)
            def _(ref=ref):
                ref[...] = jnp.zeros_like(ref)

            ref[...] += g


def map_bwd(name, fn, grid, args, douts, wrt, add=None):
    add = add or {}
    n_in, n_d, n_add = len(args), len(douts), len(add)
    add_keys = sorted(add)
    gouts = _grad_outs(args, wrt)

    def body(*refs):
        pid = tuple(pl.program_id(a) for a in range(len(grid)))
        vals = [r[...] for r in refs[:n_in]]
        dvals = [r[...].astype(f32) for r in refs[n_in:n_in + n_d]]
        avals = {k: refs[n_in + n_d + j][...].astype(f32) for j, k in enumerate(add_keys)}
        grefs = refs[n_in + n_d + n_add:]

        def f(*w):
            full = list(vals)
            for i, x in zip(wrt, w):
                full[i] = x
            return tuple(fn(pid, *full))

        _, vjp = jax.vjp(f, *[vals[i] for i in wrt])
        grads = vjp(tuple(dvals))
        _store_grads(pid, args, wrt, grads, grefs, avals)

    ins = list(args) + list(douts) + [add[k] for k in add_keys]
    res = pl.pallas_call(
        body, name=name, grid=grid,
        in_specs=[pl.BlockSpec(a.block, a.imap) for a in ins],
        out_specs=[pl.BlockSpec(o.block, o.imap) for o in gouts],
        out_shape=[jax.ShapeDtypeStruct(o.shape, o.dtype) for o in gouts],
        compiler_params=_cparams(len(grid)))(*[a.arr for a in ins])
    return tuple(res)


class Side(NamedTuple):
    arrs: list
    out_shapes: list
    sem_shapes: list
    phases: list


def _run_side(side, step, total, src_refs, out_refs, sem_refs):
    for frac, phase in side.phases:
        @pl.when(step == int(round(frac * (total - 1))))
        def _(phase=phase):
            phase(src_refs, out_refs, *sem_refs)


_ANY = pl.BlockSpec(memory_space=pl.ANY)


def scan_fwd(name, fn, grid, carry_shapes, args, outs, side=None):
    no, nt = grid
    n_in, n_out, n_c = len(args), len(outs), len(carry_shapes)
    ns_in, ns_out = (len(side.arrs), len(side.out_shapes)) if side else (0, 0)
    cks = [Out((no, nt) + cs, f32, (None, None) + cs, lambda o, t, n=len(cs): (o, t) + (0,) * n) for cs in carry_shapes]

    def body(*refs):
        pid = (pl.program_id(0), pl.program_id(1))
        ins = refs[:n_in]
        sins = refs[n_in:n_in + ns_in]
        refs = refs[n_in + ns_in:]
        orefs = refs[:n_out]
        ckrefs = refs[n_out:n_out + n_c]
        souts = refs[n_out + n_c:n_out + n_c + ns_out]
        crefs = refs[n_out + n_c + ns_out:n_out + n_c + ns_out + n_c]
        if side:
            _run_side(side, pid[0] * nt + pid[1], no * nt, sins, souts, refs[n_out + n_c + ns_out + n_c:])

        @pl.when(pid[1] == 0)
        def _():
            for c in crefs:
                c[...] = jnp.zeros_like(c)

        carry = tuple(c[...] for c in crefs)
        for ck, c in zip(ckrefs, carry):
            ck[...] = c
        res, newc = fn(pid, carry, *[r[...] for r in ins])
        for o, r in zip(orefs, res):
            o[...] = r.astype(o.dtype)
        for c, v in zip(crefs, newc):
            c[...] = v

    allouts = list(outs) + cks
    res = pl.pallas_call(
        body, name=name, grid=grid,
        in_specs=[pl.BlockSpec(a.block, a.imap) for a in args] + [_ANY] * ns_in,
        out_specs=[pl.BlockSpec(o.block, o.imap) for o in allouts] + [_ANY] * ns_out,
        out_shape=[jax.ShapeDtypeStruct(o.shape, o.dtype) for o in allouts] + (list(side.out_shapes) if side else []),
        scratch_shapes=[pltpu.VMEM(cs, f32) for cs in carry_shapes] + (list(side.sem_shapes) if side else []),
        compiler_params=_cparams(2))(*[a.arr for a in args], *(side.arrs if side else []))
    if side:
        return tuple(res[:n_out]), tuple(res[n_out:n_out + n_c]), list(res[n_out + n_c:])
    return tuple(res[:n_out]), tuple(res[n_out:])


def scan_bwd(name, fn, grid, carry_shapes, args, ckpts, douts, wrt, bwd_fn=None, side=None):
    no, nt = grid
    n_in, n_d, n_c = len(args), len(douts), len(carry_shapes)
    ns_in, ns_out = (len(side.arrs), len(side.out_shapes)) if side else (0, 0)

    def rev(imap):
        return lambda o, t: imap(o, nt - 1 - t)

    rargs = [a._replace(imap=rev(a.imap), gimap=None if a.gimap is None else rev(a.gimap)) for a in args]
    rdouts = [a._replace(imap=rev(a.imap)) for a in douts]
    ckargs = [Arg(ck, (None, None) + cs, rev(lambda o, t, n=len(cs): (o, t) + (0,) * n)) for ck, cs in zip(ckpts, carry_shapes)]
    gouts = _grad_outs(rargs, wrt)

    def body(*refs):
        o, t = pl.program_id(0), pl.program_id(1)
        tt = nt - 1 - t
        vals = [r[...] for r in refs[:n_in]]
        dvals = [r[...].astype(f32) for r in refs[n_in:n_in + n_d]]
        carry = tuple(r[...] for r in refs[n_in + n_d:n_in + n_d + n_c])
        sins = refs[n_in + n_d + n_c:n_in + n_d + n_c + ns_in]
        refs = refs[n_in + n_d + n_c + ns_in:]
        grefs = refs[:len(wrt)]
        souts = refs[len(wrt):len(wrt) + ns_out]
        dcrefs = refs[len(wrt) + ns_out:len(wrt) + ns_out + n_c]
        if side:
            _run_side(side, o * nt + t, no * nt, sins, souts, refs[len(wrt) + ns_out + n_c:])

        @pl.when(t == 0)
        def _():
            for c in dcrefs:
                c[...] = jnp.zeros_like(c)

        def f(carry, *w):
            full = list(vals)
            for i, x in zip(wrt, w):
                full[i] = x
            res, newc = fn((o, tt), carry, *full)
            return tuple(res), tuple(newc)

        dcarry = tuple(c[...] for c in dcrefs)
        if bwd_fn is None:
            _, vjp = jax.vjp(f, carry, *[vals[i] for i in wrt])
            grads = vjp((tuple(dvals), dcarry))
            dcarry_in, grads = grads[0], grads[1:]
        else:
            dcarry_in, grads = bwd_fn((o, tt), carry, vals, dvals, dcarry)
        for c, g in zip(dcrefs, dcarry_in):
            c[...] = g
        _store_grads((o, t), rargs, wrt, grads, grefs, {})

    ins = rargs + rdouts + ckargs
    res = pl.pallas_call(
        body, name=name, grid=grid,
        in_specs=[pl.BlockSpec(a.block, a.imap) for a in ins] + [_ANY] * ns_in,
        out_specs=[pl.BlockSpec(g.block, g.imap) for g in gouts] + [_ANY] * ns_out,
        out_shape=[jax.ShapeDtypeStruct(g.shape, g.dtype) for g in gouts] + (list(side.out_shapes) if side else []),
        scratch_shapes=[pltpu.VMEM(cs, f32) for cs in carry_shapes] + (list(side.sem_shapes) if side else []),
        compiler_params=_cparams(2))(*[a.arr for a in ins], *(side.arrs if side else []))
    if side:
        return tuple(res[:len(gouts)]), list(res[len(gouts):])
    return tuple(res)


def _pick(dim, target):
    if dim <= target:
        return dim
    for t in range(target // 128 * 128, 127, -128):
        if dim % t == 0:
            return t
    return dim


def matmul(name, a, b, mode='nn', add=None, out_dtype=f32, tm=None, tn=1152, tk=None):
    if mode == 'tn':
        K, M = a.shape
    else:
        M, K = a.shape
    N = b.shape[0] if mode == 'nt' else b.shape[1]
    assert (b.shape[1] if mode == 'nt' else b.shape[0]) == K
    tm = (1024 if mode == 'tn' else 512) if tm is None else tm
    tk = (512 if mode == 'tn' else 1152) if tk is None else tk
    tm, tn, tk = _pick(M, tm), _pick(N, tn), _pick(K, tk)
    nk = K // tk
    a_spec = pl.BlockSpec((tk, tm), lambda i, j, k: (k, i)) if mode == 'tn' else pl.BlockSpec((tm, tk), lambda i, j, k: (i, k))
    b_spec = pl.BlockSpec((tn, tk), lambda i, j, k: (j, k)) if mode == 'nt' else pl.BlockSpec((tk, tn), lambda i, j, k: (k, j))
    dims = {'nn': (((1,), (0,)), ((), ())), 'nt': (((1,), (1,)), ((), ())), 'tn': (((0,), (0,)), ((), ()))}[mode]
    has_add = add is not None

    def body(*refs):
        if has_add:
            a_ref, b_ref, add_ref, o_ref, acc = refs
        else:
            a_ref, b_ref, o_ref, acc = refs
        k = pl.program_id(2)

        @pl.when(k == 0)
        def _():
            acc[...] = add_ref[...].astype(f32) if has_add else jnp.zeros_like(acc)

        acc[...] += lax.dot_general(a_ref[...].astype(bf16), b_ref[...].astype(bf16), dims, preferred_element_type=f32)

        @pl.when(k == nk - 1)
        def _():
            o_ref[...] = acc[...].astype(o_ref.dtype)

    in_specs = [a_spec, b_spec] + ([pl.BlockSpec((tm, tn), lambda i, j, k: (i, j))] if has_add else [])
    ops = [a, b] + ([add] if has_add else [])
    return pl.pallas_call(
        body, name=name, grid=(M // tm, N // tn, nk), in_specs=in_specs,
        out_specs=pl.BlockSpec((tm, tn), lambda i, j, k: (i, j)),
        out_shape=jax.ShapeDtypeStruct((M, N), out_dtype),
        scratch_shapes=[pltpu.VMEM((tm, tn), f32)],
        compiler_params=pltpu.CompilerParams(dimension_semantics=("parallel", "parallel", "arbitrary"), vmem_limit_bytes=VMEM_LIMIT))(*ops)


def matmul_nt_sum(name, lhs, rhs, tm=1024, tk=768):
    M, N = lhs[0].shape[0], rhs[0].shape[0]
    tm = _pick(M, tm)
    tks = [_pick(a.shape[1], tk) for a in lhs]
    starts, total = [], 0
    for a, t in zip(lhs, tks):
        starts.append(total)
        total += a.shape[1] // t
    npc = len(lhs)

    def body(*refs):
        a_refs, b_refs, o_ref, acc = refs[:npc], refs[npc:2 * npc], refs[2 * npc], refs[2 * npc + 1]
        k = pl.program_id(1)

        @pl.when(k == 0)
        def _():
            acc[...] = jnp.zeros_like(acc)

        for p in range(npc):
            @pl.when((k >= starts[p]) & (k < starts[p] + lhs[p].shape[1] // tks[p]))
            def _(p=p):
                acc[...] += lax.dot_general(a_refs[p][...].astype(bf16), b_refs[p][...].astype(bf16), _NT, preferred_element_type=f32)

        @pl.when(k == total - 1)
        def _():
            o_ref[...] = acc[...]

    def kblock(p):
        return lambda k: jnp.clip(k - starts[p], 0, lhs[p].shape[1] // tks[p] - 1)

    in_specs = [pl.BlockSpec((tm, tks[p]), lambda i, k, kb=kblock(p): (i, kb(k))) for p in range(npc)]
    in_specs += [pl.BlockSpec((N, tks[p]), lambda i, k, kb=kblock(p): (0, kb(k))) for p in range(npc)]
    return pl.pallas_call(
        body, name=name, grid=(M // tm, total), in_specs=in_specs,
        out_specs=pl.BlockSpec((tm, N), lambda i, k: (i, 0)),
        out_shape=jax.ShapeDtypeStruct((M, N), f32),
        scratch_shapes=[pltpu.VMEM((tm, N), f32)],
        compiler_params=pltpu.CompilerParams(dimension_semantics=("parallel", "arbitrary"), vmem_limit_bytes=VMEM_LIMIT))(*lhs, *rhs)


def wgrad(name, act, dout):
    return matmul(name, act, dout, 'tn', out_dtype=bf16)


def _dot(a, b, dims=(((1,), (0,)), ((), ()))):
    return lax.dot_general(a.astype(bf16), b.astype(bf16), dims, preferred_element_type=f32)


_NT = (((1,), (1,)), ((), ()))
_TN = (((0,), (0,)), ((), ()))


def _three_term_dot(v, sel, dims):
    hi = v.astype(bf16)
    rest = v - hi.astype(f32)
    mid = rest.astype(bf16)
    lo = (rest - mid.astype(f32)).astype(bf16)
    dot = lambda t: lax.dot_general(t, sel, dims, preferred_element_type=f32)
    return dot(hi) + dot(mid) + dot(lo)


@jax.custom_vjp
def _dot_exact01(v, sel):
    return _three_term_dot(v, sel, (((1,), (0,)), ((), ())))


def _dot_exact01_fwd(v, sel):
    return _dot_exact01(v, sel), sel


def _dot_exact01_bwd(sel, ct):
    return _three_term_dot(ct, sel, _NT), jnp.zeros_like(sel)


_dot_exact01.defvjp(_dot_exact01_fwd, _dot_exact01_bwd)


def _spread_heads(v, width):
    r = lax.broadcasted_iota(jnp.int32, (HPAD, SSD_HEADS * width), 0)
    c = lax.broadcasted_iota(jnp.int32, (HPAD, SSD_HEADS * width), 1)
    return _dot_exact01(v, (r == c // width).astype(bf16))


def _rmsnorm_tile(pid, x, w):
    return (x * lax.rsqrt(jnp.mean(x * x, axis=-1, keepdims=True) + RMS_EPS) * w,)


def _shift_rows(h, d, fill):
    pad = jnp.full((d, h.shape[1]), fill, f32)
    return jnp.concatenate([pad, h[:-d]], axis=0)


def _s5_prep_tile(pid, a_re, a_im, ls, btr, bti, ctr, cti):
    o = pid[0]
    w = a_re.shape[1]
    r = lax.broadcasted_iota(jnp.int32, (S5_GROUPS, w), 0)
    c = lax.broadcasted_iota(jnp.int32, (S5_GROUPS, w), 1)
    sel = (r == o * (w // S5_STATE) + c // S5_STATE).astype(f32)
    step = jnp.dot(jnp.exp(ls), sel, precision=lax.Precision.HIGHEST, preferred_element_type=f32)
    mag = jnp.exp(a_re * step)
    ang = a_im * step
    lr, li = mag * jnp.cos(ang), mag * jnp.sin(ang)
    nr, ni = lr - 1.0, li
    den = a_re * a_re + a_im * a_im
    fr = (nr * a_re + ni * a_im) / den
    fi = (ni * a_re - nr * a_im) / den
    bbr = fr * btr - fi * bti
    bbi = fr * bti + fi * btr
    reps = w // S5_STATE
    rr = lax.broadcasted_iota(jnp.int32, (reps * S5_GROUP, w), 0)
    cc = lax.broadcasted_iota(jnp.int32, (reps * S5_GROUP, w), 1)
    diag = (rr // S5_GROUP) == (cc // S5_STATE)

    def expand(m):
        return jnp.where(diag, jnp.concatenate([m] * reps, axis=0), 0.0)

    pr, pi = lr, li
    rows_r, rows_i = [pr], [pi]
    for _ in range(S5_ND - 1):
        pr, pi = pr * pr - pi * pi, 2.0 * pr * pi
        rows_r.append(pr)
        rows_i.append(pi)
    lamd_r, lamd_i = jnp.concatenate(rows_r, axis=0), jnp.concatenate(rows_i, axis=0)
    tr = jnp.broadcast_to(lr, (S5_SUB, w))
    ti = jnp.broadcast_to(li, (S5_SUB, w))
    for j in range(S5_ND):
        sr, si = _shift_rows(tr, 1 << j, 1.0), _shift_rows(ti, 1 << j, 0.0)
        tr, ti = tr * sr - ti * si, tr * si + ti * sr
    return lamd_r, lamd_i, tr, ti, expand(bbr), expand(bbi), expand(ctr), expand(cti)


def _s5_tile(pid, carry, u, lamd_r, lamd_i, lam8_r, lam8_i, bbr, bbi, ccr, cci, dvec):
    hr, hi = _s5_scan(_dot(u, bbr), _dot(u, bbi), carry, lamd_r, lamd_i, lam8_r, lam8_i, reverse=False)
    return (_s5_readout(hr, hi, u, ccr, cci, dvec),), (hr[-1:], hi[-1:])


def _s5_readout(hr, hi, u, ccr, cci, dvec):
    return jax.nn.gelu(_dot(hr, ccr, _NT) - _dot(hi, cci, _NT) + dvec * u)


def _s5_scan(xr, xi, carry, lamd_r, lamd_i, lam8_r, lam8_i, reverse):
    cr, ci = carry
    T, G = xr.shape[0], S5_SUB
    sign = -1.0 if reverse else 1.0
    sub = lax.broadcasted_iota(jnp.int32, (T, 1), 0) % G
    for j in range(S5_ND):
        d = 1 << j
        if reverse:
            keep = sub < G - d
            sr = jnp.where(keep, jnp.concatenate([xr[d:], jnp.zeros((d, xr.shape[1]), f32)], axis=0), 0.0)
            si = jnp.where(keep, jnp.concatenate([xi[d:], jnp.zeros((d, xi.shape[1]), f32)], axis=0), 0.0)
        else:
            keep = sub >= d
            sr = jnp.where(keep, _shift_rows(xr, d, 0.0), 0.0)
            si = jnp.where(keep, _shift_rows(xi, d, 0.0), 0.0)
        ar, ai = lamd_r[j:j + 1], sign * lamd_i[j:j + 1]
        xr, xi = xr + ar * sr - ai * si, xi + ar * si + ai * sr
    if reverse:
        pr = jnp.concatenate([lam8_r[G - 1 - s:G - s] for s in range(G)], axis=0)
        pi = -jnp.concatenate([lam8_i[G - 1 - s:G - s] for s in range(G)], axis=0)
    else:
        pr, pi = lam8_r, lam8_i
    n = T // G
    rows_r, rows_i = [None] * n, [None] * n
    for i in (reversed(range(n)) if reverse else range(n)):
        gr_, gi_ = xr[i * G:(i + 1) * G], xi[i * G:(i + 1) * G]
        gr_, gi_ = gr_ + pr * cr - pi * ci, gi_ + pr * ci + pi * cr
        cr, ci = (gr_[:1], gi_[:1]) if reverse else (gr_[G - 1:], gi_[G - 1:])
        rows_r[i], rows_i[i] = gr_, gi_
    return jnp.concatenate(rows_r, axis=0), jnp.concatenate(rows_i, axis=0)


def _s5_tile_bwd(pid, carry, vals, douts, dcarry):
    u, lamd_r, lamd_i, lam8_r, lam8_i, bbr, bbi, ccr, cci, dvec = vals
    (dg,) = douts
    hr, hi = _s5_scan(_dot(u, bbr), _dot(u, bbi), carry, lamd_r, lamd_i, lam8_r, lam8_i, reverse=False)
    _, vjp = jax.vjp(_s5_readout, hr, hi, u, ccr, cci, dvec)
    dhr, dhi, du, dccr, dcci, ddvec = vjp(dg)
    Hr, Hi = _s5_scan(dhr, dhi, dcarry, lamd_r, lamd_i, lam8_r, lam8_i, reverse=True)
    _, vjp_in = jax.vjp(lambda u, bbr, bbi: (_dot(u, bbr), _dot(u, bbi)), u, bbr, bbi)
    du2, dbbr, dbbi = vjp_in((Hr, Hi))
    pr = jnp.concatenate([carry[0], hr[:-1]], axis=0)
    pi = jnp.concatenate([carry[1], hi[:-1]], axis=0)
    dlam_r = jnp.sum(Hr * pr + Hi * pi, axis=0, keepdims=True)
    dlam_i = jnp.sum(Hi * pr - Hr * pi, axis=0, keepdims=True)
    zrow = jnp.zeros((S5_ND - 1, dlam_r.shape[1]), f32)
    dlamd_r, dlamd_i = jnp.concatenate([dlam_r, zrow], axis=0), jnp.concatenate([dlam_i, zrow], axis=0)
    grads = (du + du2, dlamd_r, dlamd_i, jnp.zeros_like(lam8_r), jnp.zeros_like(lam8_i), dbbr, dbbi, dccr, dcci, ddvec)
    return (Hr[:1], Hi[:1]), grads


def _glu_tile(pid, g, glu, za, b):
    return (g * jax.nn.sigmoid(glu + b) * jax.nn.silu(za),)


def _attn_tile(pid, carry, q, k, v, qw, kw):
    n = pid[1]
    kp, vp = carry
    D, B = ATT_HEAD_DIM, ATT_BLOCK
    W = 2 * D
    nq, ncol = q.shape[0] // B, q.shape[1] // W
    r = lax.broadcasted_iota(jnp.int32, (B, 2 * B), 0)
    c = lax.broadcasted_iota(jnp.int32, (B, 2 * B), 1)
    diff = r + B - c
    band = (diff >= 0) & (diff <= B)
    band_first = band & ((c >= B) | (n > 0))
    low = lax.broadcasted_iota(jnp.int32, (1, W), 1) < D
    same_head = (lax.broadcasted_iota(jnp.int32, (W, W), 0) // D == lax.broadcasted_iota(jnp.int32, (W, W), 1) // D).astype(bf16)

    def hnorm(x, w):
        rows = x.shape[0]
        t = jnp.concatenate([x[:, j * W:(j + 1) * W] for j in range(ncol)], axis=0) if ncol > 1 else x
        ms = _dot_exact01(t * t, same_head) * (1.0 / D)
        t = t * lax.rsqrt(ms + RMS_EPS) * jnp.concatenate([w, w], axis=1)
        return [t[j * rows:(j + 1) * rows] for j in range(ncol)]

    qns, kns = hnorm(q, qw), hnorm(k, kw)
    out_cols, lse_cols, kn_cols = [], [], []
    for j in range(ncol):
        sl = slice(j * W, (j + 1) * W)
        qn, kn, vj = qns[j], kns[j], v[:, sl]
        kn_cols.append(kn[(nq - 1) * B:])
        outs, lses = [], []
        for b in range(nq):
            rows = slice(b * B, (b + 1) * B)
            prev = slice((b - 1) * B, b * B)
            kk = jnp.concatenate([kp[:, sl] if b == 0 else kn[prev], kn[rows]], axis=0)
            vv = jnp.concatenate([vp[:, sl] if b == 0 else vj[prev], vj[rows]], axis=0)
            o2, l2 = [], []
            for head_lanes in (low, ~low):
                s = _dot(jnp.where(head_lanes, qn[rows], 0.0), kk, _NT) * (D ** -0.5)
                s = jnp.where(band_first if b == 0 else band, s, -1e30)
                m = jnp.max(s, axis=-1, keepdims=True)
                p = jnp.exp(s - m)
                l = jnp.sum(p, axis=-1, keepdims=True)
                o2.append(_dot(p / l, vv))
                l2.append(m + jnp.log(l))
            outs.append(jnp.where(low, o2[0], o2[1]))
            lses.append(jnp.where(low, l2[0], l2[1]))
        out_cols.append(jnp.concatenate(outs, axis=0) if nq > 1 else outs[0])
        lse_cols.append(jnp.concatenate(lses, axis=0) if nq > 1 else lses[0])
    return ((jnp.concatenate(out_cols, axis=1), jnp.concatenate(lse_cols, axis=1)),
            (jnp.concatenate(kn_cols, axis=1), v[(nq - 1) * B:]))


def _combine_tile(pid, o1, l1, o2, l2, o3, l3, zb):
    m = jnp.maximum(jnp.maximum(l1, l2), l3)
    e1, e2, e3 = jnp.exp(l1 - m), jnp.exp(l2 - m), jnp.exp(l3 - m)
    y = (e1 * o1 + e2 * o2 + e3 * o3) / (e1 + e2 + e3)
    return (y * jax.nn.silu(zb),)


def _softplus(x):
    return jnp.maximum(x, 0.0) + jnp.log(1.0 + jnp.exp(-jnp.abs(x)))


def _ssd_tile(pid, carry, xbc, dt, z, conv_w, conv_b, dt_bias, a_log, dvec, norm_w):
    xprev, state = carry
    T, P, N = SSD_CHUNK, SSD_HEAD_DIM, SSD_STATE
    xx = jnp.concatenate([xprev, xbc], axis=0)
    conv = conv_b
    for k in range(SSD_CONV):
        off = 8 - (SSD_CONV - 1) + k
        conv = conv + conv_w[k:k + 1] * xx[off:off + T]
    xc = jax.nn.silu(conv)
    dtp = _softplus(dt + dt_bias)
    a_dt = dtp * (-jnp.exp(a_log))
    r = lax.broadcasted_iota(jnp.int32, (T, T), 0)
    c = lax.broadcasted_iota(jnp.int32, (T, T), 1)
    tri = r >= c
    trif = tri.astype(f32)
    hi = lax.Precision.HIGHEST
    a_cs = jnp.dot(trif, a_dt, precision=hi, preferred_element_type=f32)
    a_cs_t = lax.dot_general(a_dt, trif, (((0,), (1,)), ((), ())), precision=hi, preferred_element_type=f32)
    xs = xc[:, :SSD_WIDTH]
    acs_p = _spread_heads(a_cs, P)
    acs_t = _spread_heads(a_cs, T)
    xdt = xs * _spread_heads(dtp, P)
    skip = _spread_heads(dvec, P)
    to_end = jnp.exp(acs_p[T - 1:T] - acs_p)
    low = lax.broadcasted_iota(jnp.int32, (1, 2 * P), 1) < P
    low_rows = lax.broadcasted_iota(jnp.int32, (2 * P, 1), 0) < P
    ys, states = [], []
    for j in range(SSD_HEADS // 2):
        g = 2 * j // (SSD_HEADS // SSD_GROUPS)
        if 2 * j % (SSD_HEADS // SSD_GROUPS) == 0:
            bg = xc[:, SSD_WIDTH + g * N:SSD_WIDTH + (g + 1) * N]
            cg = xc[:, SSD_WIDTH + SSD_GROUPS * N + g * N:SSD_WIDTH + SSD_GROUPS * N + (g + 1) * N]
            cb = _dot(cg, bg, _NT)
        lanes = slice(2 * j * P, 2 * (j + 1) * P)
        st = state[lanes, :]
        diag, last = [], []
        for h in (2 * j, 2 * j + 1):
            decay = jnp.exp(jnp.where(tri, acs_t[:, h * T:(h + 1) * T] - a_cs_t[h:h + 1, :], -1e30))
            diag.append(_dot(cb * decay, xdt[:, lanes]))
            last.append(jnp.exp(a_cs_t[h:h + 1, T - 1:T]))
        y = (jnp.where(low, diag[0], diag[1]) + _dot(cg, st, _NT) * jnp.exp(acs_p[:, lanes])
             + xs[:, lanes] * skip[:, lanes])
        ys.append(y)
        states.append(jnp.where(low_rows, last[0], last[1]) * st + _dot(xdt[:, lanes] * to_end[:, lanes], bg, _TN))
    y = jnp.concatenate(ys, axis=1) * jax.nn.silu(z)
    out = y * lax.rsqrt(jnp.mean(y * y, axis=-1, keepdims=True) + RMS_EPS) * norm_w
    return (out,), (xbc[T - 8:], jnp.concatenate(states, axis=0))


def _merge_tile(pid, pa, pb, pc, g0, g1, g2):
    return (jax.nn.sigmoid(g0) * pa + jax.nn.sigmoid(g1) * pb + jax.nn.sigmoid(g2) * pc,)


def loss_and_grad(y, target, tm=512):
    S, D = y.shape
    nt = S // tm

    def body(y_ref, t_ref, dy_ref, l_ref, acc):
        i = pl.program_id(0)

        @pl.when(i == 0)
        def _():
            acc[...] = jnp.zeros_like(acc)

        diff = y_ref[...] - t_ref[...]
        dy_ref[...] = diff * (1.0 / D)
        acc[...] += jnp.sum((diff * diff).reshape(tm // 8, 8, D), axis=0)

        @pl.when(i == nt - 1)
        def _():
            l_ref[...] = jnp.broadcast_to(0.5 / D * jnp.sum(acc[...]), l_ref.shape)

    dy, l = pl.pallas_call(
        body, name="loss_head", grid=(nt,),
        in_specs=[pl.BlockSpec((tm, D), lambda i: (i, 0))] * 2,
        out_specs=[pl.BlockSpec((tm, D), lambda i: (i, 0)), pl.BlockSpec((8, 128), lambda i: (0, 0))],
        out_shape=[jax.ShapeDtypeStruct((S, D), f32), jax.ShapeDtypeStruct((8, 128), f32)],
        scratch_shapes=[pltpu.VMEM((8, D), f32)],
        compiler_params=_cparams(1))(y, target)
    return dy, l[0, 0]


def _row_tile(R, C, budget=1 << 20):
    best = R
    for t in range(8, R, 8):
        if R % t == 0 and t * C * 4 <= budget:
            best = t
    if best == R and R * C * 4 > budget:
        for t in range(8, R, 8):
            if R % t == 0:
                return t
    return best


def _as2d(t, lead=0):
    return t.reshape(t.shape[:lead] + (math.prod(t.shape[lead:-1]), t.shape[-1]))


def adamw(name, w, gslots, m, v):
    shape = w.shape
    n = gslots.shape[0]
    C = shape[-1]
    R = math.prod(shape[:-1])
    lanes = -(-C // 128) * 128
    tr = _row_tile(R, lanes * (n + 7), budget=10 << 20)

    def body(w_ref, g_ref, m_ref, v_ref, go_ref, d_ref, nm_ref, nv_ref):
        gg = g_ref[0].astype(f32)
        for s in range(1, n):
            gg = gg + g_ref[s].astype(f32)
        go_ref[...] = gg
        nm = ADAM_B1 * m_ref[...] + (1.0 - ADAM_B1) * gg
        nv = ADAM_B2 * v_ref[...] + (1.0 - ADAM_B2) * jnp.square(gg)
        m_hat = nm / (1.0 - ADAM_B1 ** ADAM_STEP)
        v_hat = nv / (1.0 - ADAM_B2 ** ADAM_STEP)
        d_ref[...] = -ADAM_LR * (m_hat / (jnp.sqrt(v_hat) + ADAM_EPS) + ADAM_WD * w_ref[...])
        nm_ref[...] = nm
        nv_ref[...] = nv

    spec = pl.BlockSpec((tr, C), lambda i: (i, 0))
    res = pl.pallas_call(
        body, name=name, grid=(R // tr,),
        in_specs=[spec, pl.BlockSpec((n, tr, C), lambda i: (0, i, 0)), spec, spec], out_specs=[spec] * 4,
        out_shape=[jax.ShapeDtypeStruct((R, C), f32)] * 4,
        compiler_params=_cparams(1))(w.reshape(R, C), gslots.reshape(n, R, C), m.reshape(R, C), v.reshape(R, C))
    return tuple(t.reshape(shape) for t in res)


PACK_ROWS = 256


def sum_slots(name, x):
    n, R, C = x.shape

    def body(x_ref, o_ref):
        acc = x_ref[0]
        for s in range(1, n):
            acc = acc + x_ref[s]
        o_ref[...] = acc

    return pl.pallas_call(
        body, name=name, grid=(R // PACK_ROWS,),
        in_specs=[pl.BlockSpec((n, PACK_ROWS, C), lambda i: (0, i, 0))],
        out_specs=pl.BlockSpec((PACK_ROWS, C), lambda i: (i, 0)),
        out_shape=jax.ShapeDtypeStruct((R, C), f32), compiler_params=_cparams(1))(x)


def _pack(parts):
    flat = jnp.concatenate([p.reshape(-1) for p in parts])
    unit = 128 * PACK_ROWS
    tot = -(-flat.shape[0] // unit) * unit
    return jnp.pad(flat, (0, tot - flat.shape[0])).reshape(tot // 128, 128)


def _unpack(buf, shapes):
    flat = buf.reshape(-1)
    out, off = [], 0
    for s in shapes:
        size = math.prod(s)
        out.append(flat[off:off + size].reshape(s))
        off += size
    return out


def _comm_sems(nt):
    return [pltpu.SemaphoreType.DMA((nt, N_DEV - 1)), pltpu.SemaphoreType.DMA((nt, N_DEV - 1)), pltpu.SemaphoreType.DMA((nt,))]


def exchange_side(srcs, modes):
    nt = len(srcs)
    slabs = []
    for s, mode in zip(srcs, modes):
        R, C = s.shape
        slabs.append({'all': (R, C), 'rows': (R // N_DEV, C), 'cols': (R, C // N_DEV)}[mode])

    def piece(ref, mode, slab, p):
        if mode == 'all':
            return ref
        if mode == 'rows':
            return ref.at[pl.ds(p * slab[0], slab[0]), :]
        return ref.at[:, pl.ds(p * slab[1], slab[1])]

    def copies(src_refs, out_refs, send_sems, recv_sems, local_sems):
        x, y, c = lax.axis_index("x"), lax.axis_index("y"), lax.axis_index("c")
        me = 4 * x + 2 * y + c
        out = []
        for k in (1, 2, 4, 3, 5, 6, 7):
            px = 1 - x if k & 4 else x
            py = 1 - y if k & 2 else y
            pc = 1 - c if k & 1 else c
            for t in range(nt):
                out.append(pltpu.make_async_remote_copy(
                    src_ref=piece(src_refs[t], modes[t], slabs[t], 4 * px + 2 * py + pc), dst_ref=out_refs[t].at[me],
                    send_sem=send_sems.at[t, k - 1], recv_sem=recv_sems.at[t, k - 1],
                    device_id=(px, py, pc), device_id_type=pl.DeviceIdType.MESH))
        for t in range(nt):
            out.append(pltpu.make_async_copy(piece(src_refs[t], modes[t], slabs[t], me), out_refs[t].at[me], local_sems.at[t]))
        return out

    def start(*refs):
        for cp in copies(*refs):
            cp.start()

    def finish(*refs):
        for cp in copies(*refs):
            cp.wait()

    out_shapes = [jax.ShapeDtypeStruct((N_DEV,) + sl, s.dtype) for s, sl in zip(srcs, slabs)]
    return Side(list(srcs), out_shapes, _comm_sems(nt), [(0.0, start), (1.0, finish)])


def gather_side(srcs):
    nt = len(srcs)

    def plan(src_refs, out_refs, send_sems, recv_sems, local_sems):
        x, y, c = lax.axis_index("x"), lax.axis_index("y"), lax.axis_index("c")
        me, sibling = (x, y, c), (x, y, 1 - c)
        chips = [(1 - x, y), (x, 1 - y), (1 - x, 1 - y)]

        def slot(t, dev):
            return out_refs[t].at[4 * dev[0] + 2 * dev[1] + dev[2]]

        def copy(t, k, block, to, src=None):
            return pltpu.make_async_remote_copy(
                src_ref=slot(t, block) if src is None else src, dst_ref=slot(t, block),
                send_sem=send_sems.at[t, k], recv_sem=recv_sems.at[t, k], device_id=to, device_id_type=pl.DeviceIdType.MESH)

        mine = [pltpu.make_async_copy(src_refs[t], slot(t, me), local_sems.at[t]) for t in range(nt)]
        first = []
        for t in range(nt):
            first.append(copy(t, 0, me, sibling, src=src_refs[t]))
            first += [copy(t, 1 + j, me, (*chip, c), src=src_refs[t]) for j, chip in enumerate(chips)]
        landed = [copy(t, 1 + j, (*chip, c), me) for j, chip in enumerate(chips) for t in range(nt)]
        passed = [copy(t, 4 + j, (*chip, c), sibling) for j, chip in enumerate(chips) for t in range(nt)]
        from_sibling = [copy(t, 0, sibling, me) for t in range(nt)]
        from_sibling += [copy(t, 4 + j, (*chip, 1 - c), me) for t in range(nt) for j, chip in enumerate(chips)]
        return mine, first, landed, passed, from_sibling

    def start(*refs):
        mine, first, _, _, _ = plan(*refs)
        for cp in mine + first:
            cp.start()

    def forward(*refs):
        _, _, landed, passed, _ = plan(*refs)
        for got, fwd in zip(landed, passed):
            got.wait_recv()
            fwd.start()

    def finish(*refs):
        mine, first, _, passed, from_sibling = plan(*refs)
        for cp in from_sibling:
            cp.wait_recv()
        for cp in first + passed:
            cp.wait_send()
        for cp in mine:
            cp.wait()

    out_shapes = [jax.ShapeDtypeStruct((N_DEV,) + s.shape, s.dtype) for s in srcs]
    return Side(list(srcs), out_shapes, _comm_sems(nt), [(0.0, start), (0.5, forward), (1.0, finish)])


def run_side(name, side):
    ns = len(side.arrs)

    def body(*refs):
        for _, phase in side.phases:
            phase(refs[:ns], refs[ns:ns + len(side.out_shapes)], *refs[ns + len(side.out_shapes):])

    return list(pl.pallas_call(
        body, name=name, in_specs=[_ANY] * ns, out_specs=[_ANY] * len(side.out_shapes),
        out_shape=list(side.out_shapes), scratch_shapes=list(side.sem_shapes))(*side.arrs))


def _relayout_w_in(w):
    offs = [0]
    for s in IN_SPLITS:
        offs.append(offs[-1] + s)
    p = [w[:, offs[i]:offs[i + 1]] for i in range(len(IN_SPLITS))]
    ua, za, q, k, v, zb, xbc, dt, zc, gates = p
    dtp = jnp.pad(dt, ((0, 0), (0, HPAD - dt.shape[1])))
    return (jnp.concatenate([ua, za, dtp], 1), w[:, offs[2]:offs[5]], jnp.concatenate([xbc, zb, zc], 1), gates)


def _pad_lanes(v, n=HPAD):
    return jnp.pad(v.reshape(1, -1), ((0, 0), (0, n - v.shape[-1])))


def _s5_prep_args(W):
    g2 = S5_GROUPS * S5_STATE
    w = g2 // S5_CHUNKS
    a_re, a_im = W['s5_a_re'].reshape(1, g2), W['s5_a_im'].reshape(1, g2)
    ls = W['s5_log_step'].reshape(1, S5_GROUPS)
    btr, bti = W['s5_b_re'].reshape(g2, S5_GROUP).T, W['s5_b_im'].reshape(g2, S5_GROUP).T
    ctr = W['s5_c_re'].transpose(1, 0, 2).reshape(S5_GROUP, g2)
    cti = W['s5_c_im'].transpose(1, 0, 2).reshape(S5_GROUP, g2)
    col = lambda a, rows: Arg(a, (rows, w), lambda o: (0, o), 'tile')
    return [col(a_re, 1), col(a_im, 1), _whole(ls, 'acc'), col(btr, S5_GROUP), col(bti, S5_GROUP), col(ctr, S5_GROUP), col(cti, S5_GROUP)]


def _s5_prep_outs():
    g2 = S5_GROUPS * S5_STATE
    w = g2 // S5_CHUNKS
    rows = (S5_ND, S5_ND, S5_SUB, S5_SUB, 128, 128, 128, 128)
    return [Out((r, g2), f32, (r, w), lambda o: (0, o)) for r in rows]


def _s5_args(A, prep, dvec, S):
    w = S5_GROUPS * S5_STATE // S5_CHUNKS
    args = [Arg(A, (S5_TILE, 128), lambda o, t: (t, o), 'tile', (S, S5_WIDTH), None, bf16)]
    for p in prep:
        args.append(Arg(p, (p.shape[0], w), lambda o, t: (0, o), 'acc0'))
    args.append(Arg(dvec, (1, 128), lambda o, t: (0, o), 'acc0'))
    return args


def _attn_args(QKV, g, r, qw, kw, S):
    L = S // r
    nq, rb = _attn_plan(r)
    block = (nq * ATT_BLOCK, rb * ATT_GW)
    gshape = (L, r * ATT_GW)
    gimap = lambda rho, n: (n, rho)
    if r == 1:
        mk = lambda j: Arg(QKV, block, lambda rho, n, j=j: (n, j), 'tile', gshape, gimap, bf16)
    else:
        def mk(j):
            view = QKV[:, j * ATT_GW:(j + 1) * ATT_GW].reshape(L, r * ATT_GW)
            return Arg(view, block, gimap, 'tile', None, None, bf16)
    return [mk(g), mk(3 + g), mk(6 + g), _whole(qw, 'acc'), _whole(kw, 'acc')]


def _attn_plan(r):
    return (4, 1) if r == 1 else (1, min(r, 4))


def _attn_grid(r, S):
    nq, rb = _attn_plan(r)
    return (r // rb, S // r // ATT_BLOCK // nq)


def _attn_carry(r):
    return ((ATT_BLOCK, _attn_plan(r)[1] * ATT_GW),) * 2


def _ssd_args(C, A, W, S):
    T = SSD_CHUNK
    return [Arg(C, (T, SSD_CONV_DIM), lambda o, t: (t, 0), 'tile', (S, SSD_CONV_DIM), None, bf16),
            Arg(A, (T, HPAD), lambda o, t: (t, 2 * S5_WIDTH // HPAD), 'tile', (S, HPAD), lambda o, t: (t, 0), bf16),
            Arg(C, (T, SSD_WIDTH), lambda o, t: (t, 2), 'tile', (S, SSD_WIDTH), lambda o, t: (t, 0), bf16),
            _whole(W['conv_w'], 'acc'), _whole(W['conv_b'].reshape(1, -1), 'acc'),
            _whole(_pad_lanes(W['dt_bias']), 'acc'), _whole(_pad_lanes(W['ssd_a_log']), 'acc'),
            _whole(_pad_lanes(W['ssd_d']), 'acc'), _whole(W['ssd_norm_w'].reshape(1, -1), 'acc')]


_SSD_CARRY = ((8, SSD_CONV_DIM), (SSD_WIDTH, SSD_STATE))
_S5_CARRY = ((1, 512), (1, 512))


def layer_fwd(li, x, W, side=None):
    S = x.shape[0]
    n = lambda s: f"l{li}_{s}"
    sv = {'x': x}
    (h,) = map_fwd(n("norm"), _rmsnorm_tile, (S // 512,), [_rows(x, 512), _whole(W['norm_w'].reshape(1, -1))],
                   [Out((S, D_MODEL), bf16, (512, D_MODEL), lambda i: (i, 0))])
    wA, wQ, wC, wG = W['w_in_pieces']
    A = matmul(n("in_a"), h, wA)
    QKV = matmul(n("in_qkv"), h, wQ)
    C = matmul(n("in_c"), h, wC)
    G = matmul(n("in_g"), h, wG)
    sv.update(h=h, A=A, QKV=QKV, C=C, G=G)

    prep = map_fwd(n("s5_prep"), _s5_prep_tile, (S5_CHUNKS,), _s5_prep_args(W), _s5_prep_outs())
    dvec = W['s5_d'].reshape(1, -1)
    (g,), s5_ck, *got = scan_fwd(n("s5_scan"), _s5_tile, (S5_CHUNKS, S // S5_TILE), _S5_CARRY, _s5_args(A, prep, dvec, S),
                                 [Out((S, S5_WIDTH), f32, (S5_TILE, 128), lambda o, t: (t, o))], side=side)
    glu = matmul(n("glu"), g, W['s5_glu_w'])
    glu_b = W['s5_glu_b'].reshape(1, -1)
    (ya,) = map_fwd(n("glu_gate"), _glu_tile, (S // 512,),
                    [_rows(g, 512), _rows(glu, 512), _rows(A, 512, col=1, width=S5_WIDTH), _whole(glu_b)],
                    [Out((S, S5_WIDTH), bf16, (512, S5_WIDTH), lambda i: (i, 0))])
    sv.update(prep=prep, g=g, glu=glu, ya=ya, s5_ck=s5_ck)

    qw, kw = W['q_norm_w'].reshape(1, -1), W['k_norm_w'].reshape(1, -1)
    att, att_ck = [], []
    for gi, (window, r) in enumerate(ATT_PAIRS):
        assert window // r == ATT_BLOCK and S % (r * ATT_BLOCK) == 0
        L = S // r
        nq, rb = _attn_plan(r)
        assert S // r // ATT_BLOCK % nq == 0
        spec = Out((L, r * ATT_GW), f32, (nq * ATT_BLOCK, rb * ATT_GW), lambda rho, nb: (nb, rho))
        (o, lse), ck = scan_fwd(n(f"attn{gi}"), _attn_tile, _attn_grid(r, S), _attn_carry(r), _attn_args(QKV, gi, r, qw, kw, S), [spec, spec])
        att += [o.reshape(S, ATT_GW), lse.reshape(S, ATT_GW)]
        att_ck.append(ck)
    (yb,) = map_fwd(n("combine"), _combine_tile, (S // 512,),
                    [_rows(t, 512) for t in att] + [_rows(C, 512, col=SSD_CONV_DIM // ATT_GW, width=ATT_GW)],
                    [Out((S, ATT_GW), bf16, (512, ATT_GW), lambda i: (i, 0))])
    sv.update(att=att, att_ck=att_ck, yb=yb)

    (yc,), ssd_ck = scan_fwd(n("ssd"), _ssd_tile, (1, S // SSD_CHUNK), _SSD_CARRY, _ssd_args(C, A, W, S),
                             [Out((S, SSD_WIDTH), bf16, (SSD_CHUNK, SSD_WIDTH), lambda o, t: (t, 0))])
    sv.update(yc=yc, ssd_ck=ssd_ck)

    pa = matmul(n("proj_a"), ya, W['proj_a'])
    pb = matmul(n("proj_b"), yb, W['proj_b'])
    pc = matmul(n("proj_c"), yc, W['proj_c'])
    (merged,) = map_fwd(n("merge"), _merge_tile, (S // 256,),
                        [_rows(pa, 256), _rows(pb, 256), _rows(pc, 256)] + [_rows(G, 256, col=j, width=D_MODEL) for j in range(3)],
                        [Out((S, D_MODEL), bf16, (256, D_MODEL), lambda i: (i, 0))])
    out = matmul(n("w_out"), merged, W['w_out'], add=x)
    sv.update(pa=pa, pb=pb, pc=pc, merged=merged)
    return out, sv, (got[0] if got else None)


def layer_bwd(li, dout, sv, W, side=None):
    S = dout.shape[0]
    n = lambda s: f"l{li}_{s}"
    gr = {}
    x, A, QKV, C, G = sv['x'], sv['A'], sv['QKV'], sv['C'], sv['G']

    dmerged = matmul(n("d_merged"), dout, W['w_out'], 'nt')
    gr['w_out'] = wgrad(n("g_w_out"), sv['merged'], dout)
    margs = [_rows(sv['pa'], 256, gdtype=bf16), _rows(sv['pb'], 256, gdtype=bf16), _rows(sv['pc'], 256, gdtype=bf16)] + \
            [_rows(G, 256, col=j, width=D_MODEL, gshape=(S, D_MODEL), gdtype=bf16) for j in range(3)]
    dpa, dpb, dpc, dg0, dg1, dg2 = map_bwd(n("merge_bwd"), _merge_tile, (S // 256,), margs, [_rows(dmerged, 256)], list(range(6)))
    dya = matmul(n("d_ya"), dpa, W['proj_a'], 'nt')
    dyb = matmul(n("d_yb"), dpb, W['proj_b'], 'nt')
    dyc = matmul(n("d_yc"), dpc, W['proj_c'], 'nt')
    gr['proj_a'] = wgrad(n("g_proj_a"), sv['ya'], dpa)
    gr['proj_b'] = wgrad(n("g_proj_b"), sv['yb'], dpb)
    gr['proj_c'] = wgrad(n("g_proj_c"), sv['yc'], dpc)

    glu_b = W['s5_glu_b'].reshape(1, -1)
    gargs = [_rows(sv['g'], 512), _rows(sv['glu'], 512, gdtype=bf16),
             _rows(A, 512, col=1, width=S5_WIDTH, gshape=(S, S5_WIDTH), gdtype=bf16), _whole(glu_b, 'acc')]
    dg_a, dglu, dza, dglu_b = map_bwd(n("glu_gate_bwd"), _glu_tile, (S // 512,), gargs, [_rows(dya, 512)], [0, 1, 2, 3])
    gr['s5_glu_b'] = dglu_b.reshape(-1)
    dg = matmul(n("d_g"), dglu, W['s5_glu_w'], 'nt', add=dg_a)
    gr['s5_glu_w'] = wgrad(n("g_glu_w"), sv['g'], dglu)
    dvec = W['s5_d'].reshape(1, -1)
    sargs = _s5_args(A, sv['prep'], dvec, S)
    res = scan_bwd(n("s5_scan_bwd"), _s5_tile, (S5_CHUNKS, S // S5_TILE), _S5_CARRY, sargs, sv['s5_ck'],
                   [Arg(dg, (S5_TILE, 128), lambda o, t: (t, o))], list(range(len(sargs))), bwd_fn=_s5_tile_bwd)
    dua, dprep, dd = res[0], res[1:9], res[9]
    gr['s5_d'] = dd.reshape(-1)
    pargs = _s5_prep_args(W)
    pouts = _s5_prep_outs()
    da_re, da_im, dls, dbtr, dbti, dctr, dcti = map_bwd(
        n("s5_prep_bwd"), _s5_prep_tile, (S5_CHUNKS,), pargs,
        [Arg(d, o.block, o.imap) for d, o in zip(dprep, pouts)], list(range(7)))
    gshape = (S5_GROUPS, S5_STATE)
    gr['s5_a_re'], gr['s5_a_im'] = da_re.reshape(gshape), da_im.reshape(gshape)
    gr['s5_log_step'] = dls.reshape(-1)
    gr['s5_b_re'] = dbtr.T.reshape(S5_GROUPS, S5_STATE, S5_GROUP)
    gr['s5_b_im'] = dbti.T.reshape(S5_GROUPS, S5_STATE, S5_GROUP)
    gr['s5_c_re'] = dctr.reshape(S5_GROUP, S5_GROUPS, S5_STATE).transpose(1, 0, 2)
    gr['s5_c_im'] = dcti.reshape(S5_GROUP, S5_GROUPS, S5_STATE).transpose(1, 0, 2)

    cargs = [_rows(t, 512) for t in sv['att']] + \
            [_rows(C, 512, col=SSD_CONV_DIM // ATT_GW, width=ATT_GW, gshape=(S, ATT_GW), gdtype=bf16)]
    cres = map_bwd(n("combine_bwd"), _combine_tile, (S // 512,), cargs, [_rows(dyb, 512)], list(range(7)))
    dzb = cres[6]
    qw, kw = W['q_norm_w'].reshape(1, -1), W['k_norm_w'].reshape(1, -1)
    dqs, dks, dvs = [], [], []
    dqw = dkw = None
    for gi, (window, r) in enumerate(ATT_PAIRS):
        L = S // r
        nq, rb = _attn_plan(r)
        dspec = lambda t: Arg(t.reshape(L, r * ATT_GW), (nq * ATT_BLOCK, rb * ATT_GW), lambda rho, nb: (nb, rho))
        dq, dk, dv, dqw_g, dkw_g = scan_bwd(n(f"attn{gi}_bwd"), _attn_tile, _attn_grid(r, S), _attn_carry(r),
                                            _attn_args(QKV, gi, r, qw, kw, S), sv['att_ck'][gi],
                                            [dspec(cres[2 * gi]), dspec(cres[2 * gi + 1])], [0, 1, 2, 3, 4])
        dqs.append(dq.reshape(S, ATT_GW))
        dks.append(dk.reshape(S, ATT_GW))
        dvs.append(dv.reshape(S, ATT_GW))
        dqw = dqw_g if dqw is None else dqw + dqw_g
        dkw = dkw_g if dkw is None else dkw + dkw_g
    gr['q_norm_w'], gr['k_norm_w'] = dqw.reshape(-1), dkw.reshape(-1)

    ssd_args = _ssd_args(C, A, W, S)
    sres = scan_bwd(n("ssd_bwd"), _ssd_tile, (1, S // SSD_CHUNK), _SSD_CARRY, ssd_args, sv['ssd_ck'],
                    [Arg(dyc, (SSD_CHUNK, SSD_WIDTH), lambda o, t: (t, 0))], list(range(9)), side=side)
    sres, got = sres if side else (sres, None)
    dxbc, ddt, dzc = sres[0], sres[1], sres[2]
    gr['conv_w'] = sres[3]
    gr['conv_b'] = sres[4].reshape(-1)
    gr['dt_bias'] = sres[5].reshape(-1)[:SSD_HEADS]
    gr['ssd_a_log'] = sres[6].reshape(-1)[:SSD_HEADS]
    gr['ssd_d'] = sres[7].reshape(-1)[:SSD_HEADS]
    gr['ssd_norm_w'] = sres[8].reshape(-1)

    dpieces = [jnp.concatenate([dua, dza, ddt], axis=1), jnp.concatenate(dqs + dks + dvs, axis=1),
               jnp.concatenate([dxbc, dzb, dzc], axis=1), jnp.concatenate([dg0, dg1, dg2], axis=1)]
    dh = matmul_nt_sum(n("d_h"), dpieces, list(W['w_in_pieces']))
    gr['w_in'] = _unrelayout_w_in_grad([wgrad(n(f"g_w_in{j}"), sv['h'], dp) for j, dp in enumerate(dpieces)])
    nargs = [_rows(x, 512), _whole(W['norm_w'].reshape(1, -1), 'acc')]
    dx, dnw = map_bwd(n("norm_bwd"), _rmsnorm_tile, (S // 512,), nargs, [_rows(dh, 512)], [0, 1], add={0: _rows(dout, 512)})
    gr['norm_w'] = dnw.reshape(-1)
    return dx, gr, got


def _unrelayout_w_in_grad(pieces):
    gA, gQ, gC, gG = pieces
    uaza, dt = gA[:, :2 * S5_WIDTH], gA[:, 2 * S5_WIDTH:2 * S5_WIDTH + SSD_HEADS]
    xbc, zb, zc = gC[:, :SSD_CONV_DIM], gC[:, SSD_CONV_DIM:SSD_CONV_DIM + ATT_GW], gC[:, SSD_CONV_DIM + ATT_GW:]
    return jnp.concatenate([uaza, gQ, zb, xbc, dt, zc, gG], axis=1)


def kernel(x, norm_w, w_in, s5_a_re, s5_a_im, s5_log_step, s5_b_re, s5_b_im, s5_c_re, s5_c_im, s5_d, s5_glu_w, s5_glu_b, q_norm_w, k_norm_w, conv_w, conv_b, dt_bias, ssd_a_log, ssd_d, ssd_norm_w, proj_a, proj_b, proj_c, w_out, loss_target, m_norm_w, m_w_in, m_s5_a_re, m_s5_a_im, m_s5_log_step, m_s5_b_re, m_s5_b_im, m_s5_c_re, m_s5_c_im, m_s5_d, m_s5_glu_w, m_s5_glu_b, m_q_norm_w, m_k_norm_w, m_conv_w, m_conv_b, m_dt_bias, m_ssd_a_log, m_ssd_d, m_ssd_norm_w, m_proj_a, m_proj_b, m_proj_c, m_w_out, v_norm_w, v_w_in, v_s5_a_re, v_s5_a_im, v_s5_log_step, v_s5_b_re, v_s5_b_im, v_s5_c_re, v_s5_c_im, v_s5_d, v_s5_glu_w, v_s5_glu_b, v_q_norm_w, v_k_norm_w, v_conv_w, v_conv_b, v_dt_bias, v_ssd_a_log, v_ssd_d, v_ssd_norm_w, v_proj_a, v_proj_b, v_proj_c, v_w_out):
    args = dict(locals())
    w = {k: args[k] for k in WEIGHTS}
    m = {k: args['m_' + k] for k in WEIGHTS}
    v = {k: args['v_' + k] for k in WEIGHTS}
    depth = norm_w.shape[0]
    S = x.shape[1]
    xs = x.reshape(S, D_MODEL)
    tgt = loss_target.reshape(S, D_MODEL)

    def weight_gather(li):
        return gather_side([w[k][li].astype(bf16) for k in SHARDED])

    def assemble(li, gathered):
        W = {k: w[k][li] for k in WEIGHTS if k not in SHARDED}
        for k, t in zip(SHARDED, gathered):
            n_dev, R, C = t.shape
            W[k] = t.reshape(n_dev * R, C) if k in ROW_SHARDED else t.transpose(1, 0, 2).reshape(R, n_dev * C)
        W['w_in_pieces'] = _relayout_w_in(W['w_in'])
        return W

    layers = [assemble(0, run_side("gather_weights0", weight_gather(0)))]
    act, saved = xs, []
    for li in range(depth):
        act, sv, got = layer_fwd(li, act, layers[li], weight_gather(li + 1) if li + 1 < depth else None)
        saved.append(sv)
        if got is not None:
            layers.append(assemble(li + 1, got))
    dy, loss_local = loss_and_grad(act, tgt)
    loss = lax.psum(loss_local, ("x", "y", "c"))

    big = [k for k in SHARDED if k != 'conv_w']

    def grad_scatter(gr, extra=()):
        return exchange_side([gr[k] for k in big] + list(extra), ['rows' if k in ROW_SHARDED else 'cols' for k in big] + ['all'] * len(extra))

    grads, slots = [None] * depth, [None] * depth
    for li in reversed(range(depth)):
        dy, grads[li], got = layer_bwd(li, dy, saved[li], layers[li], grad_scatter(grads[li + 1]) if li + 1 < depth else None)
        if got is not None:
            slots[li + 1] = got
    small_keys = [k for k in WEIGHTS if k not in SHARDED] + ['conv_w']
    stacked = [jnp.stack([grads[li][k] for li in range(depth)], axis=0) for k in small_keys]
    *slots[0], small_slots = run_side("scatter_grads0", grad_scatter(grads[0], [_pack(stacked)]))
    grad_x = dy.reshape(x.shape)
    per_layer = {}
    for li in range(depth):
        for k, s in zip(big, slots[li]):
            per_layer[li, k] = adamw(f"adamw_{k}{li}", w[k][li], s, m[k][li], v[k][li])
    result = {k: tuple(jnp.stack([per_layer[li, k][j] for li in range(depth)], axis=0) for j in range(4)) for k in big}

    totals = _unpack(sum_slots("sum_small_grads", small_slots), [t.shape for t in stacked])
    for k, g in zip(small_keys, totals):
        if k == 'conv_w':
            width = w[k].shape[-1]
            me = 4 * lax.axis_index("x") + 2 * lax.axis_index("y") + lax.axis_index("c")
            g = lax.dynamic_slice_in_dim(g, me * width, width, axis=2)
        result[k] = adamw("adamw_" + k, w[k], g[None], m[k], v[k])

    return (loss, grad_x, *[result[k][0] for k in WEIGHTS], *[result[k][1] for k in WEIGHTS],
            *[result[k][2] for k in WEIGHTS], *[result[k][3] for k in WEIGHTS])
```

```python
import functools
import math
from typing import Any, NamedTuple

import jax
import jax.numpy as jnp
from jax import lax
from jax.experimental import pallas as pl
from jax.experimental.pallas import tpu as pltpu

f32 = jnp.float32
bf16 = jnp.bfloat16

N_DEV = 8
D_MODEL = 1024
RMS_EPS = 1e-6
S5_WIDTH = 512
S5_GROUPS = 32
S5_GROUP = 16
S5_STATE = 64
S5_TILE = 512
S5_SUB = 8
S5_ND = 3
S5_CHUNKS = 4
ATT_HEAD_DIM = 64
ATT_PAIRS = ((128, 1), (512, 4), (2048, 16))
ATT_HPG = 4
ATT_BLOCK = 128
ATT_GW = ATT_HPG * ATT_HEAD_DIM
ATT_WIDTH = 768
SSD_HEADS = 12
SSD_HEAD_DIM = 64
SSD_WIDTH = 768
SSD_STATE = 128
SSD_GROUPS = 2
SSD_CHUNK = 128
SSD_CONV = 4
SSD_CONV_DIM = 1280
HPAD = 128
IN_SPLITS = (512, 512, 768, 768, 768, 256, 1280, 12, 768, 3072)
ADAM_LR, ADAM_B1, ADAM_B2, ADAM_EPS, ADAM_WD, ADAM_STEP = 0.001, 0.9, 0.999, 1e-08, 0.01, 10
VMEM_LIMIT = 56 * 1024 * 1024

WEIGHTS = ['norm_w', 'w_in', 's5_a_re', 's5_a_im', 's5_log_step', 's5_b_re', 's5_b_im', 's5_c_re',
           's5_c_im', 's5_d', 's5_glu_w', 's5_glu_b', 'q_norm_w', 'k_norm_w', 'conv_w', 'conv_b',
           'dt_bias', 'ssd_a_log', 'ssd_d', 'ssd_norm_w', 'proj_a', 'proj_b', 'proj_c', 'w_out']
ROW_SHARDED = ('w_in', 's5_glu_w', 'w_out')
SHARDED = ROW_SHARDED + ('conv_w', 'proj_a', 'proj_b', 'proj_c')


class Arg(NamedTuple):
    arr: Any
    block: tuple
    imap: Any
    kind: str = 'const'
    gshape: Any = None
    gimap: Any = None
    gdtype: Any = None


class Out(NamedTuple):
    shape: tuple
    dtype: Any
    block: tuple
    imap: Any


def _cparams(n):
    return pltpu.CompilerParams(dimension_semantics=("arbitrary",) * n, vmem_limit_bytes=VMEM_LIMIT)


def _rows(a, tm, kind='tile', col=0, width=None, gshape=None, gcol=None, gdtype=None):
    width = a.shape[1] if width is None else width
    g = None if gshape is None else (lambda i, gc=(0 if gcol is None else gcol): (i, gc))
    return Arg(a, (tm, width), lambda i, c=col: (i, c), kind, gshape, g, gdtype)


def _whole(a, kind='const'):
    nd = a.ndim
    return Arg(a, a.shape, lambda *i, nd=nd: (0,) * nd, kind)


def map_fwd(name, fn, grid, args, outs):
    n_in = len(args)

    def body(*refs):
        pid = tuple(pl.program_id(a) for a in range(len(grid)))
        res = fn(pid, *[r[...] for r in refs[:n_in]])
        for o, r in zip(refs[n_in:], res):
            o[...] = r.astype(o.dtype)

    res = pl.pallas_call(
        body, name=name, grid=grid,
        in_specs=[pl.BlockSpec(a.block, a.imap) for a in args],
        out_specs=[pl.BlockSpec(o.block, o.imap) for o in outs],
        out_shape=[jax.ShapeDtypeStruct(o.shape, o.dtype) for o in outs],
        compiler_params=_cparams(len(grid)))(*[a.arr for a in args])
    return tuple(res)


def _grad_outs(args, wrt):
    outs = []
    for i in wrt:
        a = args[i]
        shape = a.arr.shape if a.gshape is None else a.gshape
        imap = a.imap if a.gimap is None else a.gimap
        outs.append(Out(shape, f32 if a.gdtype is None else a.gdtype, a.block, imap))
    return outs


def _store_grads(pid, args, wrt, grads, grefs, adds):
    first_all = functools.reduce(jnp.logical_and, [p == 0 for p in pid])
    first_in = functools.reduce(jnp.logical_and, [p == 0 for p in pid[1:]]) if len(pid) > 1 else first_all
    for j, i in enumerate(wrt):
        g = grads[j].astype(f32)
        ref = grefs[j]
        kind = args[i].kind
        if kind == 'tile':
            if j in adds:
                g = g + adds[j]
            ref[...] = g.astype(ref.dtype)
        else:
            first = first_all if kind == 'acc' else first_in

            @pl.when(first)
            def _(ref=ref):
                ref[...] = jnp.zeros_like(ref)

            ref[...] += g


def map_bwd(name, fn, grid, args, douts, wrt, add=None):
    add = add or {}
    n_in, n_d, n_add = len(args), len(douts), len(add)
    add_keys = sorted(add)
    gouts = _grad_outs(args, wrt)

    def body(*refs):
        pid = tuple(pl.program_id(a) for a in range(len(grid)))
        vals = [r[...] for r in refs[:n_in]]
        dvals = [r[...].astype(f32) for r in refs[n_in:n_in + n_d]]
        avals = {k: refs[n_in + n_d + j][...].astype(f32) for j, k in enumerate(add_keys)}
        grefs = refs[n_in + n_d + n_add:]

        def f(*w):
            full = list(vals)
            for i, x in zip(wrt, w):
                full[i] = x
            return tuple(fn(pid, *full))

        _, vjp = jax.vjp(f, *[vals[i] for i in wrt])
        grads = vjp(tuple(dvals))
        _store_grads(pid, args, wrt, grads, grefs, avals)

    ins = list(args) + list(douts) + [add[k] for k in add_keys]
    res = pl.pallas_call(
        body, name=name, grid=grid,
        in_specs=[pl.BlockSpec(a.block, a.imap) for a in ins],
        out_specs=[pl.BlockSpec(o.block, o.imap) for o in gouts],
        out_shape=[jax.ShapeDtypeStruct(o.shape, o.dtype) for o in gouts],
        compiler_params=_cparams(len(grid)))(*[a.arr for a in ins])
    return tuple(res)


class Side(NamedTuple):
    arrs: list
    out_shapes: list
    sem_shapes: list
    phases: list


def _run_side(side, step, total, src_refs, out_refs, sem_refs):
    for frac, phase in side.phases:
        @pl.when(step == int(round(frac * (total - 1))))
        def _(phase=phase):
            phase(src_refs, out_refs, *sem_refs)


_ANY = pl.BlockSpec(memory_space=pl.ANY)


def scan_fwd(name, fn, grid, carry_shapes, args, outs, side=None):
    no, nt = grid
    n_in, n_out, n_c = len(args), len(outs), len(carry_shapes)
    ns_in, ns_out = (len(side.arrs), len(side.out_shapes)) if side else (0, 0)
    cks = [Out((no, nt) + cs, f32, (None, None) + cs, lambda o, t, n=len(cs): (o, t) + (0,) * n) for cs in carry_shapes]

    def body(*refs):
        pid = (pl.program_id(0), pl.program_id(1))
        ins = refs[:n_in]
        sins = refs[n_in:n_in + ns_in]
        refs = refs[n_in + ns_in:]
        orefs = refs[:n_out]
        ckrefs = refs[n_out:n_out + n_c]
        souts = refs[n_out + n_c:n_out + n_c + ns_out]
        crefs = refs[n_out + n_c + ns_out:n_out + n_c + ns_out + n_c]
        if side:
            _run_side(side, pid[0] * nt + pid[1], no * nt, sins, souts, refs[n_out + n_c + ns_out + n_c:])

        @pl.when(pid[1] == 0)
        def _():
            for c in crefs:
                c[...] = jnp.zeros_like(c)

        carry = tuple(c[...] for c in crefs)
        for ck, c in zip(ckrefs, carry):
            ck[...] = c
        res, newc = fn(pid, carry, *[r[...] for r in ins])
        for o, r in zip(orefs, res):
            o[...] = r.astype(o.dtype)
        for c, v in zip(crefs, newc):
            c[...] = v

    allouts = list(outs) + cks
    res = pl.pallas_call(
        body, name=name, grid=grid,
        in_specs=[pl.BlockSpec(a.block, a.imap) for a in args] + [_ANY] * ns_in,
        out_specs=[pl.BlockSpec(o.block, o.imap) for o in allouts] + [_ANY] * ns_out,
        out_shape=[jax.ShapeDtypeStruct(o.shape, o.dtype) for o in allouts] + (list(side.out_shapes) if side else []),
        scratch_shapes=[pltpu.VMEM(cs, f32) for cs in carry_shapes] + (list(side.sem_shapes) if side else []),
        compiler_params=_cparams(2))(*[a.arr for a in args], *(side.arrs if side else []))
    if side:
        return tuple(res[:n_out]), tuple(res[n_out:n_out + n_c]), list(res[n_out + n_c:])
    return tuple(res[:n_out]), tuple(res[n_out:])


def scan_bwd(name, fn, grid, carry_shapes, args, ckpts, douts, wrt, bwd_fn=None, side=None):
    no, nt = grid
    n_in, n_d, n_c = len(args), len(douts), len(carry_shapes)
    ns_in, ns_out = (len(side.arrs), len(side.out_shapes)) if side else (0, 0)

    def rev(imap):
        return lambda o, t: imap(o, nt - 1 - t)

    rargs = [a._replace(imap=rev(a.imap), gimap=None if a.gimap is None else rev(a.gimap)) for a in args]
    rdouts = [a._replace(imap=rev(a.imap)) for a in douts]
    ckargs = [Arg(ck, (None, None) + cs, rev(lambda o, t, n=len(cs): (o, t) + (0,) * n)) for ck, cs in zip(ckpts, carry_shapes)]
    gouts = _grad_outs(rargs, wrt)

    def body(*refs):
        o, t = pl.program_id(0), pl.program_id(1)
        tt = nt - 1 - t
        vals = [r[...] for r in refs[:n_in]]
        dvals = [r[...].astype(f32) for r in refs[n_in:n_in + n_d]]
        carry = tuple(r[...] for r in refs[n_in + n_d:n_in + n_d + n_c])
        sins = refs[n_in + n_d + n_c:n_in + n_d + n_c + ns_in]
        refs = refs[n_in + n_d + n_c + ns_in:]
        grefs = refs[:len(wrt)]
        souts = refs[len(wrt):len(wrt) + ns_out]
        dcrefs = refs[len(wrt) + ns_out:len(wrt) + ns_out + n_c]
        if side:
            _run_side(side, o * nt + t, no * nt, sins, souts, refs[len(wrt) + ns_out + n_c:])

        @pl.when(t == 0)
        def _():
            for c in dcrefs:
                c[...] = jnp.zeros_like(c)

        def f(carry, *w):
            full = list(vals)
            for i, x in zip(wrt, w):
                full[i] = x
            res, newc = fn((o, tt), carry, *full)
            return tuple(res), tuple(newc)

        dcarry = tuple(c[...] for c in dcrefs)
        if bwd_fn is None:
            _, vjp = jax.vjp(f, carry, *[vals[i] for i in wrt])
            grads = vjp((tuple(dvals), dcarry))
            dcarry_in, grads = grads[0], grads[1:]
        else:
            dcarry_in, grads = bwd_fn((o, tt), carry, vals, dvals, dcarry)
        for c, g in zip(dcrefs, dcarry_in):
            c[...] = g
        _store_grads((o, t), rargs, wrt, grads, grefs, {})

    ins = rargs + rdouts + ckargs
    res = pl.pallas_call(
        body, name=name, grid=grid,
        in_specs=[pl.BlockSpec(a.block, a.imap) for a in ins] + [_ANY] * ns_in,
        out_specs=[pl.BlockSpec(g.block, g.imap) for g in gouts] + [_ANY] * ns_out,
        out_shape=[jax.ShapeDtypeStruct(g.shape, g.dtype) for g in gouts] + (list(side.out_shapes) if side else []),
        scratch_shapes=[pltpu.VMEM(cs, f32) for cs in carry_shapes] + (list(side.sem_shapes) if side else []),
        compiler_params=_cparams(2))(*[a.arr for a in ins], *(side.arrs if side else []))
    if side:
        return tuple(res[:len(gouts)]), list(res[len(gouts):])
    return tuple(res)


def _pick(dim, target):
    if dim <= target:
        return dim
    for t in range(target // 128 * 128, 127, -128):
        if dim % t == 0:
            return t
    return dim


def matmul(name, a, b, mode='nn', add=None, out_dtype=f32, tm=None, tn=1152, tk=None):
    if mode == 'tn':
        K, M = a.shape
    else:
        M, K = a.shape
    N = b.shape[0] if mode == 'nt' else b.shape[1]
    assert (b.shape[1] if mode == 'nt' else b.shape[0]) == K
    tm = (1024 if mode == 'tn' else 512) if tm is None else tm
    tk = (512 if mode == 'tn' else 1152) if tk is None else tk
    tm, tn, tk = _pick(M, tm), _pick(N, tn), _pick(K, tk)
    nk = K // tk
    a_spec = pl.BlockSpec((tk, tm), lambda i, j, k: (k, i)) if mode == 'tn' else pl.BlockSpec((tm, tk), lambda i, j, k: (i, k))
    b_spec = pl.BlockSpec((tn, tk), lambda i, j, k: (j, k)) if mode == 'nt' else pl.BlockSpec((tk, tn), lambda i, j, k: (k, j))
    dims = {'nn': (((1,), (0,)), ((), ())), 'nt': (((1,), (1,)), ((), ())), 'tn': (((0,), (0,)), ((), ()))}[mode]
    has_add = add is not None

    def body(*refs):
        if has_add:
            a_ref, b_ref, add_ref, o_ref, acc = refs
        else:
            a_ref, b_ref, o_ref, acc = refs
        k = pl.program_id(2)

        @pl.when(k == 0)
        def _():
            acc[...] = add_ref[...].astype(f32) if has_add else jnp.zeros_like(acc)

        acc[...] += lax.dot_general(a_ref[...].astype(bf16), b_ref[...].astype(bf16), dims, preferred_element_type=f32)

        @pl.when(k == nk - 1)
        def _():
            o_ref[...] = acc[...].astype(o_ref.dtype)

    in_specs = [a_spec, b_spec] + ([pl.BlockSpec((tm, tn), lambda i, j, k: (i, j))] if has_add else [])
    ops = [a, b] + ([add] if has_add else [])
    return pl.pallas_call(
        body, name=name, grid=(M // tm, N // tn, nk), in_specs=in_specs,
        out_specs=pl.BlockSpec((tm, tn), lambda i, j, k: (i, j)),
        out_shape=jax.ShapeDtypeStruct((M, N), out_dtype),
        scratch_shapes=[pltpu.VMEM((tm, tn), f32)],
        compiler_params=pltpu.CompilerParams(dimension_semantics=("parallel", "parallel", "arbitrary"), vmem_limit_bytes=VMEM_LIMIT))(*ops)


def matmul_nt_sum(name, lhs, rhs, tm=1024, tk=768, side=None):
    M, N = lhs[0].shape[0], rhs[0].shape[0]
    tm = _pick(M, tm)
    tks = [_pick(a.shape[1], tk) for a in lhs]
    starts, total = [], 0
    for a, t in zip(lhs, tks):
        starts.append(total)
        total += a.shape[1] // t
    npc = len(lhs)
    ns_in, ns_out = (len(side.arrs), len(side.out_shapes)) if side else (0, 0)

    def body(*refs):
        a_refs, b_refs, sins = refs[:npc], refs[npc:2 * npc], refs[2 * npc:2 * npc + ns_in]
        refs = refs[2 * npc + ns_in:]
        o_ref, souts, acc = refs[0], refs[1:1 + ns_out], refs[1 + ns_out]
        k = pl.program_id(1)
        if side:
            _run_side(side, pl.program_id(0) * total + k, (M // tm) * total, sins, souts, refs[2 + ns_out:])

        @pl.when(k == 0)
        def _():
            acc[...] = jnp.zeros_like(acc)

        for p in range(npc):
            @pl.when((k >= starts[p]) & (k < starts[p] + lhs[p].shape[1] // tks[p]))
            def _(p=p):
                acc[...] += lax.dot_general(a_refs[p][...].astype(bf16), b_refs[p][...].astype(bf16), _NT, preferred_element_type=f32)

        @pl.when(k == total - 1)
        def _():
            o_ref[...] = acc[...]

    def kblock(p):
        return lambda k: jnp.clip(k - starts[p], 0, lhs[p].shape[1] // tks[p] - 1)

    in_specs = [pl.BlockSpec((tm, tks[p]), lambda i, k, kb=kblock(p): (i, kb(k))) for p in range(npc)]
    in_specs += [pl.BlockSpec((N, tks[p]), lambda i, k, kb=kblock(p): (0, kb(k))) for p in range(npc)]
    res = pl.pallas_call(
        body, name=name, grid=(M // tm, total), in_specs=in_specs + [_ANY] * ns_in,
        out_specs=[pl.BlockSpec((tm, N), lambda i, k: (i, 0))] + [_ANY] * ns_out,
        out_shape=[jax.ShapeDtypeStruct((M, N), f32)] + (list(side.out_shapes) if side else []),
        scratch_shapes=[pltpu.VMEM((tm, N), f32)] + (list(side.sem_shapes) if side else []),
        compiler_params=_cparams(2))(*lhs, *rhs, *(side.arrs if side else []))
    return (res[0], list(res[1:])) if side else res[0]


def wgrad(name, act, dout):
    return matmul(name, act, dout, 'tn', out_dtype=bf16)


def _dot(a, b, dims=(((1,), (0,)), ((), ()))):
    return lax.dot_general(a.astype(bf16), b.astype(bf16), dims, preferred_element_type=f32)


_NT = (((1,), (1,)), ((), ()))
_TN = (((0,), (0,)), ((), ()))


def _three_term_dot(v, sel, dims):
    hi = v.astype(bf16)
    rest = v - hi.astype(f32)
    mid = rest.astype(bf16)
    lo = (rest - mid.astype(f32)).astype(bf16)
    dot = lambda t: lax.dot_general(t, sel, dims, preferred_element_type=f32)
    return dot(hi) + dot(mid) + dot(lo)


@jax.custom_vjp
def _dot_exact01(v, sel):
    return _three_term_dot(v, sel, (((1,), (0,)), ((), ())))


def _dot_exact01_fwd(v, sel):
    return _dot_exact01(v, sel), sel


def _dot_exact01_bwd(sel, ct):
    return _three_term_dot(ct, sel, _NT), jnp.zeros_like(sel)


_dot_exact01.defvjp(_dot_exact01_fwd, _dot_exact01_bwd)


def _spread_heads(v, width):
    r = lax.broadcasted_iota(jnp.int32, (HPAD, SSD_HEADS * width), 0)
    c = lax.broadcasted_iota(jnp.int32, (HPAD, SSD_HEADS * width), 1)
    return _dot_exact01(v, (r == c // width).astype(bf16))


def _rmsnorm_tile(pid, x, w):
    return (x * lax.rsqrt(jnp.mean(x * x, axis=-1, keepdims=True) + RMS_EPS) * w,)


def _shift_rows(h, d, fill):
    pad = jnp.full((d, h.shape[1]), fill, f32)
    return jnp.concatenate([pad, h[:-d]], axis=0)


def _s5_prep_tile(pid, a_re, a_im, ls, btr, bti, ctr, cti):
    o = pid[0]
    w = a_re.shape[1]
    r = lax.broadcasted_iota(jnp.int32, (S5_GROUPS, w), 0)
    c = lax.broadcasted_iota(jnp.int32, (S5_GROUPS, w), 1)
    sel = (r == o * (w // S5_STATE) + c // S5_STATE).astype(f32)
    step = jnp.dot(jnp.exp(ls), sel, precision=lax.Precision.HIGHEST, preferred_element_type=f32)
    mag = jnp.exp(a_re * step)
    ang = a_im * step
    lr, li = mag * jnp.cos(ang), mag * jnp.sin(ang)
    nr, ni = lr - 1.0, li
    den = a_re * a_re + a_im * a_im
    fr = (nr * a_re + ni * a_im) / den
    fi = (ni * a_re - nr * a_im) / den
    bbr = fr * btr - fi * bti
    bbi = fr * bti + fi * btr
    reps = w // S5_STATE
    rr = lax.broadcasted_iota(jnp.int32, (reps * S5_GROUP, w), 0)
    cc = lax.broadcasted_iota(jnp.int32, (reps * S5_GROUP, w), 1)
    diag = (rr // S5_GROUP) == (cc // S5_STATE)

    def expand(m):
        return jnp.where(diag, jnp.concatenate([m] * reps, axis=0), 0.0)

    pr, pi = lr, li
    rows_r, rows_i = [pr], [pi]
    for _ in range(S5_ND - 1):
        pr, pi = pr * pr - pi * pi, 2.0 * pr * pi
        rows_r.append(pr)
        rows_i.append(pi)
    lamd_r, lamd_i = jnp.concatenate(rows_r, axis=0), jnp.concatenate(rows_i, axis=0)
    tr = jnp.broadcast_to(lr, (S5_SUB, w))
    ti = jnp.broadcast_to(li, (S5_SUB, w))
    for j in range(S5_ND):
        sr, si = _shift_rows(tr, 1 << j, 1.0), _shift_rows(ti, 1 << j, 0.0)
        tr, ti = tr * sr - ti * si, tr * si + ti * sr
    return lamd_r, lamd_i, tr, ti, expand(bbr), expand(bbi), expand(ctr), expand(cti)


def _s5_tile(pid, carry, u, lamd_r, lamd_i, lam8_r, lam8_i, bbr, bbi, ccr, cci, dvec):
    hr, hi = _s5_scan(_dot(u, bbr), _dot(u, bbi), carry, lamd_r, lamd_i, lam8_r, lam8_i, reverse=False)
    return (_s5_readout(hr, hi, u, ccr, cci, dvec),), (hr[-1:], hi[-1:])


def _s5_readout(hr, hi, u, ccr, cci, dvec):
    return jax.nn.gelu(_dot(hr, ccr, _NT) - _dot(hi, cci, _NT) + dvec * u)


def _s5_scan(xr, xi, carry, lamd_r, lamd_i, lam8_r, lam8_i, reverse):
    cr, ci = carry
    T, G = xr.shape[0], S5_SUB
    sign = -1.0 if reverse else 1.0
    sub = lax.broadcasted_iota(jnp.int32, (T, 1), 0) % G
    for j in range(S5_ND):
        d = 1 << j
        if reverse:
            keep = sub < G - d
            sr = jnp.where(keep, jnp.concatenate([xr[d:], jnp.zeros((d, xr.shape[1]), f32)], axis=0), 0.0)
            si = jnp.where(keep, jnp.concatenate([xi[d:], jnp.zeros((d, xi.shape[1]), f32)], axis=0), 0.0)
        else:
            keep = sub >= d
            sr = jnp.where(keep, _shift_rows(xr, d, 0.0), 0.0)
            si = jnp.where(keep, _shift_rows(xi, d, 0.0), 0.0)
        ar, ai = lamd_r[j:j + 1], sign * lamd_i[j:j + 1]
        xr, xi = xr + ar * sr - ai * si, xi + ar * si + ai * sr
    if reverse:
        pr = jnp.concatenate([lam8_r[G - 1 - s:G - s] for s in range(G)], axis=0)
        pi = -jnp.concatenate([lam8_i[G - 1 - s:G - s] for s in range(G)], axis=0)
    else:
        pr, pi = lam8_r, lam8_i
    n = T // G
    rows_r, rows_i = [None] * n, [None] * n
    for i in (reversed(range(n)) if reverse else range(n)):
        gr_, gi_ = xr[i * G:(i + 1) * G], xi[i * G:(i + 1) * G]
        gr_, gi_ = gr_ + pr * cr - pi * ci, gi_ + pr * ci + pi * cr
        cr, ci = (gr_[:1], gi_[:1]) if reverse else (gr_[G - 1:], gi_[G - 1:])
        rows_r[i], rows_i[i] = gr_, gi_
    return jnp.concatenate(rows_r, axis=0), jnp.concatenate(rows_i, axis=0)


def _s5_tile_bwd(pid, carry, vals, douts, dcarry):
    u, lamd_r, lamd_i, lam8_r, lam8_i, bbr, bbi, ccr, cci, dvec = vals
    (dg,) = douts
    hr, hi = _s5_scan(_dot(u, bbr), _dot(u, bbi), carry, lamd_r, lamd_i, lam8_r, lam8_i, reverse=False)
    _, vjp = jax.vjp(_s5_readout, hr, hi, u, ccr, cci, dvec)
    dhr, dhi, du, dccr, dcci, ddvec = vjp(dg)
    Hr, Hi = _s5_scan(dhr, dhi, dcarry, lamd_r, lamd_i, lam8_r, lam8_i, reverse=True)
    _, vjp_in = jax.vjp(lambda u, bbr, bbi: (_dot(u, bbr), _dot(u, bbi)), u, bbr, bbi)
    du2, dbbr, dbbi = vjp_in((Hr, Hi))
    pr = jnp.concatenate([carry[0], hr[:-1]], axis=0)
    pi = jnp.concatenate([carry[1], hi[:-1]], axis=0)
    dlam_r = jnp.sum(Hr * pr + Hi * pi, axis=0, keepdims=True)
    dlam_i = jnp.sum(Hi * pr - Hr * pi, axis=0, keepdims=True)
    zrow = jnp.zeros((S5_ND - 1, dlam_r.shape[1]), f32)
    dlamd_r, dlamd_i = jnp.concatenate([dlam_r, zrow], axis=0), jnp.concatenate([dlam_i, zrow], axis=0)
    grads = (du + du2, dlamd_r, dlamd_i, jnp.zeros_like(lam8_r), jnp.zeros_like(lam8_i), dbbr, dbbi, dccr, dcci, ddvec)
    return (Hr[:1], Hi[:1]), grads


def _glu_tile(pid, g, glu, za, b):
    return (g * jax.nn.sigmoid(glu + b) * jax.nn.silu(za),)


def _attn_tile(pid, carry, q, k, v, qw, kw):
    n = pid[1]
    kp, vp = carry
    D, B = ATT_HEAD_DIM, ATT_BLOCK
    W = 2 * D
    nq, ncol = q.shape[0] // B, q.shape[1] // W
    r = lax.broadcasted_iota(jnp.int32, (B, 2 * B), 0)
    c = lax.broadcasted_iota(jnp.int32, (B, 2 * B), 1)
    diff = r + B - c
    band = (diff >= 0) & (diff <= B)
    band_first = band & ((c >= B) | (n > 0))
    low = lax.broadcasted_iota(jnp.int32, (1, W), 1) < D
    same_head = (lax.broadcasted_iota(jnp.int32, (W, W), 0) // D == lax.broadcasted_iota(jnp.int32, (W, W), 1) // D).astype(bf16)

    def hnorm(x, w):
        rows = x.shape[0]
        t = jnp.concatenate([x[:, j * W:(j + 1) * W] for j in range(ncol)], axis=0) if ncol > 1 else x
        ms = _dot_exact01(t * t, same_head) * (1.0 / D)
        t = t * lax.rsqrt(ms + RMS_EPS) * jnp.concatenate([w, w], axis=1)
        return [t[j * rows:(j + 1) * rows] for j in range(ncol)]

    qns, kns = hnorm(q, qw), hnorm(k, kw)
    out_cols, lse_cols, kn_cols = [], [], []
    for j in range(ncol):
        sl = slice(j * W, (j + 1) * W)
        qn, kn, vj = qns[j], kns[j], v[:, sl]
        kn_cols.append(kn[(nq - 1) * B:])
        outs, lses = [], []
        for b in range(nq):
            rows = slice(b * B, (b + 1) * B)
            prev = slice((b - 1) * B, b * B)
            kk = jnp.concatenate([kp[:, sl] if b == 0 else kn[prev], kn[rows]], axis=0)
            vv = jnp.concatenate([vp[:, sl] if b == 0 else vj[prev], vj[rows]], axis=0)
            o2, l2 = [], []
            for head_lanes in (low, ~low):
                s = _dot(jnp.where(head_lanes, qn[rows], 0.0), kk, _NT) * (D ** -0.5)
                s = jnp.where(band_first if b == 0 else band, s, -1e30)
                m = jnp.max(s, axis=-1, keepdims=True)
                p = jnp.exp(s - m)
                l = jnp.sum(p, axis=-1, keepdims=True)
                o2.append(_dot(p / l, vv))
                l2.append(m + jnp.log(l))
            outs.append(jnp.where(low, o2[0], o2[1]))
            lses.append(jnp.where(low, l2[0], l2[1]))
        out_cols.append(jnp.concatenate(outs, axis=0) if nq > 1 else outs[0])
        lse_cols.append(jnp.concatenate(lses, axis=0) if nq > 1 else lses[0])
    return ((jnp.concatenate(out_cols, axis=1), jnp.concatenate(lse_cols, axis=1)),
            (jnp.concatenate(kn_cols, axis=1), v[(nq - 1) * B:]))


def _combine_tile(pid, o1, l1, o2, l2, o3, l3, zb):
    m = jnp.maximum(jnp.maximum(l1, l2), l3)
    e1, e2, e3 = jnp.exp(l1 - m), jnp.exp(l2 - m), jnp.exp(l3 - m)
    y = (e1 * o1 + e2 * o2 + e3 * o3) / (e1 + e2 + e3)
    return (y * jax.nn.silu(zb),)


def _softplus(x):
    return jnp.maximum(x, 0.0) + jnp.log(1.0 + jnp.exp(-jnp.abs(x)))


def _ssd_tile(pid, carry, xbc, dt, z, conv_w, conv_b, dt_bias, a_log, dvec, norm_w):
    xprev, state = carry
    T, P, N = SSD_CHUNK, SSD_HEAD_DIM, SSD_STATE
    xx = jnp.concatenate([xprev, xbc], axis=0)
    conv = conv_b
    for k in range(SSD_CONV):
        off = 8 - (SSD_CONV - 1) + k
        conv = conv + conv_w[k:k + 1] * xx[off:off + T]
    xc = jax.nn.silu(conv)
    dtp = _softplus(dt + dt_bias)
    a_dt = dtp * (-jnp.exp(a_log))
    r = lax.broadcasted_iota(jnp.int32, (T, T), 0)
    c = lax.broadcasted_iota(jnp.int32, (T, T), 1)
    tri = r >= c
    trif = tri.astype(f32)
    hi = lax.Precision.HIGHEST
    a_cs = jnp.dot(trif, a_dt, precision=hi, preferred_element_type=f32)
    a_cs_t = lax.dot_general(a_dt, trif, (((0,), (1,)), ((), ())), precision=hi, preferred_element_type=f32)
    xs = xc[:, :SSD_WIDTH]
    acs_p = _spread_heads(a_cs, P)
    acs_t = _spread_heads(a_cs, T)
    xdt = xs * _spread_heads(dtp, P)
    skip = _spread_heads(dvec, P)
    to_end = jnp.exp(acs_p[T - 1:T] - acs_p)
    low = lax.broadcasted_iota(jnp.int32, (1, 2 * P), 1) < P
    low_rows = lax.broadcasted_iota(jnp.int32, (2 * P, 1), 0) < P
    ys, states = [], []
    for j in range(SSD_HEADS // 2):
        g = 2 * j // (SSD_HEADS // SSD_GROUPS)
        if 2 * j % (SSD_HEADS // SSD_GROUPS) == 0:
            bg = xc[:, SSD_WIDTH + g * N:SSD_WIDTH + (g + 1) * N]
            cg = xc[:, SSD_WIDTH + SSD_GROUPS * N + g * N:SSD_WIDTH + SSD_GROUPS * N + (g + 1) * N]
            cb = _dot(cg, bg, _NT)
        lanes = slice(2 * j * P, 2 * (j + 1) * P)
        st = state[lanes, :]
        diag, last = [], []
        for h in (2 * j, 2 * j + 1):
            decay = jnp.exp(jnp.where(tri, acs_t[:, h * T:(h + 1) * T] - a_cs_t[h:h + 1, :], -1e30))
            diag.append(_dot(cb * decay, xdt[:, lanes]))
            last.append(jnp.exp(a_cs_t[h:h + 1, T - 1:T]))
        y = (jnp.where(low, diag[0], diag[1]) + _dot(cg, st, _NT) * jnp.exp(acs_p[:, lanes])
             + xs[:, lanes] * skip[:, lanes])
        ys.append(y)
        states.append(jnp.where(low_rows, last[0], last[1]) * st + _dot(xdt[:, lanes] * to_end[:, lanes], bg, _TN))
    y = jnp.concatenate(ys, axis=1) * jax.nn.silu(z)
    out = y * lax.rsqrt(jnp.mean(y * y, axis=-1, keepdims=True) + RMS_EPS) * norm_w
    return (out,), (xbc[T - 8:], jnp.concatenate(states, axis=0))


def _merge_tile(pid, pa, pb, pc, g0, g1, g2):
    return (jax.nn.sigmoid(g0) * pa + jax.nn.sigmoid(g1) * pb + jax.nn.sigmoid(g2) * pc,)


def loss_and_grad(y, target, tm=512):
    S, D = y.shape
    nt = S // tm

    def body(y_ref, t_ref, dy_ref, l_ref, acc):
        i = pl.program_id(0)

        @pl.when(i == 0)
        def _():
            acc[...] = jnp.zeros_like(acc)

        diff = y_ref[...] - t_ref[...]
        dy_ref[...] = diff * (1.0 / D)
        acc[...] += jnp.sum((diff * diff).reshape(tm // 8, 8, D), axis=0)

        @pl.when(i == nt - 1)
        def _():
            l_ref[...] = jnp.broadcast_to(0.5 / D * jnp.sum(acc[...]), l_ref.shape)

    dy, l = pl.pallas_call(
        body, name="loss_head", grid=(nt,),
        in_specs=[pl.BlockSpec((tm, D), lambda i: (i, 0))] * 2,
        out_specs=[pl.BlockSpec((tm, D), lambda i: (i, 0)), pl.BlockSpec((8, 128), lambda i: (0, 0))],
        out_shape=[jax.ShapeDtypeStruct((S, D), f32), jax.ShapeDtypeStruct((8, 128), f32)],
        scratch_shapes=[pltpu.VMEM((8, D), f32)],
        compiler_params=_cparams(1))(y, target)
    return dy, l[0, 0]


def _row_tile(R, C, budget=1 << 20):
    best = R
    for t in range(8, R, 8):
        if R % t == 0 and t * C * 4 <= budget:
            best = t
    if best == R and R * C * 4 > budget:
        for t in range(8, R, 8):
            if R % t == 0:
                return t
    return best


def _as2d(t, lead=0):
    return t.reshape(t.shape[:lead] + (math.prod(t.shape[lead:-1]), t.shape[-1]))


def adamw(name, w, gslots, m, v):
    shape = w.shape
    n = gslots.shape[0]
    C = shape[-1]
    R = math.prod(shape[:-1])
    lanes = -(-C // 128) * 128
    tr = _row_tile(R, lanes * (n + 7), budget=10 << 20)

    def body(w_ref, g_ref, m_ref, v_ref, go_ref, d_ref, nm_ref, nv_ref):
        gg = g_ref[0].astype(f32)
        for s in range(1, n):
            gg = gg + g_ref[s].astype(f32)
        go_ref[...] = gg
        nm = ADAM_B1 * m_ref[...] + (1.0 - ADAM_B1) * gg
        nv = ADAM_B2 * v_ref[...] + (1.0 - ADAM_B2) * jnp.square(gg)
        m_hat = nm / (1.0 - ADAM_B1 ** ADAM_STEP)
        v_hat = nv / (1.0 - ADAM_B2 ** ADAM_STEP)
        d_ref[...] = -ADAM_LR * (m_hat / (jnp.sqrt(v_hat) + ADAM_EPS) + ADAM_WD * w_ref[...])
        nm_ref[...] = nm
        nv_ref[...] = nv

    spec = pl.BlockSpec((tr, C), lambda i: (i, 0))
    res = pl.pallas_call(
        body, name=name, grid=(R // tr,),
        in_specs=[spec, pl.BlockSpec((n, tr, C), lambda i: (0, i, 0)), spec, spec], out_specs=[spec] * 4,
        out_shape=[jax.ShapeDtypeStruct((R, C), f32)] * 4,
        compiler_params=_cparams(1))(w.reshape(R, C), gslots.reshape(n, R, C), m.reshape(R, C), v.reshape(R, C))
    return tuple(t.reshape(shape) for t in res)


PACK_ROWS = 256


def sum_slots(name, x):
    n, R, C = x.shape

    def body(x_ref, o_ref):
        acc = x_ref[0]
        for s in range(1, n):
            acc = acc + x_ref[s]
        o_ref[...] = acc

    return pl.pallas_call(
        body, name=name, grid=(R // PACK_ROWS,),
        in_specs=[pl.BlockSpec((n, PACK_ROWS, C), lambda i: (0, i, 0))],
        out_specs=pl.BlockSpec((PACK_ROWS, C), lambda i: (i, 0)),
        out_shape=jax.ShapeDtypeStruct((R, C), f32), compiler_params=_cparams(1))(x)


def _pack(parts):
    flat = jnp.concatenate([p.reshape(-1) for p in parts])
    unit = 128 * PACK_ROWS
    tot = -(-flat.shape[0] // unit) * unit
    return jnp.pad(flat, (0, tot - flat.shape[0])).reshape(tot // 128, 128)


def _unpack(buf, shapes):
    flat = buf.reshape(-1)
    out, off = [], 0
    for s in shapes:
        size = math.prod(s)
        out.append(flat[off:off + size].reshape(s))
        off += size
    return out


def _comm_sems(nt):
    return [pltpu.SemaphoreType.DMA((nt, N_DEV - 1)), pltpu.SemaphoreType.DMA((nt, N_DEV - 1)), pltpu.SemaphoreType.DMA((nt,))]


def exchange_side(srcs, modes):
    nt = len(srcs)
    slabs = []
    for s, mode in zip(srcs, modes):
        R, C = s.shape
        slabs.append({'all': (R, C), 'rows': (R // N_DEV, C), 'cols': (R, C // N_DEV)}[mode])

    def piece(ref, mode, slab, p):
        if mode == 'all':
            return ref
        if mode == 'rows':
            return ref.at[pl.ds(p * slab[0], slab[0]), :]
        return ref.at[:, pl.ds(p * slab[1], slab[1])]

    def copies(src_refs, out_refs, send_sems, recv_sems, local_sems):
        x, y, c = lax.axis_index("x"), lax.axis_index("y"), lax.axis_index("c")
        me = 4 * x + 2 * y + c
        out = []
        for k in (1, 2, 4, 3, 5, 6, 7):
            px = 1 - x if k & 4 else x
            py = 1 - y if k & 2 else y
            pc = 1 - c if k & 1 else c
            for t in range(nt):
                out.append(pltpu.make_async_remote_copy(
                    src_ref=piece(src_refs[t], modes[t], slabs[t], 4 * px + 2 * py + pc), dst_ref=out_refs[t].at[me],
                    send_sem=send_sems.at[t, k - 1], recv_sem=recv_sems.at[t, k - 1],
                    device_id=(px, py, pc), device_id_type=pl.DeviceIdType.MESH))
        for t in range(nt):
            out.append(pltpu.make_async_copy(piece(src_refs[t], modes[t], slabs[t], me), out_refs[t].at[me], local_sems.at[t]))
        return out

    def start(*refs):
        for cp in copies(*refs):
            cp.start()

    def finish(*refs):
        for cp in copies(*refs):
            cp.wait()

    out_shapes = [jax.ShapeDtypeStruct((N_DEV,) + sl, s.dtype) for s, sl in zip(srcs, slabs)]
    return Side(list(srcs), out_shapes, _comm_sems(nt), [(0.0, start), (1.0, finish)])


def gather_side(srcs):
    nt = len(srcs)

    def plan(src_refs, out_refs, send_sems, recv_sems, local_sems):
        x, y, c = lax.axis_index("x"), lax.axis_index("y"), lax.axis_index("c")
        me, sibling = (x, y, c), (x, y, 1 - c)
        chips = [(1 - x, y), (x, 1 - y), (1 - x, 1 - y)]

        def slot(t, dev):
            return out_refs[t].at[4 * dev[0] + 2 * dev[1] + dev[2]]

        def copy(t, k, block, to, src=None):
            return pltpu.make_async_remote_copy(
                src_ref=slot(t, block) if src is None else src, dst_ref=slot(t, block),
                send_sem=send_sems.at[t, k], recv_sem=recv_sems.at[t, k], device_id=to, device_id_type=pl.DeviceIdType.MESH)

        mine = [pltpu.make_async_copy(src_refs[t], slot(t, me), local_sems.at[t]) for t in range(nt)]
        first = []
        for t in range(nt):
            first.append(copy(t, 0, me, sibling, src=src_refs[t]))
            first += [copy(t, 1 + j, me, (*chip, c), src=src_refs[t]) for j, chip in enumerate(chips)]
        landed = [copy(t, 1 + j, (*chip, c), me) for j, chip in enumerate(chips) for t in range(nt)]
        passed = [copy(t, 4 + j, (*chip, c), sibling) for j, chip in enumerate(chips) for t in range(nt)]
        from_sibling = [copy(t, 0, sibling, me) for t in range(nt)]
        from_sibling += [copy(t, 4 + j, (*chip, 1 - c), me) for t in range(nt) for j, chip in enumerate(chips)]
        return mine, first, landed, passed, from_sibling

    def start(*refs):
        mine, first, _, _, _ = plan(*refs)
        for cp in mine + first:
            cp.start()

    def forward(*refs):
        _, _, landed, passed, _ = plan(*refs)
        for got, fwd in zip(landed, passed):
            got.wait_recv()
            fwd.start()

    def finish(*refs):
        mine, first, _, passed, from_sibling = plan(*refs)
        for cp in from_sibling:
            cp.wait_recv()
        for cp in first + passed:
            cp.wait_send()
        for cp in mine:
            cp.wait()

    out_shapes = [jax.ShapeDtypeStruct((N_DEV,) + s.shape, s.dtype) for s in srcs]
    return Side(list(srcs), out_shapes, _comm_sems(nt), [(0.0, start), (0.5, forward), (1.0, finish)])


def run_side(name, side):
    ns = len(side.arrs)

    def body(*refs):
        for _, phase in side.phases:
            phase(refs[:ns], refs[ns:ns + len(side.out_shapes)], *refs[ns + len(side.out_shapes):])

    return list(pl.pallas_call(
        body, name=name, in_specs=[_ANY] * ns, out_specs=[_ANY] * len(side.out_shapes),
        out_shape=list(side.out_shapes), scratch_shapes=list(side.sem_shapes))(*side.arrs))


def _relayout_w_in(w):
    offs = [0]
    for s in IN_SPLITS:
        offs.append(offs[-1] + s)
    p = [w[:, offs[i]:offs[i + 1]] for i in range(len(IN_SPLITS))]
    ua, za, q, k, v, zb, xbc, dt, zc, gates = p
    dtp = jnp.pad(dt, ((0, 0), (0, HPAD - dt.shape[1])))
    return (jnp.concatenate([ua, za, dtp], 1), w[:, offs[2]:offs[5]], jnp.concatenate([xbc, zb, zc], 1), gates)


def _pad_lanes(v, n=HPAD):
    return jnp.pad(v.reshape(1, -1), ((0, 0), (0, n - v.shape[-1])))


def _s5_prep_args(W):
    g2 = S5_GROUPS * S5_STATE
    w = g2 // S5_CHUNKS
    a_re, a_im = W['s5_a_re'].reshape(1, g2), W['s5_a_im'].reshape(1, g2)
    ls = W['s5_log_step'].reshape(1, S5_GROUPS)
    btr, bti = W['s5_b_re'].reshape(g2, S5_GROUP).T, W['s5_b_im'].reshape(g2, S5_GROUP).T
    ctr = W['s5_c_re'].transpose(1, 0, 2).reshape(S5_GROUP, g2)
    cti = W['s5_c_im'].transpose(1, 0, 2).reshape(S5_GROUP, g2)
    col = lambda a, rows: Arg(a, (rows, w), lambda o: (0, o), 'tile')
    return [col(a_re, 1), col(a_im, 1), _whole(ls, 'acc'), col(btr, S5_GROUP), col(bti, S5_GROUP), col(ctr, S5_GROUP), col(cti, S5_GROUP)]


def _s5_prep_outs():
    g2 = S5_GROUPS * S5_STATE
    w = g2 // S5_CHUNKS
    rows = (S5_ND, S5_ND, S5_SUB, S5_SUB, 128, 128, 128, 128)
    return [Out((r, g2), f32, (r, w), lambda o: (0, o)) for r in rows]


def _s5_args(A, prep, dvec, S):
    w = S5_GROUPS * S5_STATE // S5_CHUNKS
    args = [Arg(A, (S5_TILE, 128), lambda o, t: (t, o), 'tile', (S, S5_WIDTH), None, bf16)]
    for p in prep:
        args.append(Arg(p, (p.shape[0], w), lambda o, t: (0, o), 'acc0'))
    args.append(Arg(dvec, (1, 128), lambda o, t: (0, o), 'acc0'))
    return args


def _attn_args(QKV, g, r, qw, kw, S):
    L = S // r
    nq, rb = _attn_plan(r)
    block = (nq * ATT_BLOCK, rb * ATT_GW)
    gshape = (L, r * ATT_GW)
    gimap = lambda rho, n: (n, rho)
    if r == 1:
        mk = lambda j: Arg(QKV, block, lambda rho, n, j=j: (n, j), 'tile', gshape, gimap, bf16)
    else:
        def mk(j):
            view = QKV[:, j * ATT_GW:(j + 1) * ATT_GW].reshape(L, r * ATT_GW)
            return Arg(view, block, gimap, 'tile', None, None, bf16)
    return [mk(g), mk(3 + g), mk(6 + g), _whole(qw, 'acc'), _whole(kw, 'acc')]


def _attn_plan(r):
    return (4, 1) if r == 1 else (1, min(r, 4))


def _attn_grid(r, S):
    nq, rb = _attn_plan(r)
    return (r // rb, S // r // ATT_BLOCK // nq)


def _attn_carry(r):
    return ((ATT_BLOCK, _attn_plan(r)[1] * ATT_GW),) * 2


def _ssd_args(C, A, W, S):
    T = SSD_CHUNK
    return [Arg(C, (T, SSD_CONV_DIM), lambda o, t: (t, 0), 'tile', (S, SSD_CONV_DIM), None, bf16),
            Arg(A, (T, HPAD), lambda o, t: (t, 2 * S5_WIDTH // HPAD), 'tile', (S, HPAD), lambda o, t: (t, 0), bf16),
            Arg(C, (T, SSD_WIDTH), lambda o, t: (t, 2), 'tile', (S, SSD_WIDTH), lambda o, t: (t, 0), bf16),
            _whole(W['conv_w'], 'acc'), _whole(W['conv_b'].reshape(1, -1), 'acc'),
            _whole(_pad_lanes(W['dt_bias']), 'acc'), _whole(_pad_lanes(W['ssd_a_log']), 'acc'),
            _whole(_pad_lanes(W['ssd_d']), 'acc'), _whole(W['ssd_norm_w'].reshape(1, -1), 'acc')]


_SSD_CARRY = ((8, SSD_CONV_DIM), (SSD_WIDTH, SSD_STATE))
_S5_CARRY = ((1, 512), (1, 512))


def layer_fwd(li, x, W, side=None):
    S = x.shape[0]
    n = lambda s: f"l{li}_{s}"
    sv = {'x': x}
    (h,) = map_fwd(n("norm"), _rmsnorm_tile, (S // 512,), [_rows(x, 512), _whole(W['norm_w'].reshape(1, -1))],
                   [Out((S, D_MODEL), bf16, (512, D_MODEL), lambda i: (i, 0))])
    wA, wQ, wC, wG = W['w_in_pieces']
    A = matmul(n("in_a"), h, wA)
    QKV = matmul(n("in_qkv"), h, wQ)
    C = matmul(n("in_c"), h, wC)
    G = matmul(n("in_g"), h, wG)
    sv.update(h=h, A=A, QKV=QKV, C=C, G=G)

    prep = map_fwd(n("s5_prep"), _s5_prep_tile, (S5_CHUNKS,), _s5_prep_args(W), _s5_prep_outs())
    dvec = W['s5_d'].reshape(1, -1)
    (g,), s5_ck, *got = scan_fwd(n("s5_scan"), _s5_tile, (S5_CHUNKS, S // S5_TILE), _S5_CARRY, _s5_args(A, prep, dvec, S),
                                 [Out((S, S5_WIDTH), f32, (S5_TILE, 128), lambda o, t: (t, o))], side=side)
    glu = matmul(n("glu"), g, W['s5_glu_w'])
    glu_b = W['s5_glu_b'].reshape(1, -1)
    (ya,) = map_fwd(n("glu_gate"), _glu_tile, (S // 512,),
                    [_rows(g, 512), _rows(glu, 512), _rows(A, 512, col=1, width=S5_WIDTH), _whole(glu_b)],
                    [Out((S, S5_WIDTH), bf16, (512, S5_WIDTH), lambda i: (i, 0))])
    sv.update(prep=prep, g=g, glu=glu, ya=ya, s5_ck=s5_ck)

    qw, kw = W['q_norm_w'].reshape(1, -1), W['k_norm_w'].reshape(1, -1)
    att, att_ck = [], []
    for gi, (window, r) in enumerate(ATT_PAIRS):
        assert window // r == ATT_BLOCK and S % (r * ATT_BLOCK) == 0
        L = S // r
        nq, rb = _attn_plan(r)
        assert S // r // ATT_BLOCK % nq == 0
        spec = Out((L, r * ATT_GW), f32, (nq * ATT_BLOCK, rb * ATT_GW), lambda rho, nb: (nb, rho))
        (o, lse), ck = scan_fwd(n(f"attn{gi}"), _attn_tile, _attn_grid(r, S), _attn_carry(r), _attn_args(QKV, gi, r, qw, kw, S), [spec, spec])
        att += [o.reshape(S, ATT_GW), lse.reshape(S, ATT_GW)]
        att_ck.append(ck)
    (yb,) = map_fwd(n("combine"), _combine_tile, (S // 512,),
                    [_rows(t, 512) for t in att] + [_rows(C, 512, col=SSD_CONV_DIM // ATT_GW, width=ATT_GW)],
                    [Out((S, ATT_GW), bf16, (512, ATT_GW), lambda i: (i, 0))])
    sv.update(att=att, att_ck=att_ck, yb=yb)

    (yc,), ssd_ck = scan_fwd(n("ssd"), _ssd_tile, (1, S // SSD_CHUNK), _SSD_CARRY, _ssd_args(C, A, W, S),
                             [Out((S, SSD_WIDTH), bf16, (SSD_CHUNK, SSD_WIDTH), lambda o, t: (t, 0))])
    sv.update(yc=yc, ssd_ck=ssd_ck)

    pa = matmul(n("proj_a"), ya, W['proj_a'])
    pb = matmul(n("proj_b"), yb, W['proj_b'])
    pc = matmul(n("proj_c"), yc, W['proj_c'])
    (merged,) = map_fwd(n("merge"), _merge_tile, (S // 256,),
                        [_rows(pa, 256), _rows(pb, 256), _rows(pc, 256)] + [_rows(G, 256, col=j, width=D_MODEL) for j in range(3)],
                        [Out((S, D_MODEL), bf16, (256, D_MODEL), lambda i: (i, 0))])
    out = matmul(n("w_out"), merged, W['w_out'], add=x)
    sv.update(pa=pa, pb=pb, pc=pc, merged=merged)
    return out, sv, (got[0] if got else None)


def layer_bwd(li, dout, sv, W, side=None, own_scatter=None):
    S = dout.shape[0]
    n = lambda s: f"l{li}_{s}"
    gr = {}
    x, A, QKV, C, G = sv['x'], sv['A'], sv['QKV'], sv['C'], sv['G']

    dmerged = matmul(n("d_merged"), dout, W['w_out'], 'nt')
    gr['w_out'] = wgrad(n("g_w_out"), sv['merged'], dout)
    margs = [_rows(sv['pa'], 256, gdtype=bf16), _rows(sv['pb'], 256, gdtype=bf16), _rows(sv['pc'], 256, gdtype=bf16)] + \
            [_rows(G, 256, col=j, width=D_MODEL, gshape=(S, D_MODEL), gdtype=bf16) for j in range(3)]
    dpa, dpb, dpc, dg0, dg1, dg2 = map_bwd(n("merge_bwd"), _merge_tile, (S // 256,), margs, [_rows(dmerged, 256)], list(range(6)))
    dya = matmul(n("d_ya"), dpa, W['proj_a'], 'nt')
    dyb = matmul(n("d_yb"), dpb, W['proj_b'], 'nt')
    dyc = matmul(n("d_yc"), dpc, W['proj_c'], 'nt')
    gr['proj_a'] = wgrad(n("g_proj_a"), sv['ya'], dpa)
    gr['proj_b'] = wgrad(n("g_proj_b"), sv['yb'], dpb)
    gr['proj_c'] = wgrad(n("g_proj_c"), sv['yc'], dpc)

    glu_b = W['s5_glu_b'].reshape(1, -1)
    gargs = [_rows(sv['g'], 512), _rows(sv['glu'], 512, gdtype=bf16),
             _rows(A, 512, col=1, width=S5_WIDTH, gshape=(S, S5_WIDTH), gdtype=bf16), _whole(glu_b, 'acc')]
    dg_a, dglu, dza, dglu_b = map_bwd(n("glu_gate_bwd"), _glu_tile, (S // 512,), gargs, [_rows(dya, 512)], [0, 1, 2, 3])
    gr['s5_glu_b'] = dglu_b.reshape(-1)
    dg = matmul(n("d_g"), dglu, W['s5_glu_w'], 'nt', add=dg_a)
    gr['s5_glu_w'] = wgrad(n("g_glu_w"), sv['g'], dglu)
    dvec = W['s5_d'].reshape(1, -1)
    sargs = _s5_args(A, sv['prep'], dvec, S)
    res = scan_bwd(n("s5_scan_bwd"), _s5_tile, (S5_CHUNKS, S // S5_TILE), _S5_CARRY, sargs, sv['s5_ck'],
                   [Arg(dg, (S5_TILE, 128), lambda o, t: (t, o))], list(range(len(sargs))), bwd_fn=_s5_tile_bwd)
    dua, dprep, dd = res[0], res[1:9], res[9]
    gr['s5_d'] = dd.reshape(-1)
    pargs = _s5_prep_args(W)
    pouts = _s5_prep_outs()
    da_re, da_im, dls, dbtr, dbti, dctr, dcti = map_bwd(
        n("s5_prep_bwd"), _s5_prep_tile, (S5_CHUNKS,), pargs,
        [Arg(d, o.block, o.imap) for d, o in zip(dprep, pouts)], list(range(7)))
    gshape = (S5_GROUPS, S5_STATE)
    gr['s5_a_re'], gr['s5_a_im'] = da_re.reshape(gshape), da_im.reshape(gshape)
    gr['s5_log_step'] = dls.reshape(-1)
    gr['s5_b_re'] = dbtr.T.reshape(S5_GROUPS, S5_STATE, S5_GROUP)
    gr['s5_b_im'] = dbti.T.reshape(S5_GROUPS, S5_STATE, S5_GROUP)
    gr['s5_c_re'] = dctr.reshape(S5_GROUP, S5_GROUPS, S5_STATE).transpose(1, 0, 2)
    gr['s5_c_im'] = dcti.reshape(S5_GROUP, S5_GROUPS, S5_STATE).transpose(1, 0, 2)

    cargs = [_rows(t, 512) for t in sv['att']] + \
            [_rows(C, 512, col=SSD_CONV_DIM // ATT_GW, width=ATT_GW, gshape=(S, ATT_GW), gdtype=bf16)]
    cres = map_bwd(n("combine_bwd"), _combine_tile, (S // 512,), cargs, [_rows(dyb, 512)], list(range(7)))
    dzb = cres[6]
    qw, kw = W['q_norm_w'].reshape(1, -1), W['k_norm_w'].reshape(1, -1)
    dqs, dks, dvs = [], [], []
    dqw = dkw = None
    for gi, (window, r) in enumerate(ATT_PAIRS):
        L = S // r
        nq, rb = _attn_plan(r)
        dspec = lambda t: Arg(t.reshape(L, r * ATT_GW), (nq * ATT_BLOCK, rb * ATT_GW), lambda rho, nb: (nb, rho))
        dq, dk, dv, dqw_g, dkw_g = scan_bwd(n(f"attn{gi}_bwd"), _attn_tile, _attn_grid(r, S), _attn_carry(r),
                                            _attn_args(QKV, gi, r, qw, kw, S), sv['att_ck'][gi],
                                            [dspec(cres[2 * gi]), dspec(cres[2 * gi + 1])], [0, 1, 2, 3, 4])
        dqs.append(dq.reshape(S, ATT_GW))
        dks.append(dk.reshape(S, ATT_GW))
        dvs.append(dv.reshape(S, ATT_GW))
        dqw = dqw_g if dqw is None else dqw + dqw_g
        dkw = dkw_g if dkw is None else dkw + dkw_g
    gr['q_norm_w'], gr['k_norm_w'] = dqw.reshape(-1), dkw.reshape(-1)

    ssd_args = _ssd_args(C, A, W, S)
    sres = scan_bwd(n("ssd_bwd"), _ssd_tile, (1, S // SSD_CHUNK), _SSD_CARRY, ssd_args, sv['ssd_ck'],
                    [Arg(dyc, (SSD_CHUNK, SSD_WIDTH), lambda o, t: (t, 0))], list(range(9)), side=side)
    sres, got = sres if side else (sres, None)
    dxbc, ddt, dzc = sres[0], sres[1], sres[2]
    gr['conv_w'] = sres[3]
    gr['conv_b'] = sres[4].reshape(-1)
    gr['dt_bias'] = sres[5].reshape(-1)[:SSD_HEADS]
    gr['ssd_a_log'] = sres[6].reshape(-1)[:SSD_HEADS]
    gr['ssd_d'] = sres[7].reshape(-1)[:SSD_HEADS]
    gr['ssd_norm_w'] = sres[8].reshape(-1)

    dpieces = [jnp.concatenate([dua, dza, ddt], axis=1), jnp.concatenate(dqs + dks + dvs, axis=1),
               jnp.concatenate([dxbc, dzb, dzc], axis=1), jnp.concatenate([dg0, dg1, dg2], axis=1)]
    gr['w_in'] = _unrelayout_w_in_grad([wgrad(n(f"g_w_in{j}"), sv['h'], dp) for j, dp in enumerate(dpieces)])
    dh = matmul_nt_sum(n("d_h"), dpieces, list(W['w_in_pieces']), side=own_scatter(gr) if own_scatter else None)
    dh, got_own = dh if own_scatter else (dh, None)
    nargs = [_rows(x, 512), _whole(W['norm_w'].reshape(1, -1), 'acc')]
    dx, dnw = map_bwd(n("norm_bwd"), _rmsnorm_tile, (S // 512,), nargs, [_rows(dh, 512)], [0, 1], add={0: _rows(dout, 512)})
    gr['norm_w'] = dnw.reshape(-1)
    return dx, gr, got, got_own


def _unrelayout_w_in_grad(pieces):
    gA, gQ, gC, gG = pieces
    uaza, dt = gA[:, :2 * S5_WIDTH], gA[:, 2 * S5_WIDTH:2 * S5_WIDTH + SSD_HEADS]
    xbc, zb, zc = gC[:, :SSD_CONV_DIM], gC[:, SSD_CONV_DIM:SSD_CONV_DIM + ATT_GW], gC[:, SSD_CONV_DIM + ATT_GW:]
    return jnp.concatenate([uaza, gQ, zb, xbc, dt, zc, gG], axis=1)


def kernel(x, norm_w, w_in, s5_a_re, s5_a_im, s5_log_step, s5_b_re, s5_b_im, s5_c_re, s5_c_im, s5_d, s5_glu_w, s5_glu_b, q_norm_w, k_norm_w, conv_w, conv_b, dt_bias, ssd_a_log, ssd_d, ssd_norm_w, proj_a, proj_b, proj_c, w_out, loss_target, m_norm_w, m_w_in, m_s5_a_re, m_s5_a_im, m_s5_log_step, m_s5_b_re, m_s5_b_im, m_s5_c_re, m_s5_c_im, m_s5_d, m_s5_glu_w, m_s5_glu_b, m_q_norm_w, m_k_norm_w, m_conv_w, m_conv_b, m_dt_bias, m_ssd_a_log, m_ssd_d, m_ssd_norm_w, m_proj_a, m_proj_b, m_proj_c, m_w_out, v_norm_w, v_w_in, v_s5_a_re, v_s5_a_im, v_s5_log_step, v_s5_b_re, v_s5_b_im, v_s5_c_re, v_s5_c_im, v_s5_d, v_s5_glu_w, v_s5_glu_b, v_q_norm_w, v_k_norm_w, v_conv_w, v_conv_b, v_dt_bias, v_ssd_a_log, v_ssd_d, v_ssd_norm_w, v_proj_a, v_proj_b, v_proj_c, v_w_out):
    args = dict(locals())
    w = {k: args[k] for k in WEIGHTS}
    m = {k: args['m_' + k] for k in WEIGHTS}
    v = {k: args['v_' + k] for k in WEIGHTS}
    depth = norm_w.shape[0]
    S = x.shape[1]
    xs = x.reshape(S, D_MODEL)
    tgt = loss_target.reshape(S, D_MODEL)

    def weight_gather(li):
        return gather_side([w[k][li].astype(bf16) for k in SHARDED])

    def assemble(li, gathered):
        W = {k: w[k][li] for k in WEIGHTS if k not in SHARDED}
        for k, t in zip(SHARDED, gathered):
            n_dev, R, C = t.shape
            W[k] = t.reshape(n_dev * R, C) if k in ROW_SHARDED else t.transpose(1, 0, 2).reshape(R, n_dev * C)
        W['w_in_pieces'] = _relayout_w_in(W['w_in'])
        return W

    layers = [assemble(0, run_side("gather_weights0", weight_gather(0)))]
    act, saved = xs, []
    for li in range(depth):
        act, sv, got = layer_fwd(li, act, layers[li], weight_gather(li + 1) if li + 1 < depth else None)
        saved.append(sv)
        if got is not None:
            layers.append(assemble(li + 1, got))
    dy, loss_local = loss_and_grad(act, tgt)
    loss = lax.psum(loss_local, ("x", "y", "c"))

    big = [k for k in SHARDED if k != 'conv_w']

    def grad_scatter(gr):
        return exchange_side([gr[k] for k in big], ['rows' if k in ROW_SHARDED else 'cols' for k in big])

    grads, slots = [None] * depth, [None] * depth
    for li in reversed(range(depth)):
        dy, grads[li], got, got_own = layer_bwd(li, dy, saved[li], layers[li], grad_scatter(grads[li + 1]) if li + 1 < depth else None,
                                                grad_scatter if li == 0 else None)
        if got is not None:
            slots[li + 1] = got
        if got_own is not None:
            slots[li] = got_own
    small_keys = [k for k in WEIGHTS if k not in SHARDED] + ['conv_w']
    stacked = [jnp.stack([grads[li][k] for li in range(depth)], axis=0) for k in small_keys]
    (small_slots,) = run_side("gather_small_grads", exchange_side([_pack(stacked)], ['all']))
    grad_x = dy.reshape(x.shape)
    per_layer = {}
    for li in range(depth):
        for k, s in zip(big, slots[li]):
            per_layer[li, k] = adamw(f"adamw_{k}{li}", w[k][li], s, m[k][li], v[k][li])
    result = {k: tuple(jnp.stack([per_layer[li, k][j] for li in range(depth)], axis=0) for j in range(4)) for k in big}

    totals = _unpack(sum_slots("sum_small_grads", small_slots), [t.shape for t in stacked])
    for k, g in zip(small_keys, totals):
        if k == 'conv_w':
            width = w[k].shape[-1]
            me = 4 * lax.axis_index("x") + 2 * lax.axis_index("y") + lax.axis_index("c")
            g = lax.dynamic_slice_in_dim(g, me * width, width, axis=2)
        result[k] = adamw("adamw_" + k, w[k], g[None], m[k], v[k])

    return (loss, grad_x, *[result[k][0] for k in WEIGHTS], *[result[k][1] for k in WEIGHTS],
            *[result[k][2] for k in WEIGHTS], *[result[k][3] for k in WEIGHTS])
```

```python
import functools
import math
from typing import Any, NamedTuple

import jax
import jax.numpy as jnp
from jax import lax
from jax.experimental import pallas as pl
from jax.experimental.pallas import tpu as pltpu

f32 = jnp.float32
bf16 = jnp.bfloat16

N_DEV = 8
D_MODEL = 1024
RMS_EPS = 1e-6
S5_WIDTH = 512
S5_GROUPS = 32
S5_GROUP = 16
S5_STATE = 64
S5_TILE = 512
S5_SUB = 8
S5_ND = 3
S5_CHUNKS = 4
ATT_HEAD_DIM = 64
ATT_PAIRS = ((128, 1), (512, 4), (2048, 16))
ATT_HPG = 4
ATT_BLOCK = 128
ATT_GW = ATT_HPG * ATT_HEAD_DIM
ATT_WIDTH = 768
SSD_HEADS = 12
SSD_HEAD_DIM = 64
SSD_WIDTH = 768
SSD_STATE = 128
SSD_GROUPS = 2
SSD_CHUNK = 128
SSD_CONV = 4
SSD_CONV_DIM = 1280
HPAD = 128
IN_SPLITS = (512, 512, 768, 768, 768, 256, 1280, 12, 768, 3072)
ADAM_LR, ADAM_B1, ADAM_B2, ADAM_EPS, ADAM_WD, ADAM_STEP = 0.001, 0.9, 0.999, 1e-08, 0.01, 10
VMEM_LIMIT = 56 * 1024 * 1024

WEIGHTS = ['norm_w', 'w_in', 's5_a_re', 's5_a_im', 's5_log_step', 's5_b_re', 's5_b_im', 's5_c_re',
           's5_c_im', 's5_d', 's5_glu_w', 's5_glu_b', 'q_norm_w', 'k_norm_w', 'conv_w', 'conv_b',
           'dt_bias', 'ssd_a_log', 'ssd_d', 'ssd_norm_w', 'proj_a', 'proj_b', 'proj_c', 'w_out']
ROW_SHARDED = ('w_in', 's5_glu_w', 'w_out')
SHARDED = ROW_SHARDED + ('conv_w', 'proj_a', 'proj_b', 'proj_c')


class Arg(NamedTuple):
    arr: Any
    block: tuple
    imap: Any
    kind: str = 'const'
    gshape: Any = None
    gimap: Any = None
    gdtype: Any = None


class Out(NamedTuple):
    shape: tuple
    dtype: Any
    block: tuple
    imap: Any


def _cparams(n):
    return pltpu.CompilerParams(dimension_semantics=("arbitrary",) * n, vmem_limit_bytes=VMEM_LIMIT)


def _rows(a, tm, kind='tile', col=0, width=None, gshape=None, gcol=None, gdtype=None):
    width = a.shape[1] if width is None else width
    g = None if gshape is None else (lambda i, gc=(0 if gcol is None else gcol): (i, gc))
    return Arg(a, (tm, width), lambda i, c=col: (i, c), kind, gshape, g, gdtype)


def _whole(a, kind='const'):
    nd = a.ndim
    return Arg(a, a.shape, lambda *i, nd=nd: (0,) * nd, kind)


def map_fwd(name, fn, grid, args, outs):
    n_in = len(args)

    def body(*refs):
        pid = tuple(pl.program_id(a) for a in range(len(grid)))
        res = fn(pid, *[r[...] for r in refs[:n_in]])
        for o, r in zip(refs[n_in:], res):
            o[...] = r.astype(o.dtype)

    res = pl.pallas_call(
        body, name=name, grid=grid,
        in_specs=[pl.BlockSpec(a.block, a.imap) for a in args],
        out_specs=[pl.BlockSpec(o.block, o.imap) for o in outs],
        out_shape=[jax.ShapeDtypeStruct(o.shape, o.dtype) for o in outs],
        compiler_params=_cparams(len(grid)))(*[a.arr for a in args])
    return tuple(res)


def _grad_outs(args, wrt):
    outs = []
    for i in wrt:
        a = args[i]
        shape = a.arr.shape if a.gshape is None else a.gshape
        imap = a.imap if a.gimap is None else a.gimap
        outs.append(Out(shape, f32 if a.gdtype is None else a.gdtype, a.block, imap))
    return outs


def _store_grads(pid, args, wrt, grads, grefs, adds):
    first_all = functools.reduce(jnp.logical_and, [p == 0 for p in pid])
    first_in = functools.reduce(jnp.logical_and, [p == 0 for p in pid[1:]]) if len(pid) > 1 else first_all
    for j, i in enumerate(wrt):
        g = grads[j].astype(f32)
        ref = grefs[j]
        kind = args[i].kind
        if kind == 'tile':
            if j in adds:
                g = g + adds[j]
            ref[...] = g.astype(ref.dtype)
        else:
            first = first_all if kind == 'acc' else first_in

            @pl.when(first)
            def _(ref=ref):
                ref[...] = jnp.zeros_like(ref)

            ref[...] += g


def map_bwd(name, fn, grid, args, douts, wrt, add=None):
    add = add or {}
    n_in, n_d, n_add = len(args), len(douts), len(add)
    add_keys = sorted(add)
    gouts = _grad_outs(args, wrt)

    def body(*refs):
        pid = tuple(pl.program_id(a) for a in range(len(grid)))
        vals = [r[...] for r in refs[:n_in]]
        dvals = [r[...].astype(f32) for r in refs[n_in:n_in + n_d]]
        avals = {k: refs[n_in + n_d + j][...].astype(f32) for j, k in enumerate(add_keys)}
        grefs = refs[n_in + n_d + n_add:]

        def f(*w):
            full = list(vals)
            for i, x in zip(wrt, w):
                full[i] = x
            return tuple(fn(pid, *full))

        _, vjp = jax.vjp(f, *[vals[i] for i in wrt])
        grads = vjp(tuple(dvals))
        _store_grads(pid, args, wrt, grads, grefs, avals)

    ins = list(args) + list(douts) + [add[k] for k in add_keys]
    res = pl.pallas_call(
        body, name=name, grid=grid,
        in_specs=[pl.BlockSpec(a.block, a.imap) for a in ins],
        out_specs=[pl.BlockSpec(o.block, o.imap) for o in gouts],
        out_shape=[jax.ShapeDtypeStruct(o.shape, o.dtype) for o in gouts],
        compiler_params=_cparams(len(grid)))(*[a.arr for a in ins])
    return tuple(res)


class Side(NamedTuple):
    arrs: list
    out_shapes: list
    sem_shapes: list
    phases: list


def _run_side(side, step, total, src_refs, out_refs, sem_refs):
    for frac, phase in side.phases:
        @pl.when(step == int(round(frac * (total - 1))))
        def _(phase=phase):
            phase(src_refs, out_refs, *sem_refs)


_ANY = pl.BlockSpec(memory_space=pl.ANY)


def scan_fwd(name, fn, grid, carry_shapes, args, outs, side=None):
    no, nt = grid
    n_in, n_out, n_c = len(args), len(outs), len(carry_shapes)
    ns_in, ns_out = (len(side.arrs), len(side.out_shapes)) if side else (0, 0)
    cks = [Out((no, nt) + cs, f32, (None, None) + cs, lambda o, t, n=len(cs): (o, t) + (0,) * n) for cs in carry_shapes]

    def body(*refs):
        pid = (pl.program_id(0), pl.program_id(1))
        ins = refs[:n_in]
        sins = refs[n_in:n_in + ns_in]
        refs = refs[n_in + ns_in:]
        orefs = refs[:n_out]
        ckrefs = refs[n_out:n_out + n_c]
        souts = refs[n_out + n_c:n_out + n_c + ns_out]
        crefs = refs[n_out + n_c + ns_out:n_out + n_c + ns_out + n_c]
        if side:
            _run_side(side, pid[0] * nt + pid[1], no * nt, sins, souts, refs[n_out + n_c + ns_out + n_c:])

        @pl.when(pid[1] == 0)
        def _():
            for c in crefs:
                c[...] = jnp.zeros_like(c)

        carry = tuple(c[...] for c in crefs)
        for ck, c in zip(ckrefs, carry):
            ck[...] = c
        res, newc = fn(pid, carry, *[r[...] for r in ins])
        for o, r in zip(orefs, res):
            o[...] = r.astype(o.dtype)
        for c, v in zip(crefs, newc):
            c[...] = v

    allouts = list(outs) + cks
    res = pl.pallas_call(
        body, name=name, grid=grid,
        in_specs=[pl.BlockSpec(a.block, a.imap) for a in args] + [_ANY] * ns_in,
        out_specs=[pl.BlockSpec(o.block, o.imap) for o in allouts] + [_ANY] * ns_out,
        out_shape=[jax.ShapeDtypeStruct(o.shape, o.dtype) for o in allouts] + (list(side.out_shapes) if side else []),
        scratch_shapes=[pltpu.VMEM(cs, f32) for cs in carry_shapes] + (list(side.sem_shapes) if side else []),
        compiler_params=_cparams(2))(*[a.arr for a in args], *(side.arrs if side else []))
    if side:
        return tuple(res[:n_out]), tuple(res[n_out:n_out + n_c]), list(res[n_out + n_c:])
    return tuple(res[:n_out]), tuple(res[n_out:])


def scan_bwd(name, fn, grid, carry_shapes, args, ckpts, douts, wrt, bwd_fn=None, side=None):
    no, nt = grid
    n_in, n_d, n_c = len(args), len(douts), len(carry_shapes)
    ns_in, ns_out = (len(side.arrs), len(side.out_shapes)) if side else (0, 0)

    def rev(imap):
        return lambda o, t: imap(o, nt - 1 - t)

    rargs = [a._replace(imap=rev(a.imap), gimap=None if a.gimap is None else rev(a.gimap)) for a in args]
    rdouts = [a._replace(imap=rev(a.imap)) for a in douts]
    ckargs = [Arg(ck, (None, None) + cs, rev(lambda o, t, n=len(cs): (o, t) + (0,) * n)) for ck, cs in zip(ckpts, carry_shapes)]
    gouts = _grad_outs(rargs, wrt)

    def body(*refs):
        o, t = pl.program_id(0), pl.program_id(1)
        tt = nt - 1 - t
        vals = [r[...] for r in refs[:n_in]]
        dvals = [r[...].astype(f32) for r in refs[n_in:n_in + n_d]]
        carry = tuple(r[...] for r in refs[n_in + n_d:n_in + n_d + n_c])
        sins = refs[n_in + n_d + n_c:n_in + n_d + n_c + ns_in]
        refs = refs[n_in + n_d + n_c + ns_in:]
        grefs = refs[:len(wrt)]
        souts = refs[len(wrt):len(wrt) + ns_out]
        dcrefs = refs[len(wrt) + ns_out:len(wrt) + ns_out + n_c]
        if side:
            _run_side(side, o * nt + t, no * nt, sins, souts, refs[len(wrt) + ns_out + n_c:])

        @pl.when(t == 0)
        def _():
            for c in dcrefs:
                c[...] = jnp.zeros_like(c)

        def f(carry, *w):
            full = list(vals)
            for i, x in zip(wrt, w):
                full[i] = x
            res, newc = fn((o, tt), carry, *full)
            return tuple(res), tuple(newc)

        dcarry = tuple(c[...] for c in dcrefs)
        if bwd_fn is None:
            _, vjp = jax.vjp(f, carry, *[vals[i] for i in wrt])
            grads = vjp((tuple(dvals), dcarry))
            dcarry_in, grads = grads[0], grads[1:]
        else:
            dcarry_in, grads = bwd_fn((o, tt), carry, vals, dvals, dcarry)
        for c, g in zip(dcrefs, dcarry_in):
            c[...] = g
        _store_grads((o, t), rargs, wrt, grads, grefs, {})

    ins = rargs + rdouts + ckargs
    res = pl.pallas_call(
        body, name=name, grid=grid,
        in_specs=[pl.BlockSpec(a.block, a.imap) for a in ins] + [_ANY] * ns_in,
        out_specs=[pl.BlockSpec(g.block, g.imap) for g in gouts] + [_ANY] * ns_out,
        out_shape=[jax.ShapeDtypeStruct(g.shape, g.dtype) for g in gouts] + (list(side.out_shapes) if side else []),
        scratch_shapes=[pltpu.VMEM(cs, f32) for cs in carry_shapes] + (list(side.sem_shapes) if side else []),
        compiler_params=_cparams(2))(*[a.arr for a in ins], *(side.arrs if side else []))
    if side:
        return tuple(res[:len(gouts)]), list(res[len(gouts):])
    return tuple(res)


def _pick(dim, target):
    if dim <= target:
        return dim
    for t in range(target // 128 * 128, 127, -128):
        if dim % t == 0:
            return t
    return dim


def matmul(name, a, b, mode='nn', add=None, out_dtype=f32, tm=None, tn=1152, tk=None):
    if mode == 'tn':
        K, M = a.shape
    else:
        M, K = a.shape
    N = b.shape[0] if mode == 'nt' else b.shape[1]
    assert (b.shape[1] if mode == 'nt' else b.shape[0]) == K
    tm = (1024 if mode == 'tn' else 512) if tm is None else tm
    tk = (512 if mode == 'tn' else 1152) if tk is None else tk
    tm, tn, tk = _pick(M, tm), _pick(N, tn), _pick(K, tk)
    nk = K // tk
    a_spec = pl.BlockSpec((tk, tm), lambda i, j, k: (k, i)) if mode == 'tn' else pl.BlockSpec((tm, tk), lambda i, j, k: (i, k))
    b_spec = pl.BlockSpec((tn, tk), lambda i, j, k: (j, k)) if mode == 'nt' else pl.BlockSpec((tk, tn), lambda i, j, k: (k, j))
    dims = {'nn': (((1,), (0,)), ((), ())), 'nt': (((1,), (1,)), ((), ())), 'tn': (((0,), (0,)), ((), ()))}[mode]
    has_add = add is not None

    def body(*refs):
        if has_add:
            a_ref, b_ref, add_ref, o_ref, acc = refs
        else:
            a_ref, b_ref, o_ref, acc = refs
        k = pl.program_id(2)

        @pl.when(k == 0)
        def _():
            acc[...] = add_ref[...].astype(f32) if has_add else jnp.zeros_like(acc)

        acc[...] += lax.dot_general(a_ref[...].astype(bf16), b_ref[...].astype(bf16), dims, preferred_element_type=f32)

        @pl.when(k == nk - 1)
        def _():
            o_ref[...] = acc[...].astype(o_ref.dtype)

    in_specs = [a_spec, b_spec] + ([pl.BlockSpec((tm, tn), lambda i, j, k: (i, j))] if has_add else [])
    ops = [a, b] + ([add] if has_add else [])
    return pl.pallas_call(
        body, name=name, grid=(M // tm, N // tn, nk), in_specs=in_specs,
        out_specs=pl.BlockSpec((tm, tn), lambda i, j, k: (i, j)),
        out_shape=jax.ShapeDtypeStruct((M, N), out_dtype),
        scratch_shapes=[pltpu.VMEM((tm, tn), f32)],
        compiler_params=pltpu.CompilerParams(dimension_semantics=("parallel", "parallel", "arbitrary"), vmem_limit_bytes=VMEM_LIMIT))(*ops)


def matmul_nt_sum(name, lhs, rhs, tm=1024, tk=768, side=None):
    M, N = lhs[0].shape[0], rhs[0].shape[0]
    tm = _pick(M, tm)
    tks = [_pick(a.shape[1], tk) for a in lhs]
    starts, total = [], 0
    for a, t in zip(lhs, tks):
        starts.append(total)
        total += a.shape[1] // t
    npc = len(lhs)
    ns_in, ns_out = (len(side.arrs), len(side.out_shapes)) if side else (0, 0)

    def body(*refs):
        a_refs, b_refs, sins = refs[:npc], refs[npc:2 * npc], refs[2 * npc:2 * npc + ns_in]
        refs = refs[2 * npc + ns_in:]
        o_ref, souts, acc = refs[0], refs[1:1 + ns_out], refs[1 + ns_out]
        k = pl.program_id(1)
        if side:
            _run_side(side, pl.program_id(0) * total + k, (M // tm) * total, sins, souts, refs[2 + ns_out:])

        @pl.when(k == 0)
        def _():
            acc[...] = jnp.zeros_like(acc)

        for p in range(npc):
            @pl.when((k >= starts[p]) & (k < starts[p] + lhs[p].shape[1] // tks[p]))
            def _(p=p):
                acc[...] += lax.dot_general(a_refs[p][...].astype(bf16), b_refs[p][...].astype(bf16), _NT, preferred_element_type=f32)

        @pl.when(k == total - 1)
        def _():
            o_ref[...] = acc[...]

    def kblock(p):
        return lambda k: jnp.clip(k - starts[p], 0, lhs[p].shape[1] // tks[p] - 1)

    in_specs = [pl.BlockSpec((tm, tks[p]), lambda i, k, kb=kblock(p): (i, kb(k))) for p in range(npc)]
    in_specs += [pl.BlockSpec((N, tks[p]), lambda i, k, kb=kblock(p): (0, kb(k))) for p in range(npc)]
    res = pl.pallas_call(
        body, name=name, grid=(M // tm, total), in_specs=in_specs + [_ANY] * ns_in,
        out_specs=[pl.BlockSpec((tm, N), lambda i, k: (i, 0))] + [_ANY] * ns_out,
        out_shape=[jax.ShapeDtypeStruct((M, N), f32)] + (list(side.out_shapes) if side else []),
        scratch_shapes=[pltpu.VMEM((tm, N), f32)] + (list(side.sem_shapes) if side else []),
        compiler_params=_cparams(2))(*lhs, *rhs, *(side.arrs if side else []))
    return (res[0], list(res[1:])) if side else res[0]


def wgrad(name, act, dout):
    return matmul(name, act, dout, 'tn', out_dtype=bf16)


def _dot(a, b, dims=(((1,), (0,)), ((), ()))):
    return lax.dot_general(a.astype(bf16), b.astype(bf16), dims, preferred_element_type=f32)


_NT = (((1,), (1,)), ((), ()))
_TN = (((0,), (0,)), ((), ()))


def _three_term_dot(v, sel, dims):
    hi = v.astype(bf16)
    rest = v - hi.astype(f32)
    mid = rest.astype(bf16)
    lo = (rest - mid.astype(f32)).astype(bf16)
    dot = lambda t: lax.dot_general(t, sel, dims, preferred_element_type=f32)
    return dot(hi) + dot(mid) + dot(lo)


@jax.custom_vjp
def _dot_exact01(v, sel):
    return _three_term_dot(v, sel, (((1,), (0,)), ((), ())))


def _dot_exact01_fwd(v, sel):
    return _dot_exact01(v, sel), sel


def _dot_exact01_bwd(sel, ct):
    return _three_term_dot(ct, sel, _NT), jnp.zeros_like(sel)


_dot_exact01.defvjp(_dot_exact01_fwd, _dot_exact01_bwd)


def _spread_heads(v, width):
    r = lax.broadcasted_iota(jnp.int32, (HPAD, SSD_HEADS * width), 0)
    c = lax.broadcasted_iota(jnp.int32, (HPAD, SSD_HEADS * width), 1)
    return _dot_exact01(v, (r == c // width).astype(bf16))


def _rmsnorm_tile(pid, x, w):
    return (x * lax.rsqrt(jnp.mean(x * x, axis=-1, keepdims=True) + RMS_EPS) * w,)


def _shift_rows(h, d, fill):
    pad = jnp.full((d, h.shape[1]), fill, f32)
    return jnp.concatenate([pad, h[:-d]], axis=0)


def _s5_prep_tile(pid, a_re, a_im, ls, btr, bti, ctr, cti):
    o = pid[0]
    w = a_re.shape[1]
    r = lax.broadcasted_iota(jnp.int32, (S5_GROUPS, w), 0)
    c = lax.broadcasted_iota(jnp.int32, (S5_GROUPS, w), 1)
    sel = (r == o * (w // S5_STATE) + c // S5_STATE).astype(f32)
    step = jnp.dot(jnp.exp(ls), sel, precision=lax.Precision.HIGHEST, preferred_element_type=f32)
    mag = jnp.exp(a_re * step)
    ang = a_im * step
    lr, li = mag * jnp.cos(ang), mag * jnp.sin(ang)
    nr, ni = lr - 1.0, li
    den = a_re * a_re + a_im * a_im
    fr = (nr * a_re + ni * a_im) / den
    fi = (ni * a_re - nr * a_im) / den
    bbr = fr * btr - fi * bti
    bbi = fr * bti + fi * btr
    reps = w // S5_STATE
    rr = lax.broadcasted_iota(jnp.int32, (reps * S5_GROUP, w), 0)
    cc = lax.broadcasted_iota(jnp.int32, (reps * S5_GROUP, w), 1)
    diag = (rr // S5_GROUP) == (cc // S5_STATE)

    def expand(m):
        return jnp.where(diag, jnp.concatenate([m] * reps, axis=0), 0.0)

    pr, pi = lr, li
    rows_r, rows_i = [pr], [pi]
    for _ in range(S5_ND - 1):
        pr, pi = pr * pr - pi * pi, 2.0 * pr * pi
        rows_r.append(pr)
        rows_i.append(pi)
    lamd_r, lamd_i = jnp.concatenate(rows_r, axis=0), jnp.concatenate(rows_i, axis=0)
    tr = jnp.broadcast_to(lr, (S5_SUB, w))
    ti = jnp.broadcast_to(li, (S5_SUB, w))
    for j in range(S5_ND):
        sr, si = _shift_rows(tr, 1 << j, 1.0), _shift_rows(ti, 1 << j, 0.0)
        tr, ti = tr * sr - ti * si, tr * si + ti * sr
    return lamd_r, lamd_i, tr, ti, expand(bbr), expand(bbi), expand(ctr), expand(cti)


def _s5_tile(pid, carry, u, lamd_r, lamd_i, lam8_r, lam8_i, bbr, bbi, ccr, cci, dvec):
    hr, hi = _s5_scan(_dot(u, bbr), _dot(u, bbi), carry, lamd_r, lamd_i, lam8_r, lam8_i, reverse=False)
    return (_s5_readout(hr, hi, u, ccr, cci, dvec),), (hr[-1:], hi[-1:])


def _s5_readout(hr, hi, u, ccr, cci, dvec):
    return jax.nn.gelu(_dot(hr, ccr, _NT) - _dot(hi, cci, _NT) + dvec * u)


def _s5_scan(xr, xi, carry, lamd_r, lamd_i, lam8_r, lam8_i, reverse):
    cr, ci = carry
    T, G = xr.shape[0], S5_SUB
    sign = -1.0 if reverse else 1.0
    sub = lax.broadcasted_iota(jnp.int32, (T, 1), 0) % G
    for j in range(S5_ND):
        d = 1 << j
        if reverse:
            keep = sub < G - d
            sr = jnp.where(keep, jnp.concatenate([xr[d:], jnp.zeros((d, xr.shape[1]), f32)], axis=0), 0.0)
            si = jnp.where(keep, jnp.concatenate([xi[d:], jnp.zeros((d, xi.shape[1]), f32)], axis=0), 0.0)
        else:
            keep = sub >= d
            sr = jnp.where(keep, _shift_rows(xr, d, 0.0), 0.0)
            si = jnp.where(keep, _shift_rows(xi, d, 0.0), 0.0)
        ar, ai = lamd_r[j:j + 1], sign * lamd_i[j:j + 1]
        xr, xi = xr + ar * sr - ai * si, xi + ar * si + ai * sr
    if reverse:
        pr = jnp.concatenate([lam8_r[G - 1 - s:G - s] for s in range(G)], axis=0)
        pi = -jnp.concatenate([lam8_i[G - 1 - s:G - s] for s in range(G)], axis=0)
    else:
        pr, pi = lam8_r, lam8_i
    n = T // G
    rows_r, rows_i = [None] * n, [None] * n
    for i in (reversed(range(n)) if reverse else range(n)):
        gr_, gi_ = xr[i * G:(i + 1) * G], xi[i * G:(i + 1) * G]
        gr_, gi_ = gr_ + pr * cr - pi * ci, gi_ + pr * ci + pi * cr
        cr, ci = (gr_[:1], gi_[:1]) if reverse else (gr_[G - 1:], gi_[G - 1:])
        rows_r[i], rows_i[i] = gr_, gi_
    return jnp.concatenate(rows_r, axis=0), jnp.concatenate(rows_i, axis=0)


def _s5_tile_bwd(pid, carry, vals, douts, dcarry):
    u, lamd_r, lamd_i, lam8_r, lam8_i, bbr, bbi, ccr, cci, dvec = vals
    (dg,) = douts
    hr, hi = _s5_scan(_dot(u, bbr), _dot(u, bbi), carry, lamd_r, lamd_i, lam8_r, lam8_i, reverse=False)
    _, vjp = jax.vjp(_s5_readout, hr, hi, u, ccr, cci, dvec)
    dhr, dhi, du, dccr, dcci, ddvec = vjp(dg)
    Hr, Hi = _s5_scan(dhr, dhi, dcarry, lamd_r, lamd_i, lam8_r, lam8_i, reverse=True)
    _, vjp_in = jax.vjp(lambda u, bbr, bbi: (_dot(u, bbr), _dot(u, bbi)), u, bbr, bbi)
    du2, dbbr, dbbi = vjp_in((Hr, Hi))
    pr = jnp.concatenate([carry[0], hr[:-1]], axis=0)
    pi = jnp.concatenate([carry[1], hi[:-1]], axis=0)
    dlam_r = jnp.sum(Hr * pr + Hi * pi, axis=0, keepdims=True)
    dlam_i = jnp.sum(Hi * pr - Hr * pi, axis=0, keepdims=True)
    zrow = jnp.zeros((S5_ND - 1, dlam_r.shape[1]), f32)
    dlamd_r, dlamd_i = jnp.concatenate([dlam_r, zrow], axis=0), jnp.concatenate([dlam_i, zrow], axis=0)
    grads = (du + du2, dlamd_r, dlamd_i, jnp.zeros_like(lam8_r), jnp.zeros_like(lam8_i), dbbr, dbbi, dccr, dcci, ddvec)
    return (Hr[:1], Hi[:1]), grads


def _glu_tile(pid, g, glu, za, b):
    return (g * jax.nn.sigmoid(glu + b) * jax.nn.silu(za),)


def _attn_tile(pid, carry, q, k, v, qw, kw):
    n = pid[1]
    kp, vp = carry
    D, B = ATT_HEAD_DIM, ATT_BLOCK
    W = 2 * D
    nq, ncol = q.shape[0] // B, q.shape[1] // W
    r = lax.broadcasted_iota(jnp.int32, (B, 2 * B), 0)
    c = lax.broadcasted_iota(jnp.int32, (B, 2 * B), 1)
    diff = r + B - c
    band = (diff >= 0) & (diff <= B)
    band_first = band & ((c >= B) | (n > 0))
    low = lax.broadcasted_iota(jnp.int32, (1, W), 1) < D
    same_head = (lax.broadcasted_iota(jnp.int32, (W, W), 0) // D == lax.broadcasted_iota(jnp.int32, (W, W), 1) // D).astype(bf16)

    def hnorm(x, w):
        rows = x.shape[0]
        t = jnp.concatenate([x[:, j * W:(j + 1) * W] for j in range(ncol)], axis=0) if ncol > 1 else x
        ms = _dot_exact01(t * t, same_head) * (1.0 / D)
        t = t * lax.rsqrt(ms + RMS_EPS) * jnp.concatenate([w, w], axis=1)
        return [t[j * rows:(j + 1) * rows] for j in range(ncol)]

    qns, kns = hnorm(q, qw), hnorm(k, kw)
    out_cols, lse_cols, kn_cols = [], [], []
    for j in range(ncol):
        sl = slice(j * W, (j + 1) * W)
        qn, kn, vj = qns[j], kns[j], v[:, sl]
        kn_cols.append(kn[(nq - 1) * B:])
        outs, lses = [], []
        for b in range(nq):
            rows = slice(b * B, (b + 1) * B)
            prev = slice((b - 1) * B, b * B)
            kk = jnp.concatenate([kp[:, sl] if b == 0 else kn[prev], kn[rows]], axis=0)
            vv = jnp.concatenate([vp[:, sl] if b == 0 else vj[prev], vj[rows]], axis=0)
            o2, l2 = [], []
            for head_lanes in (low, ~low):
                s = _dot(jnp.where(head_lanes, qn[rows], 0.0), kk, _NT) * (D ** -0.5)
                s = jnp.where(band_first if b == 0 else band, s, -1e30)
                m = jnp.max(s, axis=-1, keepdims=True)
                p = jnp.exp(s - m)
                l = jnp.sum(p, axis=-1, keepdims=True)
                o2.append(_dot(p / l, vv))
                l2.append(m + jnp.log(l))
            outs.append(jnp.where(low, o2[0], o2[1]))
            lses.append(jnp.where(low, l2[0], l2[1]))
        out_cols.append(jnp.concatenate(outs, axis=0) if nq > 1 else outs[0])
        lse_cols.append(jnp.concatenate(lses, axis=0) if nq > 1 else lses[0])
    return ((jnp.concatenate(out_cols, axis=1), jnp.concatenate(lse_cols, axis=1)),
            (jnp.concatenate(kn_cols, axis=1), v[(nq - 1) * B:]))


def _combine_tile(pid, o1, l1, o2, l2, o3, l3, zb):
    m = jnp.maximum(jnp.maximum(l1, l2), l3)
    e1, e2, e3 = jnp.exp(l1 - m), jnp.exp(l2 - m), jnp.exp(l3 - m)
    y = (e1 * o1 + e2 * o2 + e3 * o3) / (e1 + e2 + e3)
    return (y * jax.nn.silu(zb),)


def _softplus(x):
    return jnp.maximum(x, 0.0) + jnp.log(1.0 + jnp.exp(-jnp.abs(x)))


def _ssd_tile(pid, carry, xbc, dt, z, conv_w, conv_b, dt_bias, a_log, dvec, norm_w):
    xprev, state = carry
    T, P, N = SSD_CHUNK, SSD_HEAD_DIM, SSD_STATE
    xx = jnp.concatenate([xprev, xbc], axis=0)
    conv = conv_b
    for k in range(SSD_CONV):
        off = 8 - (SSD_CONV - 1) + k
        conv = conv + conv_w[k:k + 1] * xx[off:off + T]
    xc = jax.nn.silu(conv)
    dtp = _softplus(dt + dt_bias)
    a_dt = dtp * (-jnp.exp(a_log))
    r = lax.broadcasted_iota(jnp.int32, (T, T), 0)
    c = lax.broadcasted_iota(jnp.int32, (T, T), 1)
    tri = r >= c
    trif = tri.astype(f32)
    hi = lax.Precision.HIGHEST
    a_cs = jnp.dot(trif, a_dt, precision=hi, preferred_element_type=f32)
    a_cs_t = lax.dot_general(a_dt, trif, (((0,), (1,)), ((), ())), precision=hi, preferred_element_type=f32)
    xs = xc[:, :SSD_WIDTH]
    acs_p = _spread_heads(a_cs, P)
    acs_t = _spread_heads(a_cs, T)
    xdt = xs * _spread_heads(dtp, P)
    skip = _spread_heads(dvec, P)
    to_end = jnp.exp(acs_p[T - 1:T] - acs_p)
    low = lax.broadcasted_iota(jnp.int32, (1, 2 * P), 1) < P
    low_rows = lax.broadcasted_iota(jnp.int32, (2 * P, 1), 0) < P
    ys, states = [], []
    for j in range(SSD_HEADS // 2):
        g = 2 * j // (SSD_HEADS // SSD_GROUPS)
        if 2 * j % (SSD_HEADS // SSD_GROUPS) == 0:
            bg = xc[:, SSD_WIDTH + g * N:SSD_WIDTH + (g + 1) * N]
            cg = xc[:, SSD_WIDTH + SSD_GROUPS * N + g * N:SSD_WIDTH + SSD_GROUPS * N + (g + 1) * N]
            cb = _dot(cg, bg, _NT)
        lanes = slice(2 * j * P, 2 * (j + 1) * P)
        st = state[lanes, :]
        diag, last = [], []
        for h in (2 * j, 2 * j + 1):
            decay = jnp.exp(jnp.where(tri, acs_t[:, h * T:(h + 1) * T] - a_cs_t[h:h + 1, :], -1e30))
            diag.append(_dot(cb * decay, xdt[:, lanes]))
            last.append(jnp.exp(a_cs_t[h:h + 1, T - 1:T]))
        y = (jnp.where(low, diag[0], diag[1]) + _dot(cg, st, _NT) * jnp.exp(acs_p[:, lanes])
             + xs[:, lanes] * skip[:, lanes])
        ys.append(y)
        states.append(jnp.where(low_rows, last[0], last[1]) * st + _dot(xdt[:, lanes] * to_end[:, lanes], bg, _TN))
    y = jnp.concatenate(ys, axis=1) * jax.nn.silu(z)
    out = y * lax.rsqrt(jnp.mean(y * y, axis=-1, keepdims=True) + RMS_EPS) * norm_w
    return (out,), (xbc[T - 8:], jnp.concatenate(states, axis=0))


def _merge_tile(pid, pa, pb, pc, g0, g1, g2):
    return (jax.nn.sigmoid(g0) * pa + jax.nn.sigmoid(g1) * pb + jax.nn.sigmoid(g2) * pc,)


def loss_and_grad(y, target, tm=512):
    S, D = y.shape
    nt = S // tm

    def body(y_ref, t_ref, dy_ref, l_ref, acc):
        i = pl.program_id(0)

        @pl.when(i == 0)
        def _():
            acc[...] = jnp.zeros_like(acc)

        diff = y_ref[...] - t_ref[...]
        dy_ref[...] = diff * (1.0 / D)
        acc[...] += jnp.sum((diff * diff).reshape(tm // 8, 8, D), axis=0)

        @pl.when(i == nt - 1)
        def _():
            l_ref[...] = jnp.broadcast_to(0.5 / D * jnp.sum(acc[...]), l_ref.shape)

    dy, l = pl.pallas_call(
        body, name="loss_head", grid=(nt,),
        in_specs=[pl.BlockSpec((tm, D), lambda i: (i, 0))] * 2,
        out_specs=[pl.BlockSpec((tm, D), lambda i: (i, 0)), pl.BlockSpec((8, 128), lambda i: (0, 0))],
        out_shape=[jax.ShapeDtypeStruct((S, D), f32), jax.ShapeDtypeStruct((8, 128), f32)],
        scratch_shapes=[pltpu.VMEM((8, D), f32)],
        compiler_params=_cparams(1))(y, target)
    return dy, l[0, 0]


def _row_tile(R, C, budget=1 << 20):
    best = R
    for t in range(8, R, 8):
        if R % t == 0 and t * C * 4 <= budget:
            best = t
    if best == R and R * C * 4 > budget:
        for t in range(8, R, 8):
            if R % t == 0:
                return t
    return best


def _as2d(t, lead=0):
    return t.reshape(t.shape[:lead] + (math.prod(t.shape[lead:-1]), t.shape[-1]))


def adamw_layers(name, w, slots, m, v):
    L, R, C = w.shape
    n = slots[0].shape[0]
    lanes = -(-C // 128) * 128
    tr = _row_tile(R, lanes * (n * L + 7), budget=10 << 20)

    def body(*refs):
        w_ref, m_ref, v_ref = refs[0], refs[1 + L], refs[2 + L]
        go_ref, d_ref, nm_ref, nv_ref = refs[3 + L:]
        layer = pl.program_id(0)
        gg = None
        for l in range(L):
            s = refs[1 + l][0].astype(f32)
            for j in range(1, n):
                s = s + refs[1 + l][j].astype(f32)
            gg = s if gg is None else jnp.where(layer == l, s, gg)
        go_ref[...] = gg
        nm = ADAM_B1 * m_ref[...] + (1.0 - ADAM_B1) * gg
        nv = ADAM_B2 * v_ref[...] + (1.0 - ADAM_B2) * jnp.square(gg)
        m_hat = nm / (1.0 - ADAM_B1 ** ADAM_STEP)
        v_hat = nv / (1.0 - ADAM_B2 ** ADAM_STEP)
        d_ref[...] = -ADAM_LR * (m_hat / (jnp.sqrt(v_hat) + ADAM_EPS) + ADAM_WD * w_ref[...])
        nm_ref[...] = nm
        nv_ref[...] = nv

    spec = pl.BlockSpec((None, tr, C), lambda l, i: (l, i, 0))
    slot_specs = [pl.BlockSpec((n, tr, C), lambda l, i, own=own: (0, jnp.where(l == own, i, 0), 0)) for own in range(L)]
    res = pl.pallas_call(
        body, name=name, grid=(L, R // tr),
        in_specs=[spec] + slot_specs + [spec, spec], out_specs=[spec] * 4,
        out_shape=[jax.ShapeDtypeStruct((L, R, C), f32)] * 4,
        compiler_params=_cparams(2))(w, *slots, m, v)
    return tuple(res)


def adamw(name, w, gslots, m, v):
    shape = w.shape
    n = gslots.shape[0]
    C = shape[-1]
    R = math.prod(shape[:-1])
    lanes = -(-C // 128) * 128
    tr = _row_tile(R, lanes * (n + 7), budget=10 << 20)

    def body(w_ref, g_ref, m_ref, v_ref, go_ref, d_ref, nm_ref, nv_ref):
        gg = g_ref[0].astype(f32)
        for s in range(1, n):
            gg = gg + g_ref[s].astype(f32)
        go_ref[...] = gg
        nm = ADAM_B1 * m_ref[...] + (1.0 - ADAM_B1) * gg
        nv = ADAM_B2 * v_ref[...] + (1.0 - ADAM_B2) * jnp.square(gg)
        m_hat = nm / (1.0 - ADAM_B1 ** ADAM_STEP)
        v_hat = nv / (1.0 - ADAM_B2 ** ADAM_STEP)
        d_ref[...] = -ADAM_LR * (m_hat / (jnp.sqrt(v_hat) + ADAM_EPS) + ADAM_WD * w_ref[...])
        nm_ref[...] = nm
        nv_ref[...] = nv

    spec = pl.BlockSpec((tr, C), lambda i: (i, 0))
    res = pl.pallas_call(
        body, name=name, grid=(R // tr,),
        in_specs=[spec, pl.BlockSpec((n, tr, C), lambda i: (0, i, 0)), spec, spec], out_specs=[spec] * 4,
        out_shape=[jax.ShapeDtypeStruct((R, C), f32)] * 4,
        compiler_params=_cparams(1))(w.reshape(R, C), gslots.reshape(n, R, C), m.reshape(R, C), v.reshape(R, C))
    return tuple(t.reshape(shape) for t in res)


PACK_ROWS = 256


def sum_slots(name, x):
    n, R, C = x.shape

    def body(x_ref, o_ref):
        acc = x_ref[0]
        for s in range(1, n):
            acc = acc + x_ref[s]
        o_ref[...] = acc

    return pl.pallas_call(
        body, name=name, grid=(R // PACK_ROWS,),
        in_specs=[pl.BlockSpec((n, PACK_ROWS, C), lambda i: (0, i, 0))],
        out_specs=pl.BlockSpec((PACK_ROWS, C), lambda i: (i, 0)),
        out_shape=jax.ShapeDtypeStruct((R, C), f32), compiler_params=_cparams(1))(x)


def _pack(parts):
    flat = jnp.concatenate([p.reshape(-1) for p in parts])
    unit = 128 * PACK_ROWS
    tot = -(-flat.shape[0] // unit) * unit
    return jnp.pad(flat, (0, tot - flat.shape[0])).reshape(tot // 128, 128)


def _unpack(buf, shapes):
    flat = buf.reshape(-1)
    out, off = [], 0
    for s in shapes:
        size = math.prod(s)
        out.append(flat[off:off + size].reshape(s))
        off += size
    return out


def _comm_sems(nt):
    return [pltpu.SemaphoreType.DMA((nt, N_DEV - 1)), pltpu.SemaphoreType.DMA((nt, N_DEV - 1)), pltpu.SemaphoreType.DMA((nt,))]


def exchange_side(srcs, modes):
    nt = len(srcs)
    slabs = []
    for s, mode in zip(srcs, modes):
        R, C = s.shape
        slabs.append({'all': (R, C), 'rows': (R // N_DEV, C), 'cols': (R, C // N_DEV)}[mode])

    def piece(ref, mode, slab, p):
        if mode == 'all':
            return ref
        if mode == 'rows':
            return ref.at[pl.ds(p * slab[0], slab[0]), :]
        return ref.at[:, pl.ds(p * slab[1], slab[1])]

    def copies(src_refs, out_refs, send_sems, recv_sems, local_sems):
        x, y, c = lax.axis_index("x"), lax.axis_index("y"), lax.axis_index("c")
        me = 4 * x + 2 * y + c
        out = []
        for k in (1, 2, 4, 3, 5, 6, 7):
            px = 1 - x if k & 4 else x
            py = 1 - y if k & 2 else y
            pc = 1 - c if k & 1 else c
            for t in range(nt):
                out.append(pltpu.make_async_remote_copy(
                    src_ref=piece(src_refs[t], modes[t], slabs[t], 4 * px + 2 * py + pc), dst_ref=out_refs[t].at[me],
                    send_sem=send_sems.at[t, k - 1], recv_sem=recv_sems.at[t, k - 1],
                    device_id=(px, py, pc), device_id_type=pl.DeviceIdType.MESH))
        for t in range(nt):
            out.append(pltpu.make_async_copy(piece(src_refs[t], modes[t], slabs[t], me), out_refs[t].at[me], local_sems.at[t]))
        return out

    def start(*refs):
        for cp in copies(*refs):
            cp.start()

    def finish(*refs):
        for cp in copies(*refs):
            cp.wait()

    out_shapes = [jax.ShapeDtypeStruct((N_DEV,) + sl, s.dtype) for s, sl in zip(srcs, slabs)]
    return Side(list(srcs), out_shapes, _comm_sems(nt), [(0.0, start), (1.0, finish)])


def gather_side(srcs):
    nt = len(srcs)

    def plan(src_refs, out_refs, send_sems, recv_sems, local_sems):
        x, y, c = lax.axis_index("x"), lax.axis_index("y"), lax.axis_index("c")
        me, sibling = (x, y, c), (x, y, 1 - c)
        chips = [(1 - x, y), (x, 1 - y), (1 - x, 1 - y)]

        def slot(t, dev):
            return out_refs[t].at[4 * dev[0] + 2 * dev[1] + dev[2]]

        def copy(t, k, block, to, src=None):
            return pltpu.make_async_remote_copy(
                src_ref=slot(t, block) if src is None else src, dst_ref=slot(t, block),
                send_sem=send_sems.at[t, k], recv_sem=recv_sems.at[t, k], device_id=to, device_id_type=pl.DeviceIdType.MESH)

        mine = [pltpu.make_async_copy(src_refs[t], slot(t, me), local_sems.at[t]) for t in range(nt)]
        first = []
        for t in range(nt):
            first.append(copy(t, 0, me, sibling, src=src_refs[t]))
            first += [copy(t, 1 + j, me, (*chip, c), src=src_refs[t]) for j, chip in enumerate(chips)]
        landed = [copy(t, 1 + j, (*chip, c), me) for j, chip in enumerate(chips) for t in range(nt)]
        passed = [copy(t, 4 + j, (*chip, c), sibling) for j, chip in enumerate(chips) for t in range(nt)]
        from_sibling = [copy(t, 0, sibling, me) for t in range(nt)]
        from_sibling += [copy(t, 4 + j, (*chip, 1 - c), me) for t in range(nt) for j, chip in enumerate(chips)]
        return mine, first, landed, passed, from_sibling

    def start(*refs):
        mine, first, _, _, _ = plan(*refs)
        for cp in mine + first:
            cp.start()

    def forward(*refs):
        _, _, landed, passed, _ = plan(*refs)
        for got, fwd in zip(landed, passed):
            got.wait_recv()
            fwd.start()

    def finish(*refs):
        mine, first, _, passed, from_sibling = plan(*refs)
        for cp in from_sibling:
            cp.wait_recv()
        for cp in first + passed:
            cp.wait_send()
        for cp in mine:
            cp.wait()

    out_shapes = [jax.ShapeDtypeStruct((N_DEV,) + s.shape, s.dtype) for s in srcs]
    return Side(list(srcs), out_shapes, _comm_sems(nt), [(0.0, start), (0.5, forward), (1.0, finish)])


def run_side(name, side):
    ns = len(side.arrs)

    def body(*refs):
        for _, phase in side.phases:
            phase(refs[:ns], refs[ns:ns + len(side.out_shapes)], *refs[ns + len(side.out_shapes):])

    return list(pl.pallas_call(
        body, name=name, in_specs=[_ANY] * ns, out_specs=[_ANY] * len(side.out_shapes),
        out_shape=list(side.out_shapes), scratch_shapes=list(side.sem_shapes))(*side.arrs))


def _relayout_w_in(w):
    offs = [0]
    for s in IN_SPLITS:
        offs.append(offs[-1] + s)
    p = [w[:, offs[i]:offs[i + 1]] for i in range(len(IN_SPLITS))]
    ua, za, q, k, v, zb, xbc, dt, zc, gates = p
    dtp = jnp.pad(dt, ((0, 0), (0, HPAD - dt.shape[1])))
    return (jnp.concatenate([ua, za, dtp], 1), w[:, offs[2]:offs[5]], jnp.concatenate([xbc, zb, zc], 1), gates)


def _pad_lanes(v, n=HPAD):
    return jnp.pad(v.reshape(1, -1), ((0, 0), (0, n - v.shape[-1])))


def _s5_prep_args(W):
    g2 = S5_GROUPS * S5_STATE
    w = g2 // S5_CHUNKS
    a_re, a_im = W['s5_a_re'].reshape(1, g2), W['s5_a_im'].reshape(1, g2)
    ls = W['s5_log_step'].reshape(1, S5_GROUPS)
    btr, bti = W['s5_b_re'].reshape(g2, S5_GROUP).T, W['s5_b_im'].reshape(g2, S5_GROUP).T
    ctr = W['s5_c_re'].transpose(1, 0, 2).reshape(S5_GROUP, g2)
    cti = W['s5_c_im'].transpose(1, 0, 2).reshape(S5_GROUP, g2)
    col = lambda a, rows: Arg(a, (rows, w), lambda o: (0, o), 'tile')
    return [col(a_re, 1), col(a_im, 1), _whole(ls, 'acc'), col(btr, S5_GROUP), col(bti, S5_GROUP), col(ctr, S5_GROUP), col(cti, S5_GROUP)]


def _s5_prep_outs():
    g2 = S5_GROUPS * S5_STATE
    w = g2 // S5_CHUNKS
    rows = (S5_ND, S5_ND, S5_SUB, S5_SUB, 128, 128, 128, 128)
    return [Out((r, g2), f32, (r, w), lambda o: (0, o)) for r in rows]


def _s5_args(A, prep, dvec, S):
    w = S5_GROUPS * S5_STATE // S5_CHUNKS
    args = [Arg(A, (S5_TILE, 128), lambda o, t: (t, o), 'tile', (S, S5_WIDTH), None, bf16)]
    for p in prep:
        args.append(Arg(p, (p.shape[0], w), lambda o, t: (0, o), 'acc0'))
    args.append(Arg(dvec, (1, 128), lambda o, t: (0, o), 'acc0'))
    return args


def _attn_args(QKV, g, r, qw, kw, S):
    L = S // r
    nq, rb = _attn_plan(r)
    block = (nq * ATT_BLOCK, rb * ATT_GW)
    gshape = (L, r * ATT_GW)
    gimap = lambda rho, n: (n, rho)
    if r == 1:
        mk = lambda j: Arg(QKV, block, lambda rho, n, j=j: (n, j), 'tile', gshape, gimap, bf16)
    else:
        def mk(j):
            view = QKV[:, j * ATT_GW:(j + 1) * ATT_GW].reshape(L, r * ATT_GW)
            return Arg(view, block, gimap, 'tile', None, None, bf16)
    return [mk(g), mk(3 + g), mk(6 + g), _whole(qw, 'acc'), _whole(kw, 'acc')]


def _attn_plan(r):
    return (4, 1) if r == 1 else (1, min(r, 4))


def _attn_grid(r, S):
    nq, rb = _attn_plan(r)
    return (r // rb, S // r // ATT_BLOCK // nq)


def _attn_carry(r):
    return ((ATT_BLOCK, _attn_plan(r)[1] * ATT_GW),) * 2


def _ssd_args(C, A, W, S):
    T = SSD_CHUNK
    return [Arg(C, (T, SSD_CONV_DIM), lambda o, t: (t, 0), 'tile', (S, SSD_CONV_DIM), None, bf16),
            Arg(A, (T, HPAD), lambda o, t: (t, 2 * S5_WIDTH // HPAD), 'tile', (S, HPAD), lambda o, t: (t, 0), bf16),
            Arg(C, (T, SSD_WIDTH), lambda o, t: (t, 2), 'tile', (S, SSD_WIDTH), lambda o, t: (t, 0), bf16),
            _whole(W['conv_w'], 'acc'), _whole(W['conv_b'].reshape(1, -1), 'acc'),
            _whole(_pad_lanes(W['dt_bias']), 'acc'), _whole(_pad_lanes(W['ssd_a_log']), 'acc'),
            _whole(_pad_lanes(W['ssd_d']), 'acc'), _whole(W['ssd_norm_w'].reshape(1, -1), 'acc')]


_SSD_CARRY = ((8, SSD_CONV_DIM), (SSD_WIDTH, SSD_STATE))
_S5_CARRY = ((1, 512), (1, 512))


def layer_fwd(li, x, W, side=None):
    S = x.shape[0]
    n = lambda s: f"l{li}_{s}"
    sv = {'x': x}
    (h,) = map_fwd(n("norm"), _rmsnorm_tile, (S // 512,), [_rows(x, 512), _whole(W['norm_w'].reshape(1, -1))],
                   [Out((S, D_MODEL), bf16, (512, D_MODEL), lambda i: (i, 0))])
    wA, wQ, wC, wG = W['w_in_pieces']
    A = matmul(n("in_a"), h, wA)
    QKV = matmul(n("in_qkv"), h, wQ)
    C = matmul(n("in_c"), h, wC)
    G = matmul(n("in_g"), h, wG)
    sv.update(h=h, A=A, QKV=QKV, C=C, G=G)

    prep = map_fwd(n("s5_prep"), _s5_prep_tile, (S5_CHUNKS,), _s5_prep_args(W), _s5_prep_outs())
    dvec = W['s5_d'].reshape(1, -1)
    (g,), s5_ck, *got = scan_fwd(n("s5_scan"), _s5_tile, (S5_CHUNKS, S // S5_TILE), _S5_CARRY, _s5_args(A, prep, dvec, S),
                                 [Out((S, S5_WIDTH), f32, (S5_TILE, 128), lambda o, t: (t, o))], side=side)
    glu = matmul(n("glu"), g, W['s5_glu_w'])
    glu_b = W['s5_glu_b'].reshape(1, -1)
    (ya,) = map_fwd(n("glu_gate"), _glu_tile, (S // 512,),
                    [_rows(g, 512), _rows(glu, 512), _rows(A, 512, col=1, width=S5_WIDTH), _whole(glu_b)],
                    [Out((S, S5_WIDTH), bf16, (512, S5_WIDTH), lambda i: (i, 0))])
    sv.update(prep=prep, g=g, glu=glu, ya=ya, s5_ck=s5_ck)

    qw, kw = W['q_norm_w'].reshape(1, -1), W['k_norm_w'].reshape(1, -1)
    att, att_ck = [], []
    for gi, (window, r) in enumerate(ATT_PAIRS):
        assert window // r == ATT_BLOCK and S % (r * ATT_BLOCK) == 0
        L = S // r
        nq, rb = _attn_plan(r)
        assert S // r // ATT_BLOCK % nq == 0
        spec = Out((L, r * ATT_GW), f32, (nq * ATT_BLOCK, rb * ATT_GW), lambda rho, nb: (nb, rho))
        (o, lse), ck = scan_fwd(n(f"attn{gi}"), _attn_tile, _attn_grid(r, S), _attn_carry(r), _attn_args(QKV, gi, r, qw, kw, S), [spec, spec])
        att += [o.reshape(S, ATT_GW), lse.reshape(S, ATT_GW)]
        att_ck.append(ck)
    (yb,) = map_fwd(n("combine"), _combine_tile, (S // 512,),
                    [_rows(t, 512) for t in att] + [_rows(C, 512, col=SSD_CONV_DIM // ATT_GW, width=ATT_GW)],
                    [Out((S, ATT_GW), bf16, (512, ATT_GW), lambda i: (i, 0))])
    sv.update(att=att, att_ck=att_ck, yb=yb)

    (yc,), ssd_ck = scan_fwd(n("ssd"), _ssd_tile, (1, S // SSD_CHUNK), _SSD_CARRY, _ssd_args(C, A, W, S),
                             [Out((S, SSD_WIDTH), bf16, (SSD_CHUNK, SSD_WIDTH), lambda o, t: (t, 0))])
    sv.update(yc=yc, ssd_ck=ssd_ck)

    pa = matmul(n("proj_a"), ya, W['proj_a'])
    pb = matmul(n("proj_b"), yb, W['proj_b'])
    pc = matmul(n("proj_c"), yc, W['proj_c'])
    (merged,) = map_fwd(n("merge"), _merge_tile, (S // 256,),
                        [_rows(pa, 256), _rows(pb, 256), _rows(pc, 256)] + [_rows(G, 256, col=j, width=D_MODEL) for j in range(3)],
                        [Out((S, D_MODEL), bf16, (256, D_MODEL), lambda i: (i, 0))])
    out = matmul(n("w_out"), merged, W['w_out'], add=x)
    sv.update(pa=pa, pb=pb, pc=pc, merged=merged)
    return out, sv, (got[0] if got else None)


def layer_bwd(li, dout, sv, W, side=None, own_scatter=None):
    S = dout.shape[0]
    n = lambda s: f"l{li}_{s}"
    gr = {}
    x, A, QKV, C, G = sv['x'], sv['A'], sv['QKV'], sv['C'], sv['G']

    dmerged = matmul(n("d_merged"), dout, W['w_out'], 'nt')
    gr['w_out'] = wgrad(n("g_w_out"), sv['merged'], dout)
    margs = [_rows(sv['pa'], 256, gdtype=bf16), _rows(sv['pb'], 256, gdtype=bf16), _rows(sv['pc'], 256, gdtype=bf16)] + \
            [_rows(G, 256, col=j, width=D_MODEL, gshape=(S, D_MODEL), gdtype=bf16) for j in range(3)]
    dpa, dpb, dpc, dg0, dg1, dg2 = map_bwd(n("merge_bwd"), _merge_tile, (S // 256,), margs, [_rows(dmerged, 256)], list(range(6)))
    dya = matmul(n("d_ya"), dpa, W['proj_a'], 'nt')
    dyb = matmul(n("d_yb"), dpb, W['proj_b'], 'nt')
    dyc = matmul(n("d_yc"), dpc, W['proj_c'], 'nt')
    gr['proj_a'] = wgrad(n("g_proj_a"), sv['ya'], dpa)
    gr['proj_b'] = wgrad(n("g_proj_b"), sv['yb'], dpb)
    gr['proj_c'] = wgrad(n("g_proj_c"), sv['yc'], dpc)

    glu_b = W['s5_glu_b'].reshape(1, -1)
    gargs = [_rows(sv['g'], 512), _rows(sv['glu'], 512, gdtype=bf16),
             _rows(A, 512, col=1, width=S5_WIDTH, gshape=(S, S5_WIDTH), gdtype=bf16), _whole(glu_b, 'acc')]
    dg_a, dglu, dza, dglu_b = map_bwd(n("glu_gate_bwd"), _glu_tile, (S // 512,), gargs, [_rows(dya, 512)], [0, 1, 2, 3])
    gr['s5_glu_b'] = dglu_b.reshape(-1)
    dg = matmul(n("d_g"), dglu, W['s5_glu_w'], 'nt', add=dg_a)
    gr['s5_glu_w'] = wgrad(n("g_glu_w"), sv['g'], dglu)
    dvec = W['s5_d'].reshape(1, -1)
    sargs = _s5_args(A, sv['prep'], dvec, S)
    res = scan_bwd(n("s5_scan_bwd"), _s5_tile, (S5_CHUNKS, S // S5_TILE), _S5_CARRY, sargs, sv['s5_ck'],
                   [Arg(dg, (S5_TILE, 128), lambda o, t: (t, o))], list(range(len(sargs))), bwd_fn=_s5_tile_bwd)
    dua, dprep, dd = res[0], res[1:9], res[9]
    gr['s5_d'] = dd.reshape(-1)
    pargs = _s5_prep_args(W)
    pouts = _s5_prep_outs()
    da_re, da_im, dls, dbtr, dbti, dctr, dcti = map_bwd(
        n("s5_prep_bwd"), _s5_prep_tile, (S5_CHUNKS,), pargs,
        [Arg(d, o.block, o.imap) for d, o in zip(dprep, pouts)], list(range(7)))
    gshape = (S5_GROUPS, S5_STATE)
    gr['s5_a_re'], gr['s5_a_im'] = da_re.reshape(gshape), da_im.reshape(gshape)
    gr['s5_log_step'] = dls.reshape(-1)
    gr['s5_b_re'] = dbtr.T.reshape(S5_GROUPS, S5_STATE, S5_GROUP)
    gr['s5_b_im'] = dbti.T.reshape(S5_GROUPS, S5_STATE, S5_GROUP)
    gr['s5_c_re'] = dctr.reshape(S5_GROUP, S5_GROUPS, S5_STATE).transpose(1, 0, 2)
    gr['s5_c_im'] = dcti.reshape(S5_GROUP, S5_GROUPS, S5_STATE).transpose(1, 0, 2)

    cargs = [_rows(t, 512) for t in sv['att']] + \
            [_rows(C, 512, col=SSD_CONV_DIM // ATT_GW, width=ATT_GW, gshape=(S, ATT_GW), gdtype=bf16)]
    cres = map_bwd(n("combine_bwd"), _combine_tile, (S // 512,), cargs, [_rows(dyb, 512)], list(range(7)))
    dzb = cres[6]
    qw, kw = W['q_norm_w'].reshape(1, -1), W['k_norm_w'].reshape(1, -1)
    dqs, dks, dvs = [], [], []
    dqw = dkw = None
    for gi, (window, r) in enumerate(ATT_PAIRS):
        L = S // r
        nq, rb = _attn_plan(r)
        dspec = lambda t: Arg(t.reshape(L, r * ATT_GW), (nq * ATT_BLOCK, rb * ATT_GW), lambda rho, nb: (nb, rho))
        dq, dk, dv, dqw_g, dkw_g = scan_bwd(n(f"attn{gi}_bwd"), _attn_tile, _attn_grid(r, S), _attn_carry(r),
                                            _attn_args(QKV, gi, r, qw, kw, S), sv['att_ck'][gi],
                                            [dspec(cres[2 * gi]), dspec(cres[2 * gi + 1])], [0, 1, 2, 3, 4])
        dqs.append(dq.reshape(S, ATT_GW))
        dks.append(dk.reshape(S, ATT_GW))
        dvs.append(dv.reshape(S, ATT_GW))
        dqw = dqw_g if dqw is None else dqw + dqw_g
        dkw = dkw_g if dkw is None else dkw + dkw_g
    gr['q_norm_w'], gr['k_norm_w'] = dqw.reshape(-1), dkw.reshape(-1)

    ssd_args = _ssd_args(C, A, W, S)
    sres = scan_bwd(n("ssd_bwd"), _ssd_tile, (1, S // SSD_CHUNK), _SSD_CARRY, ssd_args, sv['ssd_ck'],
                    [Arg(dyc, (SSD_CHUNK, SSD_WIDTH), lambda o, t: (t, 0))], list(range(9)), side=side)
    sres, got = sres if side else (sres, None)
    dxbc, ddt, dzc = sres[0], sres[1], sres[2]
    gr['conv_w'] = sres[3]
    gr['conv_b'] = sres[4].reshape(-1)
    gr['dt_bias'] = sres[5].reshape(-1)[:SSD_HEADS]
    gr['ssd_a_log'] = sres[6].reshape(-1)[:SSD_HEADS]
    gr['ssd_d'] = sres[7].reshape(-1)[:SSD_HEADS]
    gr['ssd_norm_w'] = sres[8].reshape(-1)

    dpieces = [jnp.concatenate([dua, dza, ddt], axis=1), jnp.concatenate(dqs + dks + dvs, axis=1),
               jnp.concatenate([dxbc, dzb, dzc], axis=1), jnp.concatenate([dg0, dg1, dg2], axis=1)]
    gr['w_in'] = _unrelayout_w_in_grad([wgrad(n(f"g_w_in{j}"), sv['h'], dp) for j, dp in enumerate(dpieces)])
    dh = matmul_nt_sum(n("d_h"), dpieces, list(W['w_in_pieces']), side=own_scatter(gr) if own_scatter else None)
    dh, got_own = dh if own_scatter else (dh, None)
    nargs = [_rows(x, 512), _whole(W['norm_w'].reshape(1, -1), 'acc')]
    dx, dnw = map_bwd(n("norm_bwd"), _rmsnorm_tile, (S // 512,), nargs, [_rows(dh, 512)], [0, 1], add={0: _rows(dout, 512)})
    gr['norm_w'] = dnw.reshape(-1)
    return dx, gr, got, got_own


def _unrelayout_w_in_grad(pieces):
    gA, gQ, gC, gG = pieces
    uaza, dt = gA[:, :2 * S5_WIDTH], gA[:, 2 * S5_WIDTH:2 * S5_WIDTH + SSD_HEADS]
    xbc, zb, zc = gC[:, :SSD_CONV_DIM], gC[:, SSD_CONV_DIM:SSD_CONV_DIM + ATT_GW], gC[:, SSD_CONV_DIM + ATT_GW:]
    return jnp.concatenate([uaza, gQ, zb, xbc, dt, zc, gG], axis=1)


def kernel(x, norm_w, w_in, s5_a_re, s5_a_im, s5_log_step, s5_b_re, s5_b_im, s5_c_re, s5_c_im, s5_d, s5_glu_w, s5_glu_b, q_norm_w, k_norm_w, conv_w, conv_b, dt_bias, ssd_a_log, ssd_d, ssd_norm_w, proj_a, proj_b, proj_c, w_out, loss_target, m_norm_w, m_w_in, m_s5_a_re, m_s5_a_im, m_s5_log_step, m_s5_b_re, m_s5_b_im, m_s5_c_re, m_s5_c_im, m_s5_d, m_s5_glu_w, m_s5_glu_b, m_q_norm_w, m_k_norm_w, m_conv_w, m_conv_b, m_dt_bias, m_ssd_a_log, m_ssd_d, m_ssd_norm_w, m_proj_a, m_proj_b, m_proj_c, m_w_out, v_norm_w, v_w_in, v_s5_a_re, v_s5_a_im, v_s5_log_step, v_s5_b_re, v_s5_b_im, v_s5_c_re, v_s5_c_im, v_s5_d, v_s5_glu_w, v_s5_glu_b, v_q_norm_w, v_k_norm_w, v_conv_w, v_conv_b, v_dt_bias, v_ssd_a_log, v_ssd_d, v_ssd_norm_w, v_proj_a, v_proj_b, v_proj_c, v_w_out):
    args = dict(locals())
    w = {k: args[k] for k in WEIGHTS}
    m = {k: args['m_' + k] for k in WEIGHTS}
    v = {k: args['v_' + k] for k in WEIGHTS}
    depth = norm_w.shape[0]
    S = x.shape[1]
    xs = x.reshape(S, D_MODEL)
    tgt = loss_target.reshape(S, D_MODEL)

    others = [k for k in SHARDED if k != 'w_in']

    def weight_gather(li):
        return gather_side(list(_relayout_w_in(w['w_in'][li].astype(bf16))) + [w[k][li].astype(bf16) for k in others])

    def assemble(li, gathered):
        W = {k: w[k][li] for k in WEIGHTS if k not in SHARDED}
        W['w_in_pieces'] = [t.reshape(t.shape[0] * t.shape[1], t.shape[2]) for t in gathered[:4]]
        for k, t in zip(others, gathered[4:]):
            n_dev, R, C = t.shape
            W[k] = t.reshape(n_dev * R, C) if k in ROW_SHARDED else t.transpose(1, 0, 2).reshape(R, n_dev * C)
        return W

    layers = [assemble(0, run_side("gather_weights0", weight_gather(0)))]
    act, saved = xs, []
    for li in range(depth):
        act, sv, got = layer_fwd(li, act, layers[li], weight_gather(li + 1) if li + 1 < depth else None)
        saved.append(sv)
        if got is not None:
            layers.append(assemble(li + 1, got))
    dy, loss_local = loss_and_grad(act, tgt)
    loss = lax.psum(loss_local, ("x", "y", "c"))

    big = [k for k in SHARDED if k != 'conv_w']

    def grad_scatter(gr):
        return exchange_side([gr[k] for k in big], ['rows' if k in ROW_SHARDED else 'cols' for k in big])

    grads, slots = [None] * depth, [None] * depth
    for li in reversed(range(depth)):
        dy, grads[li], got, got_own = layer_bwd(li, dy, saved[li], layers[li], grad_scatter(grads[li + 1]) if li + 1 < depth else None,
                                                grad_scatter if li == 0 else None)
        if got is not None:
            slots[li + 1] = got
        if got_own is not None:
            slots[li] = got_own
    small_keys = [k for k in WEIGHTS if k not in SHARDED] + ['conv_w']
    stacked = [jnp.stack([grads[li][k] for li in range(depth)], axis=0) for k in small_keys]
    (small_slots,) = run_side("gather_small_grads", exchange_side([_pack(stacked)], ['all']))
    grad_x = dy.reshape(x.shape)
    result = {k: adamw_layers("adamw_" + k, w[k], [slots[li][j] for li in range(depth)], m[k], v[k]) for j, k in enumerate(big)}

    totals = _unpack(sum_slots("sum_small_grads", small_slots), [t.shape for t in stacked])
    for k, g in zip(small_keys, totals):
        if k == 'conv_w':
            width = w[k].shape[-1]
            me = 4 * lax.axis_index("x") + 2 * lax.axis_index("y") + lax.axis_index("c")
            g = lax.dynamic_slice_in_dim(g, me * width, width, axis=2)
        result[k] = adamw("adamw_" + k, w[k], g[None], m[k], v[k])

    return (loss, grad_x, *[result[k][0] for k in WEIGHTS], *[result[k][1] for k in WEIGHTS],
            *[result[k][2] for k in WEIGHTS], *[result[k][3] for k in WEIGHTS])
```

```python
import functools
import math
from typing import Any, NamedTuple

import jax
import jax.numpy as jnp
from jax import lax
from jax.experimental import pallas as pl
from jax.experimental.pallas import tpu as pltpu

f32 = jnp.float32
bf16 = jnp.bfloat16

N_DEV = 8
D_MODEL = 1024
RMS_EPS = 1e-6
S5_WIDTH = 512
S5_GROUPS = 32
S5_GROUP = 16
S5_STATE = 64
S5_TILE = 512
S5_SUB = 8
S5_ND = 3
S5_CHUNKS = 4
ATT_HEAD_DIM = 64
ATT_PAIRS = ((128, 1), (512, 4), (2048, 16))
ATT_HPG = 4
ATT_BLOCK = 128
ATT_GW = ATT_HPG * ATT_HEAD_DIM
ATT_WIDTH = 768
SSD_HEADS = 12
SSD_HEAD_DIM = 64
SSD_WIDTH = 768
SSD_STATE = 128
SSD_GROUPS = 2
SSD_CHUNK = 128
SSD_CONV = 4
SSD_CONV_DIM = 1280
HPAD = 128
IN_SPLITS = (512, 512, 768, 768, 768, 256, 1280, 12, 768, 3072)
ADAM_LR, ADAM_B1, ADAM_B2, ADAM_EPS, ADAM_WD, ADAM_STEP = 0.001, 0.9, 0.999, 1e-08, 0.01, 10
VMEM_LIMIT = 56 * 1024 * 1024

WEIGHTS = ['norm_w', 'w_in', 's5_a_re', 's5_a_im', 's5_log_step', 's5_b_re', 's5_b_im', 's5_c_re',
           's5_c_im', 's5_d', 's5_glu_w', 's5_glu_b', 'q_norm_w', 'k_norm_w', 'conv_w', 'conv_b',
           'dt_bias', 'ssd_a_log', 'ssd_d', 'ssd_norm_w', 'proj_a', 'proj_b', 'proj_c', 'w_out']
ROW_SHARDED = ('w_in', 's5_glu_w', 'w_out')
SHARDED = ROW_SHARDED + ('conv_w', 'proj_a', 'proj_b', 'proj_c')


class Arg(NamedTuple):
    arr: Any
    block: tuple
    imap: Any
    kind: str = 'const'
    gshape: Any = None
    gimap: Any = None
    gdtype: Any = None


class Out(NamedTuple):
    shape: tuple
    dtype: Any
    block: tuple
    imap: Any


def _cparams(n):
    return pltpu.CompilerParams(dimension_semantics=("arbitrary",) * n, vmem_limit_bytes=VMEM_LIMIT)


def _rows(a, tm, kind='tile', col=0, width=None, gshape=None, gcol=None, gdtype=None):
    width = a.shape[1] if width is None else width
    g = None if gshape is None else (lambda i, gc=(0 if gcol is None else gcol): (i, gc))
    return Arg(a, (tm, width), lambda i, c=col: (i, c), kind, gshape, g, gdtype)


def _whole(a, kind='const'):
    nd = a.ndim
    return Arg(a, a.shape, lambda *i, nd=nd: (0,) * nd, kind)


def map_fwd(name, fn, grid, args, outs):
    n_in = len(args)

    def body(*refs):
        pid = tuple(pl.program_id(a) for a in range(len(grid)))
        res = fn(pid, *[r[...] for r in refs[:n_in]])
        for o, r in zip(refs[n_in:], res):
            o[...] = r.astype(o.dtype)

    res = pl.pallas_call(
        body, name=name, grid=grid,
        in_specs=[pl.BlockSpec(a.block, a.imap) for a in args],
        out_specs=[pl.BlockSpec(o.block, o.imap) for o in outs],
        out_shape=[jax.ShapeDtypeStruct(o.shape, o.dtype) for o in outs],
        compiler_params=_cparams(len(grid)))(*[a.arr for a in args])
    return tuple(res)


def _grad_outs(args, wrt):
    outs = []
    for i in wrt:
        a = args[i]
        shape = a.arr.shape if a.gshape is None else a.gshape
        imap = a.imap if a.gimap is None else a.gimap
        outs.append(Out(shape, f32 if a.gdtype is None else a.gdtype, a.block, imap))
    return outs


def _store_grads(pid, args, wrt, grads, grefs, adds):
    first_all = functools.reduce(jnp.logical_and, [p == 0 for p in pid])
    first_in = functools.reduce(jnp.logical_and, [p == 0 for p in pid[1:]]) if len(pid) > 1 else first_all
    for j, i in enumerate(wrt):
        g = grads[j].astype(f32)
        ref = grefs[j]
        kind = args[i].kind
        if kind == 'tile':
            if j in adds:
                g = g + adds[j]
            ref[...] = g.astype(ref.dtype)
        else:
            first = first_all if kind == 'acc' else first_in

            @pl.when(first)
            def _(ref=ref):
                ref[...] = jnp.zeros_like(ref)

            ref[...] += g


def map_bwd(name, fn, grid, args, douts, wrt, add=None):
    add = add or {}
    n_in, n_d, n_add = len(args), len(douts), len(add)
    add_keys = sorted(add)
    gouts = _grad_outs(args, wrt)

    def body(*refs):
        pid = tuple(pl.program_id(a) for a in range(len(grid)))
        vals = [r[...] for r in refs[:n_in]]
        dvals = [r[...].astype(f32) for r in refs[n_in:n_in + n_d]]
        avals = {k: refs[n_in + n_d + j][...].astype(f32) for j, k in enumerate(add_keys)}
        grefs = refs[n_in + n_d + n_add:]

        def f(*w):
            full = list(vals)
            for i, x in zip(wrt, w):
                full[i] = x
            return tuple(fn(pid, *full))

        _, vjp = jax.vjp(f, *[vals[i] for i in wrt])
        grads = vjp(tuple(dvals))
        _store_grads(pid, args, wrt, grads, grefs, avals)

    ins = list(args) + list(douts) + [add[k] for k in add_keys]
    res = pl.pallas_call(
        body, name=name, grid=grid,
        in_specs=[pl.BlockSpec(a.block, a.imap) for a in ins],
        out_specs=[pl.BlockSpec(o.block, o.imap) for o in gouts],
        out_shape=[jax.ShapeDtypeStruct(o.shape, o.dtype) for o in gouts],
        compiler_params=_cparams(len(grid)))(*[a.arr for a in ins])
    return tuple(res)


class Side(NamedTuple):
    arrs: list
    out_shapes: list
    sem_shapes: list
    phases: list


def _run_side(side, step, total, src_refs, out_refs, sem_refs):
    for frac, phase in side.phases:
        @pl.when(step == int(round(frac * (total - 1))))
        def _(phase=phase):
            phase(src_refs, out_refs, *sem_refs)


_ANY = pl.BlockSpec(memory_space=pl.ANY)


def scan_fwd(name, fn, grid, carry_shapes, args, outs, side=None):
    no, nt = grid
    n_in, n_out, n_c = len(args), len(outs), len(carry_shapes)
    ns_in, ns_out = (len(side.arrs), len(side.out_shapes)) if side else (0, 0)
    cks = [Out((no, nt) + cs, f32, (None, None) + cs, lambda o, t, n=len(cs): (o, t) + (0,) * n) for cs in carry_shapes]

    def body(*refs):
        pid = (pl.program_id(0), pl.program_id(1))
        ins = refs[:n_in]
        sins = refs[n_in:n_in + ns_in]
        refs = refs[n_in + ns_in:]
        orefs = refs[:n_out]
        ckrefs = refs[n_out:n_out + n_c]
        souts = refs[n_out + n_c:n_out + n_c + ns_out]
        crefs = refs[n_out + n_c + ns_out:n_out + n_c + ns_out + n_c]
        if side:
            _run_side(side, pid[0] * nt + pid[1], no * nt, sins, souts, refs[n_out + n_c + ns_out + n_c:])

        @pl.when(pid[1] == 0)
        def _():
            for c in crefs:
                c[...] = jnp.zeros_like(c)

        carry = tuple(c[...] for c in crefs)
        for ck, c in zip(ckrefs, carry):
            ck[...] = c
        res, newc = fn(pid, carry, *[r[...] for r in ins])
        for o, r in zip(orefs, res):
            o[...] = r.astype(o.dtype)
        for c, v in zip(crefs, newc):
            c[...] = v

    allouts = list(outs) + cks
    res = pl.pallas_call(
        body, name=name, grid=grid,
        in_specs=[pl.BlockSpec(a.block, a.imap) for a in args] + [_ANY] * ns_in,
        out_specs=[pl.BlockSpec(o.block, o.imap) for o in allouts] + [_ANY] * ns_out,
        out_shape=[jax.ShapeDtypeStruct(o.shape, o.dtype) for o in allouts] + (list(side.out_shapes) if side else []),
        scratch_shapes=[pltpu.VMEM(cs, f32) for cs in carry_shapes] + (list(side.sem_shapes) if side else []),
        compiler_params=_cparams(2))(*[a.arr for a in args], *(side.arrs if side else []))
    if side:
        return tuple(res[:n_out]), tuple(res[n_out:n_out + n_c]), list(res[n_out + n_c:])
    return tuple(res[:n_out]), tuple(res[n_out:])


def scan_bwd(name, fn, grid, carry_shapes, args, ckpts, douts, wrt, bwd_fn=None, side=None):
    no, nt = grid
    n_in, n_d, n_c = len(args), len(douts), len(carry_shapes)
    ns_in, ns_out = (len(side.arrs), len(side.out_shapes)) if side else (0, 0)

    def rev(imap):
        return lambda o, t: imap(o, nt - 1 - t)

    rargs = [a._replace(imap=rev(a.imap), gimap=None if a.gimap is None else rev(a.gimap)) for a in args]
    rdouts = [a._replace(imap=rev(a.imap)) for a in douts]
    ckargs = [Arg(ck, (None, None) + cs, rev(lambda o, t, n=len(cs): (o, t) + (0,) * n)) for ck, cs in zip(ckpts, carry_shapes)]
    gouts = _grad_outs(rargs, wrt)

    def body(*refs):
        o, t = pl.program_id(0), pl.program_id(1)
        tt = nt - 1 - t
        vals = [r[...] for r in refs[:n_in]]
        dvals = [r[...].astype(f32) for r in refs[n_in:n_in + n_d]]
        carry = tuple(r[...] for r in refs[n_in + n_d:n_in + n_d + n_c])
        sins = refs[n_in + n_d + n_c:n_in + n_d + n_c + ns_in]
        refs = refs[n_in + n_d + n_c + ns_in:]
        grefs = refs[:len(wrt)]
        souts = refs[len(wrt):len(wrt) + ns_out]
        dcrefs = refs[len(wrt) + ns_out:len(wrt) + ns_out + n_c]
        if side:
            _run_side(side, o * nt + t, no * nt, sins, souts, refs[len(wrt) + ns_out + n_c:])

        @pl.when(t == 0)
        def _():
            for c in dcrefs:
                c[...] = jnp.zeros_like(c)

        def f(carry, *w):
            full = list(vals)
            for i, x in zip(wrt, w):
                full[i] = x
            res, newc = fn((o, tt), carry, *full)
            return tuple(res), tuple(newc)

        dcarry = tuple(c[...] for c in dcrefs)
        if bwd_fn is None:
            _, vjp = jax.vjp(f, carry, *[vals[i] for i in wrt])
            grads = vjp((tuple(dvals), dcarry))
            dcarry_in, grads = grads[0], grads[1:]
        else:
            dcarry_in, grads = bwd_fn((o, tt), carry, vals, dvals, dcarry)
        for c, g in zip(dcrefs, dcarry_in):
            c[...] = g
        _store_grads((o, t), rargs, wrt, grads, grefs, {})

    ins = rargs + rdouts + ckargs
    res = pl.pallas_call(
        body, name=name, grid=grid,
        in_specs=[pl.BlockSpec(a.block, a.imap) for a in ins] + [_ANY] * ns_in,
        out_specs=[pl.BlockSpec(g.block, g.imap) for g in gouts] + [_ANY] * ns_out,
        out_shape=[jax.ShapeDtypeStruct(g.shape, g.dtype) for g in gouts] + (list(side.out_shapes) if side else []),
        scratch_shapes=[pltpu.VMEM(cs, f32) for cs in carry_shapes] + (list(side.sem_shapes) if side else []),
        compiler_params=_cparams(2))(*[a.arr for a in ins], *(side.arrs if side else []))
    if side:
        return tuple(res[:len(gouts)]), list(res[len(gouts):])
    return tuple(res)


def _pick(dim, target):
    if dim <= target:
        return dim
    for t in range(target // 128 * 128, 127, -128):
        if dim % t == 0:
            return t
    return dim


def matmul(name, a, b, mode='nn', add=None, out_dtype=f32, tm=None, tn=1152, tk=None):
    if mode == 'tn':
        K, M = a.shape
    else:
        M, K = a.shape
    N = b.shape[0] if mode == 'nt' else b.shape[1]
    assert (b.shape[1] if mode == 'nt' else b.shape[0]) == K
    tm = (1024 if mode == 'tn' else 512) if tm is None else tm
    tk = (512 if mode == 'tn' else 1152) if tk is None else tk
    tm, tn, tk = _pick(M, tm), _pick(N, tn), _pick(K, tk)
    nk = K // tk
    a_spec = pl.BlockSpec((tk, tm), lambda i, j, k: (k, i)) if mode == 'tn' else pl.BlockSpec((tm, tk), lambda i, j, k: (i, k))
    b_spec = pl.BlockSpec((tn, tk), lambda i, j, k: (j, k)) if mode == 'nt' else pl.BlockSpec((tk, tn), lambda i, j, k: (k, j))
    dims = {'nn': (((1,), (0,)), ((), ())), 'nt': (((1,), (1,)), ((), ())), 'tn': (((0,), (0,)), ((), ()))}[mode]
    has_add = add is not None

    def body(*refs):
        if has_add:
            a_ref, b_ref, add_ref, o_ref, acc = refs
        else:
            a_ref, b_ref, o_ref, acc = refs
        k = pl.program_id(2)

        @pl.when(k == 0)
        def _():
            acc[...] = add_ref[...].astype(f32) if has_add else jnp.zeros_like(acc)

        acc[...] += lax.dot_general(a_ref[...].astype(bf16), b_ref[...].astype(bf16), dims, preferred_element_type=f32)

        @pl.when(k == nk - 1)
        def _():
            o_ref[...] = acc[...].astype(o_ref.dtype)

    in_specs = [a_spec, b_spec] + ([pl.BlockSpec((tm, tn), lambda i, j, k: (i, j))] if has_add else [])
    ops = [a, b] + ([add] if has_add else [])
    return pl.pallas_call(
        body, name=name, grid=(M // tm, N // tn, nk), in_specs=in_specs,
        out_specs=pl.BlockSpec((tm, tn), lambda i, j, k: (i, j)),
        out_shape=jax.ShapeDtypeStruct((M, N), out_dtype),
        scratch_shapes=[pltpu.VMEM((tm, tn), f32)],
        compiler_params=pltpu.CompilerParams(dimension_semantics=("parallel", "parallel", "arbitrary"), vmem_limit_bytes=VMEM_LIMIT))(*ops)


def matmul_nt_sum(name, lhs, rhs, tm=1024, tk=768, side=None):
    M, N = lhs[0].shape[0], rhs[0].shape[0]
    tm = _pick(M, tm)
    tks = [_pick(a.shape[1], tk) for a in lhs]
    starts, total = [], 0
    for a, t in zip(lhs, tks):
        starts.append(total)
        total += a.shape[1] // t
    npc = len(lhs)
    ns_in, ns_out = (len(side.arrs), len(side.out_shapes)) if side else (0, 0)

    def body(*refs):
        a_refs, b_refs, sins = refs[:npc], refs[npc:2 * npc], refs[2 * npc:2 * npc + ns_in]
        refs = refs[2 * npc + ns_in:]
        o_ref, souts, acc = refs[0], refs[1:1 + ns_out], refs[1 + ns_out]
        k = pl.program_id(1)
        if side:
            _run_side(side, pl.program_id(0) * total + k, (M // tm) * total, sins, souts, refs[2 + ns_out:])

        @pl.when(k == 0)
        def _():
            acc[...] = jnp.zeros_like(acc)

        for p in range(npc):
            @pl.when((k >= starts[p]) & (k < starts[p] + lhs[p].shape[1] // tks[p]))
            def _(p=p):
                acc[...] += lax.dot_general(a_refs[p][...].astype(bf16), b_refs[p][...].astype(bf16), _NT, preferred_element_type=f32)

        @pl.when(k == total - 1)
        def _():
            o_ref[...] = acc[...]

    def kblock(p):
        return lambda k: jnp.clip(k - starts[p], 0, lhs[p].shape[1] // tks[p] - 1)

    in_specs = [pl.BlockSpec((tm, tks[p]), lambda i, k, kb=kblock(p): (i, kb(k))) for p in range(npc)]
    in_specs += [pl.BlockSpec((N, tks[p]), lambda i, k, kb=kblock(p): (0, kb(k))) for p in range(npc)]
    res = pl.pallas_call(
        body, name=name, grid=(M // tm, total), in_specs=in_specs + [_ANY] * ns_in,
        out_specs=[pl.BlockSpec((tm, N), lambda i, k: (i, 0))] + [_ANY] * ns_out,
        out_shape=[jax.ShapeDtypeStruct((M, N), f32)] + (list(side.out_shapes) if side else []),
        scratch_shapes=[pltpu.VMEM((tm, N), f32)] + (list(side.sem_shapes) if side else []),
        compiler_params=_cparams(2))(*lhs, *rhs, *(side.arrs if side else []))
    return (res[0], list(res[1:])) if side else res[0]


def wgrad(name, act, dout):
    return matmul(name, act, dout, 'tn', out_dtype=bf16)


def _dot(a, b, dims=(((1,), (0,)), ((), ()))):
    return lax.dot_general(a.astype(bf16), b.astype(bf16), dims, preferred_element_type=f32)


_NT = (((1,), (1,)), ((), ()))
_TN = (((0,), (0,)), ((), ()))


def _three_term_dot(v, sel, dims):
    hi = v.astype(bf16)
    rest = v - hi.astype(f32)
    mid = rest.astype(bf16)
    lo = (rest - mid.astype(f32)).astype(bf16)
    dot = lambda t: lax.dot_general(t, sel, dims, preferred_element_type=f32)
    return dot(hi) + dot(mid) + dot(lo)


@jax.custom_vjp
def _dot_exact01(v, sel):
    return _three_term_dot(v, sel, (((1,), (0,)), ((), ())))


def _dot_exact01_fwd(v, sel):
    return _dot_exact01(v, sel), sel


def _dot_exact01_bwd(sel, ct):
    return _three_term_dot(ct, sel, _NT), jnp.zeros_like(sel)


_dot_exact01.defvjp(_dot_exact01_fwd, _dot_exact01_bwd)


def _spread_heads(v, width):
    r = lax.broadcasted_iota(jnp.int32, (HPAD, SSD_HEADS * width), 0)
    c = lax.broadcasted_iota(jnp.int32, (HPAD, SSD_HEADS * width), 1)
    return _dot_exact01(v, (r == c // width).astype(bf16))


def _rmsnorm_tile(pid, x, w):
    return (x * lax.rsqrt(jnp.mean(x * x, axis=-1, keepdims=True) + RMS_EPS) * w,)


def _shift_rows(h, d, fill):
    pad = jnp.full((d, h.shape[1]), fill, f32)
    return jnp.concatenate([pad, h[:-d]], axis=0)


def _s5_prep_tile(pid, a_re, a_im, ls, btr, bti, ctr, cti):
    o = pid[0]
    w = a_re.shape[1]
    r = lax.broadcasted_iota(jnp.int32, (S5_GROUPS, w), 0)
    c = lax.broadcasted_iota(jnp.int32, (S5_GROUPS, w), 1)
    sel = (r == o * (w // S5_STATE) + c // S5_STATE).astype(f32)
    step = jnp.dot(jnp.exp(ls), sel, precision=lax.Precision.HIGHEST, preferred_element_type=f32)
    mag = jnp.exp(a_re * step)
    ang = a_im * step
    lr, li = mag * jnp.cos(ang), mag * jnp.sin(ang)
    nr, ni = lr - 1.0, li
    den = a_re * a_re + a_im * a_im
    fr = (nr * a_re + ni * a_im) / den
    fi = (ni * a_re - nr * a_im) / den
    bbr = fr * btr - fi * bti
    bbi = fr * bti + fi * btr
    reps = w // S5_STATE
    rr = lax.broadcasted_iota(jnp.int32, (reps * S5_GROUP, w), 0)
    cc = lax.broadcasted_iota(jnp.int32, (reps * S5_GROUP, w), 1)
    diag = (rr // S5_GROUP) == (cc // S5_STATE)

    def expand(m):
        return jnp.where(diag, jnp.concatenate([m] * reps, axis=0), 0.0)

    pr, pi = lr, li
    rows_r, rows_i = [pr], [pi]
    for _ in range(S5_ND - 1):
        pr, pi = pr * pr - pi * pi, 2.0 * pr * pi
        rows_r.append(pr)
        rows_i.append(pi)
    lamd_r, lamd_i = jnp.concatenate(rows_r, axis=0), jnp.concatenate(rows_i, axis=0)
    tr = jnp.broadcast_to(lr, (S5_SUB, w))
    ti = jnp.broadcast_to(li, (S5_SUB, w))
    for j in range(S5_ND):
        sr, si = _shift_rows(tr, 1 << j, 1.0), _shift_rows(ti, 1 << j, 0.0)
        tr, ti = tr * sr - ti * si, tr * si + ti * sr
    return lamd_r, lamd_i, tr, ti, expand(bbr), expand(bbi), expand(ctr), expand(cti)


def _s5_tile(pid, carry, u, lamd_r, lamd_i, lam8_r, lam8_i, bbr, bbi, ccr, cci, dvec):
    hr, hi = _s5_scan(_dot(u, bbr), _dot(u, bbi), carry, lamd_r, lamd_i, lam8_r, lam8_i, reverse=False)
    return (_s5_readout(hr, hi, u, ccr, cci, dvec),), (hr[-1:], hi[-1:])


def _s5_readout(hr, hi, u, ccr, cci, dvec):
    return jax.nn.gelu(_dot(hr, ccr, _NT) - _dot(hi, cci, _NT) + dvec * u)


def _s5_scan(xr, xi, carry, lamd_r, lamd_i, lam8_r, lam8_i, reverse):
    cr, ci = carry
    T, G = xr.shape[0], S5_SUB
    sign = -1.0 if reverse else 1.0
    sub = lax.broadcasted_iota(jnp.int32, (T, 1), 0) % G

    def rotate(x, k):
        return pltpu.roll(x.reshape(T // G, G, x.shape[1]), k, axis=1).reshape(x.shape)

    for j in range(S5_ND):
        d = 1 << j
        keep = (sub < G - d) if reverse else (sub >= d)
        sr = jnp.where(keep, rotate(xr, G - d if reverse else d), 0.0)
        si = jnp.where(keep, rotate(xi, G - d if reverse else d), 0.0)
        ar, ai = lamd_r[j:j + 1], sign * lamd_i[j:j + 1]
        xr, xi = xr + ar * sr - ai * si, xi + ar * si + ai * sr
    if reverse:
        pr = jnp.concatenate([lam8_r[G - 1 - s:G - s] for s in range(G)], axis=0)
        pi = -jnp.concatenate([lam8_i[G - 1 - s:G - s] for s in range(G)], axis=0)
    else:
        pr, pi = lam8_r, lam8_i
    n = T // G
    rows_r, rows_i = [None] * n, [None] * n
    for i in (reversed(range(n)) if reverse else range(n)):
        gr_, gi_ = xr[i * G:(i + 1) * G], xi[i * G:(i + 1) * G]
        gr_, gi_ = gr_ + pr * cr - pi * ci, gi_ + pr * ci + pi * cr
        cr, ci = (gr_[:1], gi_[:1]) if reverse else (gr_[G - 1:], gi_[G - 1:])
        rows_r[i], rows_i[i] = gr_, gi_
    return jnp.concatenate(rows_r, axis=0), jnp.concatenate(rows_i, axis=0)


def _s5_tile_bwd(pid, carry, vals, douts, dcarry):
    u, lamd_r, lamd_i, lam8_r, lam8_i, bbr, bbi, ccr, cci, dvec = vals
    (dg,) = douts
    hr, hi = _s5_scan(_dot(u, bbr), _dot(u, bbi), carry, lamd_r, lamd_i, lam8_r, lam8_i, reverse=False)
    _, vjp = jax.vjp(_s5_readout, hr, hi, u, ccr, cci, dvec)
    dhr, dhi, du, dccr, dcci, ddvec = vjp(dg)
    Hr, Hi = _s5_scan(dhr, dhi, dcarry, lamd_r, lamd_i, lam8_r, lam8_i, reverse=True)
    _, vjp_in = jax.vjp(lambda u, bbr, bbi: (_dot(u, bbr), _dot(u, bbi)), u, bbr, bbi)
    du2, dbbr, dbbi = vjp_in((Hr, Hi))
    pr = jnp.concatenate([carry[0], hr[:-1]], axis=0)
    pi = jnp.concatenate([carry[1], hi[:-1]], axis=0)
    dlam_r = jnp.sum(Hr * pr + Hi * pi, axis=0, keepdims=True)
    dlam_i = jnp.sum(Hi * pr - Hr * pi, axis=0, keepdims=True)
    zrow = jnp.zeros((S5_ND - 1, dlam_r.shape[1]), f32)
    dlamd_r, dlamd_i = jnp.concatenate([dlam_r, zrow], axis=0), jnp.concatenate([dlam_i, zrow], axis=0)
    grads = (du + du2, dlamd_r, dlamd_i, jnp.zeros_like(lam8_r), jnp.zeros_like(lam8_i), dbbr, dbbi, dccr, dcci, ddvec)
    return (Hr[:1], Hi[:1]), grads


def _glu_tile(pid, g, glu, za, b):
    return (g * jax.nn.sigmoid(glu + b) * jax.nn.silu(za),)


def _attn_tile(pid, carry, q, k, v, qw, kw):
    n = pid[1]
    kp, vp = carry
    D, B = ATT_HEAD_DIM, ATT_BLOCK
    W = 2 * D
    nq, ncol = q.shape[0] // B, q.shape[1] // W
    r = lax.broadcasted_iota(jnp.int32, (B, 2 * B), 0)
    c = lax.broadcasted_iota(jnp.int32, (B, 2 * B), 1)
    diff = r + B - c
    band = (diff >= 0) & (diff <= B)
    band_first = band & ((c >= B) | (n > 0))
    low = lax.broadcasted_iota(jnp.int32, (1, W), 1) < D
    same_head = (lax.broadcasted_iota(jnp.int32, (W, W), 0) // D == lax.broadcasted_iota(jnp.int32, (W, W), 1) // D).astype(bf16)

    def hnorm(x, w):
        rows = x.shape[0]
        t = jnp.concatenate([x[:, j * W:(j + 1) * W] for j in range(ncol)], axis=0) if ncol > 1 else x
        ms = _dot_exact01(t * t, same_head) * (1.0 / D)
        t = t * lax.rsqrt(ms + RMS_EPS) * jnp.concatenate([w, w], axis=1)
        return [t[j * rows:(j + 1) * rows] for j in range(ncol)]

    qns, kns = hnorm(q, qw), hnorm(k, kw)
    out_cols, lse_cols, kn_cols = [], [], []
    for j in range(ncol):
        sl = slice(j * W, (j + 1) * W)
        qn, kn, vj = qns[j], kns[j], v[:, sl]
        kn_cols.append(kn[(nq - 1) * B:])
        outs, lses = [], []
        for b in range(nq):
            rows = slice(b * B, (b + 1) * B)
            prev = slice((b - 1) * B, b * B)
            kk = jnp.concatenate([kp[:, sl] if b == 0 else kn[prev], kn[rows]], axis=0)
            vv = jnp.concatenate([vp[:, sl] if b == 0 else vj[prev], vj[rows]], axis=0)
            o2, l2 = [], []
            for head_lanes in (low, ~low):
                s = _dot(jnp.where(head_lanes, qn[rows], 0.0), kk, _NT) * (D ** -0.5)
                s = jnp.where(band_first if b == 0 else band, s, -1e30)
                m = jnp.max(s, axis=-1, keepdims=True)
                p = jnp.exp(s - m)
                l = jnp.sum(p, axis=-1, keepdims=True)
                o2.append(_dot(p / l, vv))
                l2.append(m + jnp.log(l))
            outs.append(jnp.where(low, o2[0], o2[1]))
            lses.append(jnp.where(low, l2[0], l2[1]))
        out_cols.append(jnp.concatenate(outs, axis=0) if nq > 1 else outs[0])
        lse_cols.append(jnp.concatenate(lses, axis=0) if nq > 1 else lses[0])
    return ((jnp.concatenate(out_cols, axis=1), jnp.concatenate(lse_cols, axis=1)),
            (jnp.concatenate(kn_cols, axis=1), v[(nq - 1) * B:]))


def _combine_tile(pid, o1, l1, o2, l2, o3, l3, zb):
    m = jnp.maximum(jnp.maximum(l1, l2), l3)
    e1, e2, e3 = jnp.exp(l1 - m), jnp.exp(l2 - m), jnp.exp(l3 - m)
    y = (e1 * o1 + e2 * o2 + e3 * o3) / (e1 + e2 + e3)
    return (y * jax.nn.silu(zb),)


def _softplus(x):
    return jnp.maximum(x, 0.0) + jnp.log(1.0 + jnp.exp(-jnp.abs(x)))


def _ssd_tile(pid, carry, xbc, dt, z, conv_w, conv_b, dt_bias, a_log, dvec, norm_w):
    xprev, state = carry
    T, P, N = SSD_CHUNK, SSD_HEAD_DIM, SSD_STATE
    xx = jnp.concatenate([xprev, xbc], axis=0)
    conv = conv_b
    for k in range(SSD_CONV):
        off = 8 - (SSD_CONV - 1) + k
        conv = conv + conv_w[k:k + 1] * xx[off:off + T]
    xc = jax.nn.silu(conv)
    dtp = _softplus(dt + dt_bias)
    a_dt = dtp * (-jnp.exp(a_log))
    r = lax.broadcasted_iota(jnp.int32, (T, T), 0)
    c = lax.broadcasted_iota(jnp.int32, (T, T), 1)
    tri = r >= c
    trif = tri.astype(f32)
    hi = lax.Precision.HIGHEST
    a_cs = jnp.dot(trif, a_dt, precision=hi, preferred_element_type=f32)
    a_cs_t = lax.dot_general(a_dt, trif, (((0,), (1,)), ((), ())), precision=hi, preferred_element_type=f32)
    xs = xc[:, :SSD_WIDTH]
    acs_p = _spread_heads(a_cs, P)
    xdt = xs * _spread_heads(dtp, P)
    skip = _spread_heads(dvec, P)
    to_end = jnp.exp(acs_p[T - 1:T] - acs_p)
    low = lax.broadcasted_iota(jnp.int32, (1, 2 * P), 1) < P
    low_rows = lax.broadcasted_iota(jnp.int32, (2 * P, 1), 0) < P
    ys, states = [], []
    for j in range(SSD_HEADS // 2):
        g = 2 * j // (SSD_HEADS // SSD_GROUPS)
        if 2 * j % (SSD_HEADS // SSD_GROUPS) == 0:
            bg = xc[:, SSD_WIDTH + g * N:SSD_WIDTH + (g + 1) * N]
            cg = xc[:, SSD_WIDTH + SSD_GROUPS * N + g * N:SSD_WIDTH + SSD_GROUPS * N + (g + 1) * N]
            cb = _dot(cg, bg, _NT)
        lanes = slice(2 * j * P, 2 * (j + 1) * P)
        st = state[lanes, :]
        diag, last = [], []
        for h in (2 * j, 2 * j + 1):
            decay = jnp.exp(jnp.where(tri, a_cs[:, h:h + 1] - a_cs_t[h:h + 1, :], -1e30))
            diag.append(_dot(cb * decay, xdt[:, lanes]))
            last.append(jnp.exp(a_cs_t[h:h + 1, T - 1:T]))
        y = (jnp.where(low, diag[0], diag[1]) + _dot(cg, st, _NT) * jnp.exp(acs_p[:, lanes])
             + xs[:, lanes] * skip[:, lanes])
        ys.append(y)
        states.append(jnp.where(low_rows, last[0], last[1]) * st + _dot(xdt[:, lanes] * to_end[:, lanes], bg, _TN))
    y = jnp.concatenate(ys, axis=1) * jax.nn.silu(z)
    out = y * lax.rsqrt(jnp.mean(y * y, axis=-1, keepdims=True) + RMS_EPS) * norm_w
    return (out,), (xbc[T - 8:], jnp.concatenate(states, axis=0))


def _merge_tile(pid, pa, pb, pc, gates):
    d = pa.shape[1]
    g = jax.nn.sigmoid(gates)
    return (g[:, :d] * pa + g[:, d:2 * d] * pb + g[:, 2 * d:] * pc,)


def loss_and_grad(y, target, tm=512):
    S, D = y.shape
    nt = S // tm

    def body(y_ref, t_ref, dy_ref, l_ref, acc):
        i = pl.program_id(0)

        @pl.when(i == 0)
        def _():
            acc[...] = jnp.zeros_like(acc)

        diff = y_ref[...] - t_ref[...]
        dy_ref[...] = diff * (1.0 / D)
        acc[...] += jnp.sum((diff * diff).reshape(tm // 8, 8, D), axis=0)

        @pl.when(i == nt - 1)
        def _():
            l_ref[...] = jnp.broadcast_to(0.5 / D * jnp.sum(acc[...]), l_ref.shape)

    dy, l = pl.pallas_call(
        body, name="loss_head", grid=(nt,),
        in_specs=[pl.BlockSpec((tm, D), lambda i: (i, 0))] * 2,
        out_specs=[pl.BlockSpec((tm, D), lambda i: (i, 0)), pl.BlockSpec((8, 128), lambda i: (0, 0))],
        out_shape=[jax.ShapeDtypeStruct((S, D), f32), jax.ShapeDtypeStruct((8, 128), f32)],
        scratch_shapes=[pltpu.VMEM((8, D), f32)],
        compiler_params=_cparams(1))(y, target)
    return dy, l[0, 0]


def _row_tile(R, C, budget=1 << 20):
    best = R
    for t in range(8, R, 8):
        if R % t == 0 and t * C * 4 <= budget:
            best = t
    if best == R and R * C * 4 > budget:
        for t in range(8, R, 8):
            if R % t == 0:
                return t
    return best


def _as2d(t, lead=0):
    return t.reshape(t.shape[:lead] + (math.prod(t.shape[lead:-1]), t.shape[-1]))


def adamw_layers(name, w, slots, m, v):
    L, R, C = w.shape
    n = slots[0].shape[0]
    lanes = -(-C // 128) * 128
    tr = _row_tile(R, lanes * (n * L + 7), budget=10 << 20)

    def body(*refs):
        w_ref, m_ref, v_ref = refs[0], refs[1 + L], refs[2 + L]
        go_ref, d_ref, nm_ref, nv_ref = refs[3 + L:]
        layer = pl.program_id(0)
        gg = None
        for l in range(L):
            s = refs[1 + l][0].astype(f32)
            for j in range(1, n):
                s = s + refs[1 + l][j].astype(f32)
            gg = s if gg is None else jnp.where(layer == l, s, gg)
        go_ref[...] = gg
        nm = ADAM_B1 * m_ref[...] + (1.0 - ADAM_B1) * gg
        nv = ADAM_B2 * v_ref[...] + (1.0 - ADAM_B2) * jnp.square(gg)
        m_hat = nm / (1.0 - ADAM_B1 ** ADAM_STEP)
        v_hat = nv / (1.0 - ADAM_B2 ** ADAM_STEP)
        d_ref[...] = -ADAM_LR * (m_hat / (jnp.sqrt(v_hat) + ADAM_EPS) + ADAM_WD * w_ref[...])
        nm_ref[...] = nm
        nv_ref[...] = nv

    spec = pl.BlockSpec((None, tr, C), lambda l, i: (l, i, 0))
    slot_specs = [pl.BlockSpec((n, tr, C), lambda l, i, own=own: (0, jnp.where(l == own, i, 0), 0)) for own in range(L)]
    res = pl.pallas_call(
        body, name=name, grid=(L, R // tr),
        in_specs=[spec] + slot_specs + [spec, spec], out_specs=[spec] * 4,
        out_shape=[jax.ShapeDtypeStruct((L, R, C), f32)] * 4,
        compiler_params=_cparams(2))(w, *slots, m, v)
    return tuple(res)


def adamw(name, w, gslots, m, v):
    shape = w.shape
    n = gslots.shape[0]
    C = shape[-1]
    R = math.prod(shape[:-1])
    lanes = -(-C // 128) * 128
    tr = _row_tile(R, lanes * (n + 7), budget=10 << 20)

    def body(w_ref, g_ref, m_ref, v_ref, go_ref, d_ref, nm_ref, nv_ref):
        gg = g_ref[0].astype(f32)
        for s in range(1, n):
            gg = gg + g_ref[s].astype(f32)
        go_ref[...] = gg
        nm = ADAM_B1 * m_ref[...] + (1.0 - ADAM_B1) * gg
        nv = ADAM_B2 * v_ref[...] + (1.0 - ADAM_B2) * jnp.square(gg)
        m_hat = nm / (1.0 - ADAM_B1 ** ADAM_STEP)
        v_hat = nv / (1.0 - ADAM_B2 ** ADAM_STEP)
        d_ref[...] = -ADAM_LR * (m_hat / (jnp.sqrt(v_hat) + ADAM_EPS) + ADAM_WD * w_ref[...])
        nm_ref[...] = nm
        nv_ref[...] = nv

    spec = pl.BlockSpec((tr, C), lambda i: (i, 0))
    res = pl.pallas_call(
        body, name=name, grid=(R // tr,),
        in_specs=[spec, pl.BlockSpec((n, tr, C), lambda i: (0, i, 0)), spec, spec], out_specs=[spec] * 4,
        out_shape=[jax.ShapeDtypeStruct((R, C), f32)] * 4,
        compiler_params=_cparams(1))(w.reshape(R, C), gslots.reshape(n, R, C), m.reshape(R, C), v.reshape(R, C))
    return tuple(t.reshape(shape) for t in res)


PACK_ROWS = 256


def sum_slots(name, x):
    n, R, C = x.shape

    def body(x_ref, o_ref):
        acc = x_ref[0]
        for s in range(1, n):
            acc = acc + x_ref[s]
        o_ref[...] = acc

    return pl.pallas_call(
        body, name=name, grid=(R // PACK_ROWS,),
        in_specs=[pl.BlockSpec((n, PACK_ROWS, C), lambda i: (0, i, 0))],
        out_specs=pl.BlockSpec((PACK_ROWS, C), lambda i: (i, 0)),
        out_shape=jax.ShapeDtypeStruct((R, C), f32), compiler_params=_cparams(1))(x)


def _pack(parts):
    flat = jnp.concatenate([p.reshape(-1) for p in parts])
    unit = 128 * PACK_ROWS
    tot = -(-flat.shape[0] // unit) * unit
    return jnp.pad(flat, (0, tot - flat.shape[0])).reshape(tot // 128, 128)


def _unpack(buf, shapes):
    flat = buf.reshape(-1)
    out, off = [], 0
    for s in shapes:
        size = math.prod(s)
        out.append(flat[off:off + size].reshape(s))
        off += size
    return out


def _comm_sems(nt):
    return [pltpu.SemaphoreType.DMA((nt, N_DEV - 1)), pltpu.SemaphoreType.DMA((nt, N_DEV - 1)), pltpu.SemaphoreType.DMA((nt,))]


def exchange_side(srcs, modes):
    nt = len(srcs)
    slabs = []
    for s, mode in zip(srcs, modes):
        R, C = s.shape
        slabs.append({'all': (R, C), 'rows': (R // N_DEV, C), 'cols': (R, C // N_DEV)}[mode])

    def piece(ref, mode, slab, p):
        if mode == 'all':
            return ref
        if mode == 'rows':
            return ref.at[pl.ds(p * slab[0], slab[0]), :]
        return ref.at[:, pl.ds(p * slab[1], slab[1])]

    def copies(src_refs, out_refs, send_sems, recv_sems, local_sems):
        x, y, c = lax.axis_index("x"), lax.axis_index("y"), lax.axis_index("c")
        me = 4 * x + 2 * y + c
        out = []
        for k in (1, 2, 4, 3, 5, 6, 7):
            px = 1 - x if k & 4 else x
            py = 1 - y if k & 2 else y
            pc = 1 - c if k & 1 else c
            for t in range(nt):
                out.append(pltpu.make_async_remote_copy(
                    src_ref=piece(src_refs[t], modes[t], slabs[t], 4 * px + 2 * py + pc), dst_ref=out_refs[t].at[me],
                    send_sem=send_sems.at[t, k - 1], recv_sem=recv_sems.at[t, k - 1],
                    device_id=(px, py, pc), device_id_type=pl.DeviceIdType.MESH))
        for t in range(nt):
            out.append(pltpu.make_async_copy(piece(src_refs[t], modes[t], slabs[t], me), out_refs[t].at[me], local_sems.at[t]))
        return out

    def start(*refs):
        for cp in copies(*refs):
            cp.start()

    def finish(*refs):
        for cp in copies(*refs):
            cp.wait()

    out_shapes = [jax.ShapeDtypeStruct((N_DEV,) + sl, s.dtype) for s, sl in zip(srcs, slabs)]
    return Side(list(srcs), out_shapes, _comm_sems(nt), [(0.0, start), (1.0, finish)])


def gather_side(srcs):
    nt = len(srcs)

    def plan(src_refs, out_refs, send_sems, recv_sems, local_sems):
        x, y, c = lax.axis_index("x"), lax.axis_index("y"), lax.axis_index("c")
        me, sibling = (x, y, c), (x, y, 1 - c)
        chips = [(1 - x, y), (x, 1 - y), (1 - x, 1 - y)]

        def slot(t, dev):
            return out_refs[t].at[4 * dev[0] + 2 * dev[1] + dev[2]]

        def copy(t, k, block, to, src=None):
            return pltpu.make_async_remote_copy(
                src_ref=slot(t, block) if src is None else src, dst_ref=slot(t, block),
                send_sem=send_sems.at[t, k], recv_sem=recv_sems.at[t, k], device_id=to, device_id_type=pl.DeviceIdType.MESH)

        mine = [pltpu.make_async_copy(src_refs[t], slot(t, me), local_sems.at[t]) for t in range(nt)]
        first = []
        for t in range(nt):
            first.append(copy(t, 0, me, sibling, src=src_refs[t]))
            first += [copy(t, 1 + j, me, (*chip, c), src=src_refs[t]) for j, chip in enumerate(chips)]
        landed = [copy(t, 1 + j, (*chip, c), me) for j, chip in enumerate(chips) for t in range(nt)]
        passed = [copy(t, 4 + j, (*chip, c), sibling) for j, chip in enumerate(chips) for t in range(nt)]
        from_sibling = [copy(t, 0, sibling, me) for t in range(nt)]
        from_sibling += [copy(t, 4 + j, (*chip, 1 - c), me) for t in range(nt) for j, chip in enumerate(chips)]
        return mine, first, landed, passed, from_sibling

    def start(*refs):
        mine, first, _, _, _ = plan(*refs)
        for cp in mine + first:
            cp.start()

    def forward(*refs):
        _, _, landed, passed, _ = plan(*refs)
        for got, fwd in zip(landed, passed):
            got.wait_recv()
            fwd.start()

    def finish(*refs):
        mine, first, _, passed, from_sibling = plan(*refs)
        for cp in from_sibling:
            cp.wait_recv()
        for cp in first + passed:
            cp.wait_send()
        for cp in mine:
            cp.wait()

    out_shapes = [jax.ShapeDtypeStruct((N_DEV,) + s.shape, s.dtype) for s in srcs]
    return Side(list(srcs), out_shapes, _comm_sems(nt), [(0.0, start), (0.5, forward), (1.0, finish)])


def run_side(name, side):
    ns = len(side.arrs)

    def body(*refs):
        for _, phase in side.phases:
            phase(refs[:ns], refs[ns:ns + len(side.out_shapes)], *refs[ns + len(side.out_shapes):])

    return list(pl.pallas_call(
        body, name=name, in_specs=[_ANY] * ns, out_specs=[_ANY] * len(side.out_shapes),
        out_shape=list(side.out_shapes), scratch_shapes=list(side.sem_shapes))(*side.arrs))


def _relayout_w_in(w):
    offs = [0]
    for s in IN_SPLITS:
        offs.append(offs[-1] + s)
    p = [w[:, offs[i]:offs[i + 1]] for i in range(len(IN_SPLITS))]
    ua, za, q, k, v, zb, xbc, dt, zc, gates = p
    dtp = jnp.pad(dt, ((0, 0), (0, HPAD - dt.shape[1])))
    return (jnp.concatenate([ua, za, dtp], 1), w[:, offs[2]:offs[5]], jnp.concatenate([xbc, zb, zc], 1), gates)


def _pad_lanes(v, n=HPAD):
    return jnp.pad(v.reshape(1, -1), ((0, 0), (0, n - v.shape[-1])))


def _s5_prep_args(W):
    g2 = S5_GROUPS * S5_STATE
    w = g2 // S5_CHUNKS
    a_re, a_im = W['s5_a_re'].reshape(1, g2), W['s5_a_im'].reshape(1, g2)
    ls = W['s5_log_step'].reshape(1, S5_GROUPS)
    btr, bti = W['s5_b_re'].reshape(g2, S5_GROUP).T, W['s5_b_im'].reshape(g2, S5_GROUP).T
    ctr = W['s5_c_re'].transpose(1, 0, 2).reshape(S5_GROUP, g2)
    cti = W['s5_c_im'].transpose(1, 0, 2).reshape(S5_GROUP, g2)
    col = lambda a, rows: Arg(a, (rows, w), lambda o: (0, o), 'tile')
    return [col(a_re, 1), col(a_im, 1), _whole(ls, 'acc'), col(btr, S5_GROUP), col(bti, S5_GROUP), col(ctr, S5_GROUP), col(cti, S5_GROUP)]


def _s5_prep_outs():
    g2 = S5_GROUPS * S5_STATE
    w = g2 // S5_CHUNKS
    rows = (S5_ND, S5_ND, S5_SUB, S5_SUB, 128, 128, 128, 128)
    return [Out((r, g2), f32, (r, w), lambda o: (0, o)) for r in rows]


def _s5_args(A, prep, dvec, S):
    w = S5_GROUPS * S5_STATE // S5_CHUNKS
    args = [Arg(A, (S5_TILE, 128), lambda o, t: (t, o), 'tile', (S, S5_WIDTH), None, bf16)]
    for p in prep:
        args.append(Arg(p, (p.shape[0], w), lambda o, t: (0, o), 'acc0'))
    args.append(Arg(dvec, (1, 128), lambda o, t: (0, o), 'acc0'))
    return args


def _attn_args(QKV, g, r, qw, kw, S):
    L = S // r
    nq, rb = _attn_plan(r)
    block = (nq * ATT_BLOCK, rb * ATT_GW)
    gshape = (L, r * ATT_GW)
    gimap = lambda rho, n: (n, rho)
    if r == 1:
        mk = lambda j: Arg(QKV, block, lambda rho, n, j=j: (n, j), 'tile', gshape, gimap, bf16)
    else:
        def mk(j):
            view = QKV[:, j * ATT_GW:(j + 1) * ATT_GW].reshape(L, r * ATT_GW)
            return Arg(view, block, gimap, 'tile', None, None, bf16)
    return [mk(g), mk(3 + g), mk(6 + g), _whole(qw, 'acc'), _whole(kw, 'acc')]


def _attn_plan(r):
    return (4, 1) if r == 1 else (1, min(r, 4))


def _attn_grid(r, S):
    nq, rb = _attn_plan(r)
    return (r // rb, S // r // ATT_BLOCK // nq)


def _attn_carry(r):
    return ((ATT_BLOCK, _attn_plan(r)[1] * ATT_GW),) * 2


def _ssd_args(C, A, W, S):
    T = SSD_CHUNK
    return [Arg(C, (T, SSD_CONV_DIM), lambda o, t: (t, 0), 'tile', (S, SSD_CONV_DIM), None, bf16),
            Arg(A, (T, HPAD), lambda o, t: (t, 2 * S5_WIDTH // HPAD), 'tile', (S, HPAD), lambda o, t: (t, 0), bf16),
            Arg(C, (T, SSD_WIDTH), lambda o, t: (t, 2), 'tile', (S, SSD_WIDTH), lambda o, t: (t, 0), bf16),
            _whole(W['conv_w'], 'acc'), _whole(W['conv_b'].reshape(1, -1), 'acc'),
            _whole(_pad_lanes(W['dt_bias']), 'acc'), _whole(_pad_lanes(W['ssd_a_log']), 'acc'),
            _whole(_pad_lanes(W['ssd_d']), 'acc'), _whole(W['ssd_norm_w'].reshape(1, -1), 'acc')]


_SSD_CARRY = ((8, SSD_CONV_DIM), (SSD_WIDTH, SSD_STATE))
_S5_CARRY = ((1, 512), (1, 512))


def layer_fwd(li, x, W, side=None):
    S = x.shape[0]
    n = lambda s: f"l{li}_{s}"
    sv = {'x': x}
    (h,) = map_fwd(n("norm"), _rmsnorm_tile, (S // 512,), [_rows(x, 512), _whole(W['norm_w'].reshape(1, -1))],
                   [Out((S, D_MODEL), bf16, (512, D_MODEL), lambda i: (i, 0))])
    wA, wQ, wC, wG = W['w_in_pieces']
    A = matmul(n("in_a"), h, wA)
    QKV = matmul(n("in_qkv"), h, wQ)
    C = matmul(n("in_c"), h, wC)
    G = matmul(n("in_g"), h, wG)
    sv.update(h=h, A=A, QKV=QKV, C=C, G=G)

    prep = map_fwd(n("s5_prep"), _s5_prep_tile, (S5_CHUNKS,), _s5_prep_args(W), _s5_prep_outs())
    dvec = W['s5_d'].reshape(1, -1)
    (g,), s5_ck, *got = scan_fwd(n("s5_scan"), _s5_tile, (S5_CHUNKS, S // S5_TILE), _S5_CARRY, _s5_args(A, prep, dvec, S),
                                 [Out((S, S5_WIDTH), f32, (S5_TILE, 128), lambda o, t: (t, o))], side=side)
    glu = matmul(n("glu"), g, W['s5_glu_w'])
    glu_b = W['s5_glu_b'].reshape(1, -1)
    (ya,) = map_fwd(n("glu_gate"), _glu_tile, (S // 512,),
                    [_rows(g, 512), _rows(glu, 512), _rows(A, 512, col=1, width=S5_WIDTH), _whole(glu_b)],
                    [Out((S, S5_WIDTH), bf16, (512, S5_WIDTH), lambda i: (i, 0))])
    sv.update(prep=prep, g=g, glu=glu, ya=ya, s5_ck=s5_ck)

    qw, kw = W['q_norm_w'].reshape(1, -1), W['k_norm_w'].reshape(1, -1)
    att, att_ck = [], []
    for gi, (window, r) in enumerate(ATT_PAIRS):
        assert window // r == ATT_BLOCK and S % (r * ATT_BLOCK) == 0
        L = S // r
        nq, rb = _attn_plan(r)
        assert S // r // ATT_BLOCK % nq == 0
        spec = Out((L, r * ATT_GW), f32, (nq * ATT_BLOCK, rb * ATT_GW), lambda rho, nb: (nb, rho))
        (o, lse), ck = scan_fwd(n(f"attn{gi}"), _attn_tile, _attn_grid(r, S), _attn_carry(r), _attn_args(QKV, gi, r, qw, kw, S), [spec, spec])
        att += [o.reshape(S, ATT_GW), lse.reshape(S, ATT_GW)]
        att_ck.append(ck)
    (yb,) = map_fwd(n("combine"), _combine_tile, (S // 512,),
                    [_rows(t, 512) for t in att] + [_rows(C, 512, col=SSD_CONV_DIM // ATT_GW, width=ATT_GW)],
                    [Out((S, ATT_GW), bf16, (512, ATT_GW), lambda i: (i, 0))])
    sv.update(att=att, att_ck=att_ck, yb=yb)

    (yc,), ssd_ck = scan_fwd(n("ssd"), _ssd_tile, (1, S // SSD_CHUNK), _SSD_CARRY, _ssd_args(C, A, W, S),
                             [Out((S, SSD_WIDTH), bf16, (SSD_CHUNK, SSD_WIDTH), lambda o, t: (t, 0))])
    sv.update(yc=yc, ssd_ck=ssd_ck)

    pa = matmul(n("proj_a"), ya, W['proj_a'])
    pb = matmul(n("proj_b"), yb, W['proj_b'])
    pc = matmul(n("proj_c"), yc, W['proj_c'])
    (merged,) = map_fwd(n("merge"), _merge_tile, (S // 256,),
                        [_rows(pa, 256), _rows(pb, 256), _rows(pc, 256), _rows(G, 256)],
                        [Out((S, D_MODEL), bf16, (256, D_MODEL), lambda i: (i, 0))])
    out = matmul(n("w_out"), merged, W['w_out'], add=x)
    sv.update(pa=pa, pb=pb, pc=pc, merged=merged)
    return out, sv, (got[0] if got else None)


def layer_bwd(li, dout, sv, W, side=None, own_scatter=None):
    S = dout.shape[0]
    n = lambda s: f"l{li}_{s}"
    gr = {}
    x, A, QKV, C, G = sv['x'], sv['A'], sv['QKV'], sv['C'], sv['G']

    dmerged = matmul(n("d_merged"), dout, W['w_out'], 'nt')
    gr['w_out'] = wgrad(n("g_w_out"), sv['merged'], dout)
    margs = [_rows(sv['pa'], 256, gdtype=bf16), _rows(sv['pb'], 256, gdtype=bf16), _rows(sv['pc'], 256, gdtype=bf16),
             _rows(G, 256, gdtype=bf16)]
    dpa, dpb, dpc, dgates = map_bwd(n("merge_bwd"), _merge_tile, (S // 256,), margs, [_rows(dmerged, 256)], list(range(4)))
    dya = matmul(n("d_ya"), dpa, W['proj_a'], 'nt')
    dyb = matmul(n("d_yb"), dpb, W['proj_b'], 'nt')
    dyc = matmul(n("d_yc"), dpc, W['proj_c'], 'nt')
    gr['proj_a'] = wgrad(n("g_proj_a"), sv['ya'], dpa)
    gr['proj_b'] = wgrad(n("g_proj_b"), sv['yb'], dpb)
    gr['proj_c'] = wgrad(n("g_proj_c"), sv['yc'], dpc)

    glu_b = W['s5_glu_b'].reshape(1, -1)
    gargs = [_rows(sv['g'], 512), _rows(sv['glu'], 512, gdtype=bf16),
             _rows(A, 512, col=1, width=S5_WIDTH, gshape=(S, S5_WIDTH), gdtype=bf16), _whole(glu_b, 'acc')]
    dg_a, dglu, dza, dglu_b = map_bwd(n("glu_gate_bwd"), _glu_tile, (S // 512,), gargs, [_rows(dya, 512)], [0, 1, 2, 3])
    gr['s5_glu_b'] = dglu_b.reshape(-1)
    dg = matmul(n("d_g"), dglu, W['s5_glu_w'], 'nt', add=dg_a)
    gr['s5_glu_w'] = wgrad(n("g_glu_w"), sv['g'], dglu)
    dvec = W['s5_d'].reshape(1, -1)
    sargs = _s5_args(A, sv['prep'], dvec, S)
    res = scan_bwd(n("s5_scan_bwd"), _s5_tile, (S5_CHUNKS, S // S5_TILE), _S5_CARRY, sargs, sv['s5_ck'],
                   [Arg(dg, (S5_TILE, 128), lambda o, t: (t, o))], list(range(len(sargs))), bwd_fn=_s5_tile_bwd)
    dua, dprep, dd = res[0], res[1:9], res[9]
    gr['s5_d'] = dd.reshape(-1)
    pargs = _s5_prep_args(W)
    pouts = _s5_prep_outs()
    da_re, da_im, dls, dbtr, dbti, dctr, dcti = map_bwd(
        n("s5_prep_bwd"), _s5_prep_tile, (S5_CHUNKS,), pargs,
        [Arg(d, o.block, o.imap) for d, o in zip(dprep, pouts)], list(range(7)))
    gshape = (S5_GROUPS, S5_STATE)
    gr['s5_a_re'], gr['s5_a_im'] = da_re.reshape(gshape), da_im.reshape(gshape)
    gr['s5_log_step'] = dls.reshape(-1)
    gr['s5_b_re'] = dbtr.T.reshape(S5_GROUPS, S5_STATE, S5_GROUP)
    gr['s5_b_im'] = dbti.T.reshape(S5_GROUPS, S5_STATE, S5_GROUP)
    gr['s5_c_re'] = dctr.reshape(S5_GROUP, S5_GROUPS, S5_STATE).transpose(1, 0, 2)
    gr['s5_c_im'] = dcti.reshape(S5_GROUP, S5_GROUPS, S5_STATE).transpose(1, 0, 2)

    cargs = [_rows(t, 512) for t in sv['att']] + \
            [_rows(C, 512, col=SSD_CONV_DIM // ATT_GW, width=ATT_GW, gshape=(S, ATT_GW), gdtype=bf16)]
    cres = map_bwd(n("combine_bwd"), _combine_tile, (S // 512,), cargs, [_rows(dyb, 512)], list(range(7)))
    dzb = cres[6]
    qw, kw = W['q_norm_w'].reshape(1, -1), W['k_norm_w'].reshape(1, -1)
    dqs, dks, dvs = [], [], []
    dqw = dkw = None
    for gi, (window, r) in enumerate(ATT_PAIRS):
        L = S // r
        nq, rb = _attn_plan(r)
        dspec = lambda t: Arg(t.reshape(L, r * ATT_GW), (nq * ATT_BLOCK, rb * ATT_GW), lambda rho, nb: (nb, rho))
        dq, dk, dv, dqw_g, dkw_g = scan_bwd(n(f"attn{gi}_bwd"), _attn_tile, _attn_grid(r, S), _attn_carry(r),
                                            _attn_args(QKV, gi, r, qw, kw, S), sv['att_ck'][gi],
                                            [dspec(cres[2 * gi]), dspec(cres[2 * gi + 1])], [0, 1, 2, 3, 4])
        dqs.append(dq.reshape(S, ATT_GW))
        dks.append(dk.reshape(S, ATT_GW))
        dvs.append(dv.reshape(S, ATT_GW))
        dqw = dqw_g if dqw is None else dqw + dqw_g
        dkw = dkw_g if dkw is None else dkw + dkw_g
    gr['q_norm_w'], gr['k_norm_w'] = dqw.reshape(-1), dkw.reshape(-1)

    ssd_args = _ssd_args(C, A, W, S)
    sres = scan_bwd(n("ssd_bwd"), _ssd_tile, (1, S // SSD_CHUNK), _SSD_CARRY, ssd_args, sv['ssd_ck'],
                    [Arg(dyc, (SSD_CHUNK, SSD_WIDTH), lambda o, t: (t, 0))], list(range(9)), side=side)
    sres, got = sres if side else (sres, None)
    dxbc, ddt, dzc = sres[0], sres[1], sres[2]
    gr['conv_w'] = sres[3]
    gr['conv_b'] = sres[4].reshape(-1)
    gr['dt_bias'] = sres[5].reshape(-1)[:SSD_HEADS]
    gr['ssd_a_log'] = sres[6].reshape(-1)[:SSD_HEADS]
    gr['ssd_d'] = sres[7].reshape(-1)[:SSD_HEADS]
    gr['ssd_norm_w'] = sres[8].reshape(-1)

    dpieces = [jnp.concatenate([dua, dza, ddt], axis=1), jnp.concatenate(dqs + dks + dvs, axis=1),
               jnp.concatenate([dxbc, dzb, dzc], axis=1), dgates]
    gr['w_in'] = _unrelayout_w_in_grad([wgrad(n(f"g_w_in{j}"), sv['h'], dp) for j, dp in enumerate(dpieces)])
    dh = matmul_nt_sum(n("d_h"), dpieces, list(W['w_in_pieces']), side=own_scatter(gr) if own_scatter else None)
    dh, got_own = dh if own_scatter else (dh, None)
    nargs = [_rows(x, 512), _whole(W['norm_w'].reshape(1, -1), 'acc')]
    dx, dnw = map_bwd(n("norm_bwd"), _rmsnorm_tile, (S // 512,), nargs, [_rows(dh, 512)], [0, 1], add={0: _rows(dout, 512)})
    gr['norm_w'] = dnw.reshape(-1)
    return dx, gr, got, got_own


def _unrelayout_w_in_grad(pieces):
    gA, gQ, gC, gG = pieces
    uaza, dt = gA[:, :2 * S5_WIDTH], gA[:, 2 * S5_WIDTH:2 * S5_WIDTH + SSD_HEADS]
    xbc, zb, zc = gC[:, :SSD_CONV_DIM], gC[:, SSD_CONV_DIM:SSD_CONV_DIM + ATT_GW], gC[:, SSD_CONV_DIM + ATT_GW:]
    return jnp.concatenate([uaza, gQ, zb, xbc, dt, zc, gG], axis=1)


def kernel(x, norm_w, w_in, s5_a_re, s5_a_im, s5_log_step, s5_b_re, s5_b_im, s5_c_re, s5_c_im, s5_d, s5_glu_w, s5_glu_b, q_norm_w, k_norm_w, conv_w, conv_b, dt_bias, ssd_a_log, ssd_d, ssd_norm_w, proj_a, proj_b, proj_c, w_out, loss_target, m_norm_w, m_w_in, m_s5_a_re, m_s5_a_im, m_s5_log_step, m_s5_b_re, m_s5_b_im, m_s5_c_re, m_s5_c_im, m_s5_d, m_s5_glu_w, m_s5_glu_b, m_q_norm_w, m_k_norm_w, m_conv_w, m_conv_b, m_dt_bias, m_ssd_a_log, m_ssd_d, m_ssd_norm_w, m_proj_a, m_proj_b, m_proj_c, m_w_out, v_norm_w, v_w_in, v_s5_a_re, v_s5_a_im, v_s5_log_step, v_s5_b_re, v_s5_b_im, v_s5_c_re, v_s5_c_im, v_s5_d, v_s5_glu_w, v_s5_glu_b, v_q_norm_w, v_k_norm_w, v_conv_w, v_conv_b, v_dt_bias, v_ssd_a_log, v_ssd_d, v_ssd_norm_w, v_proj_a, v_proj_b, v_proj_c, v_w_out):
    args = dict(locals())
    w = {k: args[k] for k in WEIGHTS}
    m = {k: args['m_' + k] for k in WEIGHTS}
    v = {k: args['v_' + k] for k in WEIGHTS}
    depth = norm_w.shape[0]
    S = x.shape[1]
    xs = x.reshape(S, D_MODEL)
    tgt = loss_target.reshape(S, D_MODEL)

    others = [k for k in SHARDED if k != 'w_in']

    def weight_gather(li):
        return gather_side(list(_relayout_w_in(w['w_in'][li].astype(bf16))) + [w[k][li].astype(bf16) for k in others])

    def assemble(li, gathered):
        W = {k: w[k][li] for k in WEIGHTS if k not in SHARDED}
        W['w_in_pieces'] = [t.reshape(t.shape[0] * t.shape[1], t.shape[2]) for t in gathered[:4]]
        for k, t in zip(others, gathered[4:]):
            n_dev, R, C = t.shape
            W[k] = t.reshape(n_dev * R, C) if k in ROW_SHARDED else t.transpose(1, 0, 2).reshape(R, n_dev * C)
        return W

    layers = [assemble(0, run_side("gather_weights0", weight_gather(0)))]
    act, saved = xs, []
    for li in range(depth):
        act, sv, got = layer_fwd(li, act, layers[li], weight_gather(li + 1) if li + 1 < depth else None)
        saved.append(sv)
        if got is not None:
            layers.append(assemble(li + 1, got))
    dy, loss_local = loss_and_grad(act, tgt)
    loss = lax.psum(loss_local, ("x", "y", "c"))

    big = [k for k in SHARDED if k != 'conv_w']

    def grad_scatter(gr):
        return exchange_side([gr[k] for k in big], ['rows' if k in ROW_SHARDED else 'cols' for k in big])

    grads, slots = [None] * depth, [None] * depth
    for li in reversed(range(depth)):
        dy, grads[li], got, got_own = layer_bwd(li, dy, saved[li], layers[li], grad_scatter(grads[li + 1]) if li + 1 < depth else None,
                                                grad_scatter if li == 0 else None)
        if got is not None:
            slots[li + 1] = got
        if got_own is not None:
            slots[li] = got_own
    small_keys = [k for k in WEIGHTS if k not in SHARDED] + ['conv_w']
    stacked = [jnp.stack([grads[li][k] for li in range(depth)], axis=0) for k in small_keys]
    (small_slots,) = run_side("gather_small_grads", exchange_side([_pack(stacked)], ['all']))
    grad_x = dy.reshape(x.shape)
    result = {k: adamw_layers("adamw_" + k, w[k], [slots[li][j] for li in range(depth)], m[k], v[k]) for j, k in enumerate(big)}

    totals = _unpack(sum_slots("sum_small_grads", small_slots), [t.shape for t in stacked])
    for k, g in zip(small_keys, totals):
        if k == 'conv_w':
            width = w[k].shape[-1]
            me = 4 * lax.axis_index("x") + 2 * lax.axis_index("y") + lax.axis_index("c")
            g = lax.dynamic_slice_in_dim(g, me * width, width, axis=2)
        result[k] = adamw("adamw_" + k, w[k], g[None], m[k], v[k])

    return (loss, grad_x, *[result[k][0] for k in WEIGHTS], *[result[k][1] for k in WEIGHTS],
            *[result[k][2] for k in WEIGHTS], *[result[k][3] for k in WEIGHTS])
```

```python
import functools
import math
from typing import Any, NamedTuple

import jax
import jax.numpy as jnp
from jax import lax
from jax.experimental import pallas as pl
from jax.experimental.pallas import tpu as pltpu

f32 = jnp.float32
bf16 = jnp.bfloat16

N_DEV = 8
D_MODEL = 1024
RMS_EPS = 1e-6
S5_WIDTH = 512
S5_GROUPS = 32
S5_GROUP = 16
S5_STATE = 64
S5_TILE = 512
S5_SUB = 8
S5_ND = 3
S5_CHUNKS = 4
ATT_HEAD_DIM = 64
ATT_PAIRS = ((128, 1), (512, 4), (2048, 16))
ATT_HPG = 4
ATT_BLOCK = 128
ATT_GW = ATT_HPG * ATT_HEAD_DIM
ATT_WIDTH = 768
SSD_HEADS = 12
SSD_HEAD_DIM = 64
SSD_WIDTH = 768
SSD_STATE = 128
SSD_GROUPS = 2
SSD_CHUNK = 128
SSD_CONV = 4
SSD_CONV_DIM = 1280
HPAD = 128
IN_SPLITS = (512, 512, 768, 768, 768, 256, 1280, 12, 768, 3072)
ADAM_LR, ADAM_B1, ADAM_B2, ADAM_EPS, ADAM_WD, ADAM_STEP = 0.001, 0.9, 0.999, 1e-08, 0.01, 10
VMEM_LIMIT = 56 * 1024 * 1024

WEIGHTS = ['norm_w', 'w_in', 's5_a_re', 's5_a_im', 's5_log_step', 's5_b_re', 's5_b_im', 's5_c_re',
           's5_c_im', 's5_d', 's5_glu_w', 's5_glu_b', 'q_norm_w', 'k_norm_w', 'conv_w', 'conv_b',
           'dt_bias', 'ssd_a_log', 'ssd_d', 'ssd_norm_w', 'proj_a', 'proj_b', 'proj_c', 'w_out']
ROW_SHARDED = ('w_in', 's5_glu_w', 'w_out')
SHARDED = ROW_SHARDED + ('conv_w', 'proj_a', 'proj_b', 'proj_c')


class Arg(NamedTuple):
    arr: Any
    block: tuple
    imap: Any
    kind: str = 'const'
    gshape: Any = None
    gimap: Any = None
    gdtype: Any = None


class Out(NamedTuple):
    shape: tuple
    dtype: Any
    block: tuple
    imap: Any


def _cparams(n):
    return pltpu.CompilerParams(dimension_semantics=("arbitrary",) * n, vmem_limit_bytes=VMEM_LIMIT)


def _rows(a, tm, kind='tile', col=0, width=None, gshape=None, gcol=None, gdtype=None):
    width = a.shape[1] if width is None else width
    g = None if gshape is None else (lambda i, gc=(0 if gcol is None else gcol): (i, gc))
    return Arg(a, (tm, width), lambda i, c=col: (i, c), kind, gshape, g, gdtype)


def _whole(a, kind='const'):
    nd = a.ndim
    return Arg(a, a.shape, lambda *i, nd=nd: (0,) * nd, kind)


def map_fwd(name, fn, grid, args, outs):
    n_in = len(args)

    def body(*refs):
        pid = tuple(pl.program_id(a) for a in range(len(grid)))
        res = fn(pid, *[r[...] for r in refs[:n_in]])
        for o, r in zip(refs[n_in:], res):
            o[...] = r.astype(o.dtype)

    res = pl.pallas_call(
        body, name=name, grid=grid,
        in_specs=[pl.BlockSpec(a.block, a.imap) for a in args],
        out_specs=[pl.BlockSpec(o.block, o.imap) for o in outs],
        out_shape=[jax.ShapeDtypeStruct(o.shape, o.dtype) for o in outs],
        compiler_params=_cparams(len(grid)))(*[a.arr for a in args])
    return tuple(res)


def _grad_outs(args, wrt):
    outs = []
    for i in wrt:
        a = args[i]
        shape = a.arr.shape if a.gshape is None else a.gshape
        imap = a.imap if a.gimap is None else a.gimap
        outs.append(Out(shape, f32 if a.gdtype is None else a.gdtype, a.block, imap))
    return outs


def _store_grads(pid, args, wrt, grads, grefs, adds):
    first_all = functools.reduce(jnp.logical_and, [p == 0 for p in pid])
    first_in = functools.reduce(jnp.logical_and, [p == 0 for p in pid[1:]]) if len(pid) > 1 else first_all
    for j, i in enumerate(wrt):
        g = grads[j].astype(f32)
        ref = grefs[j]
        kind = args[i].kind
        if kind == 'tile':
            if j in adds:
                g = g + adds[j]
            ref[...] = g.astype(ref.dtype)
        else:
            first = first_all if kind == 'acc' else first_in

            @pl.when(first)
            def _(ref=ref):
                ref[...] = jnp.zeros_like(ref)

            ref[...] += g


def map_bwd(name, fn, grid, args, douts, wrt, add=None):
    add = add or {}
    n_in, n_d, n_add = len(args), len(douts), len(add)
    add_keys = sorted(add)
    gouts = _grad_outs(args, wrt)

    def body(*refs):
        pid = tuple(pl.program_id(a) for a in range(len(grid)))
        vals = [r[...] for r in refs[:n_in]]
        dvals = [r[...].astype(f32) for r in refs[n_in:n_in + n_d]]
        avals = {k: refs[n_in + n_d + j][...].astype(f32) for j, k in enumerate(add_keys)}
        grefs = refs[n_in + n_d + n_add:]

        def f(*w):
            full = list(vals)
            for i, x in zip(wrt, w):
                full[i] = x
            return tuple(fn(pid, *full))

        _, vjp = jax.vjp(f, *[vals[i] for i in wrt])
        grads = vjp(tuple(dvals))
        _store_grads(pid, args, wrt, grads, grefs, avals)

    ins = list(args) + list(douts) + [add[k] for k in add_keys]
    res = pl.pallas_call(
        body, name=name, grid=grid,
        in_specs=[pl.BlockSpec(a.block, a.imap) for a in ins],
        out_specs=[pl.BlockSpec(o.block, o.imap) for o in gouts],
        out_shape=[jax.ShapeDtypeStruct(o.shape, o.dtype) for o in gouts],
        compiler_params=_cparams(len(grid)))(*[a.arr for a in ins])
    return tuple(res)


class Side(NamedTuple):
    arrs: list
    out_shapes: list
    sem_shapes: list
    phases: list


def _run_side(side, step, total, src_refs, out_refs, sem_refs):
    for frac, phase in side.phases:
        @pl.when(step == int(round(frac * (total - 1))))
        def _(phase=phase):
            phase(src_refs, out_refs, *sem_refs)


_ANY = pl.BlockSpec(memory_space=pl.ANY)


def scan_fwd(name, fn, grid, carry_shapes, args, outs, side=None):
    no, nt = grid
    n_in, n_out, n_c = len(args), len(outs), len(carry_shapes)
    ns_in, ns_out = (len(side.arrs), len(side.out_shapes)) if side else (0, 0)
    cks = [Out((no, nt) + cs, f32, (None, None) + cs, lambda o, t, n=len(cs): (o, t) + (0,) * n) for cs in carry_shapes]

    def body(*refs):
        pid = (pl.program_id(0), pl.program_id(1))
        ins = refs[:n_in]
        sins = refs[n_in:n_in + ns_in]
        refs = refs[n_in + ns_in:]
        orefs = refs[:n_out]
        ckrefs = refs[n_out:n_out + n_c]
        souts = refs[n_out + n_c:n_out + n_c + ns_out]
        crefs = refs[n_out + n_c + ns_out:n_out + n_c + ns_out + n_c]
        if side:
            _run_side(side, pid[0] * nt + pid[1], no * nt, sins, souts, refs[n_out + n_c + ns_out + n_c:])

        @pl.when(pid[1] == 0)
        def _():
            for c in crefs:
                c[...] = jnp.zeros_like(c)

        carry = tuple(c[...] for c in crefs)
        for ck, c in zip(ckrefs, carry):
            ck[...] = c
        res, newc = fn(pid, carry, *[r[...] for r in ins])
        for o, r in zip(orefs, res):
            o[...] = r.astype(o.dtype)
        for c, v in zip(crefs, newc):
            c[...] = v

    allouts = list(outs) + cks
    res = pl.pallas_call(
        body, name=name, grid=grid,
        in_specs=[pl.BlockSpec(a.block, a.imap) for a in args] + [_ANY] * ns_in,
        out_specs=[pl.BlockSpec(o.block, o.imap) for o in allouts] + [_ANY] * ns_out,
        out_shape=[jax.ShapeDtypeStruct(o.shape, o.dtype) for o in allouts] + (list(side.out_shapes) if side else []),
        scratch_shapes=[pltpu.VMEM(cs, f32) for cs in carry_shapes] + (list(side.sem_shapes) if side else []),
        compiler_params=_cparams(2))(*[a.arr for a in args], *(side.arrs if side else []))
    if side:
        return tuple(res[:n_out]), tuple(res[n_out:n_out + n_c]), list(res[n_out + n_c:])
    return tuple(res[:n_out]), tuple(res[n_out:])


def scan_bwd(name, fn, grid, carry_shapes, args, ckpts, douts, wrt, bwd_fn=None, side=None):
    no, nt = grid
    n_in, n_d, n_c = len(args), len(douts), len(carry_shapes)
    ns_in, ns_out = (len(side.arrs), len(side.out_shapes)) if side else (0, 0)

    def rev(imap):
        return lambda o, t: imap(o, nt - 1 - t)

    rargs = [a._replace(imap=rev(a.imap), gimap=None if a.gimap is None else rev(a.gimap)) for a in args]
    rdouts = [a._replace(imap=rev(a.imap)) for a in douts]
    ckargs = [Arg(ck, (None, None) + cs, rev(lambda o, t, n=len(cs): (o, t) + (0,) * n)) for ck, cs in zip(ckpts, carry_shapes)]
    gouts = _grad_outs(rargs, wrt)

    def body(*refs):
        o, t = pl.program_id(0), pl.program_id(1)
        tt = nt - 1 - t
        vals = [r[...] for r in refs[:n_in]]
        dvals = [r[...].astype(f32) for r in refs[n_in:n_in + n_d]]
        carry = tuple(r[...] for r in refs[n_in + n_d:n_in + n_d + n_c])
        sins = refs[n_in + n_d + n_c:n_in + n_d + n_c + ns_in]
        refs = refs[n_in + n_d + n_c + ns_in:]
        grefs = refs[:len(wrt)]
        souts = refs[len(wrt):len(wrt) + ns_out]
        dcrefs = refs[len(wrt) + ns_out:len(wrt) + ns_out + n_c]
        if side:
            _run_side(side, o * nt + t, no * nt, sins, souts, refs[len(wrt) + ns_out + n_c:])

        @pl.when(t == 0)
        def _():
            for c in dcrefs:
                c[...] = jnp.zeros_like(c)

        def f(carry, *w):
            full = list(vals)
            for i, x in zip(wrt, w):
                full[i] = x
            res, newc = fn((o, tt), carry, *full)
            return tuple(res), tuple(newc)

        dcarry = tuple(c[...] for c in dcrefs)
        if bwd_fn is None:
            _, vjp = jax.vjp(f, carry, *[vals[i] for i in wrt])
            grads = vjp((tuple(dvals), dcarry))
            dcarry_in, grads = grads[0], grads[1:]
        else:
            dcarry_in, grads = bwd_fn((o, tt), carry, vals, dvals, dcarry)
        for c, g in zip(dcrefs, dcarry_in):
            c[...] = g
        _store_grads((o, t), rargs, wrt, grads, grefs, {})

    ins = rargs + rdouts + ckargs
    res = pl.pallas_call(
        body, name=name, grid=grid,
        in_specs=[pl.BlockSpec(a.block, a.imap) for a in ins] + [_ANY] * ns_in,
        out_specs=[pl.BlockSpec(g.block, g.imap) for g in gouts] + [_ANY] * ns_out,
        out_shape=[jax.ShapeDtypeStruct(g.shape, g.dtype) for g in gouts] + (list(side.out_shapes) if side else []),
        scratch_shapes=[pltpu.VMEM(cs, f32) for cs in carry_shapes] + (list(side.sem_shapes) if side else []),
        compiler_params=_cparams(2))(*[a.arr for a in ins], *(side.arrs if side else []))
    if side:
        return tuple(res[:len(gouts)]), list(res[len(gouts):])
    return tuple(res)


def _pick(dim, target):
    if dim <= target:
        return dim
    for t in range(target // 128 * 128, 127, -128):
        if dim % t == 0:
            return t
    return dim


def matmul(name, a, b, mode='nn', add=None, out_dtype=f32, tm=None, tn=1152, tk=None):
    if mode == 'tn':
        K, M = a.shape
    else:
        M, K = a.shape
    N = b.shape[0] if mode == 'nt' else b.shape[1]
    assert (b.shape[1] if mode == 'nt' else b.shape[0]) == K
    tm = (1024 if mode == 'tn' else 512) if tm is None else tm
    tk = (1024 if mode == 'tn' else 1152) if tk is None else tk
    tm, tn, tk = _pick(M, tm), _pick(N, tn), _pick(K, tk)
    nk = K // tk
    a_spec = pl.BlockSpec((tk, tm), lambda i, j, k: (k, i)) if mode == 'tn' else pl.BlockSpec((tm, tk), lambda i, j, k: (i, k))
    b_spec = pl.BlockSpec((tn, tk), lambda i, j, k: (j, k)) if mode == 'nt' else pl.BlockSpec((tk, tn), lambda i, j, k: (k, j))
    dims = {'nn': (((1,), (0,)), ((), ())), 'nt': (((1,), (1,)), ((), ())), 'tn': (((0,), (0,)), ((), ()))}[mode]
    has_add = add is not None

    def body(*refs):
        if has_add:
            a_ref, b_ref, add_ref, o_ref, acc = refs
        else:
            a_ref, b_ref, o_ref, acc = refs
        k = pl.program_id(2)
        prod = lax.dot_general(a_ref[...].astype(bf16), b_ref[...].astype(bf16), dims, preferred_element_type=f32)
        if nk == 1:
            o_ref[...] = (prod + add_ref[...].astype(f32) if has_add else prod).astype(o_ref.dtype)
            return

        @pl.when(k == 0)
        def _():
            acc[...] = add_ref[...].astype(f32) if has_add else jnp.zeros_like(acc)

        acc[...] += prod

        @pl.when(k == nk - 1)
        def _():
            o_ref[...] = acc[...].astype(o_ref.dtype)

    in_specs = [a_spec, b_spec] + ([pl.BlockSpec((tm, tn), lambda i, j, k: (i, j))] if has_add else [])
    ops = [a, b] + ([add] if has_add else [])
    return pl.pallas_call(
        body, name=name, grid=(M // tm, N // tn, nk), in_specs=in_specs,
        out_specs=pl.BlockSpec((tm, tn), lambda i, j, k: (i, j)),
        out_shape=jax.ShapeDtypeStruct((M, N), out_dtype),
        scratch_shapes=[pltpu.VMEM((tm, tn), f32)],
        compiler_params=pltpu.CompilerParams(dimension_semantics=("parallel", "parallel", "arbitrary"), vmem_limit_bytes=VMEM_LIMIT))(*ops)


def matmul_nt_sum(name, lhs, rhs, tm=1024, tk=768, side=None):
    M, N = lhs[0].shape[0], rhs[0].shape[0]
    tm = _pick(M, tm)
    tks = [_pick(a.shape[1], tk) for a in lhs]
    starts, total = [], 0
    for a, t in zip(lhs, tks):
        starts.append(total)
        total += a.shape[1] // t
    npc = len(lhs)
    ns_in, ns_out = (len(side.arrs), len(side.out_shapes)) if side else (0, 0)

    def body(*refs):
        a_refs, b_refs, sins = refs[:npc], refs[npc:2 * npc], refs[2 * npc:2 * npc + ns_in]
        refs = refs[2 * npc + ns_in:]
        o_ref, souts, acc = refs[0], refs[1:1 + ns_out], refs[1 + ns_out]
        k = pl.program_id(1)
        if side:
            _run_side(side, pl.program_id(0) * total + k, (M // tm) * total, sins, souts, refs[2 + ns_out:])

        @pl.when(k == 0)
        def _():
            acc[...] = jnp.zeros_like(acc)

        for p in range(npc):
            @pl.when((k >= starts[p]) & (k < starts[p] + lhs[p].shape[1] // tks[p]))
            def _(p=p):
                acc[...] += lax.dot_general(a_refs[p][...].astype(bf16), b_refs[p][...].astype(bf16), _NT, preferred_element_type=f32)

        @pl.when(k == total - 1)
        def _():
            o_ref[...] = acc[...]

    def kblock(p):
        return lambda k: jnp.clip(k - starts[p], 0, lhs[p].shape[1] // tks[p] - 1)

    in_specs = [pl.BlockSpec((tm, tks[p]), lambda i, k, kb=kblock(p): (i, kb(k))) for p in range(npc)]
    in_specs += [pl.BlockSpec((N, tks[p]), lambda i, k, kb=kblock(p): (0, kb(k))) for p in range(npc)]
    res = pl.pallas_call(
        body, name=name, grid=(M // tm, total), in_specs=in_specs + [_ANY] * ns_in,
        out_specs=[pl.BlockSpec((tm, N), lambda i, k: (i, 0))] + [_ANY] * ns_out,
        out_shape=[jax.ShapeDtypeStruct((M, N), f32)] + (list(side.out_shapes) if side else []),
        scratch_shapes=[pltpu.VMEM((tm, N), f32)] + (list(side.sem_shapes) if side else []),
        compiler_params=_cparams(2))(*lhs, *rhs, *(side.arrs if side else []))
    return (res[0], list(res[1:])) if side else res[0]


def wgrad(name, act, dout):
    return matmul(name, act, dout, 'tn', out_dtype=bf16)


def _dot(a, b, dims=(((1,), (0,)), ((), ()))):
    return lax.dot_general(a.astype(bf16), b.astype(bf16), dims, preferred_element_type=f32)


_NT = (((1,), (1,)), ((), ()))
_TN = (((0,), (0,)), ((), ()))


def _three_term_dot(v, sel, dims):
    hi = v.astype(bf16)
    rest = v - hi.astype(f32)
    mid = rest.astype(bf16)
    lo = (rest - mid.astype(f32)).astype(bf16)
    dot = lambda t: lax.dot_general(t, sel, dims, preferred_element_type=f32)
    return dot(hi) + dot(mid) + dot(lo)


@jax.custom_vjp
def _dot_exact01(v, sel):
    return _three_term_dot(v, sel, (((1,), (0,)), ((), ())))


def _dot_exact01_fwd(v, sel):
    return _dot_exact01(v, sel), sel


def _dot_exact01_bwd(sel, ct):
    return _three_term_dot(ct, sel, _NT), jnp.zeros_like(sel)


_dot_exact01.defvjp(_dot_exact01_fwd, _dot_exact01_bwd)


def _spread_heads(v, width):
    r = lax.broadcasted_iota(jnp.int32, (HPAD, SSD_HEADS * width), 0)
    c = lax.broadcasted_iota(jnp.int32, (HPAD, SSD_HEADS * width), 1)
    return _dot_exact01(v, (r == c // width).astype(bf16))


def _rmsnorm_tile(pid, x, w):
    return (x * lax.rsqrt(jnp.mean(x * x, axis=-1, keepdims=True) + RMS_EPS) * w,)


def _shift_rows(h, d, fill):
    pad = jnp.full((d, h.shape[1]), fill, f32)
    return jnp.concatenate([pad, h[:-d]], axis=0)


def _s5_prep_tile(pid, a_re, a_im, ls, btr, bti, ctr, cti):
    o = pid[0]
    w = a_re.shape[1]
    r = lax.broadcasted_iota(jnp.int32, (S5_GROUPS, w), 0)
    c = lax.broadcasted_iota(jnp.int32, (S5_GROUPS, w), 1)
    sel = (r == o * (w // S5_STATE) + c // S5_STATE).astype(f32)
    step = jnp.dot(jnp.exp(ls), sel, precision=lax.Precision.HIGHEST, preferred_element_type=f32)
    mag = jnp.exp(a_re * step)
    ang = a_im * step
    lr, li = mag * jnp.cos(ang), mag * jnp.sin(ang)
    nr, ni = lr - 1.0, li
    den = a_re * a_re + a_im * a_im
    fr = (nr * a_re + ni * a_im) / den
    fi = (ni * a_re - nr * a_im) / den
    bbr = fr * btr - fi * bti
    bbi = fr * bti + fi * btr
    reps = w // S5_STATE
    rr = lax.broadcasted_iota(jnp.int32, (reps * S5_GROUP, w), 0)
    cc = lax.broadcasted_iota(jnp.int32, (reps * S5_GROUP, w), 1)
    diag = (rr // S5_GROUP) == (cc // S5_STATE)

    def expand(m):
        return jnp.where(diag, jnp.concatenate([m] * reps, axis=0), 0.0)

    pr, pi = lr, li
    rows_r, rows_i = [pr], [pi]
    for _ in range(S5_ND - 1):
        pr, pi = pr * pr - pi * pi, 2.0 * pr * pi
        rows_r.append(pr)
        rows_i.append(pi)
    lamd_r, lamd_i = jnp.concatenate(rows_r, axis=0), jnp.concatenate(rows_i, axis=0)
    tr = jnp.broadcast_to(lr, (S5_SUB, w))
    ti = jnp.broadcast_to(li, (S5_SUB, w))
    for j in range(S5_ND):
        sr, si = _shift_rows(tr, 1 << j, 1.0), _shift_rows(ti, 1 << j, 0.0)
        tr, ti = tr * sr - ti * si, tr * si + ti * sr
    return lamd_r, lamd_i, tr, ti, expand(bbr), expand(bbi), expand(ctr), expand(cti)


def _s5_tile(pid, carry, u, lamd_r, lamd_i, lam8_r, lam8_i, bbr, bbi, ccr, cci, dvec):
    hr, hi = _s5_scan(_dot(u, bbr), _dot(u, bbi), carry, lamd_r, lamd_i, lam8_r, lam8_i, reverse=False)
    return (_s5_readout(hr, hi, u, ccr, cci, dvec),), (hr[-1:], hi[-1:])


def _s5_readout(hr, hi, u, ccr, cci, dvec):
    return jax.nn.gelu(_dot(hr, ccr, _NT) - _dot(hi, cci, _NT) + dvec * u)


def _s5_scan(xr, xi, carry, lamd_r, lamd_i, lam8_r, lam8_i, reverse):
    cr, ci = carry
    T, G = xr.shape[0], S5_SUB
    sign = -1.0 if reverse else 1.0
    sub = lax.broadcasted_iota(jnp.int32, (T, 1), 0) % G

    def rotate(x, k):
        return pltpu.roll(x.reshape(T // G, G, x.shape[1]), k, axis=1).reshape(x.shape)

    for j in range(S5_ND):
        d = 1 << j
        keep = (sub < G - d) if reverse else (sub >= d)
        sr = jnp.where(keep, rotate(xr, G - d if reverse else d), 0.0)
        si = jnp.where(keep, rotate(xi, G - d if reverse else d), 0.0)
        ar, ai = lamd_r[j:j + 1], sign * lamd_i[j:j + 1]
        xr, xi = xr + ar * sr - ai * si, xi + ar * si + ai * sr
    if reverse:
        pr = jnp.concatenate([lam8_r[G - 1 - s:G - s] for s in range(G)], axis=0)
        pi = -jnp.concatenate([lam8_i[G - 1 - s:G - s] for s in range(G)], axis=0)
    else:
        pr, pi = lam8_r, lam8_i
    n = T // G
    rows_r, rows_i = [None] * n, [None] * n
    for i in (reversed(range(n)) if reverse else range(n)):
        gr_, gi_ = xr[i * G:(i + 1) * G], xi[i * G:(i + 1) * G]
        gr_, gi_ = gr_ + pr * cr - pi * ci, gi_ + pr * ci + pi * cr
        cr, ci = (gr_[:1], gi_[:1]) if reverse else (gr_[G - 1:], gi_[G - 1:])
        rows_r[i], rows_i[i] = gr_, gi_
    return jnp.concatenate(rows_r, axis=0), jnp.concatenate(rows_i, axis=0)


def _s5_tile_bwd(pid, carry, vals, douts, dcarry):
    u, lamd_r, lamd_i, lam8_r, lam8_i, bbr, bbi, ccr, cci, dvec = vals
    (dg,) = douts
    hr, hi = _s5_scan(_dot(u, bbr), _dot(u, bbi), carry, lamd_r, lamd_i, lam8_r, lam8_i, reverse=False)
    _, vjp = jax.vjp(_s5_readout, hr, hi, u, ccr, cci, dvec)
    dhr, dhi, du, dccr, dcci, ddvec = vjp(dg)
    Hr, Hi = _s5_scan(dhr, dhi, dcarry, lamd_r, lamd_i, lam8_r, lam8_i, reverse=True)
    _, vjp_in = jax.vjp(lambda u, bbr, bbi: (_dot(u, bbr), _dot(u, bbi)), u, bbr, bbi)
    du2, dbbr, dbbi = vjp_in((Hr, Hi))
    pr = jnp.concatenate([carry[0], hr[:-1]], axis=0)
    pi = jnp.concatenate([carry[1], hi[:-1]], axis=0)
    dlam_r = jnp.sum(Hr * pr + Hi * pi, axis=0, keepdims=True)
    dlam_i = jnp.sum(Hi * pr - Hr * pi, axis=0, keepdims=True)
    zrow = jnp.zeros((S5_ND - 1, dlam_r.shape[1]), f32)
    dlamd_r, dlamd_i = jnp.concatenate([dlam_r, zrow], axis=0), jnp.concatenate([dlam_i, zrow], axis=0)
    grads = (du + du2, dlamd_r, dlamd_i, jnp.zeros_like(lam8_r), jnp.zeros_like(lam8_i), dbbr, dbbi, dccr, dcci, ddvec)
    return (Hr[:1], Hi[:1]), grads


def _glu_tile(pid, g, glu, za, b):
    return (g * jax.nn.sigmoid(glu + b) * jax.nn.silu(za),)


def _attn_tile(pid, carry, q, k, v, qw, kw):
    n = pid[1]
    kp, vp = carry
    D, B = ATT_HEAD_DIM, ATT_BLOCK
    W = 2 * D
    nq, ncol = q.shape[0] // B, q.shape[1] // W
    r = lax.broadcasted_iota(jnp.int32, (B, 2 * B), 0)
    c = lax.broadcasted_iota(jnp.int32, (B, 2 * B), 1)
    diff = r + B - c
    band = (diff >= 0) & (diff <= B)
    band_first = band & ((c >= B) | (n > 0))
    low = lax.broadcasted_iota(jnp.int32, (1, W), 1) < D
    same_head = (lax.broadcasted_iota(jnp.int32, (W, W), 0) // D == lax.broadcasted_iota(jnp.int32, (W, W), 1) // D).astype(bf16)

    def hnorm(x, w):
        rows = x.shape[0]
        t = jnp.concatenate([x[:, j * W:(j + 1) * W] for j in range(ncol)], axis=0) if ncol > 1 else x
        ms = _dot_exact01(t * t, same_head) * (1.0 / D)
        t = t * lax.rsqrt(ms + RMS_EPS) * jnp.concatenate([w, w], axis=1)
        return [t[j * rows:(j + 1) * rows] for j in range(ncol)]

    qns, kns = hnorm(q, qw), hnorm(k, kw)
    out_cols, lse_cols, kn_cols = [], [], []
    for j in range(ncol):
        sl = slice(j * W, (j + 1) * W)
        qn, kn, vj = qns[j], kns[j], v[:, sl]
        kn_cols.append(kn[(nq - 1) * B:])
        outs, lses = [], []
        for b in range(nq):
            rows = slice(b * B, (b + 1) * B)
            prev = slice((b - 1) * B, b * B)
            kk = jnp.concatenate([kp[:, sl] if b == 0 else kn[prev], kn[rows]], axis=0)
            vv = jnp.concatenate([vp[:, sl] if b == 0 else vj[prev], vj[rows]], axis=0)
            o2, l2 = [], []
            for head_lanes in (low, ~low):
                s = _dot(jnp.where(head_lanes, qn[rows], 0.0), kk, _NT) * (D ** -0.5)
                s = jnp.where(band_first if b == 0 else band, s, -1e30)
                m = jnp.max(s, axis=-1, keepdims=True)
                p = jnp.exp(s - m)
                l = jnp.sum(p, axis=-1, keepdims=True)
                o2.append(_dot(p / l, vv))
                l2.append(m + jnp.log(l))
            outs.append(jnp.where(low, o2[0], o2[1]))
            lses.append(jnp.where(low, l2[0], l2[1]))
        out_cols.append(jnp.concatenate(outs, axis=0) if nq > 1 else outs[0])
        lse_cols.append(jnp.concatenate(lses, axis=0) if nq > 1 else lses[0])
    return ((jnp.concatenate(out_cols, axis=1), jnp.concatenate(lse_cols, axis=1)),
            (jnp.concatenate(kn_cols, axis=1), v[(nq - 1) * B:]))


def _combine_tile(pid, o1, l1, o2, l2, o3, l3, zb):
    m = jnp.maximum(jnp.maximum(l1, l2), l3)
    e1, e2, e3 = jnp.exp(l1 - m), jnp.exp(l2 - m), jnp.exp(l3 - m)
    y = (e1 * o1 + e2 * o2 + e3 * o3) / (e1 + e2 + e3)
    return (y * jax.nn.silu(zb),)


def _softplus(x):
    return jnp.maximum(x, 0.0) + jnp.log(1.0 + jnp.exp(-jnp.abs(x)))


def _ssd_tile(pid, carry, xbc, dt, z, conv_w, conv_b, dt_bias, a_log, dvec, norm_w):
    xprev, state = carry
    T, P, N = SSD_CHUNK, SSD_HEAD_DIM, SSD_STATE
    xx = jnp.concatenate([xprev, xbc], axis=0)
    conv = conv_b
    for k in range(SSD_CONV):
        off = 8 - (SSD_CONV - 1) + k
        conv = conv + conv_w[k:k + 1] * xx[off:off + T]
    xc = jax.nn.silu(conv)
    dtp = _softplus(dt + dt_bias)
    a_dt = dtp * (-jnp.exp(a_log))
    r = lax.broadcasted_iota(jnp.int32, (T, T), 0)
    c = lax.broadcasted_iota(jnp.int32, (T, T), 1)
    tri = r >= c
    trif = tri.astype(f32)
    hi = lax.Precision.HIGHEST
    a_cs = jnp.dot(trif, a_dt, precision=hi, preferred_element_type=f32)
    a_cs_t = lax.dot_general(a_dt, trif, (((0,), (1,)), ((), ())), precision=hi, preferred_element_type=f32)
    xs = xc[:, :SSD_WIDTH]
    acs_p = _spread_heads(a_cs, P)
    xdt = xs * _spread_heads(dtp, P)
    skip = _spread_heads(dvec, P)
    to_end = jnp.exp(acs_p[T - 1:T] - acs_p)
    low = lax.broadcasted_iota(jnp.int32, (1, 2 * P), 1) < P
    low_rows = lax.broadcasted_iota(jnp.int32, (2 * P, 1), 0) < P
    ys, states = [], []
    for j in range(SSD_HEADS // 2):
        g = 2 * j // (SSD_HEADS // SSD_GROUPS)
        if 2 * j % (SSD_HEADS // SSD_GROUPS) == 0:
            bg = xc[:, SSD_WIDTH + g * N:SSD_WIDTH + (g + 1) * N]
            cg = xc[:, SSD_WIDTH + SSD_GROUPS * N + g * N:SSD_WIDTH + SSD_GROUPS * N + (g + 1) * N]
            cb = _dot(cg, bg, _NT)
        lanes = slice(2 * j * P, 2 * (j + 1) * P)
        st = state[lanes, :]
        diag, last = [], []
        for h in (2 * j, 2 * j + 1):
            decay = jnp.exp(jnp.where(tri, a_cs[:, h:h + 1] - a_cs_t[h:h + 1, :], -1e30))
            diag.append(_dot(cb * decay, xdt[:, lanes]))
            last.append(jnp.exp(a_cs_t[h:h + 1, T - 1:T]))
        y = (jnp.where(low, diag[0], diag[1]) + _dot(cg, st, _NT) * jnp.exp(acs_p[:, lanes])
             + xs[:, lanes] * skip[:, lanes])
        ys.append(y)
        states.append(jnp.where(low_rows, last[0], last[1]) * st + _dot(xdt[:, lanes] * to_end[:, lanes], bg, _TN))
    y = jnp.concatenate(ys, axis=1) * jax.nn.silu(z)
    out = y * lax.rsqrt(jnp.mean(y * y, axis=-1, keepdims=True) + RMS_EPS) * norm_w
    return (out,), (xbc[T - 8:], jnp.concatenate(states, axis=0))


def _merge_tile(pid, pa, pb, pc, gates):
    d = pa.shape[1]
    g = jax.nn.sigmoid(gates)
    return (g[:, :d] * pa + g[:, d:2 * d] * pb + g[:, 2 * d:] * pc,)


def loss_and_grad(y, target, tm=512):
    S, D = y.shape
    nt = S // tm

    def body(y_ref, t_ref, dy_ref, l_ref, acc):
        i = pl.program_id(0)

        @pl.when(i == 0)
        def _():
            acc[...] = jnp.zeros_like(acc)

        diff = y_ref[...] - t_ref[...]
        dy_ref[...] = diff * (1.0 / D)
        acc[...] += jnp.sum((diff * diff).reshape(tm // 8, 8, D), axis=0)

        @pl.when(i == nt - 1)
        def _():
            l_ref[...] = jnp.broadcast_to(0.5 / D * jnp.sum(acc[...]), l_ref.shape)

    dy, l = pl.pallas_call(
        body, name="loss_head", grid=(nt,),
        in_specs=[pl.BlockSpec((tm, D), lambda i: (i, 0))] * 2,
        out_specs=[pl.BlockSpec((tm, D), lambda i: (i, 0)), pl.BlockSpec((8, 128), lambda i: (0, 0))],
        out_shape=[jax.ShapeDtypeStruct((S, D), f32), jax.ShapeDtypeStruct((8, 128), f32)],
        scratch_shapes=[pltpu.VMEM((8, D), f32)],
        compiler_params=_cparams(1))(y, target)
    return dy, l[0, 0]


def _row_tile(R, C, budget=1 << 20):
    best = R
    for t in range(8, R, 8):
        if R % t == 0 and t * C * 4 <= budget:
            best = t
    if best == R and R * C * 4 > budget:
        for t in range(8, R, 8):
            if R % t == 0:
                return t
    return best


def _as2d(t, lead=0):
    return t.reshape(t.shape[:lead] + (math.prod(t.shape[lead:-1]), t.shape[-1]))


def adamw_layers(name, w, slots, m, v):
    L, R, C = w.shape
    n = slots[0].shape[0]
    lanes = -(-C // 128) * 128
    tr = _row_tile(R, lanes * (n * L + 7), budget=10 << 20)

    def body(*refs):
        w_ref, m_ref, v_ref = refs[0], refs[1 + L], refs[2 + L]
        go_ref, d_ref, nm_ref, nv_ref = refs[3 + L:]
        layer = pl.program_id(0)
        gg = None
        for l in range(L):
            s = refs[1 + l][0].astype(f32)
            for j in range(1, n):
                s = s + refs[1 + l][j].astype(f32)
            gg = s if gg is None else jnp.where(layer == l, s, gg)
        go_ref[...] = gg
        nm = ADAM_B1 * m_ref[...] + (1.0 - ADAM_B1) * gg
        nv = ADAM_B2 * v_ref[...] + (1.0 - ADAM_B2) * jnp.square(gg)
        m_hat = nm / (1.0 - ADAM_B1 ** ADAM_STEP)
        v_hat = nv / (1.0 - ADAM_B2 ** ADAM_STEP)
        d_ref[...] = -ADAM_LR * (m_hat / (jnp.sqrt(v_hat) + ADAM_EPS) + ADAM_WD * w_ref[...])
        nm_ref[...] = nm
        nv_ref[...] = nv

    spec = pl.BlockSpec((None, tr, C), lambda l, i: (l, i, 0))
    slot_specs = [pl.BlockSpec((n, tr, C), lambda l, i, own=own: (0, jnp.where(l == own, i, 0), 0)) for own in range(L)]
    res = pl.pallas_call(
        body, name=name, grid=(L, R // tr),
        in_specs=[spec] + slot_specs + [spec, spec], out_specs=[spec] * 4,
        out_shape=[jax.ShapeDtypeStruct((L, R, C), f32)] * 4,
        compiler_params=_cparams(2))(w, *slots, m, v)
    return tuple(res)


def adamw(name, w, gslots, m, v):
    shape = w.shape
    n = gslots.shape[0]
    C = shape[-1]
    R = math.prod(shape[:-1])
    lanes = -(-C // 128) * 128
    tr = _row_tile(R, lanes * (n + 7), budget=10 << 20)

    def body(w_ref, g_ref, m_ref, v_ref, go_ref, d_ref, nm_ref, nv_ref):
        gg = g_ref[0].astype(f32)
        for s in range(1, n):
            gg = gg + g_ref[s].astype(f32)
        go_ref[...] = gg
        nm = ADAM_B1 * m_ref[...] + (1.0 - ADAM_B1) * gg
        nv = ADAM_B2 * v_ref[...] + (1.0 - ADAM_B2) * jnp.square(gg)
        m_hat = nm / (1.0 - ADAM_B1 ** ADAM_STEP)
        v_hat = nv / (1.0 - ADAM_B2 ** ADAM_STEP)
        d_ref[...] = -ADAM_LR * (m_hat / (jnp.sqrt(v_hat) + ADAM_EPS) + ADAM_WD * w_ref[...])
        nm_ref[...] = nm
        nv_ref[...] = nv

    spec = pl.BlockSpec((tr, C), lambda i: (i, 0))
    res = pl.pallas_call(
        body, name=name, grid=(R // tr,),
        in_specs=[spec, pl.BlockSpec((n, tr, C), lambda i: (0, i, 0)), spec, spec], out_specs=[spec] * 4,
        out_shape=[jax.ShapeDtypeStruct((R, C), f32)] * 4,
        compiler_params=_cparams(1))(w.reshape(R, C), gslots.reshape(n, R, C), m.reshape(R, C), v.reshape(R, C))
    return tuple(t.reshape(shape) for t in res)


PACK_ROWS = 256


def sum_slots(name, x):
    n, R, C = x.shape

    def body(x_ref, o_ref):
        acc = x_ref[0]
        for s in range(1, n):
            acc = acc + x_ref[s]
        o_ref[...] = acc

    return pl.pallas_call(
        body, name=name, grid=(R // PACK_ROWS,),
        in_specs=[pl.BlockSpec((n, PACK_ROWS, C), lambda i: (0, i, 0))],
        out_specs=pl.BlockSpec((PACK_ROWS, C), lambda i: (i, 0)),
        out_shape=jax.ShapeDtypeStruct((R, C), f32), compiler_params=_cparams(1))(x)


def _pack(parts):
    flat = jnp.concatenate([p.reshape(-1) for p in parts])
    unit = 128 * PACK_ROWS
    tot = -(-flat.shape[0] // unit) * unit
    return jnp.pad(flat, (0, tot - flat.shape[0])).reshape(tot // 128, 128)


def _unpack(buf, shapes):
    flat = buf.reshape(-1)
    out, off = [], 0
    for s in shapes:
        size = math.prod(s)
        out.append(flat[off:off + size].reshape(s))
        off += size
    return out


def _comm_sems(nt):
    return [pltpu.SemaphoreType.DMA((nt, N_DEV - 1)), pltpu.SemaphoreType.DMA((nt, N_DEV - 1)), pltpu.SemaphoreType.DMA((nt,))]


def exchange_side(srcs, modes):
    nt = len(srcs)
    slabs = []
    for s, mode in zip(srcs, modes):
        R, C = s.shape
        slabs.append({'all': (R, C), 'rows': (R // N_DEV, C), 'cols': (R, C // N_DEV)}[mode])

    def piece(ref, mode, slab, p):
        if mode == 'all':
            return ref
        if mode == 'rows':
            return ref.at[pl.ds(p * slab[0], slab[0]), :]
        return ref.at[:, pl.ds(p * slab[1], slab[1])]

    def copies(src_refs, out_refs, send_sems, recv_sems, local_sems):
        x, y, c = lax.axis_index("x"), lax.axis_index("y"), lax.axis_index("c")
        me = 4 * x + 2 * y + c
        out = []
        for k in (1, 2, 4, 3, 5, 6, 7):
            px = 1 - x if k & 4 else x
            py = 1 - y if k & 2 else y
            pc = 1 - c if k & 1 else c
            for t in range(nt):
                out.append(pltpu.make_async_remote_copy(
                    src_ref=piece(src_refs[t], modes[t], slabs[t], 4 * px + 2 * py + pc), dst_ref=out_refs[t].at[me],
                    send_sem=send_sems.at[t, k - 1], recv_sem=recv_sems.at[t, k - 1],
                    device_id=(px, py, pc), device_id_type=pl.DeviceIdType.MESH))
        for t in range(nt):
            out.append(pltpu.make_async_copy(piece(src_refs[t], modes[t], slabs[t], me), out_refs[t].at[me], local_sems.at[t]))
        return out

    def start(*refs):
        for cp in copies(*refs):
            cp.start()

    def finish(*refs):
        for cp in copies(*refs):
            cp.wait()

    out_shapes = [jax.ShapeDtypeStruct((N_DEV,) + sl, s.dtype) for s, sl in zip(srcs, slabs)]
    return Side(list(srcs), out_shapes, _comm_sems(nt), [(0.0, start), (1.0, finish)])


def gather_side(srcs):
    nt = len(srcs)

    def plan(src_refs, out_refs, send_sems, recv_sems, local_sems):
        x, y, c = lax.axis_index("x"), lax.axis_index("y"), lax.axis_index("c")
        me, sibling = (x, y, c), (x, y, 1 - c)
        chips = [(1 - x, y), (x, 1 - y), (1 - x, 1 - y)]

        def slot(t, dev):
            return out_refs[t].at[4 * dev[0] + 2 * dev[1] + dev[2]]

        def copy(t, k, block, to, src=None):
            return pltpu.make_async_remote_copy(
                src_ref=slot(t, block) if src is None else src, dst_ref=slot(t, block),
                send_sem=send_sems.at[t, k], recv_sem=recv_sems.at[t, k], device_id=to, device_id_type=pl.DeviceIdType.MESH)

        mine = [pltpu.make_async_copy(src_refs[t], slot(t, me), local_sems.at[t]) for t in range(nt)]
        first = []
        for t in range(nt):
            first.append(copy(t, 0, me, sibling, src=src_refs[t]))
            first += [copy(t, 1 + j, me, (*chip, c), src=src_refs[t]) for j, chip in enumerate(chips)]
        landed = [copy(t, 1 + j, (*chip, c), me) for j, chip in enumerate(chips) for t in range(nt)]
        passed = [copy(t, 4 + j, (*chip, c), sibling) for j, chip in enumerate(chips) for t in range(nt)]
        from_sibling = [copy(t, 0, sibling, me) for t in range(nt)]
        from_sibling += [copy(t, 4 + j, (*chip, 1 - c), me) for t in range(nt) for j, chip in enumerate(chips)]
        return mine, first, landed, passed, from_sibling

    def start(*refs):
        mine, first, _, _, _ = plan(*refs)
        for cp in mine + first:
            cp.start()

    def forward(*refs):
        _, _, landed, passed, _ = plan(*refs)
        for got, fwd in zip(landed, passed):
            got.wait_recv()
            fwd.start()

    def finish(*refs):
        mine, first, _, passed, from_sibling = plan(*refs)
        for cp in from_sibling:
            cp.wait_recv()
        for cp in first + passed:
            cp.wait_send()
        for cp in mine:
            cp.wait()

    out_shapes = [jax.ShapeDtypeStruct((N_DEV,) + s.shape, s.dtype) for s in srcs]
    return Side(list(srcs), out_shapes, _comm_sems(nt), [(0.0, start), (0.5, forward), (1.0, finish)])


def run_side(name, side):
    ns = len(side.arrs)

    def body(*refs):
        for _, phase in side.phases:
            phase(refs[:ns], refs[ns:ns + len(side.out_shapes)], *refs[ns + len(side.out_shapes):])

    return list(pl.pallas_call(
        body, name=name, in_specs=[_ANY] * ns, out_specs=[_ANY] * len(side.out_shapes),
        out_shape=list(side.out_shapes), scratch_shapes=list(side.sem_shapes))(*side.arrs))


def _relayout_w_in(w):
    offs = [0]
    for s in IN_SPLITS:
        offs.append(offs[-1] + s)
    p = [w[:, offs[i]:offs[i + 1]] for i in range(len(IN_SPLITS))]
    ua, za, q, k, v, zb, xbc, dt, zc, gates = p
    dtp = jnp.pad(dt, ((0, 0), (0, HPAD - dt.shape[1])))
    return (jnp.concatenate([ua, za, dtp], 1), w[:, offs[2]:offs[5]], jnp.concatenate([xbc, zb, zc], 1), gates)


def _pad_lanes(v, n=HPAD):
    return jnp.pad(v.reshape(1, -1), ((0, 0), (0, n - v.shape[-1])))


def _s5_prep_args(W):
    g2 = S5_GROUPS * S5_STATE
    w = g2 // S5_CHUNKS
    a_re, a_im = W['s5_a_re'].reshape(1, g2), W['s5_a_im'].reshape(1, g2)
    ls = W['s5_log_step'].reshape(1, S5_GROUPS)
    btr, bti = W['s5_b_re'].reshape(g2, S5_GROUP).T, W['s5_b_im'].reshape(g2, S5_GROUP).T
    ctr = W['s5_c_re'].transpose(1, 0, 2).reshape(S5_GROUP, g2)
    cti = W['s5_c_im'].transpose(1, 0, 2).reshape(S5_GROUP, g2)
    col = lambda a, rows: Arg(a, (rows, w), lambda o: (0, o), 'tile')
    return [col(a_re, 1), col(a_im, 1), _whole(ls, 'acc'), col(btr, S5_GROUP), col(bti, S5_GROUP), col(ctr, S5_GROUP), col(cti, S5_GROUP)]


def _s5_prep_outs():
    g2 = S5_GROUPS * S5_STATE
    w = g2 // S5_CHUNKS
    rows = (S5_ND, S5_ND, S5_SUB, S5_SUB, 128, 128, 128, 128)
    return [Out((r, g2), f32, (r, w), lambda o: (0, o)) for r in rows]


def _s5_args(A, prep, dvec, S):
    w = S5_GROUPS * S5_STATE // S5_CHUNKS
    args = [Arg(A, (S5_TILE, 128), lambda o, t: (t, o), 'tile', (S, S5_WIDTH), None, bf16)]
    for p in prep:
        args.append(Arg(p, (p.shape[0], w), lambda o, t: (0, o), 'acc0'))
    args.append(Arg(dvec, (1, 128), lambda o, t: (0, o), 'acc0'))
    return args


def _attn_args(QKV, g, r, qw, kw, S):
    L = S // r
    nq, rb = _attn_plan(r)
    block = (nq * ATT_BLOCK, rb * ATT_GW)
    gshape = (L, r * ATT_GW)
    gimap = lambda rho, n: (n, rho)
    if r == 1:
        mk = lambda j: Arg(QKV, block, lambda rho, n, j=j: (n, j), 'tile', gshape, gimap, bf16)
    else:
        def mk(j):
            view = QKV[:, j * ATT_GW:(j + 1) * ATT_GW].reshape(L, r * ATT_GW)
            return Arg(view, block, gimap, 'tile', None, None, bf16)
    return [mk(g), mk(3 + g), mk(6 + g), _whole(qw, 'acc'), _whole(kw, 'acc')]


def _attn_plan(r):
    return (4, 1) if r == 1 else (1, min(r, 4))


def _attn_grid(r, S):
    nq, rb = _attn_plan(r)
    return (r // rb, S // r // ATT_BLOCK // nq)


def _attn_carry(r):
    return ((ATT_BLOCK, _attn_plan(r)[1] * ATT_GW),) * 2


def _ssd_args(C, A, W, S):
    T = SSD_CHUNK
    return [Arg(C, (T, SSD_CONV_DIM), lambda o, t: (t, 0), 'tile', (S, SSD_CONV_DIM), None, bf16),
            Arg(A, (T, HPAD), lambda o, t: (t, 2 * S5_WIDTH // HPAD), 'tile', (S, HPAD), lambda o, t: (t, 0), bf16),
            Arg(C, (T, SSD_WIDTH), lambda o, t: (t, 2), 'tile', (S, SSD_WIDTH), lambda o, t: (t, 0), bf16),
            _whole(W['conv_w'], 'acc'), _whole(W['conv_b'].reshape(1, -1), 'acc'),
            _whole(_pad_lanes(W['dt_bias']), 'acc'), _whole(_pad_lanes(W['ssd_a_log']), 'acc'),
            _whole(_pad_lanes(W['ssd_d']), 'acc'), _whole(W['ssd_norm_w'].reshape(1, -1), 'acc')]


_SSD_CARRY = ((8, SSD_CONV_DIM), (SSD_WIDTH, SSD_STATE))
_S5_CARRY = ((1, 512), (1, 512))


def layer_fwd(li, x, W, side=None):
    S = x.shape[0]
    n = lambda s: f"l{li}_{s}"
    sv = {'x': x}
    (h,) = map_fwd(n("norm"), _rmsnorm_tile, (S // 512,), [_rows(x, 512), _whole(W['norm_w'].reshape(1, -1))],
                   [Out((S, D_MODEL), bf16, (512, D_MODEL), lambda i: (i, 0))])
    wA, wQ, wC, wG = W['w_in_pieces']
    A = matmul(n("in_a"), h, wA)
    QKV = matmul(n("in_qkv"), h, wQ)
    C = matmul(n("in_c"), h, wC)
    G = matmul(n("in_g"), h, wG)
    sv.update(h=h, A=A, QKV=QKV, C=C, G=G)

    prep = map_fwd(n("s5_prep"), _s5_prep_tile, (S5_CHUNKS,), _s5_prep_args(W), _s5_prep_outs())
    dvec = W['s5_d'].reshape(1, -1)
    (g,), s5_ck, *got = scan_fwd(n("s5_scan"), _s5_tile, (S5_CHUNKS, S // S5_TILE), _S5_CARRY, _s5_args(A, prep, dvec, S),
                                 [Out((S, S5_WIDTH), f32, (S5_TILE, 128), lambda o, t: (t, o))], side=side)
    glu = matmul(n("glu"), g, W['s5_glu_w'])
    glu_b = W['s5_glu_b'].reshape(1, -1)
    (ya,) = map_fwd(n("glu_gate"), _glu_tile, (S // 512,),
                    [_rows(g, 512), _rows(glu, 512), _rows(A, 512, col=1, width=S5_WIDTH), _whole(glu_b)],
                    [Out((S, S5_WIDTH), bf16, (512, S5_WIDTH), lambda i: (i, 0))])
    sv.update(prep=prep, g=g, glu=glu, ya=ya, s5_ck=s5_ck)

    qw, kw = W['q_norm_w'].reshape(1, -1), W['k_norm_w'].reshape(1, -1)
    att, att_ck = [], []
    for gi, (window, r) in enumerate(ATT_PAIRS):
        assert window // r == ATT_BLOCK and S % (r * ATT_BLOCK) == 0
        L = S // r
        nq, rb = _attn_plan(r)
        assert S // r // ATT_BLOCK % nq == 0
        spec = Out((L, r * ATT_GW), f32, (nq * ATT_BLOCK, rb * ATT_GW), lambda rho, nb: (nb, rho))
        (o, lse), ck = scan_fwd(n(f"attn{gi}"), _attn_tile, _attn_grid(r, S), _attn_carry(r), _attn_args(QKV, gi, r, qw, kw, S), [spec, spec])
        att += [o.reshape(S, ATT_GW), lse.reshape(S, ATT_GW)]
        att_ck.append(ck)
    (yb,) = map_fwd(n("combine"), _combine_tile, (S // 512,),
                    [_rows(t, 512) for t in att] + [_rows(C, 512, col=SSD_CONV_DIM // ATT_GW, width=ATT_GW)],
                    [Out((S, ATT_GW), bf16, (512, ATT_GW), lambda i: (i, 0))])
    sv.update(att=att, att_ck=att_ck, yb=yb)

    (yc,), ssd_ck = scan_fwd(n("ssd"), _ssd_tile, (1, S // SSD_CHUNK), _SSD_CARRY, _ssd_args(C, A, W, S),
                             [Out((S, SSD_WIDTH), bf16, (SSD_CHUNK, SSD_WIDTH), lambda o, t: (t, 0))])
    sv.update(yc=yc, ssd_ck=ssd_ck)

    pa = matmul(n("proj_a"), ya, W['proj_a'])
    pb = matmul(n("proj_b"), yb, W['proj_b'])
    pc = matmul(n("proj_c"), yc, W['proj_c'])
    (merged,) = map_fwd(n("merge"), _merge_tile, (S // 256,),
                        [_rows(pa, 256), _rows(pb, 256), _rows(pc, 256), _rows(G, 256)],
                        [Out((S, D_MODEL), bf16, (256, D_MODEL), lambda i: (i, 0))])
    out = matmul(n("w_out"), merged, W['w_out'], add=x)
    sv.update(pa=pa, pb=pb, pc=pc, merged=merged)
    return out, sv, (got[0] if got else None)


def layer_bwd(li, dout, sv, W, side=None, own_scatter=None):
    S = dout.shape[0]
    n = lambda s: f"l{li}_{s}"
    gr = {}
    x, A, QKV, C, G = sv['x'], sv['A'], sv['QKV'], sv['C'], sv['G']

    dmerged = matmul(n("d_merged"), dout, W['w_out'], 'nt')
    gr['w_out'] = wgrad(n("g_w_out"), sv['merged'], dout)
    margs = [_rows(sv['pa'], 256, gdtype=bf16), _rows(sv['pb'], 256, gdtype=bf16), _rows(sv['pc'], 256, gdtype=bf16),
             _rows(G, 256, gdtype=bf16)]
    dpa, dpb, dpc, dgates = map_bwd(n("merge_bwd"), _merge_tile, (S // 256,), margs, [_rows(dmerged, 256)], list(range(4)))
    dya = matmul(n("d_ya"), dpa, W['proj_a'], 'nt')
    dyb = matmul(n("d_yb"), dpb, W['proj_b'], 'nt')
    dyc = matmul(n("d_yc"), dpc, W['proj_c'], 'nt')
    gr['proj_a'] = wgrad(n("g_proj_a"), sv['ya'], dpa)
    gr['proj_b'] = wgrad(n("g_proj_b"), sv['yb'], dpb)
    gr['proj_c'] = wgrad(n("g_proj_c"), sv['yc'], dpc)

    glu_b = W['s5_glu_b'].reshape(1, -1)
    gargs = [_rows(sv['g'], 512), _rows(sv['glu'], 512, gdtype=bf16),
             _rows(A, 512, col=1, width=S5_WIDTH, gshape=(S, S5_WIDTH), gdtype=bf16), _whole(glu_b, 'acc')]
    dg_a, dglu, dza, dglu_b = map_bwd(n("glu_gate_bwd"), _glu_tile, (S // 512,), gargs, [_rows(dya, 512)], [0, 1, 2, 3])
    gr['s5_glu_b'] = dglu_b.reshape(-1)
    dg = matmul(n("d_g"), dglu, W['s5_glu_w'], 'nt', add=dg_a)
    gr['s5_glu_w'] = wgrad(n("g_glu_w"), sv['g'], dglu)
    dvec = W['s5_d'].reshape(1, -1)
    sargs = _s5_args(A, sv['prep'], dvec, S)
    res = scan_bwd(n("s5_scan_bwd"), _s5_tile, (S5_CHUNKS, S // S5_TILE), _S5_CARRY, sargs, sv['s5_ck'],
                   [Arg(dg, (S5_TILE, 128), lambda o, t: (t, o))], list(range(len(sargs))), bwd_fn=_s5_tile_bwd)
    dua, dprep, dd = res[0], res[1:9], res[9]
    gr['s5_d'] = dd.reshape(-1)
    pargs = _s5_prep_args(W)
    pouts = _s5_prep_outs()
    da_re, da_im, dls, dbtr, dbti, dctr, dcti = map_bwd(
        n("s5_prep_bwd"), _s5_prep_tile, (S5_CHUNKS,), pargs,
        [Arg(d, o.block, o.imap) for d, o in zip(dprep, pouts)], list(range(7)))
    gshape = (S5_GROUPS, S5_STATE)
    gr['s5_a_re'], gr['s5_a_im'] = da_re.reshape(gshape), da_im.reshape(gshape)
    gr['s5_log_step'] = dls.reshape(-1)
    gr['s5_b_re'] = dbtr.T.reshape(S5_GROUPS, S5_STATE, S5_GROUP)
    gr['s5_b_im'] = dbti.T.reshape(S5_GROUPS, S5_STATE, S5_GROUP)
    gr['s5_c_re'] = dctr.reshape(S5_GROUP, S5_GROUPS, S5_STATE).transpose(1, 0, 2)
    gr['s5_c_im'] = dcti.reshape(S5_GROUP, S5_GROUPS, S5_STATE).transpose(1, 0, 2)

    cargs = [_rows(t, 512) for t in sv['att']] + \
            [_rows(C, 512, col=SSD_CONV_DIM // ATT_GW, width=ATT_GW, gshape=(S, ATT_GW), gdtype=bf16)]
    cres = map_bwd(n("combine_bwd"), _combine_tile, (S // 512,), cargs, [_rows(dyb, 512)], list(range(7)))
    dzb = cres[6]
    qw, kw = W['q_norm_w'].reshape(1, -1), W['k_norm_w'].reshape(1, -1)
    dqs, dks, dvs = [], [], []
    dqw = dkw = None
    for gi, (window, r) in enumerate(ATT_PAIRS):
        L = S // r
        nq, rb = _attn_plan(r)
        dspec = lambda t: Arg(t.reshape(L, r * ATT_GW), (nq * ATT_BLOCK, rb * ATT_GW), lambda rho, nb: (nb, rho))
        dq, dk, dv, dqw_g, dkw_g = scan_bwd(n(f"attn{gi}_bwd"), _attn_tile, _attn_grid(r, S), _attn_carry(r),
                                            _attn_args(QKV, gi, r, qw, kw, S), sv['att_ck'][gi],
                                            [dspec(cres[2 * gi]), dspec(cres[2 * gi + 1])], [0, 1, 2, 3, 4])
        dqs.append(dq.reshape(S, ATT_GW))
        dks.append(dk.reshape(S, ATT_GW))
        dvs.append(dv.reshape(S, ATT_GW))
        dqw = dqw_g if dqw is None else dqw + dqw_g
        dkw = dkw_g if dkw is None else dkw + dkw_g
    gr['q_norm_w'], gr['k_norm_w'] = dqw.reshape(-1), dkw.reshape(-1)

    ssd_args = _ssd_args(C, A, W, S)
    sres = scan_bwd(n("ssd_bwd"), _ssd_tile, (1, S // SSD_CHUNK), _SSD_CARRY, ssd_args, sv['ssd_ck'],
                    [Arg(dyc, (SSD_CHUNK, SSD_WIDTH), lambda o, t: (t, 0))], list(range(9)), side=side)
    sres, got = sres if side else (sres, None)
    dxbc, ddt, dzc = sres[0], sres[1], sres[2]
    gr['conv_w'] = sres[3]
    gr['conv_b'] = sres[4].reshape(-1)
    gr['dt_bias'] = sres[5].reshape(-1)[:SSD_HEADS]
    gr['ssd_a_log'] = sres[6].reshape(-1)[:SSD_HEADS]
    gr['ssd_d'] = sres[7].reshape(-1)[:SSD_HEADS]
    gr['ssd_norm_w'] = sres[8].reshape(-1)

    dpieces = [jnp.concatenate([dua, dza, ddt], axis=1), jnp.concatenate(dqs + dks + dvs, axis=1),
               jnp.concatenate([dxbc, dzb, dzc], axis=1), dgates]
    gr['w_in'] = _unrelayout_w_in_grad([wgrad(n(f"g_w_in{j}"), sv['h'], dp) for j, dp in enumerate(dpieces)])
    dh = matmul_nt_sum(n("d_h"), dpieces, list(W['w_in_pieces']), side=own_scatter(gr) if own_scatter else None)
    dh, got_own = dh if own_scatter else (dh, None)
    nargs = [_rows(x, 512), _whole(W['norm_w'].reshape(1, -1), 'acc')]
    dx, dnw = map_bwd(n("norm_bwd"), _rmsnorm_tile, (S // 512,), nargs, [_rows(dh, 512)], [0, 1], add={0: _rows(dout, 512)})
    gr['norm_w'] = dnw.reshape(-1)
    return dx, gr, got, got_own


def _unrelayout_w_in_grad(pieces):
    gA, gQ, gC, gG = pieces
    uaza, dt = gA[:, :2 * S5_WIDTH], gA[:, 2 * S5_WIDTH:2 * S5_WIDTH + SSD_HEADS]
    xbc, zb, zc = gC[:, :SSD_CONV_DIM], gC[:, SSD_CONV_DIM:SSD_CONV_DIM + ATT_GW], gC[:, SSD_CONV_DIM + ATT_GW:]
    return jnp.concatenate([uaza, gQ, zb, xbc, dt, zc, gG], axis=1)


def kernel(x, norm_w, w_in, s5_a_re, s5_a_im, s5_log_step, s5_b_re, s5_b_im, s5_c_re, s5_c_im, s5_d, s5_glu_w, s5_glu_b, q_norm_w, k_norm_w, conv_w, conv_b, dt_bias, ssd_a_log, ssd_d, ssd_norm_w, proj_a, proj_b, proj_c, w_out, loss_target, m_norm_w, m_w_in, m_s5_a_re, m_s5_a_im, m_s5_log_step, m_s5_b_re, m_s5_b_im, m_s5_c_re, m_s5_c_im, m_s5_d, m_s5_glu_w, m_s5_glu_b, m_q_norm_w, m_k_norm_w, m_conv_w, m_conv_b, m_dt_bias, m_ssd_a_log, m_ssd_d, m_ssd_norm_w, m_proj_a, m_proj_b, m_proj_c, m_w_out, v_norm_w, v_w_in, v_s5_a_re, v_s5_a_im, v_s5_log_step, v_s5_b_re, v_s5_b_im, v_s5_c_re, v_s5_c_im, v_s5_d, v_s5_glu_w, v_s5_glu_b, v_q_norm_w, v_k_norm_w, v_conv_w, v_conv_b, v_dt_bias, v_ssd_a_log, v_ssd_d, v_ssd_norm_w, v_proj_a, v_proj_b, v_proj_c, v_w_out):
    args = dict(locals())
    w = {k: args[k] for k in WEIGHTS}
    m = {k: args['m_' + k] for k in WEIGHTS}
    v = {k: args['v_' + k] for k in WEIGHTS}
    depth = norm_w.shape[0]
    S = x.shape[1]
    xs = x.reshape(S, D_MODEL)
    tgt = loss_target.reshape(S, D_MODEL)

    others = [k for k in SHARDED if k != 'w_in']

    def weight_gather(li):
        return gather_side(list(_relayout_w_in(w['w_in'][li].astype(bf16))) + [w[k][li].astype(bf16) for k in others])

    def assemble(li, gathered):
        W = {k: w[k][li] for k in WEIGHTS if k not in SHARDED}
        W['w_in_pieces'] = [t.reshape(t.shape[0] * t.shape[1], t.shape[2]) for t in gathered[:4]]
        for k, t in zip(others, gathered[4:]):
            n_dev, R, C = t.shape
            W[k] = t.reshape(n_dev * R, C) if k in ROW_SHARDED else t.transpose(1, 0, 2).reshape(R, n_dev * C)
        return W

    layers = [assemble(0, run_side("gather_weights0", weight_gather(0)))]
    act, saved = xs, []
    for li in range(depth):
        act, sv, got = layer_fwd(li, act, layers[li], weight_gather(li + 1) if li + 1 < depth else None)
        saved.append(sv)
        if got is not None:
            layers.append(assemble(li + 1, got))
    dy, loss_local = loss_and_grad(act, tgt)
    loss = lax.psum(loss_local, ("x", "y", "c"))

    big = [k for k in SHARDED if k != 'conv_w']

    def grad_scatter(gr):
        return exchange_side([gr[k] for k in big], ['rows' if k in ROW_SHARDED else 'cols' for k in big])

    grads, slots = [None] * depth, [None] * depth
    for li in reversed(range(depth)):
        dy, grads[li], got, got_own = layer_bwd(li, dy, saved[li], layers[li], grad_scatter(grads[li + 1]) if li + 1 < depth else None,
                                                grad_scatter if li == 0 else None)
        if got is not None:
            slots[li + 1] = got
        if got_own is not None:
            slots[li] = got_own
    small_keys = [k for k in WEIGHTS if k not in SHARDED] + ['conv_w']
    stacked = [jnp.stack([grads[li][k] for li in range(depth)], axis=0) for k in small_keys]
    (small_slots,) = run_side("gather_small_grads", exchange_side([_pack(stacked)], ['all']))
    grad_x = dy.reshape(x.shape)
    result = {k: adamw_layers("adamw_" + k, w[k], [slots[li][j] for li in range(depth)], m[k], v[k]) for j, k in enumerate(big)}

    totals = _unpack(sum_slots("sum_small_grads", small_slots), [t.shape for t in stacked])
    for k, g in zip(small_keys, totals):
        if k == 'conv_w':
            width = w[k].shape[-1]
            me = 4 * lax.axis_index("x") + 2 * lax.axis_index("y") + lax.axis_index("c")
            g = lax.dynamic_slice_in_dim(g, me * width, width, axis=2)
        result[k] = adamw("adamw_" + k, w[k], g[None], m[k], v[k])

    return (loss, grad_x, *[result[k][0] for k in WEIGHTS], *[result[k][1] for k in WEIGHTS],
            *[result[k][2] for k in WEIGHTS], *[result[k][3] for k in WEIGHTS])
```

```python
import functools
import math
from typing import Any, NamedTuple

import jax
import jax.numpy as jnp
from jax import lax
from jax.experimental import pallas as pl
from jax.experimental.pallas import tpu as pltpu

f32 = jnp.float32
bf16 = jnp.bfloat16

N_DEV = 8
D_MODEL = 1024
RMS_EPS = 1e-6
S5_WIDTH = 512
S5_GROUPS = 32
S5_GROUP = 16
S5_STATE = 64
S5_TILE = 512
S5_SUB = 8
S5_ND = 3
S5_CHUNKS = 4
ATT_HEAD_DIM = 64
ATT_PAIRS = ((128, 1), (512, 4), (2048, 16))
ATT_HPG = 4
ATT_BLOCK = 128
ATT_GW = ATT_HPG * ATT_HEAD_DIM
ATT_WIDTH = 768
SSD_HEADS = 12
SSD_HEAD_DIM = 64
SSD_WIDTH = 768
SSD_STATE = 128
SSD_GROUPS = 2
SSD_CHUNK = 128
SSD_CONV = 4
SSD_CONV_DIM = 1280
HPAD = 128
IN_SPLITS = (512, 512, 768, 768, 768, 256, 1280, 12, 768, 3072)
ADAM_LR, ADAM_B1, ADAM_B2, ADAM_EPS, ADAM_WD, ADAM_STEP = 0.001, 0.9, 0.999, 1e-08, 0.01, 10
VMEM_LIMIT = 56 * 1024 * 1024

WEIGHTS = ['norm_w', 'w_in', 's5_a_re', 's5_a_im', 's5_log_step', 's5_b_re', 's5_b_im', 's5_c_re',
           's5_c_im', 's5_d', 's5_glu_w', 's5_glu_b', 'q_norm_w', 'k_norm_w', 'conv_w', 'conv_b',
           'dt_bias', 'ssd_a_log', 'ssd_d', 'ssd_norm_w', 'proj_a', 'proj_b', 'proj_c', 'w_out']
ROW_SHARDED = ('w_in', 's5_glu_w', 'w_out')
SHARDED = ROW_SHARDED + ('conv_w', 'proj_a', 'proj_b', 'proj_c')


class Arg(NamedTuple):
    arr: Any
    block: tuple
    imap: Any
    kind: str = 'const'
    gshape: Any = None
    gimap: Any = None
    gdtype: Any = None


class Out(NamedTuple):
    shape: tuple
    dtype: Any
    block: tuple
    imap: Any


def _cparams(n):
    return pltpu.CompilerParams(dimension_semantics=("arbitrary",) * n, vmem_limit_bytes=VMEM_LIMIT)


def _rows(a, tm, kind='tile', col=0, width=None, gshape=None, gcol=None, gdtype=None):
    width = a.shape[1] if width is None else width
    g = None if gshape is None else (lambda i, gc=(0 if gcol is None else gcol): (i, gc))
    return Arg(a, (tm, width), lambda i, c=col: (i, c), kind, gshape, g, gdtype)


def _whole(a, kind='const'):
    nd = a.ndim
    return Arg(a, a.shape, lambda *i, nd=nd: (0,) * nd, kind)


def map_fwd(name, fn, grid, args, outs):
    n_in = len(args)

    def body(*refs):
        pid = tuple(pl.program_id(a) for a in range(len(grid)))
        res = fn(pid, *[r[...] for r in refs[:n_in]])
        for o, r in zip(refs[n_in:], res):
            o[...] = r.astype(o.dtype)

    res = pl.pallas_call(
        body, name=name, grid=grid,
        in_specs=[pl.BlockSpec(a.block, a.imap) for a in args],
        out_specs=[pl.BlockSpec(o.block, o.imap) for o in outs],
        out_shape=[jax.ShapeDtypeStruct(o.shape, o.dtype) for o in outs],
        compiler_params=_cparams(len(grid)))(*[a.arr for a in args])
    return tuple(res)


def _grad_outs(args, wrt):
    outs = []
    for i in wrt:
        a = args[i]
        shape = a.arr.shape if a.gshape is None else a.gshape
        imap = a.imap if a.gimap is None else a.gimap
        outs.append(Out(shape, f32 if a.gdtype is None else a.gdtype, a.block, imap))
    return outs


def _store_grads(pid, args, wrt, grads, grefs, adds):
    first_all = functools.reduce(jnp.logical_and, [p == 0 for p in pid])
    first_in = functools.reduce(jnp.logical_and, [p == 0 for p in pid[1:]]) if len(pid) > 1 else first_all
    for j, i in enumerate(wrt):
        g = grads[j].astype(f32)
        ref = grefs[j]
        kind = args[i].kind
        if kind == 'tile':
            if j in adds:
                g = g + adds[j]
            ref[...] = g.astype(ref.dtype)
        else:
            first = first_all if kind == 'acc' else first_in

            @pl.when(first)
            def _(ref=ref):
                ref[...] = jnp.zeros_like(ref)

            ref[...] += g


def map_bwd(name, fn, grid, args, douts, wrt, add=None):
    add = add or {}
    n_in, n_d, n_add = len(args), len(douts), len(add)
    add_keys = sorted(add)
    gouts = _grad_outs(args, wrt)

    def body(*refs):
        pid = tuple(pl.program_id(a) for a in range(len(grid)))
        vals = [r[...] for r in refs[:n_in]]
        dvals = [r[...].astype(f32) for r in refs[n_in:n_in + n_d]]
        avals = {k: refs[n_in + n_d + j][...].astype(f32) for j, k in enumerate(add_keys)}
        grefs = refs[n_in + n_d + n_add:]

        def f(*w):
            full = list(vals)
            for i, x in zip(wrt, w):
                full[i] = x
            return tuple(fn(pid, *full))

        _, vjp = jax.vjp(f, *[vals[i] for i in wrt])
        grads = vjp(tuple(dvals))
        _store_grads(pid, args, wrt, grads, grefs, avals)

    ins = list(args) + list(douts) + [add[k] for k in add_keys]
    res = pl.pallas_call(
        body, name=name, grid=grid,
        in_specs=[pl.BlockSpec(a.block, a.imap) for a in ins],
        out_specs=[pl.BlockSpec(o.block, o.imap) for o in gouts],
        out_shape=[jax.ShapeDtypeStruct(o.shape, o.dtype) for o in gouts],
        compiler_params=_cparams(len(grid)))(*[a.arr for a in ins])
    return tuple(res)


class Side(NamedTuple):
    arrs: list
    out_shapes: list
    sem_shapes: list
    phases: list


def _run_side(side, step, total, src_refs, out_refs, sem_refs):
    for frac, phase in side.phases:
        @pl.when(step == int(round(frac * (total - 1))))
        def _(phase=phase):
            phase(src_refs, out_refs, *sem_refs)


_ANY = pl.BlockSpec(memory_space=pl.ANY)


def scan_fwd(name, fn, grid, carry_shapes, args, outs, side=None):
    no, nt = grid
    n_in, n_out, n_c = len(args), len(outs), len(carry_shapes)
    ns_in, ns_out = (len(side.arrs), len(side.out_shapes)) if side else (0, 0)
    cks = [Out((no, nt) + cs, f32, (None, None) + cs, lambda o, t, n=len(cs): (o, t) + (0,) * n) for cs in carry_shapes]

    def body(*refs):
        pid = (pl.program_id(0), pl.program_id(1))
        ins = refs[:n_in]
        sins = refs[n_in:n_in + ns_in]
        refs = refs[n_in + ns_in:]
        orefs = refs[:n_out]
        ckrefs = refs[n_out:n_out + n_c]
        souts = refs[n_out + n_c:n_out + n_c + ns_out]
        crefs = refs[n_out + n_c + ns_out:n_out + n_c + ns_out + n_c]
        if side:
            _run_side(side, pid[0] * nt + pid[1], no * nt, sins, souts, refs[n_out + n_c + ns_out + n_c:])

        @pl.when(pid[1] == 0)
        def _():
            for c in crefs:
                c[...] = jnp.zeros_like(c)

        carry = tuple(c[...] for c in crefs)
        for ck, c in zip(ckrefs, carry):
            ck[...] = c
        res, newc = fn(pid, carry, *[r[...] for r in ins])
        for o, r in zip(orefs, res):
            o[...] = r.astype(o.dtype)
        for c, v in zip(crefs, newc):
            c[...] = v

    allouts = list(outs) + cks
    res = pl.pallas_call(
        body, name=name, grid=grid,
        in_specs=[pl.BlockSpec(a.block, a.imap) for a in args] + [_ANY] * ns_in,
        out_specs=[pl.BlockSpec(o.block, o.imap) for o in allouts] + [_ANY] * ns_out,
        out_shape=[jax.ShapeDtypeStruct(o.shape, o.dtype) for o in allouts] + (list(side.out_shapes) if side else []),
        scratch_shapes=[pltpu.VMEM(cs, f32) for cs in carry_shapes] + (list(side.sem_shapes) if side else []),
        compiler_params=_cparams(2))(*[a.arr for a in args], *(side.arrs if side else []))
    if side:
        return tuple(res[:n_out]), tuple(res[n_out:n_out + n_c]), list(res[n_out + n_c:])
    return tuple(res[:n_out]), tuple(res[n_out:])


def scan_bwd(name, fn, grid, carry_shapes, args, ckpts, douts, wrt, bwd_fn=None, side=None):
    no, nt = grid
    n_in, n_d, n_c = len(args), len(douts), len(carry_shapes)
    ns_in, ns_out = (len(side.arrs), len(side.out_shapes)) if side else (0, 0)

    def rev(imap):
        return lambda o, t: imap(o, nt - 1 - t)

    rargs = [a._replace(imap=rev(a.imap), gimap=None if a.gimap is None else rev(a.gimap)) for a in args]
    rdouts = [a._replace(imap=rev(a.imap)) for a in douts]
    ckargs = [Arg(ck, (None, None) + cs, rev(lambda o, t, n=len(cs): (o, t) + (0,) * n)) for ck, cs in zip(ckpts, carry_shapes)]
    gouts = _grad_outs(rargs, wrt)

    def body(*refs):
        o, t = pl.program_id(0), pl.program_id(1)
        tt = nt - 1 - t
        vals = [r[...] for r in refs[:n_in]]
        dvals = [r[...].astype(f32) for r in refs[n_in:n_in + n_d]]
        carry = tuple(r[...] for r in refs[n_in + n_d:n_in + n_d + n_c])
        sins = refs[n_in + n_d + n_c:n_in + n_d + n_c + ns_in]
        refs = refs[n_in + n_d + n_c + ns_in:]
        grefs = refs[:len(wrt)]
        souts = refs[len(wrt):len(wrt) + ns_out]
        dcrefs = refs[len(wrt) + ns_out:len(wrt) + ns_out + n_c]
        if side:
            _run_side(side, o * nt + t, no * nt, sins, souts, refs[len(wrt) + ns_out + n_c:])

        @pl.when(t == 0)
        def _():
            for c in dcrefs:
                c[...] = jnp.zeros_like(c)

        def f(carry, *w):
            full = list(vals)
            for i, x in zip(wrt, w):
                full[i] = x
            res, newc = fn((o, tt), carry, *full)
            return tuple(res), tuple(newc)

        dcarry = tuple(c[...] for c in dcrefs)
        if bwd_fn is None:
            _, vjp = jax.vjp(f, carry, *[vals[i] for i in wrt])
            grads = vjp((tuple(dvals), dcarry))
            dcarry_in, grads = grads[0], grads[1:]
        else:
            dcarry_in, grads = bwd_fn((o, tt), carry, vals, dvals, dcarry)
        for c, g in zip(dcrefs, dcarry_in):
            c[...] = g
        _store_grads((o, t), rargs, wrt, grads, grefs, {})

    ins = rargs + rdouts + ckargs
    res = pl.pallas_call(
        body, name=name, grid=grid,
        in_specs=[pl.BlockSpec(a.block, a.imap) for a in ins] + [_ANY] * ns_in,
        out_specs=[pl.BlockSpec(g.block, g.imap) for g in gouts] + [_ANY] * ns_out,
        out_shape=[jax.ShapeDtypeStruct(g.shape, g.dtype) for g in gouts] + (list(side.out_shapes) if side else []),
        scratch_shapes=[pltpu.VMEM(cs, f32) for cs in carry_shapes] + (list(side.sem_shapes) if side else []),
        compiler_params=_cparams(2))(*[a.arr for a in ins], *(side.arrs if side else []))
    if side:
        return tuple(res[:len(gouts)]), list(res[len(gouts):])
    return tuple(res)


def _pick(dim, target):
    if dim <= target:
        return dim
    for t in range(target // 128 * 128, 127, -128):
        if dim % t == 0:
            return t
    return dim


def matmul(name, a, b, mode='nn', add=None, out_dtype=f32, tm=None, tn=1152, tk=None):
    if mode == 'tn':
        K, M = a.shape
    else:
        M, K = a.shape
    N = b.shape[0] if mode == 'nt' else b.shape[1]
    assert (b.shape[1] if mode == 'nt' else b.shape[0]) == K
    tm = (1024 if mode == 'tn' else 512) if tm is None else tm
    tk = (1024 if mode == 'tn' else 1152) if tk is None else tk
    tm, tn, tk = _pick(M, tm), _pick(N, tn), _pick(K, tk)
    nk = K // tk
    a_spec = pl.BlockSpec((tk, tm), lambda i, j, k: (k, i)) if mode == 'tn' else pl.BlockSpec((tm, tk), lambda i, j, k: (i, k))
    b_spec = pl.BlockSpec((tn, tk), lambda i, j, k: (j, k)) if mode == 'nt' else pl.BlockSpec((tk, tn), lambda i, j, k: (k, j))
    dims = {'nn': (((1,), (0,)), ((), ())), 'nt': (((1,), (1,)), ((), ())), 'tn': (((0,), (0,)), ((), ()))}[mode]
    has_add = add is not None

    def body(*refs):
        if has_add:
            a_ref, b_ref, add_ref, o_ref, acc = refs
        else:
            a_ref, b_ref, o_ref, acc = refs
        k = pl.program_id(2)
        prod = lax.dot_general(a_ref[...].astype(bf16), b_ref[...].astype(bf16), dims, preferred_element_type=f32)
        if nk == 1:
            o_ref[...] = (prod + add_ref[...].astype(f32) if has_add else prod).astype(o_ref.dtype)
            return

        @pl.when(k == 0)
        def _():
            acc[...] = add_ref[...].astype(f32) if has_add else jnp.zeros_like(acc)

        acc[...] += prod

        @pl.when(k == nk - 1)
        def _():
            o_ref[...] = acc[...].astype(o_ref.dtype)

    in_specs = [a_spec, b_spec] + ([pl.BlockSpec((tm, tn), lambda i, j, k: (i, j))] if has_add else [])
    ops = [a, b] + ([add] if has_add else [])
    return pl.pallas_call(
        body, name=name, grid=(M // tm, N // tn, nk), in_specs=in_specs,
        out_specs=pl.BlockSpec((tm, tn), lambda i, j, k: (i, j)),
        out_shape=jax.ShapeDtypeStruct((M, N), out_dtype),
        scratch_shapes=[pltpu.VMEM((tm, tn), f32)],
        compiler_params=pltpu.CompilerParams(dimension_semantics=("parallel", "parallel", "arbitrary"), vmem_limit_bytes=VMEM_LIMIT))(*ops)


def matmul_nt_sum(name, lhs, rhs, tm=1024, tk=768, side=None):
    M, N = lhs[0].shape[0], rhs[0].shape[0]
    tm = _pick(M, tm)
    tks = [_pick(a.shape[1], tk) for a in lhs]
    starts, total = [], 0
    for a, t in zip(lhs, tks):
        starts.append(total)
        total += a.shape[1] // t
    npc = len(lhs)
    ns_in, ns_out = (len(side.arrs), len(side.out_shapes)) if side else (0, 0)

    def body(*refs):
        a_refs, b_refs, sins = refs[:npc], refs[npc:2 * npc], refs[2 * npc:2 * npc + ns_in]
        refs = refs[2 * npc + ns_in:]
        o_ref, souts, acc = refs[0], refs[1:1 + ns_out], refs[1 + ns_out]
        k = pl.program_id(1)
        if side:
            _run_side(side, pl.program_id(0) * total + k, (M // tm) * total, sins, souts, refs[2 + ns_out:])

        @pl.when(k == 0)
        def _():
            acc[...] = jnp.zeros_like(acc)

        for p in range(npc):
            @pl.when((k >= starts[p]) & (k < starts[p] + lhs[p].shape[1] // tks[p]))
            def _(p=p):
                acc[...] += lax.dot_general(a_refs[p][...].astype(bf16), b_refs[p][...].astype(bf16), _NT, preferred_element_type=f32)

        @pl.when(k == total - 1)
        def _():
            o_ref[...] = acc[...]

    def kblock(p):
        return lambda k: jnp.clip(k - starts[p], 0, lhs[p].shape[1] // tks[p] - 1)

    in_specs = [pl.BlockSpec((tm, tks[p]), lambda i, k, kb=kblock(p): (i, kb(k))) for p in range(npc)]
    in_specs += [pl.BlockSpec((N, tks[p]), lambda i, k, kb=kblock(p): (0, kb(k))) for p in range(npc)]
    res = pl.pallas_call(
        body, name=name, grid=(M // tm, total), in_specs=in_specs + [_ANY] * ns_in,
        out_specs=[pl.BlockSpec((tm, N), lambda i, k: (i, 0))] + [_ANY] * ns_out,
        out_shape=[jax.ShapeDtypeStruct((M, N), f32)] + (list(side.out_shapes) if side else []),
        scratch_shapes=[pltpu.VMEM((tm, N), f32)] + (list(side.sem_shapes) if side else []),
        compiler_params=_cparams(2))(*lhs, *rhs, *(side.arrs if side else []))
    return (res[0], list(res[1:])) if side else res[0]


def wgrad(name, act, dout):
    return matmul(name, act, dout, 'tn', out_dtype=bf16)


def _dot(a, b, dims=(((1,), (0,)), ((), ()))):
    return lax.dot_general(a.astype(bf16), b.astype(bf16), dims, preferred_element_type=f32)


_NT = (((1,), (1,)), ((), ()))
_TN = (((0,), (0,)), ((), ()))


def _three_term_dot(v, sel, dims):
    hi = v.astype(bf16)
    rest = v - hi.astype(f32)
    mid = rest.astype(bf16)
    lo = (rest - mid.astype(f32)).astype(bf16)
    dot = lambda t: lax.dot_general(t, sel, dims, preferred_element_type=f32)
    return dot(hi) + dot(mid) + dot(lo)


@jax.custom_vjp
def _dot_exact01(v, sel):
    return _three_term_dot(v, sel, (((1,), (0,)), ((), ())))


def _dot_exact01_fwd(v, sel):
    return _dot_exact01(v, sel), sel


def _dot_exact01_bwd(sel, ct):
    return _three_term_dot(ct, sel, _NT), jnp.zeros_like(sel)


_dot_exact01.defvjp(_dot_exact01_fwd, _dot_exact01_bwd)


def _spread_heads(v, width):
    r = lax.broadcasted_iota(jnp.int32, (HPAD, SSD_HEADS * width), 0)
    c = lax.broadcasted_iota(jnp.int32, (HPAD, SSD_HEADS * width), 1)
    return _dot_exact01(v, (r == c // width).astype(bf16))


def _rmsnorm_tile(pid, x, w):
    return (x * lax.rsqrt(jnp.mean(x * x, axis=-1, keepdims=True) + RMS_EPS) * w,)


def _shift_rows(h, d, fill):
    pad = jnp.full((d, h.shape[1]), fill, f32)
    return jnp.concatenate([pad, h[:-d]], axis=0)


def _s5_prep_tile(pid, a_re, a_im, ls, btr, bti, ctr, cti):
    o = pid[0]
    w = a_re.shape[1]
    r = lax.broadcasted_iota(jnp.int32, (S5_GROUPS, w), 0)
    c = lax.broadcasted_iota(jnp.int32, (S5_GROUPS, w), 1)
    sel = (r == o * (w // S5_STATE) + c // S5_STATE).astype(f32)
    step = jnp.dot(jnp.exp(ls), sel, precision=lax.Precision.HIGHEST, preferred_element_type=f32)
    mag = jnp.exp(a_re * step)
    ang = a_im * step
    lr, li = mag * jnp.cos(ang), mag * jnp.sin(ang)
    nr, ni = lr - 1.0, li
    den = a_re * a_re + a_im * a_im
    fr = (nr * a_re + ni * a_im) / den
    fi = (ni * a_re - nr * a_im) / den
    bbr = fr * btr - fi * bti
    bbi = fr * bti + fi * btr
    reps = w // S5_STATE
    rr = lax.broadcasted_iota(jnp.int32, (reps * S5_GROUP, w), 0)
    cc = lax.broadcasted_iota(jnp.int32, (reps * S5_GROUP, w), 1)
    diag = (rr // S5_GROUP) == (cc // S5_STATE)

    def expand(m):
        return jnp.where(diag, jnp.concatenate([m] * reps, axis=0), 0.0)

    pr, pi = lr, li
    rows_r, rows_i = [pr], [pi]
    for _ in range(S5_ND - 1):
        pr, pi = pr * pr - pi * pi, 2.0 * pr * pi
        rows_r.append(pr)
        rows_i.append(pi)
    lamd_r, lamd_i = jnp.concatenate(rows_r, axis=0), jnp.concatenate(rows_i, axis=0)
    tr = jnp.broadcast_to(lr, (S5_SUB, w))
    ti = jnp.broadcast_to(li, (S5_SUB, w))
    for j in range(S5_ND):
        sr, si = _shift_rows(tr, 1 << j, 1.0), _shift_rows(ti, 1 << j, 0.0)
        tr, ti = tr * sr - ti * si, tr * si + ti * sr
    return lamd_r, lamd_i, tr, ti, expand(bbr), expand(bbi), expand(ctr), expand(cti)


def _s5_tile(pid, carry, u, lamd_r, lamd_i, lam8_r, lam8_i, bbr, bbi, ccr, cci, dvec):
    hr, hi = _s5_scan(_dot(u, bbr), _dot(u, bbi), carry, lamd_r, lamd_i, lam8_r, lam8_i, reverse=False)
    return (_s5_readout(hr, hi, u, ccr, cci, dvec),), (hr[-1:], hi[-1:])


def _s5_readout(hr, hi, u, ccr, cci, dvec):
    return jax.nn.gelu(_dot(hr, ccr, _NT) - _dot(hi, cci, _NT) + dvec * u)


def _s5_scan(xr, xi, carry, lamd_r, lamd_i, lam8_r, lam8_i, reverse):
    cr, ci = carry
    T, G = xr.shape[0], S5_SUB
    sign = -1.0 if reverse else 1.0
    sub = lax.broadcasted_iota(jnp.int32, (G, 1), 0)
    xr, xi = xr.reshape(T // G, G, xr.shape[1]), xi.reshape(T // G, G, xi.shape[1])
    for j in range(S5_ND):
        d = 1 << j
        keep = (sub < G - d) if reverse else (sub >= d)
        ar = jnp.where(keep, lamd_r[j:j + 1], 0.0)
        ai = jnp.where(keep, sign * lamd_i[j:j + 1], 0.0)
        sr = pltpu.roll(xr, G - d if reverse else d, axis=1)
        si = pltpu.roll(xi, G - d if reverse else d, axis=1)
        xr, xi = xr + ar * sr - ai * si, xi + ar * si + ai * sr
    xr, xi = xr.reshape(T, xr.shape[2]), xi.reshape(T, xi.shape[2])
    if reverse:
        pr = jnp.concatenate([lam8_r[G - 1 - s:G - s] for s in range(G)], axis=0)
        pi = -jnp.concatenate([lam8_i[G - 1 - s:G - s] for s in range(G)], axis=0)
    else:
        pr, pi = lam8_r, lam8_i
    n = T // G
    rows_r, rows_i = [None] * n, [None] * n
    for i in (reversed(range(n)) if reverse else range(n)):
        gr_, gi_ = xr[i * G:(i + 1) * G], xi[i * G:(i + 1) * G]
        gr_, gi_ = gr_ + pr * cr - pi * ci, gi_ + pr * ci + pi * cr
        cr, ci = (gr_[:1], gi_[:1]) if reverse else (gr_[G - 1:], gi_[G - 1:])
        rows_r[i], rows_i[i] = gr_, gi_
    return jnp.concatenate(rows_r, axis=0), jnp.concatenate(rows_i, axis=0)


def _s5_tile_bwd(pid, carry, vals, douts, dcarry):
    u, lamd_r, lamd_i, lam8_r, lam8_i, bbr, bbi, ccr, cci, dvec = vals
    (dg,) = douts
    hr, hi = _s5_scan(_dot(u, bbr), _dot(u, bbi), carry, lamd_r, lamd_i, lam8_r, lam8_i, reverse=False)
    _, vjp = jax.vjp(_s5_readout, hr, hi, u, ccr, cci, dvec)
    dhr, dhi, du, dccr, dcci, ddvec = vjp(dg)
    Hr, Hi = _s5_scan(dhr, dhi, dcarry, lamd_r, lamd_i, lam8_r, lam8_i, reverse=True)
    _, vjp_in = jax.vjp(lambda u, bbr, bbi: (_dot(u, bbr), _dot(u, bbi)), u, bbr, bbi)
    du2, dbbr, dbbi = vjp_in((Hr, Hi))
    pr = jnp.concatenate([carry[0], hr[:-1]], axis=0)
    pi = jnp.concatenate([carry[1], hi[:-1]], axis=0)
    dlam_r = jnp.sum(Hr * pr + Hi * pi, axis=0, keepdims=True)
    dlam_i = jnp.sum(Hi * pr - Hr * pi, axis=0, keepdims=True)
    zrow = jnp.zeros((S5_ND - 1, dlam_r.shape[1]), f32)
    dlamd_r, dlamd_i = jnp.concatenate([dlam_r, zrow], axis=0), jnp.concatenate([dlam_i, zrow], axis=0)
    grads = (du + du2, dlamd_r, dlamd_i, jnp.zeros_like(lam8_r), jnp.zeros_like(lam8_i), dbbr, dbbi, dccr, dcci, ddvec)
    return (Hr[:1], Hi[:1]), grads


def _glu_tile(pid, g, glu, za, b):
    return (g * jax.nn.sigmoid(glu + b) * jax.nn.silu(za),)


def _attn_tile(pid, carry, q, k, v, qw, kw):
    n = pid[1]
    kp, vp = carry
    D, B = ATT_HEAD_DIM, ATT_BLOCK
    W = 2 * D
    nq, ncol = q.shape[0] // B, q.shape[1] // W
    r = lax.broadcasted_iota(jnp.int32, (B, 2 * B), 0)
    c = lax.broadcasted_iota(jnp.int32, (B, 2 * B), 1)
    diff = r + B - c
    band = (diff >= 0) & (diff <= B)
    band_first = band & ((c >= B) | (n > 0))
    low = lax.broadcasted_iota(jnp.int32, (1, W), 1) < D
    same_head = (lax.broadcasted_iota(jnp.int32, (W, W), 0) // D == lax.broadcasted_iota(jnp.int32, (W, W), 1) // D).astype(bf16)

    def hnorm(x, w):
        rows = x.shape[0]
        t = jnp.concatenate([x[:, j * W:(j + 1) * W] for j in range(ncol)], axis=0) if ncol > 1 else x
        ms = _dot_exact01(t * t, same_head) * (1.0 / D)
        t = t * lax.rsqrt(ms + RMS_EPS) * jnp.concatenate([w, w], axis=1)
        return [t[j * rows:(j + 1) * rows] for j in range(ncol)]

    qns, kns = hnorm(q, qw), hnorm(k, kw)
    out_cols, lse_cols, kn_cols = [], [], []
    for j in range(ncol):
        sl = slice(j * W, (j + 1) * W)
        qn, kn, vj = qns[j], kns[j], v[:, sl]
        kn_cols.append(kn[(nq - 1) * B:])
        outs, lses = [], []
        for b in range(nq):
            rows = slice(b * B, (b + 1) * B)
            prev = slice((b - 1) * B, b * B)
            kk = jnp.concatenate([kp[:, sl] if b == 0 else kn[prev], kn[rows]], axis=0)
            vv = jnp.concatenate([vp[:, sl] if b == 0 else vj[prev], vj[rows]], axis=0)
            o2, l2 = [], []
            for head_lanes in (low, ~low):
                s = _dot(jnp.where(head_lanes, qn[rows], 0.0), kk, _NT) * (D ** -0.5)
                s = jnp.where(band_first if b == 0 else band, s, -1e30)
                m = jnp.max(s, axis=-1, keepdims=True)
                p = jnp.exp(s - m)
                l = jnp.sum(p, axis=-1, keepdims=True)
                o2.append(_dot(p / l, vv))
                l2.append(m + jnp.log(l))
            outs.append(jnp.where(low, o2[0], o2[1]))
            lses.append(jnp.where(low, l2[0], l2[1]))
        out_cols.append(jnp.concatenate(outs, axis=0) if nq > 1 else outs[0])
        lse_cols.append(jnp.concatenate(lses, axis=0) if nq > 1 else lses[0])
    return ((jnp.concatenate(out_cols, axis=1), jnp.concatenate(lse_cols, axis=1)),
            (jnp.concatenate(kn_cols, axis=1), v[(nq - 1) * B:]))


def _combine_tile(pid, o1, l1, o2, l2, o3, l3, zb):
    m = jnp.maximum(jnp.maximum(l1, l2), l3)
    e1, e2, e3 = jnp.exp(l1 - m), jnp.exp(l2 - m), jnp.exp(l3 - m)
    y = (e1 * o1 + e2 * o2 + e3 * o3) / (e1 + e2 + e3)
    return (y * jax.nn.silu(zb),)


def _softplus(x):
    return jnp.maximum(x, 0.0) + jnp.log(1.0 + jnp.exp(-jnp.abs(x)))


def _ssd_tile(pid, carry, xbc, dt, z, conv_w, conv_b, dt_bias, a_log, dvec, norm_w):
    xprev, state = carry
    T, P, N = SSD_CHUNK, SSD_HEAD_DIM, SSD_STATE
    xx = jnp.concatenate([xprev, xbc], axis=0)
    conv = conv_b
    for k in range(SSD_CONV):
        off = 8 - (SSD_CONV - 1) + k
        conv = conv + conv_w[k:k + 1] * xx[off:off + T]
    xc = jax.nn.silu(conv)
    dtp = _softplus(dt + dt_bias)
    a_dt = dtp * (-jnp.exp(a_log))
    r = lax.broadcasted_iota(jnp.int32, (T, T), 0)
    c = lax.broadcasted_iota(jnp.int32, (T, T), 1)
    tri = r >= c
    trif = tri.astype(f32)
    hi = lax.Precision.HIGHEST
    a_cs = jnp.dot(trif, a_dt, precision=hi, preferred_element_type=f32)
    a_cs_t = lax.dot_general(a_dt, trif, (((0,), (1,)), ((), ())), precision=hi, preferred_element_type=f32)
    xs = xc[:, :SSD_WIDTH]
    acs_p = _spread_heads(a_cs, P)
    xdt = xs * _spread_heads(dtp, P)
    skip = _spread_heads(dvec, P)
    to_end = jnp.exp(acs_p[T - 1:T] - acs_p)
    low = lax.broadcasted_iota(jnp.int32, (1, 2 * P), 1) < P
    low_rows = lax.broadcasted_iota(jnp.int32, (2 * P, 1), 0) < P
    ys, states = [], []
    for j in range(SSD_HEADS // 2):
        g = 2 * j // (SSD_HEADS // SSD_GROUPS)
        if 2 * j % (SSD_HEADS // SSD_GROUPS) == 0:
            bg = xc[:, SSD_WIDTH + g * N:SSD_WIDTH + (g + 1) * N]
            cg = xc[:, SSD_WIDTH + SSD_GROUPS * N + g * N:SSD_WIDTH + SSD_GROUPS * N + (g + 1) * N]
            cb = _dot(cg, bg, _NT)
        lanes = slice(2 * j * P, 2 * (j + 1) * P)
        st = state[lanes, :]
        diag, last = [], []
        for h in (2 * j, 2 * j + 1):
            decay = jnp.exp(jnp.where(tri, a_cs[:, h:h + 1] - a_cs_t[h:h + 1, :], -1e30))
            diag.append(_dot(cb * decay, xdt[:, lanes]))
            last.append(jnp.exp(a_cs_t[h:h + 1, T - 1:T]))
        y = (jnp.where(low, diag[0], diag[1]) + _dot(cg, st, _NT) * jnp.exp(acs_p[:, lanes])
             + xs[:, lanes] * skip[:, lanes])
        ys.append(y)
        states.append(jnp.where(low_rows, last[0], last[1]) * st + _dot(xdt[:, lanes] * to_end[:, lanes], bg, _TN))
    y = jnp.concatenate(ys, axis=1) * jax.nn.silu(z)
    out = y * lax.rsqrt(jnp.mean(y * y, axis=-1, keepdims=True) + RMS_EPS) * norm_w
    return (out,), (xbc[T - 8:], jnp.concatenate(states, axis=0))


def _merge_tile(pid, pa, pb, pc, gates):
    d = pa.shape[1]
    g = jax.nn.sigmoid(gates)
    return (g[:, :d] * pa + g[:, d:2 * d] * pb + g[:, 2 * d:] * pc,)


def loss_and_grad(y, target, tm=512):
    S, D = y.shape
    nt = S // tm

    def body(y_ref, t_ref, dy_ref, l_ref, acc):
        i = pl.program_id(0)

        @pl.when(i == 0)
        def _():
            acc[...] = jnp.zeros_like(acc)

        diff = y_ref[...] - t_ref[...]
        dy_ref[...] = diff * (1.0 / D)
        acc[...] += jnp.sum((diff * diff).reshape(tm // 8, 8, D), axis=0)

        @pl.when(i == nt - 1)
        def _():
            l_ref[...] = jnp.broadcast_to(0.5 / D * jnp.sum(acc[...]), l_ref.shape)

    dy, l = pl.pallas_call(
        body, name="loss_head", grid=(nt,),
        in_specs=[pl.BlockSpec((tm, D), lambda i: (i, 0))] * 2,
        out_specs=[pl.BlockSpec((tm, D), lambda i: (i, 0)), pl.BlockSpec((8, 128), lambda i: (0, 0))],
        out_shape=[jax.ShapeDtypeStruct((S, D), f32), jax.ShapeDtypeStruct((8, 128), f32)],
        scratch_shapes=[pltpu.VMEM((8, D), f32)],
        compiler_params=_cparams(1))(y, target)
    return dy, l[0, 0]


def _row_tile(R, C, budget=1 << 20):
    best = R
    for t in range(8, R, 8):
        if R % t == 0 and t * C * 4 <= budget:
            best = t
    if best == R and R * C * 4 > budget:
        for t in range(8, R, 8):
            if R % t == 0:
                return t
    return best


def _as2d(t, lead=0):
    return t.reshape(t.shape[:lead] + (math.prod(t.shape[lead:-1]), t.shape[-1]))


def adamw_layers(name, w, slots, m, v):
    L, R, C = w.shape
    n = slots[0].shape[0]
    lanes = -(-C // 128) * 128
    tr = _row_tile(R, lanes * (n * L + 7), budget=10 << 20)

    def body(*refs):
        w_ref, m_ref, v_ref = refs[0], refs[1 + L], refs[2 + L]
        go_ref, d_ref, nm_ref, nv_ref = refs[3 + L:]
        layer = pl.program_id(0)
        gg = None
        for l in range(L):
            s = refs[1 + l][0].astype(f32)
            for j in range(1, n):
                s = s + refs[1 + l][j].astype(f32)
            gg = s if gg is None else jnp.where(layer == l, s, gg)
        go_ref[...] = gg
        nm = ADAM_B1 * m_ref[...] + (1.0 - ADAM_B1) * gg
        nv = ADAM_B2 * v_ref[...] + (1.0 - ADAM_B2) * jnp.square(gg)
        m_hat = nm / (1.0 - ADAM_B1 ** ADAM_STEP)
        v_hat = nv / (1.0 - ADAM_B2 ** ADAM_STEP)
        d_ref[...] = -ADAM_LR * (m_hat / (jnp.sqrt(v_hat) + ADAM_EPS) + ADAM_WD * w_ref[...])
        nm_ref[...] = nm
        nv_ref[...] = nv

    spec = pl.BlockSpec((None, tr, C), lambda l, i: (l, i, 0))
    slot_specs = [pl.BlockSpec((n, tr, C), lambda l, i, own=own: (0, jnp.where(l == own, i, 0), 0)) for own in range(L)]
    res = pl.pallas_call(
        body, name=name, grid=(L, R // tr),
        in_specs=[spec] + slot_specs + [spec, spec], out_specs=[spec] * 4,
        out_shape=[jax.ShapeDtypeStruct((L, R, C), f32)] * 4,
        compiler_params=_cparams(2))(w, *slots, m, v)
    return tuple(res)


def adamw(name, w, gslots, m, v):
    shape = w.shape
    n = gslots.shape[0]
    C = shape[-1]
    R = math.prod(shape[:-1])
    lanes = -(-C // 128) * 128
    tr = _row_tile(R, lanes * (n + 7), budget=10 << 20)

    def body(w_ref, g_ref, m_ref, v_ref, go_ref, d_ref, nm_ref, nv_ref):
        gg = g_ref[0].astype(f32)
        for s in range(1, n):
            gg = gg + g_ref[s].astype(f32)
        go_ref[...] = gg
        nm = ADAM_B1 * m_ref[...] + (1.0 - ADAM_B1) * gg
        nv = ADAM_B2 * v_ref[...] + (1.0 - ADAM_B2) * jnp.square(gg)
        m_hat = nm / (1.0 - ADAM_B1 ** ADAM_STEP)
        v_hat = nv / (1.0 - ADAM_B2 ** ADAM_STEP)
        d_ref[...] = -ADAM_LR * (m_hat / (jnp.sqrt(v_hat) + ADAM_EPS) + ADAM_WD * w_ref[...])
        nm_ref[...] = nm
        nv_ref[...] = nv

    spec = pl.BlockSpec((tr, C), lambda i: (i, 0))
    res = pl.pallas_call(
        body, name=name, grid=(R // tr,),
        in_specs=[spec, pl.BlockSpec((n, tr, C), lambda i: (0, i, 0)), spec, spec], out_specs=[spec] * 4,
        out_shape=[jax.ShapeDtypeStruct((R, C), f32)] * 4,
        compiler_params=_cparams(1))(w.reshape(R, C), gslots.reshape(n, R, C), m.reshape(R, C), v.reshape(R, C))
    return tuple(t.reshape(shape) for t in res)


PACK_ROWS = 256


def sum_slots(name, x):
    n, R, C = x.shape

    def body(x_ref, o_ref):
        acc = x_ref[0]
        for s in range(1, n):
            acc = acc + x_ref[s]
        o_ref[...] = acc

    return pl.pallas_call(
        body, name=name, grid=(R // PACK_ROWS,),
        in_specs=[pl.BlockSpec((n, PACK_ROWS, C), lambda i: (0, i, 0))],
        out_specs=pl.BlockSpec((PACK_ROWS, C), lambda i: (i, 0)),
        out_shape=jax.ShapeDtypeStruct((R, C), f32), compiler_params=_cparams(1))(x)


def _pack(parts):
    flat = jnp.concatenate([p.reshape(-1) for p in parts])
    unit = 128 * PACK_ROWS
    tot = -(-flat.shape[0] // unit) * unit
    return jnp.pad(flat, (0, tot - flat.shape[0])).reshape(tot // 128, 128)


def _unpack(buf, shapes):
    flat = buf.reshape(-1)
    out, off = [], 0
    for s in shapes:
        size = math.prod(s)
        out.append(flat[off:off + size].reshape(s))
        off += size
    return out


def _comm_sems(nt):
    return [pltpu.SemaphoreType.DMA((nt, N_DEV - 1)), pltpu.SemaphoreType.DMA((nt, N_DEV - 1)), pltpu.SemaphoreType.DMA((nt,))]


def exchange_side(srcs, modes):
    nt = len(srcs)
    slabs = []
    for s, mode in zip(srcs, modes):
        R, C = s.shape
        slabs.append({'all': (R, C), 'rows': (R // N_DEV, C), 'cols': (R, C // N_DEV)}[mode])

    def piece(ref, mode, slab, p):
        if mode == 'all':
            return ref
        if mode == 'rows':
            return ref.at[pl.ds(p * slab[0], slab[0]), :]
        return ref.at[:, pl.ds(p * slab[1], slab[1])]

    def copies(src_refs, out_refs, send_sems, recv_sems, local_sems):
        x, y, c = lax.axis_index("x"), lax.axis_index("y"), lax.axis_index("c")
        me = 4 * x + 2 * y + c
        out = []
        for k in (1, 2, 4, 3, 5, 6, 7):
            px = 1 - x if k & 4 else x
            py = 1 - y if k & 2 else y
            pc = 1 - c if k & 1 else c
            for t in range(nt):
                out.append(pltpu.make_async_remote_copy(
                    src_ref=piece(src_refs[t], modes[t], slabs[t], 4 * px + 2 * py + pc), dst_ref=out_refs[t].at[me],
                    send_sem=send_sems.at[t, k - 1], recv_sem=recv_sems.at[t, k - 1],
                    device_id=(px, py, pc), device_id_type=pl.DeviceIdType.MESH))
        for t in range(nt):
            out.append(pltpu.make_async_copy(piece(src_refs[t], modes[t], slabs[t], me), out_refs[t].at[me], local_sems.at[t]))
        return out

    def start(*refs):
        for cp in copies(*refs):
            cp.start()

    def finish(*refs):
        for cp in copies(*refs):
            cp.wait()

    out_shapes = [jax.ShapeDtypeStruct((N_DEV,) + sl, s.dtype) for s, sl in zip(srcs, slabs)]
    return Side(list(srcs), out_shapes, _comm_sems(nt), [(0.0, start), (1.0, finish)])


def gather_side(srcs):
    nt = len(srcs)

    def plan(src_refs, out_refs, send_sems, recv_sems, local_sems):
        x, y, c = lax.axis_index("x"), lax.axis_index("y"), lax.axis_index("c")
        me, sibling = (x, y, c), (x, y, 1 - c)
        chips = [(1 - x, y), (x, 1 - y), (1 - x, 1 - y)]

        def slot(t, dev):
            return out_refs[t].at[4 * dev[0] + 2 * dev[1] + dev[2]]

        def copy(t, k, block, to, src=None):
            return pltpu.make_async_remote_copy(
                src_ref=slot(t, block) if src is None else src, dst_ref=slot(t, block),
                send_sem=send_sems.at[t, k], recv_sem=recv_sems.at[t, k], device_id=to, device_id_type=pl.DeviceIdType.MESH)

        mine = [pltpu.make_async_copy(src_refs[t], slot(t, me), local_sems.at[t]) for t in range(nt)]
        first = []
        for t in range(nt):
            first.append(copy(t, 0, me, sibling, src=src_refs[t]))
            first += [copy(t, 1 + j, me, (*chip, c), src=src_refs[t]) for j, chip in enumerate(chips)]
        landed = [copy(t, 1 + j, (*chip, c), me) for j, chip in enumerate(chips) for t in range(nt)]
        passed = [copy(t, 4 + j, (*chip, c), sibling) for j, chip in enumerate(chips) for t in range(nt)]
        from_sibling = [copy(t, 0, sibling, me) for t in range(nt)]
        from_sibling += [copy(t, 4 + j, (*chip, 1 - c), me) for t in range(nt) for j, chip in enumerate(chips)]
        return mine, first, landed, passed, from_sibling

    def start(*refs):
        mine, first, _, _, _ = plan(*refs)
        for cp in mine + first:
            cp.start()

    def forward(*refs):
        _, _, landed, passed, _ = plan(*refs)
        for got, fwd in zip(landed, passed):
            got.wait_recv()
            fwd.start()

    def finish(*refs):
        mine, first, _, passed, from_sibling = plan(*refs)
        for cp in from_sibling:
            cp.wait_recv()
        for cp in first + passed:
            cp.wait_send()
        for cp in mine:
            cp.wait()

    out_shapes = [jax.ShapeDtypeStruct((N_DEV,) + s.shape, s.dtype) for s in srcs]
    return Side(list(srcs), out_shapes, _comm_sems(nt), [(0.0, start), (0.5, forward), (1.0, finish)])


def run_side(name, side):
    ns = len(side.arrs)

    def body(*refs):
        for _, phase in side.phases:
            phase(refs[:ns], refs[ns:ns + len(side.out_shapes)], *refs[ns + len(side.out_shapes):])

    return list(pl.pallas_call(
        body, name=name, in_specs=[_ANY] * ns, out_specs=[_ANY] * len(side.out_shapes),
        out_shape=list(side.out_shapes), scratch_shapes=list(side.sem_shapes))(*side.arrs))


def _relayout_w_in(w):
    offs = [0]
    for s in IN_SPLITS:
        offs.append(offs[-1] + s)
    p = [w[:, offs[i]:offs[i + 1]] for i in range(len(IN_SPLITS))]
    ua, za, q, k, v, zb, xbc, dt, zc, gates = p
    dtp = jnp.pad(dt, ((0, 0), (0, HPAD - dt.shape[1])))
    return (jnp.concatenate([ua, za, dtp], 1), w[:, offs[2]:offs[5]], jnp.concatenate([xbc, zb, zc], 1), gates)


def _pad_lanes(v, n=HPAD):
    return jnp.pad(v.reshape(1, -1), ((0, 0), (0, n - v.shape[-1])))


def _s5_prep_args(W):
    g2 = S5_GROUPS * S5_STATE
    w = g2 // S5_CHUNKS
    a_re, a_im = W['s5_a_re'].reshape(1, g2), W['s5_a_im'].reshape(1, g2)
    ls = W['s5_log_step'].reshape(1, S5_GROUPS)
    btr, bti = W['s5_b_re'].reshape(g2, S5_GROUP).T, W['s5_b_im'].reshape(g2, S5_GROUP).T
    ctr = W['s5_c_re'].transpose(1, 0, 2).reshape(S5_GROUP, g2)
    cti = W['s5_c_im'].transpose(1, 0, 2).reshape(S5_GROUP, g2)
    col = lambda a, rows: Arg(a, (rows, w), lambda o: (0, o), 'tile')
    return [col(a_re, 1), col(a_im, 1), _whole(ls, 'acc'), col(btr, S5_GROUP), col(bti, S5_GROUP), col(ctr, S5_GROUP), col(cti, S5_GROUP)]


def _s5_prep_outs():
    g2 = S5_GROUPS * S5_STATE
    w = g2 // S5_CHUNKS
    rows = (S5_ND, S5_ND, S5_SUB, S5_SUB, 128, 128, 128, 128)
    return [Out((r, g2), f32, (r, w), lambda o: (0, o)) for r in rows]


def _s5_args(A, prep, dvec, S):
    w = S5_GROUPS * S5_STATE // S5_CHUNKS
    args = [Arg(A, (S5_TILE, 128), lambda o, t: (t, o), 'tile', (S, S5_WIDTH), None, bf16)]
    for p in prep:
        args.append(Arg(p, (p.shape[0], w), lambda o, t: (0, o), 'acc0'))
    args.append(Arg(dvec, (1, 128), lambda o, t: (0, o), 'acc0'))
    return args


def _attn_args(QKV, g, r, qw, kw, S):
    L = S // r
    nq, rb = _attn_plan(r)
    block = (nq * ATT_BLOCK, rb * ATT_GW)
    gshape = (L, r * ATT_GW)
    gimap = lambda rho, n: (n, rho)
    if r == 1:
        mk = lambda j: Arg(QKV, block, lambda rho, n, j=j: (n, j), 'tile', gshape, gimap, bf16)
    else:
        def mk(j):
            view = QKV[:, j * ATT_GW:(j + 1) * ATT_GW].reshape(L, r * ATT_GW)
            return Arg(view, block, gimap, 'tile', None, None, bf16)
    return [mk(g), mk(3 + g), mk(6 + g), _whole(qw, 'acc'), _whole(kw, 'acc')]


def _attn_plan(r):
    return (4, 1) if r == 1 else (1, min(r, 4))


def _attn_grid(r, S):
    nq, rb = _attn_plan(r)
    return (r // rb, S // r // ATT_BLOCK // nq)


def _attn_carry(r):
    return ((ATT_BLOCK, _attn_plan(r)[1] * ATT_GW),) * 2


def _ssd_args(C, A, W, S):
    T = SSD_CHUNK
    return [Arg(C, (T, SSD_CONV_DIM), lambda o, t: (t, 0), 'tile', (S, SSD_CONV_DIM), None, bf16),
            Arg(A, (T, HPAD), lambda o, t: (t, 2 * S5_WIDTH // HPAD), 'tile', (S, HPAD), lambda o, t: (t, 0), bf16),
            Arg(C, (T, SSD_WIDTH), lambda o, t: (t, 2), 'tile', (S, SSD_WIDTH), lambda o, t: (t, 0), bf16),
            _whole(W['conv_w'], 'acc'), _whole(W['conv_b'].reshape(1, -1), 'acc'),
            _whole(_pad_lanes(W['dt_bias']), 'acc'), _whole(_pad_lanes(W['ssd_a_log']), 'acc'),
            _whole(_pad_lanes(W['ssd_d']), 'acc'), _whole(W['ssd_norm_w'].reshape(1, -1), 'acc')]


_SSD_CARRY = ((8, SSD_CONV_DIM), (SSD_WIDTH, SSD_STATE))
_S5_CARRY = ((1, 512), (1, 512))


def layer_fwd(li, x, W, side=None):
    S = x.shape[0]
    n = lambda s: f"l{li}_{s}"
    sv = {'x': x}
    (h,) = map_fwd(n("norm"), _rmsnorm_tile, (S // 512,), [_rows(x, 512), _whole(W['norm_w'].reshape(1, -1))],
                   [Out((S, D_MODEL), bf16, (512, D_MODEL), lambda i: (i, 0))])
    wA, wQ, wC, wG = W['w_in_pieces']
    A = matmul(n("in_a"), h, wA)
    QKV = matmul(n("in_qkv"), h, wQ)
    C = matmul(n("in_c"), h, wC)
    G = matmul(n("in_g"), h, wG)
    sv.update(h=h, A=A, QKV=QKV, C=C, G=G)

    prep = map_fwd(n("s5_prep"), _s5_prep_tile, (S5_CHUNKS,), _s5_prep_args(W), _s5_prep_outs())
    dvec = W['s5_d'].reshape(1, -1)
    (g,), s5_ck, *got = scan_fwd(n("s5_scan"), _s5_tile, (S5_CHUNKS, S // S5_TILE), _S5_CARRY, _s5_args(A, prep, dvec, S),
                                 [Out((S, S5_WIDTH), f32, (S5_TILE, 128), lambda o, t: (t, o))], side=side)
    glu = matmul(n("glu"), g, W['s5_glu_w'])
    glu_b = W['s5_glu_b'].reshape(1, -1)
    (ya,) = map_fwd(n("glu_gate"), _glu_tile, (S // 512,),
                    [_rows(g, 512), _rows(glu, 512), _rows(A, 512, col=1, width=S5_WIDTH), _whole(glu_b)],
                    [Out((S, S5_WIDTH), bf16, (512, S5_WIDTH), lambda i: (i, 0))])
    sv.update(prep=prep, g=g, glu=glu, ya=ya, s5_ck=s5_ck)

    qw, kw = W['q_norm_w'].reshape(1, -1), W['k_norm_w'].reshape(1, -1)
    att, att_ck = [], []
    for gi, (window, r) in enumerate(ATT_PAIRS):
        assert window // r == ATT_BLOCK and S % (r * ATT_BLOCK) == 0
        L = S // r
        nq, rb = _attn_plan(r)
        assert S // r // ATT_BLOCK % nq == 0
        spec = Out((L, r * ATT_GW), f32, (nq * ATT_BLOCK, rb * ATT_GW), lambda rho, nb: (nb, rho))
        (o, lse), ck = scan_fwd(n(f"attn{gi}"), _attn_tile, _attn_grid(r, S), _attn_carry(r), _attn_args(QKV, gi, r, qw, kw, S), [spec, spec])
        att += [o.reshape(S, ATT_GW), lse.reshape(S, ATT_GW)]
        att_ck.append(ck)
    (yb,) = map_fwd(n("combine"), _combine_tile, (S // 512,),
                    [_rows(t, 512) for t in att] + [_rows(C, 512, col=SSD_CONV_DIM // ATT_GW, width=ATT_GW)],
                    [Out((S, ATT_GW), bf16, (512, ATT_GW), lambda i: (i, 0))])
    sv.update(att=att, att_ck=att_ck, yb=yb)

    (yc,), ssd_ck = scan_fwd(n("ssd"), _ssd_tile, (1, S // SSD_CHUNK), _SSD_CARRY, _ssd_args(C, A, W, S),
                             [Out((S, SSD_WIDTH), bf16, (SSD_CHUNK, SSD_WIDTH), lambda o, t: (t, 0))])
    sv.update(yc=yc, ssd_ck=ssd_ck)

    pa = matmul(n("proj_a"), ya, W['proj_a'])
    pb = matmul(n("proj_b"), yb, W['proj_b'])
    pc = matmul(n("proj_c"), yc, W['proj_c'])
    (merged,) = map_fwd(n("merge"), _merge_tile, (S // 256,),
                        [_rows(pa, 256), _rows(pb, 256), _rows(pc, 256), _rows(G, 256)],
                        [Out((S, D_MODEL), bf16, (256, D_MODEL), lambda i: (i, 0))])
    out = matmul(n("w_out"), merged, W['w_out'], add=x)
    sv.update(pa=pa, pb=pb, pc=pc, merged=merged)
    return out, sv, (got[0] if got else None)


def layer_bwd(li, dout, sv, W, side=None, own_scatter=None):
    S = dout.shape[0]
    n = lambda s: f"l{li}_{s}"
    gr = {}
    x, A, QKV, C, G = sv['x'], sv['A'], sv['QKV'], sv['C'], sv['G']

    dmerged = matmul(n("d_merged"), dout, W['w_out'], 'nt')
    gr['w_out'] = wgrad(n("g_w_out"), sv['merged'], dout)
    margs = [_rows(sv['pa'], 256, gdtype=bf16), _rows(sv['pb'], 256, gdtype=bf16), _rows(sv['pc'], 256, gdtype=bf16),
             _rows(G, 256, gdtype=bf16)]
    dpa, dpb, dpc, dgates = map_bwd(n("merge_bwd"), _merge_tile, (S // 256,), margs, [_rows(dmerged, 256)], list(range(4)))
    dya = matmul(n("d_ya"), dpa, W['proj_a'], 'nt')
    dyb = matmul(n("d_yb"), dpb, W['proj_b'], 'nt')
    dyc = matmul(n("d_yc"), dpc, W['proj_c'], 'nt')
    gr['proj_a'] = wgrad(n("g_proj_a"), sv['ya'], dpa)
    gr['proj_b'] = wgrad(n("g_proj_b"), sv['yb'], dpb)
    gr['proj_c'] = wgrad(n("g_proj_c"), sv['yc'], dpc)

    glu_b = W['s5_glu_b'].reshape(1, -1)
    gargs = [_rows(sv['g'], 512), _rows(sv['glu'], 512, gdtype=bf16),
             _rows(A, 512, col=1, width=S5_WIDTH, gshape=(S, S5_WIDTH), gdtype=bf16), _whole(glu_b, 'acc')]
    dg_a, dglu, dza, dglu_b = map_bwd(n("glu_gate_bwd"), _glu_tile, (S // 512,), gargs, [_rows(dya, 512)], [0, 1, 2, 3])
    gr['s5_glu_b'] = dglu_b.reshape(-1)
    dg = matmul(n("d_g"), dglu, W['s5_glu_w'], 'nt', add=dg_a)
    gr['s5_glu_w'] = wgrad(n("g_glu_w"), sv['g'], dglu)
    dvec = W['s5_d'].reshape(1, -1)
    sargs = _s5_args(A, sv['prep'], dvec, S)
    res = scan_bwd(n("s5_scan_bwd"), _s5_tile, (S5_CHUNKS, S // S5_TILE), _S5_CARRY, sargs, sv['s5_ck'],
                   [Arg(dg, (S5_TILE, 128), lambda o, t: (t, o))], list(range(len(sargs))), bwd_fn=_s5_tile_bwd)
    dua, dprep, dd = res[0], res[1:9], res[9]
    gr['s5_d'] = dd.reshape(-1)
    pargs = _s5_prep_args(W)
    pouts = _s5_prep_outs()
    da_re, da_im, dls, dbtr, dbti, dctr, dcti = map_bwd(
        n("s5_prep_bwd"), _s5_prep_tile, (S5_CHUNKS,), pargs,
        [Arg(d, o.block, o.imap) for d, o in zip(dprep, pouts)], list(range(7)))
    gshape = (S5_GROUPS, S5_STATE)
    gr['s5_a_re'], gr['s5_a_im'] = da_re.reshape(gshape), da_im.reshape(gshape)
    gr['s5_log_step'] = dls.reshape(-1)
    gr['s5_b_re'] = dbtr.T.reshape(S5_GROUPS, S5_STATE, S5_GROUP)
    gr['s5_b_im'] = dbti.T.reshape(S5_GROUPS, S5_STATE, S5_GROUP)
    gr['s5_c_re'] = dctr.reshape(S5_GROUP, S5_GROUPS, S5_STATE).transpose(1, 0, 2)
    gr['s5_c_im'] = dcti.reshape(S5_GROUP, S5_GROUPS, S5_STATE).transpose(1, 0, 2)

    cargs = [_rows(t, 512) for t in sv['att']] + \
            [_rows(C, 512, col=SSD_CONV_DIM // ATT_GW, width=ATT_GW, gshape=(S, ATT_GW), gdtype=bf16)]
    cres = map_bwd(n("combine_bwd"), _combine_tile, (S // 512,), cargs, [_rows(dyb, 512)], list(range(7)))
    dzb = cres[6]
    qw, kw = W['q_norm_w'].reshape(1, -1), W['k_norm_w'].reshape(1, -1)
    dqs, dks, dvs = [], [], []
    dqw = dkw = None
    for gi, (window, r) in enumerate(ATT_PAIRS):
        L = S // r
        nq, rb = _attn_plan(r)
        dspec = lambda t: Arg(t.reshape(L, r * ATT_GW), (nq * ATT_BLOCK, rb * ATT_GW), lambda rho, nb: (nb, rho))
        dq, dk, dv, dqw_g, dkw_g = scan_bwd(n(f"attn{gi}_bwd"), _attn_tile, _attn_grid(r, S), _attn_carry(r),
                                            _attn_args(QKV, gi, r, qw, kw, S), sv['att_ck'][gi],
                                            [dspec(cres[2 * gi]), dspec(cres[2 * gi + 1])], [0, 1, 2, 3, 4])
        dqs.append(dq.reshape(S, ATT_GW))
        dks.append(dk.reshape(S, ATT_GW))
        dvs.append(dv.reshape(S, ATT_GW))
        dqw = dqw_g if dqw is None else dqw + dqw_g
        dkw = dkw_g if dkw is None else dkw + dkw_g
    gr['q_norm_w'], gr['k_norm_w'] = dqw.reshape(-1), dkw.reshape(-1)

    ssd_args = _ssd_args(C, A, W, S)
    sres = scan_bwd(n("ssd_bwd"), _ssd_tile, (1, S // SSD_CHUNK), _SSD_CARRY, ssd_args, sv['ssd_ck'],
                    [Arg(dyc, (SSD_CHUNK, SSD_WIDTH), lambda o, t: (t, 0))], list(range(9)), side=side)
    sres, got = sres if side else (sres, None)
    dxbc, ddt, dzc = sres[0], sres[1], sres[2]
    gr['conv_w'] = sres[3]
    gr['conv_b'] = sres[4].reshape(-1)
    gr['dt_bias'] = sres[5].reshape(-1)[:SSD_HEADS]
    gr['ssd_a_log'] = sres[6].reshape(-1)[:SSD_HEADS]
    gr['ssd_d'] = sres[7].reshape(-1)[:SSD_HEADS]
    gr['ssd_norm_w'] = sres[8].reshape(-1)

    dpieces = [jnp.concatenate([dua, dza, ddt], axis=1), jnp.concatenate(dqs + dks + dvs, axis=1),
               jnp.concatenate([dxbc, dzb, dzc], axis=1), dgates]
    gr['w_in'] = _unrelayout_w_in_grad([wgrad(n(f"g_w_in{j}"), sv['h'], dp) for j, dp in enumerate(dpieces)])
    dh = matmul_nt_sum(n("d_h"), dpieces, list(W['w_in_pieces']), side=own_scatter(gr) if own_scatter else None)
    dh, got_own = dh if own_scatter else (dh, None)
    nargs = [_rows(x, 512), _whole(W['norm_w'].reshape(1, -1), 'acc')]
    dx, dnw = map_bwd(n("norm_bwd"), _rmsnorm_tile, (S // 512,), nargs, [_rows(dh, 512)], [0, 1], add={0: _rows(dout, 512)})
    gr['norm_w'] = dnw.reshape(-1)
    return dx, gr, got, got_own


def _unrelayout_w_in_grad(pieces):
    gA, gQ, gC, gG = pieces
    uaza, dt = gA[:, :2 * S5_WIDTH], gA[:, 2 * S5_WIDTH:2 * S5_WIDTH + SSD_HEADS]
    xbc, zb, zc = gC[:, :SSD_CONV_DIM], gC[:, SSD_CONV_DIM:SSD_CONV_DIM + ATT_GW], gC[:, SSD_CONV_DIM + ATT_GW:]
    return jnp.concatenate([uaza, gQ, zb, xbc, dt, zc, gG], axis=1)


def kernel(x, norm_w, w_in, s5_a_re, s5_a_im, s5_log_step, s5_b_re, s5_b_im, s5_c_re, s5_c_im, s5_d, s5_glu_w, s5_glu_b, q_norm_w, k_norm_w, conv_w, conv_b, dt_bias, ssd_a_log, ssd_d, ssd_norm_w, proj_a, proj_b, proj_c, w_out, loss_target, m_norm_w, m_w_in, m_s5_a_re, m_s5_a_im, m_s5_log_step, m_s5_b_re, m_s5_b_im, m_s5_c_re, m_s5_c_im, m_s5_d, m_s5_glu_w, m_s5_glu_b, m_q_norm_w, m_k_norm_w, m_conv_w, m_conv_b, m_dt_bias, m_ssd_a_log, m_ssd_d, m_ssd_norm_w, m_proj_a, m_proj_b, m_proj_c, m_w_out, v_norm_w, v_w_in, v_s5_a_re, v_s5_a_im, v_s5_log_step, v_s5_b_re, v_s5_b_im, v_s5_c_re, v_s5_c_im, v_s5_d, v_s5_glu_w, v_s5_glu_b, v_q_norm_w, v_k_norm_w, v_conv_w, v_conv_b, v_dt_bias, v_ssd_a_log, v_ssd_d, v_ssd_norm_w, v_proj_a, v_proj_b, v_proj_c, v_w_out):
    args = dict(locals())
    w = {k: args[k] for k in WEIGHTS}
    m = {k: args['m_' + k] for k in WEIGHTS}
    v = {k: args['v_' + k] for k in WEIGHTS}
    depth = norm_w.shape[0]
    S = x.shape[1]
    xs = x.reshape(S, D_MODEL)
    tgt = loss_target.reshape(S, D_MODEL)

    others = [k for k in SHARDED if k != 'w_in']

    def weight_gather(li):
        return gather_side(list(_relayout_w_in(w['w_in'][li].astype(bf16))) + [w[k][li].astype(bf16) for k in others])

    def assemble(li, gathered):
        W = {k: w[k][li] for k in WEIGHTS if k not in SHARDED}
        W['w_in_pieces'] = [t.reshape(t.shape[0] * t.shape[1], t.shape[2]) for t in gathered[:4]]
        for k, t in zip(others, gathered[4:]):
            n_dev, R, C = t.shape
            W[k] = t.reshape(n_dev * R, C) if k in ROW_SHARDED else t.transpose(1, 0, 2).reshape(R, n_dev * C)
        return W

    layers = [assemble(0, run_side("gather_weights0", weight_gather(0)))]
    act, saved = xs, []
    for li in range(depth):
        act, sv, got = layer_fwd(li, act, layers[li], weight_gather(li + 1) if li + 1 < depth else None)
        saved.append(sv)
        if got is not None:
            layers.append(assemble(li + 1, got))
    dy, loss_local = loss_and_grad(act, tgt)
    loss = lax.psum(loss_local, ("x", "y", "c"))

    big = [k for k in SHARDED if k != 'conv_w']

    def grad_scatter(gr):
        return exchange_side([gr[k] for k in big], ['rows' if k in ROW_SHARDED else 'cols' for k in big])

    grads, slots = [None] * depth, [None] * depth
    for li in reversed(range(depth)):
        dy, grads[li], got, got_own = layer_bwd(li, dy, saved[li], layers[li], grad_scatter(grads[li + 1]) if li + 1 < depth else None,
                                                grad_scatter if li == 0 else None)
        if got is not None:
            slots[li + 1] = got
        if got_own is not None:
            slots[li] = got_own
    small_keys = [k for k in WEIGHTS if k not in SHARDED] + ['conv_w']
    stacked = [jnp.stack([grads[li][k] for li in range(depth)], axis=0) for k in small_keys]
    (small_slots,) = run_side("gather_small_grads", exchange_side([_pack(stacked)], ['all']))
    grad_x = dy.reshape(x.shape)
    result = {k: adamw_layers("adamw_" + k, w[k], [slots[li][j] for li in range(depth)], m[k], v[k]) for j, k in enumerate(big)}

    totals = _unpack(sum_slots("sum_small_grads", small_slots), [t.shape for t in stacked])
    for k, g in zip(small_keys, totals):
        if k == 'conv_w':
            width = w[k].shape[-1]
            me = 4 * lax.axis_index("x") + 2 * lax.axis_index("y") + lax.axis_index("c")
            g = lax.dynamic_slice_in_dim(g, me * width, width, axis=2)
        result[k] = adamw("adamw_" + k, w[k], g[None], m[k], v[k])

    return (loss, grad_x, *[result[k][0] for k in WEIGHTS], *[result[k][1] for k in WEIGHTS],
            *[result[k][2] for k in WEIGHTS], *[result[k][3] for k in WEIGHTS])
```

```python
import functools
import math
from typing import Any, NamedTuple

import jax
import jax.numpy as jnp
from jax import lax
from jax.experimental import pallas as pl
from jax.experimental.pallas import tpu as pltpu

f32 = jnp.float32
bf16 = jnp.bfloat16

N_DEV = 8
D_MODEL = 1024
RMS_EPS = 1e-6
S5_WIDTH = 512
S5_GROUPS = 32
S5_GROUP = 16
S5_STATE = 64
S5_TILE = 512
S5_SUB = 8
S5_ND = 3
S5_CHUNKS = 4
ATT_HEAD_DIM = 64
ATT_PAIRS = ((128, 1), (512, 4), (2048, 16))
ATT_HPG = 4
ATT_BLOCK = 128
ATT_GW = ATT_HPG * ATT_HEAD_DIM
ATT_WIDTH = 768
SSD_HEADS = 12
SSD_HEAD_DIM = 64
SSD_WIDTH = 768
SSD_STATE = 128
SSD_GROUPS = 2
SSD_CHUNK = 128
SSD_CONV = 4
SSD_CONV_DIM = 1280
HPAD = 128
IN_SPLITS = (512, 512, 768, 768, 768, 256, 1280, 12, 768, 3072)
ADAM_LR, ADAM_B1, ADAM_B2, ADAM_EPS, ADAM_WD, ADAM_STEP = 0.001, 0.9, 0.999, 1e-08, 0.01, 10
VMEM_LIMIT = 56 * 1024 * 1024

WEIGHTS = ['norm_w', 'w_in', 's5_a_re', 's5_a_im', 's5_log_step', 's5_b_re', 's5_b_im', 's5_c_re',
           's5_c_im', 's5_d', 's5_glu_w', 's5_glu_b', 'q_norm_w', 'k_norm_w', 'conv_w', 'conv_b',
           'dt_bias', 'ssd_a_log', 'ssd_d', 'ssd_norm_w', 'proj_a', 'proj_b', 'proj_c', 'w_out']
ROW_SHARDED = ('w_in', 's5_glu_w', 'w_out')
SHARDED = ROW_SHARDED + ('conv_w', 'proj_a', 'proj_b', 'proj_c')


class Arg(NamedTuple):
    arr: Any
    block: tuple
    imap: Any
    kind: str = 'const'
    gshape: Any = None
    gimap: Any = None
    gdtype: Any = None


class Out(NamedTuple):
    shape: tuple
    dtype: Any
    block: tuple
    imap: Any


def _cparams(n):
    return pltpu.CompilerParams(dimension_semantics=("arbitrary",) * n, vmem_limit_bytes=VMEM_LIMIT)


def _rows(a, tm, kind='tile', col=0, width=None, gshape=None, gcol=None, gdtype=None):
    width = a.shape[1] if width is None else width
    g = None if gshape is None else (lambda i, gc=(0 if gcol is None else gcol): (i, gc))
    return Arg(a, (tm, width), lambda i, c=col: (i, c), kind, gshape, g, gdtype)


def _whole(a, kind='const'):
    nd = a.ndim
    return Arg(a, a.shape, lambda *i, nd=nd: (0,) * nd, kind)


def map_fwd(name, fn, grid, args, outs):
    n_in = len(args)

    def body(*refs):
        pid = tuple(pl.program_id(a) for a in range(len(grid)))
        res = fn(pid, *[r[...] for r in refs[:n_in]])
        for o, r in zip(refs[n_in:], res):
            o[...] = r.astype(o.dtype)

    res = pl.pallas_call(
        body, name=name, grid=grid,
        in_specs=[pl.BlockSpec(a.block, a.imap) for a in args],
        out_specs=[pl.BlockSpec(o.block, o.imap) for o in outs],
        out_shape=[jax.ShapeDtypeStruct(o.shape, o.dtype) for o in outs],
        compiler_params=_cparams(len(grid)))(*[a.arr for a in args])
    return tuple(res)


def _grad_outs(args, wrt):
    outs = []
    for i in wrt:
        a = args[i]
        shape = a.arr.shape if a.gshape is None else a.gshape
        imap = a.imap if a.gimap is None else a.gimap
        outs.append(Out(shape, f32 if a.gdtype is None else a.gdtype, a.block, imap))
    return outs


def _store_grads(pid, args, wrt, grads, grefs, adds):
    first_all = functools.reduce(jnp.logical_and, [p == 0 for p in pid])
    first_in = functools.reduce(jnp.logical_and, [p == 0 for p in pid[1:]]) if len(pid) > 1 else first_all
    for j, i in enumerate(wrt):
        g = grads[j].astype(f32)
        ref = grefs[j]
        kind = args[i].kind
        if kind == 'tile':
            if j in adds:
                g = g + adds[j]
            ref[...] = g.astype(ref.dtype)
        else:
            first = first_all if kind == 'acc' else first_in

            @pl.when(first)
            def _(ref=ref):
                ref[...] = jnp.zeros_like(ref)

            ref[...] += g


def map_bwd(name, fn, grid, args, douts, wrt, add=None):
    add = add or {}
    n_in, n_d, n_add = len(args), len(douts), len(add)
    add_keys = sorted(add)
    gouts = _grad_outs(args, wrt)

    def body(*refs):
        pid = tuple(pl.program_id(a) for a in range(len(grid)))
        vals = [r[...] for r in refs[:n_in]]
        dvals = [r[...].astype(f32) for r in refs[n_in:n_in + n_d]]
        avals = {k: refs[n_in + n_d + j][...].astype(f32) for j, k in enumerate(add_keys)}
        grefs = refs[n_in + n_d + n_add:]

        def f(*w):
            full = list(vals)
            for i, x in zip(wrt, w):
                full[i] = x
            return tuple(fn(pid, *full))

        _, vjp = jax.vjp(f, *[vals[i] for i in wrt])
        grads = vjp(tuple(dvals))
        _store_grads(pid, args, wrt, grads, grefs, avals)

    ins = list(args) + list(douts) + [add[k] for k in add_keys]
    res = pl.pallas_call(
        body, name=name, grid=grid,
        in_specs=[pl.BlockSpec(a.block, a.imap) for a in ins],
        out_specs=[pl.BlockSpec(o.block, o.imap) for o in gouts],
        out_shape=[jax.ShapeDtypeStruct(o.shape, o.dtype) for o in gouts],
        compiler_params=_cparams(len(grid)))(*[a.arr for a in ins])
    return tuple(res)


class Side(NamedTuple):
    arrs: list
    out_shapes: list
    sem_shapes: list
    phases: list


def _run_side(side, step, total, src_refs, out_refs, sem_refs):
    for frac, phase in side.phases:
        @pl.when(step == int(round(frac * (total - 1))))
        def _(phase=phase):
            phase(src_refs, out_refs, *sem_refs)


_ANY = pl.BlockSpec(memory_space=pl.ANY)


def scan_fwd(name, fn, grid, carry_shapes, args, outs, side=None):
    no, nt = grid
    n_in, n_out, n_c = len(args), len(outs), len(carry_shapes)
    ns_in, ns_out = (len(side.arrs), len(side.out_shapes)) if side else (0, 0)
    cks = [Out((no, nt) + cs, f32, (None, None) + cs, lambda o, t, n=len(cs): (o, t) + (0,) * n) for cs in carry_shapes]

    def body(*refs):
        pid = (pl.program_id(0), pl.program_id(1))
        ins = refs[:n_in]
        sins = refs[n_in:n_in + ns_in]
        refs = refs[n_in + ns_in:]
        orefs = refs[:n_out]
        ckrefs = refs[n_out:n_out + n_c]
        souts = refs[n_out + n_c:n_out + n_c + ns_out]
        crefs = refs[n_out + n_c + ns_out:n_out + n_c + ns_out + n_c]
        if side:
            _run_side(side, pid[0] * nt + pid[1], no * nt, sins, souts, refs[n_out + n_c + ns_out + n_c:])

        @pl.when(pid[1] == 0)
        def _():
            for c in crefs:
                c[...] = jnp.zeros_like(c)

        carry = tuple(c[...] for c in crefs)
        for ck, c in zip(ckrefs, carry):
            ck[...] = c
        res, newc = fn(pid, carry, *[r[...] for r in ins])
        for o, r in zip(orefs, res):
            o[...] = r.astype(o.dtype)
        for c, v in zip(crefs, newc):
            c[...] = v

    allouts = list(outs) + cks
    res = pl.pallas_call(
        body, name=name, grid=grid,
        in_specs=[pl.BlockSpec(a.block, a.imap) for a in args] + [_ANY] * ns_in,
        out_specs=[pl.BlockSpec(o.block, o.imap) for o in allouts] + [_ANY] * ns_out,
        out_shape=[jax.ShapeDtypeStruct(o.shape, o.dtype) for o in allouts] + (list(side.out_shapes) if side else []),
        scratch_shapes=[pltpu.VMEM(cs, f32) for cs in carry_shapes] + (list(side.sem_shapes) if side else []),
        compiler_params=_cparams(2))(*[a.arr for a in args], *(side.arrs if side else []))
    if side:
        return tuple(res[:n_out]), tuple(res[n_out:n_out + n_c]), list(res[n_out + n_c:])
    return tuple(res[:n_out]), tuple(res[n_out:])


def scan_bwd(name, fn, grid, carry_shapes, args, ckpts, douts, wrt, bwd_fn=None, side=None):
    no, nt = grid
    n_in, n_d, n_c = len(args), len(douts), len(carry_shapes)
    ns_in, ns_out = (len(side.arrs), len(side.out_shapes)) if side else (0, 0)

    def rev(imap):
        return lambda o, t: imap(o, nt - 1 - t)

    rargs = [a._replace(imap=rev(a.imap), gimap=None if a.gimap is None else rev(a.gimap)) for a in args]
    rdouts = [a._replace(imap=rev(a.imap)) for a in douts]
    ckargs = [Arg(ck, (None, None) + cs, rev(lambda o, t, n=len(cs): (o, t) + (0,) * n)) for ck, cs in zip(ckpts, carry_shapes)]
    gouts = _grad_outs(rargs, wrt)

    def body(*refs):
        o, t = pl.program_id(0), pl.program_id(1)
        tt = nt - 1 - t
        vals = [r[...] for r in refs[:n_in]]
        dvals = [r[...].astype(f32) for r in refs[n_in:n_in + n_d]]
        carry = tuple(r[...] for r in refs[n_in + n_d:n_in + n_d + n_c])
        sins = refs[n_in + n_d + n_c:n_in + n_d + n_c + ns_in]
        refs = refs[n_in + n_d + n_c + ns_in:]
        grefs = refs[:len(wrt)]
        souts = refs[len(wrt):len(wrt) + ns_out]
        dcrefs = refs[len(wrt) + ns_out:len(wrt) + ns_out + n_c]
        if side:
            _run_side(side, o * nt + t, no * nt, sins, souts, refs[len(wrt) + ns_out + n_c:])

        @pl.when(t == 0)
        def _():
            for c in dcrefs:
                c[...] = jnp.zeros_like(c)

        def f(carry, *w):
            full = list(vals)
            for i, x in zip(wrt, w):
                full[i] = x
            res, newc = fn((o, tt), carry, *full)
            return tuple(res), tuple(newc)

        dcarry = tuple(c[...] for c in dcrefs)
        if bwd_fn is None:
            _, vjp = jax.vjp(f, carry, *[vals[i] for i in wrt])
            grads = vjp((tuple(dvals), dcarry))
            dcarry_in, grads = grads[0], grads[1:]
        else:
            dcarry_in, grads = bwd_fn((o, tt), carry, vals, dvals, dcarry)
        for c, g in zip(dcrefs, dcarry_in):
            c[...] = g
        _store_grads((o, t), rargs, wrt, grads, grefs, {})

    ins = rargs + rdouts + ckargs
    res = pl.pallas_call(
        body, name=name, grid=grid,
        in_specs=[pl.BlockSpec(a.block, a.imap) for a in ins] + [_ANY] * ns_in,
        out_specs=[pl.BlockSpec(g.block, g.imap) for g in gouts] + [_ANY] * ns_out,
        out_shape=[jax.ShapeDtypeStruct(g.shape, g.dtype) for g in gouts] + (list(side.out_shapes) if side else []),
        scratch_shapes=[pltpu.VMEM(cs, f32) for cs in carry_shapes] + (list(side.sem_shapes) if side else []),
        compiler_params=_cparams(2))(*[a.arr for a in ins], *(side.arrs if side else []))
    if side:
        return tuple(res[:len(gouts)]), list(res[len(gouts):])
    return tuple(res)


def _pick(dim, target):
    if dim <= target:
        return dim
    for t in range(target // 128 * 128, 127, -128):
        if dim % t == 0:
            return t
    return dim


def matmul(name, a, b, mode='nn', add=None, out_dtype=f32, tm=None, tn=1152, tk=None):
    if mode == 'tn':
        K, M = a.shape
    else:
        M, K = a.shape
    N = b.shape[0] if mode == 'nt' else b.shape[1]
    assert (b.shape[1] if mode == 'nt' else b.shape[0]) == K
    tm = (1024 if mode == 'tn' else 512) if tm is None else tm
    tk = (1024 if mode == 'tn' else 1152) if tk is None else tk
    tm, tn, tk = _pick(M, tm), _pick(N, tn), _pick(K, tk)
    nk = K // tk
    a_spec = pl.BlockSpec((tk, tm), lambda i, j, k: (k, i)) if mode == 'tn' else pl.BlockSpec((tm, tk), lambda i, j, k: (i, k))
    b_spec = pl.BlockSpec((tn, tk), lambda i, j, k: (j, k)) if mode == 'nt' else pl.BlockSpec((tk, tn), lambda i, j, k: (k, j))
    dims = {'nn': (((1,), (0,)), ((), ())), 'nt': (((1,), (1,)), ((), ())), 'tn': (((0,), (0,)), ((), ()))}[mode]
    has_add = add is not None

    def body(*refs):
        if has_add:
            a_ref, b_ref, add_ref, o_ref, acc = refs
        else:
            a_ref, b_ref, o_ref, acc = refs
        k = pl.program_id(2)
        prod = lax.dot_general(a_ref[...].astype(bf16), b_ref[...].astype(bf16), dims, preferred_element_type=f32)
        if nk == 1:
            o_ref[...] = (prod + add_ref[...].astype(f32) if has_add else prod).astype(o_ref.dtype)
            return

        @pl.when(k == 0)
        def _():
            acc[...] = add_ref[...].astype(f32) if has_add else jnp.zeros_like(acc)

        acc[...] += prod

        @pl.when(k == nk - 1)
        def _():
            o_ref[...] = acc[...].astype(o_ref.dtype)

    in_specs = [a_spec, b_spec] + ([pl.BlockSpec((tm, tn), lambda i, j, k: (i, j))] if has_add else [])
    ops = [a, b] + ([add] if has_add else [])
    return pl.pallas_call(
        body, name=name, grid=(M // tm, N // tn, nk), in_specs=in_specs,
        out_specs=pl.BlockSpec((tm, tn), lambda i, j, k: (i, j)),
        out_shape=jax.ShapeDtypeStruct((M, N), out_dtype),
        scratch_shapes=[pltpu.VMEM((tm, tn), f32)],
        compiler_params=pltpu.CompilerParams(dimension_semantics=("parallel", "parallel", "arbitrary"), vmem_limit_bytes=VMEM_LIMIT))(*ops)


def matmul_nt_sum(name, lhs, rhs, tm=1024, tk=768, side=None):
    M, N = lhs[0].shape[0], rhs[0].shape[0]
    tm = _pick(M, tm)
    tks = [_pick(a.shape[1], tk) for a in lhs]
    starts, total = [], 0
    for a, t in zip(lhs, tks):
        starts.append(total)
        total += a.shape[1] // t
    npc = len(lhs)
    ns_in, ns_out = (len(side.arrs), len(side.out_shapes)) if side else (0, 0)

    def body(*refs):
        a_refs, b_refs, sins = refs[:npc], refs[npc:2 * npc], refs[2 * npc:2 * npc + ns_in]
        refs = refs[2 * npc + ns_in:]
        o_ref, souts, acc = refs[0], refs[1:1 + ns_out], refs[1 + ns_out]
        k = pl.program_id(1)
        if side:
            _run_side(side, pl.program_id(0) * total + k, (M // tm) * total, sins, souts, refs[2 + ns_out:])

        @pl.when(k == 0)
        def _():
            acc[...] = jnp.zeros_like(acc)

        for p in range(npc):
            @pl.when((k >= starts[p]) & (k < starts[p] + lhs[p].shape[1] // tks[p]))
            def _(p=p):
                acc[...] += lax.dot_general(a_refs[p][...].astype(bf16), b_refs[p][...].astype(bf16), _NT, preferred_element_type=f32)

        @pl.when(k == total - 1)
        def _():
            o_ref[...] = acc[...]

    def kblock(p):
        return lambda k: jnp.clip(k - starts[p], 0, lhs[p].shape[1] // tks[p] - 1)

    in_specs = [pl.BlockSpec((tm, tks[p]), lambda i, k, kb=kblock(p): (i, kb(k))) for p in range(npc)]
    in_specs += [pl.BlockSpec((N, tks[p]), lambda i, k, kb=kblock(p): (0, kb(k))) for p in range(npc)]
    res = pl.pallas_call(
        body, name=name, grid=(M // tm, total), in_specs=in_specs + [_ANY] * ns_in,
        out_specs=[pl.BlockSpec((tm, N), lambda i, k: (i, 0))] + [_ANY] * ns_out,
        out_shape=[jax.ShapeDtypeStruct((M, N), f32)] + (list(side.out_shapes) if side else []),
        scratch_shapes=[pltpu.VMEM((tm, N), f32)] + (list(side.sem_shapes) if side else []),
        compiler_params=_cparams(2))(*lhs, *rhs, *(side.arrs if side else []))
    return (res[0], list(res[1:])) if side else res[0]


def wgrad(name, act, dout):
    return matmul(name, act, dout, 'tn', out_dtype=bf16)


def _dot(a, b, dims=(((1,), (0,)), ((), ()))):
    return lax.dot_general(a.astype(bf16), b.astype(bf16), dims, preferred_element_type=f32)


_NT = (((1,), (1,)), ((), ()))
_TN = (((0,), (0,)), ((), ()))


def _three_term_dot(v, sel, dims):
    hi = v.astype(bf16)
    rest = v - hi.astype(f32)
    mid = rest.astype(bf16)
    lo = (rest - mid.astype(f32)).astype(bf16)
    dot = lambda t: lax.dot_general(t, sel, dims, preferred_element_type=f32)
    return dot(hi) + dot(mid) + dot(lo)


@jax.custom_vjp
def _dot_exact01(v, sel):
    return _three_term_dot(v, sel, (((1,), (0,)), ((), ())))


def _dot_exact01_fwd(v, sel):
    return _dot_exact01(v, sel), sel


def _dot_exact01_bwd(sel, ct):
    return _three_term_dot(ct, sel, _NT), jnp.zeros_like(sel)


_dot_exact01.defvjp(_dot_exact01_fwd, _dot_exact01_bwd)


def _spread_heads(v, width):
    r = lax.broadcasted_iota(jnp.int32, (HPAD, SSD_HEADS * width), 0)
    c = lax.broadcasted_iota(jnp.int32, (HPAD, SSD_HEADS * width), 1)
    return _dot_exact01(v, (r == c // width).astype(bf16))


def _rmsnorm_tile(pid, x, w):
    return (x * lax.rsqrt(jnp.mean(x * x, axis=-1, keepdims=True) + RMS_EPS) * w,)


def _shift_rows(h, d, fill):
    pad = jnp.full((d, h.shape[1]), fill, f32)
    return jnp.concatenate([pad, h[:-d]], axis=0)


def _s5_prep_tile(pid, a_re, a_im, ls, btr, bti, ctr, cti):
    o = pid[0]
    w = a_re.shape[1]
    r = lax.broadcasted_iota(jnp.int32, (S5_GROUPS, w), 0)
    c = lax.broadcasted_iota(jnp.int32, (S5_GROUPS, w), 1)
    sel = (r == o * (w // S5_STATE) + c // S5_STATE).astype(f32)
    step = jnp.dot(jnp.exp(ls), sel, precision=lax.Precision.HIGHEST, preferred_element_type=f32)
    mag = jnp.exp(a_re * step)
    ang = a_im * step
    lr, li = mag * jnp.cos(ang), mag * jnp.sin(ang)
    nr, ni = lr - 1.0, li
    den = a_re * a_re + a_im * a_im
    fr = (nr * a_re + ni * a_im) / den
    fi = (ni * a_re - nr * a_im) / den
    bbr = fr * btr - fi * bti
    bbi = fr * bti + fi * btr
    reps = w // S5_STATE
    rr = lax.broadcasted_iota(jnp.int32, (reps * S5_GROUP, w), 0)
    cc = lax.broadcasted_iota(jnp.int32, (reps * S5_GROUP, w), 1)
    diag = (rr // S5_GROUP) == (cc // S5_STATE)

    def expand(m):
        return jnp.where(diag, jnp.concatenate([m] * reps, axis=0), 0.0)

    pr, pi = lr, li
    rows_r, rows_i = [pr], [pi]
    for _ in range(S5_ND - 1):
        pr, pi = pr * pr - pi * pi, 2.0 * pr * pi
        rows_r.append(pr)
        rows_i.append(pi)
    lamd_r, lamd_i = jnp.concatenate(rows_r, axis=0), jnp.concatenate(rows_i, axis=0)
    tr = jnp.broadcast_to(lr, (S5_SUB, w))
    ti = jnp.broadcast_to(li, (S5_SUB, w))
    for j in range(S5_ND):
        sr, si = _shift_rows(tr, 1 << j, 1.0), _shift_rows(ti, 1 << j, 0.0)
        tr, ti = tr * sr - ti * si, tr * si + ti * sr
    return lamd_r, lamd_i, tr, ti, expand(bbr), expand(bbi), expand(ctr), expand(cti)


def _s5_tile(pid, carry, u, lamd_r, lamd_i, lam8_r, lam8_i, bbr, bbi, ccr, cci, dvec):
    hr, hi = _s5_scan(_dot(u, bbr), _dot(u, bbi), carry, lamd_r, lamd_i, lam8_r, lam8_i, reverse=False)
    return (_s5_readout(hr, hi, u, ccr, cci, dvec),), (hr[-1:], hi[-1:])


def _s5_readout(hr, hi, u, ccr, cci, dvec):
    return jax.nn.gelu(_dot(hr, ccr, _NT) - _dot(hi, cci, _NT) + dvec * u)


def _s5_scan(xr, xi, carry, lamd_r, lamd_i, lam8_r, lam8_i, reverse):
    cr, ci = carry
    T, G = xr.shape[0], S5_SUB
    sign = -1.0 if reverse else 1.0
    sub = lax.broadcasted_iota(jnp.int32, (G, 1), 0)
    xr, xi = xr.reshape(T // G, G, xr.shape[1]), xi.reshape(T // G, G, xi.shape[1])
    for j in range(S5_ND):
        d = 1 << j
        keep = (sub < G - d) if reverse else (sub >= d)
        ar = jnp.where(keep, lamd_r[j:j + 1], 0.0)
        ai = jnp.where(keep, sign * lamd_i[j:j + 1], 0.0)
        sr = pltpu.roll(xr, G - d if reverse else d, axis=1)
        si = pltpu.roll(xi, G - d if reverse else d, axis=1)
        xr, xi = xr + ar * sr - ai * si, xi + ar * si + ai * sr
    xr, xi = xr.reshape(T, xr.shape[2]), xi.reshape(T, xi.shape[2])
    if reverse:
        pr = jnp.concatenate([lam8_r[G - 1 - s:G - s] for s in range(G)], axis=0)
        pi = -jnp.concatenate([lam8_i[G - 1 - s:G - s] for s in range(G)], axis=0)
    else:
        pr, pi = lam8_r, lam8_i
    n = T // G
    rows_r, rows_i = [None] * n, [None] * n
    for i in (reversed(range(n)) if reverse else range(n)):
        gr_, gi_ = xr[i * G:(i + 1) * G], xi[i * G:(i + 1) * G]
        gr_, gi_ = gr_ + pr * cr - pi * ci, gi_ + pr * ci + pi * cr
        cr, ci = (gr_[:1], gi_[:1]) if reverse else (gr_[G - 1:], gi_[G - 1:])
        rows_r[i], rows_i[i] = gr_, gi_
    return jnp.concatenate(rows_r, axis=0), jnp.concatenate(rows_i, axis=0)


def _s5_tile_bwd(pid, carry, vals, douts, dcarry):
    u, lamd_r, lamd_i, lam8_r, lam8_i, bbr, bbi, ccr, cci, dvec = vals
    (dg,) = douts
    hr, hi = _s5_scan(_dot(u, bbr), _dot(u, bbi), carry, lamd_r, lamd_i, lam8_r, lam8_i, reverse=False)
    _, vjp = jax.vjp(_s5_readout, hr, hi, u, ccr, cci, dvec)
    dhr, dhi, du, dccr, dcci, ddvec = vjp(dg)
    Hr, Hi = _s5_scan(dhr, dhi, dcarry, lamd_r, lamd_i, lam8_r, lam8_i, reverse=True)
    _, vjp_in = jax.vjp(lambda u, bbr, bbi: (_dot(u, bbr), _dot(u, bbi)), u, bbr, bbi)
    du2, dbbr, dbbi = vjp_in((Hr, Hi))
    pr = jnp.concatenate([carry[0], hr[:-1]], axis=0)
    pi = jnp.concatenate([carry[1], hi[:-1]], axis=0)
    dlam_r = jnp.sum(Hr * pr + Hi * pi, axis=0, keepdims=True)
    dlam_i = jnp.sum(Hi * pr - Hr * pi, axis=0, keepdims=True)
    zrow = jnp.zeros((S5_ND - 1, dlam_r.shape[1]), f32)
    dlamd_r, dlamd_i = jnp.concatenate([dlam_r, zrow], axis=0), jnp.concatenate([dlam_i, zrow], axis=0)
    grads = (du + du2, dlamd_r, dlamd_i, jnp.zeros_like(lam8_r), jnp.zeros_like(lam8_i), dbbr, dbbi, dccr, dcci, ddvec)
    return (Hr[:1], Hi[:1]), grads


def _glu_tile(pid, g, glu, za, b):
    return (g * jax.nn.sigmoid(glu + b) * jax.nn.silu(za),)


def _attn_tile(pid, carry, q, k, v, qw, kw):
    n = pid[1]
    kp, vp = carry
    D, B = ATT_HEAD_DIM, ATT_BLOCK
    W = 2 * D
    nq, ncol = q.shape[0] // B, q.shape[1] // W
    r = lax.broadcasted_iota(jnp.int32, (B, 2 * B), 0)
    c = lax.broadcasted_iota(jnp.int32, (B, 2 * B), 1)
    diff = r + B - c
    band = (diff >= 0) & (diff <= B)
    band_first = band & ((c >= B) | (n > 0))
    low = lax.broadcasted_iota(jnp.int32, (1, W), 1) < D
    same_head = (lax.broadcasted_iota(jnp.int32, (W, W), 0) // D == lax.broadcasted_iota(jnp.int32, (W, W), 1) // D).astype(bf16)

    def hnorm(x, w):
        rows = x.shape[0]
        t = jnp.concatenate([x[:, j * W:(j + 1) * W] for j in range(ncol)], axis=0) if ncol > 1 else x
        ms = _dot_exact01(t * t, same_head) * (1.0 / D)
        t = t * lax.rsqrt(ms + RMS_EPS) * jnp.concatenate([w, w], axis=1)
        return [t[j * rows:(j + 1) * rows] for j in range(ncol)]

    qns, kns = hnorm(q, qw), hnorm(k, kw)
    out_cols, lse_cols, kn_cols = [], [], []
    for j in range(ncol):
        sl = slice(j * W, (j + 1) * W)
        qn, kn, vj = qns[j], kns[j], v[:, sl]
        kn_cols.append(kn[(nq - 1) * B:])
        outs, lses = [], []
        for b in range(nq):
            rows = slice(b * B, (b + 1) * B)
            prev = slice((b - 1) * B, b * B)
            kk = jnp.concatenate([kp[:, sl] if b == 0 else kn[prev], kn[rows]], axis=0)
            vv = jnp.concatenate([vp[:, sl] if b == 0 else vj[prev], vj[rows]], axis=0)
            o2, l2 = [], []
            for head_lanes in (low, ~low):
                s = _dot(jnp.where(head_lanes, qn[rows], 0.0), kk, _NT) * (D ** -0.5)
                s = jnp.where(band_first if b == 0 else band, s, -1e30)
                m = jnp.max(s, axis=-1, keepdims=True)
                p = jnp.exp(s - m)
                l = jnp.sum(p, axis=-1, keepdims=True)
                o2.append(_dot(p / l, vv))
                l2.append(m + jnp.log(l))
            outs.append(jnp.where(low, o2[0], o2[1]))
            lses.append(jnp.where(low, l2[0], l2[1]))
        out_cols.append(jnp.concatenate(outs, axis=0) if nq > 1 else outs[0])
        lse_cols.append(jnp.concatenate(lses, axis=0) if nq > 1 else lses[0])
    return ((jnp.concatenate(out_cols, axis=1), jnp.concatenate(lse_cols, axis=1)),
            (jnp.concatenate(kn_cols, axis=1), v[(nq - 1) * B:]))


def _combine_tile(pid, o1, l1, o2, l2, o3, l3, zb):
    m = jnp.maximum(jnp.maximum(l1, l2), l3)
    e1, e2, e3 = jnp.exp(l1 - m), jnp.exp(l2 - m), jnp.exp(l3 - m)
    y = (e1 * o1 + e2 * o2 + e3 * o3) / (e1 + e2 + e3)
    return (y * jax.nn.silu(zb),)


def _softplus(x):
    return jnp.maximum(x, 0.0) + jnp.log(1.0 + jnp.exp(-jnp.abs(x)))


def _ssd_tile(pid, carry, xbc, dt, z, conv_w, conv_b, dt_bias, a_log, dvec, norm_w):
    xprev, state = carry
    T, P, N = SSD_CHUNK, SSD_HEAD_DIM, SSD_STATE
    xx = jnp.concatenate([xprev, xbc], axis=0)
    conv = conv_b
    for k in range(SSD_CONV):
        off = 8 - (SSD_CONV - 1) + k
        conv = conv + conv_w[k:k + 1] * xx[off:off + T]
    xc = jax.nn.silu(conv)
    dtp = _softplus(dt + dt_bias)
    a_dt = dtp * (-jnp.exp(a_log))
    r = lax.broadcasted_iota(jnp.int32, (T, T), 0)
    c = lax.broadcasted_iota(jnp.int32, (T, T), 1)
    tri = r >= c
    trif = tri.astype(f32)
    hi = lax.Precision.HIGHEST
    a_cs = jnp.dot(trif, a_dt, precision=hi, preferred_element_type=f32)
    a_cs_t = lax.dot_general(a_dt, trif, (((0,), (1,)), ((), ())), precision=hi, preferred_element_type=f32)
    xs = xc[:, :SSD_WIDTH]
    acs_p = _spread_heads(a_cs, P)
    xdt = xs * _spread_heads(dtp, P)
    skip = _spread_heads(dvec, P)
    to_end = jnp.exp(acs_p[T - 1:T] - acs_p)
    low = lax.broadcasted_iota(jnp.int32, (1, 2 * P), 1) < P
    low_rows = lax.broadcasted_iota(jnp.int32, (2 * P, 1), 0) < P
    ys, states = [], []
    for j in range(SSD_HEADS // 2):
        g = 2 * j // (SSD_HEADS // SSD_GROUPS)
        if 2 * j % (SSD_HEADS // SSD_GROUPS) == 0:
            bg = xc[:, SSD_WIDTH + g * N:SSD_WIDTH + (g + 1) * N]
            cg = xc[:, SSD_WIDTH + SSD_GROUPS * N + g * N:SSD_WIDTH + SSD_GROUPS * N + (g + 1) * N]
            cb = _dot(cg, bg, _NT)
        lanes = slice(2 * j * P, 2 * (j + 1) * P)
        st = state[lanes, :]
        diag, last = [], []
        for h in (2 * j, 2 * j + 1):
            decay = jnp.exp(jnp.where(tri, a_cs[:, h:h + 1] - a_cs_t[h:h + 1, :], -1e30))
            diag.append(_dot(cb * decay, xdt[:, lanes]))
            last.append(jnp.exp(a_cs_t[h:h + 1, T - 1:T]))
        y = (jnp.where(low, diag[0], diag[1]) + _dot(cg, st, _NT) * jnp.exp(acs_p[:, lanes])
             + xs[:, lanes] * skip[:, lanes])
        ys.append(y)
        states.append(jnp.where(low_rows, last[0], last[1]) * st + _dot(xdt[:, lanes] * to_end[:, lanes], bg, _TN))
    y = jnp.concatenate(ys, axis=1) * jax.nn.silu(z)
    out = y * lax.rsqrt(jnp.mean(y * y, axis=-1, keepdims=True) + RMS_EPS) * norm_w
    return (out,), (xbc[T - 8:], jnp.concatenate(states, axis=0))


def _merge_tile(pid, pa, pb, pc, gates):
    d = pa.shape[1]
    g = jax.nn.sigmoid(gates)
    return (g[:, :d] * pa + g[:, d:2 * d] * pb + g[:, 2 * d:] * pc,)


def loss_and_grad(y, target, tm=512):
    S, D = y.shape
    nt = S // tm

    def body(y_ref, t_ref, dy_ref, l_ref, acc):
        i = pl.program_id(0)

        @pl.when(i == 0)
        def _():
            acc[...] = jnp.zeros_like(acc)

        diff = y_ref[...] - t_ref[...]
        dy_ref[...] = diff * (1.0 / D)
        acc[...] += jnp.sum((diff * diff).reshape(tm // 8, 8, D), axis=0)

        @pl.when(i == nt - 1)
        def _():
            l_ref[...] = jnp.broadcast_to(0.5 / D * jnp.sum(acc[...]), l_ref.shape)

    dy, l = pl.pallas_call(
        body, name="loss_head", grid=(nt,),
        in_specs=[pl.BlockSpec((tm, D), lambda i: (i, 0))] * 2,
        out_specs=[pl.BlockSpec((tm, D), lambda i: (i, 0)), pl.BlockSpec((8, 128), lambda i: (0, 0))],
        out_shape=[jax.ShapeDtypeStruct((S, D), f32), jax.ShapeDtypeStruct((8, 128), f32)],
        scratch_shapes=[pltpu.VMEM((8, D), f32)],
        compiler_params=_cparams(1))(y, target)
    return dy, l[0, 0]


def _row_tile(R, C, budget=1 << 20):
    best = R
    for t in range(8, R, 8):
        if R % t == 0 and t * C * 4 <= budget:
            best = t
    if best == R and R * C * 4 > budget:
        for t in range(8, R, 8):
            if R % t == 0:
                return t
    return best


def _as2d(t, lead=0):
    return t.reshape(t.shape[:lead] + (math.prod(t.shape[lead:-1]), t.shape[-1]))


def adamw_layers(name, w, slots, m, v):
    L, R, C = w.shape
    n = slots[0].shape[0]
    lanes = -(-C // 128) * 128
    tr = _row_tile(R, lanes * (n * L + 7), budget=10 << 20)

    def body(*refs):
        w_ref, m_ref, v_ref = refs[0], refs[1 + L], refs[2 + L]
        go_ref, d_ref, nm_ref, nv_ref = refs[3 + L:]
        layer = pl.program_id(0)
        gg = None
        for l in range(L):
            s = refs[1 + l][0].astype(f32)
            for j in range(1, n):
                s = s + refs[1 + l][j].astype(f32)
            gg = s if gg is None else jnp.where(layer == l, s, gg)
        go_ref[...] = gg
        nm = ADAM_B1 * m_ref[...] + (1.0 - ADAM_B1) * gg
        nv = ADAM_B2 * v_ref[...] + (1.0 - ADAM_B2) * jnp.square(gg)
        m_hat = nm / (1.0 - ADAM_B1 ** ADAM_STEP)
        v_hat = nv / (1.0 - ADAM_B2 ** ADAM_STEP)
        d_ref[...] = -ADAM_LR * (m_hat / (jnp.sqrt(v_hat) + ADAM_EPS) + ADAM_WD * w_ref[...])
        nm_ref[...] = nm
        nv_ref[...] = nv

    spec = pl.BlockSpec((None, tr, C), lambda l, i: (l, i, 0))
    slot_specs = [pl.BlockSpec((n, tr, C), lambda l, i, own=own: (0, jnp.where(l == own, i, 0), 0)) for own in range(L)]
    res = pl.pallas_call(
        body, name=name, grid=(L, R // tr),
        in_specs=[spec] + slot_specs + [spec, spec], out_specs=[spec] * 4,
        out_shape=[jax.ShapeDtypeStruct((L, R, C), f32)] * 4,
        compiler_params=_cparams(2))(w, *slots, m, v)
    return tuple(res)


def adamw(name, w, gslots, m, v):
    shape = w.shape
    n = gslots.shape[0]
    C = shape[-1]
    R = math.prod(shape[:-1])
    lanes = -(-C // 128) * 128
    tr = _row_tile(R, lanes * (n + 7), budget=10 << 20)

    def body(w_ref, g_ref, m_ref, v_ref, go_ref, d_ref, nm_ref, nv_ref):
        gg = g_ref[0].astype(f32)
        for s in range(1, n):
            gg = gg + g_ref[s].astype(f32)
        go_ref[...] = gg
        nm = ADAM_B1 * m_ref[...] + (1.0 - ADAM_B1) * gg
        nv = ADAM_B2 * v_ref[...] + (1.0 - ADAM_B2) * jnp.square(gg)
        m_hat = nm / (1.0 - ADAM_B1 ** ADAM_STEP)
        v_hat = nv / (1.0 - ADAM_B2 ** ADAM_STEP)
        d_ref[...] = -ADAM_LR * (m_hat / (jnp.sqrt(v_hat) + ADAM_EPS) + ADAM_WD * w_ref[...])
        nm_ref[...] = nm
        nv_ref[...] = nv

    spec = pl.BlockSpec((tr, C), lambda i: (i, 0))
    res = pl.pallas_call(
        body, name=name, grid=(R // tr,),
        in_specs=[spec, pl.BlockSpec((n, tr, C), lambda i: (0, i, 0)), spec, spec], out_specs=[spec] * 4,
        out_shape=[jax.ShapeDtypeStruct((R, C), f32)] * 4,
        compiler_params=_cparams(1))(w.reshape(R, C), gslots.reshape(n, R, C), m.reshape(R, C), v.reshape(R, C))
    return tuple(t.reshape(shape) for t in res)


PACK_ROWS = 256


def sum_slots(name, x):
    n, R, C = x.shape

    def body(x_ref, o_ref):
        acc = x_ref[0]
        for s in range(1, n):
            acc = acc + x_ref[s]
        o_ref[...] = acc

    return pl.pallas_call(
        body, name=name, grid=(R // PACK_ROWS,),
        in_specs=[pl.BlockSpec((n, PACK_ROWS, C), lambda i: (0, i, 0))],
        out_specs=pl.BlockSpec((PACK_ROWS, C), lambda i: (i, 0)),
        out_shape=jax.ShapeDtypeStruct((R, C), f32), compiler_params=_cparams(1))(x)


def _pack(parts):
    flat = jnp.concatenate([p.reshape(-1) for p in parts])
    unit = 128 * PACK_ROWS
    tot = -(-flat.shape[0] // unit) * unit
    return jnp.pad(flat, (0, tot - flat.shape[0])).reshape(tot // 128, 128)


def _unpack(buf, shapes):
    flat = buf.reshape(-1)
    out, off = [], 0
    for s in shapes:
        size = math.prod(s)
        out.append(flat[off:off + size].reshape(s))
        off += size
    return out


def _comm_sems(nt):
    return [pltpu.SemaphoreType.DMA((nt, N_DEV - 1)), pltpu.SemaphoreType.DMA((nt, N_DEV - 1)), pltpu.SemaphoreType.DMA((nt,))]


def exchange_side(srcs, modes):
    nt = len(srcs)
    slabs = []
    for s, mode in zip(srcs, modes):
        R, C = s.shape
        slabs.append({'all': (R, C), 'rows': (R // N_DEV, C), 'cols': (R, C // N_DEV)}[mode])

    def piece(ref, mode, slab, p):
        if mode == 'all':
            return ref
        if mode == 'rows':
            return ref.at[pl.ds(p * slab[0], slab[0]), :]
        return ref.at[:, pl.ds(p * slab[1], slab[1])]

    def copies(src_refs, out_refs, send_sems, recv_sems, local_sems):
        x, y, c = lax.axis_index("x"), lax.axis_index("y"), lax.axis_index("c")
        me = 4 * x + 2 * y + c
        out = []
        for k in (1, 2, 4, 3, 5, 6, 7):
            px = 1 - x if k & 4 else x
            py = 1 - y if k & 2 else y
            pc = 1 - c if k & 1 else c
            for t in range(nt):
                out.append(pltpu.make_async_remote_copy(
                    src_ref=piece(src_refs[t], modes[t], slabs[t], 4 * px + 2 * py + pc), dst_ref=out_refs[t].at[me],
                    send_sem=send_sems.at[t, k - 1], recv_sem=recv_sems.at[t, k - 1],
                    device_id=(px, py, pc), device_id_type=pl.DeviceIdType.MESH))
        for t in range(nt):
            out.append(pltpu.make_async_copy(piece(src_refs[t], modes[t], slabs[t], me), out_refs[t].at[me], local_sems.at[t]))
        return out

    def start(*refs):
        for cp in copies(*refs):
            cp.start()

    def finish(*refs):
        for cp in copies(*refs):
            cp.wait()

    out_shapes = [jax.ShapeDtypeStruct((N_DEV,) + sl, s.dtype) for s, sl in zip(srcs, slabs)]
    return Side(list(srcs), out_shapes, _comm_sems(nt), [(0.0, start), (1.0, finish)])


def gather_side(srcs):
    nt = len(srcs)

    def plan(src_refs, out_refs, send_sems, recv_sems, local_sems):
        x, y, c = lax.axis_index("x"), lax.axis_index("y"), lax.axis_index("c")
        me, sibling = (x, y, c), (x, y, 1 - c)
        chips = [(1 - x, y), (x, 1 - y), (1 - x, 1 - y)]

        def slot(t, dev):
            return out_refs[t].at[4 * dev[0] + 2 * dev[1] + dev[2]]

        def copy(t, k, block, to, src=None):
            return pltpu.make_async_remote_copy(
                src_ref=slot(t, block) if src is None else src, dst_ref=slot(t, block),
                send_sem=send_sems.at[t, k], recv_sem=recv_sems.at[t, k], device_id=to, device_id_type=pl.DeviceIdType.MESH)

        mine = [pltpu.make_async_copy(src_refs[t], slot(t, me), local_sems.at[t]) for t in range(nt)]
        first = []
        for t in range(nt):
            first.append(copy(t, 0, me, sibling, src=src_refs[t]))
            first += [copy(t, 1 + j, me, (*chip, c), src=src_refs[t]) for j, chip in enumerate(chips)]
        landed = [copy(t, 1 + j, (*chip, c), me) for j, chip in enumerate(chips) for t in range(nt)]
        passed = [copy(t, 4 + j, (*chip, c), sibling) for j, chip in enumerate(chips) for t in range(nt)]
        from_sibling = [copy(t, 0, sibling, me) for t in range(nt)]
        from_sibling += [copy(t, 4 + j, (*chip, 1 - c), me) for t in range(nt) for j, chip in enumerate(chips)]
        return mine, first, landed, passed, from_sibling

    def start(*refs):
        mine, first, _, _, _ = plan(*refs)
        for cp in mine + first:
            cp.start()

    def forward(*refs):
        _, _, landed, passed, _ = plan(*refs)
        for got, fwd in zip(landed, passed):
            got.wait_recv()
            fwd.start()

    def finish(*refs):
        mine, first, _, passed, from_sibling = plan(*refs)
        for cp in from_sibling:
            cp.wait_recv()
        for cp in first + passed:
            cp.wait_send()
        for cp in mine:
            cp.wait()

    out_shapes = [jax.ShapeDtypeStruct((N_DEV,) + s.shape, s.dtype) for s in srcs]
    return Side(list(srcs), out_shapes, _comm_sems(nt), [(0.0, start), (0.5, forward), (1.0, finish)])


def run_side(name, side):
    ns = len(side.arrs)

    def body(*refs):
        for _, phase in side.phases:
            phase(refs[:ns], refs[ns:ns + len(side.out_shapes)], *refs[ns + len(side.out_shapes):])

    return list(pl.pallas_call(
        body, name=name, in_specs=[_ANY] * ns, out_specs=[_ANY] * len(side.out_shapes),
        out_shape=list(side.out_shapes), scratch_shapes=list(side.sem_shapes))(*side.arrs))


def _relayout_w_in(w):
    offs = [0]
    for s in IN_SPLITS:
        offs.append(offs[-1] + s)
    p = [w[:, offs[i]:offs[i + 1]] for i in range(len(IN_SPLITS))]
    ua, za, q, k, v, zb, xbc, dt, zc, gates = p
    dtp = jnp.pad(dt, ((0, 0), (0, HPAD - dt.shape[1])))
    return (jnp.concatenate([ua, za, dtp], 1), w[:, offs[2]:offs[5]], jnp.concatenate([xbc, zb, zc], 1), gates)


def _pad_lanes(v, n=HPAD):
    return jnp.pad(v.reshape(1, -1), ((0, 0), (0, n - v.shape[-1])))


def _s5_prep_args(W):
    g2 = S5_GROUPS * S5_STATE
    w = g2 // S5_CHUNKS
    a_re, a_im = W['s5_a_re'].reshape(1, g2), W['s5_a_im'].reshape(1, g2)
    ls = W['s5_log_step'].reshape(1, S5_GROUPS)
    btr, bti = W['s5_b_re'].reshape(g2, S5_GROUP).T, W['s5_b_im'].reshape(g2, S5_GROUP).T
    ctr = W['s5_c_re'].transpose(1, 0, 2).reshape(S5_GROUP, g2)
    cti = W['s5_c_im'].transpose(1, 0, 2).reshape(S5_GROUP, g2)
    col = lambda a, rows: Arg(a, (rows, w), lambda o: (0, o), 'tile')
    return [col(a_re, 1), col(a_im, 1), _whole(ls, 'acc'), col(btr, S5_GROUP), col(bti, S5_GROUP), col(ctr, S5_GROUP), col(cti, S5_GROUP)]


def _s5_prep_outs():
    g2 = S5_GROUPS * S5_STATE
    w = g2 // S5_CHUNKS
    rows = (S5_ND, S5_ND, S5_SUB, S5_SUB, 128, 128, 128, 128)
    return [Out((r, g2), f32, (r, w), lambda o: (0, o)) for r in rows]


def _s5_args(A, prep, dvec, S):
    w = S5_GROUPS * S5_STATE // S5_CHUNKS
    args = [Arg(A, (S5_TILE, 128), lambda o, t: (t, o), 'tile', (S, S5_WIDTH), None, bf16)]
    for p in prep:
        args.append(Arg(p, (p.shape[0], w), lambda o, t: (0, o), 'acc0'))
    args.append(Arg(dvec, (1, 128), lambda o, t: (0, o), 'acc0'))
    return args


def _attn_args(QKV, g, r, qw, kw, S):
    L = S // r
    nq, rb = _attn_plan(r)
    block = (nq * ATT_BLOCK, rb * ATT_GW)
    gshape = (L, r * ATT_GW)
    gimap = lambda rho, n: (n, rho)
    if r == 1:
        mk = lambda j: Arg(QKV, block, lambda rho, n, j=j: (n, j), 'tile', gshape, gimap, bf16)
    else:
        def mk(j):
            view = QKV[:, j * ATT_GW:(j + 1) * ATT_GW].reshape(L, r * ATT_GW)
            return Arg(view, block, gimap, 'tile', None, None, bf16)
    return [mk(g), mk(3 + g), mk(6 + g), _whole(qw, 'acc'), _whole(kw, 'acc')]


def _attn_plan(r):
    return (4, 1) if r == 1 else (1, min(r, 4))


def _attn_grid(r, S):
    nq, rb = _attn_plan(r)
    return (r // rb, S // r // ATT_BLOCK // nq)


def _attn_carry(r):
    return ((ATT_BLOCK, _attn_plan(r)[1] * ATT_GW),) * 2


def _ssd_args(C, A, W, S):
    T = SSD_CHUNK
    return [Arg(C, (T, SSD_CONV_DIM), lambda o, t: (t, 0), 'tile', (S, SSD_CONV_DIM), None, bf16),
            Arg(A, (T, HPAD), lambda o, t: (t, 2 * S5_WIDTH // HPAD), 'tile', (S, HPAD), lambda o, t: (t, 0), bf16),
            Arg(C, (T, SSD_WIDTH), lambda o, t: (t, 2), 'tile', (S, SSD_WIDTH), lambda o, t: (t, 0), bf16),
            _whole(W['conv_w'], 'acc'), _whole(W['conv_b'].reshape(1, -1), 'acc'),
            _whole(_pad_lanes(W['dt_bias']), 'acc'), _whole(_pad_lanes(W['ssd_a_log']), 'acc'),
            _whole(_pad_lanes(W['ssd_d']), 'acc'), _whole(W['ssd_norm_w'].reshape(1, -1), 'acc')]


_SSD_CARRY = ((8, SSD_CONV_DIM), (SSD_WIDTH, SSD_STATE))
_S5_CARRY = ((1, 512), (1, 512))


def layer_fwd(li, x, W, side=None):
    S = x.shape[0]
    n = lambda s: f"l{li}_{s}"
    sv = {'x': x}
    (h,) = map_fwd(n("norm"), _rmsnorm_tile, (S // 512,), [_rows(x, 512), _whole(W['norm_w'].reshape(1, -1))],
                   [Out((S, D_MODEL), bf16, (512, D_MODEL), lambda i: (i, 0))])
    wA, wQ, wC, wG = W['w_in_pieces']
    A = matmul(n("in_a"), h, wA)
    QKV = matmul(n("in_qkv"), h, wQ)
    C = matmul(n("in_c"), h, wC)
    G = matmul(n("in_g"), h, wG)
    sv.update(h=h, A=A, QKV=QKV, C=C, G=G)

    prep = map_fwd(n("s5_prep"), _s5_prep_tile, (S5_CHUNKS,), _s5_prep_args(W), _s5_prep_outs())
    dvec = W['s5_d'].reshape(1, -1)
    (g,), s5_ck, *got = scan_fwd(n("s5_scan"), _s5_tile, (S5_CHUNKS, S // S5_TILE), _S5_CARRY, _s5_args(A, prep, dvec, S),
                                 [Out((S, S5_WIDTH), f32, (S5_TILE, 128), lambda o, t: (t, o))], side=side)
    glu = matmul(n("glu"), g, W['s5_glu_w'])
    glu_b = W['s5_glu_b'].reshape(1, -1)
    (ya,) = map_fwd(n("glu_gate"), _glu_tile, (S // 512,),
                    [_rows(g, 512), _rows(glu, 512), _rows(A, 512, col=1, width=S5_WIDTH), _whole(glu_b)],
                    [Out((S, S5_WIDTH), bf16, (512, S5_WIDTH), lambda i: (i, 0))])
    sv.update(prep=prep, g=g, glu=glu, ya=ya, s5_ck=s5_ck)

    qw, kw = W['q_norm_w'].reshape(1, -1), W['k_norm_w'].reshape(1, -1)
    att, att_ck, att_args = [], [], []
    for gi, (window, r) in enumerate(ATT_PAIRS):
        assert window // r == ATT_BLOCK and S % (r * ATT_BLOCK) == 0
        L = S // r
        nq, rb = _attn_plan(r)
        assert S // r // ATT_BLOCK % nq == 0
        spec = Out((L, r * ATT_GW), f32, (nq * ATT_BLOCK, rb * ATT_GW), lambda rho, nb: (nb, rho))
        att_args.append(_attn_args(QKV, gi, r, qw, kw, S))
        (o, lse), ck = scan_fwd(n(f"attn{gi}"), _attn_tile, _attn_grid(r, S), _attn_carry(r), att_args[gi], [spec, spec])
        att += [o.reshape(S, ATT_GW), lse.reshape(S, ATT_GW)]
        att_ck.append(ck)
    sv['att_args'] = att_args
    (yb,) = map_fwd(n("combine"), _combine_tile, (S // 512,),
                    [_rows(t, 512) for t in att] + [_rows(C, 512, col=SSD_CONV_DIM // ATT_GW, width=ATT_GW)],
                    [Out((S, ATT_GW), bf16, (512, ATT_GW), lambda i: (i, 0))])
    sv.update(att=att, att_ck=att_ck, yb=yb)

    (yc,), ssd_ck = scan_fwd(n("ssd"), _ssd_tile, (1, S // SSD_CHUNK), _SSD_CARRY, _ssd_args(C, A, W, S),
                             [Out((S, SSD_WIDTH), bf16, (SSD_CHUNK, SSD_WIDTH), lambda o, t: (t, 0))])
    sv.update(yc=yc, ssd_ck=ssd_ck)

    pa = matmul(n("proj_a"), ya, W['proj_a'])
    pb = matmul(n("proj_b"), yb, W['proj_b'])
    pc = matmul(n("proj_c"), yc, W['proj_c'])
    (merged,) = map_fwd(n("merge"), _merge_tile, (S // 256,),
                        [_rows(pa, 256), _rows(pb, 256), _rows(pc, 256), _rows(G, 256)],
                        [Out((S, D_MODEL), bf16, (256, D_MODEL), lambda i: (i, 0))])
    out = matmul(n("w_out"), merged, W['w_out'], add=x)
    sv.update(pa=pa, pb=pb, pc=pc, merged=merged)
    return out, sv, (got[0] if got else None)


def layer_bwd(li, dout, sv, W, side=None, own_scatter=None):
    S = dout.shape[0]
    n = lambda s: f"l{li}_{s}"
    gr = {}
    x, A, QKV, C, G = sv['x'], sv['A'], sv['QKV'], sv['C'], sv['G']

    dmerged = matmul(n("d_merged"), dout, W['w_out'], 'nt')
    gr['w_out'] = wgrad(n("g_w_out"), sv['merged'], dout)
    margs = [_rows(sv['pa'], 256, gdtype=bf16), _rows(sv['pb'], 256, gdtype=bf16), _rows(sv['pc'], 256, gdtype=bf16),
             _rows(G, 256, gdtype=bf16)]
    dpa, dpb, dpc, dgates = map_bwd(n("merge_bwd"), _merge_tile, (S // 256,), margs, [_rows(dmerged, 256)], list(range(4)))
    dya = matmul(n("d_ya"), dpa, W['proj_a'], 'nt')
    dyb = matmul(n("d_yb"), dpb, W['proj_b'], 'nt')
    dyc = matmul(n("d_yc"), dpc, W['proj_c'], 'nt')
    gr['proj_a'] = wgrad(n("g_proj_a"), sv['ya'], dpa)
    gr['proj_b'] = wgrad(n("g_proj_b"), sv['yb'], dpb)
    gr['proj_c'] = wgrad(n("g_proj_c"), sv['yc'], dpc)

    glu_b = W['s5_glu_b'].reshape(1, -1)
    gargs = [_rows(sv['g'], 512), _rows(sv['glu'], 512, gdtype=bf16),
             _rows(A, 512, col=1, width=S5_WIDTH, gshape=(S, S5_WIDTH), gdtype=bf16), _whole(glu_b, 'acc')]
    dg_a, dglu, dza, dglu_b = map_bwd(n("glu_gate_bwd"), _glu_tile, (S // 512,), gargs, [_rows(dya, 512)], [0, 1, 2, 3])
    gr['s5_glu_b'] = dglu_b.reshape(-1)
    dg = matmul(n("d_g"), dglu, W['s5_glu_w'], 'nt', add=dg_a)
    gr['s5_glu_w'] = wgrad(n("g_glu_w"), sv['g'], dglu)
    dvec = W['s5_d'].reshape(1, -1)
    sargs = _s5_args(A, sv['prep'], dvec, S)
    res = scan_bwd(n("s5_scan_bwd"), _s5_tile, (S5_CHUNKS, S // S5_TILE), _S5_CARRY, sargs, sv['s5_ck'],
                   [Arg(dg, (S5_TILE, 128), lambda o, t: (t, o))], list(range(len(sargs))), bwd_fn=_s5_tile_bwd)
    dua, dprep, dd = res[0], res[1:9], res[9]
    gr['s5_d'] = dd.reshape(-1)
    pargs = _s5_prep_args(W)
    pouts = _s5_prep_outs()
    da_re, da_im, dls, dbtr, dbti, dctr, dcti = map_bwd(
        n("s5_prep_bwd"), _s5_prep_tile, (S5_CHUNKS,), pargs,
        [Arg(d, o.block, o.imap) for d, o in zip(dprep, pouts)], list(range(7)))
    gshape = (S5_GROUPS, S5_STATE)
    gr['s5_a_re'], gr['s5_a_im'] = da_re.reshape(gshape), da_im.reshape(gshape)
    gr['s5_log_step'] = dls.reshape(-1)
    gr['s5_b_re'] = dbtr.T.reshape(S5_GROUPS, S5_STATE, S5_GROUP)
    gr['s5_b_im'] = dbti.T.reshape(S5_GROUPS, S5_STATE, S5_GROUP)
    gr['s5_c_re'] = dctr.reshape(S5_GROUP, S5_GROUPS, S5_STATE).transpose(1, 0, 2)
    gr['s5_c_im'] = dcti.reshape(S5_GROUP, S5_GROUPS, S5_STATE).transpose(1, 0, 2)

    cargs = [_rows(t, 512) for t in sv['att']] + \
            [_rows(C, 512, col=SSD_CONV_DIM // ATT_GW, width=ATT_GW, gshape=(S, ATT_GW), gdtype=bf16)]
    cres = map_bwd(n("combine_bwd"), _combine_tile, (S // 512,), cargs, [_rows(dyb, 512)], list(range(7)))
    dzb = cres[6]
    qw, kw = W['q_norm_w'].reshape(1, -1), W['k_norm_w'].reshape(1, -1)
    dqs, dks, dvs = [], [], []
    dqw = dkw = None
    for gi, (window, r) in enumerate(ATT_PAIRS):
        L = S // r
        nq, rb = _attn_plan(r)
        dspec = lambda t: Arg(t.reshape(L, r * ATT_GW), (nq * ATT_BLOCK, rb * ATT_GW), lambda rho, nb: (nb, rho))
        dq, dk, dv, dqw_g, dkw_g = scan_bwd(n(f"attn{gi}_bwd"), _attn_tile, _attn_grid(r, S), _attn_carry(r),
                                            sv['att_args'][gi], sv['att_ck'][gi],
                                            [dspec(cres[2 * gi]), dspec(cres[2 * gi + 1])], [0, 1, 2, 3, 4])
        dqs.append(dq.reshape(S, ATT_GW))
        dks.append(dk.reshape(S, ATT_GW))
        dvs.append(dv.reshape(S, ATT_GW))
        dqw = dqw_g if dqw is None else dqw + dqw_g
        dkw = dkw_g if dkw is None else dkw + dkw_g
    gr['q_norm_w'], gr['k_norm_w'] = dqw.reshape(-1), dkw.reshape(-1)

    ssd_args = _ssd_args(C, A, W, S)
    sres = scan_bwd(n("ssd_bwd"), _ssd_tile, (1, S // SSD_CHUNK), _SSD_CARRY, ssd_args, sv['ssd_ck'],
                    [Arg(dyc, (SSD_CHUNK, SSD_WIDTH), lambda o, t: (t, 0))], list(range(9)), side=side)
    sres, got = sres if side else (sres, None)
    dxbc, ddt, dzc = sres[0], sres[1], sres[2]
    gr['conv_w'] = sres[3]
    gr['conv_b'] = sres[4].reshape(-1)
    gr['dt_bias'] = sres[5].reshape(-1)[:SSD_HEADS]
    gr['ssd_a_log'] = sres[6].reshape(-1)[:SSD_HEADS]
    gr['ssd_d'] = sres[7].reshape(-1)[:SSD_HEADS]
    gr['ssd_norm_w'] = sres[8].reshape(-1)

    dpieces = [jnp.concatenate([dua, dza, ddt], axis=1), jnp.concatenate(dqs + dks + dvs, axis=1),
               jnp.concatenate([dxbc, dzb, dzc], axis=1), dgates]
    gr['w_in'] = _unrelayout_w_in_grad([wgrad(n(f"g_w_in{j}"), sv['h'], dp) for j, dp in enumerate(dpieces)])
    dh = matmul_nt_sum(n("d_h"), dpieces, list(W['w_in_pieces']), side=own_scatter(gr) if own_scatter else None)
    dh, got_own = dh if own_scatter else (dh, None)
    nargs = [_rows(x, 512), _whole(W['norm_w'].reshape(1, -1), 'acc')]
    dx, dnw = map_bwd(n("norm_bwd"), _rmsnorm_tile, (S // 512,), nargs, [_rows(dh, 512)], [0, 1], add={0: _rows(dout, 512)})
    gr['norm_w'] = dnw.reshape(-1)
    return dx, gr, got, got_own


def _unrelayout_w_in_grad(pieces):
    gA, gQ, gC, gG = pieces
    uaza, dt = gA[:, :2 * S5_WIDTH], gA[:, 2 * S5_WIDTH:2 * S5_WIDTH + SSD_HEADS]
    xbc, zb, zc = gC[:, :SSD_CONV_DIM], gC[:, SSD_CONV_DIM:SSD_CONV_DIM + ATT_GW], gC[:, SSD_CONV_DIM + ATT_GW:]
    return jnp.concatenate([uaza, gQ, zb, xbc, dt, zc, gG], axis=1)


def kernel(x, norm_w, w_in, s5_a_re, s5_a_im, s5_log_step, s5_b_re, s5_b_im, s5_c_re, s5_c_im, s5_d, s5_glu_w, s5_glu_b, q_norm_w, k_norm_w, conv_w, conv_b, dt_bias, ssd_a_log, ssd_d, ssd_norm_w, proj_a, proj_b, proj_c, w_out, loss_target, m_norm_w, m_w_in, m_s5_a_re, m_s5_a_im, m_s5_log_step, m_s5_b_re, m_s5_b_im, m_s5_c_re, m_s5_c_im, m_s5_d, m_s5_glu_w, m_s5_glu_b, m_q_norm_w, m_k_norm_w, m_conv_w, m_conv_b, m_dt_bias, m_ssd_a_log, m_ssd_d, m_ssd_norm_w, m_proj_a, m_proj_b, m_proj_c, m_w_out, v_norm_w, v_w_in, v_s5_a_re, v_s5_a_im, v_s5_log_step, v_s5_b_re, v_s5_b_im, v_s5_c_re, v_s5_c_im, v_s5_d, v_s5_glu_w, v_s5_glu_b, v_q_norm_w, v_k_norm_w, v_conv_w, v_conv_b, v_dt_bias, v_ssd_a_log, v_ssd_d, v_ssd_norm_w, v_proj_a, v_proj_b, v_proj_c, v_w_out):
    args = dict(locals())
    w = {k: args[k] for k in WEIGHTS}
    m = {k: args['m_' + k] for k in WEIGHTS}
    v = {k: args['v_' + k] for k in WEIGHTS}
    depth = norm_w.shape[0]
    S = x.shape[1]
    xs = x.reshape(S, D_MODEL)
    tgt = loss_target.reshape(S, D_MODEL)

    others = [k for k in SHARDED if k != 'w_in']

    def weight_gather(li):
        return gather_side(list(_relayout_w_in(w['w_in'][li].astype(bf16))) + [w[k][li].astype(bf16) for k in others])

    def assemble(li, gathered):
        W = {k: w[k][li] for k in WEIGHTS if k not in SHARDED}
        W['w_in_pieces'] = [t.reshape(t.shape[0] * t.shape[1], t.shape[2]) for t in gathered[:4]]
        for k, t in zip(others, gathered[4:]):
            n_dev, R, C = t.shape
            W[k] = t.reshape(n_dev * R, C) if k in ROW_SHARDED else t.transpose(1, 0, 2).reshape(R, n_dev * C)
        return W

    layers = [assemble(0, run_side("gather_weights0", weight_gather(0)))]
    act, saved = xs, []
    for li in range(depth):
        act, sv, got = layer_fwd(li, act, layers[li], weight_gather(li + 1) if li + 1 < depth else None)
        saved.append(sv)
        if got is not None:
            layers.append(assemble(li + 1, got))
    dy, loss_local = loss_and_grad(act, tgt)
    loss = lax.psum(loss_local, ("x", "y", "c"))

    big = [k for k in SHARDED if k != 'conv_w']

    def grad_scatter(gr):
        return exchange_side([gr[k] for k in big], ['rows' if k in ROW_SHARDED else 'cols' for k in big])

    grads, slots = [None] * depth, [None] * depth
    for li in reversed(range(depth)):
        dy, grads[li], got, got_own = layer_bwd(li, dy, saved[li], layers[li], grad_scatter(grads[li + 1]) if li + 1 < depth else None,
                                                grad_scatter if li == 0 else None)
        if got is not None:
            slots[li + 1] = got
        if got_own is not None:
            slots[li] = got_own
    small_keys = [k for k in WEIGHTS if k not in SHARDED] + ['conv_w']
    stacked = [jnp.stack([grads[li][k] for li in range(depth)], axis=0) for k in small_keys]
    (small_slots,) = run_side("gather_small_grads", exchange_side([_pack(stacked)], ['all']))
    grad_x = dy.reshape(x.shape)
    result = {k: adamw_layers("adamw_" + k, w[k], [slots[li][j] for li in range(depth)], m[k], v[k]) for j, k in enumerate(big)}

    totals = _unpack(sum_slots("sum_small_grads", small_slots), [t.shape for t in stacked])
    for k, g in zip(small_keys, totals):
        if k == 'conv_w':
            width = w[k].shape[-1]
            me = 4 * lax.axis_index("x") + 2 * lax.axis_index("y") + lax.axis_index("c")
            g = lax.dynamic_slice_in_dim(g, me * width, width, axis=2)
        result[k] = adamw("adamw_" + k, w[k], g[None], m[k], v[k])

    return (loss, grad_x, *[result[k][0] for k in WEIGHTS], *[result[k][1] for k in WEIGHTS],
            *[result[k][2] for k in WEIGHTS], *[result[k][3] for k in WEIGHTS])
```

```python
import functools
import math
from typing import Any, NamedTuple

import jax
import jax.numpy as jnp
from jax import lax
from jax.experimental import pallas as pl
from jax.experimental.pallas import tpu as pltpu

f32 = jnp.float32
bf16 = jnp.bfloat16

N_DEV = 8
D_MODEL = 1024
RMS_EPS = 1e-6
S5_WIDTH = 512
S5_GROUPS = 32
S5_GROUP = 16
S5_STATE = 64
S5_TILE = 512
S5_SUB = 8
S5_ND = 3
S5_CHUNKS = 4
ATT_HEAD_DIM = 64
ATT_PAIRS = ((128, 1), (512, 4), (2048, 16))
ATT_HPG = 4
ATT_BLOCK = 128
ATT_GW = ATT_HPG * ATT_HEAD_DIM
ATT_WIDTH = 768
SSD_HEADS = 12
SSD_HEAD_DIM = 64
SSD_WIDTH = 768
SSD_STATE = 128
SSD_GROUPS = 2
SSD_CHUNK = 128
SSD_CONV = 4
SSD_CONV_DIM = 1280
HPAD = 128
IN_SPLITS = (512, 512, 768, 768, 768, 256, 1280, 12, 768, 3072)
ADAM_LR, ADAM_B1, ADAM_B2, ADAM_EPS, ADAM_WD, ADAM_STEP = 0.001, 0.9, 0.999, 1e-08, 0.01, 10
VMEM_LIMIT = 56 * 1024 * 1024

WEIGHTS = ['norm_w', 'w_in', 's5_a_re', 's5_a_im', 's5_log_step', 's5_b_re', 's5_b_im', 's5_c_re',
           's5_c_im', 's5_d', 's5_glu_w', 's5_glu_b', 'q_norm_w', 'k_norm_w', 'conv_w', 'conv_b',
           'dt_bias', 'ssd_a_log', 'ssd_d', 'ssd_norm_w', 'proj_a', 'proj_b', 'proj_c', 'w_out']
ROW_SHARDED = ('w_in', 's5_glu_w', 'w_out')
SHARDED = ROW_SHARDED + ('conv_w', 'proj_a', 'proj_b', 'proj_c')


class Arg(NamedTuple):
    arr: Any
    block: tuple
    imap: Any
    kind: str = 'const'
    gshape: Any = None
    gimap: Any = None
    gdtype: Any = None


class Out(NamedTuple):
    shape: tuple
    dtype: Any
    block: tuple
    imap: Any


def _cparams(n):
    return pltpu.CompilerParams(dimension_semantics=("arbitrary",) * n, vmem_limit_bytes=VMEM_LIMIT)


def _rows(a, tm, kind='tile', col=0, width=None, gshape=None, gcol=None, gdtype=None):
    width = a.shape[1] if width is None else width
    g = None if gshape is None else (lambda i, gc=(0 if gcol is None else gcol): (i, gc))
    return Arg(a, (tm, width), lambda i, c=col: (i, c), kind, gshape, g, gdtype)


def _whole(a, kind='const'):
    nd = a.ndim
    return Arg(a, a.shape, lambda *i, nd=nd: (0,) * nd, kind)


def map_fwd(name, fn, grid, args, outs):
    n_in = len(args)

    def body(*refs):
        pid = tuple(pl.program_id(a) for a in range(len(grid)))
        res = fn(pid, *[r[...] for r in refs[:n_in]])
        for o, r in zip(refs[n_in:], res):
            o[...] = r.astype(o.dtype)

    res = pl.pallas_call(
        body, name=name, grid=grid,
        in_specs=[pl.BlockSpec(a.block, a.imap) for a in args],
        out_specs=[pl.BlockSpec(o.block, o.imap) for o in outs],
        out_shape=[jax.ShapeDtypeStruct(o.shape, o.dtype) for o in outs],
        compiler_params=_cparams(len(grid)))(*[a.arr for a in args])
    return tuple(res)


def _grad_outs(args, wrt):
    outs = []
    for i in wrt:
        a = args[i]
        shape = a.arr.shape if a.gshape is None else a.gshape
        imap = a.imap if a.gimap is None else a.gimap
        outs.append(Out(shape, f32 if a.gdtype is None else a.gdtype, a.block, imap))
    return outs


def _store_grads(pid, args, wrt, grads, grefs, adds):
    first_all = functools.reduce(jnp.logical_and, [p == 0 for p in pid])
    first_in = functools.reduce(jnp.logical_and, [p == 0 for p in pid[1:]]) if len(pid) > 1 else first_all
    for j, i in enumerate(wrt):
        g = grads[j].astype(f32)
        ref = grefs[j]
        kind = args[i].kind
        if kind == 'tile':
            if j in adds:
                g = g + adds[j]
            ref[...] = g.astype(ref.dtype)
        else:
            first = first_all if kind == 'acc' else first_in

            @pl.when(first)
            def _(ref=ref):
                ref[...] = jnp.zeros_like(ref)

            ref[...] += g


def map_bwd(name, fn, grid, args, douts, wrt, add=None):
    add = add or {}
    n_in, n_d, n_add = len(args), len(douts), len(add)
    add_keys = sorted(add)
    gouts = _grad_outs(args, wrt)

    def body(*refs):
        pid = tuple(pl.program_id(a) for a in range(len(grid)))
        vals = [r[...] for r in refs[:n_in]]
        dvals = [r[...].astype(f32) for r in refs[n_in:n_in + n_d]]
        avals = {k: refs[n_in + n_d + j][...].astype(f32) for j, k in enumerate(add_keys)}
        grefs = refs[n_in + n_d + n_add:]

        def f(*w):
            full = list(vals)
            for i, x in zip(wrt, w):
                full[i] = x
            return tuple(fn(pid, *full))

        _, vjp = jax.vjp(f, *[vals[i] for i in wrt])
        grads = vjp(tuple(dvals))
        _store_grads(pid, args, wrt, grads, grefs, avals)

    ins = list(args) + list(douts) + [add[k] for k in add_keys]
    res = pl.pallas_call(
        body, name=name, grid=grid,
        in_specs=[pl.BlockSpec(a.block, a.imap) for a in ins],
        out_specs=[pl.BlockSpec(o.block, o.imap) for o in gouts],
        out_shape=[jax.ShapeDtypeStruct(o.shape, o.dtype) for o in gouts],
        compiler_params=_cparams(len(grid)))(*[a.arr for a in ins])
    return tuple(res)


class Side(NamedTuple):
    arrs: list
    out_shapes: list
    sem_shapes: list
    phases: list


def _run_side(side, step, total, src_refs, out_refs, sem_refs):
    for frac, phase in side.phases:
        @pl.when(step == int(round(frac * (total - 1))))
        def _(phase=phase):
            phase(src_refs, out_refs, *sem_refs)


_ANY = pl.BlockSpec(memory_space=pl.ANY)


def scan_fwd(name, fn, grid, carry_shapes, args, outs, side=None):
    no, nt = grid
    n_in, n_out, n_c = len(args), len(outs), len(carry_shapes)
    ns_in, ns_out = (len(side.arrs), len(side.out_shapes)) if side else (0, 0)
    cks = [Out((no, nt) + cs, f32, (None, None) + cs, lambda o, t, n=len(cs): (o, t) + (0,) * n) for cs in carry_shapes]

    def body(*refs):
        pid = (pl.program_id(0), pl.program_id(1))
        ins = refs[:n_in]
        sins = refs[n_in:n_in + ns_in]
        refs = refs[n_in + ns_in:]
        orefs = refs[:n_out]
        ckrefs = refs[n_out:n_out + n_c]
        souts = refs[n_out + n_c:n_out + n_c + ns_out]
        crefs = refs[n_out + n_c + ns_out:n_out + n_c + ns_out + n_c]
        if side:
            _run_side(side, pid[0] * nt + pid[1], no * nt, sins, souts, refs[n_out + n_c + ns_out + n_c:])

        @pl.when(pid[1] == 0)
        def _():
            for c in crefs:
                c[...] = jnp.zeros_like(c)

        carry = tuple(c[...] for c in crefs)
        for ck, c in zip(ckrefs, carry):
            ck[...] = c
        res, newc = fn(pid, carry, *[r[...] for r in ins])
        for o, r in zip(orefs, res):
            o[...] = r.astype(o.dtype)
        for c, v in zip(crefs, newc):
            c[...] = v

    allouts = list(outs) + cks
    res = pl.pallas_call(
        body, name=name, grid=grid,
        in_specs=[pl.BlockSpec(a.block, a.imap) for a in args] + [_ANY] * ns_in,
        out_specs=[pl.BlockSpec(o.block, o.imap) for o in allouts] + [_ANY] * ns_out,
        out_shape=[jax.ShapeDtypeStruct(o.shape, o.dtype) for o in allouts] + (list(side.out_shapes) if side else []),
        scratch_shapes=[pltpu.VMEM(cs, f32) for cs in carry_shapes] + (list(side.sem_shapes) if side else []),
        compiler_params=_cparams(2))(*[a.arr for a in args], *(side.arrs if side else []))
    if side:
        return tuple(res[:n_out]), tuple(res[n_out:n_out + n_c]), list(res[n_out + n_c:])
    return tuple(res[:n_out]), tuple(res[n_out:])


def scan_bwd(name, fn, grid, carry_shapes, args, ckpts, douts, wrt, bwd_fn=None, side=None):
    no, nt = grid
    n_in, n_d, n_c = len(args), len(douts), len(carry_shapes)
    ns_in, ns_out = (len(side.arrs), len(side.out_shapes)) if side else (0, 0)

    def rev(imap):
        return lambda o, t: imap(o, nt - 1 - t)

    rargs = [a._replace(imap=rev(a.imap), gimap=None if a.gimap is None else rev(a.gimap)) for a in args]
    rdouts = [a._replace(imap=rev(a.imap)) for a in douts]
    ckargs = [Arg(ck, (None, None) + cs, rev(lambda o, t, n=len(cs): (o, t) + (0,) * n)) for ck, cs in zip(ckpts, carry_shapes)]
    gouts = _grad_outs(rargs, wrt)

    def body(*refs):
        o, t = pl.program_id(0), pl.program_id(1)
        tt = nt - 1 - t
        vals = [r[...] for r in refs[:n_in]]
        dvals = [r[...].astype(f32) for r in refs[n_in:n_in + n_d]]
        carry = tuple(r[...] for r in refs[n_in + n_d:n_in + n_d + n_c])
        sins = refs[n_in + n_d + n_c:n_in + n_d + n_c + ns_in]
        refs = refs[n_in + n_d + n_c + ns_in:]
        grefs = refs[:len(wrt)]
        souts = refs[len(wrt):len(wrt) + ns_out]
        dcrefs = refs[len(wrt) + ns_out:len(wrt) + ns_out + n_c]
        if side:
            _run_side(side, o * nt + t, no * nt, sins, souts, refs[len(wrt) + ns_out + n_c:])

        @pl.when(t == 0)
        def _():
            for c in dcrefs:
                c[...] = jnp.zeros_like(c)

        def f(carry, *w):
            full = list(vals)
            for i, x in zip(wrt, w):
                full[i] = x
            res, newc = fn((o, tt), carry, *full)
            return tuple(res), tuple(newc)

        dcarry = tuple(c[...] for c in dcrefs)
        if bwd_fn is None:
            _, vjp = jax.vjp(f, carry, *[vals[i] for i in wrt])
            grads = vjp((tuple(dvals), dcarry))
            dcarry_in, grads = grads[0], grads[1:]
        else:
            dcarry_in, grads = bwd_fn((o, tt), carry, vals, dvals, dcarry)
        for c, g in zip(dcrefs, dcarry_in):
            c[...] = g
        _store_grads((o, t), rargs, wrt, grads, grefs, {})

    ins = rargs + rdouts + ckargs
    res = pl.pallas_call(
        body, name=name, grid=grid,
        in_specs=[pl.BlockSpec(a.block, a.imap) for a in ins] + [_ANY] * ns_in,
        out_specs=[pl.BlockSpec(g.block, g.imap) for g in gouts] + [_ANY] * ns_out,
        out_shape=[jax.ShapeDtypeStruct(g.shape, g.dtype) for g in gouts] + (list(side.out_shapes) if side else []),
        scratch_shapes=[pltpu.VMEM(cs, f32) for cs in carry_shapes] + (list(side.sem_shapes) if side else []),
        compiler_params=_cparams(2))(*[a.arr for a in ins], *(side.arrs if side else []))
    if side:
        return tuple(res[:len(gouts)]), list(res[len(gouts):])
    return tuple(res)


def _pick(dim, target):
    if dim <= target:
        return dim
    for t in range(target // 128 * 128, 127, -128):
        if dim % t == 0:
            return t
    return dim


def matmul(name, a, b, mode='nn', add=None, out_dtype=f32, tm=None, tn=1152, tk=None):
    if mode == 'tn':
        K, M = a.shape
    else:
        M, K = a.shape
    N = b.shape[0] if mode == 'nt' else b.shape[1]
    assert (b.shape[1] if mode == 'nt' else b.shape[0]) == K
    tm = (1024 if mode == 'tn' else 512) if tm is None else tm
    tk = (1024 if mode == 'tn' else 1152) if tk is None else tk
    tm, tn, tk = _pick(M, tm), _pick(N, tn), _pick(K, tk)
    nk = K // tk
    a_spec = pl.BlockSpec((tk, tm), lambda i, j, k: (k, i)) if mode == 'tn' else pl.BlockSpec((tm, tk), lambda i, j, k: (i, k))
    b_spec = pl.BlockSpec((tn, tk), lambda i, j, k: (j, k)) if mode == 'nt' else pl.BlockSpec((tk, tn), lambda i, j, k: (k, j))
    dims = {'nn': (((1,), (0,)), ((), ())), 'nt': (((1,), (1,)), ((), ())), 'tn': (((0,), (0,)), ((), ()))}[mode]
    has_add = add is not None

    def body(*refs):
        if has_add:
            a_ref, b_ref, add_ref, o_ref, acc = refs
        else:
            a_ref, b_ref, o_ref, acc = refs
        k = pl.program_id(2)
        prod = lax.dot_general(a_ref[...].astype(bf16), b_ref[...].astype(bf16), dims, preferred_element_type=f32)
        if nk == 1:
            o_ref[...] = (prod + add_ref[...].astype(f32) if has_add else prod).astype(o_ref.dtype)
            return

        @pl.when(k == 0)
        def _():
            acc[...] = add_ref[...].astype(f32) if has_add else jnp.zeros_like(acc)

        acc[...] += prod

        @pl.when(k == nk - 1)
        def _():
            o_ref[...] = acc[...].astype(o_ref.dtype)

    in_specs = [a_spec, b_spec] + ([pl.BlockSpec((tm, tn), lambda i, j, k: (i, j))] if has_add else [])
    ops = [a, b] + ([add] if has_add else [])
    return pl.pallas_call(
        body, name=name, grid=(M // tm, N // tn, nk), in_specs=in_specs,
        out_specs=pl.BlockSpec((tm, tn), lambda i, j, k: (i, j)),
        out_shape=jax.ShapeDtypeStruct((M, N), out_dtype),
        scratch_shapes=[pltpu.VMEM((tm, tn), f32)],
        compiler_params=pltpu.CompilerParams(dimension_semantics=("parallel", "parallel", "arbitrary"), vmem_limit_bytes=VMEM_LIMIT))(*ops)


def matmul_nt_sum(name, lhs, rhs, tm=1024, tk=768, side=None):
    M, N = lhs[0].shape[0], rhs[0].shape[0]
    tm = _pick(M, tm)
    tks = [_pick(a.shape[1], tk) for a in lhs]
    starts, total = [], 0
    for a, t in zip(lhs, tks):
        starts.append(total)
        total += a.shape[1] // t
    npc = len(lhs)
    ns_in, ns_out = (len(side.arrs), len(side.out_shapes)) if side else (0, 0)

    def body(*refs):
        a_refs, b_refs, sins = refs[:npc], refs[npc:2 * npc], refs[2 * npc:2 * npc + ns_in]
        refs = refs[2 * npc + ns_in:]
        o_ref, souts, acc = refs[0], refs[1:1 + ns_out], refs[1 + ns_out]
        k = pl.program_id(1)
        if side:
            _run_side(side, pl.program_id(0) * total + k, (M // tm) * total, sins, souts, refs[2 + ns_out:])

        @pl.when(k == 0)
        def _():
            acc[...] = jnp.zeros_like(acc)

        for p in range(npc):
            @pl.when((k >= starts[p]) & (k < starts[p] + lhs[p].shape[1] // tks[p]))
            def _(p=p):
                acc[...] += lax.dot_general(a_refs[p][...].astype(bf16), b_refs[p][...].astype(bf16), _NT, preferred_element_type=f32)

        @pl.when(k == total - 1)
        def _():
            o_ref[...] = acc[...]

    def kblock(p):
        return lambda k: jnp.clip(k - starts[p], 0, lhs[p].shape[1] // tks[p] - 1)

    in_specs = [pl.BlockSpec((tm, tks[p]), lambda i, k, kb=kblock(p): (i, kb(k))) for p in range(npc)]
    in_specs += [pl.BlockSpec((N, tks[p]), lambda i, k, kb=kblock(p): (0, kb(k))) for p in range(npc)]
    res = pl.pallas_call(
        body, name=name, grid=(M // tm, total), in_specs=in_specs + [_ANY] * ns_in,
        out_specs=[pl.BlockSpec((tm, N), lambda i, k: (i, 0))] + [_ANY] * ns_out,
        out_shape=[jax.ShapeDtypeStruct((M, N), f32)] + (list(side.out_shapes) if side else []),
        scratch_shapes=[pltpu.VMEM((tm, N), f32)] + (list(side.sem_shapes) if side else []),
        compiler_params=_cparams(2))(*lhs, *rhs, *(side.arrs if side else []))
    return (res[0], list(res[1:])) if side else res[0]


def wgrad(name, act, dout):
    return matmul(name, act, dout, 'tn', out_dtype=bf16)


def _dot(a, b, dims=(((1,), (0,)), ((), ()))):
    return lax.dot_general(a.astype(bf16), b.astype(bf16), dims, preferred_element_type=f32)


_NT = (((1,), (1,)), ((), ()))
_TN = (((0,), (0,)), ((), ()))


def _three_term_dot(v, sel, dims):
    hi = v.astype(bf16)
    rest = v - hi.astype(f32)
    mid = rest.astype(bf16)
    lo = (rest - mid.astype(f32)).astype(bf16)
    dot = lambda t: lax.dot_general(t, sel, dims, preferred_element_type=f32)
    return dot(hi) + dot(mid) + dot(lo)


@jax.custom_vjp
def _dot_exact01(v, sel):
    return _three_term_dot(v, sel, (((1,), (0,)), ((), ())))


def _dot_exact01_fwd(v, sel):
    return _dot_exact01(v, sel), sel


def _dot_exact01_bwd(sel, ct):
    return _three_term_dot(ct, sel, _NT), jnp.zeros_like(sel)


_dot_exact01.defvjp(_dot_exact01_fwd, _dot_exact01_bwd)


def _spread_heads(v, width):
    r = lax.broadcasted_iota(jnp.int32, (HPAD, SSD_HEADS * width), 0)
    c = lax.broadcasted_iota(jnp.int32, (HPAD, SSD_HEADS * width), 1)
    return _dot_exact01(v, (r == c // width).astype(bf16))


def _rmsnorm_tile(pid, x, w):
    return (x * lax.rsqrt(jnp.mean(x * x, axis=-1, keepdims=True) + RMS_EPS) * w,)


def _shift_rows(h, d, fill):
    pad = jnp.full((d, h.shape[1]), fill, f32)
    return jnp.concatenate([pad, h[:-d]], axis=0)


def _s5_prep_tile(pid, a_re, a_im, ls, btr, bti, ctr, cti):
    o = pid[0]
    w = a_re.shape[1]
    r = lax.broadcasted_iota(jnp.int32, (S5_GROUPS, w), 0)
    c = lax.broadcasted_iota(jnp.int32, (S5_GROUPS, w), 1)
    sel = (r == o * (w // S5_STATE) + c // S5_STATE).astype(f32)
    step = jnp.dot(jnp.exp(ls), sel, precision=lax.Precision.HIGHEST, preferred_element_type=f32)
    mag = jnp.exp(a_re * step)
    ang = a_im * step
    lr, li = mag * jnp.cos(ang), mag * jnp.sin(ang)
    nr, ni = lr - 1.0, li
    den = a_re * a_re + a_im * a_im
    fr = (nr * a_re + ni * a_im) / den
    fi = (ni * a_re - nr * a_im) / den
    bbr = fr * btr - fi * bti
    bbi = fr * bti + fi * btr
    reps = w // S5_STATE
    rr = lax.broadcasted_iota(jnp.int32, (reps * S5_GROUP, w), 0)
    cc = lax.broadcasted_iota(jnp.int32, (reps * S5_GROUP, w), 1)
    diag = (rr // S5_GROUP) == (cc // S5_STATE)

    def expand(m):
        return jnp.where(diag, jnp.concatenate([m] * reps, axis=0), 0.0)

    pr, pi = lr, li
    rows_r, rows_i = [pr], [pi]
    for _ in range(S5_ND - 1):
        pr, pi = pr * pr - pi * pi, 2.0 * pr * pi
        rows_r.append(pr)
        rows_i.append(pi)
    lamd_r, lamd_i = jnp.concatenate(rows_r, axis=0), jnp.concatenate(rows_i, axis=0)
    tr = jnp.broadcast_to(lr, (S5_SUB, w))
    ti = jnp.broadcast_to(li, (S5_SUB, w))
    for j in range(S5_ND):
        sr, si = _shift_rows(tr, 1 << j, 1.0), _shift_rows(ti, 1 << j, 0.0)
        tr, ti = tr * sr - ti * si, tr * si + ti * sr
    return lamd_r, lamd_i, tr, ti, expand(bbr), expand(bbi), expand(ctr), expand(cti)


def _s5_tile(pid, carry, u, lamd_r, lamd_i, lam8_r, lam8_i, bbr, bbi, ccr, cci, dvec):
    hr, hi = _s5_scan(_dot(u, bbr), _dot(u, bbi), carry, lamd_r, lamd_i, lam8_r, lam8_i, reverse=False)
    return (_s5_readout(hr, hi, u, ccr, cci, dvec),), (hr[-1:], hi[-1:])


def _s5_readout(hr, hi, u, ccr, cci, dvec):
    return jax.nn.gelu(_dot(hr, ccr, _NT) - _dot(hi, cci, _NT) + dvec * u)


def _s5_scan(xr, xi, carry, lamd_r, lamd_i, lam8_r, lam8_i, reverse):
    cr, ci = carry
    T, G = xr.shape[0], S5_SUB
    sign = -1.0 if reverse else 1.0
    sub = lax.broadcasted_iota(jnp.int32, (G, 1), 0)
    xr, xi = xr.reshape(T // G, G, xr.shape[1]), xi.reshape(T // G, G, xi.shape[1])
    for j in range(S5_ND):
        d = 1 << j
        keep = (sub < G - d) if reverse else (sub >= d)
        ar = jnp.where(keep, lamd_r[j:j + 1], 0.0)
        ai = jnp.where(keep, sign * lamd_i[j:j + 1], 0.0)
        sr = pltpu.roll(xr, G - d if reverse else d, axis=1)
        si = pltpu.roll(xi, G - d if reverse else d, axis=1)
        xr, xi = xr + ar * sr - ai * si, xi + ar * si + ai * sr
    xr, xi = xr.reshape(T, xr.shape[2]), xi.reshape(T, xi.shape[2])
    if reverse:
        pr = jnp.concatenate([lam8_r[G - 1 - s:G - s] for s in range(G)], axis=0)
        pi = -jnp.concatenate([lam8_i[G - 1 - s:G - s] for s in range(G)], axis=0)
    else:
        pr, pi = lam8_r, lam8_i
    n = T // G
    rows_r, rows_i = [None] * n, [None] * n
    for i in (reversed(range(n)) if reverse else range(n)):
        gr_, gi_ = xr[i * G:(i + 1) * G], xi[i * G:(i + 1) * G]
        gr_, gi_ = gr_ + pr * cr - pi * ci, gi_ + pr * ci + pi * cr
        cr, ci = (gr_[:1], gi_[:1]) if reverse else (gr_[G - 1:], gi_[G - 1:])
        rows_r[i], rows_i[i] = gr_, gi_
    return jnp.concatenate(rows_r, axis=0), jnp.concatenate(rows_i, axis=0)


def _s5_tile_bwd(pid, carry, vals, douts, dcarry):
    u, lamd_r, lamd_i, lam8_r, lam8_i, bbr, bbi, ccr, cci, dvec = vals
    (dg,) = douts
    hr, hi = _s5_scan(_dot(u, bbr), _dot(u, bbi), carry, lamd_r, lamd_i, lam8_r, lam8_i, reverse=False)
    _, vjp = jax.vjp(_s5_readout, hr, hi, u, ccr, cci, dvec)
    dhr, dhi, du, dccr, dcci, ddvec = vjp(dg)
    Hr, Hi = _s5_scan(dhr, dhi, dcarry, lamd_r, lamd_i, lam8_r, lam8_i, reverse=True)
    _, vjp_in = jax.vjp(lambda u, bbr, bbi: (_dot(u, bbr), _dot(u, bbi)), u, bbr, bbi)
    du2, dbbr, dbbi = vjp_in((Hr, Hi))
    pr = jnp.concatenate([carry[0], hr[:-1]], axis=0)
    pi = jnp.concatenate([carry[1], hi[:-1]], axis=0)
    dlam_r = jnp.sum(Hr * pr + Hi * pi, axis=0, keepdims=True)
    dlam_i = jnp.sum(Hi * pr - Hr * pi, axis=0, keepdims=True)
    zrow = jnp.zeros((S5_ND - 1, dlam_r.shape[1]), f32)
    dlamd_r, dlamd_i = jnp.concatenate([dlam_r, zrow], axis=0), jnp.concatenate([dlam_i, zrow], axis=0)
    grads = (du + du2, dlamd_r, dlamd_i, jnp.zeros_like(lam8_r), jnp.zeros_like(lam8_i), dbbr, dbbi, dccr, dcci, ddvec)
    return (Hr[:1], Hi[:1]), grads


def _glu_tile(pid, g, glu, za, b):
    return (g * jax.nn.sigmoid(glu + b) * jax.nn.silu(za),)


def _attn_tile(pid, carry, q, k, v, qw, kw):
    n = pid[1]
    kp, vp = carry
    D, B = ATT_HEAD_DIM, ATT_BLOCK
    W = 2 * D
    nq, ncol = q.shape[0] // B, q.shape[1] // W
    r = lax.broadcasted_iota(jnp.int32, (B, 2 * B), 0)
    c = lax.broadcasted_iota(jnp.int32, (B, 2 * B), 1)
    diff = r + B - c
    band = (diff >= 0) & (diff <= B)
    band_first = band & ((c >= B) | (n > 0))
    low = lax.broadcasted_iota(jnp.int32, (1, W), 1) < D
    same_head = (lax.broadcasted_iota(jnp.int32, (W, W), 0) // D == lax.broadcasted_iota(jnp.int32, (W, W), 1) // D).astype(bf16)

    def hnorm(x, w):
        rows = x.shape[0]
        t = jnp.concatenate([x[:, j * W:(j + 1) * W] for j in range(ncol)], axis=0) if ncol > 1 else x
        ms = _dot_exact01(t * t, same_head) * (1.0 / D)
        t = t * lax.rsqrt(ms + RMS_EPS) * jnp.concatenate([w, w], axis=1)
        return [t[j * rows:(j + 1) * rows] for j in range(ncol)]

    qns, kns = hnorm(q, qw), hnorm(k, kw)
    out_cols, lse_cols, kn_cols = [], [], []
    for j in range(ncol):
        sl = slice(j * W, (j + 1) * W)
        qn, kn, vj = qns[j], kns[j], v[:, sl]
        kn_cols.append(kn[(nq - 1) * B:])
        outs, lses = [], []
        for b in range(nq):
            rows = slice(b * B, (b + 1) * B)
            prev = slice((b - 1) * B, b * B)
            kk = jnp.concatenate([kp[:, sl] if b == 0 else kn[prev], kn[rows]], axis=0)
            vv = jnp.concatenate([vp[:, sl] if b == 0 else vj[prev], vj[rows]], axis=0)
            o2, l2 = [], []
            for head_lanes in (low, ~low):
                s = _dot(jnp.where(head_lanes, qn[rows], 0.0), kk, _NT) * (D ** -0.5)
                s = jnp.where(band_first if b == 0 else band, s, -1e30)
                m = jnp.max(s, axis=-1, keepdims=True)
                p = jnp.exp(s - m)
                l = jnp.sum(p, axis=-1, keepdims=True)
                o2.append(_dot(p / l, vv))
                l2.append(m + jnp.log(l))
            outs.append(jnp.where(low, o2[0], o2[1]))
            lses.append(jnp.where(low, l2[0], l2[1]))
        out_cols.append(jnp.concatenate(outs, axis=0) if nq > 1 else outs[0])
        lse_cols.append(jnp.concatenate(lses, axis=0) if nq > 1 else lses[0])
    return ((jnp.concatenate(out_cols, axis=1), jnp.concatenate(lse_cols, axis=1)),
            (jnp.concatenate(kn_cols, axis=1), v[(nq - 1) * B:]))


def _combine_tile(pid, o1, l1, o2, l2, o3, l3, zb):
    m = jnp.maximum(jnp.maximum(l1, l2), l3)
    e1, e2, e3 = jnp.exp(l1 - m), jnp.exp(l2 - m), jnp.exp(l3 - m)
    y = (e1 * o1 + e2 * o2 + e3 * o3) / (e1 + e2 + e3)
    return (y * jax.nn.silu(zb),)


def _softplus(x):
    return jnp.maximum(x, 0.0) + jnp.log(1.0 + jnp.exp(-jnp.abs(x)))


def _ssd_tile(pid, carry, xbc, dt, z, conv_w, conv_b, dt_bias, a_log, dvec, norm_w):
    xprev, state = carry
    T, P, N = SSD_CHUNK, SSD_HEAD_DIM, SSD_STATE
    xx = jnp.concatenate([xprev, xbc], axis=0)
    conv = conv_b
    for k in range(SSD_CONV):
        off = 8 - (SSD_CONV - 1) + k
        conv = conv + conv_w[k:k + 1] * xx[off:off + T]
    xc = jax.nn.silu(conv)
    dtp = _softplus(dt + dt_bias)
    a_dt = dtp * (-jnp.exp(a_log))
    r = lax.broadcasted_iota(jnp.int32, (T, T), 0)
    c = lax.broadcasted_iota(jnp.int32, (T, T), 1)
    tri = r >= c
    trif = tri.astype(f32)
    hi = lax.Precision.HIGHEST
    a_cs = jnp.dot(trif, a_dt, precision=hi, preferred_element_type=f32)
    a_cs_t = lax.dot_general(a_dt, trif, (((0,), (1,)), ((), ())), precision=hi, preferred_element_type=f32)
    xs = xc[:, :SSD_WIDTH]
    acs_p = _spread_heads(a_cs, P)
    xdt = xs * _spread_heads(dtp, P)
    skip = _spread_heads(dvec, P)
    to_end = jnp.exp(acs_p[T - 1:T] - acs_p)
    low = lax.broadcasted_iota(jnp.int32, (1, 2 * P), 1) < P
    low_rows = lax.broadcasted_iota(jnp.int32, (2 * P, 1), 0) < P
    ys, states = [], []
    for j in range(SSD_HEADS // 2):
        g = 2 * j // (SSD_HEADS // SSD_GROUPS)
        if 2 * j % (SSD_HEADS // SSD_GROUPS) == 0:
            bg = xc[:, SSD_WIDTH + g * N:SSD_WIDTH + (g + 1) * N]
            cg = xc[:, SSD_WIDTH + SSD_GROUPS * N + g * N:SSD_WIDTH + SSD_GROUPS * N + (g + 1) * N]
            cb = _dot(cg, bg, _NT)
        lanes = slice(2 * j * P, 2 * (j + 1) * P)
        st = state[lanes, :]
        diag, last = [], []
        for h in (2 * j, 2 * j + 1):
            decay = jnp.exp(jnp.where(tri, a_cs[:, h:h + 1] - a_cs_t[h:h + 1, :], -1e30))
            diag.append(_dot(cb * decay, xdt[:, lanes]))
            last.append(jnp.exp(a_cs_t[h:h + 1, T - 1:T]))
        y = (jnp.where(low, diag[0], diag[1]) + _dot(cg, st, _NT) * jnp.exp(acs_p[:, lanes])
             + xs[:, lanes] * skip[:, lanes])
        ys.append(y)
        states.append(jnp.where(low_rows, last[0], last[1]) * st + _dot(xdt[:, lanes] * to_end[:, lanes], bg, _TN))
    y = jnp.concatenate(ys, axis=1) * jax.nn.silu(z)
    out = y * lax.rsqrt(jnp.mean(y * y, axis=-1, keepdims=True) + RMS_EPS) * norm_w
    return (out,), (xbc[T - 8:], jnp.concatenate(states, axis=0))


def _merge_tile(pid, pa, pb, pc, gates):
    d = pa.shape[1]
    g = jax.nn.sigmoid(gates)
    return (g[:, :d] * pa + g[:, d:2 * d] * pb + g[:, 2 * d:] * pc,)


def loss_and_grad(y, target, tm=512):
    S, D = y.shape
    nt = S // tm

    def body(y_ref, t_ref, dy_ref, l_ref, acc):
        i = pl.program_id(0)

        @pl.when(i == 0)
        def _():
            acc[...] = jnp.zeros_like(acc)

        diff = y_ref[...] - t_ref[...]
        dy_ref[...] = diff * (1.0 / D)
        acc[...] += jnp.sum((diff * diff).reshape(tm // 8, 8, D), axis=0)

        @pl.when(i == nt - 1)
        def _():
            l_ref[...] = jnp.broadcast_to(0.5 / D * jnp.sum(acc[...]), l_ref.shape)

    dy, l = pl.pallas_call(
        body, name="loss_head", grid=(nt,),
        in_specs=[pl.BlockSpec((tm, D), lambda i: (i, 0))] * 2,
        out_specs=[pl.BlockSpec((tm, D), lambda i: (i, 0)), pl.BlockSpec((8, 128), lambda i: (0, 0))],
        out_shape=[jax.ShapeDtypeStruct((S, D), f32), jax.ShapeDtypeStruct((8, 128), f32)],
        scratch_shapes=[pltpu.VMEM((8, D), f32)],
        compiler_params=_cparams(1))(y, target)
    return dy, l[0, 0]


def _row_tile(R, C, budget=1 << 20):
    best = R
    for t in range(8, R, 8):
        if R % t == 0 and t * C * 4 <= budget:
            best = t
    if best == R and R * C * 4 > budget:
        for t in range(8, R, 8):
            if R % t == 0:
                return t
    return best


def _as2d(t, lead=0):
    return t.reshape(t.shape[:lead] + (math.prod(t.shape[lead:-1]), t.shape[-1]))


def adamw_layers(name, w, slots, m, v):
    L, R, C = w.shape
    n = slots[0].shape[0]
    lanes = -(-C // 128) * 128
    tr = _row_tile(R, lanes * (n * L + 7), budget=10 << 20)

    def body(*refs):
        w_ref, m_ref, v_ref = refs[0], refs[1 + L], refs[2 + L]
        go_ref, d_ref, nm_ref, nv_ref = refs[3 + L:]
        layer = pl.program_id(0)
        gg = None
        for l in range(L):
            s = refs[1 + l][0].astype(f32)
            for j in range(1, n):
                s = s + refs[1 + l][j].astype(f32)
            gg = s if gg is None else jnp.where(layer == l, s, gg)
        go_ref[...] = gg
        nm = ADAM_B1 * m_ref[...] + (1.0 - ADAM_B1) * gg
        nv = ADAM_B2 * v_ref[...] + (1.0 - ADAM_B2) * jnp.square(gg)
        m_hat = nm / (1.0 - ADAM_B1 ** ADAM_STEP)
        v_hat = nv / (1.0 - ADAM_B2 ** ADAM_STEP)
        d_ref[...] = -ADAM_LR * (m_hat / (jnp.sqrt(v_hat) + ADAM_EPS) + ADAM_WD * w_ref[...])
        nm_ref[...] = nm
        nv_ref[...] = nv

    spec = pl.BlockSpec((None, tr, C), lambda l, i: (l, i, 0))
    slot_specs = [pl.BlockSpec((n, tr, C), lambda l, i, own=own: (0, jnp.where(l == own, i, 0), 0)) for own in range(L)]
    res = pl.pallas_call(
        body, name=name, grid=(L, R // tr),
        in_specs=[spec] + slot_specs + [spec, spec], out_specs=[spec] * 4,
        out_shape=[jax.ShapeDtypeStruct((L, R, C), f32)] * 4,
        compiler_params=_cparams(2))(w, *slots, m, v)
    return tuple(res)


def adamw(name, w, gslots, m, v):
    shape = w.shape
    n = gslots.shape[0]
    C = shape[-1]
    R = math.prod(shape[:-1])
    lanes = -(-C // 128) * 128
    tr = _row_tile(R, lanes * (n + 7), budget=10 << 20)

    def body(w_ref, g_ref, m_ref, v_ref, go_ref, d_ref, nm_ref, nv_ref):
        gg = g_ref[0].astype(f32)
        for s in range(1, n):
            gg = gg + g_ref[s].astype(f32)
        go_ref[...] = gg
        nm = ADAM_B1 * m_ref[...] + (1.0 - ADAM_B1) * gg
        nv = ADAM_B2 * v_ref[...] + (1.0 - ADAM_B2) * jnp.square(gg)
        m_hat = nm / (1.0 - ADAM_B1 ** ADAM_STEP)
        v_hat = nv / (1.0 - ADAM_B2 ** ADAM_STEP)
        d_ref[...] = -ADAM_LR * (m_hat / (jnp.sqrt(v_hat) + ADAM_EPS) + ADAM_WD * w_ref[...])
        nm_ref[...] = nm
        nv_ref[...] = nv

    spec = pl.BlockSpec((tr, C), lambda i: (i, 0))
    res = pl.pallas_call(
        body, name=name, grid=(R // tr,),
        in_specs=[spec, pl.BlockSpec((n, tr, C), lambda i: (0, i, 0)), spec, spec], out_specs=[spec] * 4,
        out_shape=[jax.ShapeDtypeStruct((R, C), f32)] * 4,
        compiler_params=_cparams(1))(w.reshape(R, C), gslots.reshape(n, R, C), m.reshape(R, C), v.reshape(R, C))
    return tuple(t.reshape(shape) for t in res)


PACK_ROWS = 256


def sum_slots(name, x):
    n, R, C = x.shape

    def body(x_ref, o_ref):
        acc = x_ref[0]
        for s in range(1, n):
            acc = acc + x_ref[s]
        o_ref[...] = acc

    return pl.pallas_call(
        body, name=name, grid=(R // PACK_ROWS,),
        in_specs=[pl.BlockSpec((n, PACK_ROWS, C), lambda i: (0, i, 0))],
        out_specs=pl.BlockSpec((PACK_ROWS, C), lambda i: (i, 0)),
        out_shape=jax.ShapeDtypeStruct((R, C), f32), compiler_params=_cparams(1))(x)


def _pack(parts):
    flat = jnp.concatenate([p.reshape(-1) for p in parts])
    unit = 128 * PACK_ROWS
    tot = -(-flat.shape[0] // unit) * unit
    return jnp.pad(flat, (0, tot - flat.shape[0])).reshape(tot // 128, 128)


def _unpack(buf, shapes):
    flat = buf.reshape(-1)
    out, off = [], 0
    for s in shapes:
        size = math.prod(s)
        out.append(flat[off:off + size].reshape(s))
        off += size
    return out


def _comm_sems(nt):
    return [pltpu.SemaphoreType.DMA((nt, N_DEV - 1)), pltpu.SemaphoreType.DMA((nt, N_DEV - 1)), pltpu.SemaphoreType.DMA((nt,))]


def exchange_side(srcs, modes):
    nt = len(srcs)
    slabs = []
    for s, mode in zip(srcs, modes):
        R, C = s.shape
        slabs.append({'all': (R, C), 'rows': (R // N_DEV, C), 'cols': (R, C // N_DEV)}[mode])

    def piece(ref, mode, slab, p):
        if mode == 'all':
            return ref
        if mode == 'rows':
            return ref.at[pl.ds(p * slab[0], slab[0]), :]
        return ref.at[:, pl.ds(p * slab[1], slab[1])]

    def copies(src_refs, out_refs, send_sems, recv_sems, local_sems):
        x, y, c = lax.axis_index("x"), lax.axis_index("y"), lax.axis_index("c")
        me = 4 * x + 2 * y + c
        out = []
        for k in (1, 2, 4, 3, 5, 6, 7):
            px = 1 - x if k & 4 else x
            py = 1 - y if k & 2 else y
            pc = 1 - c if k & 1 else c
            for t in range(nt):
                out.append(pltpu.make_async_remote_copy(
                    src_ref=piece(src_refs[t], modes[t], slabs[t], 4 * px + 2 * py + pc), dst_ref=out_refs[t].at[me],
                    send_sem=send_sems.at[t, k - 1], recv_sem=recv_sems.at[t, k - 1],
                    device_id=(px, py, pc), device_id_type=pl.DeviceIdType.MESH))
        for t in range(nt):
            out.append(pltpu.make_async_copy(piece(src_refs[t], modes[t], slabs[t], me), out_refs[t].at[me], local_sems.at[t]))
        return out

    def start(*refs):
        for cp in copies(*refs):
            cp.start()

    def finish(*refs):
        for cp in copies(*refs):
            cp.wait()

    out_shapes = [jax.ShapeDtypeStruct((N_DEV,) + sl, s.dtype) for s, sl in zip(srcs, slabs)]
    return Side(list(srcs), out_shapes, _comm_sems(nt), [(0.0, start), (1.0, finish)])


def gather_side(srcs):
    nt = len(srcs)

    def plan(src_refs, out_refs, send_sems, recv_sems, local_sems):
        x, y, c = lax.axis_index("x"), lax.axis_index("y"), lax.axis_index("c")
        me, sibling = (x, y, c), (x, y, 1 - c)
        chips = [(1 - x, y), (x, 1 - y), (1 - x, 1 - y)]

        def slot(t, dev):
            return out_refs[t].at[4 * dev[0] + 2 * dev[1] + dev[2]]

        def copy(t, k, block, to, src=None):
            return pltpu.make_async_remote_copy(
                src_ref=slot(t, block) if src is None else src, dst_ref=slot(t, block),
                send_sem=send_sems.at[t, k], recv_sem=recv_sems.at[t, k], device_id=to, device_id_type=pl.DeviceIdType.MESH)

        mine = [pltpu.make_async_copy(src_refs[t], slot(t, me), local_sems.at[t]) for t in range(nt)]
        first = []
        for t in range(nt):
            first.append(copy(t, 0, me, sibling, src=src_refs[t]))
            first += [copy(t, 1 + j, me, (*chip, c), src=src_refs[t]) for j, chip in enumerate(chips)]
        landed = [copy(t, 1 + j, (*chip, c), me) for j, chip in enumerate(chips) for t in range(nt)]
        passed = [copy(t, 4 + j, (*chip, c), sibling) for j, chip in enumerate(chips) for t in range(nt)]
        from_sibling = [copy(t, 0, sibling, me) for t in range(nt)]
        from_sibling += [copy(t, 4 + j, (*chip, 1 - c), me) for t in range(nt) for j, chip in enumerate(chips)]
        return mine, first, landed, passed, from_sibling

    def start(*refs):
        mine, first, _, _, _ = plan(*refs)
        for cp in mine + first:
            cp.start()

    def forward(*refs):
        _, _, landed, passed, _ = plan(*refs)
        for got, fwd in zip(landed, passed):
            got.wait_recv()
            fwd.start()

    def finish(*refs):
        mine, first, _, passed, from_sibling = plan(*refs)
        for cp in from_sibling:
            cp.wait_recv()
        for cp in first + passed:
            cp.wait_send()
        for cp in mine:
            cp.wait()

    out_shapes = [jax.ShapeDtypeStruct((N_DEV,) + s.shape, s.dtype) for s in srcs]
    return Side(list(srcs), out_shapes, _comm_sems(nt), [(0.0, start), (0.5, forward), (1.0, finish)])


def run_side(name, side):
    ns = len(side.arrs)

    def body(*refs):
        for _, phase in side.phases:
            phase(refs[:ns], refs[ns:ns + len(side.out_shapes)], *refs[ns + len(side.out_shapes):])

    return list(pl.pallas_call(
        body, name=name, in_specs=[_ANY] * ns, out_specs=[_ANY] * len(side.out_shapes),
        out_shape=list(side.out_shapes), scratch_shapes=list(side.sem_shapes))(*side.arrs))


def _relayout_w_in(w):
    offs = [0]
    for s in IN_SPLITS:
        offs.append(offs[-1] + s)
    p = [w[:, offs[i]:offs[i + 1]] for i in range(len(IN_SPLITS))]
    ua, za, q, k, v, zb, xbc, dt, zc, gates = p
    dtp = jnp.pad(dt, ((0, 0), (0, HPAD - dt.shape[1])))
    return (jnp.concatenate([ua, za, dtp], 1), w[:, offs[2]:offs[5]], jnp.concatenate([xbc, zb, zc], 1), gates)


def _pad_lanes(v, n=HPAD):
    return jnp.pad(v.reshape(1, -1), ((0, 0), (0, n - v.shape[-1])))


def _s5_prep_args(W):
    g2 = S5_GROUPS * S5_STATE
    w = g2 // S5_CHUNKS
    a_re, a_im = W['s5_a_re'].reshape(1, g2), W['s5_a_im'].reshape(1, g2)
    ls = W['s5_log_step'].reshape(1, S5_GROUPS)
    btr, bti = W['s5_b_re'].reshape(g2, S5_GROUP).T, W['s5_b_im'].reshape(g2, S5_GROUP).T
    ctr = W['s5_c_re'].transpose(1, 0, 2).reshape(S5_GROUP, g2)
    cti = W['s5_c_im'].transpose(1, 0, 2).reshape(S5_GROUP, g2)
    col = lambda a, rows: Arg(a, (rows, w), lambda o: (0, o), 'tile')
    return [col(a_re, 1), col(a_im, 1), _whole(ls, 'acc'), col(btr, S5_GROUP), col(bti, S5_GROUP), col(ctr, S5_GROUP), col(cti, S5_GROUP)]


def _s5_prep_outs():
    g2 = S5_GROUPS * S5_STATE
    w = g2 // S5_CHUNKS
    rows = (S5_ND, S5_ND, S5_SUB, S5_SUB, 128, 128, 128, 128)
    return [Out((r, g2), f32, (r, w), lambda o: (0, o)) for r in rows]


def _s5_args(A, prep, dvec, S):
    w = S5_GROUPS * S5_STATE // S5_CHUNKS
    args = [Arg(A, (S5_TILE, 128), lambda o, t: (t, o), 'tile', (S, S5_WIDTH), None, bf16)]
    for p in prep:
        args.append(Arg(p, (p.shape[0], w), lambda o, t: (0, o), 'acc0'))
    args.append(Arg(dvec, (1, 128), lambda o, t: (0, o), 'acc0'))
    return args


def _attn_args(QKV, g, r, qw, kw, S):
    L = S // r
    nq, rb = _attn_plan(r)
    block = (nq * ATT_BLOCK, rb * ATT_GW)
    gshape = (L, r * ATT_GW)
    gimap = lambda rho, n: (n, rho)
    if r == 1:
        mk = lambda j: Arg(QKV, block, lambda rho, n, j=j: (n, j), 'tile', gshape, gimap, bf16)
    else:
        def mk(j):
            view = QKV[:, j * ATT_GW:(j + 1) * ATT_GW].reshape(L, r * ATT_GW)
            return Arg(view, block, gimap, 'tile', None, None, bf16)
    return [mk(g), mk(3 + g), mk(6 + g), _whole(qw, 'acc'), _whole(kw, 'acc')]


def _attn_plan(r):
    return (4, 1) if r == 1 else (1, min(r, 4))


def _attn_grid(r, S):
    nq, rb = _attn_plan(r)
    return (r // rb, S // r // ATT_BLOCK // nq)


def _attn_carry(r):
    return ((ATT_BLOCK, _attn_plan(r)[1] * ATT_GW),) * 2


def _ssd_args(C, A, W, S):
    T = SSD_CHUNK
    return [Arg(C, (T, SSD_CONV_DIM), lambda o, t: (t, 0), 'tile', (S, SSD_CONV_DIM), None, bf16),
            Arg(A, (T, HPAD), lambda o, t: (t, 2 * S5_WIDTH // HPAD), 'tile', (S, HPAD), lambda o, t: (t, 0), bf16),
            Arg(C, (T, SSD_WIDTH), lambda o, t: (t, 2), 'tile', (S, SSD_WIDTH), lambda o, t: (t, 0), bf16),
            _whole(W['conv_w'], 'acc'), _whole(W['conv_b'].reshape(1, -1), 'acc'),
            _whole(_pad_lanes(W['dt_bias']), 'acc'), _whole(_pad_lanes(W['ssd_a_log']), 'acc'),
            _whole(_pad_lanes(W['ssd_d']), 'acc'), _whole(W['ssd_norm_w'].reshape(1, -1), 'acc')]


_SSD_CARRY = ((8, SSD_CONV_DIM), (SSD_WIDTH, SSD_STATE))
_S5_CARRY = ((1, 512), (1, 512))


def layer_fwd(li, x, W, side=None):
    S = x.shape[0]
    n = lambda s: f"l{li}_{s}"
    sv = {'x': x}
    (h,) = map_fwd(n("norm"), _rmsnorm_tile, (S // 512,), [_rows(x, 512), _whole(W['norm_w'].reshape(1, -1))],
                   [Out((S, D_MODEL), bf16, (512, D_MODEL), lambda i: (i, 0))])
    wA, wQ, wC, wG = W['w_in_pieces']
    A = matmul(n("in_a"), h, wA)
    QKV = matmul(n("in_qkv"), h, wQ)
    C = matmul(n("in_c"), h, wC)
    G = matmul(n("in_g"), h, wG)
    sv.update(h=h, A=A, QKV=QKV, C=C, G=G)

    prep = map_fwd(n("s5_prep"), _s5_prep_tile, (S5_CHUNKS,), _s5_prep_args(W), _s5_prep_outs())
    dvec = W['s5_d'].reshape(1, -1)
    (g,), s5_ck, *got = scan_fwd(n("s5_scan"), _s5_tile, (S5_CHUNKS, S // S5_TILE), _S5_CARRY, _s5_args(A, prep, dvec, S),
                                 [Out((S, S5_WIDTH), f32, (S5_TILE, 128), lambda o, t: (t, o))], side=side)
    glu = matmul(n("glu"), g, W['s5_glu_w'])
    glu_b = W['s5_glu_b'].reshape(1, -1)
    (ya,) = map_fwd(n("glu_gate"), _glu_tile, (S // 512,),
                    [_rows(g, 512), _rows(glu, 512), _rows(A, 512, col=1, width=S5_WIDTH), _whole(glu_b)],
                    [Out((S, S5_WIDTH), bf16, (512, S5_WIDTH), lambda i: (i, 0))])
    sv.update(prep=prep, g=g, glu=glu, ya=ya, s5_ck=s5_ck)

    qw, kw = W['q_norm_w'].reshape(1, -1), W['k_norm_w'].reshape(1, -1)
    att, att_ck = [], []
    for gi, (window, r) in enumerate(ATT_PAIRS):
        assert window // r == ATT_BLOCK and S % (r * ATT_BLOCK) == 0
        L = S // r
        nq, rb = _attn_plan(r)
        assert S // r // ATT_BLOCK % nq == 0
        spec = Out((L, r * ATT_GW), f32, (nq * ATT_BLOCK, rb * ATT_GW), lambda rho, nb: (nb, rho))
        (o, lse), ck = scan_fwd(n(f"attn{gi}"), _attn_tile, _attn_grid(r, S), _attn_carry(r), _attn_args(QKV, gi, r, qw, kw, S), [spec, spec])
        att += [o.reshape(S, ATT_GW), lse.reshape(S, ATT_GW)]
        att_ck.append(ck)
    (yb,) = map_fwd(n("combine"), _combine_tile, (S // 512,),
                    [_rows(t, 512) for t in att] + [_rows(C, 512, col=SSD_CONV_DIM // ATT_GW, width=ATT_GW)],
                    [Out((S, ATT_GW), bf16, (512, ATT_GW), lambda i: (i, 0))])
    sv.update(att=att, att_ck=att_ck, yb=yb)

    (yc,), ssd_ck = scan_fwd(n("ssd"), _ssd_tile, (1, S // SSD_CHUNK), _SSD_CARRY, _ssd_args(C, A, W, S),
                             [Out((S, SSD_WIDTH), bf16, (SSD_CHUNK, SSD_WIDTH), lambda o, t: (t, 0))])
    sv.update(yc=yc, ssd_ck=ssd_ck)

    pa = matmul(n("proj_a"), ya, W['proj_a'])
    pb = matmul(n("proj_b"), yb, W['proj_b'])
    pc = matmul(n("proj_c"), yc, W['proj_c'])
    (merged,) = map_fwd(n("merge"), _merge_tile, (S // 256,),
                        [_rows(pa, 256), _rows(pb, 256), _rows(pc, 256), _rows(G, 256)],
                        [Out((S, D_MODEL), bf16, (256, D_MODEL), lambda i: (i, 0))])
    out = matmul(n("w_out"), merged, W['w_out'], add=x)
    sv.update(pa=pa, pb=pb, pc=pc, merged=merged)
    return out, sv, (got[0] if got else None)


def layer_bwd(li, dout, sv, W, side=None, own_scatter=None):
    S = dout.shape[0]
    n = lambda s: f"l{li}_{s}"
    gr = {}
    x, A, QKV, C, G = sv['x'], sv['A'], sv['QKV'], sv['C'], sv['G']

    dmerged = matmul(n("d_merged"), dout, W['w_out'], 'nt')
    gr['w_out'] = wgrad(n("g_w_out"), sv['merged'], dout)
    margs = [_rows(sv['pa'], 256, gdtype=bf16), _rows(sv['pb'], 256, gdtype=bf16), _rows(sv['pc'], 256, gdtype=bf16),
             _rows(G, 256, gdtype=bf16)]
    dpa, dpb, dpc, dgates = map_bwd(n("merge_bwd"), _merge_tile, (S // 256,), margs, [_rows(dmerged, 256)], list(range(4)))
    dya = matmul(n("d_ya"), dpa, W['proj_a'], 'nt')
    dyb = matmul(n("d_yb"), dpb, W['proj_b'], 'nt')
    dyc = matmul(n("d_yc"), dpc, W['proj_c'], 'nt')
    gr['proj_a'] = wgrad(n("g_proj_a"), sv['ya'], dpa)
    gr['proj_b'] = wgrad(n("g_proj_b"), sv['yb'], dpb)
    gr['proj_c'] = wgrad(n("g_proj_c"), sv['yc'], dpc)

    glu_b = W['s5_glu_b'].reshape(1, -1)
    gargs = [_rows(sv['g'], 512), _rows(sv['glu'], 512, gdtype=bf16),
             _rows(A, 512, col=1, width=S5_WIDTH, gshape=(S, S5_WIDTH), gdtype=bf16), _whole(glu_b, 'acc')]
    dg_a, dglu, dza, dglu_b = map_bwd(n("glu_gate_bwd"), _glu_tile, (S // 512,), gargs, [_rows(dya, 512)], [0, 1, 2, 3])
    gr['s5_glu_b'] = dglu_b.reshape(-1)
    dg = matmul(n("d_g"), dglu, W['s5_glu_w'], 'nt', add=dg_a)
    gr['s5_glu_w'] = wgrad(n("g_glu_w"), sv['g'], dglu)
    dvec = W['s5_d'].reshape(1, -1)
    sargs = _s5_args(A, sv['prep'], dvec, S)
    res = scan_bwd(n("s5_scan_bwd"), _s5_tile, (S5_CHUNKS, S // S5_TILE), _S5_CARRY, sargs, sv['s5_ck'],
                   [Arg(dg, (S5_TILE, 128), lambda o, t: (t, o))], list(range(len(sargs))), bwd_fn=_s5_tile_bwd)
    dua, dprep, dd = res[0], res[1:9], res[9]
    gr['s5_d'] = dd.reshape(-1)
    pargs = _s5_prep_args(W)
    pouts = _s5_prep_outs()
    da_re, da_im, dls, dbtr, dbti, dctr, dcti = map_bwd(
        n("s5_prep_bwd"), _s5_prep_tile, (S5_CHUNKS,), pargs,
        [Arg(d, o.block, o.imap) for d, o in zip(dprep, pouts)], list(range(7)))
    gshape = (S5_GROUPS, S5_STATE)
    gr['s5_a_re'], gr['s5_a_im'] = da_re.reshape(gshape), da_im.reshape(gshape)
    gr['s5_log_step'] = dls.reshape(-1)
    gr['s5_b_re'] = dbtr.T.reshape(S5_GROUPS, S5_STATE, S5_GROUP)
    gr['s5_b_im'] = dbti.T.reshape(S5_GROUPS, S5_STATE, S5_GROUP)
    gr['s5_c_re'] = dctr.reshape(S5_GROUP, S5_GROUPS, S5_STATE).transpose(1, 0, 2)
    gr['s5_c_im'] = dcti.reshape(S5_GROUP, S5_GROUPS, S5_STATE).transpose(1, 0, 2)

    cargs = [_rows(t, 512) for t in sv['att']] + \
            [_rows(C, 512, col=SSD_CONV_DIM // ATT_GW, width=ATT_GW, gshape=(S, ATT_GW), gdtype=bf16)]
    cres = map_bwd(n("combine_bwd"), _combine_tile, (S // 512,), cargs, [_rows(dyb, 512)], list(range(7)))
    dzb = cres[6]
    qw, kw = W['q_norm_w'].reshape(1, -1), W['k_norm_w'].reshape(1, -1)
    dqs, dks, dvs = [], [], []
    dqw = dkw = None
    for gi, (window, r) in enumerate(ATT_PAIRS):
        L = S // r
        nq, rb = _attn_plan(r)
        dspec = lambda t: Arg(t.reshape(L, r * ATT_GW), (nq * ATT_BLOCK, rb * ATT_GW), lambda rho, nb: (nb, rho))
        dq, dk, dv, dqw_g, dkw_g = scan_bwd(n(f"attn{gi}_bwd"), _attn_tile, _attn_grid(r, S), _attn_carry(r),
                                            _attn_args(QKV, gi, r, qw, kw, S), sv['att_ck'][gi],
                                            [dspec(cres[2 * gi]), dspec(cres[2 * gi + 1])], [0, 1, 2, 3, 4])
        dqs.append(dq.reshape(S, ATT_GW))
        dks.append(dk.reshape(S, ATT_GW))
        dvs.append(dv.reshape(S, ATT_GW))
        dqw = dqw_g if dqw is None else dqw + dqw_g
        dkw = dkw_g if dkw is None else dkw + dkw_g
    gr['q_norm_w'], gr['k_norm_w'] = dqw.reshape(-1), dkw.reshape(-1)

    ssd_args = _ssd_args(C, A, W, S)
    sres = scan_bwd(n("ssd_bwd"), _ssd_tile, (1, S // SSD_CHUNK), _SSD_CARRY, ssd_args, sv['ssd_ck'],
                    [Arg(dyc, (SSD_CHUNK, SSD_WIDTH), lambda o, t: (t, 0))], list(range(9)), side=side)
    sres, got = sres if side else (sres, None)
    dxbc, ddt, dzc = sres[0], sres[1], sres[2]
    gr['conv_w'] = sres[3]
    gr['conv_b'] = sres[4].reshape(-1)
    gr['dt_bias'] = sres[5].reshape(-1)[:SSD_HEADS]
    gr['ssd_a_log'] = sres[6].reshape(-1)[:SSD_HEADS]
    gr['ssd_d'] = sres[7].reshape(-1)[:SSD_HEADS]
    gr['ssd_norm_w'] = sres[8].reshape(-1)

    dpieces = [jnp.concatenate([dua, dza, ddt], axis=1), jnp.concatenate(dqs + dks + dvs, axis=1),
               jnp.concatenate([dxbc, dzb, dzc], axis=1), dgates]
    gr['w_in'] = _unrelayout_w_in_grad([wgrad(n(f"g_w_in{j}"), sv['h'], dp) for j, dp in enumerate(dpieces)])
    dh = matmul_nt_sum(n("d_h"), dpieces, list(W['w_in_pieces']), side=own_scatter(gr) if own_scatter else None)
    dh, got_own = dh if own_scatter else (dh, None)
    nargs = [_rows(x, 512), _whole(W['norm_w'].reshape(1, -1), 'acc')]
    dx, dnw = map_bwd(n("norm_bwd"), _rmsnorm_tile, (S // 512,), nargs, [_rows(dh, 512)], [0, 1], add={0: _rows(dout, 512)})
    gr['norm_w'] = dnw.reshape(-1)
    return dx, gr, got, got_own


def _unrelayout_w_in_grad(pieces):
    gA, gQ, gC, gG = pieces
    uaza, dt = gA[:, :2 * S5_WIDTH], gA[:, 2 * S5_WIDTH:2 * S5_WIDTH + SSD_HEADS]
    xbc, zb, zc = gC[:, :SSD_CONV_DIM], gC[:, SSD_CONV_DIM:SSD_CONV_DIM + ATT_GW], gC[:, SSD_CONV_DIM + ATT_GW:]
    return jnp.concatenate([uaza, gQ, zb, xbc, dt, zc, gG], axis=1)


def kernel(x, norm_w, w_in, s5_a_re, s5_a_im, s5_log_step, s5_b_re, s5_b_im, s5_c_re, s5_c_im, s5_d, s5_glu_w, s5_glu_b, q_norm_w, k_norm_w, conv_w, conv_b, dt_bias, ssd_a_log, ssd_d, ssd_norm_w, proj_a, proj_b, proj_c, w_out, loss_target, m_norm_w, m_w_in, m_s5_a_re, m_s5_a_im, m_s5_log_step, m_s5_b_re, m_s5_b_im, m_s5_c_re, m_s5_c_im, m_s5_d, m_s5_glu_w, m_s5_glu_b, m_q_norm_w, m_k_norm_w, m_conv_w, m_conv_b, m_dt_bias, m_ssd_a_log, m_ssd_d, m_ssd_norm_w, m_proj_a, m_proj_b, m_proj_c, m_w_out, v_norm_w, v_w_in, v_s5_a_re, v_s5_a_im, v_s5_log_step, v_s5_b_re, v_s5_b_im, v_s5_c_re, v_s5_c_im, v_s5_d, v_s5_glu_w, v_s5_glu_b, v_q_norm_w, v_k_norm_w, v_conv_w, v_conv_b, v_dt_bias, v_ssd_a_log, v_ssd_d, v_ssd_norm_w, v_proj_a, v_proj_b, v_proj_c, v_w_out):
    args = dict(locals())
    w = {k: args[k] for k in WEIGHTS}
    m = {k: args['m_' + k] for k in WEIGHTS}
    v = {k: args['v_' + k] for k in WEIGHTS}
    depth = norm_w.shape[0]
    S = x.shape[1]
    xs = x.reshape(S, D_MODEL)
    tgt = loss_target.reshape(S, D_MODEL)

    others = [k for k in SHARDED if k != 'w_in']

    def weight_gather(li):
        return gather_side(list(_relayout_w_in(w['w_in'][li].astype(bf16))) + [w[k][li].astype(bf16) for k in others])

    def assemble(li, gathered):
        W = {k: w[k][li] for k in WEIGHTS if k not in SHARDED}
        W['w_in_pieces'] = [t.reshape(t.shape[0] * t.shape[1], t.shape[2]) for t in gathered[:4]]
        for k, t in zip(others, gathered[4:]):
            n_dev, R, C = t.shape
            W[k] = t.reshape(n_dev * R, C) if k in ROW_SHARDED else t.transpose(1, 0, 2).reshape(R, n_dev * C)
        return W

    layers = [assemble(0, run_side("gather_weights0", weight_gather(0)))]
    act, saved = xs, []
    for li in range(depth):
        act, sv, got = layer_fwd(li, act, layers[li], weight_gather(li + 1) if li + 1 < depth else None)
        saved.append(sv)
        if got is not None:
            layers.append(assemble(li + 1, got))
    dy, loss_local = loss_and_grad(act, tgt)
    loss = lax.psum(loss_local, ("x", "y", "c"))

    big = [k for k in SHARDED if k != 'conv_w']

    def grad_scatter(gr):
        return exchange_side([gr[k] for k in big], ['rows' if k in ROW_SHARDED else 'cols' for k in big])

    grads, slots = [None] * depth, [None] * depth
    for li in reversed(range(depth)):
        dy, grads[li], got, got_own = layer_bwd(li, dy, saved[li], layers[li], grad_scatter(grads[li + 1]) if li + 1 < depth else None,
                                                grad_scatter if li == 0 else None)
        if got is not None:
            slots[li + 1] = got
        if got_own is not None:
            slots[li] = got_own
    small_keys = [k for k in WEIGHTS if k not in SHARDED] + ['conv_w']
    stacked = [jnp.stack([grads[li][k] for li in range(depth)], axis=0) for k in small_keys]
    (small_slots,) = run_side("gather_small_grads", gather_side([_pack(stacked)]))
    grad_x = dy.reshape(x.shape)
    result = {k: adamw_layers("adamw_" + k, w[k], [slots[li][j] for li in range(depth)], m[k], v[k]) for j, k in enumerate(big)}

    totals = _unpack(sum_slots("sum_small_grads", small_slots), [t.shape for t in stacked])
    for k, g in zip(small_keys, totals):
        if k == 'conv_w':
            width = w[k].shape[-1]
            me = 4 * lax.axis_index("x") + 2 * lax.axis_index("y") + lax.axis_index("c")
            g = lax.dynamic_slice_in_dim(g, me * width, width, axis=2)
        result[k] = adamw("adamw_" + k, w[k], g[None], m[k], v[k])

    return (loss, grad_x, *[result[k][0] for k in WEIGHTS], *[result[k][1] for k in WEIGHTS],
            *[result[k][2] for k in WEIGHTS], *[result[k][3] for k in WEIGHTS])
```
